```python
import jax, jax.numpy as jnp
from jax import lax
import numpy as np

D_MODEL = 1024
BATCH = 2
SEQ = 8192
DEPTH = 2

RWKV_HEADS = 8
RWKV_HEAD_DIM = 64
RWKV_WIDTH = RWKV_HEADS * RWKV_HEAD_DIM
DECAY_LORA = 64
ICLR_LORA = 64
GATE_LORA = 128
RWKV_GN_EPS = 64e-5
RWKV_COLS = 3 * RWKV_WIDTH + DECAY_LORA + ICLR_LORA + GATE_LORA
RWKV_SPLITS = (RWKV_WIDTH, 2 * RWKV_WIDTH, 3 * RWKV_WIDTH, 3 * RWKV_WIDTH + DECAY_LORA, 3 * RWKV_WIDTH + DECAY_LORA + ICLR_LORA)

NSA_Q_HEADS = 8
NSA_KV_HEADS = 2
NSA_GROUP = NSA_Q_HEADS // NSA_KV_HEADS
NSA_HEAD_DIM = 64
NSA_WIDTH = NSA_Q_HEADS * NSA_HEAD_DIM
CMP_STRIDE = 16
CMP_BLOCK = 2 * CMP_STRIDE
CMP_HIDDEN = 256
SEL_BLOCK = 64
SEL_TOPK = 16
WINDOW = 512
Q_BLOCK = 128
FORCE_SCORE = 1e4
NEG_INF = -1e30
ROPE_THETA = 500000.0
ROPE_DIM = NSA_HEAD_DIM // 4
KV_COLS = 6 * NSA_KV_HEADS * NSA_HEAD_DIM
NSA_GATE_COLS = 3 * NSA_Q_HEADS

MERGE_COLS = 2 * D_MODEL
IN_COLS = RWKV_COLS + NSA_WIDTH + KV_COLS + NSA_GATE_COLS + MERGE_COLS
IN_SPLITS = (RWKV_COLS, RWKV_COLS + NSA_WIDTH, RWKV_COLS + NSA_WIDTH + KV_COLS, RWKV_COLS + NSA_WIDTH + KV_COLS + NSA_GATE_COLS)

N_EXPERTS = 16
N_GROUPS = 4
EXPERTS_PER_GROUP = N_EXPERTS // N_GROUPS
TOP_K = 2
D_EXPERT = 512
MOE_BLOCK = 256

NORM_EPS = 1e-6

kernel_name = "hybrid_rwkv7_nsa_grouped_moe_adaln"


def rms_norm(x, g):
    xf = x.astype(jnp.float32)
    y = xf * lax.rsqrt(jnp.mean(xf * xf, axis=-1, keepdims=True) + NORM_EPS)
    return (y * g.astype(jnp.float32)).astype(x.dtype)


def rope_partial(x, pos):
    half = ROPE_DIM // 2
    inv = jnp.power(ROPE_THETA, -jnp.arange(half, dtype=jnp.float32) * 2.0 / ROPE_DIM)
    ang = pos.astype(jnp.float32)[:, None] * inv[None, :]
    cos, sin = jnp.cos(ang), jnp.sin(ang)
    xf = x.astype(jnp.float32)
    x1, x2, rest = xf[..., :half], xf[..., half:ROPE_DIM], xf[..., ROPE_DIM:]
    out = jnp.concatenate([x1 * cos - x2 * sin, x2 * cos + x1 * sin, rest], axis=-1)
    return out.astype(x.dtype)


def masked_softmax(s, mask):
    s = jnp.where(mask, s.astype(jnp.float32), NEG_INF)
    m = jnp.max(s, axis=-1, keepdims=True)
    p = jnp.where(mask, jnp.exp(s - m), 0.0)
    return p / jnp.maximum(jnp.sum(p, axis=-1, keepdims=True), jnp.finfo(jnp.float32).tiny)


def token_shift(z):
    return jnp.pad(z[:, :-1], ((0, 0), (1, 0), (0, 0)))


def rwkv7_mix(p, mu, w0, w2, a0, a2, g2, k_k, k_a, r_k, gn_g, gn_b):
    B, T, _ = p.shape
    H, N = RWKV_HEADS, RWKV_HEAD_DIM
    f32 = jnp.float32
    pf = p.astype(f32)
    pf = pf + (token_shift(pf) - pf) * mu.astype(f32)
    r, k, v, wl, al, gl = jnp.split(pf, RWKV_SPLITS, axis=-1)
    w = -jax.nn.softplus(-(w0 + jnp.tanh(wl) @ w2)) - 0.5
    a = jax.nn.sigmoid(a0 + al @ a2)
    g = jax.nn.sigmoid(gl) @ g2
    heads = lambda z: z.reshape(B, T, H, N)
    kk = heads(k * k_k)
    kk = kk / jnp.maximum(jnp.sqrt(jnp.sum(kk * kk, axis=-1, keepdims=True)), 1e-12)
    k = k * (1.0 + (a - 1.0) * k_a)
    r, k, v, a = heads(r), heads(k), heads(v), heads(a)
    decay = jnp.exp(-jnp.exp(heads(w)))

    def step(S, inp):
        d_t, k_t, v_t, kk_t, a_t, r_t = inp
        sa = -jnp.einsum('bhij,bhj->bhi', S, kk_t)
        S = (S * d_t[:, :, None, :] + sa[..., None] * (kk_t * a_t)[:, :, None, :]
             + v_t[..., None] * k_t[:, :, None, :])
        return S, jnp.einsum('bhij,bhj->bhi', S, r_t)

    xs = tuple(jnp.moveaxis(z, 1, 0) for z in (decay, k, v, kk, a, r))
    _, y = lax.scan(step, jnp.zeros((B, H, N, N), f32), xs)
    y = jnp.moveaxis(y, 0, 1)
    mean = jnp.mean(y, axis=-1, keepdims=True)
    var = jnp.mean(jnp.square(y - mean), axis=-1, keepdims=True)
    y = (y - mean) * lax.rsqrt(var + RWKV_GN_EPS) * gn_g.reshape(H, N) + gn_b.reshape(H, N)
    y = y + jnp.sum(r * k * r_k, axis=-1, keepdims=True) * v
    return (y.reshape(B, T, RWKV_WIDTH) * g).astype(p.dtype)


def nsa_mix(q, kv, gates, qk_g, cmp_pos, cmp_w1, cmp_w2):
    B, T, _ = q.shape
    G, R, dh = NSA_KV_HEADS, NSA_GROUP, NSA_HEAD_DIM
    pos = jnp.arange(T, dtype=jnp.int32)
    scale = dh ** -0.5
    q = q.reshape(B, T, G, R, dh).transpose(0, 2, 3, 1, 4)
    q = rope_partial(rms_norm(q, qk_g[0]), pos)
    k_c, v_c, k_s, v_s, k_w, v_w = kv.reshape(B, T, 6, G, dh).transpose(2, 0, 3, 1, 4)
    k_s = rope_partial(rms_norm(k_s, qk_g[2]), pos)
    k_w = rope_partial(rms_norm(k_w, qk_g[3]), pos)

    n_chunk = T // CMP_STRIDE
    n_cmp = n_chunk - 1

    def compress(z, i):
        ch = z.reshape(B, G, n_chunk, CMP_STRIDE, dh)
        blk = jnp.concatenate([ch[:, :, :-1], ch[:, :, 1:]], axis=3) + cmp_pos[i]
        hid = jax.nn.gelu(blk.reshape(B, G, n_cmp, CMP_BLOCK * dh) @ cmp_w1[i])
        return hid @ cmp_w2[i]

    cmp_end = jnp.arange(n_cmp, dtype=jnp.int32) * CMP_STRIDE + CMP_BLOCK - 1
    k_cmp = rope_partial(rms_norm(compress(k_c, 0), qk_g[1]), cmp_end)
    v_cmp = compress(v_c, 1)

    n_sel = T // SEL_BLOCK
    topk = min(SEL_TOPK, n_sel)
    k_sb = k_s.reshape(B, G, n_sel, SEL_BLOCK, dh)
    v_sb = v_s.reshape(B, G, n_sel, SEL_BLOCK, dh)
    ci = jnp.arange(n_cmp)[:, None] * CMP_STRIDE
    sj = jnp.arange(n_sel)[None, :] * SEL_BLOCK
    overlap = ((ci <= sj + SEL_BLOCK - 1) & (ci + CMP_BLOCK - 1 >= sj)).astype(jnp.float32)
    bi = jnp.arange(B)[:, None, None, None]
    gi = jnp.arange(G)[None, :, None, None]
    sel_j = jnp.arange(n_sel, dtype=jnp.int32)

    k_wp = jnp.pad(k_w, ((0, 0), (0, 0), (WINDOW, 0), (0, 0)))
    v_wp = jnp.pad(v_w, ((0, 0), (0, 0), (WINDOW, 0), (0, 0)))

    def block_fn(args):
        qb, q0 = args
        t = q0 + jnp.arange(Q_BLOCK, dtype=jnp.int32)
        s = jnp.einsum('bgrqd,bgnd->bgrqn', qb, k_cmp) * scale
        p_c = masked_softmax(s, cmp_end[None, :] <= t[:, None])
        o_c = jnp.einsum('bgrqn,bgnd->bgrqd', p_c.astype(v_cmp.dtype), v_cmp)
        imp = jnp.einsum('bgqn,ns->bgqs', jnp.sum(p_c, axis=2), overlap)
        cur = (t // SEL_BLOCK)[:, None]
        forced = (sel_j[None, :] == 0) | (sel_j[None, :] == cur) | (sel_j[None, :] == cur - 1)
        valid = sel_j[None, :] * SEL_BLOCK <= t[:, None]
        imp = jnp.where(forced & valid, FORCE_SCORE, jnp.where(valid, imp, -1.0))
        _, sel = lax.top_k(imp, topk)
        ks = k_sb[bi, gi, sel].reshape(B, G, Q_BLOCK, topk * SEL_BLOCK, dh)
        vs = v_sb[bi, gi, sel].reshape(B, G, Q_BLOCK, topk * SEL_BLOCK, dh)
        kpos = (sel[..., None] * SEL_BLOCK + jnp.arange(SEL_BLOCK, dtype=jnp.int32)).reshape(B, G, Q_BLOCK, topk * SEL_BLOCK)
        s = jnp.einsum('bgrqd,bgqkd->bgrqk', qb, ks) * scale
        p_s = masked_softmax(s, (kpos <= t[:, None])[:, :, None])
        o_s = jnp.einsum('bgrqk,bgqkd->bgrqd', p_s.astype(vs.dtype), vs)
        kw = lax.dynamic_slice_in_dim(k_wp, q0, WINDOW + Q_BLOCK, axis=2)
        vw = lax.dynamic_slice_in_dim(v_wp, q0, WINDOW + Q_BLOCK, axis=2)
        spos = q0 - WINDOW + jnp.arange(WINDOW + Q_BLOCK, dtype=jnp.int32)
        wmask = (spos[None, :] <= t[:, None]) & (spos[None, :] > t[:, None] - WINDOW) & (spos[None, :] >= 0)
        s = jnp.einsum('bgrqd,bgkd->bgrqk', qb, kw) * scale
        p_w = masked_softmax(s, wmask)
        o_w = jnp.einsum('bgrqk,bgkd->bgrqd', p_w.astype(vw.dtype), vw)
        return o_c, o_s, o_w

    n_qb = T // Q_BLOCK
    qbs = q.reshape(B, G, R, n_qb, Q_BLOCK, dh).transpose(3, 0, 1, 2, 4, 5)
    starts = jnp.arange(n_qb, dtype=jnp.int32) * Q_BLOCK
    o_c, o_s, o_w = lax.map(block_fn, (qbs, starts))
    to_bthd = lambda o: o.transpose(1, 0, 4, 2, 3, 5).reshape(B, T, NSA_Q_HEADS, dh)
    gt = jax.nn.sigmoid(gates.reshape(B, T, NSA_Q_HEADS, 3))
    out = gt[..., 0:1] * to_bthd(o_c) + gt[..., 1:2] * to_bthd(o_s) + gt[..., 2:3] * to_bthd(o_w)
    return out.reshape(B, T, NSA_WIDTH).astype(q.dtype)


def moe_ffn(h, router_w, router_b, w_gate, w_up, w_down):
    B, T, D = h.shape
    N = B * T
    NK = N * TOP_K
    hf = h.reshape(N, D)
    score = jax.nn.sigmoid((hf @ router_w).astype(jnp.float32))
    sel_score = score + router_b.astype(jnp.float32)
    grp = sel_score.reshape(N, N_GROUPS, EXPERTS_PER_GROUP)
    grp_score = jnp.sum(lax.top_k(grp, TOP_K)[0], axis=-1)
    g_star = jnp.argmax(grp_score, axis=-1).astype(jnp.int32)
    in_grp = jnp.take_along_axis(grp, g_star[:, None, None], axis=1)[:, 0]
    _, loc = lax.top_k(in_grp, TOP_K)
    expert = g_star[:, None] * EXPERTS_PER_GROUP + loc.astype(jnp.int32)
    wts = jnp.take_along_axis(score, expert, axis=1)
    wts = wts / jnp.sum(wts, axis=-1, keepdims=True)

    e_flat = expert.reshape(-1)
    tok = jnp.repeat(jnp.arange(N, dtype=jnp.int32), TOP_K)
    order = jnp.argsort(e_flat)
    e_s, tok_s, w_s = e_flat[order], tok[order], wts.reshape(-1)[order]
    counts = jnp.zeros((N_EXPERTS,), jnp.int32).at[e_flat].add(1)
    seg_start = jnp.cumsum(counts) - counts
    padded = (counts + MOE_BLOCK - 1) // MOE_BLOCK * MOE_BLOCK
    pad_end = jnp.cumsum(padded)
    pad_start = pad_end - padded
    dest = pad_start[e_s] + jnp.arange(NK, dtype=jnp.int32) - seg_start[e_s]
    n_blk = -(-NK // MOE_BLOCK) + N_EXPERTS
    P = n_blk * MOE_BLOCK
    slot_tok = jnp.full((P,), N, jnp.int32).at[dest].set(tok_s)
    slot_w = jnp.zeros((P,), h.dtype).at[dest].set(w_s.astype(h.dtype))
    blk_start = jnp.arange(n_blk, dtype=jnp.int32) * MOE_BLOCK
    blk_expert = jnp.clip(jnp.searchsorted(pad_end, blk_start, side='right'), 0, N_EXPERTS - 1)
    hp = jnp.concatenate([hf, jnp.zeros((1, D), h.dtype)], axis=0)
    xb = hp[slot_tok].reshape(n_blk, MOE_BLOCK, D)

    def expert_block(args):
        xe, e = args
        return (jax.nn.silu(xe @ w_gate[e]) * (xe @ w_up[e])) @ w_down[e]

    yb = lax.map(expert_block, (xb, blk_expert)).reshape(P, D)
    y = jnp.zeros((N + 1, D), h.dtype).at[slot_tok].add(yb * slot_w[:, None])
    return y[:N].reshape(B, T, D)


def setup_inputs(seed: int = 0) -> dict:
    key = jax.random.key(seed)
    ks = jax.random.split(key, 32)
    f32 = jnp.float32
    D, L = D_MODEL, DEPTH
    nrm = lambda k, shape, s: jax.random.normal(k, shape, f32) * s
    return {
        "x": nrm(ks[0], (BATCH, SEQ, D), 1.0),
        "c": nrm(ks[1], (BATCH, D), 1.0),
        "w_ada": nrm(ks[2], (L, D, 6 * D), 0.5 * D ** -0.5),
        "b_ada": nrm(ks[3], (L, 6 * D), 0.02),
        "norm_g": 1.0 + nrm(ks[4], (L, 2, D), 0.02),
        "w_in": nrm(ks[5], (L, D, IN_COLS), D ** -0.5),
        "b_in": nrm(ks[6], (L, IN_COLS), 0.02),
        "rwkv_mu": jax.random.uniform(ks[7], (L, RWKV_COLS), f32, 0.0, 1.0),
        "rwkv_w0": jax.random.uniform(ks[8], (L, RWKV_WIDTH), f32, -6.0, -1.0),
        "rwkv_w2": nrm(ks[9], (L, DECAY_LORA, RWKV_WIDTH), DECAY_LORA ** -0.5),
        "rwkv_a0": nrm(ks[10], (L, RWKV_WIDTH), 0.1),
        "rwkv_a2": nrm(ks[11], (L, ICLR_LORA, RWKV_WIDTH), 0.5 * ICLR_LORA ** -0.5),
        "rwkv_g2": nrm(ks[12], (L, GATE_LORA, RWKV_WIDTH), GATE_LORA ** -0.5),
        "rwkv_k_k": 0.85 + nrm(ks[13], (L, RWKV_WIDTH), 0.02),
        "rwkv_k_a": 1.0 + nrm(ks[14], (L, RWKV_WIDTH), 0.02),
        "rwkv_r_k": nrm(ks[15], (L, RWKV_HEADS, RWKV_HEAD_DIM), 0.1),
        "rwkv_gn_g": 1.0 + nrm(ks[16], (L, RWKV_WIDTH), 0.02),
        "rwkv_gn_b": nrm(ks[17], (L, RWKV_WIDTH), 0.02),
        "qk_norm_g": 1.0 + nrm(ks[18], (L, 4, NSA_HEAD_DIM), 0.02),
        "cmp_pos": nrm(ks[19], (L, 2, CMP_BLOCK, NSA_HEAD_DIM), 0.1),
        "cmp_w1": nrm(ks[20], (L, 2, CMP_BLOCK * NSA_HEAD_DIM, CMP_HIDDEN), (CMP_BLOCK * NSA_HEAD_DIM) ** -0.5),
        "cmp_w2": nrm(ks[21], (L, 2, CMP_HIDDEN, NSA_HEAD_DIM), CMP_HIDDEN ** -0.5),
        "w_up_rwkv": nrm(ks[22], (L, RWKV_WIDTH, D), RWKV_WIDTH ** -0.5),
        "w_up_nsa": nrm(ks[23], (L, NSA_WIDTH, D), NSA_WIDTH ** -0.5),
        "w_out": nrm(ks[24], (L, D, D), D ** -0.5),
        "router_w": nrm(ks[25], (D, N_EXPERTS), D ** -0.5),
        "router_b": nrm(ks[26], (N_EXPERTS,), 0.01),
        "exp_w_gate": nrm(ks[27], (L, N_EXPERTS, D, D_EXPERT), D ** -0.5),
        "exp_w_up": nrm(ks[28], (L, N_EXPERTS, D, D_EXPERT), D ** -0.5),
        "exp_w_down": nrm(ks[29], (L, N_EXPERTS, D_EXPERT, D), D_EXPERT ** -0.5),
    }


def reference(x, c, w_ada, b_ada, norm_g, w_in, b_in, rwkv_mu, rwkv_w0, rwkv_w2, rwkv_a0, rwkv_a2,
              rwkv_g2, rwkv_k_k, rwkv_k_a, rwkv_r_k, rwkv_gn_g, rwkv_gn_b, qk_norm_g, cmp_pos, cmp_w1,
              cmp_w2, w_up_rwkv, w_up_nsa, w_out, router_w, router_b, exp_w_gate, exp_w_up, exp_w_down):
    for l in range(DEPTH):
        mod = jax.nn.silu(c) @ w_ada[l] + b_ada[l]
        sh1, sc1, gate_mix, sh2, sc2, gate_ffn = jnp.split(mod[:, None, :], 6, axis=-1)
        h = rms_norm(x, norm_g[l, 0]) * (1.0 + sc1) + sh1
        p = h @ w_in[l] + b_in[l]
        p_rwkv, p_q, p_kv, p_gate, p_merge = jnp.split(p, IN_SPLITS, axis=-1)
        y_a = rwkv7_mix(p_rwkv, rwkv_mu[l], rwkv_w0[l], rwkv_w2[l], rwkv_a0[l], rwkv_a2[l], rwkv_g2[l],
                        rwkv_k_k[l], rwkv_k_a[l], rwkv_r_k[l], rwkv_gn_g[l], rwkv_gn_b[l])
        y_b = nsa_mix(p_q, p_kv, p_gate, qk_norm_g[l], cmp_pos[l], cmp_w1[l], cmp_w2[l])
        m_a, m_b = jnp.split(jax.nn.sigmoid(p_merge), 2, axis=-1)
        mix = m_a * (y_a @ w_up_rwkv[l]) + m_b * (y_b @ w_up_nsa[l])
        x = x + gate_mix * (mix @ w_out[l])
        h2 = rms_norm(x, norm_g[l, 1]) * (1.0 + sc2) + sh2
        x = x + gate_ffn * moe_ffn(h2, router_w, router_b, exp_w_gate[l], exp_w_up[l], exp_w_down[l])
    return x
```

```python
import functools
import math

import jax
import jax.numpy as jnp
from jax import lax
from jax.experimental import pallas as pl
from jax.experimental.pallas import tpu as pltpu

F32 = jnp.float32
BF16 = jnp.bfloat16
HI = lax.Precision.HIGHEST

D_MODEL = 1024
RWKV_HEADS = 8
HEAD_DIM = 64
RWKV_WIDTH = RWKV_HEADS * HEAD_DIM
DECAY_LORA = 64
ICLR_LORA = 64
GATE_LORA = 128
RWKV_GN_EPS = 64e-5
RWKV_COLS = 3 * RWKV_WIDTH + DECAY_LORA + ICLR_LORA + GATE_LORA

NSA_Q_HEADS = 8
NSA_KV_HEADS = 2
NSA_GROUP = NSA_Q_HEADS // NSA_KV_HEADS
NSA_WIDTH = NSA_Q_HEADS * HEAD_DIM
CMP_STRIDE = 16
CMP_BLOCK = 2 * CMP_STRIDE
CMP_HIDDEN = 256
SEL_BLOCK = 64
SEL_SHIFT = 6
SEL_TOPK = 16
WINDOW = 512
FORCE_SCORE = 1e4
NEG_INF = -1e30
ROPE_THETA = 500000.0
ROPE_DIM = HEAD_DIM // 4
KV_COLS = 6 * NSA_KV_HEADS * HEAD_DIM
NSA_GATE_COLS = 3 * NSA_Q_HEADS
GATE_PAD = 128

N_EXPERTS = 16
N_GROUPS = 4
EXPERTS_PER_GROUP = N_EXPERTS // N_GROUPS
TOP_K = 2
D_EXPERT = 512
MOE_BLOCK = 256
NORM_EPS = 1e-6

LANES = 128
CHUNK = 64
KEY_TILE = 128
Q_TILE = 128
F32_TINY = float(jnp.finfo(jnp.float32).tiny)

_SEG_RW = (0, RWKV_COLS)
_SEG_Q = (_SEG_RW[1], _SEG_RW[1] + NSA_WIDTH)
_SEG_KV = (_SEG_Q[1], _SEG_Q[1] + KV_COLS)
_SEG_GATE = (_SEG_KV[1], _SEG_KV[1] + GATE_PAD)
_SEG_MERGE = (_SEG_GATE[1], _SEG_GATE[1] + 2 * D_MODEL)
IN_COLS_PAD = _SEG_MERGE[1]

_VMEM_LIMIT = 56 * 1024 * 1024


def _dot(a, b, precision=None):
    return jnp.dot(a, b, preferred_element_type=F32, precision=precision)


def _dot_tb(a, b, precision=None):
    return lax.dot_general(a, b, (((1,), (1,)), ((), ())), preferred_element_type=F32,
                           precision=precision)


def _dot_ta(a, b, precision=None):
    return lax.dot_general(a, b, (((0,), (0,)), ((), ())), preferred_element_type=F32,
                           precision=precision)


def _params(*sem):
    return pltpu.CompilerParams(dimension_semantics=sem, vmem_limit_bytes=_VMEM_LIMIT)


def _sigmoid(x):
    return 1.0 / (1.0 + jnp.exp(-x))


def _ada_kernel(c_ref, w_ref, b_ref, o_ref):
    c = c_ref[...]
    s = c * _sigmoid(c)
    o_ref[0] = _dot(s, w_ref[0], HI) + b_ref[0]


def _ada(c, w_ada, b_ada):
    L, D, D6 = w_ada.shape
    B = c.shape[0]
    rows = 8
    cp = jnp.zeros((rows, D), F32).at[:B].set(c)
    tn = 1536
    out = pl.pallas_call(
        _ada_kernel,
        grid=(L, D6 // tn),
        in_specs=[pl.BlockSpec((rows, D), lambda l, j: (0, 0)),
                  pl.BlockSpec((1, D, tn), lambda l, j: (l, 0, j)),
                  pl.BlockSpec((1, 1, tn), lambda l, j: (l, 0, j))],
        out_specs=pl.BlockSpec((1, rows, tn), lambda l, j: (l, 0, j)),
        out_shape=jax.ShapeDtypeStruct((L, rows, D6), F32),
        compiler_params=_params("arbitrary", "arbitrary"),
        name="ada_mod",
    )(cp, w_ada, b_ada.reshape(L, 1, D6))
    return out[:, :B].reshape(L, B, 6, D)


def _inproj_kernel(x_ref, mod_ref, g_ref, w_ref, b_ref, o_rw, o_q, o_kv, o_gate, o_merge):
    m = mod_ref[pl.program_id(0)]
    x = x_ref[...]
    ms = jnp.mean(x * x, axis=-1, keepdims=True)
    h = x * lax.rsqrt(ms + NORM_EPS) * g_ref[...]
    h = h * (1.0 + m[1:2]) + m[0:1]
    hb = h.astype(BF16)
    for o, (a, e) in ((o_rw, _SEG_RW), (o_q, _SEG_Q), (o_kv, _SEG_KV), (o_gate, _SEG_GATE),
                      (o_merge, _SEG_MERGE)):
        o[...] = _dot(hb, w_ref[:, a:e]) + b_ref[:, a:e]


def _inproj(x2, mod, g, w_pad, b_pad, B, T, tm=256):
    N, D = x2.shape
    nt = T // tm
    row = lambda b, t: (b * nt + t, 0)
    widths = [e - a for a, e in (_SEG_RW, _SEG_Q, _SEG_KV, _SEG_GATE, _SEG_MERGE)]
    return pl.pallas_call(
        _inproj_kernel,
        grid=(B, nt),
        in_specs=[pl.BlockSpec((tm, D), row),
                  pl.BlockSpec((B, 6, D), lambda b, t: (0, 0, 0)),
                  pl.BlockSpec((1, D), lambda b, t: (0, 0)),
                  pl.BlockSpec((D, IN_COLS_PAD), lambda b, t: (0, 0)),
                  pl.BlockSpec((1, IN_COLS_PAD), lambda b, t: (0, 0))],
        out_specs=[pl.BlockSpec((tm, w), row) for w in widths],
        out_shape=[jax.ShapeDtypeStruct((N, w), F32) for w in widths],
        compiler_params=_params("arbitrary", "arbitrary"),
        name="in_proj",
    )(x2, mod, g, w_pad, b_pad)


def _rwkv_pre_kernel(p_ref, mu_ref, w0_ref, w2_ref, a0_ref, a2_ref, g2_ref, kk_ref, ka_ref, rk_ref,
                     bd_ref, o_r, o_k, o_v, o_al, o_b, o_ld, o_g, o_bonus, carry_ref):
    W = RWKV_WIDTH

    @pl.when(pl.program_id(1) == 0)
    def _():
        carry_ref[...] = jnp.zeros_like(carry_ref)

    p = p_ref[...]
    ts = p.shape[0]
    rows = lax.broadcasted_iota(jnp.int32, p.shape, 0)
    shifted = jnp.where(rows == 0, carry_ref[0:1, :], pltpu.roll(p, 1, 0))
    carry_ref[0:1, :] = p[ts - 1:ts, :]
    pm = p + (shifted - p) * mu_ref[...]
    r = pm[:, 0:W]
    k = pm[:, W:2 * W]
    v = pm[:, 2 * W:3 * W]
    wa = pm[:, 3 * W:3 * W + DECAY_LORA + ICLR_LORA]
    gl = pm[:, 3 * W + DECAY_LORA + ICLR_LORA:]
    xw = w0_ref[...] + _dot(jnp.tanh(wa), w2_ref[...], HI)
    ld = -math.exp(-0.5) * _sigmoid(xw)
    a = _sigmoid(a0_ref[...] + _dot(wa, a2_ref[...], HI))
    g = _dot(_sigmoid(gl), g2_ref[...], HI)
    bd = bd_ref[...]
    kk = k * kk_ref[...]
    nrm = jnp.sqrt(_dot(kk * kk, bd, HI))
    kk = kk / jnp.maximum(nrm, 1e-12)
    k2 = k * (1.0 + (a - 1.0) * ka_ref[...])
    bonus = _dot(r * k2 * rk_ref[...], bd, HI) * v
    o_r[...] = r
    o_k[...] = k2
    o_v[...] = v
    o_al[...] = kk
    o_b[...] = -kk * a
    o_ld[...] = ld
    o_g[...] = g
    o_bonus[...] = bonus


def _head_block_diag(width, scale=1.0):
    i = jnp.arange(width) // HEAD_DIM
    return (i[:, None] == i[None, :]).astype(F32) * scale


def _rwkv_pre(p_rw, mu, w0, w2, a0, a2, g2, k_k, k_a, r_k, B, T, ts=256):
    N = p_rw.shape[0]
    W = RWKV_WIDTH
    nt = T // ts
    row = lambda b, t: (b * nt + t, 0)
    zl = jnp.zeros((DECAY_LORA, W), F32)
    w2p = jnp.concatenate([w2, zl], axis=0)
    a2p = jnp.concatenate([zl, a2], axis=0)
    full = lambda shape: pl.BlockSpec(shape, lambda b, t: (0,) * len(shape))
    vec = lambda z: z.reshape(1, -1)
    return pl.pallas_call(
        _rwkv_pre_kernel,
        grid=(B, nt),
        in_specs=[pl.BlockSpec((ts, RWKV_COLS), row), full((1, RWKV_COLS)), full((1, W)),
                  full((2 * DECAY_LORA, W)), full((1, W)), full((2 * DECAY_LORA, W)),
                  full((GATE_LORA, W)), full((1, W)), full((1, W)), full((1, W)), full((W, W))],
        out_specs=[pl.BlockSpec((ts, W), row)] * 8,
        out_shape=[jax.ShapeDtypeStruct((N, W), F32)] * 8,
        scratch_shapes=[pltpu.VMEM((8, RWKV_COLS), F32)],
        compiler_params=_params("arbitrary", "arbitrary"),
        name="rwkv_pre",
    )(p_rw, vec(mu), vec(w0), w2p, vec(a0), a2p, g2, vec(k_k), vec(k_a), vec(r_k),
      _head_block_diag(W))


def _scan_chunk(r, k, v, al, bb, ld, H, tri, eye, strict, incl, m0, m1):
    C = CHUNK
    cum = _dot(tri, ld, HI)
    tot = cum[C - 1:C, :]
    rt = r * jnp.exp(cum)
    at = al * jnp.exp(cum - ld)
    dinv = jnp.exp(-cum)
    bt = bb * dinv
    kt = k * dinv
    dend = jnp.exp(tot - cum)
    bh = bb * dend
    kh = k * dend
    st = lambda z: jnp.concatenate([z * m0, z * m1], axis=0)
    at_s, rt_s, vs = st(at), st(rt), st(v)
    A = _dot_tb(jnp.concatenate([at_s, rt_s], axis=0),
                jnp.concatenate([st(bt), st(kt)], axis=0), HI)
    zero = jnp.zeros((2 * C, 2 * C), F32)
    a_ab = jnp.where(strict, A[0:2 * C, 0:2 * C], zero)
    a_ak = jnp.where(strict, A[0:2 * C, 2 * C:4 * C], zero)
    a_rb = jnp.where(incl, A[2 * C:4 * C, 0:2 * C], zero)
    a_rk = jnp.where(incl, A[2 * C:4 * C, 2 * C:4 * C], zero)
    pw = a_ab
    tinv = eye + pw
    for _ in range(5):
        pw = _dot(pw, pw, HI)
        tinv = tinv + _dot(pw, tinv, HI)
    X = _dot(tinv, jnp.concatenate([at_s, _dot(a_ak, vs, HI)], axis=1), HI)
    Z = _dot(jnp.concatenate([X[:, 0:LANES], rt_s], axis=0), H, HI)
    U = Z[0:2 * C] + X[:, LANES:2 * LANES]
    UV = jnp.concatenate([U, vs], axis=0)
    Y = Z[2 * C:4 * C] + _dot(jnp.concatenate([a_rb, a_rk], axis=1), UV, HI)
    y = Y[0:C] + Y[C:2 * C]
    dC = eye * jnp.exp(tot)
    Hn = _dot_ta(jnp.concatenate([dC, st(bh), st(kh)], axis=0),
                 jnp.concatenate([H, UV], axis=0), HI)
    return y, Hn


def _rwkv_scan_kernel(r_ref, k_ref, v_ref, al_ref, b_ref, ld_ref, o_ref, h_ref):
    C = CHUNK

    @pl.when(pl.program_id(2) == 0)
    def _():
        h_ref[...] = jnp.zeros_like(h_ref)

    ri = lax.broadcasted_iota(jnp.int32, (C, C), 0)
    ci = lax.broadcasted_iota(jnp.int32, (C, C), 1)
    tri = (ci <= ri).astype(F32)
    r2 = lax.broadcasted_iota(jnp.int32, (2 * C, 2 * C), 0)
    c2 = lax.broadcasted_iota(jnp.int32, (2 * C, 2 * C), 1)
    eye = (r2 == c2).astype(F32)
    strict = (c2 & (C - 1)) < (r2 & (C - 1))
    incl = (c2 & (C - 1)) <= (r2 & (C - 1))
    lane = lax.broadcasted_iota(jnp.int32, (C, LANES), 1)
    m0 = (lane < HEAD_DIM).astype(F32)
    m1 = 1.0 - m0
    H = h_ref[...]
    for c in range(r_ref.shape[0] // C):
        sl = slice(c * C, (c + 1) * C)
        y, H = _scan_chunk(r_ref[sl, :], k_ref[sl, :], v_ref[sl, :], al_ref[sl, :], b_ref[sl, :],
                           ld_ref[sl, :], H, tri, eye, strict, incl, m0, m1)
        o_ref[sl, :] = y
    h_ref[...] = H


def _rwkv_scan(r, k, v, al, bb, ld, B, T, tc=256):
    N, W = r.shape
    nt = T // tc
    spec = pl.BlockSpec((tc, LANES), lambda b, h, t: (b * nt + t, h))
    return pl.pallas_call(
        _rwkv_scan_kernel,
        grid=(B, W // LANES, nt),
        in_specs=[spec] * 6,
        out_specs=spec,
        out_shape=jax.ShapeDtypeStruct((N, W), F32),
        scratch_shapes=[pltpu.VMEM((LANES, LANES), F32)],
        compiler_params=_params("arbitrary", "arbitrary", "arbitrary"),
        name="rwkv_scan",
    )(r, k, v, al, bb, ld)


def _rope_tables(pos):
    half = ROPE_DIM // 2
    inv = jnp.power(ROPE_THETA, -jnp.arange(half, dtype=F32) * 2.0 / ROPE_DIM)
    ang = pos.astype(F32)[:, None] * inv[None, :]
    cos, sin = jnp.cos(ang), jnp.sin(ang)
    n = pos.shape[0]
    rest = HEAD_DIM - ROPE_DIM
    c = jnp.concatenate([cos, cos, jnp.ones((n, rest), F32)], axis=1)
    s_dn = jnp.concatenate([-sin, jnp.zeros((n, half + rest), F32)], axis=1)
    s_up = jnp.concatenate([jnp.zeros((n, half), F32), sin, jnp.zeros((n, rest), F32)], axis=1)
    rep = LANES // HEAD_DIM
    return jnp.tile(c, (1, rep)), jnp.tile(s_dn, (1, rep)), jnp.tile(s_up, (1, rep))


def _norm_rope(x, bd, g, c, s_dn, s_up):
    width = x.shape[1]
    half = ROPE_DIM // 2
    rep = width // LANES
    tile = (lambda z: jnp.concatenate([z] * rep, axis=1)) if rep > 1 else (lambda z: z)
    ms = _dot(x * x, bd, HI)
    xn = x * lax.rsqrt(ms + NORM_EPS) * g
    return (xn * tile(c) + pltpu.roll(xn, width - half, 1) * tile(s_dn)
            + pltpu.roll(xn, half, 1) * tile(s_up))


def _nsa_prep_kernel(q_ref, kv_ref, c_ref, sd_ref, su_ref, gq_ref, gs_ref, gw_ref, bdq_ref, bdk_ref,
                     o_qt, o_ks, o_kw, o_vst, o_vwt):
    c, sd, su = c_ref[...], sd_ref[...], su_ref[...]
    q = _norm_rope(q_ref[...], bdq_ref[...], gq_ref[...], c, sd, su) * (HEAD_DIM ** -0.5)
    qt = q.T
    kv = kv_ref[...]
    bdk = bdk_ref[...]
    o_ks[...] = _norm_rope(kv[:, 2 * LANES:3 * LANES], bdk, gs_ref[...], c, sd, su).astype(BF16)
    o_kw[...] = _norm_rope(kv[:, 4 * LANES:5 * LANES], bdk, gw_ref[...], c, sd, su).astype(BF16)
    vst = kv[:, 3 * LANES:4 * LANES].T
    vwt = kv[:, 5 * LANES:6 * LANES].T
    for j in range(q.shape[0] // KEY_TILE):
        sl = slice(j * KEY_TILE, (j + 1) * KEY_TILE)
        o_qt[0, j] = qt[:, sl].astype(BF16)
        o_vst[0, j] = vst[:, sl].astype(BF16)
        o_vwt[0, j] = vwt[:, sl].astype(BF16)


def _nsa_prep(q, kv, tables, qk_g, B, T, ts=256):
    N = q.shape[0]
    nt = T // ts
    nk = ts // KEY_TILE
    row = lambda b, t: (b * nt + t, 0)
    full = lambda shape: pl.BlockSpec(shape, lambda b, t: (0,) * len(shape))
    tab = pl.BlockSpec((ts, LANES), lambda b, t: (t, 0))
    gq = jnp.tile(qk_g[0], NSA_Q_HEADS).reshape(1, NSA_WIDTH)
    gs = jnp.tile(qk_g[2], NSA_KV_HEADS).reshape(1, LANES)
    gw = jnp.tile(qk_g[3], NSA_KV_HEADS).reshape(1, LANES)
    tiled = lambda rows: pl.BlockSpec((1, nk, rows, KEY_TILE), lambda b, t: (b, t, 0, 0))
    return pl.pallas_call(
        _nsa_prep_kernel,
        grid=(B, nt),
        in_specs=[pl.BlockSpec((ts, NSA_WIDTH), row), pl.BlockSpec((ts, KV_COLS), row), tab, tab, tab,
                  full((1, NSA_WIDTH)), full((1, LANES)), full((1, LANES)),
                  full((NSA_WIDTH, NSA_WIDTH)), full((LANES, LANES))],
        out_specs=[tiled(NSA_WIDTH), pl.BlockSpec((ts, LANES), row), pl.BlockSpec((ts, LANES), row),
                   tiled(LANES), tiled(LANES)],
        out_shape=[jax.ShapeDtypeStruct((B, T // KEY_TILE, NSA_WIDTH, KEY_TILE), BF16),
                   jax.ShapeDtypeStruct((N, LANES), BF16), jax.ShapeDtypeStruct((N, LANES), BF16),
                   jax.ShapeDtypeStruct((B, T // KEY_TILE, LANES, KEY_TILE), BF16),
                   jax.ShapeDtypeStruct((B, T // KEY_TILE, LANES, KEY_TILE), BF16)],
        compiler_params=_params("arbitrary", "arbitrary"),
        name="nsa_prep",
    )(q, kv, *tables, gq, gs, gw, _head_block_diag(NSA_WIDTH, 1.0 / HEAD_DIM),
      _head_block_diag(LANES, 1.0 / HEAD_DIM))


def _gelu_tanh(x):
    return 0.5 * x * (1.0 + jnp.tanh(0.7978845608028654 * (x + 0.044715 * x * x * x)))


def _nsa_cmp_kernel(x_ref, pos_ref, w1_ref, w2_ref, *rest, is_key):
    if is_key:
        g_ref, c_ref, sd_ref, su_ref, bd_ref, o_ref, xs_ref = rest
    else:
        o_ref, xs_ref = rest
    nch = xs_ref.shape[0]
    S = CMP_STRIDE
    for j in range(S):
        xs_ref[:, j * LANES:(j + 1) * LANES] = x_ref[0, pl.ds(j, nch, stride=S), :]
    xs = xs_ref[...]
    first = _dot((xs + pos_ref[0:1, :]).astype(BF16), w1_ref[0])
    second = _dot((xs + pos_ref[1:2, :]).astype(BF16), w1_ref[1])
    hid = first + pltpu.roll(second, nch - 1, 0)
    out = _dot(_gelu_tanh(hid).astype(BF16), w2_ref[...])
    rows = lax.broadcasted_iota(jnp.int32, out.shape, 0)
    if is_key:
        out = _norm_rope(out, bd_ref[...], g_ref[...], c_ref[...], sd_ref[...], su_ref[...])
        o_ref[0] = jnp.where(rows < nch - 1, out, 0.0).astype(BF16)
    else:
        o_ref[0] = jnp.where(rows < nch - 1, out, 0.0).T.astype(BF16)


def _nsa_cmp(kv3, which, cmp_pos, cmp_w1, cmp_w2, g_k, tables_cmp):
    B, T, _ = kv3.shape
    S = CMP_STRIDE
    nch = T // S
    is_key = which == 0
    eye2 = jnp.eye(NSA_KV_HEADS, dtype=F32)
    w1 = cmp_w1[which].reshape(CMP_BLOCK, HEAD_DIM, CMP_HIDDEN)
    w1 = jnp.einsum('jdh,ge->jgdeh', w1, eye2).reshape(2, S * LANES, NSA_KV_HEADS * CMP_HIDDEN)
    w2 = jnp.einsum('hd,ge->ghed', cmp_w2[which], eye2).reshape(NSA_KV_HEADS * CMP_HIDDEN, LANES)
    pos = jnp.tile(cmp_pos[which].reshape(2, S, 1, HEAD_DIM), (1, 1, NSA_KV_HEADS, 1)).reshape(2, S * LANES)
    full = lambda shape: pl.BlockSpec(shape, lambda b: (0,) * len(shape))
    in_specs = [pl.BlockSpec((1, T, LANES), lambda b: (b, 0, which)), full(pos.shape), full(w1.shape),
                full(w2.shape)]
    args = [kv3, pos, w1.astype(BF16), w2.astype(BF16)]
    if is_key:
        in_specs += [full((1, LANES)), full((nch, LANES)), full((nch, LANES)), full((nch, LANES)),
                     full((LANES, LANES))]
        args += [jnp.tile(g_k, NSA_KV_HEADS).reshape(1, LANES), *tables_cmp,
                 _head_block_diag(LANES, 1.0 / HEAD_DIM)]
        out_spec = pl.BlockSpec((1, nch, LANES), lambda b: (b, 0, 0))
        out_shape = jax.ShapeDtypeStruct((B, nch, LANES), BF16)
    else:
        out_spec = pl.BlockSpec((1, LANES, nch), lambda b: (b, 0, 0))
        out_shape = jax.ShapeDtypeStruct((B, LANES, nch), BF16)
    return pl.pallas_call(
        functools.partial(_nsa_cmp_kernel, is_key=is_key),
        grid=(B,),
        in_specs=in_specs,
        out_specs=out_spec,
        out_shape=out_shape,
        scratch_shapes=[pltpu.VMEM((nch, S * LANES), F32)],
        compiler_params=_params("arbitrary"),
        name="nsa_cmp_k" if is_key else "nsa_cmp_v",
    )(*args)


def _nsa_attn_kernel(qt_ref, kc_ref, vct_ref, ks_ref, vst_ref, kw_ref, vwt_ref, gt_ref, ov_ref, o_ref):
    g = pl.program_id(1)
    qb = pl.program_id(2)
    R = NSA_GROUP
    QT = Q_TILE
    KT = KEY_TILE
    NQ = R * QT
    t0 = qb * QT
    n_cmp_pad = kc_ref.shape[1]
    n_sel = ov_ref.shape[0]

    q_g = jnp.concatenate([qt_ref[0, 0, r * HEAD_DIM:(r + 1) * HEAD_DIM, :] for r in range(R)], axis=1)
    q2 = jnp.concatenate([q_g, q_g], axis=0)
    row_grp = lax.broadcasted_iota(jnp.int32, q2.shape, 0) // HEAD_DIM
    qpad = jnp.where(row_grp == g, q2, jnp.zeros_like(q2))

    tq_row = t0 + (lax.broadcasted_iota(jnp.int32, (1, NQ), 1) & (QT - 1))
    tq_tile = t0 + lax.broadcasted_iota(jnp.int32, (KT, QT), 1)
    krow = lax.broadcasted_iota(jnp.int32, (KT, QT), 0)
    tile4 = lambda z: jnp.concatenate([z] * R, axis=1)

    sc = _dot(kc_ref[0], qpad)
    n_i = lax.broadcasted_iota(jnp.int32, (n_cmp_pad, 1), 0)
    cend = jnp.where(n_i < n_cmp_pad - 1, n_i * CMP_STRIDE + (CMP_BLOCK - 1), jnp.int32(2 ** 30))
    cvalid = cend <= tq_row
    sc = jnp.where(cvalid, sc, NEG_INF)
    mc = jnp.max(sc, axis=0, keepdims=True)
    ec = jnp.where(cvalid, jnp.exp(sc - mc), 0.0)
    pc = ec / jnp.maximum(jnp.sum(ec, axis=0, keepdims=True), F32_TINY)
    o_c = _dot(vct_ref[0], pc.astype(BF16))
    pc_sum = pc[:, 0:QT]
    for r in range(1, R):
        pc_sum = pc_sum + pc[:, r * QT:(r + 1) * QT]
    imp = _dot(ov_ref[...], pc_sum, HI)

    ji = lax.broadcasted_iota(jnp.int32, (n_sel, QT), 0)
    jf = ji.astype(F32)
    tq_sel = t0 + lax.broadcasted_iota(jnp.int32, (n_sel, QT), 1)
    cur = tq_sel >> SEL_SHIFT
    forced = (ji == 0) | (ji == cur) | (ji == cur - 1)
    valid = ji * SEL_BLOCK <= tq_sel
    score = jnp.where(valid, jnp.where(forced, FORCE_SCORE, imp), -1.0)
    sel = jnp.zeros((n_sel, QT), F32)
    for _ in range(min(SEL_TOPK, n_sel)):
        mx = jnp.max(score, axis=0, keepdims=True)
        jmin = jnp.min(jnp.where(score == mx, jf, 1e9), axis=0, keepdims=True)
        hit = jf == jmin
        sel = jnp.where(hit, 1.0, sel)
        score = jnp.where(hit, -3e38, score)
    selb = sel.astype(BF16)

    def online_step(s, vt, m, l, acc):
        m_new = jnp.maximum(m, jnp.max(s, axis=0, keepdims=True))
        alpha = jnp.exp(m - m_new)
        p = jnp.exp(s - m_new)
        l = l * alpha + jnp.sum(p, axis=0, keepdims=True)
        acc = acc * alpha + _dot(vt, p.astype(BF16))
        return m_new, l, acc

    init = (jnp.full((1, NQ), NEG_INF, F32), jnp.zeros((1, NQ), F32), jnp.zeros((HEAD_DIM, NQ), F32))
    blk_lane = lax.broadcasted_iota(jnp.int32, (KT, n_sel), 1)
    blk_row = lax.broadcasted_iota(jnp.int32, (KT, n_sel), 0)

    def sel_body(kt, carry):
        k0 = pl.multiple_of(kt * KT, KT)
        s = _dot(ks_ref[0, pl.ds(k0, KT), :], qpad)
        expand = jnp.where(((k0 + blk_row) >> SEL_SHIFT) == blk_lane, 1.0, 0.0).astype(BF16)
        picked = _dot(expand, selb)
        kpos = k0 + krow
        bias = jnp.where(kpos <= tq_tile, jnp.where(picked > 0.5, 0.0, NEG_INF), NEG_INF)
        return online_step(s + tile4(bias), vst_ref[0, kt], *carry)

    _, l_s, acc_s = lax.fori_loop(0, qb + 1, sel_body, init)

    def win_body(u, carry):
        k0 = pl.multiple_of(t0 - u * KT, KT)
        s = _dot(kw_ref[0, pl.ds(k0, KT), :], qpad)
        d = tq_tile - (k0 + krow)
        bias = jnp.where(d >= 0, jnp.where(d < WINDOW, 0.0, NEG_INF), NEG_INF)
        return online_step(s + tile4(bias), vwt_ref[0, qb - u], *carry)

    _, l_w, acc_w = lax.fori_loop(0, jnp.minimum(qb, WINDOW // KT) + 1, win_body, init)

    gates = _sigmoid(gt_ref[0, 0])
    grow = lambda j: jnp.concatenate([gates[j, r:r + 1, :] for r in range(R)], axis=1)
    o = grow(0) * o_c + grow(1) * (acc_s / l_s) + grow(2) * (acc_w / l_w)
    halves = []
    for h in range(R // 2):
        pair = jnp.concatenate([o[:, (2 * h) * QT:(2 * h + 1) * QT],
                                o[:, (2 * h + 1) * QT:(2 * h + 2) * QT]], axis=0)
        halves.append(pair.T)
    o_ref[...] = jnp.concatenate(halves, axis=1)


def _nsa_attn(qt, kcmp, vct, ks3, vst, kw3, vwt, gt, ov_t, B, T):
    G, R = NSA_KV_HEADS, NSA_GROUP
    nq = T // Q_TILE
    nk = T // KEY_TILE
    nch = kcmp.shape[1]
    n_sel = ov_t.shape[0]
    return pl.pallas_call(
        _nsa_attn_kernel,
        grid=(B, G, nq),
        in_specs=[pl.BlockSpec((1, 1, R * HEAD_DIM, Q_TILE), lambda b, g, q: (b, q, g, 0)),
                  pl.BlockSpec((1, nch, LANES), lambda b, g, q: (b, 0, 0)),
                  pl.BlockSpec((1, HEAD_DIM, nch), lambda b, g, q: (b, g, 0)),
                  pl.BlockSpec((1, T, LANES), lambda b, g, q: (b, 0, 0)),
                  pl.BlockSpec((1, nk, HEAD_DIM, KEY_TILE), lambda b, g, q: (b, 0, g, 0)),
                  pl.BlockSpec((1, T, LANES), lambda b, g, q: (b, 0, 0)),
                  pl.BlockSpec((1, nk, HEAD_DIM, KEY_TILE), lambda b, g, q: (b, 0, g, 0)),
                  pl.BlockSpec((1, 1, 3, R, Q_TILE), lambda b, g, q: (b, g, 0, 0, q)),
                  pl.BlockSpec((n_sel, nch), lambda b, g, q: (0, 0))],
        out_specs=pl.BlockSpec((Q_TILE, R * HEAD_DIM), lambda b, g, q: (b * nq + q, g)),
        out_shape=jax.ShapeDtypeStruct((B * T, NSA_WIDTH), F32),
        compiler_params=_params("arbitrary", "arbitrary", "arbitrary"),
        name="nsa_attn",
    )(qt, kcmp, vct, ks3, vst, kw3, vwt, gt, ov_t)


def _merge_kernel(ys_ref, g_ref, bonus_ref, gng_ref, gnb_ref, bd_ref, yb_ref, pm_ref, x_ref, mod_ref,
                  ng_ref, wa_ref, wb_ref, wo_ref, rw_ref, o_x, o_h, o_score):
    m = mod_ref[pl.program_id(0)]
    bd = bd_ref[...]
    y = ys_ref[...]
    mean = _dot(y, bd, HI)
    yc = y - mean
    var = _dot(yc * yc, bd, HI)
    ya = (yc * lax.rsqrt(var + RWKV_GN_EPS) * gng_ref[...] + gnb_ref[...] + bonus_ref[...]) * g_ref[...]
    pm = pm_ref[...]
    D = x_ref.shape[1]
    mix = (_sigmoid(pm[:, 0:D]) * _dot(ya.astype(BF16), wa_ref[...])
           + _sigmoid(pm[:, D:2 * D]) * _dot(yb_ref[...].astype(BF16), wb_ref[...]))
    x = x_ref[...] + m[2:3] * _dot(mix.astype(BF16), wo_ref[...])
    o_x[...] = x
    ms = jnp.mean(x * x, axis=-1, keepdims=True)
    h = x * lax.rsqrt(ms + NORM_EPS) * ng_ref[...]
    h = h * (1.0 + m[4:5]) + m[3:4]
    o_h[...] = h
    o_score[...] = _sigmoid(_dot(h, rw_ref[...], HI))


def _merge(ys, g, bonus, gn_g, gn_b, yb, pm, x2, mod, ng, wa, wb, wo, rw_pad, B, T, tm=256):
    N, D = x2.shape
    W = RWKV_WIDTH
    nt = T // tm
    row = lambda b, t: (b * nt + t, 0)
    full = lambda shape: pl.BlockSpec(shape, lambda b, t: (0,) * len(shape))
    return pl.pallas_call(
        _merge_kernel,
        grid=(B, nt),
        in_specs=[pl.BlockSpec((tm, W), row), pl.BlockSpec((tm, W), row), pl.BlockSpec((tm, W), row),
                  full((1, W)), full((1, W)), full((W, W)),
                  pl.BlockSpec((tm, NSA_WIDTH), row), pl.BlockSpec((tm, 2 * D), row),
                  pl.BlockSpec((tm, D), row), full((B, 6, D)), full((1, D)),
                  full((W, D)), full((NSA_WIDTH, D)), full((D, D)), full((D, LANES))],
        out_specs=[pl.BlockSpec((tm, D), row), pl.BlockSpec((tm, D), row), pl.BlockSpec((tm, LANES), row)],
        out_shape=[jax.ShapeDtypeStruct((N, D), F32), jax.ShapeDtypeStruct((N, D), F32),
                   jax.ShapeDtypeStruct((N, LANES), F32)],
        compiler_params=_params("arbitrary", "arbitrary"),
        name="merge_out",
    )(ys, g, bonus, gn_g.reshape(1, W), gn_b.reshape(1, W), _head_block_diag(W, 1.0 / HEAD_DIM),
      yb, pm, x2, mod, ng, wa, wb, wo, rw_pad)


def _route(score, router_b, N):
    sel_score = score + router_b.astype(F32)
    grp = sel_score.reshape(N, N_GROUPS, EXPERTS_PER_GROUP)
    grp_score = jnp.sum(lax.top_k(grp, TOP_K)[0], axis=-1)
    g_star = jnp.argmax(grp_score, axis=-1).astype(jnp.int32)
    in_grp = jnp.take_along_axis(grp, g_star[:, None, None], axis=1)[:, 0]
    _, loc = lax.top_k(in_grp, TOP_K)
    expert = g_star[:, None] * EXPERTS_PER_GROUP + loc.astype(jnp.int32)
    wts = jnp.take_along_axis(score, expert, axis=1)
    wts = wts / jnp.sum(wts, axis=-1, keepdims=True)
    NK = N * TOP_K
    e_flat = expert.T.reshape(-1)
    onehot = (e_flat[:, None] == jnp.arange(N_EXPERTS, dtype=jnp.int32)[None, :]).astype(jnp.int32)
    csum = jnp.cumsum(onehot, axis=0)
    counts = csum[-1]
    rank = jnp.take_along_axis(csum, e_flat[:, None], axis=1)[:, 0] - 1
    padded = (counts + MOE_BLOCK - 1) // MOE_BLOCK * MOE_BLOCK
    pad_end = jnp.cumsum(padded)
    pad_start = pad_end - padded
    dest = pad_start[e_flat] + rank
    n_blk = -(-NK // MOE_BLOCK) + N_EXPERTS
    P = n_blk * MOE_BLOCK
    slot_assign = jnp.full((P,), -1, jnp.int32).at[dest].set(jnp.arange(NK, dtype=jnp.int32))
    blk_start = jnp.arange(n_blk, dtype=jnp.int32) * MOE_BLOCK
    blk_expert = jnp.clip(jnp.searchsorted(pad_end, blk_start, side='right'), 0, N_EXPERTS - 1)
    blk_used = (blk_start < pad_end[-1]).astype(jnp.int32)
    return wts, slot_assign, blk_expert.astype(jnp.int32), blk_used, n_blk


def _moe_kernel(be_ref, used_ref, slot_ref, h_hbm, wg_ref, wu_ref, wd_ref, o_hbm, xbuf, ybuf, sem_in,
                sem_out, *, n_tok):
    i = pl.program_id(0)
    MB = MOE_BLOCK
    base = i * MB

    @pl.when(used_ref[i] > 0)
    def _():
        def gather(r, _):
            a = slot_ref[base + r]
            tok = jnp.where(a < 0, 0, jnp.where(a >= n_tok, a - n_tok, a))
            pltpu.make_async_copy(h_hbm.at[pl.ds(tok, 1), :], xbuf.at[pl.ds(r, 1), :], sem_in).start()
            return 0

        lax.fori_loop(0, MB, gather, 0)

        def gather_wait(r, _):
            pltpu.make_async_copy(h_hbm.at[pl.ds(0, 1), :], xbuf.at[pl.ds(r, 1), :], sem_in).wait()
            return 0

        lax.fori_loop(0, MB, gather_wait, 0)
        x = xbuf[...].astype(BF16)
        gate = _dot(x, wg_ref[0])
        up = _dot(x, wu_ref[0])
        act = (gate * _sigmoid(gate) * up).astype(BF16)
        ybuf[...] = _dot(act, wd_ref[0])

        def scatter(r, _):
            a = slot_ref[base + r]

            @pl.when(a >= 0)
            def _():
                pltpu.make_async_copy(ybuf.at[pl.ds(r, 1), :], o_hbm.at[pl.ds(a, 1), :], sem_out).start()
            return 0

        lax.fori_loop(0, MB, scatter, 0)

        def scatter_wait(r, _):
            @pl.when(slot_ref[base + r] >= 0)
            def _():
                pltpu.make_async_copy(ybuf.at[pl.ds(r, 1), :], o_hbm.at[pl.ds(0, 1), :], sem_out).wait()
            return 0

        lax.fori_loop(0, MB, scatter_wait, 0)


def _moe(h_bf, slot_assign, blk_expert, blk_used, n_blk, wg, wu, wd):
    N, D = h_bf.shape
    DE = wg.shape[2]
    grid_spec = pltpu.PrefetchScalarGridSpec(
        num_scalar_prefetch=3,
        grid=(n_blk,),
        in_specs=[pl.BlockSpec(memory_space=pl.ANY),
                  pl.BlockSpec((1, D, DE), lambda i, be, us, sl: (be[i], 0, 0)),
                  pl.BlockSpec((1, D, DE), lambda i, be, us, sl: (be[i], 0, 0)),
                  pl.BlockSpec((1, DE, D), lambda i, be, us, sl: (be[i], 0, 0))],
        out_specs=pl.BlockSpec(memory_space=pl.ANY),
        scratch_shapes=[pltpu.VMEM((MOE_BLOCK, D), F32), pltpu.VMEM((MOE_BLOCK, D), F32),
                        pltpu.SemaphoreType.DMA(()), pltpu.SemaphoreType.DMA(())],
    )
    return pl.pallas_call(
        functools.partial(_moe_kernel, n_tok=N),
        grid_spec=grid_spec,
        out_shape=jax.ShapeDtypeStruct((TOP_K * N, D), F32),
        compiler_params=_params("arbitrary"),
        name="moe_experts",
    )(blk_expert, blk_used, slot_assign, h_bf, wg, wu, wd)


def _final_kernel(x_ref, y0_ref, y1_ref, w_ref, mod_ref, o_ref):
    m = mod_ref[pl.program_id(0)]
    w = w_ref[...]
    o_ref[...] = x_ref[...] + m[5:6] * (w[:, 0:1] * y0_ref[...] + w[:, 1:2] * y1_ref[...])


def _final(x2, ybuf, wts, mod, B, T, tm=512):
    N, D = x2.shape
    nt = T // tm
    row = lambda b, t: (b * nt + t, 0)
    return pl.pallas_call(
        _final_kernel,
        grid=(B, nt),
        in_specs=[pl.BlockSpec((tm, D), row), pl.BlockSpec((tm, D), row),
                  pl.BlockSpec((tm, D), lambda b, t: (N // tm + b * nt + t, 0)),
                  pl.BlockSpec((tm, TOP_K), row), pl.BlockSpec((B, 6, D), lambda b, t: (0, 0, 0))],
        out_specs=pl.BlockSpec((tm, D), row),
        out_shape=jax.ShapeDtypeStruct((N, D), F32),
        compiler_params=_params("arbitrary", "arbitrary"),
        name="moe_combine",
    )(x2, ybuf, ybuf, wts, mod)


def _overlap_t(n_sel, n_cmp_pad):
    ci = jnp.arange(n_cmp_pad)[None, :] * CMP_STRIDE
    sj = jnp.arange(n_sel)[:, None] * SEL_BLOCK
    ov = (ci <= sj + SEL_BLOCK - 1) & (ci + CMP_BLOCK - 1 >= sj) & (jnp.arange(n_cmp_pad)[None, :] < n_cmp_pad - 1)
    return ov.astype(F32)


def kernel(x, c, w_ada, b_ada, norm_g, w_in, b_in, rwkv_mu, rwkv_w0, rwkv_w2, rwkv_a0, rwkv_a2, rwkv_g2,
           rwkv_k_k, rwkv_k_a, rwkv_r_k, rwkv_gn_g, rwkv_gn_b, qk_norm_g, cmp_pos, cmp_w1, cmp_w2,
           w_up_rwkv, w_up_nsa, w_out, router_w, router_b, exp_w_gate, exp_w_up, exp_w_down):
    B, T, D = x.shape
    L = w_ada.shape[0]
    N = B * T
    mods = _ada(c, w_ada, b_ada)
    tables = _rope_tables(jnp.arange(T, dtype=jnp.int32))
    nch = T // CMP_STRIDE
    tables_cmp = _rope_tables(jnp.arange(nch, dtype=jnp.int32) * CMP_STRIDE + CMP_BLOCK - 1)
    ov_t = _overlap_t(T // SEL_BLOCK, nch)
    rw_pad = jnp.zeros((D, LANES), F32).at[:, :N_EXPERTS].set(router_w)
    n_gate = NSA_GATE_COLS
    x2 = x.reshape(N, D)
    for l in range(L):
        g0 = _SEG_KV[1] + n_gate
        w_pad = jnp.concatenate([w_in[l][:, :g0], jnp.zeros((D, GATE_PAD - n_gate), F32), w_in[l][:, g0:]],
                                axis=1).astype(BF16)
        b_pad = jnp.concatenate([b_in[l][:g0], jnp.zeros((GATE_PAD - n_gate,), F32), b_in[l][g0:]]).reshape(1, -1)
        p_rw, p_q, p_kv, p_gate, p_merge = _inproj(x2, mods[l], norm_g[l, 0].reshape(1, D), w_pad, b_pad, B, T)
        r, k, v, al, bb, ld, g, bonus = _rwkv_pre(p_rw, rwkv_mu[l], rwkv_w0[l], rwkv_w2[l], rwkv_a0[l],
                                                  rwkv_a2[l], rwkv_g2[l], rwkv_k_k[l], rwkv_k_a[l],
                                                  rwkv_r_k[l], B, T)
        ys = _rwkv_scan(r, k, v, al, bb, ld, B, T)
        qt, ks, kw, vst, vwt = _nsa_prep(p_q, p_kv, tables, qk_norm_g[l], B, T)
        kv3 = p_kv.reshape(B, T, KV_COLS)
        kcmp = _nsa_cmp(kv3, 0, cmp_pos[l], cmp_w1[l], cmp_w2[l], qk_norm_g[l, 1], tables_cmp)
        vct = _nsa_cmp(kv3, 1, cmp_pos[l], cmp_w1[l], cmp_w2[l], None, None)
        gt = p_gate[:, :n_gate].reshape(B, T, NSA_KV_HEADS, NSA_GROUP, 3).transpose(0, 2, 4, 3, 1)
        yb = _nsa_attn(qt, kcmp, vct, ks.reshape(B, T, LANES), vst, kw.reshape(B, T, LANES), vwt, gt, ov_t,
                       B, T)
        x2, h2, score = _merge(ys, g, bonus, rwkv_gn_g[l], rwkv_gn_b[l], yb, p_merge, x2, mods[l],
                               norm_g[l, 1].reshape(1, D), w_up_rwkv[l].astype(BF16),
                               w_up_nsa[l].astype(BF16), w_out[l].astype(BF16), rw_pad, B, T)
        wts, slot_assign, blk_expert, blk_used, n_blk = _route(score[:, :N_EXPERTS], router_b, N)
        ybuf = _moe(h2, slot_assign, blk_expert, blk_used, n_blk, exp_w_gate[l].astype(BF16),
                    exp_w_up[l].astype(BF16), exp_w_down[l].astype(BF16))
        x2 = _final(x2, ybuf, wts, mods[l], B, T)
    return x2.reshape(B, T, D)
```

```python
import functools
import math

import jax
import jax.numpy as jnp
from jax import lax
from jax.experimental import pallas as pl
from jax.experimental.pallas import tpu as pltpu

F32 = jnp.float32
BF16 = jnp.bfloat16
HI = lax.Precision.HIGHEST

D_MODEL = 1024
RWKV_HEADS = 8
HEAD_DIM = 64
RWKV_WIDTH = RWKV_HEADS * HEAD_DIM
DECAY_LORA = 64
ICLR_LORA = 64
GATE_LORA = 128
RWKV_GN_EPS = 64e-5
RWKV_COLS = 3 * RWKV_WIDTH + DECAY_LORA + ICLR_LORA + GATE_LORA

NSA_Q_HEADS = 8
NSA_KV_HEADS = 2
NSA_GROUP = NSA_Q_HEADS // NSA_KV_HEADS
NSA_WIDTH = NSA_Q_HEADS * HEAD_DIM
CMP_STRIDE = 16
CMP_BLOCK = 2 * CMP_STRIDE
CMP_HIDDEN = 256
SEL_BLOCK = 64
SEL_SHIFT = 6
SEL_TOPK = 16
WINDOW = 512
FORCE_SCORE = 1e4
NEG_INF = -1e30
ROPE_THETA = 500000.0
ROPE_DIM = HEAD_DIM // 4
KV_COLS = 6 * NSA_KV_HEADS * HEAD_DIM
NSA_GATE_COLS = 3 * NSA_Q_HEADS
GATE_PAD = 128

N_EXPERTS = 16
N_GROUPS = 4
EXPERTS_PER_GROUP = N_EXPERTS // N_GROUPS
TOP_K = 2
D_EXPERT = 512
MOE_BLOCK = 256
NORM_EPS = 1e-6

LANES = 128
CHUNK = 64
KEY_TILE = 128
Q_TILE = 128
F32_TINY = float(jnp.finfo(jnp.float32).tiny)

_SEG_RW = (0, RWKV_COLS)
_SEG_Q = (_SEG_RW[1], _SEG_RW[1] + NSA_WIDTH)
_SEG_KV = (_SEG_Q[1], _SEG_Q[1] + KV_COLS)
_SEG_GATE = (_SEG_KV[1], _SEG_KV[1] + GATE_PAD)
_SEG_MERGE = (_SEG_GATE[1], _SEG_GATE[1] + 2 * D_MODEL)
IN_COLS_PAD = _SEG_MERGE[1]

_VMEM_LIMIT = 56 * 1024 * 1024


def _dot(a, b, precision=None):
    return jnp.dot(a, b, preferred_element_type=F32, precision=precision)


def _dot_tb(a, b, precision=None):
    return lax.dot_general(a, b, (((1,), (1,)), ((), ())), preferred_element_type=F32,
                           precision=precision)


def _dot_ta(a, b, precision=None):
    return lax.dot_general(a, b, (((0,), (0,)), ((), ())), preferred_element_type=F32,
                           precision=precision)


def _params(*sem):
    return pltpu.CompilerParams(dimension_semantics=sem, vmem_limit_bytes=_VMEM_LIMIT)


def _sigmoid(x):
    return 1.0 / (1.0 + jnp.exp(-x))


def _ada_kernel(c_ref, w_ref, b_ref, o_ref):
    c = c_ref[...]
    s = c * _sigmoid(c)
    o_ref[0] = _dot(s, w_ref[0], HI) + b_ref[0]


def _ada(c, w_ada, b_ada):
    L, D, D6 = w_ada.shape
    B = c.shape[0]
    rows = 8
    cp = jnp.zeros((rows, D), F32).at[:B].set(c)
    tn = 1536
    out = pl.pallas_call(
        _ada_kernel,
        grid=(L, D6 // tn),
        in_specs=[pl.BlockSpec((rows, D), lambda l, j: (0, 0)),
                  pl.BlockSpec((1, D, tn), lambda l, j: (l, 0, j)),
                  pl.BlockSpec((1, 1, tn), lambda l, j: (l, 0, j))],
        out_specs=pl.BlockSpec((1, rows, tn), lambda l, j: (l, 0, j)),
        out_shape=jax.ShapeDtypeStruct((L, rows, D6), F32),
        compiler_params=_params("arbitrary", "arbitrary"),
        name="ada_mod",
    )(cp, w_ada, b_ada.reshape(L, 1, D6))
    return out[:, :B].reshape(L, B, 6, D)


def _inproj_kernel(x_ref, mod_ref, g_ref, w_ref, b_ref, o_rw, o_q, o_kv, o_gate, o_merge):
    m = mod_ref[pl.program_id(0)]
    x = x_ref[...]
    ms = jnp.mean(x * x, axis=-1, keepdims=True)
    h = x * lax.rsqrt(ms + NORM_EPS) * g_ref[...]
    h = h * (1.0 + m[1:2]) + m[0:1]
    hb = h.astype(BF16)
    for o, (a, e) in ((o_rw, _SEG_RW), (o_q, _SEG_Q), (o_kv, _SEG_KV), (o_gate, _SEG_GATE),
                      (o_merge, _SEG_MERGE)):
        o[...] = _dot(hb, w_ref[:, a:e]) + b_ref[:, a:e]


def _inproj(x2, mod, g, w_pad, b_pad, B, T, tm=256):
    N, D = x2.shape
    nt = T // tm
    row = lambda b, t: (b * nt + t, 0)
    widths = [e - a for a, e in (_SEG_RW, _SEG_Q, _SEG_KV, _SEG_GATE, _SEG_MERGE)]
    return pl.pallas_call(
        _inproj_kernel,
        grid=(B, nt),
        in_specs=[pl.BlockSpec((tm, D), row),
                  pl.BlockSpec((B, 6, D), lambda b, t: (0, 0, 0)),
                  pl.BlockSpec((1, D), lambda b, t: (0, 0)),
                  pl.BlockSpec((D, IN_COLS_PAD), lambda b, t: (0, 0)),
                  pl.BlockSpec((1, IN_COLS_PAD), lambda b, t: (0, 0))],
        out_specs=[pl.BlockSpec((tm, w), row) for w in widths],
        out_shape=[jax.ShapeDtypeStruct((N, w), F32) for w in widths],
        compiler_params=_params("arbitrary", "arbitrary"),
        name="in_proj",
    )(x2, mod, g, w_pad, b_pad)


def _rwkv_pre_kernel(p_ref, mu_ref, w0_ref, w2_ref, a0_ref, a2_ref, g2_ref, kk_ref, ka_ref, rk_ref,
                     bd_ref, o_r, o_k, o_v, o_al, o_b, o_ld, o_g, o_bonus, carry_ref):
    W = RWKV_WIDTH

    @pl.when(pl.program_id(1) == 0)
    def _():
        carry_ref[...] = jnp.zeros_like(carry_ref)

    p = p_ref[...]
    ts = p.shape[0]
    rows = lax.broadcasted_iota(jnp.int32, p.shape, 0)
    shifted = jnp.where(rows == 0, carry_ref[0:1, :], pltpu.roll(p, 1, 0))
    carry_ref[0:1, :] = p[ts - 1:ts, :]
    pm = p + (shifted - p) * mu_ref[...]
    r = pm[:, 0:W]
    k = pm[:, W:2 * W]
    v = pm[:, 2 * W:3 * W]
    wa = pm[:, 3 * W:3 * W + DECAY_LORA + ICLR_LORA]
    gl = pm[:, 3 * W + DECAY_LORA + ICLR_LORA:]
    xw = w0_ref[...] + _dot(jnp.tanh(wa), w2_ref[...], HI)
    ld = -math.exp(-0.5) * _sigmoid(xw)
    a = _sigmoid(a0_ref[...] + _dot(wa, a2_ref[...], HI))
    g = _dot(_sigmoid(gl), g2_ref[...], HI)
    bd = bd_ref[...]
    kk = k * kk_ref[...]
    nrm = jnp.sqrt(_dot(kk * kk, bd, HI))
    kk = kk / jnp.maximum(nrm, 1e-12)
    k2 = k * (1.0 + (a - 1.0) * ka_ref[...])
    bonus = _dot(r * k2 * rk_ref[...], bd, HI) * v
    o_r[...] = r
    o_k[...] = k2
    o_v[...] = v
    o_al[...] = kk
    o_b[...] = -kk * a
    o_ld[...] = ld
    o_g[...] = g
    o_bonus[...] = bonus


def _head_block_diag(width, scale=1.0):
    i = jnp.arange(width) // HEAD_DIM
    return (i[:, None] == i[None, :]).astype(F32) * scale


def _rwkv_pre(p_rw, mu, w0, w2, a0, a2, g2, k_k, k_a, r_k, B, T, ts=256):
    N = p_rw.shape[0]
    W = RWKV_WIDTH
    nt = T // ts
    row = lambda b, t: (b * nt + t, 0)
    zl = jnp.zeros((DECAY_LORA, W), F32)
    w2p = jnp.concatenate([w2, zl], axis=0)
    a2p = jnp.concatenate([zl, a2], axis=0)
    full = lambda shape: pl.BlockSpec(shape, lambda b, t: (0,) * len(shape))
    vec = lambda z: z.reshape(1, -1)
    return pl.pallas_call(
        _rwkv_pre_kernel,
        grid=(B, nt),
        in_specs=[pl.BlockSpec((ts, RWKV_COLS), row), full((1, RWKV_COLS)), full((1, W)),
                  full((2 * DECAY_LORA, W)), full((1, W)), full((2 * DECAY_LORA, W)),
                  full((GATE_LORA, W)), full((1, W)), full((1, W)), full((1, W)), full((W, W))],
        out_specs=[pl.BlockSpec((ts, W), row)] * 8,
        out_shape=[jax.ShapeDtypeStruct((N, W), F32)] * 8,
        scratch_shapes=[pltpu.VMEM((8, RWKV_COLS), F32)],
        compiler_params=_params("arbitrary", "arbitrary"),
        name="rwkv_pre",
    )(p_rw, vec(mu), vec(w0), w2p, vec(a0), a2p, g2, vec(k_k), vec(k_a), vec(r_k),
      _head_block_diag(W))


def _scan_chunk(r, k, v, al, bb, ld, H, tri, eye, strict, incl, m0, m1):
    C = CHUNK
    cum = _dot(tri, ld, HI)
    tot = cum[C - 1:C, :]
    rt = r * jnp.exp(cum)
    at = al * jnp.exp(cum - ld)
    dinv = jnp.exp(-cum)
    bt = bb * dinv
    kt = k * dinv
    dend = jnp.exp(tot - cum)
    bh = bb * dend
    kh = k * dend
    st = lambda z: jnp.concatenate([z * m0, z * m1], axis=0)
    at_s, rt_s, vs = st(at), st(rt), st(v)
    A = _dot_tb(jnp.concatenate([at_s, rt_s], axis=0),
                jnp.concatenate([st(bt), st(kt)], axis=0), HI)
    zero = jnp.zeros((2 * C, 2 * C), F32)
    a_ab = jnp.where(strict, A[0:2 * C, 0:2 * C], zero)
    a_ak = jnp.where(strict, A[0:2 * C, 2 * C:4 * C], zero)
    a_rb = jnp.where(incl, A[2 * C:4 * C, 0:2 * C], zero)
    a_rk = jnp.where(incl, A[2 * C:4 * C, 2 * C:4 * C], zero)
    pw = a_ab
    tinv = eye + pw
    for _ in range(5):
        pw = _dot(pw, pw, HI)
        tinv = tinv + _dot(pw, tinv, HI)
    X = _dot(tinv, jnp.concatenate([at_s, _dot(a_ak, vs, HI)], axis=1), HI)
    Z = _dot(jnp.concatenate([X[:, 0:LANES], rt_s], axis=0), H, HI)
    U = Z[0:2 * C] + X[:, LANES:2 * LANES]
    UV = jnp.concatenate([U, vs], axis=0)
    Y = Z[2 * C:4 * C] + _dot(jnp.concatenate([a_rb, a_rk], axis=1), UV, HI)
    y = Y[0:C] + Y[C:2 * C]
    dC = eye * jnp.exp(tot)
    Hn = _dot_ta(jnp.concatenate([dC, st(bh), st(kh)], axis=0),
                 jnp.concatenate([H, UV], axis=0), HI)
    return y, Hn


def _rwkv_scan_kernel(r_ref, k_ref, v_ref, al_ref, b_ref, ld_ref, o_ref, h_ref):
    C = CHUNK

    @pl.when(pl.program_id(2) == 0)
    def _():
        h_ref[...] = jnp.zeros_like(h_ref)

    ri = lax.broadcasted_iota(jnp.int32, (C, C), 0)
    ci = lax.broadcasted_iota(jnp.int32, (C, C), 1)
    tri = (ci <= ri).astype(F32)
    r2 = lax.broadcasted_iota(jnp.int32, (2 * C, 2 * C), 0)
    c2 = lax.broadcasted_iota(jnp.int32, (2 * C, 2 * C), 1)
    eye = (r2 == c2).astype(F32)
    strict = (c2 & (C - 1)) < (r2 & (C - 1))
    incl = (c2 & (C - 1)) <= (r2 & (C - 1))
    lane = lax.broadcasted_iota(jnp.int32, (C, LANES), 1)
    m0 = (lane < HEAD_DIM).astype(F32)
    m1 = 1.0 - m0
    H = h_ref[...]
    for c in range(r_ref.shape[0] // C):
        sl = slice(c * C, (c + 1) * C)
        y, H = _scan_chunk(r_ref[sl, :], k_ref[sl, :], v_ref[sl, :], al_ref[sl, :], b_ref[sl, :],
                           ld_ref[sl, :], H, tri, eye, strict, incl, m0, m1)
        o_ref[sl, :] = y
    h_ref[...] = H


def _rwkv_scan(r, k, v, al, bb, ld, B, T, tc=256):
    N, W = r.shape
    nt = T // tc
    spec = pl.BlockSpec((tc, LANES), lambda b, h, t: (b * nt + t, h))
    return pl.pallas_call(
        _rwkv_scan_kernel,
        grid=(B, W // LANES, nt),
        in_specs=[spec] * 6,
        out_specs=spec,
        out_shape=jax.ShapeDtypeStruct((N, W), F32),
        scratch_shapes=[pltpu.VMEM((LANES, LANES), F32)],
        compiler_params=_params("arbitrary", "arbitrary", "arbitrary"),
        name="rwkv_scan",
    )(r, k, v, al, bb, ld)


def _rope_tables(pos):
    half = ROPE_DIM // 2
    inv = jnp.power(ROPE_THETA, -jnp.arange(half, dtype=F32) * 2.0 / ROPE_DIM)
    ang = pos.astype(F32)[:, None] * inv[None, :]
    cos, sin = jnp.cos(ang), jnp.sin(ang)
    n = pos.shape[0]
    rest = HEAD_DIM - ROPE_DIM
    c = jnp.concatenate([cos, cos, jnp.ones((n, rest), F32)], axis=1)
    s_dn = jnp.concatenate([-sin, jnp.zeros((n, half + rest), F32)], axis=1)
    s_up = jnp.concatenate([jnp.zeros((n, half), F32), sin, jnp.zeros((n, rest), F32)], axis=1)
    rep = LANES // HEAD_DIM
    return jnp.tile(c, (1, rep)), jnp.tile(s_dn, (1, rep)), jnp.tile(s_up, (1, rep))


def _norm_rope(x, bd, g, c, s_dn, s_up):
    width = x.shape[1]
    half = ROPE_DIM // 2
    rep = width // LANES
    tile = (lambda z: jnp.concatenate([z] * rep, axis=1)) if rep > 1 else (lambda z: z)
    ms = _dot(x * x, bd, HI)
    xn = x * lax.rsqrt(ms + NORM_EPS) * g
    return (xn * tile(c) + pltpu.roll(xn, width - half, 1) * tile(s_dn)
            + pltpu.roll(xn, half, 1) * tile(s_up))


def _nsa_prep_kernel(q_ref, kv_ref, c_ref, sd_ref, su_ref, gq_ref, gs_ref, gw_ref, bdq_ref, bdk_ref,
                     o_qt, o_ks, o_kw, o_vst, o_vwt):
    c, sd, su = c_ref[...], sd_ref[...], su_ref[...]
    q = _norm_rope(q_ref[...], bdq_ref[...], gq_ref[...], c, sd, su) * (HEAD_DIM ** -0.5)
    qt = q.T
    kv = kv_ref[...]
    bdk = bdk_ref[...]
    o_ks[...] = _norm_rope(kv[:, 2 * LANES:3 * LANES], bdk, gs_ref[...], c, sd, su).astype(BF16)
    o_kw[...] = _norm_rope(kv[:, 4 * LANES:5 * LANES], bdk, gw_ref[...], c, sd, su).astype(BF16)
    vst = kv[:, 3 * LANES:4 * LANES].T
    vwt = kv[:, 5 * LANES:6 * LANES].T
    for j in range(q.shape[0] // KEY_TILE):
        sl = slice(j * KEY_TILE, (j + 1) * KEY_TILE)
        o_qt[0, j] = qt[:, sl].astype(BF16)
        o_vst[0, j] = vst[:, sl].astype(BF16)
        o_vwt[0, j] = vwt[:, sl].astype(BF16)


def _nsa_prep(q, kv, tables, qk_g, B, T, ts=256):
    N = q.shape[0]
    nt = T // ts
    nk = ts // KEY_TILE
    row = lambda b, t: (b * nt + t, 0)
    full = lambda shape: pl.BlockSpec(shape, lambda b, t: (0,) * len(shape))
    tab = pl.BlockSpec((ts, LANES), lambda b, t: (t, 0))
    gq = jnp.tile(qk_g[0], NSA_Q_HEADS).reshape(1, NSA_WIDTH)
    gs = jnp.tile(qk_g[2], NSA_KV_HEADS).reshape(1, LANES)
    gw = jnp.tile(qk_g[3], NSA_KV_HEADS).reshape(1, LANES)
    tiled = lambda rows: pl.BlockSpec((1, nk, rows, KEY_TILE), lambda b, t: (b, t, 0, 0))
    return pl.pallas_call(
        _nsa_prep_kernel,
        grid=(B, nt),
        in_specs=[pl.BlockSpec((ts, NSA_WIDTH), row), pl.BlockSpec((ts, KV_COLS), row), tab, tab, tab,
                  full((1, NSA_WIDTH)), full((1, LANES)), full((1, LANES)),
                  full((NSA_WIDTH, NSA_WIDTH)), full((LANES, LANES))],
        out_specs=[tiled(NSA_WIDTH), pl.BlockSpec((ts, LANES), row), pl.BlockSpec((ts, LANES), row),
                   tiled(LANES), tiled(LANES)],
        out_shape=[jax.ShapeDtypeStruct((B, T // KEY_TILE, NSA_WIDTH, KEY_TILE), BF16),
                   jax.ShapeDtypeStruct((N, LANES), BF16), jax.ShapeDtypeStruct((N, LANES), BF16),
                   jax.ShapeDtypeStruct((B, T // KEY_TILE, LANES, KEY_TILE), BF16),
                   jax.ShapeDtypeStruct((B, T // KEY_TILE, LANES, KEY_TILE), BF16)],
        compiler_params=_params("arbitrary", "arbitrary"),
        name="nsa_prep",
    )(q, kv, *tables, gq, gs, gw, _head_block_diag(NSA_WIDTH, 1.0 / HEAD_DIM),
      _head_block_diag(LANES, 1.0 / HEAD_DIM))


def _gelu_tanh(x):
    return 0.5 * x * (1.0 + jnp.tanh(0.7978845608028654 * (x + 0.044715 * x * x * x)))


def _nsa_cmp_kernel(x_ref, pos_ref, w1_ref, w2_ref, *rest, is_key):
    if is_key:
        g_ref, c_ref, sd_ref, su_ref, bd_ref, o_ref, xs_ref = rest
    else:
        o_ref, xs_ref = rest
    nch = xs_ref.shape[0]
    S = CMP_STRIDE
    for j in range(S):
        xs_ref[:, j * LANES:(j + 1) * LANES] = x_ref[0, pl.ds(j, nch, stride=S), :]
    xs = xs_ref[...]
    first = _dot((xs + pos_ref[0:1, :]).astype(BF16), w1_ref[0])
    second = _dot((xs + pos_ref[1:2, :]).astype(BF16), w1_ref[1])
    hid = first + pltpu.roll(second, nch - 1, 0)
    out = _dot(_gelu_tanh(hid).astype(BF16), w2_ref[...])
    rows = lax.broadcasted_iota(jnp.int32, out.shape, 0)
    if is_key:
        out = _norm_rope(out, bd_ref[...], g_ref[...], c_ref[...], sd_ref[...], su_ref[...])
        o_ref[0] = jnp.where(rows < nch - 1, out, 0.0).astype(BF16)
    else:
        o_ref[0] = jnp.where(rows < nch - 1, out, 0.0).T.astype(BF16)


def _nsa_cmp(kv3, which, cmp_pos, cmp_w1, cmp_w2, g_k, tables_cmp):
    B, T, _ = kv3.shape
    S = CMP_STRIDE
    nch = T // S
    is_key = which == 0
    eye2 = jnp.eye(NSA_KV_HEADS, dtype=F32)
    w1 = cmp_w1[which].reshape(CMP_BLOCK, HEAD_DIM, CMP_HIDDEN)
    w1 = jnp.einsum('jdh,ge->jgdeh', w1, eye2).reshape(2, S * LANES, NSA_KV_HEADS * CMP_HIDDEN)
    w2 = jnp.einsum('hd,ge->ghed', cmp_w2[which], eye2).reshape(NSA_KV_HEADS * CMP_HIDDEN, LANES)
    pos = jnp.tile(cmp_pos[which].reshape(2, S, 1, HEAD_DIM), (1, 1, NSA_KV_HEADS, 1)).reshape(2, S * LANES)
    full = lambda shape: pl.BlockSpec(shape, lambda b: (0,) * len(shape))
    in_specs = [pl.BlockSpec((1, T, LANES), lambda b: (b, 0, which)), full(pos.shape), full(w1.shape),
                full(w2.shape)]
    args = [kv3, pos, w1.astype(BF16), w2.astype(BF16)]
    if is_key:
        in_specs += [full((1, LANES)), full((nch, LANES)), full((nch, LANES)), full((nch, LANES)),
                     full((LANES, LANES))]
        args += [jnp.tile(g_k, NSA_KV_HEADS).reshape(1, LANES), *tables_cmp,
                 _head_block_diag(LANES, 1.0 / HEAD_DIM)]
        out_spec = pl.BlockSpec((1, nch, LANES), lambda b: (b, 0, 0))
        out_shape = jax.ShapeDtypeStruct((B, nch, LANES), BF16)
    else:
        out_spec = pl.BlockSpec((1, LANES, nch), lambda b: (b, 0, 0))
        out_shape = jax.ShapeDtypeStruct((B, LANES, nch), BF16)
    return pl.pallas_call(
        functools.partial(_nsa_cmp_kernel, is_key=is_key),
        grid=(B,),
        in_specs=in_specs,
        out_specs=out_spec,
        out_shape=out_shape,
        scratch_shapes=[pltpu.VMEM((nch, S * LANES), F32)],
        compiler_params=_params("arbitrary"),
        name="nsa_cmp_k" if is_key else "nsa_cmp_v",
    )(*args)


def _nsa_attn_kernel(qt_ref, kc_ref, vct_ref, ks_ref, vst_ref, kw_ref, vwt_ref, gt_ref, ov_ref, o_ref):
    g = pl.program_id(1)
    qb = pl.program_id(2)
    R = NSA_GROUP
    QT = Q_TILE
    KT = KEY_TILE
    NQ = R * QT
    t0 = qb * QT
    n_cmp_pad = kc_ref.shape[1]
    n_sel = ov_ref.shape[0]

    q_g = jnp.concatenate([qt_ref[0, 0, r * HEAD_DIM:(r + 1) * HEAD_DIM, :] for r in range(R)], axis=1)
    q2 = jnp.concatenate([q_g, q_g], axis=0)
    row_grp = lax.broadcasted_iota(jnp.int32, q2.shape, 0) // HEAD_DIM
    qpad = jnp.where(row_grp == g, q2, jnp.zeros_like(q2))

    tq_row = t0 + (lax.broadcasted_iota(jnp.int32, (1, NQ), 1) & (QT - 1))
    tq_tile = t0 + lax.broadcasted_iota(jnp.int32, (KT, QT), 1)
    krow = lax.broadcasted_iota(jnp.int32, (KT, QT), 0)
    tile4 = lambda z: jnp.concatenate([z] * R, axis=1)

    sc = _dot(kc_ref[0], qpad)
    n_i = lax.broadcasted_iota(jnp.int32, (n_cmp_pad, 1), 0)
    cend = jnp.where(n_i < n_cmp_pad - 1, n_i * CMP_STRIDE + (CMP_BLOCK - 1), jnp.int32(2 ** 30))
    cvalid = cend <= tq_row
    sc = jnp.where(cvalid, sc, NEG_INF)
    mc = jnp.max(sc, axis=0, keepdims=True)
    ec = jnp.where(cvalid, jnp.exp(sc - mc), 0.0)
    pc = ec / jnp.maximum(jnp.sum(ec, axis=0, keepdims=True), F32_TINY)
    o_c = _dot(vct_ref[0], pc.astype(BF16))
    pc_sum = pc[:, 0:QT]
    for r in range(1, R):
        pc_sum = pc_sum + pc[:, r * QT:(r + 1) * QT]
    imp = _dot(ov_ref[...], pc_sum, HI)

    ji = lax.broadcasted_iota(jnp.int32, (n_sel, QT), 0)
    jf = ji.astype(F32)
    tq_sel = t0 + lax.broadcasted_iota(jnp.int32, (n_sel, QT), 1)
    cur = tq_sel >> SEL_SHIFT
    forced = (ji == 0) | (ji == cur) | (ji == cur - 1)
    valid = ji * SEL_BLOCK <= tq_sel
    score = jnp.where(valid, jnp.where(forced, FORCE_SCORE, imp), -1.0)
    sel = jnp.zeros((n_sel, QT), F32)
    for _ in range(min(SEL_TOPK, n_sel)):
        mx = jnp.max(score, axis=0, keepdims=True)
        jmin = jnp.min(jnp.where(score == mx, jf, 1e9), axis=0, keepdims=True)
        hit = jf == jmin
        sel = jnp.where(hit, 1.0, sel)
        score = jnp.where(hit, -3e38, score)
    selb = sel.astype(BF16)

    def online_step(s, vt, m, l, acc):
        m_new = jnp.maximum(m, jnp.max(s, axis=0, keepdims=True))
        alpha = jnp.exp(m - m_new)
        p = jnp.exp(s - m_new)
        l = l * alpha + jnp.sum(p, axis=0, keepdims=True)
        acc = acc * alpha + _dot(vt, p.astype(BF16))
        return m_new, l, acc

    init = (jnp.full((1, NQ), NEG_INF, F32), jnp.zeros((1, NQ), F32), jnp.zeros((HEAD_DIM, NQ), F32))
    blk_lane = lax.broadcasted_iota(jnp.int32, (KT, n_sel), 1)
    blk_row = lax.broadcasted_iota(jnp.int32, (KT, n_sel), 0)

    def sel_body(kt, carry):
        k0 = pl.multiple_of(kt * KT, KT)
        s = _dot(ks_ref[0, pl.ds(k0, KT), :], qpad)
        expand = jnp.where(((k0 + blk_row) >> SEL_SHIFT) == blk_lane, 1.0, 0.0).astype(BF16)
        picked = _dot(expand, selb)
        kpos = k0 + krow
        bias = jnp.where(kpos <= tq_tile, jnp.where(picked > 0.5, 0.0, NEG_INF), NEG_INF)
        return online_step(s + tile4(bias), vst_ref[0, kt], *carry)

    _, l_s, acc_s = lax.fori_loop(0, qb + 1, sel_body, init)

    def win_body(u, carry):
        k0 = pl.multiple_of(t0 - u * KT, KT)
        s = _dot(kw_ref[0, pl.ds(k0, KT), :], qpad)
        d = tq_tile - (k0 + krow)
        bias = jnp.where(d >= 0, jnp.where(d < WINDOW, 0.0, NEG_INF), NEG_INF)
        return online_step(s + tile4(bias), vwt_ref[0, qb - u], *carry)

    _, l_w, acc_w = lax.fori_loop(0, jnp.minimum(qb, WINDOW // KT) + 1, win_body, init)

    gates = _sigmoid(gt_ref[0, 0])
    grow = lambda j: jnp.concatenate([gates[j, r:r + 1, :] for r in range(R)], axis=1)
    o = grow(0) * o_c + grow(1) * (acc_s / l_s) + grow(2) * (acc_w / l_w)
    halves = []
    for h in range(R // 2):
        pair = jnp.concatenate([o[:, (2 * h) * QT:(2 * h + 1) * QT],
                                o[:, (2 * h + 1) * QT:(2 * h + 2) * QT]], axis=0)
        halves.append(pair.T)
    o_ref[...] = jnp.concatenate(halves, axis=1)


def _nsa_attn(qt, kcmp, vct, ks3, vst, kw3, vwt, gt, ov_t, B, T):
    G, R = NSA_KV_HEADS, NSA_GROUP
    nq = T // Q_TILE
    nk = T // KEY_TILE
    nch = kcmp.shape[1]
    n_sel = ov_t.shape[0]
    return pl.pallas_call(
        _nsa_attn_kernel,
        grid=(B, G, nq),
        in_specs=[pl.BlockSpec((1, 1, R * HEAD_DIM, Q_TILE), lambda b, g, q: (b, q, g, 0)),
                  pl.BlockSpec((1, nch, LANES), lambda b, g, q: (b, 0, 0)),
                  pl.BlockSpec((1, HEAD_DIM, nch), lambda b, g, q: (b, g, 0)),
                  pl.BlockSpec((1, T, LANES), lambda b, g, q: (b, 0, 0)),
                  pl.BlockSpec((1, nk, HEAD_DIM, KEY_TILE), lambda b, g, q: (b, 0, g, 0)),
                  pl.BlockSpec((1, T, LANES), lambda b, g, q: (b, 0, 0)),
                  pl.BlockSpec((1, nk, HEAD_DIM, KEY_TILE), lambda b, g, q: (b, 0, g, 0)),
                  pl.BlockSpec((1, 1, 3, R, Q_TILE), lambda b, g, q: (b, g, 0, 0, q)),
                  pl.BlockSpec((n_sel, nch), lambda b, g, q: (0, 0))],
        out_specs=pl.BlockSpec((Q_TILE, R * HEAD_DIM), lambda b, g, q: (b * nq + q, g)),
        out_shape=jax.ShapeDtypeStruct((B * T, NSA_WIDTH), F32),
        compiler_params=_params("arbitrary", "arbitrary", "arbitrary"),
        name="nsa_attn",
    )(qt, kcmp, vct, ks3, vst, kw3, vwt, gt, ov_t)


def _first_index_of(vals, target):
    idx = jnp.full_like(target, float(len(vals) - 1))
    for i in range(len(vals) - 2, -1, -1):
        idx = jnp.where(vals[i] == target, float(i), idx)
    return idx


def _pick(vals, idx):
    out = vals[-1]
    for i in range(len(vals) - 2, -1, -1):
        out = jnp.where(idx == float(i), vals[i], out)
    return out


def _route_rows(score, bias):
    E, G, P = N_EXPERTS, N_GROUPS, EXPERTS_PER_GROUP
    sel = score + bias
    s = [sel[e:e + 1, :] for e in range(E)]
    raw = [score[e:e + 1, :] for e in range(E)]
    grp = []
    for gi in range(G):
        a = s[gi * P:(gi + 1) * P]
        best = None
        for i in range(P):
            for j in range(i + 1, P):
                pair = a[i] + a[j]
                best = pair if best is None else jnp.maximum(best, pair)
        grp.append(best)
    gmax = functools.reduce(jnp.maximum, grp)
    g_star = _first_index_of(grp, gmax)
    v = [_pick([s[gi * P + i] for gi in range(G)], g_star) for i in range(P)]
    w = [_pick([raw[gi * P + i] for gi in range(G)], g_star) for i in range(P)]
    i1 = _first_index_of(v, functools.reduce(jnp.maximum, v))
    v2 = [jnp.where(i1 == float(i), -jnp.inf, v[i]) for i in range(P)]
    i2 = _first_index_of(v2, functools.reduce(jnp.maximum, v2))
    w1, w2 = _pick(w, i1), _pick(w, i2)
    tot = w1 + w2
    zero = jnp.zeros_like(tot)
    return jnp.concatenate([g_star * P + i1, g_star * P + i2, w1 / tot, w2 / tot, zero, zero, zero, zero],
                           axis=0)


def _merge_kernel(ys_ref, g_ref, bonus_ref, gng_ref, gnb_ref, bd_ref, yb_ref, pm_ref, x_ref, mod_ref,
                  ng_ref, wa_ref, wb_ref, wo_ref, rw_ref, rb_ref, o_x, o_h, o_route):
    m = mod_ref[pl.program_id(0)]
    bd = bd_ref[...]
    y = ys_ref[...]
    mean = _dot(y, bd, HI)
    yc = y - mean
    var = _dot(yc * yc, bd, HI)
    ya = (yc * lax.rsqrt(var + RWKV_GN_EPS) * gng_ref[...] + gnb_ref[...] + bonus_ref[...]) * g_ref[...]
    pm = pm_ref[...]
    D = x_ref.shape[1]
    mix = (_sigmoid(pm[:, 0:D]) * _dot(ya.astype(BF16), wa_ref[...])
           + _sigmoid(pm[:, D:2 * D]) * _dot(yb_ref[...].astype(BF16), wb_ref[...]))
    x = x_ref[...] + m[2:3] * _dot(mix.astype(BF16), wo_ref[...])
    o_x[...] = x
    ms = jnp.mean(x * x, axis=-1, keepdims=True)
    h = x * lax.rsqrt(ms + NORM_EPS) * ng_ref[...]
    h = h * (1.0 + m[4:5]) + m[3:4]
    o_h[...] = h
    score = _sigmoid(_dot_tb(rw_ref[...], h, HI))
    o_route[...] = _route_rows(score, rb_ref[...])


def _merge(ys, g, bonus, gn_g, gn_b, yb, pm, x2, mod, ng, wa, wb, wo, router_w, router_b, B, T, tm=256):
    N, D = x2.shape
    W = RWKV_WIDTH
    nt = T // tm
    row = lambda b, t: (b * nt + t, 0)
    full = lambda shape: pl.BlockSpec(shape, lambda b, t: (0,) * len(shape))
    return pl.pallas_call(
        _merge_kernel,
        grid=(B, nt),
        in_specs=[pl.BlockSpec((tm, W), row), pl.BlockSpec((tm, W), row), pl.BlockSpec((tm, W), row),
                  full((1, W)), full((1, W)), full((W, W)),
                  pl.BlockSpec((tm, NSA_WIDTH), row), pl.BlockSpec((tm, 2 * D), row),
                  pl.BlockSpec((tm, D), row), full((B, 6, D)), full((1, D)),
                  full((W, D)), full((NSA_WIDTH, D)), full((D, D)), full((N_EXPERTS, D)),
                  full((N_EXPERTS, 1))],
        out_specs=[pl.BlockSpec((tm, D), row), pl.BlockSpec((tm, D), row),
                   pl.BlockSpec((8, tm), lambda b, t: (0, b * nt + t))],
        out_shape=[jax.ShapeDtypeStruct((N, D), F32), jax.ShapeDtypeStruct((N, D), F32),
                   jax.ShapeDtypeStruct((8, N), F32)],
        compiler_params=_params("arbitrary", "arbitrary"),
        name="merge_out",
    )(ys, g, bonus, gn_g.reshape(1, W), gn_b.reshape(1, W), _head_block_diag(W, 1.0 / HEAD_DIM),
      yb, pm, x2, mod, ng, wa, wb, wo, router_w.T, router_b.reshape(N_EXPERTS, 1))


def _route(route, N):
    wts = route[TOP_K:2 * TOP_K].T
    NK = N * TOP_K
    e_flat = route[0:TOP_K].astype(jnp.int32).reshape(-1)
    onehot = (e_flat[:, None] == jnp.arange(N_EXPERTS, dtype=jnp.int32)[None, :]).astype(jnp.int32)
    csum = jnp.cumsum(onehot, axis=0)
    counts = csum[-1]
    rank = jnp.take_along_axis(csum, e_flat[:, None], axis=1)[:, 0] - 1
    padded = (counts + MOE_BLOCK - 1) // MOE_BLOCK * MOE_BLOCK
    pad_end = jnp.cumsum(padded)
    pad_start = pad_end - padded
    dest = pad_start[e_flat] + rank
    n_blk = -(-NK // MOE_BLOCK) + N_EXPERTS
    P = n_blk * MOE_BLOCK
    slot_assign = jnp.full((P,), -1, jnp.int32).at[dest].set(jnp.arange(NK, dtype=jnp.int32))
    blk_start = jnp.arange(n_blk, dtype=jnp.int32) * MOE_BLOCK
    blk_expert = jnp.clip(jnp.searchsorted(pad_end, blk_start, side='right'), 0, N_EXPERTS - 1)
    blk_used = (blk_start < pad_end[-1]).astype(jnp.int32)
    return wts, slot_assign, blk_expert.astype(jnp.int32), blk_used, n_blk


def _moe_kernel(be_ref, used_ref, slot_ref, h_hbm, wg_ref, wu_ref, wd_ref, o_hbm, xbuf, ybuf, sem_in,
                sem_out, *, n_tok):
    i = pl.program_id(0)
    MB = MOE_BLOCK
    base = i * MB

    @pl.when(used_ref[i] > 0)
    def _():
        def gather(r, _):
            a = slot_ref[base + r]
            tok = jnp.where(a < 0, 0, jnp.where(a >= n_tok, a - n_tok, a))
            pltpu.make_async_copy(h_hbm.at[pl.ds(tok, 1), :], xbuf.at[pl.ds(r, 1), :], sem_in).start()
            return 0

        lax.fori_loop(0, MB, gather, 0)

        def gather_wait(r, _):
            pltpu.make_async_copy(h_hbm.at[pl.ds(0, 1), :], xbuf.at[pl.ds(r, 1), :], sem_in).wait()
            return 0

        lax.fori_loop(0, MB, gather_wait, 0)
        x = xbuf[...].astype(BF16)
        gate = _dot(x, wg_ref[0])
        up = _dot(x, wu_ref[0])
        act = (gate * _sigmoid(gate) * up).astype(BF16)
        ybuf[...] = _dot(act, wd_ref[0])

        def scatter(r, _):
            a = slot_ref[base + r]

            @pl.when(a >= 0)
            def _():
                pltpu.make_async_copy(ybuf.at[pl.ds(r, 1), :], o_hbm.at[pl.ds(a, 1), :], sem_out).start()
            return 0

        lax.fori_loop(0, MB, scatter, 0)

        def scatter_wait(r, _):
            @pl.when(slot_ref[base + r] >= 0)
            def _():
                pltpu.make_async_copy(ybuf.at[pl.ds(r, 1), :], o_hbm.at[pl.ds(0, 1), :], sem_out).wait()
            return 0

        lax.fori_loop(0, MB, scatter_wait, 0)


def _moe(h_bf, slot_assign, blk_expert, blk_used, n_blk, wg, wu, wd):
    N, D = h_bf.shape
    DE = wg.shape[2]
    grid_spec = pltpu.PrefetchScalarGridSpec(
        num_scalar_prefetch=3,
        grid=(n_blk,),
        in_specs=[pl.BlockSpec(memory_space=pl.ANY),
                  pl.BlockSpec((1, D, DE), lambda i, be, us, sl: (be[i], 0, 0)),
                  pl.BlockSpec((1, D, DE), lambda i, be, us, sl: (be[i], 0, 0)),
                  pl.BlockSpec((1, DE, D), lambda i, be, us, sl: (be[i], 0, 0))],
        out_specs=pl.BlockSpec(memory_space=pl.ANY),
        scratch_shapes=[pltpu.VMEM((MOE_BLOCK, D), F32), pltpu.VMEM((MOE_BLOCK, D), F32),
                        pltpu.SemaphoreType.DMA(()), pltpu.SemaphoreType.DMA(())],
    )
    return pl.pallas_call(
        functools.partial(_moe_kernel, n_tok=N),
        grid_spec=grid_spec,
        out_shape=jax.ShapeDtypeStruct((TOP_K * N, D), F32),
        compiler_params=_params("arbitrary"),
        name="moe_experts",
    )(blk_expert, blk_used, slot_assign, h_bf, wg, wu, wd)


def _final_kernel(x_ref, y0_ref, y1_ref, w_ref, mod_ref, o_ref):
    m = mod_ref[pl.program_id(0)]
    w = w_ref[...]
    o_ref[...] = x_ref[...] + m[5:6] * (w[:, 0:1] * y0_ref[...] + w[:, 1:2] * y1_ref[...])


def _final(x2, ybuf, wts, mod, B, T, tm=512):
    N, D = x2.shape
    nt = T // tm
    row = lambda b, t: (b * nt + t, 0)
    return pl.pallas_call(
        _final_kernel,
        grid=(B, nt),
        in_specs=[pl.BlockSpec((tm, D), row), pl.BlockSpec((tm, D), row),
                  pl.BlockSpec((tm, D), lambda b, t: (N // tm + b * nt + t, 0)),
                  pl.BlockSpec((tm, TOP_K), row), pl.BlockSpec((B, 6, D), lambda b, t: (0, 0, 0))],
        out_specs=pl.BlockSpec((tm, D), row),
        out_shape=jax.ShapeDtypeStruct((N, D), F32),
        compiler_params=_params("arbitrary", "arbitrary"),
        name="moe_combine",
    )(x2, ybuf, ybuf, wts, mod)


def _overlap_t(n_sel, n_cmp_pad):
    ci = jnp.arange(n_cmp_pad)[None, :] * CMP_STRIDE
    sj = jnp.arange(n_sel)[:, None] * SEL_BLOCK
    ov = (ci <= sj + SEL_BLOCK - 1) & (ci + CMP_BLOCK - 1 >= sj) & (jnp.arange(n_cmp_pad)[None, :] < n_cmp_pad - 1)
    return ov.astype(F32)


def kernel(x, c, w_ada, b_ada, norm_g, w_in, b_in, rwkv_mu, rwkv_w0, rwkv_w2, rwkv_a0, rwkv_a2, rwkv_g2,
           rwkv_k_k, rwkv_k_a, rwkv_r_k, rwkv_gn_g, rwkv_gn_b, qk_norm_g, cmp_pos, cmp_w1, cmp_w2,
           w_up_rwkv, w_up_nsa, w_out, router_w, router_b, exp_w_gate, exp_w_up, exp_w_down):
    B, T, D = x.shape
    L = w_ada.shape[0]
    N = B * T
    mods = _ada(c, w_ada, b_ada)
    tables = _rope_tables(jnp.arange(T, dtype=jnp.int32))
    nch = T // CMP_STRIDE
    tables_cmp = _rope_tables(jnp.arange(nch, dtype=jnp.int32) * CMP_STRIDE + CMP_BLOCK - 1)
    ov_t = _overlap_t(T // SEL_BLOCK, nch)
    n_gate = NSA_GATE_COLS
    x2 = x.reshape(N, D)
    for l in range(L):
        g0 = _SEG_KV[1] + n_gate
        w_pad = jnp.concatenate([w_in[l][:, :g0], jnp.zeros((D, GATE_PAD - n_gate), F32), w_in[l][:, g0:]],
                                axis=1).astype(BF16)
        b_pad = jnp.concatenate([b_in[l][:g0], jnp.zeros((GATE_PAD - n_gate,), F32), b_in[l][g0:]]).reshape(1, -1)
        p_rw, p_q, p_kv, p_gate, p_merge = _inproj(x2, mods[l], norm_g[l, 0].reshape(1, D), w_pad, b_pad, B, T)
        r, k, v, al, bb, ld, g, bonus = _rwkv_pre(p_rw, rwkv_mu[l], rwkv_w0[l], rwkv_w2[l], rwkv_a0[l],
                                                  rwkv_a2[l], rwkv_g2[l], rwkv_k_k[l], rwkv_k_a[l],
                                                  rwkv_r_k[l], B, T)
        ys = _rwkv_scan(r, k, v, al, bb, ld, B, T)
        qt, ks, kw, vst, vwt = _nsa_prep(p_q, p_kv, tables, qk_norm_g[l], B, T)
        kv3 = p_kv.reshape(B, T, KV_COLS)
        kcmp = _nsa_cmp(kv3, 0, cmp_pos[l], cmp_w1[l], cmp_w2[l], qk_norm_g[l, 1], tables_cmp)
        vct = _nsa_cmp(kv3, 1, cmp_pos[l], cmp_w1[l], cmp_w2[l], None, None)
        gt = p_gate[:, :n_gate].reshape(B, T, NSA_KV_HEADS, NSA_GROUP, 3).transpose(0, 2, 4, 3, 1)
        yb = _nsa_attn(qt, kcmp, vct, ks.reshape(B, T, LANES), vst, kw.reshape(B, T, LANES), vwt, gt, ov_t,
                       B, T)
        x2, h2, route = _merge(ys, g, bonus, rwkv_gn_g[l], rwkv_gn_b[l], yb, p_merge, x2, mods[l],
                               norm_g[l, 1].reshape(1, D), w_up_rwkv[l].astype(BF16),
                               w_up_nsa[l].astype(BF16), w_out[l].astype(BF16), router_w, router_b, B, T)
        wts, slot_assign, blk_expert, blk_used, n_blk = _route(route, N)
        ybuf = _moe(h2, slot_assign, blk_expert, blk_used, n_blk, exp_w_gate[l].astype(BF16),
                    exp_w_up[l].astype(BF16), exp_w_down[l].astype(BF16))
        x2 = _final(x2, ybuf, wts, mods[l], B, T)
    return x2.reshape(B, T, D)
```

```python
import functools
import math

import jax
import jax.numpy as jnp
from jax import lax
from jax.experimental import pallas as pl
from jax.experimental.pallas import tpu as pltpu

F32 = jnp.float32
BF16 = jnp.bfloat16
HI = lax.Precision.HIGHEST

D_MODEL = 1024
RWKV_HEADS = 8
HEAD_DIM = 64
RWKV_WIDTH = RWKV_HEADS * HEAD_DIM
DECAY_LORA = 64
ICLR_LORA = 64
GATE_LORA = 128
RWKV_GN_EPS = 64e-5
RWKV_COLS = 3 * RWKV_WIDTH + DECAY_LORA + ICLR_LORA + GATE_LORA

NSA_Q_HEADS = 8
NSA_KV_HEADS = 2
NSA_GROUP = NSA_Q_HEADS // NSA_KV_HEADS
NSA_WIDTH = NSA_Q_HEADS * HEAD_DIM
CMP_STRIDE = 16
CMP_BLOCK = 2 * CMP_STRIDE
CMP_HIDDEN = 256
SEL_BLOCK = 64
SEL_SHIFT = 6
SEL_TOPK = 16
WINDOW = 512
FORCE_SCORE = 1e4
NEG_INF = -1e30
ROPE_THETA = 500000.0
ROPE_DIM = HEAD_DIM // 4
KV_COLS = 6 * NSA_KV_HEADS * HEAD_DIM
NSA_GATE_COLS = 3 * NSA_Q_HEADS
GATE_PAD = 128

N_EXPERTS = 16
N_GROUPS = 4
EXPERTS_PER_GROUP = N_EXPERTS // N_GROUPS
TOP_K = 2
D_EXPERT = 512
MOE_BLOCK = 256
NORM_EPS = 1e-6

LANES = 128
CHUNK = 64
KEY_TILE = 128
SEL_TILE = 512
Q_TILE = 128
F32_TINY = float(jnp.finfo(jnp.float32).tiny)

_SEG_RW = (0, RWKV_COLS)
_SEG_Q = (_SEG_RW[1], _SEG_RW[1] + NSA_WIDTH)
_SEG_KV = (_SEG_Q[1], _SEG_Q[1] + KV_COLS)
_SEG_GATE = (_SEG_KV[1], _SEG_KV[1] + GATE_PAD)
_SEG_MERGE = (_SEG_GATE[1], _SEG_GATE[1] + 2 * D_MODEL)
IN_COLS_PAD = _SEG_MERGE[1]

_VMEM_LIMIT = 56 * 1024 * 1024


def _dot(a, b, precision=None):
    return jnp.dot(a, b, preferred_element_type=F32, precision=precision)


def _dot_tb(a, b, precision=None):
    return lax.dot_general(a, b, (((1,), (1,)), ((), ())), preferred_element_type=F32,
                           precision=precision)


def _dot_ta(a, b, precision=None):
    return lax.dot_general(a, b, (((0,), (0,)), ((), ())), preferred_element_type=F32,
                           precision=precision)


def _params(*sem):
    return pltpu.CompilerParams(dimension_semantics=sem, vmem_limit_bytes=_VMEM_LIMIT)


def _sigmoid(x):
    return 1.0 / (1.0 + jnp.exp(-x))


def _ada_kernel(c_ref, w_ref, b_ref, o_ref):
    c = c_ref[...]
    s = c * _sigmoid(c)
    o_ref[0] = _dot(s, w_ref[0], HI) + b_ref[0]


def _ada(c, w_ada, b_ada):
    L, D, D6 = w_ada.shape
    B = c.shape[0]
    rows = 8
    cp = jnp.zeros((rows, D), F32).at[:B].set(c)
    tn = 1536
    out = pl.pallas_call(
        _ada_kernel,
        grid=(L, D6 // tn),
        in_specs=[pl.BlockSpec((rows, D), lambda l, j: (0, 0)),
                  pl.BlockSpec((1, D, tn), lambda l, j: (l, 0, j)),
                  pl.BlockSpec((1, 1, tn), lambda l, j: (l, 0, j))],
        out_specs=pl.BlockSpec((1, rows, tn), lambda l, j: (l, 0, j)),
        out_shape=jax.ShapeDtypeStruct((L, rows, D6), F32),
        compiler_params=_params("arbitrary", "arbitrary"),
        name="ada_mod",
    )(cp, w_ada, b_ada.reshape(L, 1, D6))
    return out[:, :B].reshape(L, B, 6, D)


def _inproj_kernel(x_ref, mod_ref, g_ref, w_ref, b_ref, o_rw, o_q, o_kv, o_gate, o_merge):
    m = mod_ref[pl.program_id(0)]
    x = x_ref[...]
    ms = jnp.mean(x * x, axis=-1, keepdims=True)
    h = x * lax.rsqrt(ms + NORM_EPS) * g_ref[...]
    h = h * (1.0 + m[1:2]) + m[0:1]
    hb = h.astype(BF16)
    for o, (a, e) in ((o_rw, _SEG_RW), (o_q, _SEG_Q), (o_kv, _SEG_KV), (o_gate, _SEG_GATE),
                      (o_merge, _SEG_MERGE)):
        o[...] = _dot(hb, w_ref[:, a:e]) + b_ref[:, a:e]


def _inproj(x2, mod, g, w_pad, b_pad, B, T, tm=256):
    N, D = x2.shape
    nt = T // tm
    row = lambda b, t: (b * nt + t, 0)
    widths = [e - a for a, e in (_SEG_RW, _SEG_Q, _SEG_KV, _SEG_GATE, _SEG_MERGE)]
    return pl.pallas_call(
        _inproj_kernel,
        grid=(B, nt),
        in_specs=[pl.BlockSpec((tm, D), row),
                  pl.BlockSpec((B, 6, D), lambda b, t: (0, 0, 0)),
                  pl.BlockSpec((1, D), lambda b, t: (0, 0)),
                  pl.BlockSpec((D, IN_COLS_PAD), lambda b, t: (0, 0)),
                  pl.BlockSpec((1, IN_COLS_PAD), lambda b, t: (0, 0))],
        out_specs=[pl.BlockSpec((tm, w), row) for w in widths],
        out_shape=[jax.ShapeDtypeStruct((N, w), F32) for w in widths],
        compiler_params=_params("arbitrary", "arbitrary"),
        name="in_proj",
    )(x2, mod, g, w_pad, b_pad)


def _rwkv_pre_kernel(p_ref, mu_ref, w0_ref, w2_ref, a0_ref, a2_ref, g2_ref, kk_ref, ka_ref, rk_ref,
                     bd_ref, o_r, o_k, o_v, o_al, o_b, o_ld, o_g, o_bonus, carry_ref):
    W = RWKV_WIDTH

    @pl.when(pl.program_id(1) == 0)
    def _():
        carry_ref[...] = jnp.zeros_like(carry_ref)

    p = p_ref[...]
    ts = p.shape[0]
    rows = lax.broadcasted_iota(jnp.int32, p.shape, 0)
    shifted = jnp.where(rows == 0, carry_ref[0:1, :], pltpu.roll(p, 1, 0))
    carry_ref[0:1, :] = p[ts - 1:ts, :]
    pm = p + (shifted - p) * mu_ref[...]
    r = pm[:, 0:W]
    k = pm[:, W:2 * W]
    v = pm[:, 2 * W:3 * W]
    wa = pm[:, 3 * W:3 * W + DECAY_LORA + ICLR_LORA]
    gl = pm[:, 3 * W + DECAY_LORA + ICLR_LORA:]
    xw = w0_ref[...] + _dot(jnp.tanh(wa), w2_ref[...], HI)
    ld = -math.exp(-0.5) * _sigmoid(xw)
    a = _sigmoid(a0_ref[...] + _dot(wa, a2_ref[...], HI))
    g = _dot(_sigmoid(gl), g2_ref[...], HI)
    bd = bd_ref[...]
    kk = k * kk_ref[...]
    nrm = jnp.sqrt(_dot(kk * kk, bd, HI))
    kk = kk / jnp.maximum(nrm, 1e-12)
    k2 = k * (1.0 + (a - 1.0) * ka_ref[...])
    bonus = _dot(r * k2 * rk_ref[...], bd, HI) * v
    o_r[...] = r
    o_k[...] = k2
    o_v[...] = v
    o_al[...] = kk
    o_b[...] = -kk * a
    o_ld[...] = ld
    o_g[...] = g
    o_bonus[...] = bonus


def _head_block_diag(width, scale=1.0):
    i = jnp.arange(width) // HEAD_DIM
    return (i[:, None] == i[None, :]).astype(F32) * scale


def _rwkv_pre(p_rw, mu, w0, w2, a0, a2, g2, k_k, k_a, r_k, B, T, ts=256):
    N = p_rw.shape[0]
    W = RWKV_WIDTH
    nt = T // ts
    row = lambda b, t: (b * nt + t, 0)
    zl = jnp.zeros((DECAY_LORA, W), F32)
    w2p = jnp.concatenate([w2, zl], axis=0)
    a2p = jnp.concatenate([zl, a2], axis=0)
    full = lambda shape: pl.BlockSpec(shape, lambda b, t: (0,) * len(shape))
    vec = lambda z: z.reshape(1, -1)
    return pl.pallas_call(
        _rwkv_pre_kernel,
        grid=(B, nt),
        in_specs=[pl.BlockSpec((ts, RWKV_COLS), row), full((1, RWKV_COLS)), full((1, W)),
                  full((2 * DECAY_LORA, W)), full((1, W)), full((2 * DECAY_LORA, W)),
                  full((GATE_LORA, W)), full((1, W)), full((1, W)), full((1, W)), full((W, W))],
        out_specs=[pl.BlockSpec((ts, W), row)] * 8,
        out_shape=[jax.ShapeDtypeStruct((N, W), F32)] * 8,
        scratch_shapes=[pltpu.VMEM((8, RWKV_COLS), F32)],
        compiler_params=_params("arbitrary", "arbitrary"),
        name="rwkv_pre",
    )(p_rw, vec(mu), vec(w0), w2p, vec(a0), a2p, g2, vec(k_k), vec(k_a), vec(r_k),
      _head_block_diag(W))


def _bf(x):
    return x.astype(BF16)


def _scan_local(chunks, eye, strict, incl, m0, m1):
    C = CHUNK
    n = range(len(chunks))
    st = lambda z: jnp.concatenate([z * m0, z * m1], axis=0)
    zero = jnp.zeros((2 * C, 2 * C), F32)
    at_b, rt_s, vs, vs_b, lhs_a, rhs_a, bk_t, dcol = [], [], [], [], [], [], [], []
    for r, k, v, al, bb, ld, cum in chunks:
        tot = cum[C - 1:C, :]
        dinv = jnp.exp(-cum)
        dend = jnp.exp(tot - cum)
        at_b.append(_bf(st(al * jnp.exp(cum - ld))))
        rt_s.append(st(r * jnp.exp(cum)))
        vs.append(st(v))
        vs_b.append(_bf(vs[-1]))
        lhs_a.append(jnp.concatenate([at_b[-1], _bf(rt_s[-1])], axis=0))
        rhs_a.append(_bf(jnp.concatenate([st(bb * dinv), st(k * dinv)], axis=0)))
        bk_t.append(_bf(jnp.concatenate([st(bb * dend).T, st(k * dend).T], axis=1)))
        dcol.append(jnp.sum(eye * jnp.exp(tot), axis=1, keepdims=True))
    A = [_dot_tb(lhs_a[i], rhs_a[i]) for i in n]
    a_ab = [jnp.where(strict, A[i][0:2 * C, 0:2 * C], zero) for i in n]
    a_ak = [_bf(jnp.where(strict, A[i][0:2 * C, 2 * C:4 * C], zero)) for i in n]
    a_r = [_bf(jnp.concatenate([jnp.where(incl, A[i][2 * C:4 * C, 0:2 * C], zero),
                                jnp.where(incl, A[i][2 * C:4 * C, 2 * C:4 * C], zero)], axis=1)) for i in n]
    akv = [_bf(_dot(a_ak[i], vs_b[i])) for i in n]
    pw = a_ab
    tinv = [eye + pw[i] for i in n]
    for _ in range(5):
        pw_b = [_bf(pw[i]) for i in n]
        pw = [_dot(pw_b[i], pw_b[i]) for i in n]
        tinv = [tinv[i] + _dot(_bf(pw[i]), _bf(tinv[i])) for i in n]
    X = [_dot(_bf(tinv[i]), jnp.concatenate([at_b[i], akv[i]], axis=1)) for i in n]
    w_b = [_bf(X[i][:, 0:LANES]) for i in n]
    uv0 = [jnp.concatenate([_bf(X[i][:, LANES:2 * LANES]), vs_b[i]], axis=0) for i in n]
    m_h = [_bf(_dot(bk_t[i][:, 0:2 * C], w_b[i])) for i in n]
    g_h = [_dot(bk_t[i], uv0[i]) for i in n]
    q_h = [_bf(rt_s[i] + _dot(a_r[i][:, 0:2 * C], w_b[i])) for i in n]
    y0 = [_dot(a_r[i], uv0[i]) for i in n]
    return [(m_h[i], g_h[i], dcol[i], q_h[i], y0[i]) for i in n]


def _scan_steps(local, H):
    C = CHUNK
    ys = []
    for m_h, g_h, dcol, q_h, y0 in local:
        h_b = _bf(H)
        Y = _dot(q_h, h_b) + y0
        ys.append(Y[0:C] + Y[C:2 * C])
        H = dcol * H + _dot(m_h, h_b) + g_h
    return ys, H


def _rwkv_scan_kernel(r_ref, k_ref, v_ref, al_ref, b_ref, ld_ref, o_ref, h_ref):
    C = CHUNK
    tc = r_ref.shape[0]

    @pl.when(pl.program_id(2) == 0)
    def _():
        h_ref[...] = jnp.zeros_like(h_ref)

    tri = (lax.broadcasted_iota(jnp.int32, (C, C), 1) <= lax.broadcasted_iota(jnp.int32, (C, C), 0)).astype(F32)
    r2 = lax.broadcasted_iota(jnp.int32, (2 * C, 2 * C), 0)
    c2 = lax.broadcasted_iota(jnp.int32, (2 * C, 2 * C), 1)
    eye = (r2 == c2).astype(F32)
    strict = (c2 & (C - 1)) < (r2 & (C - 1))
    incl = (c2 & (C - 1)) <= (r2 & (C - 1))
    lane = lax.broadcasted_iota(jnp.int32, (C, LANES), 1)
    m0 = (lane < HEAD_DIM).astype(F32)
    m1 = 1.0 - m0
    nc = tc // C
    cum = _dot(tri, jnp.concatenate([ld_ref[c * C:(c + 1) * C, :] for c in range(nc)], axis=1), HI)
    chunks = []
    for c in range(nc):
        sl = slice(c * C, (c + 1) * C)
        chunks.append((r_ref[sl, :], k_ref[sl, :], v_ref[sl, :], al_ref[sl, :], b_ref[sl, :], ld_ref[sl, :],
                       cum[:, c * LANES:(c + 1) * LANES]))
    ys, H = _scan_steps(_scan_local(chunks, eye, strict, incl, m0, m1), h_ref[...])
    for c in range(nc):
        o_ref[c * C:(c + 1) * C, :] = ys[c]
    h_ref[...] = H


def _rwkv_scan(r, k, v, al, bb, ld, B, T, tc=512):
    N, W = r.shape
    nt = T // tc
    spec = pl.BlockSpec((tc, LANES), lambda b, h, t: (b * nt + t, h))
    return pl.pallas_call(
        _rwkv_scan_kernel,
        grid=(B, W // LANES, nt),
        in_specs=[spec] * 6,
        out_specs=spec,
        out_shape=jax.ShapeDtypeStruct((N, W), F32),
        scratch_shapes=[pltpu.VMEM((LANES, LANES), F32)],
        compiler_params=_params("arbitrary", "arbitrary", "arbitrary"),
        name="rwkv_scan",
    )(r, k, v, al, bb, ld)


def _rope_tables(pos):
    half = ROPE_DIM // 2
    inv = jnp.power(ROPE_THETA, -jnp.arange(half, dtype=F32) * 2.0 / ROPE_DIM)
    ang = pos.astype(F32)[:, None] * inv[None, :]
    cos, sin = jnp.cos(ang), jnp.sin(ang)
    n = pos.shape[0]
    rest = HEAD_DIM - ROPE_DIM
    c = jnp.concatenate([cos, cos, jnp.ones((n, rest), F32)], axis=1)
    s_dn = jnp.concatenate([-sin, jnp.zeros((n, half + rest), F32)], axis=1)
    s_up = jnp.concatenate([jnp.zeros((n, half), F32), sin, jnp.zeros((n, rest), F32)], axis=1)
    rep = LANES // HEAD_DIM
    return jnp.tile(c, (1, rep)), jnp.tile(s_dn, (1, rep)), jnp.tile(s_up, (1, rep))


def _norm_rope(x, bd, g, c, s_dn, s_up):
    width = x.shape[1]
    half = ROPE_DIM // 2
    rep = width // LANES
    tile = (lambda z: jnp.concatenate([z] * rep, axis=1)) if rep > 1 else (lambda z: z)
    ms = _dot(x * x, bd, HI)
    xn = x * lax.rsqrt(ms + NORM_EPS) * g
    return (xn * tile(c) + pltpu.roll(xn, width - half, 1) * tile(s_dn)
            + pltpu.roll(xn, half, 1) * tile(s_up))


def _nsa_prep_kernel(q_ref, kv_ref, c_ref, sd_ref, su_ref, gq_ref, gs_ref, gw_ref, bdq_ref, bdk_ref,
                     o_qt, o_ks, o_kw, o_vst, o_vwt):
    c, sd, su = c_ref[...], sd_ref[...], su_ref[...]
    q = _norm_rope(q_ref[...], bdq_ref[...], gq_ref[...], c, sd, su) * (HEAD_DIM ** -0.5)
    qt = q.T
    kv = kv_ref[...]
    bdk = bdk_ref[...]
    o_ks[...] = _norm_rope(kv[:, 2 * LANES:3 * LANES], bdk, gs_ref[...], c, sd, su).astype(BF16)
    o_kw[...] = _norm_rope(kv[:, 4 * LANES:5 * LANES], bdk, gw_ref[...], c, sd, su).astype(BF16)
    vst = kv[:, 3 * LANES:4 * LANES].T
    vwt = kv[:, 5 * LANES:6 * LANES].T
    for j in range(q.shape[0] // KEY_TILE):
        sl = slice(j * KEY_TILE, (j + 1) * KEY_TILE)
        o_qt[0, j] = qt[:, sl].astype(BF16)
        o_vwt[0, j] = vwt[:, sl].astype(BF16)
    for j in range(q.shape[0] // SEL_TILE):
        o_vst[0, j] = vst[:, j * SEL_TILE:(j + 1) * SEL_TILE].astype(BF16)


def _nsa_prep(q, kv, tables, qk_g, B, T, ts=512):
    N = q.shape[0]
    nt = T // ts
    nk = ts // KEY_TILE
    ns = ts // SEL_TILE
    row = lambda b, t: (b * nt + t, 0)
    full = lambda shape: pl.BlockSpec(shape, lambda b, t: (0,) * len(shape))
    tab = pl.BlockSpec((ts, LANES), lambda b, t: (t, 0))
    gq = jnp.tile(qk_g[0], NSA_Q_HEADS).reshape(1, NSA_WIDTH)
    gs = jnp.tile(qk_g[2], NSA_KV_HEADS).reshape(1, LANES)
    gw = jnp.tile(qk_g[3], NSA_KV_HEADS).reshape(1, LANES)
    tiled = lambda rows: pl.BlockSpec((1, nk, rows, KEY_TILE), lambda b, t: (b, t, 0, 0))
    return pl.pallas_call(
        _nsa_prep_kernel,
        grid=(B, nt),
        in_specs=[pl.BlockSpec((ts, NSA_WIDTH), row), pl.BlockSpec((ts, KV_COLS), row), tab, tab, tab,
                  full((1, NSA_WIDTH)), full((1, LANES)), full((1, LANES)),
                  full((NSA_WIDTH, NSA_WIDTH)), full((LANES, LANES))],
        out_specs=[tiled(NSA_WIDTH), pl.BlockSpec((ts, LANES), row), pl.BlockSpec((ts, LANES), row),
                   pl.BlockSpec((1, ns, LANES, SEL_TILE), lambda b, t: (b, t, 0, 0)), tiled(LANES)],
        out_shape=[jax.ShapeDtypeStruct((B, T // KEY_TILE, NSA_WIDTH, KEY_TILE), BF16),
                   jax.ShapeDtypeStruct((N, LANES), BF16), jax.ShapeDtypeStruct((N, LANES), BF16),
                   jax.ShapeDtypeStruct((B, T // SEL_TILE, LANES, SEL_TILE), BF16),
                   jax.ShapeDtypeStruct((B, T // KEY_TILE, LANES, KEY_TILE), BF16)],
        compiler_params=_params("arbitrary", "arbitrary"),
        name="nsa_prep",
    )(q, kv, *tables, gq, gs, gw, _head_block_diag(NSA_WIDTH, 1.0 / HEAD_DIM),
      _head_block_diag(LANES, 1.0 / HEAD_DIM))


def _gelu_tanh(x):
    return 0.5 * x * (1.0 + jnp.tanh(0.7978845608028654 * (x + 0.044715 * x * x * x)))


def _nsa_cmp_kernel(x_ref, pos_ref, w1_ref, w2_ref, *rest, is_key):
    if is_key:
        g_ref, c_ref, sd_ref, su_ref, bd_ref, o_ref, xs_ref = rest
    else:
        o_ref, xs_ref = rest
    nch = xs_ref.shape[0]
    S = CMP_STRIDE
    for j in range(S):
        xs_ref[:, j * LANES:(j + 1) * LANES] = x_ref[0, pl.ds(j, nch, stride=S), :]
    xs = xs_ref[...]
    first = _dot((xs + pos_ref[0:1, :]).astype(BF16), w1_ref[0])
    second = _dot((xs + pos_ref[1:2, :]).astype(BF16), w1_ref[1])
    hid = first + pltpu.roll(second, nch - 1, 0)
    out = _dot(_gelu_tanh(hid).astype(BF16), w2_ref[...])
    rows = lax.broadcasted_iota(jnp.int32, out.shape, 0)
    if is_key:
        out = _norm_rope(out, bd_ref[...], g_ref[...], c_ref[...], sd_ref[...], su_ref[...])
        o_ref[0] = jnp.where(rows < nch - 1, out, 0.0).astype(BF16)
    else:
        o_ref[0] = jnp.where(rows < nch - 1, out, 0.0).T.astype(BF16)


def _nsa_cmp(kv3, which, cmp_pos, cmp_w1, cmp_w2, g_k, tables_cmp):
    B, T, _ = kv3.shape
    S = CMP_STRIDE
    nch = T // S
    is_key = which == 0
    eye2 = jnp.eye(NSA_KV_HEADS, dtype=F32)
    w1 = cmp_w1[which].reshape(CMP_BLOCK, HEAD_DIM, CMP_HIDDEN)
    w1 = jnp.einsum('jdh,ge->jgdeh', w1, eye2).reshape(2, S * LANES, NSA_KV_HEADS * CMP_HIDDEN)
    w2 = jnp.einsum('hd,ge->ghed', cmp_w2[which], eye2).reshape(NSA_KV_HEADS * CMP_HIDDEN, LANES)
    pos = jnp.tile(cmp_pos[which].reshape(2, S, 1, HEAD_DIM), (1, 1, NSA_KV_HEADS, 1)).reshape(2, S * LANES)
    full = lambda shape: pl.BlockSpec(shape, lambda b: (0,) * len(shape))
    in_specs = [pl.BlockSpec((1, T, LANES), lambda b: (b, 0, which)), full(pos.shape), full(w1.shape),
                full(w2.shape)]
    args = [kv3, pos, w1.astype(BF16), w2.astype(BF16)]
    if is_key:
        in_specs += [full((1, LANES)), full((nch, LANES)), full((nch, LANES)), full((nch, LANES)),
                     full((LANES, LANES))]
        args += [jnp.tile(g_k, NSA_KV_HEADS).reshape(1, LANES), *tables_cmp,
                 _head_block_diag(LANES, 1.0 / HEAD_DIM)]
        out_spec = pl.BlockSpec((1, nch, LANES), lambda b: (b, 0, 0))
        out_shape = jax.ShapeDtypeStruct((B, nch, LANES), BF16)
    else:
        out_spec = pl.BlockSpec((1, LANES, nch), lambda b: (b, 0, 0))
        out_shape = jax.ShapeDtypeStruct((B, LANES, nch), BF16)
    return pl.pallas_call(
        functools.partial(_nsa_cmp_kernel, is_key=is_key),
        grid=(B,),
        in_specs=in_specs,
        out_specs=out_spec,
        out_shape=out_shape,
        scratch_shapes=[pltpu.VMEM((nch, S * LANES), F32)],
        compiler_params=_params("arbitrary"),
        name="nsa_cmp_k" if is_key else "nsa_cmp_v",
    )(*args)


def _nsa_attn_kernel(qt_ref, kc_ref, vct_ref, ks_ref, vst_ref, kw_ref, vwt_ref, gt_ref, ov_ref, o_ref):
    g = pl.program_id(1)
    qb = pl.program_id(2)
    R = NSA_GROUP
    QT = Q_TILE
    KT = KEY_TILE
    NQ = R * QT
    t0 = qb * QT
    n_cmp_pad = kc_ref.shape[1]
    n_sel = ov_ref.shape[0]

    q_g = jnp.concatenate([qt_ref[0, 0, r * HEAD_DIM:(r + 1) * HEAD_DIM, :] for r in range(R)], axis=1)
    q2 = jnp.concatenate([q_g, q_g], axis=0)
    row_grp = lax.broadcasted_iota(jnp.int32, q2.shape, 0) // HEAD_DIM
    qpad = jnp.where(row_grp == g, q2, jnp.zeros_like(q2))

    tq_row = t0 + (lax.broadcasted_iota(jnp.int32, (1, NQ), 1) & (QT - 1))
    tile4 = lambda z: jnp.concatenate([z] * R, axis=1)

    sc = _dot(kc_ref[0], qpad)
    n_i = lax.broadcasted_iota(jnp.int32, (n_cmp_pad, 1), 0)
    cend = jnp.where(n_i < n_cmp_pad - 1, n_i * CMP_STRIDE + (CMP_BLOCK - 1), jnp.int32(2 ** 30))
    cvalid = cend <= tq_row
    sc = jnp.where(cvalid, sc, NEG_INF)
    mc = jnp.max(sc, axis=0, keepdims=True)
    ec = jnp.where(cvalid, jnp.exp(sc - mc), 0.0)
    pc = ec / jnp.maximum(jnp.sum(ec, axis=0, keepdims=True), F32_TINY)
    o_c = _dot(vct_ref[0], pc.astype(BF16))
    pc_sum = pc[:, 0:QT]
    for r in range(1, R):
        pc_sum = pc_sum + pc[:, r * QT:(r + 1) * QT]
    imp = _dot(ov_ref[...], pc_sum, HI)

    ji = lax.broadcasted_iota(jnp.int32, (n_sel, QT), 0)
    jf = ji.astype(F32)
    tq_sel = t0 + lax.broadcasted_iota(jnp.int32, (n_sel, QT), 1)
    cur = tq_sel >> SEL_SHIFT
    forced = (ji == 0) | (ji == cur) | (ji == cur - 1)
    valid = ji * SEL_BLOCK <= tq_sel
    score = jnp.where(valid, jnp.where(forced, FORCE_SCORE, imp), -1.0)
    sel = jnp.zeros((n_sel, QT), F32)
    for _ in range(min(SEL_TOPK, n_sel)):
        mx = jnp.max(score, axis=0, keepdims=True)
        jmin = jnp.min(jnp.where(score == mx, jf, 1e9), axis=0, keepdims=True)
        hit = jf == jmin
        sel = jnp.where(hit, 1.0, sel)
        score = jnp.where(hit, -3e38, score)
    selb = sel.astype(BF16)

    def online_step(s, vt, m, l, acc):
        m_new = jnp.maximum(m, jnp.max(s, axis=0, keepdims=True))
        alpha = jnp.exp(m - m_new)
        p = jnp.exp(s - m_new)
        l = l * alpha + jnp.sum(p, axis=0, keepdims=True)
        acc = acc * alpha + _dot(vt, p.astype(BF16))
        return m_new, l, acc

    init = (jnp.full((1, NQ), NEG_INF, F32), jnp.zeros((1, NQ), F32), jnp.zeros((HEAD_DIM, NQ), F32))
    ST = SEL_TILE
    blk_lane = lax.broadcasted_iota(jnp.int32, (ST, n_sel), 1)
    blk_row = lax.broadcasted_iota(jnp.int32, (ST, n_sel), 0)
    tq_st = t0 + lax.broadcasted_iota(jnp.int32, (ST, QT), 1)
    krow_st = lax.broadcasted_iota(jnp.int32, (ST, QT), 0)

    def sel_body(kt, carry):
        k0 = pl.multiple_of(kt * ST, ST)
        s = _dot(ks_ref[0, pl.ds(k0, ST), :], qpad)
        expand = jnp.where(((k0 + blk_row) >> SEL_SHIFT) == blk_lane, 1.0, 0.0).astype(BF16)
        picked = _dot(expand, selb)
        bias = jnp.where(k0 + krow_st <= tq_st, jnp.where(picked > 0.5, 0.0, NEG_INF), NEG_INF)
        return online_step(s + tile4(bias), vst_ref[0, kt], *carry)

    _, l_s, acc_s = lax.fori_loop(0, (t0 + QT + ST - 1) // ST, sel_body, init)

    n_wt = (WINDOW + QT) // KT
    k0w = pl.multiple_of(jnp.maximum(t0 - WINDOW, 0), KT)
    sw = _dot(kw_ref[0, pl.ds(k0w, WINDOW + QT), :], qpad)
    dw = (t0 + lax.broadcasted_iota(jnp.int32, (WINDOW + QT, QT), 1)
          - (k0w + lax.broadcasted_iota(jnp.int32, (WINDOW + QT, QT), 0)))
    sw = sw + tile4(jnp.where(dw >= 0, jnp.where(dw < WINDOW, 0.0, NEG_INF), NEG_INF))
    pw = jnp.exp(sw - jnp.max(sw, axis=0, keepdims=True))
    l_w = jnp.sum(pw, axis=0, keepdims=True)
    pw = pw.astype(BF16)
    kt_w = k0w // KT
    acc_w = _dot(vwt_ref[0, kt_w], pw[0:KT])
    for j in range(1, n_wt):
        acc_w = acc_w + _dot(vwt_ref[0, kt_w + j], pw[j * KT:(j + 1) * KT])

    gates = _sigmoid(gt_ref[0, 0])
    grow = lambda j: jnp.concatenate([gates[j, r:r + 1, :] for r in range(R)], axis=1)
    o = grow(0) * o_c + grow(1) * (acc_s / l_s) + grow(2) * (acc_w / l_w)
    halves = []
    for h in range(R // 2):
        pair = jnp.concatenate([o[:, (2 * h) * QT:(2 * h + 1) * QT],
                                o[:, (2 * h + 1) * QT:(2 * h + 2) * QT]], axis=0)
        halves.append(pair.T)
    o_ref[...] = jnp.concatenate(halves, axis=1)


def _nsa_attn(qt, kcmp, vct, ks3, vst, kw3, vwt, gt, ov_t, B, T):
    G, R = NSA_KV_HEADS, NSA_GROUP
    nq = T // Q_TILE
    nk = T // KEY_TILE
    nch = kcmp.shape[1]
    n_sel = ov_t.shape[0]
    return pl.pallas_call(
        _nsa_attn_kernel,
        grid=(B, G, nq),
        in_specs=[pl.BlockSpec((1, 1, R * HEAD_DIM, Q_TILE), lambda b, g, q: (b, q, g, 0)),
                  pl.BlockSpec((1, nch, LANES), lambda b, g, q: (b, 0, 0)),
                  pl.BlockSpec((1, HEAD_DIM, nch), lambda b, g, q: (b, g, 0)),
                  pl.BlockSpec((1, T, LANES), lambda b, g, q: (b, 0, 0)),
                  pl.BlockSpec((1, T // SEL_TILE, HEAD_DIM, SEL_TILE), lambda b, g, q: (b, 0, g, 0)),
                  pl.BlockSpec((1, T, LANES), lambda b, g, q: (b, 0, 0)),
                  pl.BlockSpec((1, nk, HEAD_DIM, KEY_TILE), lambda b, g, q: (b, 0, g, 0)),
                  pl.BlockSpec((1, 1, 3, R, Q_TILE), lambda b, g, q: (b, g, 0, 0, q)),
                  pl.BlockSpec((n_sel, nch), lambda b, g, q: (0, 0))],
        out_specs=pl.BlockSpec((Q_TILE, R * HEAD_DIM), lambda b, g, q: (b * nq + q, g)),
        out_shape=jax.ShapeDtypeStruct((B * T, NSA_WIDTH), F32),
        compiler_params=_params("arbitrary", "arbitrary", "arbitrary"),
        name="nsa_attn",
    )(qt, kcmp, vct, ks3, vst, kw3, vwt, gt, ov_t)


def _first_index_of(vals, target):
    idx = jnp.full_like(target, float(len(vals) - 1))
    for i in range(len(vals) - 2, -1, -1):
        idx = jnp.where(vals[i] == target, float(i), idx)
    return idx


def _pick(vals, idx):
    out = vals[-1]
    for i in range(len(vals) - 2, -1, -1):
        out = jnp.where(idx == float(i), vals[i], out)
    return out


def _route_rows(score, bias):
    E, G, P = N_EXPERTS, N_GROUPS, EXPERTS_PER_GROUP
    sel = score + bias
    s = [sel[e:e + 1, :] for e in range(E)]
    raw = [score[e:e + 1, :] for e in range(E)]
    grp = []
    for gi in range(G):
        a = s[gi * P:(gi + 1) * P]
        best = None
        for i in range(P):
            for j in range(i + 1, P):
                pair = a[i] + a[j]
                best = pair if best is None else jnp.maximum(best, pair)
        grp.append(best)
    gmax = functools.reduce(jnp.maximum, grp)
    g_star = _first_index_of(grp, gmax)
    v = [_pick([s[gi * P + i] for gi in range(G)], g_star) for i in range(P)]
    w = [_pick([raw[gi * P + i] for gi in range(G)], g_star) for i in range(P)]
    i1 = _first_index_of(v, functools.reduce(jnp.maximum, v))
    v2 = [jnp.where(i1 == float(i), -jnp.inf, v[i]) for i in range(P)]
    i2 = _first_index_of(v2, functools.reduce(jnp.maximum, v2))
    w1, w2 = _pick(w, i1), _pick(w, i2)
    tot = w1 + w2
    zero = jnp.zeros_like(tot)
    return jnp.concatenate([g_star * P + i1, g_star * P + i2, w1 / tot, w2 / tot, zero, zero, zero, zero],
                           axis=0)


def _merge_kernel(ys_ref, g_ref, bonus_ref, gng_ref, gnb_ref, bd_ref, yb_ref, pm_ref, x_ref, mod_ref,
                  ng_ref, wa_ref, wb_ref, wo_ref, rw_ref, rb_ref, o_x, o_h, o_route):
    m = mod_ref[pl.program_id(0)]
    bd = bd_ref[...]
    y = ys_ref[...]
    mean = _dot(y, bd, HI)
    yc = y - mean
    var = _dot(yc * yc, bd, HI)
    ya = (yc * lax.rsqrt(var + RWKV_GN_EPS) * gng_ref[...] + gnb_ref[...] + bonus_ref[...]) * g_ref[...]
    pm = pm_ref[...]
    D = x_ref.shape[1]
    mix = (_sigmoid(pm[:, 0:D]) * _dot(ya.astype(BF16), wa_ref[...])
           + _sigmoid(pm[:, D:2 * D]) * _dot(yb_ref[...].astype(BF16), wb_ref[...]))
    x = x_ref[...] + m[2:3] * _dot(mix.astype(BF16), wo_ref[...])
    o_x[...] = x
    ms = jnp.mean(x * x, axis=-1, keepdims=True)
    h = x * lax.rsqrt(ms + NORM_EPS) * ng_ref[...]
    h = h * (1.0 + m[4:5]) + m[3:4]
    o_h[...] = h
    score = _sigmoid(_dot_tb(rw_ref[...], h, HI))
    o_route[...] = _route_rows(score, rb_ref[...])


def _merge(ys, g, bonus, gn_g, gn_b, yb, pm, x2, mod, ng, wa, wb, wo, router_w, router_b, B, T, tm=256):
    N, D = x2.shape
    W = RWKV_WIDTH
    nt = T // tm
    row = lambda b, t: (b * nt + t, 0)
    full = lambda shape: pl.BlockSpec(shape, lambda b, t: (0,) * len(shape))
    return pl.pallas_call(
        _merge_kernel,
        grid=(B, nt),
        in_specs=[pl.BlockSpec((tm, W), row), pl.BlockSpec((tm, W), row), pl.BlockSpec((tm, W), row),
                  full((1, W)), full((1, W)), full((W, W)),
                  pl.BlockSpec((tm, NSA_WIDTH), row), pl.BlockSpec((tm, 2 * D), row),
                  pl.BlockSpec((tm, D), row), full((B, 6, D)), full((1, D)),
                  full((W, D)), full((NSA_WIDTH, D)), full((D, D)), full((N_EXPERTS, D)),
                  full((N_EXPERTS, 1))],
        out_specs=[pl.BlockSpec((tm, D), row), pl.BlockSpec((tm, D), row),
                   pl.BlockSpec((8, tm), lambda b, t: (0, b * nt + t))],
        out_shape=[jax.ShapeDtypeStruct((N, D), F32), jax.ShapeDtypeStruct((N, D), F32),
                   jax.ShapeDtypeStruct((8, N), F32)],
        compiler_params=_params("arbitrary", "arbitrary"),
        name="merge_out",
    )(ys, g, bonus, gn_g.reshape(1, W), gn_b.reshape(1, W), _head_block_diag(W, 1.0 / HEAD_DIM),
      yb, pm, x2, mod, ng, wa, wb, wo, router_w.T, router_b.reshape(N_EXPERTS, 1))


def _route(route, N):
    wts = route[TOP_K:2 * TOP_K].T
    NK = N * TOP_K
    e_flat = route[0:TOP_K].astype(jnp.int32).reshape(-1)
    onehot = (e_flat[:, None] == jnp.arange(N_EXPERTS, dtype=jnp.int32)[None, :]).astype(jnp.int32)
    csum = jnp.cumsum(onehot, axis=0)
    counts = csum[-1]
    rank = jnp.take_along_axis(csum, e_flat[:, None], axis=1)[:, 0] - 1
    padded = (counts + MOE_BLOCK - 1) // MOE_BLOCK * MOE_BLOCK
    pad_end = jnp.cumsum(padded)
    pad_start = pad_end - padded
    dest = pad_start[e_flat] + rank
    n_blk = -(-NK // MOE_BLOCK) + N_EXPERTS
    P = n_blk * MOE_BLOCK
    slot_assign = jnp.full((P,), -1, jnp.int32).at[dest].set(jnp.arange(NK, dtype=jnp.int32))
    blk_start = jnp.arange(n_blk, dtype=jnp.int32) * MOE_BLOCK
    blk_expert = jnp.clip(jnp.searchsorted(pad_end, blk_start, side='right'), 0, N_EXPERTS - 1)
    blk_used = (blk_start < pad_end[-1]).astype(jnp.int32)
    return wts, slot_assign, blk_expert.astype(jnp.int32), blk_used, n_blk


def _moe_kernel(be_ref, used_ref, slot_ref, h_hbm, wg_ref, wu_ref, wd_ref, o_hbm, xbuf, ybuf, sem_in,
                sem_out, *, n_tok):
    i = pl.program_id(0)
    MB = MOE_BLOCK
    base = i * MB

    @pl.when(used_ref[i] > 0)
    def _():
        def gather(r, _):
            a = slot_ref[base + r]
            tok = jnp.where(a < 0, 0, jnp.where(a >= n_tok, a - n_tok, a))
            pltpu.make_async_copy(h_hbm.at[pl.ds(tok, 1), :], xbuf.at[pl.ds(r, 1), :], sem_in).start()
            return 0

        lax.fori_loop(0, MB, gather, 0)

        def gather_wait(r, _):
            pltpu.make_async_copy(h_hbm.at[pl.ds(0, 1), :], xbuf.at[pl.ds(r, 1), :], sem_in).wait()
            return 0

        lax.fori_loop(0, MB, gather_wait, 0)
        x = xbuf[...].astype(BF16)
        gate = _dot(x, wg_ref[0])
        up = _dot(x, wu_ref[0])
        act = (gate * _sigmoid(gate) * up).astype(BF16)
        ybuf[...] = _dot(act, wd_ref[0])

        def scatter(r, _):
            a = slot_ref[base + r]

            @pl.when(a >= 0)
            def _():
                pltpu.make_async_copy(ybuf.at[pl.ds(r, 1), :], o_hbm.at[pl.ds(a, 1), :], sem_out).start()
            return 0

        lax.fori_loop(0, MB, scatter, 0)

        def scatter_wait(r, _):
            @pl.when(slot_ref[base + r] >= 0)
            def _():
                pltpu.make_async_copy(ybuf.at[pl.ds(r, 1), :], o_hbm.at[pl.ds(0, 1), :], sem_out).wait()
            return 0

        lax.fori_loop(0, MB, scatter_wait, 0)


def _moe(h_bf, slot_assign, blk_expert, blk_used, n_blk, wg, wu, wd):
    N, D = h_bf.shape
    DE = wg.shape[2]
    grid_spec = pltpu.PrefetchScalarGridSpec(
        num_scalar_prefetch=3,
        grid=(n_blk,),
        in_specs=[pl.BlockSpec(memory_space=pl.ANY),
                  pl.BlockSpec((1, D, DE), lambda i, be, us, sl: (be[i], 0, 0)),
                  pl.BlockSpec((1, D, DE), lambda i, be, us, sl: (be[i], 0, 0)),
                  pl.BlockSpec((1, DE, D), lambda i, be, us, sl: (be[i], 0, 0))],
        out_specs=pl.BlockSpec(memory_space=pl.ANY),
        scratch_shapes=[pltpu.VMEM((MOE_BLOCK, D), F32), pltpu.VMEM((MOE_BLOCK, D), F32),
                        pltpu.SemaphoreType.DMA(()), pltpu.SemaphoreType.DMA(())],
    )
    return pl.pallas_call(
        functools.partial(_moe_kernel, n_tok=N),
        grid_spec=grid_spec,
        out_shape=jax.ShapeDtypeStruct((TOP_K * N, D), F32),
        compiler_params=_params("arbitrary"),
        name="moe_experts",
    )(blk_expert, blk_used, slot_assign, h_bf, wg, wu, wd)


def _final_kernel(x_ref, y0_ref, y1_ref, w_ref, mod_ref, o_ref):
    m = mod_ref[pl.program_id(0)]
    w = w_ref[...]
    o_ref[...] = x_ref[...] + m[5:6] * (w[:, 0:1] * y0_ref[...] + w[:, 1:2] * y1_ref[...])


def _final(x2, ybuf, wts, mod, B, T, tm=512):
    N, D = x2.shape
    nt = T // tm
    row = lambda b, t: (b * nt + t, 0)
    return pl.pallas_call(
        _final_kernel,
        grid=(B, nt),
        in_specs=[pl.BlockSpec((tm, D), row), pl.BlockSpec((tm, D), row),
                  pl.BlockSpec((tm, D), lambda b, t: (N // tm + b * nt + t, 0)),
                  pl.BlockSpec((tm, TOP_K), row), pl.BlockSpec((B, 6, D), lambda b, t: (0, 0, 0))],
        out_specs=pl.BlockSpec((tm, D), row),
        out_shape=jax.ShapeDtypeStruct((N, D), F32),
        compiler_params=_params("arbitrary", "arbitrary"),
        name="moe_combine",
    )(x2, ybuf, ybuf, wts, mod)


def _overlap_t(n_sel, n_cmp_pad):
    ci = jnp.arange(n_cmp_pad)[None, :] * CMP_STRIDE
    sj = jnp.arange(n_sel)[:, None] * SEL_BLOCK
    ov = (ci <= sj + SEL_BLOCK - 1) & (ci + CMP_BLOCK - 1 >= sj) & (jnp.arange(n_cmp_pad)[None, :] < n_cmp_pad - 1)
    return ov.astype(F32)


def kernel(x, c, w_ada, b_ada, norm_g, w_in, b_in, rwkv_mu, rwkv_w0, rwkv_w2, rwkv_a0, rwkv_a2, rwkv_g2,
           rwkv_k_k, rwkv_k_a, rwkv_r_k, rwkv_gn_g, rwkv_gn_b, qk_norm_g, cmp_pos, cmp_w1, cmp_w2,
           w_up_rwkv, w_up_nsa, w_out, router_w, router_b, exp_w_gate, exp_w_up, exp_w_down):
    B, T, D = x.shape
    L = w_ada.shape[0]
    N = B * T
    mods = _ada(c, w_ada, b_ada)
    tables = _rope_tables(jnp.arange(T, dtype=jnp.int32))
    nch = T // CMP_STRIDE
    tables_cmp = _rope_tables(jnp.arange(nch, dtype=jnp.int32) * CMP_STRIDE + CMP_BLOCK - 1)
    ov_t = _overlap_t(T // SEL_BLOCK, nch)
    n_gate = NSA_GATE_COLS
    x2 = x.reshape(N, D)
    for l in range(L):
        g0 = _SEG_KV[1] + n_gate
        w_pad = jnp.concatenate([w_in[l][:, :g0], jnp.zeros((D, GATE_PAD - n_gate), F32), w_in[l][:, g0:]],
                                axis=1).astype(BF16)
        b_pad = jnp.concatenate([b_in[l][:g0], jnp.zeros((GATE_PAD - n_gate,), F32), b_in[l][g0:]]).reshape(1, -1)
        p_rw, p_q, p_kv, p_gate, p_merge = _inproj(x2, mods[l], norm_g[l, 0].reshape(1, D), w_pad, b_pad, B, T)
        r, k, v, al, bb, ld, g, bonus = _rwkv_pre(p_rw, rwkv_mu[l], rwkv_w0[l], rwkv_w2[l], rwkv_a0[l],
                                                  rwkv_a2[l], rwkv_g2[l], rwkv_k_k[l], rwkv_k_a[l],
                                                  rwkv_r_k[l], B, T)
        ys = _rwkv_scan(r, k, v, al, bb, ld, B, T)
        qt, ks, kw, vst, vwt = _nsa_prep(p_q, p_kv, tables, qk_norm_g[l], B, T)
        kv3 = p_kv.reshape(B, T, KV_COLS)
        kcmp = _nsa_cmp(kv3, 0, cmp_pos[l], cmp_w1[l], cmp_w2[l], qk_norm_g[l, 1], tables_cmp)
        vct = _nsa_cmp(kv3, 1, cmp_pos[l], cmp_w1[l], cmp_w2[l], None, None)
        gt = p_gate[:, :n_gate].reshape(B, T, NSA_KV_HEADS, NSA_GROUP, 3).transpose(0, 2, 4, 3, 1)
        yb = _nsa_attn(qt, kcmp, vct, ks.reshape(B, T, LANES), vst, kw.reshape(B, T, LANES), vwt, gt, ov_t,
                       B, T)
        x2, h2, route = _merge(ys, g, bonus, rwkv_gn_g[l], rwkv_gn_b[l], yb, p_merge, x2, mods[l],
                               norm_g[l, 1].reshape(1, D), w_up_rwkv[l].astype(BF16),
                               w_up_nsa[l].astype(BF16), w_out[l].astype(BF16), router_w, router_b, B, T)
        wts, slot_assign, blk_expert, blk_used, n_blk = _route(route, N)
        ybuf = _moe(h2, slot_assign, blk_expert, blk_used, n_blk, exp_w_gate[l].astype(BF16),
                    exp_w_up[l].astype(BF16), exp_w_down[l].astype(BF16))
        x2 = _final(x2, ybuf, wts, mods[l], B, T)
    return x2.reshape(B, T, D)
```

```python
import functools
import math

import jax
import jax.numpy as jnp
from jax import lax
from jax.experimental import pallas as pl
from jax.experimental.pallas import tpu as pltpu

F32 = jnp.float32
BF16 = jnp.bfloat16
HI = lax.Precision.HIGHEST

D_MODEL = 1024
RWKV_HEADS = 8
HEAD_DIM = 64
RWKV_WIDTH = RWKV_HEADS * HEAD_DIM
DECAY_LORA = 64
ICLR_LORA = 64
GATE_LORA = 128
RWKV_GN_EPS = 64e-5
RWKV_COLS = 3 * RWKV_WIDTH + DECAY_LORA + ICLR_LORA + GATE_LORA

NSA_Q_HEADS = 8
NSA_KV_HEADS = 2
NSA_GROUP = NSA_Q_HEADS // NSA_KV_HEADS
NSA_WIDTH = NSA_Q_HEADS * HEAD_DIM
CMP_STRIDE = 16
CMP_BLOCK = 2 * CMP_STRIDE
CMP_HIDDEN = 256
SEL_BLOCK = 64
SEL_SHIFT = 6
SEL_TOPK = 16
WINDOW = 512
FORCE_SCORE = 1e4
NEG_INF = -1e30
ROPE_THETA = 500000.0
ROPE_DIM = HEAD_DIM // 4
KV_COLS = 6 * NSA_KV_HEADS * HEAD_DIM
NSA_GATE_COLS = 3 * NSA_Q_HEADS
GATE_PAD = 128

N_EXPERTS = 16
N_GROUPS = 4
EXPERTS_PER_GROUP = N_EXPERTS // N_GROUPS
TOP_K = 2
D_EXPERT = 512
MOE_BLOCK = 256
NORM_EPS = 1e-6

LANES = 128
CHUNK = 64
KEY_TILE = 128
SEL_TILE = 512
V_ROWS = 80
Q_SCALE = HEAD_DIM ** -0.5 * math.log2(math.e)
Q_TILE = 128
F32_TINY = float(jnp.finfo(jnp.float32).tiny)

_SEG_RW = (0, RWKV_COLS)
_SEG_Q = (_SEG_RW[1], _SEG_RW[1] + NSA_WIDTH)
_SEG_KV = (_SEG_Q[1], _SEG_Q[1] + KV_COLS)
_SEG_GATE = (_SEG_KV[1], _SEG_KV[1] + GATE_PAD)
_SEG_MERGE = (_SEG_GATE[1], _SEG_GATE[1] + 2 * D_MODEL)
IN_COLS_PAD = _SEG_MERGE[1]

_VMEM_LIMIT = 56 * 1024 * 1024


def _dot(a, b, precision=None):
    return jnp.dot(a, b, preferred_element_type=F32, precision=precision)


def _dot_tb(a, b, precision=None):
    return lax.dot_general(a, b, (((1,), (1,)), ((), ())), preferred_element_type=F32,
                           precision=precision)


def _dot_ta(a, b, precision=None):
    return lax.dot_general(a, b, (((0,), (0,)), ((), ())), preferred_element_type=F32,
                           precision=precision)


def _params(*sem):
    return pltpu.CompilerParams(dimension_semantics=sem, vmem_limit_bytes=_VMEM_LIMIT)


def _sigmoid(x):
    return 1.0 / (1.0 + jnp.exp(-x))


def _ada_kernel(c_ref, w_ref, b_ref, o_ref):
    c = c_ref[...]
    s = c * _sigmoid(c)
    o_ref[0] = _dot(s, w_ref[0], HI) + b_ref[0]


def _ada(c, w_ada, b_ada):
    L, D, D6 = w_ada.shape
    B = c.shape[0]
    rows = 8
    cp = jnp.zeros((rows, D), F32).at[:B].set(c)
    tn = 1536
    out = pl.pallas_call(
        _ada_kernel,
        grid=(L, D6 // tn),
        in_specs=[pl.BlockSpec((rows, D), lambda l, j: (0, 0)),
                  pl.BlockSpec((1, D, tn), lambda l, j: (l, 0, j)),
                  pl.BlockSpec((1, 1, tn), lambda l, j: (l, 0, j))],
        out_specs=pl.BlockSpec((1, rows, tn), lambda l, j: (l, 0, j)),
        out_shape=jax.ShapeDtypeStruct((L, rows, D6), F32),
        compiler_params=_params("arbitrary", "arbitrary"),
        name="ada_mod",
    )(cp, w_ada, b_ada.reshape(L, 1, D6))
    return out[:, :B].reshape(L, B, 6, D)


def _inproj_kernel(x_ref, mod_ref, g_ref, w_ref, b_ref, o_rw, o_q, o_kv, o_gate, o_merge):
    m = mod_ref[pl.program_id(0)]
    x = x_ref[...]
    ms = jnp.mean(x * x, axis=-1, keepdims=True)
    h = x * lax.rsqrt(ms + NORM_EPS) * g_ref[...]
    h = h * (1.0 + m[1:2]) + m[0:1]
    hb = h.astype(BF16)
    for o, (a, e) in ((o_rw, _SEG_RW), (o_q, _SEG_Q), (o_kv, _SEG_KV), (o_gate, _SEG_GATE),
                      (o_merge, _SEG_MERGE)):
        o[...] = _dot(hb, w_ref[:, a:e]) + b_ref[:, a:e]


def _inproj(x2, mod, g, w_pad, b_pad, B, T, tm=256):
    N, D = x2.shape
    nt = T // tm
    row = lambda b, t: (b * nt + t, 0)
    widths = [e - a for a, e in (_SEG_RW, _SEG_Q, _SEG_KV, _SEG_GATE, _SEG_MERGE)]
    return pl.pallas_call(
        _inproj_kernel,
        grid=(B, nt),
        in_specs=[pl.BlockSpec((tm, D), row),
                  pl.BlockSpec((B, 6, D), lambda b, t: (0, 0, 0)),
                  pl.BlockSpec((1, D), lambda b, t: (0, 0)),
                  pl.BlockSpec((D, IN_COLS_PAD), lambda b, t: (0, 0)),
                  pl.BlockSpec((1, IN_COLS_PAD), lambda b, t: (0, 0))],
        out_specs=[pl.BlockSpec((tm, w), row) for w in widths],
        out_shape=[jax.ShapeDtypeStruct((N, w), F32) for w in widths],
        compiler_params=_params("arbitrary", "arbitrary"),
        name="in_proj",
    )(x2, mod, g, w_pad, b_pad)


def _rwkv_pre_kernel(p_ref, mu_ref, w0_ref, w2_ref, a0_ref, a2_ref, g2_ref, kk_ref, ka_ref, rk_ref,
                     bd_ref, o_r, o_k, o_v, o_al, o_b, o_ld, o_g, o_bonus, carry_ref):
    W = RWKV_WIDTH

    @pl.when(pl.program_id(1) == 0)
    def _():
        carry_ref[...] = jnp.zeros_like(carry_ref)

    p = p_ref[...]
    ts = p.shape[0]
    rows = lax.broadcasted_iota(jnp.int32, p.shape, 0)
    shifted = jnp.where(rows == 0, carry_ref[0:1, :], pltpu.roll(p, 1, 0))
    carry_ref[0:1, :] = p[ts - 1:ts, :]
    pm = p + (shifted - p) * mu_ref[...]
    r = pm[:, 0:W]
    k = pm[:, W:2 * W]
    v = pm[:, 2 * W:3 * W]
    wa = pm[:, 3 * W:3 * W + DECAY_LORA + ICLR_LORA]
    gl = pm[:, 3 * W + DECAY_LORA + ICLR_LORA:]
    xw = w0_ref[...] + _dot(jnp.tanh(wa), w2_ref[...], HI)
    ld = -math.exp(-0.5) * _sigmoid(xw)
    a = _sigmoid(a0_ref[...] + _dot(wa, a2_ref[...], HI))
    g = _dot(_sigmoid(gl), g2_ref[...], HI)
    bd = bd_ref[...]
    kk = k * kk_ref[...]
    nrm = jnp.sqrt(_dot(kk * kk, bd, HI))
    kk = kk / jnp.maximum(nrm, 1e-12)
    k2 = k * (1.0 + (a - 1.0) * ka_ref[...])
    bonus = _dot(r * k2 * rk_ref[...], bd, HI) * v
    o_r[...] = r
    o_k[...] = k2
    o_v[...] = v
    o_al[...] = kk
    o_b[...] = -kk * a
    o_ld[...] = ld
    o_g[...] = g
    o_bonus[...] = bonus


def _head_block_diag(width, scale=1.0):
    i = jnp.arange(width) // HEAD_DIM
    return (i[:, None] == i[None, :]).astype(F32) * scale


def _rwkv_pre(p_rw, mu, w0, w2, a0, a2, g2, k_k, k_a, r_k, B, T, ts=256):
    N = p_rw.shape[0]
    W = RWKV_WIDTH
    nt = T // ts
    row = lambda b, t: (b * nt + t, 0)
    zl = jnp.zeros((DECAY_LORA, W), F32)
    w2p = jnp.concatenate([w2, zl], axis=0)
    a2p = jnp.concatenate([zl, a2], axis=0)
    full = lambda shape: pl.BlockSpec(shape, lambda b, t: (0,) * len(shape))
    vec = lambda z: z.reshape(1, -1)
    return pl.pallas_call(
        _rwkv_pre_kernel,
        grid=(B, nt),
        in_specs=[pl.BlockSpec((ts, RWKV_COLS), row), full((1, RWKV_COLS)), full((1, W)),
                  full((2 * DECAY_LORA, W)), full((1, W)), full((2 * DECAY_LORA, W)),
                  full((GATE_LORA, W)), full((1, W)), full((1, W)), full((1, W)), full((W, W))],
        out_specs=[pl.BlockSpec((ts, W), row)] * 8,
        out_shape=[jax.ShapeDtypeStruct((N, W), F32)] * 8,
        scratch_shapes=[pltpu.VMEM((8, RWKV_COLS), F32)],
        compiler_params=_params("arbitrary", "arbitrary"),
        name="rwkv_pre",
    )(p_rw, vec(mu), vec(w0), w2p, vec(a0), a2p, g2, vec(k_k), vec(k_a), vec(r_k),
      _head_block_diag(W))


def _bf(x):
    return x.astype(BF16)


def _scan_local(chunks, eye, strict, incl, m0, m1):
    C = CHUNK
    n = range(len(chunks))
    st = lambda z: jnp.concatenate([z * m0, z * m1], axis=0)
    zero = jnp.zeros((2 * C, 2 * C), F32)
    at_b, rt_s, vs, vs_b, lhs_a, rhs_a, bk_t, dcol = [], [], [], [], [], [], [], []
    for r, k, v, al, bb, ld, cum in chunks:
        tot = cum[C - 1:C, :]
        dinv = jnp.exp(-cum)
        dend = jnp.exp(tot - cum)
        at_b.append(_bf(st(al * jnp.exp(cum - ld))))
        rt_s.append(st(r * jnp.exp(cum)))
        vs.append(st(v))
        vs_b.append(_bf(vs[-1]))
        lhs_a.append(jnp.concatenate([at_b[-1], _bf(rt_s[-1])], axis=0))
        rhs_a.append(_bf(jnp.concatenate([st(bb * dinv), st(k * dinv)], axis=0)))
        bk_t.append(_bf(jnp.concatenate([st(bb * dend).T, st(k * dend).T], axis=1)))
        dcol.append(jnp.sum(eye * jnp.exp(tot), axis=1, keepdims=True))
    A = [_dot_tb(lhs_a[i], rhs_a[i]) for i in n]
    a_ab = [jnp.where(strict, A[i][0:2 * C, 0:2 * C], zero) for i in n]
    a_ak = [_bf(jnp.where(strict, A[i][0:2 * C, 2 * C:4 * C], zero)) for i in n]
    a_r = [_bf(jnp.concatenate([jnp.where(incl, A[i][2 * C:4 * C, 0:2 * C], zero),
                                jnp.where(incl, A[i][2 * C:4 * C, 2 * C:4 * C], zero)], axis=1)) for i in n]
    akv = [_bf(_dot(a_ak[i], vs_b[i])) for i in n]
    pw = a_ab
    tinv = [eye + pw[i] for i in n]
    for _ in range(5):
        pw_b = [_bf(pw[i]) for i in n]
        pw = [_dot(pw_b[i], pw_b[i]) for i in n]
        tinv = [tinv[i] + _dot(_bf(pw[i]), _bf(tinv[i])) for i in n]
    X = [_dot(_bf(tinv[i]), jnp.concatenate([at_b[i], akv[i]], axis=1)) for i in n]
    w_b = [_bf(X[i][:, 0:LANES]) for i in n]
    uv0 = [jnp.concatenate([_bf(X[i][:, LANES:2 * LANES]), vs_b[i]], axis=0) for i in n]
    m_h = [_bf(_dot(bk_t[i][:, 0:2 * C], w_b[i])) for i in n]
    g_h = [_dot(bk_t[i], uv0[i]) for i in n]
    q_h = [_bf(rt_s[i] + _dot(a_r[i][:, 0:2 * C], w_b[i])) for i in n]
    y0 = [_dot(a_r[i], uv0[i]) for i in n]
    return [(m_h[i], g_h[i], dcol[i], q_h[i], y0[i]) for i in n]


def _scan_steps(local, H):
    C = CHUNK
    ys = []
    for m_h, g_h, dcol, q_h, y0 in local:
        h_b = _bf(H)
        Y = _dot(q_h, h_b) + y0
        ys.append(Y[0:C] + Y[C:2 * C])
        H = dcol * H + _dot(m_h, h_b) + g_h
    return ys, H


def _rwkv_scan_kernel(r_ref, k_ref, v_ref, al_ref, b_ref, ld_ref, o_ref, h_ref):
    C = CHUNK
    tc = r_ref.shape[0]

    @pl.when(pl.program_id(2) == 0)
    def _():
        h_ref[...] = jnp.zeros_like(h_ref)

    tri = (lax.broadcasted_iota(jnp.int32, (C, C), 1) <= lax.broadcasted_iota(jnp.int32, (C, C), 0)).astype(F32)
    r2 = lax.broadcasted_iota(jnp.int32, (2 * C, 2 * C), 0)
    c2 = lax.broadcasted_iota(jnp.int32, (2 * C, 2 * C), 1)
    eye = (r2 == c2).astype(F32)
    strict = (c2 & (C - 1)) < (r2 & (C - 1))
    incl = (c2 & (C - 1)) <= (r2 & (C - 1))
    lane = lax.broadcasted_iota(jnp.int32, (C, LANES), 1)
    m0 = (lane < HEAD_DIM).astype(F32)
    m1 = 1.0 - m0
    nc = tc // C
    cum = _dot(tri, jnp.concatenate([ld_ref[c * C:(c + 1) * C, :] for c in range(nc)], axis=1), HI)
    chunks = []
    for c in range(nc):
        sl = slice(c * C, (c + 1) * C)
        chunks.append((r_ref[sl, :], k_ref[sl, :], v_ref[sl, :], al_ref[sl, :], b_ref[sl, :], ld_ref[sl, :],
                       cum[:, c * LANES:(c + 1) * LANES]))
    ys, H = _scan_steps(_scan_local(chunks, eye, strict, incl, m0, m1), h_ref[...])
    for c in range(nc):
        o_ref[c * C:(c + 1) * C, :] = ys[c]
    h_ref[...] = H


def _rwkv_scan(r, k, v, al, bb, ld, B, T, tc=512):
    N, W = r.shape
    nt = T // tc
    spec = pl.BlockSpec((tc, LANES), lambda b, h, t: (b * nt + t, h))
    return pl.pallas_call(
        _rwkv_scan_kernel,
        grid=(B, W // LANES, nt),
        in_specs=[spec] * 6,
        out_specs=spec,
        out_shape=jax.ShapeDtypeStruct((N, W), F32),
        scratch_shapes=[pltpu.VMEM((LANES, LANES), F32)],
        compiler_params=_params("arbitrary", "arbitrary", "arbitrary"),
        name="rwkv_scan",
    )(r, k, v, al, bb, ld)


def _rope_tables(pos):
    half = ROPE_DIM // 2
    inv = jnp.power(ROPE_THETA, -jnp.arange(half, dtype=F32) * 2.0 / ROPE_DIM)
    ang = pos.astype(F32)[:, None] * inv[None, :]
    cos, sin = jnp.cos(ang), jnp.sin(ang)
    n = pos.shape[0]
    rest = HEAD_DIM - ROPE_DIM
    c = jnp.concatenate([cos, cos, jnp.ones((n, rest), F32)], axis=1)
    s_dn = jnp.concatenate([-sin, jnp.zeros((n, half + rest), F32)], axis=1)
    s_up = jnp.concatenate([jnp.zeros((n, half), F32), sin, jnp.zeros((n, rest), F32)], axis=1)
    rep = LANES // HEAD_DIM
    return jnp.tile(c, (1, rep)), jnp.tile(s_dn, (1, rep)), jnp.tile(s_up, (1, rep))


def _norm_rope(x, bd, g, c, s_dn, s_up):
    width = x.shape[1]
    half = ROPE_DIM // 2
    rep = width // LANES
    tile = (lambda z: jnp.concatenate([z] * rep, axis=1)) if rep > 1 else (lambda z: z)
    ms = _dot(x * x, bd, HI)
    xn = x * lax.rsqrt(ms + NORM_EPS) * g
    return (xn * tile(c) + pltpu.roll(xn, width - half, 1) * tile(s_dn)
            + pltpu.roll(xn, half, 1) * tile(s_up))


def _nsa_prep_kernel(q_ref, kv_ref, c_ref, sd_ref, su_ref, gq_ref, gs_ref, gw_ref, bdq_ref, bdk_ref,
                     o_qt, o_ks, o_kw, o_vst, o_vwt):
    c, sd, su = c_ref[...], sd_ref[...], su_ref[...]
    q = _norm_rope(q_ref[...], bdq_ref[...], gq_ref[...], c, sd, su) * Q_SCALE
    qt = q.T
    kv = kv_ref[...]
    bdk = bdk_ref[...]
    o_ks[...] = _norm_rope(kv[:, 2 * LANES:3 * LANES], bdk, gs_ref[...], c, sd, su).astype(BF16)
    o_kw[...] = _norm_rope(kv[:, 4 * LANES:5 * LANES], bdk, gw_ref[...], c, sd, su).astype(BF16)
    ones_rows = jnp.where(lax.broadcasted_iota(jnp.int32, (V_ROWS - HEAD_DIM, q.shape[0]), 0) == 0, 1.0, 0.0)

    def values_t(x):
        xt = x.T
        return jnp.concatenate([xt[0:HEAD_DIM], ones_rows, xt[HEAD_DIM:2 * HEAD_DIM], ones_rows], axis=0)

    vst = values_t(kv[:, 3 * LANES:4 * LANES])
    vwt = values_t(kv[:, 5 * LANES:6 * LANES])
    for j in range(q.shape[0] // KEY_TILE):
        sl = slice(j * KEY_TILE, (j + 1) * KEY_TILE)
        o_qt[0, j] = qt[:, sl].astype(BF16)
        o_vwt[0, j] = vwt[:, sl].astype(BF16)
    for j in range(q.shape[0] // SEL_TILE):
        o_vst[0, j] = vst[:, j * SEL_TILE:(j + 1) * SEL_TILE].astype(BF16)


def _nsa_prep(q, kv, tables, qk_g, B, T, ts=512):
    N = q.shape[0]
    nt = T // ts
    nk = ts // KEY_TILE
    ns = ts // SEL_TILE
    row = lambda b, t: (b * nt + t, 0)
    full = lambda shape: pl.BlockSpec(shape, lambda b, t: (0,) * len(shape))
    tab = pl.BlockSpec((ts, LANES), lambda b, t: (t, 0))
    gq = jnp.tile(qk_g[0], NSA_Q_HEADS).reshape(1, NSA_WIDTH)
    gs = jnp.tile(qk_g[2], NSA_KV_HEADS).reshape(1, LANES)
    gw = jnp.tile(qk_g[3], NSA_KV_HEADS).reshape(1, LANES)
    tiled = lambda rows: pl.BlockSpec((1, nk, rows, KEY_TILE), lambda b, t: (b, t, 0, 0))
    return pl.pallas_call(
        _nsa_prep_kernel,
        grid=(B, nt),
        in_specs=[pl.BlockSpec((ts, NSA_WIDTH), row), pl.BlockSpec((ts, KV_COLS), row), tab, tab, tab,
                  full((1, NSA_WIDTH)), full((1, LANES)), full((1, LANES)),
                  full((NSA_WIDTH, NSA_WIDTH)), full((LANES, LANES))],
        out_specs=[tiled(NSA_WIDTH), pl.BlockSpec((ts, LANES), row), pl.BlockSpec((ts, LANES), row),
                   pl.BlockSpec((1, ns, NSA_KV_HEADS * V_ROWS, SEL_TILE), lambda b, t: (b, t, 0, 0)),
                   tiled(NSA_KV_HEADS * V_ROWS)],
        out_shape=[jax.ShapeDtypeStruct((B, T // KEY_TILE, NSA_WIDTH, KEY_TILE), BF16),
                   jax.ShapeDtypeStruct((N, LANES), BF16), jax.ShapeDtypeStruct((N, LANES), BF16),
                   jax.ShapeDtypeStruct((B, T // SEL_TILE, NSA_KV_HEADS * V_ROWS, SEL_TILE), BF16),
                   jax.ShapeDtypeStruct((B, T // KEY_TILE, NSA_KV_HEADS * V_ROWS, KEY_TILE), BF16)],
        compiler_params=_params("arbitrary", "arbitrary"),
        name="nsa_prep",
    )(q, kv, *tables, gq, gs, gw, _head_block_diag(NSA_WIDTH, 1.0 / HEAD_DIM),
      _head_block_diag(LANES, 1.0 / HEAD_DIM))


def _gelu_tanh(x):
    return 0.5 * x * (1.0 + jnp.tanh(0.7978845608028654 * (x + 0.044715 * x * x * x)))


def _nsa_cmp_kernel(x_ref, pos_ref, w1_ref, w2_ref, *rest, is_key):
    if is_key:
        g_ref, c_ref, sd_ref, su_ref, bd_ref, o_ref, xs_ref = rest
    else:
        o_ref, xs_ref = rest
    nch = xs_ref.shape[0]
    S = CMP_STRIDE
    for j in range(S):
        xs_ref[:, j * LANES:(j + 1) * LANES] = x_ref[0, pl.ds(j, nch, stride=S), :]
    xs = xs_ref[...]
    first = _dot((xs + pos_ref[0:1, :]).astype(BF16), w1_ref[0])
    second = _dot((xs + pos_ref[1:2, :]).astype(BF16), w1_ref[1])
    hid = first + pltpu.roll(second, nch - 1, 0)
    out = _dot(_gelu_tanh(hid).astype(BF16), w2_ref[...])
    rows = lax.broadcasted_iota(jnp.int32, out.shape, 0)
    if is_key:
        out = _norm_rope(out, bd_ref[...], g_ref[...], c_ref[...], sd_ref[...], su_ref[...])
        o_ref[0] = jnp.where(rows < nch - 1, out, 0.0).astype(BF16)
    else:
        o_ref[0] = jnp.where(rows < nch - 1, out, 0.0).T.astype(BF16)


def _nsa_cmp(kv3, which, cmp_pos, cmp_w1, cmp_w2, g_k, tables_cmp):
    B, T, _ = kv3.shape
    S = CMP_STRIDE
    nch = T // S
    is_key = which == 0
    eye2 = jnp.eye(NSA_KV_HEADS, dtype=F32)
    w1 = cmp_w1[which].reshape(CMP_BLOCK, HEAD_DIM, CMP_HIDDEN)
    w1 = jnp.einsum('jdh,ge->jgdeh', w1, eye2).reshape(2, S * LANES, NSA_KV_HEADS * CMP_HIDDEN)
    w2 = jnp.einsum('hd,ge->ghed', cmp_w2[which], eye2).reshape(NSA_KV_HEADS * CMP_HIDDEN, LANES)
    pos = jnp.tile(cmp_pos[which].reshape(2, S, 1, HEAD_DIM), (1, 1, NSA_KV_HEADS, 1)).reshape(2, S * LANES)
    full = lambda shape: pl.BlockSpec(shape, lambda b: (0,) * len(shape))
    in_specs = [pl.BlockSpec((1, T, LANES), lambda b: (b, 0, which)), full(pos.shape), full(w1.shape),
                full(w2.shape)]
    args = [kv3, pos, w1.astype(BF16), w2.astype(BF16)]
    if is_key:
        in_specs += [full((1, LANES)), full((nch, LANES)), full((nch, LANES)), full((nch, LANES)),
                     full((LANES, LANES))]
        args += [jnp.tile(g_k, NSA_KV_HEADS).reshape(1, LANES), *tables_cmp,
                 _head_block_diag(LANES, 1.0 / HEAD_DIM)]
        out_spec = pl.BlockSpec((1, nch, LANES), lambda b: (b, 0, 0))
        out_shape = jax.ShapeDtypeStruct((B, nch, LANES), BF16)
    else:
        out_spec = pl.BlockSpec((1, LANES, nch), lambda b: (b, 0, 0))
        out_shape = jax.ShapeDtypeStruct((B, LANES, nch), BF16)
    return pl.pallas_call(
        functools.partial(_nsa_cmp_kernel, is_key=is_key),
        grid=(B,),
        in_specs=in_specs,
        out_specs=out_spec,
        out_shape=out_shape,
        scratch_shapes=[pltpu.VMEM((nch, S * LANES), F32)],
        compiler_params=_params("arbitrary"),
        name="nsa_cmp_k" if is_key else "nsa_cmp_v",
    )(*args)


def _nsa_attn_kernel(qt_ref, kc_ref, vct_ref, ks_ref, vst_ref, kw_ref, vwt_ref, gt_ref, ov_ref, o_ref,
                     bias_ref, s0_ref, s1_ref, p0_ref, p1_ref):
    g = pl.program_id(1)
    qb = pl.program_id(2)
    R = NSA_GROUP
    QT = Q_TILE
    KT = KEY_TILE
    NQ = R * QT
    t0 = qb * QT
    n_cmp_pad = kc_ref.shape[1]
    n_sel = ov_ref.shape[0]

    q_g = jnp.concatenate([qt_ref[0, 0, r * HEAD_DIM:(r + 1) * HEAD_DIM, :] for r in range(R)], axis=1)
    q2 = jnp.concatenate([q_g, q_g], axis=0)
    row_grp = lax.broadcasted_iota(jnp.int32, q2.shape, 0) // HEAD_DIM
    qpad = jnp.where(row_grp == g, q2, jnp.zeros_like(q2))

    tq_row = t0 + (lax.broadcasted_iota(jnp.int32, (1, NQ), 1) & (QT - 1))
    tile4 = lambda z: jnp.concatenate([z] * R, axis=1)

    sc = _dot(kc_ref[0], qpad)
    n_i = lax.broadcasted_iota(jnp.int32, (n_cmp_pad, 1), 0)
    cend = jnp.where(n_i < n_cmp_pad - 1, n_i * CMP_STRIDE + (CMP_BLOCK - 1), jnp.int32(2 ** 30))
    cvalid = cend <= tq_row
    sc = jnp.where(cvalid, sc, NEG_INF)
    mc = jnp.max(sc, axis=0, keepdims=True)
    ec = jnp.where(cvalid, jnp.exp2(sc - mc), 0.0)
    pc = ec / jnp.maximum(jnp.sum(ec, axis=0, keepdims=True), F32_TINY)
    o_c = _dot(vct_ref[0], pc.astype(BF16))
    pc_sum = pc[:, 0:QT]
    for r in range(1, R):
        pc_sum = pc_sum + pc[:, r * QT:(r + 1) * QT]
    imp = _dot(ov_ref[...], pc_sum, HI)

    ji = lax.broadcasted_iota(jnp.int32, (n_sel, QT), 0)
    jf = ji.astype(F32)
    tq_sel = t0 + lax.broadcasted_iota(jnp.int32, (n_sel, QT), 1)
    cur = tq_sel >> SEL_SHIFT
    forced = (ji == 0) | (ji == cur) | (ji == cur - 1)
    valid = ji * SEL_BLOCK <= tq_sel
    score = jnp.where(valid, jnp.where(forced, FORCE_SCORE, imp), -1.0)
    sel = jnp.zeros((n_sel, QT), F32)
    for _ in range(min(SEL_TOPK, n_sel)):
        mx = jnp.max(score, axis=0, keepdims=True)
        jmin = jnp.min(jnp.where(score == mx, jf, 1e9), axis=0, keepdims=True)
        hit = jf == jmin
        sel = jnp.where(hit, 1.0, sel)
        score = jnp.where(hit, -3e38, score)
    bias_ref[...] = (sel - 1.0) * (-NEG_INF)

    CW = 2 * QT
    groups = [slice(c * CW, (c + 1) * CW) for c in range(NQ // CW)]
    wide = lambda z: jnp.concatenate([z] * (CW // QT), axis=1)
    ST = SEL_TILE
    nb = ST // SEL_BLOCK
    kt_last = (t0 + QT - 1) // ST
    s_bufs, p_bufs = (s0_ref, s1_ref), (p0_ref, p1_ref)

    def sel_scores(kt, slot):
        k0 = pl.multiple_of(jnp.minimum(kt, kt_last) * ST, ST)
        s_bufs[slot][...] = _dot(ks_ref[0, pl.ds(k0, ST), :], qpad)

    def sel_values(kt, slot, acc, alpha):
        return acc * alpha + _dot(vst_ref[0, jnp.clip(kt, 0, kt_last)], p_bufs[slot][...])

    def sel_softmax(kt, slot, m):
        blk0 = jnp.minimum(kt, kt_last) * nb
        bias = jnp.concatenate([jnp.broadcast_to(bias_ref[pl.ds(blk0 + j, 1), :], (SEL_BLOCK, QT))
                                for j in range(nb)], axis=0)
        seen = (kt * ST + lax.broadcasted_iota(jnp.int32, (ST, QT), 0)
                <= t0 + lax.broadcasted_iota(jnp.int32, (ST, QT), 1))
        s = s_bufs[slot][...] + tile4(jnp.where(seen, bias, NEG_INF))
        m_new = jnp.maximum(m, jnp.max(s, axis=0, keepdims=True))
        p_bufs[slot][...] = jnp.exp2(s - m_new).astype(BF16)
        return m_new, jnp.exp2(m - m_new)

    def sel_pair(j, carry):
        m, acc, alpha = carry
        a = 2 * j
        m, alpha_a = sel_softmax(a, 0, m)
        sel_scores(a + 1, 1)
        acc = sel_values(a - 1, 1, acc, alpha)
        m, alpha_b = sel_softmax(a + 1, 1, m)
        sel_scores(a + 2, 0)
        acc = sel_values(a, 0, acc, alpha_a)
        return m, acc, alpha_b

    sel_scores(0, 0)
    p1_ref[...] = jnp.zeros_like(p1_ref)
    n_pairs = (kt_last + 2) // 2
    _, acc_s, alpha = lax.fori_loop(
        0, n_pairs, sel_pair,
        (jnp.full((1, NQ), NEG_INF, F32), jnp.zeros((V_ROWS, NQ), F32), jnp.ones((1, NQ), F32)))
    acc_s = sel_values(2 * n_pairs - 1, 1, acc_s, alpha)

    n_wt = (WINDOW + QT) // KT
    k0w = pl.multiple_of(jnp.maximum(t0 - WINDOW, 0), KT)
    kt_w = k0w // KT
    keys_w = kw_ref[0, pl.ds(k0w, WINDOW + QT), :]
    dw = (t0 + lax.broadcasted_iota(jnp.int32, (WINDOW + QT, QT), 1)
          - (k0w + lax.broadcasted_iota(jnp.int32, (WINDOW + QT, QT), 0)))
    bias_w = wide(jnp.where(dw >= 0, jnp.where(dw < WINDOW, 0.0, NEG_INF), NEG_INF))
    acc_w = []
    for cs in groups:
        sw = _dot(keys_w, qpad[:, cs]) + bias_w
        pw = jnp.exp2(sw - jnp.max(sw, axis=0, keepdims=True)).astype(BF16)
        a = _dot(vwt_ref[0, kt_w], pw[0:KT])
        for j in range(1, n_wt):
            a = a + _dot(vwt_ref[0, kt_w + j], pw[j * KT:(j + 1) * KT])
        acc_w.append(a)
    acc_w = jnp.concatenate(acc_w, axis=1)

    gates = _sigmoid(gt_ref[0, 0])
    grow = lambda j: jnp.concatenate([gates[j, r:r + 1, :] for r in range(R)], axis=1)
    D = HEAD_DIM
    o = (grow(0) * o_c + grow(1) * (acc_s[0:D] / acc_s[D:D + 1])
         + grow(2) * (acc_w[0:D] / acc_w[D:D + 1]))
    halves = []
    for h in range(R // 2):
        pair = jnp.concatenate([o[:, (2 * h) * QT:(2 * h + 1) * QT],
                                o[:, (2 * h + 1) * QT:(2 * h + 2) * QT]], axis=0)
        halves.append(pair.T)
    o_ref[...] = jnp.concatenate(halves, axis=1)


def _nsa_attn(qt, kcmp, vct, ks3, vst, kw3, vwt, gt, ov_t, B, T):
    G, R = NSA_KV_HEADS, NSA_GROUP
    nq = T // Q_TILE
    nk = T // KEY_TILE
    nch = kcmp.shape[1]
    n_sel = ov_t.shape[0]
    return pl.pallas_call(
        _nsa_attn_kernel,
        grid=(B, G, nq),
        in_specs=[pl.BlockSpec((1, 1, R * HEAD_DIM, Q_TILE), lambda b, g, q: (b, q, g, 0)),
                  pl.BlockSpec((1, nch, LANES), lambda b, g, q: (b, 0, 0)),
                  pl.BlockSpec((1, HEAD_DIM, nch), lambda b, g, q: (b, g, 0)),
                  pl.BlockSpec((1, T, LANES), lambda b, g, q: (b, 0, 0)),
                  pl.BlockSpec((1, T // SEL_TILE, V_ROWS, SEL_TILE), lambda b, g, q: (b, 0, g, 0)),
                  pl.BlockSpec((1, T, LANES), lambda b, g, q: (b, 0, 0)),
                  pl.BlockSpec((1, nk, V_ROWS, KEY_TILE), lambda b, g, q: (b, 0, g, 0)),
                  pl.BlockSpec((1, 1, 3, R, Q_TILE), lambda b, g, q: (b, g, 0, 0, q)),
                  pl.BlockSpec((n_sel, nch), lambda b, g, q: (0, 0))],
        out_specs=pl.BlockSpec((Q_TILE, R * HEAD_DIM), lambda b, g, q: (b * nq + q, g)),
        out_shape=jax.ShapeDtypeStruct((B * T, NSA_WIDTH), F32),
        scratch_shapes=[pltpu.VMEM((n_sel, Q_TILE), F32),
                        pltpu.VMEM((SEL_TILE, R * Q_TILE), F32), pltpu.VMEM((SEL_TILE, R * Q_TILE), F32),
                        pltpu.VMEM((SEL_TILE, R * Q_TILE), BF16), pltpu.VMEM((SEL_TILE, R * Q_TILE), BF16)],
        compiler_params=_params("arbitrary", "arbitrary", "arbitrary"),
        name="nsa_attn",
    )(qt, kcmp, vct, ks3, vst, kw3, vwt, gt, ov_t)


def _first_index_of(vals, target):
    idx = jnp.full_like(target, float(len(vals) - 1))
    for i in range(len(vals) - 2, -1, -1):
        idx = jnp.where(vals[i] == target, float(i), idx)
    return idx


def _pick(vals, idx):
    out = vals[-1]
    for i in range(len(vals) - 2, -1, -1):
        out = jnp.where(idx == float(i), vals[i], out)
    return out


def _route_rows(score, bias):
    E, G, P = N_EXPERTS, N_GROUPS, EXPERTS_PER_GROUP
    sel = score + bias
    s = [sel[e:e + 1, :] for e in range(E)]
    raw = [score[e:e + 1, :] for e in range(E)]
    grp = []
    for gi in range(G):
        a = s[gi * P:(gi + 1) * P]
        best = None
        for i in range(P):
            for j in range(i + 1, P):
                pair = a[i] + a[j]
                best = pair if best is None else jnp.maximum(best, pair)
        grp.append(best)
    gmax = functools.reduce(jnp.maximum, grp)
    g_star = _first_index_of(grp, gmax)
    v = [_pick([s[gi * P + i] for gi in range(G)], g_star) for i in range(P)]
    w = [_pick([raw[gi * P + i] for gi in range(G)], g_star) for i in range(P)]
    i1 = _first_index_of(v, functools.reduce(jnp.maximum, v))
    v2 = [jnp.where(i1 == float(i), -jnp.inf, v[i]) for i in range(P)]
    i2 = _first_index_of(v2, functools.reduce(jnp.maximum, v2))
    w1, w2 = _pick(w, i1), _pick(w, i2)
    tot = w1 + w2
    zero = jnp.zeros_like(tot)
    return jnp.concatenate([g_star * P + i1, g_star * P + i2, w1 / tot, w2 / tot, zero, zero, zero, zero],
                           axis=0)


def _merge_kernel(ys_ref, g_ref, bonus_ref, gng_ref, gnb_ref, bd_ref, yb_ref, pm_ref, x_ref, mod_ref,
                  ng_ref, wa_ref, wb_ref, wo_ref, rw_ref, rb_ref, o_x, o_h, o_route):
    m = mod_ref[pl.program_id(0)]
    bd = bd_ref[...]
    y = ys_ref[...]
    mean = _dot(y, bd, HI)
    yc = y - mean
    var = _dot(yc * yc, bd, HI)
    ya = (yc * lax.rsqrt(var + RWKV_GN_EPS) * gng_ref[...] + gnb_ref[...] + bonus_ref[...]) * g_ref[...]
    pm = pm_ref[...]
    D = x_ref.shape[1]
    mix = (_sigmoid(pm[:, 0:D]) * _dot(ya.astype(BF16), wa_ref[...])
           + _sigmoid(pm[:, D:2 * D]) * _dot(yb_ref[...].astype(BF16), wb_ref[...]))
    x = x_ref[...] + m[2:3] * _dot(mix.astype(BF16), wo_ref[...])
    o_x[...] = x
    ms = jnp.mean(x * x, axis=-1, keepdims=True)
    h = x * lax.rsqrt(ms + NORM_EPS) * ng_ref[...]
    h = h * (1.0 + m[4:5]) + m[3:4]
    o_h[...] = h
    score = _sigmoid(_dot_tb(rw_ref[...], h, HI))
    o_route[...] = _route_rows(score, rb_ref[...])


def _merge(ys, g, bonus, gn_g, gn_b, yb, pm, x2, mod, ng, wa, wb, wo, router_w, router_b, B, T, tm=256):
    N, D = x2.shape
    W = RWKV_WIDTH
    nt = T // tm
    row = lambda b, t: (b * nt + t, 0)
    full = lambda shape: pl.BlockSpec(shape, lambda b, t: (0,) * len(shape))
    return pl.pallas_call(
        _merge_kernel,
        grid=(B, nt),
        in_specs=[pl.BlockSpec((tm, W), row), pl.BlockSpec((tm, W), row), pl.BlockSpec((tm, W), row),
                  full((1, W)), full((1, W)), full((W, W)),
                  pl.BlockSpec((tm, NSA_WIDTH), row), pl.BlockSpec((tm, 2 * D), row),
                  pl.BlockSpec((tm, D), row), full((B, 6, D)), full((1, D)),
                  full((W, D)), full((NSA_WIDTH, D)), full((D, D)), full((N_EXPERTS, D)),
                  full((N_EXPERTS, 1))],
        out_specs=[pl.BlockSpec((tm, D), row), pl.BlockSpec((tm, D), row),
                   pl.BlockSpec((8, tm), lambda b, t: (0, b * nt + t))],
        out_shape=[jax.ShapeDtypeStruct((N, D), F32), jax.ShapeDtypeStruct((N, D), F32),
                   jax.ShapeDtypeStruct((8, N), F32)],
        compiler_params=_params("arbitrary", "arbitrary"),
        name="merge_out",
    )(ys, g, bonus, gn_g.reshape(1, W), gn_b.reshape(1, W), _head_block_diag(W, 1.0 / HEAD_DIM),
      yb, pm, x2, mod, ng, wa, wb, wo, router_w.T, router_b.reshape(N_EXPERTS, 1))


def _route(route, N):
    wts = route[TOP_K:2 * TOP_K].T
    NK = N * TOP_K
    e_flat = route[0:TOP_K].astype(jnp.int32).reshape(-1)
    onehot = (e_flat[:, None] == jnp.arange(N_EXPERTS, dtype=jnp.int32)[None, :]).astype(jnp.int32)
    csum = jnp.cumsum(onehot, axis=0)
    counts = csum[-1]
    rank = jnp.take_along_axis(csum, e_flat[:, None], axis=1)[:, 0] - 1
    padded = (counts + MOE_BLOCK - 1) // MOE_BLOCK * MOE_BLOCK
    pad_end = jnp.cumsum(padded)
    pad_start = pad_end - padded
    dest = pad_start[e_flat] + rank
    n_blk = -(-NK // MOE_BLOCK) + N_EXPERTS
    P = n_blk * MOE_BLOCK
    slot_assign = jnp.zeros((P,), jnp.int32).at[dest].set(jnp.arange(NK, dtype=jnp.int32))
    slot_tok = jnp.where(slot_assign >= N, slot_assign - N, slot_assign)
    blk_start = jnp.arange(n_blk, dtype=jnp.int32) * MOE_BLOCK
    blk_expert = jnp.clip(jnp.searchsorted(pad_end, blk_start, side='right'), 0, N_EXPERTS - 1).astype(jnp.int32)
    blk_valid = jnp.clip((pad_start + counts)[blk_expert] - blk_start, 0, MOE_BLOCK).astype(jnp.int32)
    return wts, slot_tok, slot_assign, blk_expert, blk_valid, n_blk


MOE_ISSUE_UNROLL = 8


def _moe_kernel(be_ref, nv_ref, tok_ref, dst_ref, h_hbm, wg_ref, wu_ref, wd_ref, o_hbm, xbuf, ybuf, sem_in,
                sem_out):
    i = pl.program_id(0)
    MB = MOE_BLOCK
    U = MOE_ISSUE_UNROLL
    base = i * MB
    nv = nv_ref[i]

    @pl.when(i == 0)
    def _():
        xbuf[...] = jnp.zeros_like(xbuf)

    def gather_row(r, priority):
        pltpu.make_async_copy(h_hbm.at[pl.ds(tok_ref[base + r], 1), :], xbuf.at[pl.ds(r, 1), :],
                              sem_in).start(priority=priority)

    def scatter_row(r, priority):
        pltpu.make_async_copy(ybuf.at[pl.ds(r, 1), :], o_hbm.at[pl.ds(dst_ref[base + r], 1), :],
                              sem_out).start(priority=priority)

    def issue_rows(row_fn):
        def group(gi, _):
            for u in range(U):
                row_fn(gi * U + u, u % 2)
            return 0

        lax.fori_loop(0, nv // U, group, 0)

        def tail(r, _):
            row_fn(r, 0)
            return 0

        lax.fori_loop((nv // U) * U, nv, tail, 0)

    def wait_rows(descriptor):
        for bit in range(MB.bit_length()):
            n = 1 << bit

            @pl.when((nv & n) != 0)
            def _():
                descriptor(n).wait()

    @pl.when(nv > 0)
    def _():
        issue_rows(gather_row)
        wait_rows(lambda n: pltpu.make_async_copy(h_hbm.at[pl.ds(0, n), :], xbuf.at[pl.ds(0, n), :], sem_in))
        x = xbuf[...].astype(BF16)
        gate = _dot(x, wg_ref[0])
        up = _dot(x, wu_ref[0])
        act = (gate * _sigmoid(gate) * up).astype(BF16)
        ybuf[...] = _dot(act, wd_ref[0])
        issue_rows(scatter_row)
        wait_rows(lambda n: pltpu.make_async_copy(ybuf.at[pl.ds(0, n), :], o_hbm.at[pl.ds(0, n), :], sem_out))


def _moe(h, slot_tok, slot_assign, blk_expert, blk_valid, n_blk, wg, wu, wd):
    N, D = h.shape
    DE = wg.shape[2]
    wmap = lambda i, be, nv, tok, dst: (be[i], 0, 0)
    grid_spec = pltpu.PrefetchScalarGridSpec(
        num_scalar_prefetch=4,
        grid=(n_blk,),
        in_specs=[pl.BlockSpec(memory_space=pl.ANY), pl.BlockSpec((1, D, DE), wmap),
                  pl.BlockSpec((1, D, DE), wmap), pl.BlockSpec((1, DE, D), wmap)],
        out_specs=pl.BlockSpec(memory_space=pl.ANY),
        scratch_shapes=[pltpu.VMEM((MOE_BLOCK, D), F32), pltpu.VMEM((MOE_BLOCK, D), F32),
                        pltpu.SemaphoreType.DMA(()), pltpu.SemaphoreType.DMA(())],
    )
    return pl.pallas_call(
        _moe_kernel,
        grid_spec=grid_spec,
        out_shape=jax.ShapeDtypeStruct((TOP_K * N, D), F32),
        compiler_params=_params("arbitrary"),
        name="moe_experts",
    )(blk_expert, blk_valid, slot_tok, slot_assign, h, wg, wu, wd)


def _final_kernel(x_ref, y0_ref, y1_ref, w_ref, mod_ref, o_ref):
    m = mod_ref[pl.program_id(0)]
    w = w_ref[...]
    o_ref[...] = x_ref[...] + m[5:6] * (w[:, 0:1] * y0_ref[...] + w[:, 1:2] * y1_ref[...])


def _final(x2, ybuf, wts, mod, B, T, tm=512):
    N, D = x2.shape
    nt = T // tm
    row = lambda b, t: (b * nt + t, 0)
    return pl.pallas_call(
        _final_kernel,
        grid=(B, nt),
        in_specs=[pl.BlockSpec((tm, D), row), pl.BlockSpec((tm, D), row),
                  pl.BlockSpec((tm, D), lambda b, t: (N // tm + b * nt + t, 0)),
                  pl.BlockSpec((tm, TOP_K), row), pl.BlockSpec((B, 6, D), lambda b, t: (0, 0, 0))],
        out_specs=pl.BlockSpec((tm, D), row),
        out_shape=jax.ShapeDtypeStruct((N, D), F32),
        compiler_params=_params("arbitrary", "arbitrary"),
        name="moe_combine",
    )(x2, ybuf, ybuf, wts, mod)


def _overlap_t(n_sel, n_cmp_pad):
    ci = jnp.arange(n_cmp_pad)[None, :] * CMP_STRIDE
    sj = jnp.arange(n_sel)[:, None] * SEL_BLOCK
    ov = (ci <= sj + SEL_BLOCK - 1) & (ci + CMP_BLOCK - 1 >= sj) & (jnp.arange(n_cmp_pad)[None, :] < n_cmp_pad - 1)
    return ov.astype(F32)


def kernel(x, c, w_ada, b_ada, norm_g, w_in, b_in, rwkv_mu, rwkv_w0, rwkv_w2, rwkv_a0, rwkv_a2, rwkv_g2,
           rwkv_k_k, rwkv_k_a, rwkv_r_k, rwkv_gn_g, rwkv_gn_b, qk_norm_g, cmp_pos, cmp_w1, cmp_w2,
           w_up_rwkv, w_up_nsa, w_out, router_w, router_b, exp_w_gate, exp_w_up, exp_w_down):
    B, T, D = x.shape
    L = w_ada.shape[0]
    N = B * T
    mods = _ada(c, w_ada, b_ada)
    tables = _rope_tables(jnp.arange(T, dtype=jnp.int32))
    nch = T // CMP_STRIDE
    tables_cmp = _rope_tables(jnp.arange(nch, dtype=jnp.int32) * CMP_STRIDE + CMP_BLOCK - 1)
    ov_t = _overlap_t(T // SEL_BLOCK, nch)
    n_gate = NSA_GATE_COLS
    x2 = x.reshape(N, D)
    for l in range(L):
        g0 = _SEG_KV[1] + n_gate
        w_pad = jnp.concatenate([w_in[l][:, :g0], jnp.zeros((D, GATE_PAD - n_gate), F32), w_in[l][:, g0:]],
                                axis=1).astype(BF16)
        b_pad = jnp.concatenate([b_in[l][:g0], jnp.zeros((GATE_PAD - n_gate,), F32), b_in[l][g0:]]).reshape(1, -1)
        p_rw, p_q, p_kv, p_gate, p_merge = _inproj(x2, mods[l], norm_g[l, 0].reshape(1, D), w_pad, b_pad, B, T)
        r, k, v, al, bb, ld, g, bonus = _rwkv_pre(p_rw, rwkv_mu[l], rwkv_w0[l], rwkv_w2[l], rwkv_a0[l],
                                                  rwkv_a2[l], rwkv_g2[l], rwkv_k_k[l], rwkv_k_a[l],
                                                  rwkv_r_k[l], B, T)
        ys = _rwkv_scan(r, k, v, al, bb, ld, B, T)
        qt, ks, kw, vst, vwt = _nsa_prep(p_q, p_kv, tables, qk_norm_g[l], B, T)
        kv3 = p_kv.reshape(B, T, KV_COLS)
        kcmp = _nsa_cmp(kv3, 0, cmp_pos[l], cmp_w1[l], cmp_w2[l], qk_norm_g[l, 1], tables_cmp)
        vct = _nsa_cmp(kv3, 1, cmp_pos[l], cmp_w1[l], cmp_w2[l], None, None)
        gt = p_gate[:, :n_gate].reshape(B, T, NSA_KV_HEADS, NSA_GROUP, 3).transpose(0, 2, 4, 3, 1)
        yb = _nsa_attn(qt, kcmp, vct, ks.reshape(B, T, LANES), vst, kw.reshape(B, T, LANES), vwt, gt, ov_t,
                       B, T)
        x2, h2, route = _merge(ys, g, bonus, rwkv_gn_g[l], rwkv_gn_b[l], yb, p_merge, x2, mods[l],
                               norm_g[l, 1].reshape(1, D), w_up_rwkv[l].astype(BF16),
                               w_up_nsa[l].astype(BF16), w_out[l].astype(BF16), router_w, router_b, B, T)
        wts, slot_tok, slot_assign, blk_expert, blk_valid, n_blk = _route(route, N)
        ybuf = _moe(h2, slot_tok, slot_assign, blk_expert, blk_valid, n_blk, exp_w_gate[l].astype(BF16),
                    exp_w_up[l].astype(BF16), exp_w_down[l].astype(BF16))
        x2 = _final(x2, ybuf, wts, mods[l], B, T)
    return x2.reshape(B, T, D)
```

```python
import functools
import math

import jax
import jax.numpy as jnp
from jax import lax
from jax.experimental import pallas as pl
from jax.experimental.pallas import tpu as pltpu

F32 = jnp.float32
BF16 = jnp.bfloat16
HI = lax.Precision.HIGHEST

D_MODEL = 1024
RWKV_HEADS = 8
HEAD_DIM = 64
RWKV_WIDTH = RWKV_HEADS * HEAD_DIM
DECAY_LORA = 64
ICLR_LORA = 64
GATE_LORA = 128
RWKV_GN_EPS = 64e-5
RWKV_COLS = 3 * RWKV_WIDTH + DECAY_LORA + ICLR_LORA + GATE_LORA

NSA_Q_HEADS = 8
NSA_KV_HEADS = 2
NSA_GROUP = NSA_Q_HEADS // NSA_KV_HEADS
NSA_WIDTH = NSA_Q_HEADS * HEAD_DIM
CMP_STRIDE = 16
CMP_BLOCK = 2 * CMP_STRIDE
CMP_HIDDEN = 256
SEL_BLOCK = 64
SEL_SHIFT = 6
SEL_TOPK = 16
WINDOW = 512
FORCE_SCORE = 1e4
NEG_INF = -1e30
ROPE_THETA = 500000.0
ROPE_DIM = HEAD_DIM // 4
KV_COLS = 6 * NSA_KV_HEADS * HEAD_DIM
NSA_GATE_COLS = 3 * NSA_Q_HEADS
GATE_PAD = 128

N_EXPERTS = 16
N_GROUPS = 4
EXPERTS_PER_GROUP = N_EXPERTS // N_GROUPS
TOP_K = 2
D_EXPERT = 512
MOE_BLOCK = 256
NORM_EPS = 1e-6

LANES = 128
CHUNK = 64
KEY_TILE = 128
SEL_TILE = 512
V_ROWS = 80
Q_SCALE = HEAD_DIM ** -0.5 * math.log2(math.e)
Q_TILE = 128
F32_TINY = float(jnp.finfo(jnp.float32).tiny)

_SEG_RW = (0, RWKV_COLS)
_SEG_Q = (_SEG_RW[1], _SEG_RW[1] + NSA_WIDTH)
_SEG_KV = (_SEG_Q[1], _SEG_Q[1] + KV_COLS)
_SEG_GATE = (_SEG_KV[1], _SEG_KV[1] + GATE_PAD)
_SEG_MERGE = (_SEG_GATE[1], _SEG_GATE[1] + 2 * D_MODEL)
IN_COLS_PAD = _SEG_MERGE[1]

_VMEM_LIMIT = 56 * 1024 * 1024


def _dot(a, b, precision=None):
    return jnp.dot(a, b, preferred_element_type=F32, precision=precision)


def _dot_tb(a, b, precision=None):
    return lax.dot_general(a, b, (((1,), (1,)), ((), ())), preferred_element_type=F32,
                           precision=precision)


def _dot_ta(a, b, precision=None):
    return lax.dot_general(a, b, (((0,), (0,)), ((), ())), preferred_element_type=F32,
                           precision=precision)


def _split_bf16(x, terms):
    parts = []
    for _ in range(terms - 1):
        parts.append(x.astype(BF16))
        x = x - parts[-1].astype(F32)
    parts.append(x.astype(BF16))
    return parts


def _dot_split_lhs(x, w_bf, terms=2):
    return functools.reduce(jnp.add, [_dot(p, w_bf) for p in _split_bf16(x, terms)])


def _dot_split_rhs(w_bf, x, terms=2):
    return functools.reduce(jnp.add, [_dot(w_bf, p) for p in _split_bf16(x, terms)])


def _dot_3pass(x, w_hl_ref):
    x_hi, x_lo = _split_bf16(x, 2)
    w_hi = w_hl_ref[0]
    return _dot(x_hi, w_hi) + _dot(x_lo, w_hi) + _dot(x_hi, w_hl_ref[1])


def _hi_lo(w):
    hi = w.astype(BF16)
    return jnp.stack([hi, (w - hi.astype(F32)).astype(BF16)])


def _params(*sem):
    return pltpu.CompilerParams(dimension_semantics=sem, vmem_limit_bytes=_VMEM_LIMIT)


def _sigmoid(x):
    return 1.0 / (1.0 + jnp.exp(-x))


def _ada_kernel(c_ref, w_ref, b_ref, o_ref):
    c = c_ref[...]
    s = c * _sigmoid(c)
    o_ref[0] = _dot(s, w_ref[0], HI) + b_ref[0]


def _ada(c, w_ada, b_ada):
    L, D, D6 = w_ada.shape
    B = c.shape[0]
    rows = 8
    cp = jnp.zeros((rows, D), F32).at[:B].set(c)
    tn = 1536
    out = pl.pallas_call(
        _ada_kernel,
        grid=(L, D6 // tn),
        in_specs=[pl.BlockSpec((rows, D), lambda l, j: (0, 0)),
                  pl.BlockSpec((1, D, tn), lambda l, j: (l, 0, j)),
                  pl.BlockSpec((1, 1, tn), lambda l, j: (l, 0, j))],
        out_specs=pl.BlockSpec((1, rows, tn), lambda l, j: (l, 0, j)),
        out_shape=jax.ShapeDtypeStruct((L, rows, D6), F32),
        compiler_params=_params("arbitrary", "arbitrary"),
        name="ada_mod",
    )(cp, w_ada, b_ada.reshape(L, 1, D6))
    return out[:, :B].reshape(L, B, 6, D)


def _inproj_kernel(x_ref, mod_ref, g_ref, w_ref, b_ref, o_rw, o_q, o_kv, o_gate, o_merge):
    m = mod_ref[pl.program_id(0)]
    x = x_ref[...]
    ms = jnp.mean(x * x, axis=-1, keepdims=True)
    h = x * lax.rsqrt(ms + NORM_EPS) * g_ref[...]
    h = h * (1.0 + m[1:2]) + m[0:1]
    hb = h.astype(BF16)
    for o, (a, e) in ((o_rw, _SEG_RW), (o_q, _SEG_Q), (o_kv, _SEG_KV), (o_gate, _SEG_GATE),
                      (o_merge, _SEG_MERGE)):
        o[...] = _dot(hb, w_ref[:, a:e]) + b_ref[:, a:e]


def _inproj(x2, mod, g, w_pad, b_pad, B, T, tm=256):
    N, D = x2.shape
    nt = T // tm
    row = lambda b, t: (b * nt + t, 0)
    widths = [e - a for a, e in (_SEG_RW, _SEG_Q, _SEG_KV, _SEG_GATE, _SEG_MERGE)]
    return pl.pallas_call(
        _inproj_kernel,
        grid=(B, nt),
        in_specs=[pl.BlockSpec((tm, D), row),
                  pl.BlockSpec((B, 6, D), lambda b, t: (0, 0, 0)),
                  pl.BlockSpec((1, D), lambda b, t: (0, 0)),
                  pl.BlockSpec((D, IN_COLS_PAD), lambda b, t: (0, 0)),
                  pl.BlockSpec((1, IN_COLS_PAD), lambda b, t: (0, 0))],
        out_specs=[pl.BlockSpec((tm, w), row) for w in widths],
        out_shape=[jax.ShapeDtypeStruct((N, w), F32) for w in widths],
        compiler_params=_params("arbitrary", "arbitrary"),
        name="in_proj",
    )(x2, mod, g, w_pad, b_pad)


def _rwkv_pre_kernel(p_ref, mu_ref, w0_ref, w2_ref, a0_ref, a2_ref, g2_ref, kk_ref, ka_ref, rk_ref,
                     bd_ref, o_r, o_k, o_v, o_al, o_b, o_ld, o_g, o_bonus, carry_ref):
    W = RWKV_WIDTH

    @pl.when(pl.program_id(1) == 0)
    def _():
        carry_ref[...] = jnp.zeros_like(carry_ref)

    p = p_ref[...]
    ts = p.shape[0]
    rows = lax.broadcasted_iota(jnp.int32, p.shape, 0)
    shifted = jnp.where(rows == 0, carry_ref[0:1, :], pltpu.roll(p, 1, 0))
    carry_ref[0:1, :] = p[ts - 1:ts, :]
    pm = p + (shifted - p) * mu_ref[...]
    r = pm[:, 0:W]
    k = pm[:, W:2 * W]
    v = pm[:, 2 * W:3 * W]
    wa = pm[:, 3 * W:3 * W + DECAY_LORA + ICLR_LORA]
    gl = pm[:, 3 * W + DECAY_LORA + ICLR_LORA:]
    xw = w0_ref[...] + _dot_3pass(jnp.tanh(wa), w2_ref)
    ld = -math.exp(-0.5) * _sigmoid(xw)
    a = _sigmoid(a0_ref[...] + _dot_3pass(wa, a2_ref))
    g = _dot_3pass(_sigmoid(gl), g2_ref)
    bd = bd_ref[...]
    kk = k * kk_ref[...]
    nrm = jnp.sqrt(_dot_split_lhs(kk * kk, bd))
    kk = kk / jnp.maximum(nrm, 1e-12)
    k2 = k * (1.0 + (a - 1.0) * ka_ref[...])
    bonus = _dot_split_lhs(r * k2 * rk_ref[...], bd) * v
    o_r[...] = r
    o_k[...] = k2
    o_v[...] = v
    o_al[...] = kk
    o_b[...] = -kk * a
    o_ld[...] = ld
    o_g[...] = g
    o_bonus[...] = bonus


def _head_block_diag(width, scale=1.0):
    i = jnp.arange(width) // HEAD_DIM
    return ((i[:, None] == i[None, :]).astype(F32) * scale).astype(BF16)


def _rwkv_pre(p_rw, mu, w0, w2, a0, a2, g2, k_k, k_a, r_k, B, T, ts=256):
    N = p_rw.shape[0]
    W = RWKV_WIDTH
    nt = T // ts
    row = lambda b, t: (b * nt + t, 0)
    zl = jnp.zeros((DECAY_LORA, W), F32)
    w2p = jnp.concatenate([w2, zl], axis=0)
    a2p = jnp.concatenate([zl, a2], axis=0)
    full = lambda shape: pl.BlockSpec(shape, lambda b, t: (0,) * len(shape))
    vec = lambda z: z.reshape(1, -1)
    return pl.pallas_call(
        _rwkv_pre_kernel,
        grid=(B, nt),
        in_specs=[pl.BlockSpec((ts, RWKV_COLS), row), full((1, RWKV_COLS)), full((1, W)),
                  full((2, 2 * DECAY_LORA, W)), full((1, W)), full((2, 2 * DECAY_LORA, W)),
                  full((2, GATE_LORA, W)), full((1, W)), full((1, W)), full((1, W)), full((W, W))],
        out_specs=[pl.BlockSpec((ts, W), row)] * 8,
        out_shape=[jax.ShapeDtypeStruct((N, W), F32)] * 8,
        scratch_shapes=[pltpu.VMEM((8, RWKV_COLS), F32)],
        compiler_params=_params("arbitrary", "arbitrary"),
        name="rwkv_pre",
    )(p_rw, vec(mu), vec(w0), _hi_lo(w2p), vec(a0), _hi_lo(a2p), _hi_lo(g2), vec(k_k), vec(k_a), vec(r_k),
      _head_block_diag(W))


def _bf(x):
    return x.astype(BF16)


def _scan_local(chunks, eye, strict, incl, m0, m1):
    C = CHUNK
    n = range(len(chunks))
    st = lambda z: jnp.concatenate([z * m0, z * m1], axis=0)
    zero = jnp.zeros((2 * C, 2 * C), F32)
    at_b, rt_s, vs, vs_b, lhs_a, rhs_a, bk_t, dcol = [], [], [], [], [], [], [], []
    for r, k, v, al, bb, ld, cum in chunks:
        tot = cum[C - 1:C, :]
        dinv = jnp.exp(-cum)
        dend = jnp.exp(tot - cum)
        at_b.append(_bf(st(al * jnp.exp(cum - ld))))
        rt_s.append(st(r * jnp.exp(cum)))
        vs.append(st(v))
        vs_b.append(_bf(vs[-1]))
        lhs_a.append(jnp.concatenate([at_b[-1], _bf(rt_s[-1])], axis=0))
        rhs_a.append(_bf(jnp.concatenate([st(bb * dinv), st(k * dinv)], axis=0)))
        bk_t.append(_bf(jnp.concatenate([st(bb * dend).T, st(k * dend).T], axis=1)))
        dcol.append(jnp.sum(eye * jnp.exp(tot), axis=1, keepdims=True))
    A = [_dot_tb(lhs_a[i], rhs_a[i]) for i in n]
    a_ab = [jnp.where(strict, A[i][0:2 * C, 0:2 * C], zero) for i in n]
    a_ak = [_bf(jnp.where(strict, A[i][0:2 * C, 2 * C:4 * C], zero)) for i in n]
    a_r = [_bf(jnp.concatenate([jnp.where(incl, A[i][2 * C:4 * C, 0:2 * C], zero),
                                jnp.where(incl, A[i][2 * C:4 * C, 2 * C:4 * C], zero)], axis=1)) for i in n]
    akv = [_bf(_dot(a_ak[i], vs_b[i])) for i in n]
    pw = a_ab
    tinv = [eye + pw[i] for i in n]
    for _ in range(5):
        pw_b = [_bf(pw[i]) for i in n]
        pw = [_dot(pw_b[i], pw_b[i]) for i in n]
        tinv = [tinv[i] + _dot(_bf(pw[i]), _bf(tinv[i])) for i in n]
    X = [_dot(_bf(tinv[i]), jnp.concatenate([at_b[i], akv[i]], axis=1)) for i in n]
    w_b = [_bf(X[i][:, 0:LANES]) for i in n]
    uv0 = [jnp.concatenate([_bf(X[i][:, LANES:2 * LANES]), vs_b[i]], axis=0) for i in n]
    m_h = [_bf(_dot(bk_t[i][:, 0:2 * C], w_b[i])) for i in n]
    g_h = [_dot(bk_t[i], uv0[i]) for i in n]
    q_h = [_bf(rt_s[i] + _dot(a_r[i][:, 0:2 * C], w_b[i])) for i in n]
    y0 = [_dot(a_r[i], uv0[i]) for i in n]
    return [(m_h[i], g_h[i], dcol[i], q_h[i], y0[i]) for i in n]


def _scan_steps(local, H):
    C = CHUNK
    ys = []
    for m_h, g_h, dcol, q_h, y0 in local:
        h_b = _bf(H)
        Y = _dot(q_h, h_b) + y0
        ys.append(Y[0:C] + Y[C:2 * C])
        H = dcol * H + _dot(m_h, h_b) + g_h
    return ys, H


def _rwkv_scan_kernel(r_ref, k_ref, v_ref, al_ref, b_ref, ld_ref, o_ref, h_ref):
    C = CHUNK
    tc = r_ref.shape[0]

    @pl.when(pl.program_id(2) == 0)
    def _():
        h_ref[...] = jnp.zeros_like(h_ref)

    tri = jnp.where(lax.broadcasted_iota(jnp.int32, (C, C), 1) <= lax.broadcasted_iota(jnp.int32, (C, C), 0),
                    1.0, 0.0).astype(BF16)
    r2 = lax.broadcasted_iota(jnp.int32, (2 * C, 2 * C), 0)
    c2 = lax.broadcasted_iota(jnp.int32, (2 * C, 2 * C), 1)
    eye = (r2 == c2).astype(F32)
    strict = (c2 & (C - 1)) < (r2 & (C - 1))
    incl = (c2 & (C - 1)) <= (r2 & (C - 1))
    lane = lax.broadcasted_iota(jnp.int32, (C, LANES), 1)
    m0 = (lane < HEAD_DIM).astype(F32)
    m1 = 1.0 - m0
    nc = tc // C
    cum = _dot_split_rhs(tri, jnp.concatenate([ld_ref[c * C:(c + 1) * C, :] for c in range(nc)], axis=1), 3)
    chunks = []
    for c in range(nc):
        sl = slice(c * C, (c + 1) * C)
        chunks.append((r_ref[sl, :], k_ref[sl, :], v_ref[sl, :], al_ref[sl, :], b_ref[sl, :], ld_ref[sl, :],
                       cum[:, c * LANES:(c + 1) * LANES]))
    ys, H = _scan_steps(_scan_local(chunks, eye, strict, incl, m0, m1), h_ref[...])
    for c in range(nc):
        o_ref[c * C:(c + 1) * C, :] = ys[c]
    h_ref[...] = H


def _rwkv_scan(r, k, v, al, bb, ld, B, T, tc=512):
    N, W = r.shape
    nt = T // tc
    spec = pl.BlockSpec((tc, LANES), lambda b, h, t: (b * nt + t, h))
    return pl.pallas_call(
        _rwkv_scan_kernel,
        grid=(B, W // LANES, nt),
        in_specs=[spec] * 6,
        out_specs=spec,
        out_shape=jax.ShapeDtypeStruct((N, W), F32),
        scratch_shapes=[pltpu.VMEM((LANES, LANES), F32)],
        compiler_params=_params("arbitrary", "arbitrary", "arbitrary"),
        name="rwkv_scan",
    )(r, k, v, al, bb, ld)


def _rope_tables(pos):
    half = ROPE_DIM // 2
    inv = jnp.power(ROPE_THETA, -jnp.arange(half, dtype=F32) * 2.0 / ROPE_DIM)
    ang = pos.astype(F32)[:, None] * inv[None, :]
    cos, sin = jnp.cos(ang), jnp.sin(ang)
    n = pos.shape[0]
    rest = HEAD_DIM - ROPE_DIM
    c = jnp.concatenate([cos, cos, jnp.ones((n, rest), F32)], axis=1)
    s_dn = jnp.concatenate([-sin, jnp.zeros((n, half + rest), F32)], axis=1)
    s_up = jnp.concatenate([jnp.zeros((n, half), F32), sin, jnp.zeros((n, rest), F32)], axis=1)
    rep = LANES // HEAD_DIM
    return jnp.tile(c, (1, rep)), jnp.tile(s_dn, (1, rep)), jnp.tile(s_up, (1, rep))


def _norm_rope(x, bd, g, c, s_dn, s_up):
    width = x.shape[1]
    half = ROPE_DIM // 2
    rep = width // LANES
    tile = (lambda z: jnp.concatenate([z] * rep, axis=1)) if rep > 1 else (lambda z: z)
    ms = _dot_split_lhs(x * x, bd)
    xn = x * lax.rsqrt(ms + NORM_EPS) * g
    return (xn * tile(c) + pltpu.roll(xn, width - half, 1) * tile(s_dn)
            + pltpu.roll(xn, half, 1) * tile(s_up))


def _nsa_prep_kernel(q_ref, kv_ref, c_ref, sd_ref, su_ref, gq_ref, gs_ref, gw_ref, bdq_ref, bdk_ref,
                     o_qt, o_ks, o_kw, o_vst, o_vwt):
    c, sd, su = c_ref[...], sd_ref[...], su_ref[...]
    q = _norm_rope(q_ref[...], bdq_ref[...], gq_ref[...], c, sd, su) * Q_SCALE
    qt = q.T
    kv = kv_ref[...]
    bdk = bdk_ref[...]
    o_ks[...] = _norm_rope(kv[:, 2 * LANES:3 * LANES], bdk, gs_ref[...], c, sd, su).astype(BF16)
    o_kw[...] = _norm_rope(kv[:, 4 * LANES:5 * LANES], bdk, gw_ref[...], c, sd, su).astype(BF16)
    ones_rows = jnp.where(lax.broadcasted_iota(jnp.int32, (V_ROWS - HEAD_DIM, q.shape[0]), 0) == 0, 1.0, 0.0)

    def values_t(x):
        xt = x.T
        return jnp.concatenate([xt[0:HEAD_DIM], ones_rows, xt[HEAD_DIM:2 * HEAD_DIM], ones_rows], axis=0)

    vst = values_t(kv[:, 3 * LANES:4 * LANES])
    vwt = values_t(kv[:, 5 * LANES:6 * LANES])
    for j in range(q.shape[0] // KEY_TILE):
        sl = slice(j * KEY_TILE, (j + 1) * KEY_TILE)
        o_qt[0, j] = qt[:, sl].astype(BF16)
        o_vwt[0, j] = vwt[:, sl].astype(BF16)
    for j in range(q.shape[0] // SEL_TILE):
        o_vst[0, j] = vst[:, j * SEL_TILE:(j + 1) * SEL_TILE].astype(BF16)


def _nsa_prep(q, kv, tables, qk_g, B, T, ts=512):
    N = q.shape[0]
    nt = T // ts
    nk = ts // KEY_TILE
    ns = ts // SEL_TILE
    row = lambda b, t: (b * nt + t, 0)
    full = lambda shape: pl.BlockSpec(shape, lambda b, t: (0,) * len(shape))
    tab = pl.BlockSpec((ts, LANES), lambda b, t: (t, 0))
    gq = jnp.tile(qk_g[0], NSA_Q_HEADS).reshape(1, NSA_WIDTH)
    gs = jnp.tile(qk_g[2], NSA_KV_HEADS).reshape(1, LANES)
    gw = jnp.tile(qk_g[3], NSA_KV_HEADS).reshape(1, LANES)
    tiled = lambda rows: pl.BlockSpec((1, nk, rows, KEY_TILE), lambda b, t: (b, t, 0, 0))
    return pl.pallas_call(
        _nsa_prep_kernel,
        grid=(B, nt),
        in_specs=[pl.BlockSpec((ts, NSA_WIDTH), row), pl.BlockSpec((ts, KV_COLS), row), tab, tab, tab,
                  full((1, NSA_WIDTH)), full((1, LANES)), full((1, LANES)),
                  full((NSA_WIDTH, NSA_WIDTH)), full((LANES, LANES))],
        out_specs=[tiled(NSA_WIDTH), pl.BlockSpec((ts, LANES), row), pl.BlockSpec((ts, LANES), row),
                   pl.BlockSpec((1, ns, NSA_KV_HEADS * V_ROWS, SEL_TILE), lambda b, t: (b, t, 0, 0)),
                   tiled(NSA_KV_HEADS * V_ROWS)],
        out_shape=[jax.ShapeDtypeStruct((B, T // KEY_TILE, NSA_WIDTH, KEY_TILE), BF16),
                   jax.ShapeDtypeStruct((N, LANES), BF16), jax.ShapeDtypeStruct((N, LANES), BF16),
                   jax.ShapeDtypeStruct((B, T // SEL_TILE, NSA_KV_HEADS * V_ROWS, SEL_TILE), BF16),
                   jax.ShapeDtypeStruct((B, T // KEY_TILE, NSA_KV_HEADS * V_ROWS, KEY_TILE), BF16)],
        compiler_params=_params("arbitrary", "arbitrary"),
        name="nsa_prep",
    )(q, kv, *tables, gq, gs, gw, _head_block_diag(NSA_WIDTH, 1.0 / HEAD_DIM),
      _head_block_diag(LANES, 1.0 / HEAD_DIM))


def _gelu_tanh(x):
    return 0.5 * x * (1.0 + jnp.tanh(0.7978845608028654 * (x + 0.044715 * x * x * x)))


def _nsa_cmp_kernel(x_ref, pos_ref, w1_ref, w2_ref, *rest, is_key):
    if is_key:
        g_ref, c_ref, sd_ref, su_ref, bd_ref, o_ref, xs_ref = rest
    else:
        o_ref, xs_ref = rest
    nch = xs_ref.shape[0]
    S = CMP_STRIDE
    for j in range(S):
        xs_ref[:, j * LANES:(j + 1) * LANES] = x_ref[0, pl.ds(j, nch, stride=S), :]
    xs = xs_ref[...]
    first = _dot((xs + pos_ref[0:1, :]).astype(BF16), w1_ref[0])
    second = _dot((xs + pos_ref[1:2, :]).astype(BF16), w1_ref[1])
    hid = first + pltpu.roll(second, nch - 1, 0)
    out = _dot(_gelu_tanh(hid).astype(BF16), w2_ref[...])
    rows = lax.broadcasted_iota(jnp.int32, out.shape, 0)
    if is_key:
        out = _norm_rope(out, bd_ref[...], g_ref[...], c_ref[...], sd_ref[...], su_ref[...])
        o_ref[0] = jnp.where(rows < nch - 1, out, 0.0).astype(BF16)
    else:
        o_ref[0] = jnp.where(rows < nch - 1, out, 0.0).T.astype(BF16)


def _nsa_cmp(kv3, which, cmp_pos, cmp_w1, cmp_w2, g_k, tables_cmp):
    B, T, _ = kv3.shape
    S = CMP_STRIDE
    nch = T // S
    is_key = which == 0
    eye2 = jnp.eye(NSA_KV_HEADS, dtype=F32)
    w1 = cmp_w1[which].reshape(CMP_BLOCK, HEAD_DIM, CMP_HIDDEN)
    w1 = jnp.einsum('jdh,ge->jgdeh', w1, eye2).reshape(2, S * LANES, NSA_KV_HEADS * CMP_HIDDEN)
    w2 = jnp.einsum('hd,ge->ghed', cmp_w2[which], eye2).reshape(NSA_KV_HEADS * CMP_HIDDEN, LANES)
    pos = jnp.tile(cmp_pos[which].reshape(2, S, 1, HEAD_DIM), (1, 1, NSA_KV_HEADS, 1)).reshape(2, S * LANES)
    full = lambda shape: pl.BlockSpec(shape, lambda b: (0,) * len(shape))
    in_specs = [pl.BlockSpec((1, T, LANES), lambda b: (b, 0, which)), full(pos.shape), full(w1.shape),
                full(w2.shape)]
    args = [kv3, pos, w1.astype(BF16), w2.astype(BF16)]
    if is_key:
        in_specs += [full((1, LANES)), full((nch, LANES)), full((nch, LANES)), full((nch, LANES)),
                     full((LANES, LANES))]
        args += [jnp.tile(g_k, NSA_KV_HEADS).reshape(1, LANES), *tables_cmp,
                 _head_block_diag(LANES, 1.0 / HEAD_DIM)]
        out_spec = pl.BlockSpec((1, nch, LANES), lambda b: (b, 0, 0))
        out_shape = jax.ShapeDtypeStruct((B, nch, LANES), BF16)
    else:
        out_spec = pl.BlockSpec((1, LANES, nch), lambda b: (b, 0, 0))
        out_shape = jax.ShapeDtypeStruct((B, LANES, nch), BF16)
    return pl.pallas_call(
        functools.partial(_nsa_cmp_kernel, is_key=is_key),
        grid=(B,),
        in_specs=in_specs,
        out_specs=out_spec,
        out_shape=out_shape,
        scratch_shapes=[pltpu.VMEM((nch, S * LANES), F32)],
        compiler_params=_params("arbitrary"),
        name="nsa_cmp_k" if is_key else "nsa_cmp_v",
    )(*args)


def _nsa_attn_kernel(qt_ref, kc_ref, vct_ref, ks_ref, vst_ref, kw_ref, vwt_ref, gt_ref, ov_ref, o_ref,
                     bias_ref, s0_ref, s1_ref, p0_ref, p1_ref):
    g = pl.program_id(1)
    qb = pl.program_id(2)
    R = NSA_GROUP
    QT = Q_TILE
    KT = KEY_TILE
    NQ = R * QT
    t0 = qb * QT
    n_cmp_pad = kc_ref.shape[1]
    n_sel = ov_ref.shape[0]

    q_g = jnp.concatenate([qt_ref[0, 0, r * HEAD_DIM:(r + 1) * HEAD_DIM, :] for r in range(R)], axis=1)
    q2 = jnp.concatenate([q_g, q_g], axis=0)
    row_grp = lax.broadcasted_iota(jnp.int32, q2.shape, 0) // HEAD_DIM
    qpad = jnp.where(row_grp == g, q2, jnp.zeros_like(q2))

    tq_row = t0 + (lax.broadcasted_iota(jnp.int32, (1, NQ), 1) & (QT - 1))
    tile4 = lambda z: jnp.concatenate([z] * R, axis=1)

    sc = _dot(kc_ref[0], qpad)
    n_i = lax.broadcasted_iota(jnp.int32, (n_cmp_pad, 1), 0)
    cend = jnp.where(n_i < n_cmp_pad - 1, n_i * CMP_STRIDE + (CMP_BLOCK - 1), jnp.int32(2 ** 30))
    cvalid = cend <= tq_row
    sc = jnp.where(cvalid, sc, NEG_INF)
    mc = jnp.max(sc, axis=0, keepdims=True)
    ec = jnp.where(cvalid, jnp.exp2(sc - mc), 0.0)
    pc = ec / jnp.maximum(jnp.sum(ec, axis=0, keepdims=True), F32_TINY)
    o_c = _dot(vct_ref[0], pc.astype(BF16))
    pc_sum = pc[:, 0:QT]
    for r in range(1, R):
        pc_sum = pc_sum + pc[:, r * QT:(r + 1) * QT]
    imp = _dot_split_rhs(ov_ref[...], pc_sum)

    ji = lax.broadcasted_iota(jnp.int32, (n_sel, QT), 0)
    jf = ji.astype(F32)
    tq_sel = t0 + lax.broadcasted_iota(jnp.int32, (n_sel, QT), 1)
    cur = tq_sel >> SEL_SHIFT
    forced = (ji == 0) | (ji == cur) | (ji == cur - 1)
    valid = ji * SEL_BLOCK <= tq_sel
    score = jnp.where(valid, jnp.where(forced, FORCE_SCORE, imp), -1.0)
    sel = jnp.zeros((n_sel, QT), F32)
    for _ in range(min(SEL_TOPK, n_sel)):
        mx = jnp.max(score, axis=0, keepdims=True)
        jmin = jnp.min(jnp.where(score == mx, jf, 1e9), axis=0, keepdims=True)
        hit = jf == jmin
        sel = jnp.where(hit, 1.0, sel)
        score = jnp.where(hit, -3e38, score)
    bias_ref[...] = (sel - 1.0) * (-NEG_INF)

    CW = 2 * QT
    groups = [slice(c * CW, (c + 1) * CW) for c in range(NQ // CW)]
    wide = lambda z: jnp.concatenate([z] * (CW // QT), axis=1)
    ST = SEL_TILE
    nb = ST // SEL_BLOCK
    kt_last = (t0 + QT - 1) // ST
    s_bufs, p_bufs = (s0_ref, s1_ref), (p0_ref, p1_ref)

    def sel_scores(kt, slot):
        k0 = pl.multiple_of(jnp.minimum(kt, kt_last) * ST, ST)
        s_bufs[slot][...] = _dot(ks_ref[0, pl.ds(k0, ST), :], qpad)

    def sel_values(kt, slot, acc, alpha):
        return acc * alpha + _dot(vst_ref[0, jnp.clip(kt, 0, kt_last)], p_bufs[slot][...])

    def sel_softmax(kt, slot, m):
        blk0 = jnp.minimum(kt, kt_last) * nb
        bias = jnp.concatenate([jnp.broadcast_to(bias_ref[pl.ds(blk0 + j, 1), :], (SEL_BLOCK, QT))
                                for j in range(nb)], axis=0)
        seen = (kt * ST + lax.broadcasted_iota(jnp.int32, (ST, QT), 0)
                <= t0 + lax.broadcasted_iota(jnp.int32, (ST, QT), 1))
        s = s_bufs[slot][...] + tile4(jnp.where(seen, bias, NEG_INF))
        m_new = jnp.maximum(m, jnp.max(s, axis=0, keepdims=True))
        p_bufs[slot][...] = jnp.exp2(s - m_new).astype(BF16)
        return m_new, jnp.exp2(m - m_new)

    def sel_pair(j, carry):
        m, acc, alpha = carry
        a = 2 * j
        m, alpha_a = sel_softmax(a, 0, m)
        sel_scores(a + 1, 1)
        acc = sel_values(a - 1, 1, acc, alpha)
        m, alpha_b = sel_softmax(a + 1, 1, m)
        sel_scores(a + 2, 0)
        acc = sel_values(a, 0, acc, alpha_a)
        return m, acc, alpha_b

    sel_scores(0, 0)
    p1_ref[...] = jnp.zeros_like(p1_ref)
    n_pairs = (kt_last + 2) // 2
    _, acc_s, alpha = lax.fori_loop(
        0, n_pairs, sel_pair,
        (jnp.full((1, NQ), NEG_INF, F32), jnp.zeros((V_ROWS, NQ), F32), jnp.ones((1, NQ), F32)))
    acc_s = sel_values(2 * n_pairs - 1, 1, acc_s, alpha)

    n_wt = (WINDOW + QT) // KT
    k0w = pl.multiple_of(jnp.maximum(t0 - WINDOW, 0), KT)
    kt_w = k0w // KT
    keys_w = kw_ref[0, pl.ds(k0w, WINDOW + QT), :]
    dw = (t0 + lax.broadcasted_iota(jnp.int32, (WINDOW + QT, QT), 1)
          - (k0w + lax.broadcasted_iota(jnp.int32, (WINDOW + QT, QT), 0)))
    bias_w = wide(jnp.where(dw >= 0, jnp.where(dw < WINDOW, 0.0, NEG_INF), NEG_INF))
    acc_w = []
    for cs in groups:
        sw = _dot(keys_w, qpad[:, cs]) + bias_w
        pw = jnp.exp2(sw - jnp.max(sw, axis=0, keepdims=True)).astype(BF16)
        a = _dot(vwt_ref[0, kt_w], pw[0:KT])
        for j in range(1, n_wt):
            a = a + _dot(vwt_ref[0, kt_w + j], pw[j * KT:(j + 1) * KT])
        acc_w.append(a)
    acc_w = jnp.concatenate(acc_w, axis=1)

    gates = _sigmoid(gt_ref[0, 0])
    grow = lambda j: jnp.concatenate([gates[j, r:r + 1, :] for r in range(R)], axis=1)
    D = HEAD_DIM
    o = (grow(0) * o_c + grow(1) * (acc_s[0:D] / acc_s[D:D + 1])
         + grow(2) * (acc_w[0:D] / acc_w[D:D + 1]))
    halves = []
    for h in range(R // 2):
        pair = jnp.concatenate([o[:, (2 * h) * QT:(2 * h + 1) * QT],
                                o[:, (2 * h + 1) * QT:(2 * h + 2) * QT]], axis=0)
        halves.append(pair.T)
    o_ref[...] = jnp.concatenate(halves, axis=1)


def _nsa_attn(qt, kcmp, vct, ks3, vst, kw3, vwt, gt, ov_t, B, T):
    G, R = NSA_KV_HEADS, NSA_GROUP
    nq = T // Q_TILE
    nk = T // KEY_TILE
    nch = kcmp.shape[1]
    n_sel = ov_t.shape[0]
    return pl.pallas_call(
        _nsa_attn_kernel,
        grid=(B, G, nq),
        in_specs=[pl.BlockSpec((1, 1, R * HEAD_DIM, Q_TILE), lambda b, g, q: (b, q, g, 0)),
                  pl.BlockSpec((1, nch, LANES), lambda b, g, q: (b, 0, 0)),
                  pl.BlockSpec((1, HEAD_DIM, nch), lambda b, g, q: (b, g, 0)),
                  pl.BlockSpec((1, T, LANES), lambda b, g, q: (b, 0, 0)),
                  pl.BlockSpec((1, T // SEL_TILE, V_ROWS, SEL_TILE), lambda b, g, q: (b, 0, g, 0)),
                  pl.BlockSpec((1, T, LANES), lambda b, g, q: (b, 0, 0)),
                  pl.BlockSpec((1, nk, V_ROWS, KEY_TILE), lambda b, g, q: (b, 0, g, 0)),
                  pl.BlockSpec((1, 1, 3, R, Q_TILE), lambda b, g, q: (b, g, 0, 0, q)),
                  pl.BlockSpec((n_sel, nch), lambda b, g, q: (0, 0))],
        out_specs=pl.BlockSpec((Q_TILE, R * HEAD_DIM), lambda b, g, q: (b * nq + q, g)),
        out_shape=jax.ShapeDtypeStruct((B * T, NSA_WIDTH), F32),
        scratch_shapes=[pltpu.VMEM((n_sel, Q_TILE), F32),
                        pltpu.VMEM((SEL_TILE, R * Q_TILE), F32), pltpu.VMEM((SEL_TILE, R * Q_TILE), F32),
                        pltpu.VMEM((SEL_TILE, R * Q_TILE), BF16), pltpu.VMEM((SEL_TILE, R * Q_TILE), BF16)],
        compiler_params=_params("arbitrary", "arbitrary", "arbitrary"),
        name="nsa_attn",
    )(qt, kcmp, vct, ks3, vst, kw3, vwt, gt, ov_t)


def _first_index_of(vals, target):
    idx = jnp.full_like(target, float(len(vals) - 1))
    for i in range(len(vals) - 2, -1, -1):
        idx = jnp.where(vals[i] == target, float(i), idx)
    return idx


def _pick(vals, idx):
    out = vals[-1]
    for i in range(len(vals) - 2, -1, -1):
        out = jnp.where(idx == float(i), vals[i], out)
    return out


def _route_rows(score, bias):
    E, G, P = N_EXPERTS, N_GROUPS, EXPERTS_PER_GROUP
    sel = score + bias
    s = [sel[e:e + 1, :] for e in range(E)]
    raw = [score[e:e + 1, :] for e in range(E)]
    grp = []
    for gi in range(G):
        a = s[gi * P:(gi + 1) * P]
        best = None
        for i in range(P):
            for j in range(i + 1, P):
                pair = a[i] + a[j]
                best = pair if best is None else jnp.maximum(best, pair)
        grp.append(best)
    gmax = functools.reduce(jnp.maximum, grp)
    g_star = _first_index_of(grp, gmax)
    v = [_pick([s[gi * P + i] for gi in range(G)], g_star) for i in range(P)]
    w = [_pick([raw[gi * P + i] for gi in range(G)], g_star) for i in range(P)]
    i1 = _first_index_of(v, functools.reduce(jnp.maximum, v))
    v2 = [jnp.where(i1 == float(i), -jnp.inf, v[i]) for i in range(P)]
    i2 = _first_index_of(v2, functools.reduce(jnp.maximum, v2))
    w1, w2 = _pick(w, i1), _pick(w, i2)
    tot = w1 + w2
    zero = jnp.zeros_like(tot)
    return jnp.concatenate([g_star * P + i1, g_star * P + i2, w1 / tot, w2 / tot, zero, zero, zero, zero],
                           axis=0)


def _merge_kernel(ys_ref, g_ref, bonus_ref, gng_ref, gnb_ref, bd_ref, yb_ref, pm_ref, x_ref, mod_ref,
                  ng_ref, wa_ref, wb_ref, wo_ref, rw_ref, rb_ref, o_x, o_h, o_route):
    m = mod_ref[pl.program_id(0)]
    bd = bd_ref[...]
    y = ys_ref[...]
    mean = _dot_split_lhs(y, bd)
    yc = y - mean
    var = _dot_split_lhs(yc * yc, bd)
    ya = (yc * lax.rsqrt(var + RWKV_GN_EPS) * gng_ref[...] + gnb_ref[...] + bonus_ref[...]) * g_ref[...]
    pm = pm_ref[...]
    D = x_ref.shape[1]
    mix = (_sigmoid(pm[:, 0:D]) * _dot(ya.astype(BF16), wa_ref[...])
           + _sigmoid(pm[:, D:2 * D]) * _dot(yb_ref[...].astype(BF16), wb_ref[...]))
    x = x_ref[...] + m[2:3] * _dot(mix.astype(BF16), wo_ref[...])
    o_x[...] = x
    ms = jnp.mean(x * x, axis=-1, keepdims=True)
    h = x * lax.rsqrt(ms + NORM_EPS) * ng_ref[...]
    h = h * (1.0 + m[4:5]) + m[3:4]
    o_h[...] = h
    score = _sigmoid(_dot_3pass(h, rw_ref).T[0:N_EXPERTS, :])
    o_route[...] = _route_rows(score, rb_ref[...])


def _merge(ys, g, bonus, gn_g, gn_b, yb, pm, x2, mod, ng, wa, wb, wo, router_w, router_b, B, T, tm=256):
    N, D = x2.shape
    W = RWKV_WIDTH
    nt = T // tm
    row = lambda b, t: (b * nt + t, 0)
    full = lambda shape: pl.BlockSpec(shape, lambda b, t: (0,) * len(shape))
    return pl.pallas_call(
        _merge_kernel,
        grid=(B, nt),
        in_specs=[pl.BlockSpec((tm, W), row), pl.BlockSpec((tm, W), row), pl.BlockSpec((tm, W), row),
                  full((1, W)), full((1, W)), full((W, W)),
                  pl.BlockSpec((tm, NSA_WIDTH), row), pl.BlockSpec((tm, 2 * D), row),
                  pl.BlockSpec((tm, D), row), full((B, 6, D)), full((1, D)),
                  full((W, D)), full((NSA_WIDTH, D)), full((D, D)), full((2, D, LANES)),
                  full((N_EXPERTS, 1))],
        out_specs=[pl.BlockSpec((tm, D), row), pl.BlockSpec((tm, D), row),
                   pl.BlockSpec((8, tm), lambda b, t: (0, b * nt + t))],
        out_shape=[jax.ShapeDtypeStruct((N, D), F32), jax.ShapeDtypeStruct((N, D), F32),
                   jax.ShapeDtypeStruct((8, N), F32)],
        compiler_params=_params("arbitrary", "arbitrary"),
        name="merge_out",
    )(ys, g, bonus, gn_g.reshape(1, W), gn_b.reshape(1, W), _head_block_diag(W, 1.0 / HEAD_DIM),
      yb, pm, x2, mod, ng, wa, wb, wo,
      _hi_lo(jnp.zeros((D, LANES), F32).at[:, :N_EXPERTS].set(router_w)), router_b.reshape(N_EXPERTS, 1))


def _route(route, N):
    wts = route[TOP_K:2 * TOP_K].T
    NK = N * TOP_K
    e_flat = route[0:TOP_K].astype(jnp.int32).reshape(-1)
    onehot = (e_flat[:, None] == jnp.arange(N_EXPERTS, dtype=jnp.int32)[None, :]).astype(jnp.int32)
    csum = jnp.cumsum(onehot, axis=0)
    counts = csum[-1]
    rank = jnp.take_along_axis(csum, e_flat[:, None], axis=1)[:, 0] - 1
    padded = (counts + MOE_BLOCK - 1) // MOE_BLOCK * MOE_BLOCK
    pad_end = jnp.cumsum(padded)
    pad_start = pad_end - padded
    dest = pad_start[e_flat] + rank
    n_blk = -(-NK // MOE_BLOCK) + N_EXPERTS
    P = n_blk * MOE_BLOCK
    slot_assign = jnp.zeros((P,), jnp.int32).at[dest].set(jnp.arange(NK, dtype=jnp.int32))
    slot_tok = jnp.where(slot_assign >= N, slot_assign - N, slot_assign)
    blk_start = jnp.arange(n_blk, dtype=jnp.int32) * MOE_BLOCK
    blk_expert = jnp.clip(jnp.searchsorted(pad_end, blk_start, side='right'), 0, N_EXPERTS - 1).astype(jnp.int32)
    blk_valid = jnp.clip((pad_start + counts)[blk_expert] - blk_start, 0, MOE_BLOCK).astype(jnp.int32)
    return wts, slot_tok, slot_assign, blk_expert, blk_valid, n_blk


MOE_ISSUE_UNROLL = 8


def _moe_kernel(be_ref, nv_ref, tok_ref, dst_ref, h_hbm, wg_ref, wu_ref, wd_ref, o_hbm, xbuf, ybuf, sem_in,
                sem_out):
    i = pl.program_id(0)
    MB = MOE_BLOCK
    U = MOE_ISSUE_UNROLL
    base = i * MB
    nv = nv_ref[i]

    @pl.when(i == 0)
    def _():
        xbuf[...] = jnp.zeros_like(xbuf)

    def gather_row(r, priority):
        pltpu.make_async_copy(h_hbm.at[pl.ds(tok_ref[base + r], 1), :], xbuf.at[pl.ds(r, 1), :],
                              sem_in).start(priority=priority)

    def scatter_row(r, priority):
        pltpu.make_async_copy(ybuf.at[pl.ds(r, 1), :], o_hbm.at[pl.ds(dst_ref[base + r], 1), :],
                              sem_out).start(priority=priority)

    def issue_rows(row_fn):
        def group(gi, _):
            for u in range(U):
                row_fn(gi * U + u, u % 2)
            return 0

        lax.fori_loop(0, nv // U, group, 0)

        def tail(r, _):
            row_fn(r, 0)
            return 0

        lax.fori_loop((nv // U) * U, nv, tail, 0)

    def wait_rows(descriptor):
        for bit in range(MB.bit_length()):
            n = 1 << bit

            @pl.when((nv & n) != 0)
            def _():
                descriptor(n).wait()

    @pl.when(nv > 0)
    def _():
        issue_rows(gather_row)
        wait_rows(lambda n: pltpu.make_async_copy(h_hbm.at[pl.ds(0, n), :], xbuf.at[pl.ds(0, n), :], sem_in))
        x = xbuf[...].astype(BF16)
        gate = _dot(x, wg_ref[0])
        up = _dot(x, wu_ref[0])
        act = (gate * _sigmoid(gate) * up).astype(BF16)
        ybuf[...] = _dot(act, wd_ref[0])
        issue_rows(scatter_row)
        wait_rows(lambda n: pltpu.make_async_copy(ybuf.at[pl.ds(0, n), :], o_hbm.at[pl.ds(0, n), :], sem_out))


def _moe(h, slot_tok, slot_assign, blk_expert, blk_valid, n_blk, wg, wu, wd):
    N, D = h.shape
    DE = wg.shape[2]
    wmap = lambda i, be, nv, tok, dst: (be[i], 0, 0)
    grid_spec = pltpu.PrefetchScalarGridSpec(
        num_scalar_prefetch=4,
        grid=(n_blk,),
        in_specs=[pl.BlockSpec(memory_space=pl.ANY), pl.BlockSpec((1, D, DE), wmap),
                  pl.BlockSpec((1, D, DE), wmap), pl.BlockSpec((1, DE, D), wmap)],
        out_specs=pl.BlockSpec(memory_space=pl.ANY),
        scratch_shapes=[pltpu.VMEM((MOE_BLOCK, D), F32), pltpu.VMEM((MOE_BLOCK, D), F32),
                        pltpu.SemaphoreType.DMA(()), pltpu.SemaphoreType.DMA(())],
    )
    return pl.pallas_call(
        _moe_kernel,
        grid_spec=grid_spec,
        out_shape=jax.ShapeDtypeStruct((TOP_K * N, D), F32),
        compiler_params=_params("arbitrary"),
        name="moe_experts",
    )(blk_expert, blk_valid, slot_tok, slot_assign, h, wg, wu, wd)


def _final_kernel(x_ref, y0_ref, y1_ref, w_ref, mod_ref, o_ref):
    m = mod_ref[pl.program_id(0)]
    w = w_ref[...]
    o_ref[...] = x_ref[...] + m[5:6] * (w[:, 0:1] * y0_ref[...] + w[:, 1:2] * y1_ref[...])


def _final(x2, ybuf, wts, mod, B, T, tm=512):
    N, D = x2.shape
    nt = T // tm
    row = lambda b, t: (b * nt + t, 0)
    return pl.pallas_call(
        _final_kernel,
        grid=(B, nt),
        in_specs=[pl.BlockSpec((tm, D), row), pl.BlockSpec((tm, D), row),
                  pl.BlockSpec((tm, D), lambda b, t: (N // tm + b * nt + t, 0)),
                  pl.BlockSpec((tm, TOP_K), row), pl.BlockSpec((B, 6, D), lambda b, t: (0, 0, 0))],
        out_specs=pl.BlockSpec((tm, D), row),
        out_shape=jax.ShapeDtypeStruct((N, D), F32),
        compiler_params=_params("arbitrary", "arbitrary"),
        name="moe_combine",
    )(x2, ybuf, ybuf, wts, mod)


def _overlap_t(n_sel, n_cmp_pad):
    ci = jnp.arange(n_cmp_pad)[None, :] * CMP_STRIDE
    sj = jnp.arange(n_sel)[:, None] * SEL_BLOCK
    ov = (ci <= sj + SEL_BLOCK - 1) & (ci + CMP_BLOCK - 1 >= sj) & (jnp.arange(n_cmp_pad)[None, :] < n_cmp_pad - 1)
    return ov.astype(BF16)


def kernel(x, c, w_ada, b_ada, norm_g, w_in, b_in, rwkv_mu, rwkv_w0, rwkv_w2, rwkv_a0, rwkv_a2, rwkv_g2,
           rwkv_k_k, rwkv_k_a, rwkv_r_k, rwkv_gn_g, rwkv_gn_b, qk_norm_g, cmp_pos, cmp_w1, cmp_w2,
           w_up_rwkv, w_up_nsa, w_out, router_w, router_b, exp_w_gate, exp_w_up, exp_w_down):
    B, T, D = x.shape
    L = w_ada.shape[0]
    N = B * T
    mods = _ada(c, w_ada, b_ada)
    tables = _rope_tables(jnp.arange(T, dtype=jnp.int32))
    nch = T // CMP_STRIDE
    tables_cmp = _rope_tables(jnp.arange(nch, dtype=jnp.int32) * CMP_STRIDE + CMP_BLOCK - 1)
    ov_t = _overlap_t(T // SEL_BLOCK, nch)
    n_gate = NSA_GATE_COLS
    x2 = x.reshape(N, D)
    for l in range(L):
        g0 = _SEG_KV[1] + n_gate
        w_pad = jnp.concatenate([w_in[l][:, :g0], jnp.zeros((D, GATE_PAD - n_gate), F32), w_in[l][:, g0:]],
                                axis=1).astype(BF16)
        b_pad = jnp.concatenate([b_in[l][:g0], jnp.zeros((GATE_PAD - n_gate,), F32), b_in[l][g0:]]).reshape(1, -1)
        p_rw, p_q, p_kv, p_gate, p_merge = _inproj(x2, mods[l], norm_g[l, 0].reshape(1, D), w_pad, b_pad, B, T)
        r, k, v, al, bb, ld, g, bonus = _rwkv_pre(p_rw, rwkv_mu[l], rwkv_w0[l], rwkv_w2[l], rwkv_a0[l],
                                                  rwkv_a2[l], rwkv_g2[l], rwkv_k_k[l], rwkv_k_a[l],
                                                  rwkv_r_k[l], B, T)
        ys = _rwkv_scan(r, k, v, al, bb, ld, B, T)
        qt, ks, kw, vst, vwt = _nsa_prep(p_q, p_kv, tables, qk_norm_g[l], B, T)
        kv3 = p_kv.reshape(B, T, KV_COLS)
        kcmp = _nsa_cmp(kv3, 0, cmp_pos[l], cmp_w1[l], cmp_w2[l], qk_norm_g[l, 1], tables_cmp)
        vct = _nsa_cmp(kv3, 1, cmp_pos[l], cmp_w1[l], cmp_w2[l], None, None)
        gt = p_gate[:, :n_gate].reshape(B, T, NSA_KV_HEADS, NSA_GROUP, 3).transpose(0, 2, 4, 3, 1)
        yb = _nsa_attn(qt, kcmp, vct, ks.reshape(B, T, LANES), vst, kw.reshape(B, T, LANES), vwt, gt, ov_t,
                       B, T)
        x2, h2, route = _merge(ys, g, bonus, rwkv_gn_g[l], rwkv_gn_b[l], yb, p_merge, x2, mods[l],
                               norm_g[l, 1].reshape(1, D), w_up_rwkv[l].astype(BF16),
                               w_up_nsa[l].astype(BF16), w_out[l].astype(BF16), router_w, router_b, B, T)
        wts, slot_tok, slot_assign, blk_expert, blk_valid, n_blk = _route(route, N)
        ybuf = _moe(h2, slot_tok, slot_assign, blk_expert, blk_valid, n_blk, exp_w_gate[l].astype(BF16),
                    exp_w_up[l].astype(BF16), exp_w_down[l].astype(BF16))
        x2 = _final(x2, ybuf, wts, mods[l], B, T)
    return x2.reshape(B, T, D)
```

```python
import functools
import math

import jax
import jax.numpy as jnp
from jax import lax
from jax.experimental import pallas as pl
from jax.experimental.pallas import tpu as pltpu

F32 = jnp.float32
BF16 = jnp.bfloat16
HI = lax.Precision.HIGHEST

D_MODEL = 1024
RWKV_HEADS = 8
HEAD_DIM = 64
RWKV_WIDTH = RWKV_HEADS * HEAD_DIM
DECAY_LORA = 64
ICLR_LORA = 64
GATE_LORA = 128
RWKV_GN_EPS = 64e-5
RWKV_COLS = 3 * RWKV_WIDTH + DECAY_LORA + ICLR_LORA + GATE_LORA

NSA_Q_HEADS = 8
NSA_KV_HEADS = 2
NSA_GROUP = NSA_Q_HEADS // NSA_KV_HEADS
NSA_WIDTH = NSA_Q_HEADS * HEAD_DIM
CMP_STRIDE = 16
CMP_BLOCK = 2 * CMP_STRIDE
CMP_HIDDEN = 256
SEL_BLOCK = 64
SEL_SHIFT = 6
SEL_TOPK = 16
WINDOW = 512
FORCE_SCORE = 1e4
NEG_INF = -1e30
ROPE_THETA = 500000.0
ROPE_DIM = HEAD_DIM // 4
KV_COLS = 6 * NSA_KV_HEADS * HEAD_DIM
NSA_GATE_COLS = 3 * NSA_Q_HEADS
GATE_PAD = 128

N_EXPERTS = 16
N_GROUPS = 4
EXPERTS_PER_GROUP = N_EXPERTS // N_GROUPS
TOP_K = 2
D_EXPERT = 512
MOE_BLOCK = 256
NORM_EPS = 1e-6

LANES = 128
CHUNK = 64
KEY_TILE = 128
SEL_TILE = 512
V_ROWS = 80
Q_SCALE = HEAD_DIM ** -0.5 * math.log2(math.e)
Q_TILE = 128
F32_TINY = float(jnp.finfo(jnp.float32).tiny)

_SEG_RW = (0, RWKV_COLS)
_SEG_Q = (_SEG_RW[1], _SEG_RW[1] + NSA_WIDTH)
_SEG_KV = (_SEG_Q[1], _SEG_Q[1] + KV_COLS)
_SEG_GATE = (_SEG_KV[1], _SEG_KV[1] + GATE_PAD)
_SEG_MERGE = (_SEG_GATE[1], _SEG_GATE[1] + 2 * D_MODEL)
IN_COLS_PAD = _SEG_MERGE[1]

_VMEM_LIMIT = 56 * 1024 * 1024


def _dot(a, b, precision=None):
    return jnp.dot(a, b, preferred_element_type=F32, precision=precision)


def _dot_tb(a, b, precision=None):
    return lax.dot_general(a, b, (((1,), (1,)), ((), ())), preferred_element_type=F32,
                           precision=precision)


def _dot_ta(a, b, precision=None):
    return lax.dot_general(a, b, (((0,), (0,)), ((), ())), preferred_element_type=F32,
                           precision=precision)


def _split_bf16(x, terms):
    parts = []
    for _ in range(terms - 1):
        parts.append(x.astype(BF16))
        x = x - parts[-1].astype(F32)
    parts.append(x.astype(BF16))
    return parts


def _dot_split_lhs(x, w_bf, terms=2):
    return functools.reduce(jnp.add, [_dot(p, w_bf) for p in _split_bf16(x, terms)])


def _dot_split_rhs(w_bf, x, terms=2):
    return functools.reduce(jnp.add, [_dot(w_bf, p) for p in _split_bf16(x, terms)])


def _dot_3pass(x, w_hl_ref):
    x_hi, x_lo = _split_bf16(x, 2)
    w_hi = w_hl_ref[0]
    return _dot(x_hi, w_hi) + _dot(x_lo, w_hi) + _dot(x_hi, w_hl_ref[1])


def _hi_lo(w):
    hi = w.astype(BF16)
    return jnp.stack([hi, (w - hi.astype(F32)).astype(BF16)])


def _params(*sem):
    return pltpu.CompilerParams(dimension_semantics=sem, vmem_limit_bytes=_VMEM_LIMIT)


def _sigmoid(x):
    return 1.0 / (1.0 + jnp.exp(-x))


def _ada_kernel(c_ref, w_ref, b_ref, o_ref):
    c = c_ref[...]
    s = c * _sigmoid(c)
    o_ref[0] = _dot(s, w_ref[0], HI) + b_ref[0]


def _ada(c, w_ada, b_ada):
    L, D, D6 = w_ada.shape
    B = c.shape[0]
    rows = 8
    cp = jnp.zeros((rows, D), F32).at[:B].set(c)
    tn = 1536
    out = pl.pallas_call(
        _ada_kernel,
        grid=(L, D6 // tn),
        in_specs=[pl.BlockSpec((rows, D), lambda l, j: (0, 0)),
                  pl.BlockSpec((1, D, tn), lambda l, j: (l, 0, j)),
                  pl.BlockSpec((1, 1, tn), lambda l, j: (l, 0, j))],
        out_specs=pl.BlockSpec((1, rows, tn), lambda l, j: (l, 0, j)),
        out_shape=jax.ShapeDtypeStruct((L, rows, D6), F32),
        compiler_params=_params("arbitrary", "arbitrary"),
        name="ada_mod",
    )(cp, w_ada, b_ada.reshape(L, 1, D6))
    return out[:, :B].reshape(L, B, 6, D)


def _inproj_kernel(x_ref, mod_ref, g_ref, w_ref, b_ref, o_rw, o_q, o_kv, o_gate, o_merge):
    m = mod_ref[pl.program_id(0)]
    x = x_ref[...]
    ms = jnp.mean(x * x, axis=-1, keepdims=True)
    h = x * lax.rsqrt(ms + NORM_EPS) * g_ref[...]
    h = h * (1.0 + m[1:2]) + m[0:1]
    hb = h.astype(BF16)
    for o, (a, e) in ((o_rw, _SEG_RW), (o_q, _SEG_Q), (o_kv, _SEG_KV), (o_gate, _SEG_GATE),
                      (o_merge, _SEG_MERGE)):
        o[...] = _dot(hb, w_ref[:, a:e]) + b_ref[:, a:e]


def _inproj(x2, mod, g, w_pad, b_pad, B, T, tm=256):
    N, D = x2.shape
    nt = T // tm
    row = lambda b, t: (b * nt + t, 0)
    widths = [e - a for a, e in (_SEG_RW, _SEG_Q, _SEG_KV, _SEG_GATE, _SEG_MERGE)]
    return pl.pallas_call(
        _inproj_kernel,
        grid=(B, nt),
        in_specs=[pl.BlockSpec((tm, D), row),
                  pl.BlockSpec((B, 6, D), lambda b, t: (0, 0, 0)),
                  pl.BlockSpec((1, D), lambda b, t: (0, 0)),
                  pl.BlockSpec((D, IN_COLS_PAD), lambda b, t: (0, 0)),
                  pl.BlockSpec((1, IN_COLS_PAD), lambda b, t: (0, 0))],
        out_specs=[pl.BlockSpec((tm, w), row) for w in widths],
        out_shape=[jax.ShapeDtypeStruct((N, w), F32) for w in widths],
        compiler_params=_params("arbitrary", "arbitrary"),
        name="in_proj",
    )(x2, mod, g, w_pad, b_pad)


def _rwkv_pre_kernel(p_ref, mu_ref, w0_ref, w2_ref, a0_ref, a2_ref, g2_ref, kk_ref, ka_ref, rk_ref,
                     bd_ref, o_r, o_k, o_v, o_al, o_b, o_ld, o_g, o_bonus, carry_ref):
    W = RWKV_WIDTH

    @pl.when(pl.program_id(1) == 0)
    def _():
        carry_ref[...] = jnp.zeros_like(carry_ref)

    p = p_ref[...]
    ts = p.shape[0]
    rows = lax.broadcasted_iota(jnp.int32, p.shape, 0)
    shifted = jnp.where(rows == 0, carry_ref[0:1, :], pltpu.roll(p, 1, 0))
    carry_ref[0:1, :] = p[ts - 1:ts, :]
    pm = p + (shifted - p) * mu_ref[...]
    r = pm[:, 0:W]
    k = pm[:, W:2 * W]
    v = pm[:, 2 * W:3 * W]
    wa = pm[:, 3 * W:3 * W + DECAY_LORA + ICLR_LORA]
    gl = pm[:, 3 * W + DECAY_LORA + ICLR_LORA:]
    xw = w0_ref[...] + _dot_3pass(jnp.tanh(wa), w2_ref)
    ld = -math.exp(-0.5) * _sigmoid(xw)
    a = _sigmoid(a0_ref[...] + _dot_3pass(wa, a2_ref))
    g = _dot_3pass(_sigmoid(gl), g2_ref)
    bd = bd_ref[...]
    kk = k * kk_ref[...]
    nrm = jnp.sqrt(_dot_split_lhs(kk * kk, bd))
    kk = kk / jnp.maximum(nrm, 1e-12)
    k2 = k * (1.0 + (a - 1.0) * ka_ref[...])
    bonus = _dot_split_lhs(r * k2 * rk_ref[...], bd) * v
    o_r[...] = r
    o_k[...] = k2
    o_v[...] = v
    o_al[...] = kk
    o_b[...] = -kk * a
    o_ld[...] = ld
    o_g[...] = g
    o_bonus[...] = bonus


def _head_block_diag(width, scale=1.0):
    i = jnp.arange(width) // HEAD_DIM
    return ((i[:, None] == i[None, :]).astype(F32) * scale).astype(BF16)


def _rwkv_pre(p_rw, mu, w0, w2, a0, a2, g2, k_k, k_a, r_k, B, T, ts=256):
    N = p_rw.shape[0]
    W = RWKV_WIDTH
    nt = T // ts
    row = lambda b, t: (b * nt + t, 0)
    zl = jnp.zeros((DECAY_LORA, W), F32)
    w2p = jnp.concatenate([w2, zl], axis=0)
    a2p = jnp.concatenate([zl, a2], axis=0)
    full = lambda shape: pl.BlockSpec(shape, lambda b, t: (0,) * len(shape))
    vec = lambda z: z.reshape(1, -1)
    return pl.pallas_call(
        _rwkv_pre_kernel,
        grid=(B, nt),
        in_specs=[pl.BlockSpec((ts, RWKV_COLS), row), full((1, RWKV_COLS)), full((1, W)),
                  full((2, 2 * DECAY_LORA, W)), full((1, W)), full((2, 2 * DECAY_LORA, W)),
                  full((2, GATE_LORA, W)), full((1, W)), full((1, W)), full((1, W)), full((W, W))],
        out_specs=[pl.BlockSpec((ts, W), row)] * 8,
        out_shape=[jax.ShapeDtypeStruct((N, W), F32)] * 8,
        scratch_shapes=[pltpu.VMEM((8, RWKV_COLS), F32)],
        compiler_params=_params("arbitrary", "arbitrary"),
        name="rwkv_pre",
    )(p_rw, vec(mu), vec(w0), _hi_lo(w2p), vec(a0), _hi_lo(a2p), _hi_lo(g2), vec(k_k), vec(k_a), vec(r_k),
      _head_block_diag(W))


def _bf(x):
    return x.astype(BF16)


def _scan_local(chunks, eye, strict, incl, m0, m1):
    C = CHUNK
    n = range(len(chunks))
    st = lambda z: jnp.concatenate([z * m0, z * m1], axis=0)
    zero = jnp.zeros((2 * C, 2 * C), F32)
    at_b, rt_s, vs, vs_b, lhs_a, rhs_a, bk_t, dcol = [], [], [], [], [], [], [], []
    for r, k, v, al, bb, ld, cum in chunks:
        tot = cum[C - 1:C, :]
        dinv = jnp.exp(-cum)
        dend = jnp.exp(tot - cum)
        at_b.append(_bf(st(al * jnp.exp(cum - ld))))
        rt_s.append(st(r * jnp.exp(cum)))
        vs.append(st(v))
        vs_b.append(_bf(vs[-1]))
        lhs_a.append(jnp.concatenate([at_b[-1], _bf(rt_s[-1])], axis=0))
        rhs_a.append(_bf(jnp.concatenate([st(bb * dinv), st(k * dinv)], axis=0)))
        bk_t.append(_bf(jnp.concatenate([st(bb * dend).T, st(k * dend).T], axis=1)))
        dcol.append(jnp.sum(eye * jnp.exp(tot), axis=1, keepdims=True))
    A = [_dot_tb(lhs_a[i], rhs_a[i]) for i in n]
    a_ab = [jnp.where(strict, A[i][0:2 * C, 0:2 * C], zero) for i in n]
    a_ak = [_bf(jnp.where(strict, A[i][0:2 * C, 2 * C:4 * C], zero)) for i in n]
    a_r = [_bf(jnp.concatenate([jnp.where(incl, A[i][2 * C:4 * C, 0:2 * C], zero),
                                jnp.where(incl, A[i][2 * C:4 * C, 2 * C:4 * C], zero)], axis=1)) for i in n]
    akv = [_bf(_dot(a_ak[i], vs_b[i])) for i in n]
    pw = a_ab
    tinv = [eye + pw[i] for i in n]
    for _ in range(5):
        pw_b = [_bf(pw[i]) for i in n]
        pw = [_dot(pw_b[i], pw_b[i]) for i in n]
        tinv = [tinv[i] + _dot(_bf(pw[i]), _bf(tinv[i])) for i in n]
    X = [_dot(_bf(tinv[i]), jnp.concatenate([at_b[i], akv[i]], axis=1)) for i in n]
    w_b = [_bf(X[i][:, 0:LANES]) for i in n]
    uv0 = [jnp.concatenate([_bf(X[i][:, LANES:2 * LANES]), vs_b[i]], axis=0) for i in n]
    m_h = [_bf(_dot(bk_t[i][:, 0:2 * C], w_b[i])) for i in n]
    g_h = [_dot(bk_t[i], uv0[i]) for i in n]
    q_h = [_bf(rt_s[i] + _dot(a_r[i][:, 0:2 * C], w_b[i])) for i in n]
    y0 = [_dot(a_r[i], uv0[i]) for i in n]
    return [(m_h[i], g_h[i], dcol[i], q_h[i], y0[i]) for i in n]


def _scan_steps(local, H):
    C = CHUNK
    ys = []
    for m_h, g_h, dcol, q_h, y0 in local:
        h_b = _bf(H)
        Y = _dot(q_h, h_b) + y0
        ys.append(Y[0:C] + Y[C:2 * C])
        H = dcol * H + _dot(m_h, h_b) + g_h
    return ys, H


def _rwkv_scan_kernel(r_ref, k_ref, v_ref, al_ref, b_ref, ld_ref, o_ref, h_ref):
    C = CHUNK
    tc = r_ref.shape[0]

    @pl.when(pl.program_id(2) == 0)
    def _():
        h_ref[...] = jnp.zeros_like(h_ref)

    tri = jnp.where(lax.broadcasted_iota(jnp.int32, (C, C), 1) <= lax.broadcasted_iota(jnp.int32, (C, C), 0),
                    1.0, 0.0).astype(BF16)
    r2 = lax.broadcasted_iota(jnp.int32, (2 * C, 2 * C), 0)
    c2 = lax.broadcasted_iota(jnp.int32, (2 * C, 2 * C), 1)
    eye = (r2 == c2).astype(F32)
    strict = (c2 & (C - 1)) < (r2 & (C - 1))
    incl = (c2 & (C - 1)) <= (r2 & (C - 1))
    lane = lax.broadcasted_iota(jnp.int32, (C, LANES), 1)
    m0 = (lane < HEAD_DIM).astype(F32)
    m1 = 1.0 - m0
    nc = tc // C
    cum = _dot_split_rhs(tri, jnp.concatenate([ld_ref[c * C:(c + 1) * C, :] for c in range(nc)], axis=1), 3)
    chunks = []
    for c in range(nc):
        sl = slice(c * C, (c + 1) * C)
        chunks.append((r_ref[sl, :], k_ref[sl, :], v_ref[sl, :], al_ref[sl, :], b_ref[sl, :], ld_ref[sl, :],
                       cum[:, c * LANES:(c + 1) * LANES]))
    ys, H = _scan_steps(_scan_local(chunks, eye, strict, incl, m0, m1), h_ref[...])
    for c in range(nc):
        o_ref[c * C:(c + 1) * C, :] = ys[c]
    h_ref[...] = H


def _rwkv_scan(r, k, v, al, bb, ld, B, T, tc=512):
    N, W = r.shape
    nt = T // tc
    spec = pl.BlockSpec((tc, LANES), lambda b, h, t: (b * nt + t, h))
    return pl.pallas_call(
        _rwkv_scan_kernel,
        grid=(B, W // LANES, nt),
        in_specs=[spec] * 6,
        out_specs=spec,
        out_shape=jax.ShapeDtypeStruct((N, W), F32),
        scratch_shapes=[pltpu.VMEM((LANES, LANES), F32)],
        compiler_params=_params("arbitrary", "arbitrary", "arbitrary"),
        name="rwkv_scan",
    )(r, k, v, al, bb, ld)


def _rope_tables(pos):
    half = ROPE_DIM // 2
    inv = jnp.power(ROPE_THETA, -jnp.arange(half, dtype=F32) * 2.0 / ROPE_DIM)
    ang = pos.astype(F32)[:, None] * inv[None, :]
    cos, sin = jnp.cos(ang), jnp.sin(ang)
    n = pos.shape[0]
    rest = HEAD_DIM - ROPE_DIM
    c = jnp.concatenate([cos, cos, jnp.ones((n, rest), F32)], axis=1)
    s_dn = jnp.concatenate([-sin, jnp.zeros((n, half + rest), F32)], axis=1)
    s_up = jnp.concatenate([jnp.zeros((n, half), F32), sin, jnp.zeros((n, rest), F32)], axis=1)
    rep = LANES // HEAD_DIM
    return jnp.tile(c, (1, rep)), jnp.tile(s_dn, (1, rep)), jnp.tile(s_up, (1, rep))


def _norm_rope(x, bd, g, c, s_dn, s_up):
    width = x.shape[1]
    half = ROPE_DIM // 2
    rep = width // LANES
    tile = (lambda z: jnp.concatenate([z] * rep, axis=1)) if rep > 1 else (lambda z: z)
    ms = _dot_split_lhs(x * x, bd)
    xn = x * lax.rsqrt(ms + NORM_EPS) * g
    return (xn * tile(c) + pltpu.roll(xn, width - half, 1) * tile(s_dn)
            + pltpu.roll(xn, half, 1) * tile(s_up))


def _nsa_prep_kernel(q_ref, kv_ref, c_ref, sd_ref, su_ref, gq_ref, gs_ref, gw_ref, bdq_ref, bdk_ref,
                     o_qt, o_ks, o_kw, o_vst, o_vsd, o_vwt):
    c, sd, su = c_ref[...], sd_ref[...], su_ref[...]
    q = _norm_rope(q_ref[...], bdq_ref[...], gq_ref[...], c, sd, su) * Q_SCALE
    qt = q.T
    ts = q.shape[0]
    kv = kv_ref[...]
    bdk = bdk_ref[...]
    pos = pl.program_id(1) * ts + lax.broadcasted_iota(jnp.int32, (ts, LANES), 0)
    blk_onehot = jnp.where((pos >> SEL_SHIFT) == lax.broadcasted_iota(jnp.int32, (ts, LANES), 1), 1.0, 0.0)
    ks = _norm_rope(kv[:, 2 * LANES:3 * LANES], bdk, gs_ref[...], c, sd, su)
    o_ks[...] = jnp.concatenate([ks, blk_onehot], axis=1).astype(BF16)
    o_kw[...] = _norm_rope(kv[:, 4 * LANES:5 * LANES], bdk, gw_ref[...], c, sd, su).astype(BF16)
    ones_rows = jnp.where(lax.broadcasted_iota(jnp.int32, (V_ROWS - HEAD_DIM, q.shape[0]), 0) == 0, 1.0, 0.0)

    def values_t(x):
        xt = x.T
        return jnp.concatenate([xt[0:HEAD_DIM], ones_rows, xt[HEAD_DIM:2 * HEAD_DIM], ones_rows], axis=0)

    vst = values_t(kv[:, 3 * LANES:4 * LANES])
    vwt = values_t(kv[:, 5 * LANES:6 * LANES])
    for j in range(q.shape[0] // KEY_TILE):
        sl = slice(j * KEY_TILE, (j + 1) * KEY_TILE)
        o_qt[0, j] = qt[:, sl].astype(BF16)
        o_vsd[0, j] = vst[:, sl].astype(BF16)
        o_vwt[0, j] = vwt[:, sl].astype(BF16)
    for j in range(q.shape[0] // SEL_TILE):
        o_vst[0, j] = vst[:, j * SEL_TILE:(j + 1) * SEL_TILE].astype(BF16)


def _nsa_prep(q, kv, tables, qk_g, B, T, ts=512):
    N = q.shape[0]
    nt = T // ts
    nk = ts // KEY_TILE
    ns = ts // SEL_TILE
    row = lambda b, t: (b * nt + t, 0)
    full = lambda shape: pl.BlockSpec(shape, lambda b, t: (0,) * len(shape))
    tab = pl.BlockSpec((ts, LANES), lambda b, t: (t, 0))
    gq = jnp.tile(qk_g[0], NSA_Q_HEADS).reshape(1, NSA_WIDTH)
    gs = jnp.tile(qk_g[2], NSA_KV_HEADS).reshape(1, LANES)
    gw = jnp.tile(qk_g[3], NSA_KV_HEADS).reshape(1, LANES)
    tiled = lambda rows: pl.BlockSpec((1, nk, rows, KEY_TILE), lambda b, t: (b, t, 0, 0))
    return pl.pallas_call(
        _nsa_prep_kernel,
        grid=(B, nt),
        in_specs=[pl.BlockSpec((ts, NSA_WIDTH), row), pl.BlockSpec((ts, KV_COLS), row), tab, tab, tab,
                  full((1, NSA_WIDTH)), full((1, LANES)), full((1, LANES)),
                  full((NSA_WIDTH, NSA_WIDTH)), full((LANES, LANES))],
        out_specs=[tiled(NSA_WIDTH), pl.BlockSpec((ts, 2 * LANES), row), pl.BlockSpec((ts, LANES), row),
                   pl.BlockSpec((1, ns, NSA_KV_HEADS * V_ROWS, SEL_TILE), lambda b, t: (b, t, 0, 0)),
                   tiled(NSA_KV_HEADS * V_ROWS), tiled(NSA_KV_HEADS * V_ROWS)],
        out_shape=[jax.ShapeDtypeStruct((B, T // KEY_TILE, NSA_WIDTH, KEY_TILE), BF16),
                   jax.ShapeDtypeStruct((N, 2 * LANES), BF16), jax.ShapeDtypeStruct((N, LANES), BF16),
                   jax.ShapeDtypeStruct((B, T // SEL_TILE, NSA_KV_HEADS * V_ROWS, SEL_TILE), BF16),
                   jax.ShapeDtypeStruct((B, T // KEY_TILE, NSA_KV_HEADS * V_ROWS, KEY_TILE), BF16),
                   jax.ShapeDtypeStruct((B, T // KEY_TILE, NSA_KV_HEADS * V_ROWS, KEY_TILE), BF16)],
        compiler_params=_params("arbitrary", "arbitrary"),
        name="nsa_prep",
    )(q, kv, *tables, gq, gs, gw, _head_block_diag(NSA_WIDTH, 1.0 / HEAD_DIM),
      _head_block_diag(LANES, 1.0 / HEAD_DIM))


def _gelu_tanh(x):
    return 0.5 * x * (1.0 + jnp.tanh(0.7978845608028654 * (x + 0.044715 * x * x * x)))


def _nsa_cmp_kernel(x_ref, pos_ref, w1_ref, w2_ref, *rest, is_key):
    if is_key:
        g_ref, c_ref, sd_ref, su_ref, bd_ref, o_ref, xs_ref = rest
    else:
        o_ref, xs_ref = rest
    nch = xs_ref.shape[0]
    S = CMP_STRIDE
    for j in range(S):
        xs_ref[:, j * LANES:(j + 1) * LANES] = x_ref[0, pl.ds(j, nch, stride=S), :]
    xs = xs_ref[...]
    first = _dot((xs + pos_ref[0:1, :]).astype(BF16), w1_ref[0])
    second = _dot((xs + pos_ref[1:2, :]).astype(BF16), w1_ref[1])
    hid = first + pltpu.roll(second, nch - 1, 0)
    out = _dot(_gelu_tanh(hid).astype(BF16), w2_ref[...])
    rows = lax.broadcasted_iota(jnp.int32, out.shape, 0)
    if is_key:
        out = _norm_rope(out, bd_ref[...], g_ref[...], c_ref[...], sd_ref[...], su_ref[...])
        o_ref[0] = jnp.where(rows < nch - 1, out, 0.0).astype(BF16)
    else:
        o_ref[0] = jnp.where(rows < nch - 1, out, 0.0).T.astype(BF16)


def _nsa_cmp(kv3, which, cmp_pos, cmp_w1, cmp_w2, g_k, tables_cmp):
    B, T, _ = kv3.shape
    S = CMP_STRIDE
    nch = T // S
    is_key = which == 0
    eye2 = jnp.eye(NSA_KV_HEADS, dtype=F32)
    w1 = cmp_w1[which].reshape(CMP_BLOCK, HEAD_DIM, CMP_HIDDEN)
    w1 = jnp.einsum('jdh,ge->jgdeh', w1, eye2).reshape(2, S * LANES, NSA_KV_HEADS * CMP_HIDDEN)
    w2 = jnp.einsum('hd,ge->ghed', cmp_w2[which], eye2).reshape(NSA_KV_HEADS * CMP_HIDDEN, LANES)
    pos = jnp.tile(cmp_pos[which].reshape(2, S, 1, HEAD_DIM), (1, 1, NSA_KV_HEADS, 1)).reshape(2, S * LANES)
    full = lambda shape: pl.BlockSpec(shape, lambda b: (0,) * len(shape))
    in_specs = [pl.BlockSpec((1, T, LANES), lambda b: (b, 0, which)), full(pos.shape), full(w1.shape),
                full(w2.shape)]
    args = [kv3, pos, w1.astype(BF16), w2.astype(BF16)]
    if is_key:
        in_specs += [full((1, LANES)), full((nch, LANES)), full((nch, LANES)), full((nch, LANES)),
                     full((LANES, LANES))]
        args += [jnp.tile(g_k, NSA_KV_HEADS).reshape(1, LANES), *tables_cmp,
                 _head_block_diag(LANES, 1.0 / HEAD_DIM)]
        out_spec = pl.BlockSpec((1, nch, LANES), lambda b: (b, 0, 0))
        out_shape = jax.ShapeDtypeStruct((B, nch, LANES), BF16)
    else:
        out_spec = pl.BlockSpec((1, LANES, nch), lambda b: (b, 0, 0))
        out_shape = jax.ShapeDtypeStruct((B, LANES, nch), BF16)
    return pl.pallas_call(
        functools.partial(_nsa_cmp_kernel, is_key=is_key),
        grid=(B,),
        in_specs=in_specs,
        out_specs=out_spec,
        out_shape=out_shape,
        scratch_shapes=[pltpu.VMEM((nch, S * LANES), F32)],
        compiler_params=_params("arbitrary"),
        name="nsa_cmp_k" if is_key else "nsa_cmp_v",
    )(*args)


def _nsa_attn_kernel(qt_ref, kc_ref, vct_ref, ks_ref, vst_ref, vsd_ref, kw_ref, vwt_ref, gt_ref, ov_ref, o_ref,
                     rhs_ref, s0_ref, s1_ref, s2_ref, s3_ref, p0_ref, p1_ref):
    g = pl.program_id(1)
    qb = pl.program_id(2)
    R = NSA_GROUP
    QT = Q_TILE
    KT = KEY_TILE
    NQ = R * QT
    t0 = qb * QT
    n_cmp_pad = kc_ref.shape[1]
    n_sel = ov_ref.shape[0]

    q_g = jnp.concatenate([qt_ref[0, 0, r * HEAD_DIM:(r + 1) * HEAD_DIM, :] for r in range(R)], axis=1)
    q2 = jnp.concatenate([q_g, q_g], axis=0)
    row_grp = lax.broadcasted_iota(jnp.int32, q2.shape, 0) // HEAD_DIM
    qpad = jnp.where(row_grp == g, q2, jnp.zeros_like(q2))

    tq_row = t0 + (lax.broadcasted_iota(jnp.int32, (1, NQ), 1) & (QT - 1))
    tile4 = lambda z: jnp.concatenate([z] * R, axis=1)

    sc = _dot(kc_ref[0], qpad)
    n_i = lax.broadcasted_iota(jnp.int32, (n_cmp_pad, 1), 0)
    cend = jnp.where(n_i < n_cmp_pad - 1, n_i * CMP_STRIDE + (CMP_BLOCK - 1), jnp.int32(2 ** 30))
    cvalid = cend <= tq_row
    sc = jnp.where(cvalid, sc, NEG_INF)
    mc = jnp.max(sc, axis=0, keepdims=True)
    ec = jnp.where(cvalid, jnp.exp2(sc - mc), 0.0)
    pc = ec / jnp.maximum(jnp.sum(ec, axis=0, keepdims=True), F32_TINY)
    o_c = _dot(vct_ref[0], pc.astype(BF16))
    pc_sum = pc[:, 0:QT]
    for r in range(1, R):
        pc_sum = pc_sum + pc[:, r * QT:(r + 1) * QT]
    imp = _dot_split_rhs(ov_ref[...], pc_sum)

    ji = lax.broadcasted_iota(jnp.int32, (n_sel, QT), 0)
    jf = ji.astype(F32)
    tq_sel = t0 + lax.broadcasted_iota(jnp.int32, (n_sel, QT), 1)
    cur = tq_sel >> SEL_SHIFT
    forced = (ji == 0) | (ji == cur) | (ji == cur - 1)
    valid = ji * SEL_BLOCK <= tq_sel
    score = jnp.where(valid, jnp.where(forced, FORCE_SCORE, imp), -1.0)
    sel = jnp.zeros((n_sel, QT), F32)
    for _ in range(min(SEL_TOPK, n_sel)):
        mx = jnp.max(score, axis=0, keepdims=True)
        jmin = jnp.min(jnp.where(score == mx, jf, 1e9), axis=0, keepdims=True)
        hit = jf == jmin
        sel = jnp.where(hit, 1.0, sel)
        score = jnp.where(hit, -3e38, score)

    ST = SEL_TILE
    bias_all = (jnp.where(valid, sel, 0.0) - 1.0) * (-NEG_INF)
    first_own = t0 // SEL_BLOCK

    def with_bias_rows(bias):
        rows = tile4(bias).astype(BF16)
        if n_sel < LANES:
            rows = jnp.concatenate([rows, jnp.zeros((LANES - n_sel, NQ), BF16)], axis=0)
        return jnp.concatenate([qpad, rows], axis=0)

    rhs_ref[...] = with_bias_rows(jnp.where(ji < first_own, bias_all, NEG_INF))
    n_tiles = (t0 + ST - 1) // ST
    last_tile = ks_ref.shape[1] // ST - 1
    p_bufs = (p0_ref, p1_ref)

    def sel_scores(kt, s_ref):
        k0 = pl.multiple_of(jnp.minimum(kt, last_tile) * ST, ST)
        s_ref[...] = _dot(ks_ref[0, pl.ds(k0, ST), :], rhs_ref[...])

    def sel_values(kt, slot, acc, alpha):
        return acc * alpha + _dot(vst_ref[0, jnp.clip(kt, 0, last_tile)], p_bufs[slot][...])

    def sel_softmax(s_ref, slot, m):
        s = s_ref[...]
        m_new = jnp.maximum(m, jnp.max(s, axis=0, keepdims=True))
        p_bufs[slot][...] = jnp.exp2(s - m_new).astype(BF16)
        return m_new, jnp.exp2(m - m_new)

    def sel_pair(a, carry, s_now, s_next):
        m, acc, alpha0, alpha1 = carry
        acc = sel_values(a - 2, 0, acc, alpha0)
        acc = sel_values(a - 1, 1, acc, alpha1)
        sel_scores(a + 2, s_next[0])
        sel_scores(a + 3, s_next[1])
        m, alpha0 = sel_softmax(s_now[0], 0, m)
        m, alpha1 = sel_softmax(s_now[1], 1, m)
        return m, acc, alpha0, alpha1

    bufs_a, bufs_b = (s0_ref, s1_ref), (s2_ref, s3_ref)
    sel_scores(0, s0_ref)
    sel_scores(1, s1_ref)
    p0_ref[...] = jnp.zeros_like(p0_ref)
    p1_ref[...] = jnp.zeros_like(p1_ref)
    n_pairs = (n_tiles + 1) // 2
    one = jnp.ones((1, NQ), F32)
    m_s, acc_s, alpha0, alpha1 = lax.fori_loop(
        0, n_pairs,
        lambda j, carry: lax.cond(j % 2 == 0,
                                  lambda c: sel_pair(2 * j, c, bufs_a, bufs_b),
                                  lambda c: sel_pair(2 * j, c, bufs_b, bufs_a), carry),
        (jnp.full((1, NQ), NEG_INF, F32), jnp.zeros((V_ROWS, NQ), F32), one, one))
    acc_s = sel_values(2 * n_pairs - 2, 0, acc_s, alpha0)
    acc_s = sel_values(2 * n_pairs - 1, 1, acc_s, alpha1)
    own = _dot(ks_ref[0, pl.ds(pl.multiple_of(t0, QT), QT), :], with_bias_rows(bias_all))
    seen = lax.broadcasted_iota(jnp.int32, (QT, QT), 0) <= lax.broadcasted_iota(jnp.int32, (QT, QT), 1)
    own = jnp.where(tile4(seen), own, NEG_INF)
    m_new = jnp.maximum(m_s, jnp.max(own, axis=0, keepdims=True))
    acc_s = acc_s * jnp.exp2(m_s - m_new) + _dot(vsd_ref[0, qb], jnp.exp2(own - m_new).astype(BF16))

    n_wt = (WINDOW + QT) // KT
    k0w = pl.multiple_of(jnp.maximum(t0 - WINDOW, 0), KT)
    kt_w = k0w // KT
    keys_w = kw_ref[0, pl.ds(k0w, WINDOW + QT), :]
    dw = (t0 + lax.broadcasted_iota(jnp.int32, (WINDOW + QT, QT), 1)
          - (k0w + lax.broadcasted_iota(jnp.int32, (WINDOW + QT, QT), 0)))
    sw = _dot(keys_w, qpad) + tile4(jnp.where(dw >= 0, jnp.where(dw < WINDOW, 0.0, NEG_INF), NEG_INF))
    pw = jnp.exp2(sw - jnp.max(sw, axis=0, keepdims=True)).astype(BF16)
    acc_w = _dot(vwt_ref[0, kt_w], pw[0:KT])
    for j in range(1, n_wt):
        acc_w = acc_w + _dot(vwt_ref[0, kt_w + j], pw[j * KT:(j + 1) * KT])

    gates = _sigmoid(gt_ref[0, 0])
    grow = lambda j: jnp.concatenate([gates[j, r:r + 1, :] for r in range(R)], axis=1)
    D = HEAD_DIM
    o = (grow(0) * o_c + grow(1) * (acc_s[0:D] / acc_s[D:D + 1])
         + grow(2) * (acc_w[0:D] / acc_w[D:D + 1]))
    halves = []
    for h in range(R // 2):
        pair = jnp.concatenate([o[:, (2 * h) * QT:(2 * h + 1) * QT],
                                o[:, (2 * h + 1) * QT:(2 * h + 2) * QT]], axis=0)
        halves.append(pair.T)
    o_ref[...] = jnp.concatenate(halves, axis=1)


def _nsa_attn(qt, kcmp, vct, ks3, vst, vsd, kw3, vwt, gt, ov_t, B, T):
    G, R = NSA_KV_HEADS, NSA_GROUP
    nq = T // Q_TILE
    nk = T // KEY_TILE
    nch = kcmp.shape[1]
    n_sel = ov_t.shape[0]
    assert (T // SEL_TILE) % 2 == 0 and n_sel <= LANES and Q_TILE == KEY_TILE
    return pl.pallas_call(
        _nsa_attn_kernel,
        grid=(B, G, nq),
        in_specs=[pl.BlockSpec((1, 1, R * HEAD_DIM, Q_TILE), lambda b, g, q: (b, q, g, 0)),
                  pl.BlockSpec((1, nch, LANES), lambda b, g, q: (b, 0, 0)),
                  pl.BlockSpec((1, HEAD_DIM, nch), lambda b, g, q: (b, g, 0)),
                  pl.BlockSpec((1, T, 2 * LANES), lambda b, g, q: (b, 0, 0)),
                  pl.BlockSpec((1, T // SEL_TILE, V_ROWS, SEL_TILE), lambda b, g, q: (b, 0, g, 0)),
                  pl.BlockSpec((1, nk, V_ROWS, KEY_TILE), lambda b, g, q: (b, 0, g, 0)),
                  pl.BlockSpec((1, T, LANES), lambda b, g, q: (b, 0, 0)),
                  pl.BlockSpec((1, nk, V_ROWS, KEY_TILE), lambda b, g, q: (b, 0, g, 0)),
                  pl.BlockSpec((1, 1, 3, R, Q_TILE), lambda b, g, q: (b, g, 0, 0, q)),
                  pl.BlockSpec((n_sel, nch), lambda b, g, q: (0, 0))],
        out_specs=pl.BlockSpec((Q_TILE, R * HEAD_DIM), lambda b, g, q: (b * nq + q, g)),
        out_shape=jax.ShapeDtypeStruct((B * T, NSA_WIDTH), F32),
        scratch_shapes=[pltpu.VMEM((2 * LANES, R * Q_TILE), BF16),
                        *[pltpu.VMEM((SEL_TILE, R * Q_TILE), F32)] * 4,
                        *[pltpu.VMEM((SEL_TILE, R * Q_TILE), BF16)] * 2],
        compiler_params=_params("arbitrary", "arbitrary", "arbitrary"),
        name="nsa_attn",
    )(qt, kcmp, vct, ks3, vst, vsd, kw3, vwt, gt, ov_t)


def _first_index_of(vals, target):
    idx = jnp.full_like(target, float(len(vals) - 1))
    for i in range(len(vals) - 2, -1, -1):
        idx = jnp.where(vals[i] == target, float(i), idx)
    return idx


def _pick(vals, idx):
    out = vals[-1]
    for i in range(len(vals) - 2, -1, -1):
        out = jnp.where(idx == float(i), vals[i], out)
    return out


def _route_rows(score, bias):
    E, G, P = N_EXPERTS, N_GROUPS, EXPERTS_PER_GROUP
    sel = score + bias
    s = [sel[e:e + 1, :] for e in range(E)]
    raw = [score[e:e + 1, :] for e in range(E)]
    grp = []
    for gi in range(G):
        a = s[gi * P:(gi + 1) * P]
        best = None
        for i in range(P):
            for j in range(i + 1, P):
                pair = a[i] + a[j]
                best = pair if best is None else jnp.maximum(best, pair)
        grp.append(best)
    gmax = functools.reduce(jnp.maximum, grp)
    g_star = _first_index_of(grp, gmax)
    v = [_pick([s[gi * P + i] for gi in range(G)], g_star) for i in range(P)]
    w = [_pick([raw[gi * P + i] for gi in range(G)], g_star) for i in range(P)]
    i1 = _first_index_of(v, functools.reduce(jnp.maximum, v))
    v2 = [jnp.where(i1 == float(i), -jnp.inf, v[i]) for i in range(P)]
    i2 = _first_index_of(v2, functools.reduce(jnp.maximum, v2))
    w1, w2 = _pick(w, i1), _pick(w, i2)
    tot = w1 + w2
    zero = jnp.zeros_like(tot)
    return jnp.concatenate([g_star * P + i1, g_star * P + i2, w1 / tot, w2 / tot, zero, zero, zero, zero],
                           axis=0)


def _merge_kernel(ys_ref, g_ref, bonus_ref, gng_ref, gnb_ref, bd_ref, yb_ref, pm_ref, x_ref, mod_ref,
                  ng_ref, wa_ref, wb_ref, wo_ref, rw_ref, rb_ref, o_x, o_h, o_route):
    m = mod_ref[pl.program_id(0)]
    bd = bd_ref[...]
    y = ys_ref[...]
    mean = _dot_split_lhs(y, bd)
    yc = y - mean
    var = _dot_split_lhs(yc * yc, bd)
    ya = (yc * lax.rsqrt(var + RWKV_GN_EPS) * gng_ref[...] + gnb_ref[...] + bonus_ref[...]) * g_ref[...]
    pm = pm_ref[...]
    D = x_ref.shape[1]
    mix = (_sigmoid(pm[:, 0:D]) * _dot(ya.astype(BF16), wa_ref[...])
           + _sigmoid(pm[:, D:2 * D]) * _dot(yb_ref[...].astype(BF16), wb_ref[...]))
    x = x_ref[...] + m[2:3] * _dot(mix.astype(BF16), wo_ref[...])
    o_x[...] = x
    ms = jnp.mean(x * x, axis=-1, keepdims=True)
    h = x * lax.rsqrt(ms + NORM_EPS) * ng_ref[...]
    h = h * (1.0 + m[4:5]) + m[3:4]
    o_h[...] = h
    score = _sigmoid(_dot_3pass(h, rw_ref).T[0:N_EXPERTS, :])
    o_route[...] = _route_rows(score, rb_ref[...])


def _merge(ys, g, bonus, gn_g, gn_b, yb, pm, x2, mod, ng, wa, wb, wo, router_w, router_b, B, T, tm=256):
    N, D = x2.shape
    W = RWKV_WIDTH
    nt = T // tm
    row = lambda b, t: (b * nt + t, 0)
    full = lambda shape: pl.BlockSpec(shape, lambda b, t: (0,) * len(shape))
    return pl.pallas_call(
        _merge_kernel,
        grid=(B, nt),
        in_specs=[pl.BlockSpec((tm, W), row), pl.BlockSpec((tm, W), row), pl.BlockSpec((tm, W), row),
                  full((1, W)), full((1, W)), full((W, W)),
                  pl.BlockSpec((tm, NSA_WIDTH), row), pl.BlockSpec((tm, 2 * D), row),
                  pl.BlockSpec((tm, D), row), full((B, 6, D)), full((1, D)),
                  full((W, D)), full((NSA_WIDTH, D)), full((D, D)), full((2, D, LANES)),
                  full((N_EXPERTS, 1))],
        out_specs=[pl.BlockSpec((tm, D), row), pl.BlockSpec((tm, D), row),
                   pl.BlockSpec((8, tm), lambda b, t: (0, b * nt + t))],
        out_shape=[jax.ShapeDtypeStruct((N, D), F32), jax.ShapeDtypeStruct((N, D), F32),
                   jax.ShapeDtypeStruct((8, N), F32)],
        compiler_params=_params("arbitrary", "arbitrary"),
        name="merge_out",
    )(ys, g, bonus, gn_g.reshape(1, W), gn_b.reshape(1, W), _head_block_diag(W, 1.0 / HEAD_DIM),
      yb, pm, x2, mod, ng, wa, wb, wo,
      _hi_lo(jnp.zeros((D, LANES), F32).at[:, :N_EXPERTS].set(router_w)), router_b.reshape(N_EXPERTS, 1))


def _route(route, N):
    wts = route[TOP_K:2 * TOP_K].T
    NK = N * TOP_K
    e_flat = route[0:TOP_K].astype(jnp.int32).reshape(-1)
    onehot = (e_flat[:, None] == jnp.arange(N_EXPERTS, dtype=jnp.int32)[None, :]).astype(jnp.int32)
    csum = jnp.cumsum(onehot, axis=0)
    counts = csum[-1]
    rank = jnp.take_along_axis(csum, e_flat[:, None], axis=1)[:, 0] - 1
    padded = (counts + MOE_BLOCK - 1) // MOE_BLOCK * MOE_BLOCK
    pad_end = jnp.cumsum(padded)
    pad_start = pad_end - padded
    dest = pad_start[e_flat] + rank
    n_blk = -(-NK // MOE_BLOCK) + N_EXPERTS
    P = n_blk * MOE_BLOCK
    slot_assign = jnp.zeros((P,), jnp.int32).at[dest].set(jnp.arange(NK, dtype=jnp.int32))
    slot_tok = jnp.where(slot_assign >= N, slot_assign - N, slot_assign)
    blk_start = jnp.arange(n_blk, dtype=jnp.int32) * MOE_BLOCK
    blk_expert = jnp.clip(jnp.searchsorted(pad_end, blk_start, side='right'), 0, N_EXPERTS - 1).astype(jnp.int32)
    blk_valid = jnp.clip((pad_start + counts)[blk_expert] - blk_start, 0, MOE_BLOCK).astype(jnp.int32)
    return wts, slot_tok, slot_assign, blk_expert, blk_valid, n_blk


MOE_ISSUE_UNROLL = 8


def _moe_kernel(be_ref, nv_ref, tok_ref, dst_ref, h_hbm, wg_ref, wu_ref, wd_ref, o_hbm, xbuf, ybuf, sem_in,
                sem_out):
    i = pl.program_id(0)
    MB = MOE_BLOCK
    U = MOE_ISSUE_UNROLL
    base = i * MB
    nv = nv_ref[i]

    @pl.when(i == 0)
    def _():
        xbuf[...] = jnp.zeros_like(xbuf)

    def gather_row(r, priority):
        pltpu.make_async_copy(h_hbm.at[pl.ds(tok_ref[base + r], 1), :], xbuf.at[pl.ds(r, 1), :],
                              sem_in).start(priority=priority)

    def scatter_row(r, priority):
        pltpu.make_async_copy(ybuf.at[pl.ds(r, 1), :], o_hbm.at[pl.ds(dst_ref[base + r], 1), :],
                              sem_out).start(priority=priority)

    def issue_rows(row_fn):
        def group(gi, _):
            for u in range(U):
                row_fn(gi * U + u, u % 2)
            return 0

        lax.fori_loop(0, nv // U, group, 0)

        def tail(r, _):
            row_fn(r, 0)
            return 0

        lax.fori_loop((nv // U) * U, nv, tail, 0)

    def wait_rows(descriptor):
        for bit in range(MB.bit_length()):
            n = 1 << bit

            @pl.when((nv & n) != 0)
            def _():
                descriptor(n).wait()

    @pl.when(nv > 0)
    def _():
        issue_rows(gather_row)
        wait_rows(lambda n: pltpu.make_async_copy(h_hbm.at[pl.ds(0, n), :], xbuf.at[pl.ds(0, n), :], sem_in))
        x = xbuf[...].astype(BF16)
        gate = _dot(x, wg_ref[0])
        up = _dot(x, wu_ref[0])
        act = (gate * _sigmoid(gate) * up).astype(BF16)
        ybuf[...] = _dot(act, wd_ref[0])
        issue_rows(scatter_row)
        wait_rows(lambda n: pltpu.make_async_copy(ybuf.at[pl.ds(0, n), :], o_hbm.at[pl.ds(0, n), :], sem_out))


def _moe(h, slot_tok, slot_assign, blk_expert, blk_valid, n_blk, wg, wu, wd):
    N, D = h.shape
    DE = wg.shape[2]
    wmap = lambda i, be, nv, tok, dst: (be[i], 0, 0)
    grid_spec = pltpu.PrefetchScalarGridSpec(
        num_scalar_prefetch=4,
        grid=(n_blk,),
        in_specs=[pl.BlockSpec(memory_space=pl.ANY), pl.BlockSpec((1, D, DE), wmap),
                  pl.BlockSpec((1, D, DE), wmap), pl.BlockSpec((1, DE, D), wmap)],
        out_specs=pl.BlockSpec(memory_space=pl.ANY),
        scratch_shapes=[pltpu.VMEM((MOE_BLOCK, D), F32), pltpu.VMEM((MOE_BLOCK, D), F32),
                        pltpu.SemaphoreType.DMA(()), pltpu.SemaphoreType.DMA(())],
    )
    return pl.pallas_call(
        _moe_kernel,
        grid_spec=grid_spec,
        out_shape=jax.ShapeDtypeStruct((TOP_K * N, D), F32),
        compiler_params=_params("arbitrary"),
        name="moe_experts",
    )(blk_expert, blk_valid, slot_tok, slot_assign, h, wg, wu, wd)


def _final_kernel(x_ref, y0_ref, y1_ref, w_ref, mod_ref, o_ref):
    m = mod_ref[pl.program_id(0)]
    w = w_ref[...]
    o_ref[...] = x_ref[...] + m[5:6] * (w[:, 0:1] * y0_ref[...] + w[:, 1:2] * y1_ref[...])


def _final(x2, ybuf, wts, mod, B, T, tm=512):
    N, D = x2.shape
    nt = T // tm
    row = lambda b, t: (b * nt + t, 0)
    return pl.pallas_call(
        _final_kernel,
        grid=(B, nt),
        in_specs=[pl.BlockSpec((tm, D), row), pl.BlockSpec((tm, D), row),
                  pl.BlockSpec((tm, D), lambda b, t: (N // tm + b * nt + t, 0)),
                  pl.BlockSpec((tm, TOP_K), row), pl.BlockSpec((B, 6, D), lambda b, t: (0, 0, 0))],
        out_specs=pl.BlockSpec((tm, D), row),
        out_shape=jax.ShapeDtypeStruct((N, D), F32),
        compiler_params=_params("arbitrary", "arbitrary"),
        name="moe_combine",
    )(x2, ybuf, ybuf, wts, mod)


def _overlap_t(n_sel, n_cmp_pad):
    ci = jnp.arange(n_cmp_pad)[None, :] * CMP_STRIDE
    sj = jnp.arange(n_sel)[:, None] * SEL_BLOCK
    ov = (ci <= sj + SEL_BLOCK - 1) & (ci + CMP_BLOCK - 1 >= sj) & (jnp.arange(n_cmp_pad)[None, :] < n_cmp_pad - 1)
    return ov.astype(BF16)


def kernel(x, c, w_ada, b_ada, norm_g, w_in, b_in, rwkv_mu, rwkv_w0, rwkv_w2, rwkv_a0, rwkv_a2, rwkv_g2,
           rwkv_k_k, rwkv_k_a, rwkv_r_k, rwkv_gn_g, rwkv_gn_b, qk_norm_g, cmp_pos, cmp_w1, cmp_w2,
           w_up_rwkv, w_up_nsa, w_out, router_w, router_b, exp_w_gate, exp_w_up, exp_w_down):
    B, T, D = x.shape
    L = w_ada.shape[0]
    N = B * T
    mods = _ada(c, w_ada, b_ada)
    tables = _rope_tables(jnp.arange(T, dtype=jnp.int32))
    nch = T // CMP_STRIDE
    tables_cmp = _rope_tables(jnp.arange(nch, dtype=jnp.int32) * CMP_STRIDE + CMP_BLOCK - 1)
    ov_t = _overlap_t(T // SEL_BLOCK, nch)
    n_gate = NSA_GATE_COLS
    x2 = x.reshape(N, D)
    for l in range(L):
        g0 = _SEG_KV[1] + n_gate
        w_pad = jnp.concatenate([w_in[l][:, :g0], jnp.zeros((D, GATE_PAD - n_gate), F32), w_in[l][:, g0:]],
                                axis=1).astype(BF16)
        b_pad = jnp.concatenate([b_in[l][:g0], jnp.zeros((GATE_PAD - n_gate,), F32), b_in[l][g0:]]).reshape(1, -1)
        p_rw, p_q, p_kv, p_gate, p_merge = _inproj(x2, mods[l], norm_g[l, 0].reshape(1, D), w_pad, b_pad, B, T)
        r, k, v, al, bb, ld, g, bonus = _rwkv_pre(p_rw, rwkv_mu[l], rwkv_w0[l], rwkv_w2[l], rwkv_a0[l],
                                                  rwkv_a2[l], rwkv_g2[l], rwkv_k_k[l], rwkv_k_a[l],
                                                  rwkv_r_k[l], B, T)
        ys = _rwkv_scan(r, k, v, al, bb, ld, B, T)
        qt, ks, kw, vst, vsd, vwt = _nsa_prep(p_q, p_kv, tables, qk_norm_g[l], B, T)
        kv3 = p_kv.reshape(B, T, KV_COLS)
        kcmp = _nsa_cmp(kv3, 0, cmp_pos[l], cmp_w1[l], cmp_w2[l], qk_norm_g[l, 1], tables_cmp)
        vct = _nsa_cmp(kv3, 1, cmp_pos[l], cmp_w1[l], cmp_w2[l], None, None)
        gt = p_gate[:, :n_gate].reshape(B, T, NSA_KV_HEADS, NSA_GROUP, 3).transpose(0, 2, 4, 3, 1)
        yb = _nsa_attn(qt, kcmp, vct, ks.reshape(B, T, 2 * LANES), vst, vsd, kw.reshape(B, T, LANES), vwt, gt, ov_t,
                       B, T)
        x2, h2, route = _merge(ys, g, bonus, rwkv_gn_g[l], rwkv_gn_b[l], yb, p_merge, x2, mods[l],
                               norm_g[l, 1].reshape(1, D), w_up_rwkv[l].astype(BF16),
                               w_up_nsa[l].astype(BF16), w_out[l].astype(BF16), router_w, router_b, B, T)
        wts, slot_tok, slot_assign, blk_expert, blk_valid, n_blk = _route(route, N)
        ybuf = _moe(h2, slot_tok, slot_assign, blk_expert, blk_valid, n_blk, exp_w_gate[l].astype(BF16),
                    exp_w_up[l].astype(BF16), exp_w_down[l].astype(BF16))
        x2 = _final(x2, ybuf, wts, mods[l], B, T)
    return x2.reshape(B, T, D)
```

```python
import functools
import math

import jax
import jax.numpy as jnp
from jax import lax
from jax.experimental import pallas as pl
from jax.experimental.pallas import tpu as pltpu
from jax.experimental.pallas import tpu_sc as plsc

F32 = jnp.float32
BF16 = jnp.bfloat16
HI = lax.Precision.HIGHEST

D_MODEL = 1024
RWKV_HEADS = 8
HEAD_DIM = 64
RWKV_WIDTH = RWKV_HEADS * HEAD_DIM
DECAY_LORA = 64
ICLR_LORA = 64
GATE_LORA = 128
RWKV_GN_EPS = 64e-5
RWKV_COLS = 3 * RWKV_WIDTH + DECAY_LORA + ICLR_LORA + GATE_LORA

NSA_Q_HEADS = 8
NSA_KV_HEADS = 2
NSA_GROUP = NSA_Q_HEADS // NSA_KV_HEADS
NSA_WIDTH = NSA_Q_HEADS * HEAD_DIM
CMP_STRIDE = 16
CMP_BLOCK = 2 * CMP_STRIDE
CMP_HIDDEN = 256
SEL_BLOCK = 64
SEL_SHIFT = 6
SEL_TOPK = 16
WINDOW = 512
FORCE_SCORE = 1e4
NEG_INF = -1e30
ROPE_THETA = 500000.0
ROPE_DIM = HEAD_DIM // 4
KV_COLS = 6 * NSA_KV_HEADS * HEAD_DIM
NSA_GATE_COLS = 3 * NSA_Q_HEADS
GATE_PAD = 128

N_EXPERTS = 16
N_GROUPS = 4
EXPERTS_PER_GROUP = N_EXPERTS // N_GROUPS
TOP_K = 2
D_EXPERT = 512
MOE_BLOCK = 256
NORM_EPS = 1e-6

LANES = 128
CHUNK = 64
KEY_TILE = 128
SEL_TILE = 512
V_ROWS = 80
Q_SCALE = HEAD_DIM ** -0.5 * math.log2(math.e)
Q_TILE = 128
F32_TINY = float(jnp.finfo(jnp.float32).tiny)

_SEG_RW = (0, RWKV_COLS)
_SEG_Q = (_SEG_RW[1], _SEG_RW[1] + NSA_WIDTH)
_SEG_KV = (_SEG_Q[1], _SEG_Q[1] + KV_COLS)
_SEG_GATE = (_SEG_KV[1], _SEG_KV[1] + GATE_PAD)
_SEG_MERGE = (_SEG_GATE[1], _SEG_GATE[1] + 2 * D_MODEL)
IN_COLS_PAD = _SEG_MERGE[1]

_VMEM_LIMIT = 56 * 1024 * 1024


def _dot(a, b, precision=None):
    return jnp.dot(a, b, preferred_element_type=F32, precision=precision)


def _dot_tb(a, b, precision=None):
    return lax.dot_general(a, b, (((1,), (1,)), ((), ())), preferred_element_type=F32,
                           precision=precision)


def _dot_ta(a, b, precision=None):
    return lax.dot_general(a, b, (((0,), (0,)), ((), ())), preferred_element_type=F32,
                           precision=precision)


def _split_bf16(x, terms):
    parts = []
    for _ in range(terms - 1):
        parts.append(x.astype(BF16))
        x = x - parts[-1].astype(F32)
    parts.append(x.astype(BF16))
    return parts


def _dot_split_lhs(x, w_bf, terms=2):
    return functools.reduce(jnp.add, [_dot(p, w_bf) for p in _split_bf16(x, terms)])


def _dot_split_rhs(w_bf, x, terms=2):
    return functools.reduce(jnp.add, [_dot(w_bf, p) for p in _split_bf16(x, terms)])


def _dot_3pass(x, w_hl_ref):
    x_hi, x_lo = _split_bf16(x, 2)
    w_hi = w_hl_ref[0]
    return _dot(x_hi, w_hi) + _dot(x_lo, w_hi) + _dot(x_hi, w_hl_ref[1])


def _hi_lo(w):
    hi = w.astype(BF16)
    return jnp.stack([hi, (w - hi.astype(F32)).astype(BF16)])


def _params(*sem):
    return pltpu.CompilerParams(dimension_semantics=sem, vmem_limit_bytes=_VMEM_LIMIT)


def _sigmoid(x):
    return 1.0 / (1.0 + jnp.exp(-x))


def _ada_kernel(c_ref, w_ref, b_ref, o_ref):
    c = c_ref[...]
    s = c * _sigmoid(c)
    o_ref[0] = _dot(s, w_ref[0], HI) + b_ref[0]


def _ada(c, w_ada, b_ada):
    L, D, D6 = w_ada.shape
    B = c.shape[0]
    rows = 8
    cp = jnp.zeros((rows, D), F32).at[:B].set(c)
    tn = 1536
    out = pl.pallas_call(
        _ada_kernel,
        grid=(L, D6 // tn),
        in_specs=[pl.BlockSpec((rows, D), lambda l, j: (0, 0)),
                  pl.BlockSpec((1, D, tn), lambda l, j: (l, 0, j)),
                  pl.BlockSpec((1, 1, tn), lambda l, j: (l, 0, j))],
        out_specs=pl.BlockSpec((1, rows, tn), lambda l, j: (l, 0, j)),
        out_shape=jax.ShapeDtypeStruct((L, rows, D6), F32),
        compiler_params=_params("arbitrary", "arbitrary"),
        name="ada_mod",
    )(cp, w_ada, b_ada.reshape(L, 1, D6))
    return out[:, :B].reshape(L, B, 6, D)


def _inproj_kernel(x_ref, mod_ref, g_ref, w_ref, b_ref, o_rw, o_q, o_kv, o_gate, o_merge):
    m = mod_ref[pl.program_id(0)]
    x = x_ref[...]
    ms = jnp.mean(x * x, axis=-1, keepdims=True)
    h = x * lax.rsqrt(ms + NORM_EPS) * g_ref[...]
    h = h * (1.0 + m[1:2]) + m[0:1]
    hb = h.astype(BF16)
    for o, (a, e) in ((o_rw, _SEG_RW), (o_q, _SEG_Q), (o_kv, _SEG_KV), (o_gate, _SEG_GATE),
                      (o_merge, _SEG_MERGE)):
        o[...] = _dot(hb, w_ref[:, a:e]) + b_ref[:, a:e]


def _inproj(x2, mod, g, w_pad, b_pad, B, T, tm=256):
    N, D = x2.shape
    nt = T // tm
    row = lambda b, t: (b * nt + t, 0)
    widths = [e - a for a, e in (_SEG_RW, _SEG_Q, _SEG_KV, _SEG_GATE, _SEG_MERGE)]
    return pl.pallas_call(
        _inproj_kernel,
        grid=(B, nt),
        in_specs=[pl.BlockSpec((tm, D), row),
                  pl.BlockSpec((B, 6, D), lambda b, t: (0, 0, 0)),
                  pl.BlockSpec((1, D), lambda b, t: (0, 0)),
                  pl.BlockSpec((D, IN_COLS_PAD), lambda b, t: (0, 0)),
                  pl.BlockSpec((1, IN_COLS_PAD), lambda b, t: (0, 0))],
        out_specs=[pl.BlockSpec((tm, w), row) for w in widths],
        out_shape=[jax.ShapeDtypeStruct((N, w), F32) for w in widths],
        compiler_params=_params("arbitrary", "arbitrary"),
        name="in_proj",
    )(x2, mod, g, w_pad, b_pad)


def _rwkv_pre_kernel(p_ref, mu_ref, w0_ref, w2_ref, a0_ref, a2_ref, g2_ref, kk_ref, ka_ref, rk_ref,
                     bd_ref, o_r, o_k, o_v, o_al, o_b, o_ld, o_g, o_bonus, carry_ref):
    W = RWKV_WIDTH

    @pl.when(pl.program_id(1) == 0)
    def _():
        carry_ref[...] = jnp.zeros_like(carry_ref)

    p = p_ref[...]
    ts = p.shape[0]
    rows = lax.broadcasted_iota(jnp.int32, p.shape, 0)
    shifted = jnp.where(rows == 0, carry_ref[0:1, :], pltpu.roll(p, 1, 0))
    carry_ref[0:1, :] = p[ts - 1:ts, :]
    pm = p + (shifted - p) * mu_ref[...]
    r = pm[:, 0:W]
    k = pm[:, W:2 * W]
    v = pm[:, 2 * W:3 * W]
    wa = pm[:, 3 * W:3 * W + DECAY_LORA + ICLR_LORA]
    gl = pm[:, 3 * W + DECAY_LORA + ICLR_LORA:]
    xw = w0_ref[...] + _dot_3pass(jnp.tanh(wa), w2_ref)
    ld = -math.exp(-0.5) * _sigmoid(xw)
    a = _sigmoid(a0_ref[...] + _dot_3pass(wa, a2_ref))
    g = _dot_3pass(_sigmoid(gl), g2_ref)
    bd = bd_ref[...]
    kk = k * kk_ref[...]
    nrm = jnp.sqrt(_dot_split_lhs(kk * kk, bd))
    kk = kk / jnp.maximum(nrm, 1e-12)
    k2 = k * (1.0 + (a - 1.0) * ka_ref[...])
    bonus = _dot_split_lhs(r * k2 * rk_ref[...], bd) * v
    o_r[...] = r
    o_k[...] = k2
    o_v[...] = v
    o_al[...] = kk
    o_b[...] = -kk * a
    o_ld[...] = ld
    o_g[...] = g
    o_bonus[...] = bonus


def _head_block_diag(width, scale=1.0):
    i = jnp.arange(width) // HEAD_DIM
    return ((i[:, None] == i[None, :]).astype(F32) * scale).astype(BF16)


def _rwkv_pre(p_rw, mu, w0, w2, a0, a2, g2, k_k, k_a, r_k, B, T, ts=256):
    N = p_rw.shape[0]
    W = RWKV_WIDTH
    nt = T // ts
    row = lambda b, t: (b * nt + t, 0)
    zl = jnp.zeros((DECAY_LORA, W), F32)
    w2p = jnp.concatenate([w2, zl], axis=0)
    a2p = jnp.concatenate([zl, a2], axis=0)
    full = lambda shape: pl.BlockSpec(shape, lambda b, t: (0,) * len(shape))
    vec = lambda z: z.reshape(1, -1)
    return pl.pallas_call(
        _rwkv_pre_kernel,
        grid=(B, nt),
        in_specs=[pl.BlockSpec((ts, RWKV_COLS), row), full((1, RWKV_COLS)), full((1, W)),
                  full((2, 2 * DECAY_LORA, W)), full((1, W)), full((2, 2 * DECAY_LORA, W)),
                  full((2, GATE_LORA, W)), full((1, W)), full((1, W)), full((1, W)), full((W, W))],
        out_specs=[pl.BlockSpec((ts, W), row)] * 8,
        out_shape=[jax.ShapeDtypeStruct((N, W), F32)] * 8,
        scratch_shapes=[pltpu.VMEM((8, RWKV_COLS), F32)],
        compiler_params=_params("arbitrary", "arbitrary"),
        name="rwkv_pre",
    )(p_rw, vec(mu), vec(w0), _hi_lo(w2p), vec(a0), _hi_lo(a2p), _hi_lo(g2), vec(k_k), vec(k_a), vec(r_k),
      _head_block_diag(W))


def _bf(x):
    return x.astype(BF16)


def _scan_local(chunks, eye, strict, incl, m0, m1):
    C = CHUNK
    n = range(len(chunks))
    st = lambda z: jnp.concatenate([z * m0, z * m1], axis=0)
    zero = jnp.zeros((2 * C, 2 * C), F32)
    at_b, rt_s, vs, vs_b, lhs_a, rhs_a, bk_t, dcol = [], [], [], [], [], [], [], []
    for r, k, v, al, bb, ld, cum in chunks:
        tot = cum[C - 1:C, :]
        dinv = jnp.exp(-cum)
        dend = jnp.exp(tot - cum)
        at_b.append(_bf(st(al * jnp.exp(cum - ld))))
        rt_s.append(st(r * jnp.exp(cum)))
        vs.append(st(v))
        vs_b.append(_bf(vs[-1]))
        lhs_a.append(jnp.concatenate([at_b[-1], _bf(rt_s[-1])], axis=0))
        rhs_a.append(_bf(jnp.concatenate([st(bb * dinv), st(k * dinv)], axis=0)))
        bk_t.append(_bf(jnp.concatenate([st(bb * dend).T, st(k * dend).T], axis=1)))
        dcol.append(jnp.sum(eye * jnp.exp(tot), axis=1, keepdims=True))
    A = [_dot_tb(lhs_a[i], rhs_a[i]) for i in n]
    a_ab = [jnp.where(strict, A[i][0:2 * C, 0:2 * C], zero) for i in n]
    a_ak = [_bf(jnp.where(strict, A[i][0:2 * C, 2 * C:4 * C], zero)) for i in n]
    a_r = [_bf(jnp.concatenate([jnp.where(incl, A[i][2 * C:4 * C, 0:2 * C], zero),
                                jnp.where(incl, A[i][2 * C:4 * C, 2 * C:4 * C], zero)], axis=1)) for i in n]
    akv = [_bf(_dot(a_ak[i], vs_b[i])) for i in n]
    pw = a_ab
    tinv = [eye + pw[i] for i in n]
    for _ in range(5):
        pw_b = [_bf(pw[i]) for i in n]
        pw = [_dot(pw_b[i], pw_b[i]) for i in n]
        tinv = [tinv[i] + _dot(_bf(pw[i]), _bf(tinv[i])) for i in n]
    X = [_dot(_bf(tinv[i]), jnp.concatenate([at_b[i], akv[i]], axis=1)) for i in n]
    w_b = [_bf(X[i][:, 0:LANES]) for i in n]
    uv0 = [jnp.concatenate([_bf(X[i][:, LANES:2 * LANES]), vs_b[i]], axis=0) for i in n]
    m_h = [_bf(_dot(bk_t[i][:, 0:2 * C], w_b[i])) for i in n]
    g_h = [_dot(bk_t[i], uv0[i]) for i in n]
    q_h = [_bf(rt_s[i] + _dot(a_r[i][:, 0:2 * C], w_b[i])) for i in n]
    y0 = [_dot(a_r[i], uv0[i]) for i in n]
    return [(m_h[i], g_h[i], dcol[i], q_h[i], y0[i]) for i in n]


def _scan_steps(local, H):
    C = CHUNK
    ys = []
    for m_h, g_h, dcol, q_h, y0 in local:
        h_b = _bf(H)
        Y = _dot(q_h, h_b) + y0
        ys.append(Y[0:C] + Y[C:2 * C])
        H = dcol * H + _dot(m_h, h_b) + g_h
    return ys, H


def _rwkv_scan_kernel(r_ref, k_ref, v_ref, al_ref, b_ref, ld_ref, o_ref, h_ref):
    C = CHUNK
    tc = r_ref.shape[0]

    @pl.when(pl.program_id(2) == 0)
    def _():
        h_ref[...] = jnp.zeros_like(h_ref)

    tri = jnp.where(lax.broadcasted_iota(jnp.int32, (C, C), 1) <= lax.broadcasted_iota(jnp.int32, (C, C), 0),
                    1.0, 0.0).astype(BF16)
    r2 = lax.broadcasted_iota(jnp.int32, (2 * C, 2 * C), 0)
    c2 = lax.broadcasted_iota(jnp.int32, (2 * C, 2 * C), 1)
    eye = (r2 == c2).astype(F32)
    strict = (c2 & (C - 1)) < (r2 & (C - 1))
    incl = (c2 & (C - 1)) <= (r2 & (C - 1))
    lane = lax.broadcasted_iota(jnp.int32, (C, LANES), 1)
    m0 = (lane < HEAD_DIM).astype(F32)
    m1 = 1.0 - m0
    nc = tc // C
    cum = _dot_split_rhs(tri, jnp.concatenate([ld_ref[c * C:(c + 1) * C, :] for c in range(nc)], axis=1), 3)
    chunks = []
    for c in range(nc):
        sl = slice(c * C, (c + 1) * C)
        chunks.append((r_ref[sl, :], k_ref[sl, :], v_ref[sl, :], al_ref[sl, :], b_ref[sl, :], ld_ref[sl, :],
                       cum[:, c * LANES:(c + 1) * LANES]))
    ys, H = _scan_steps(_scan_local(chunks, eye, strict, incl, m0, m1), h_ref[...])
    for c in range(nc):
        o_ref[c * C:(c + 1) * C, :] = ys[c]
    h_ref[...] = H


def _rwkv_scan(r, k, v, al, bb, ld, B, T, tc=512):
    N, W = r.shape
    nt = T // tc
    spec = pl.BlockSpec((tc, LANES), lambda b, h, t: (b * nt + t, h))
    return pl.pallas_call(
        _rwkv_scan_kernel,
        grid=(B, W // LANES, nt),
        in_specs=[spec] * 6,
        out_specs=spec,
        out_shape=jax.ShapeDtypeStruct((N, W), F32),
        scratch_shapes=[pltpu.VMEM((LANES, LANES), F32)],
        compiler_params=_params("arbitrary", "arbitrary", "arbitrary"),
        name="rwkv_scan",
    )(r, k, v, al, bb, ld)


def _rope_tables(pos):
    half = ROPE_DIM // 2
    inv = jnp.power(ROPE_THETA, -jnp.arange(half, dtype=F32) * 2.0 / ROPE_DIM)
    ang = pos.astype(F32)[:, None] * inv[None, :]
    cos, sin = jnp.cos(ang), jnp.sin(ang)
    n = pos.shape[0]
    rest = HEAD_DIM - ROPE_DIM
    c = jnp.concatenate([cos, cos, jnp.ones((n, rest), F32)], axis=1)
    s_dn = jnp.concatenate([-sin, jnp.zeros((n, half + rest), F32)], axis=1)
    s_up = jnp.concatenate([jnp.zeros((n, half), F32), sin, jnp.zeros((n, rest), F32)], axis=1)
    rep = LANES // HEAD_DIM
    return jnp.tile(c, (1, rep)), jnp.tile(s_dn, (1, rep)), jnp.tile(s_up, (1, rep))


def _norm_rope(x, bd, g, c, s_dn, s_up):
    width = x.shape[1]
    half = ROPE_DIM // 2
    rep = width // LANES
    tile = (lambda z: jnp.concatenate([z] * rep, axis=1)) if rep > 1 else (lambda z: z)
    ms = _dot_split_lhs(x * x, bd)
    xn = x * lax.rsqrt(ms + NORM_EPS) * g
    return (xn * tile(c) + pltpu.roll(xn, width - half, 1) * tile(s_dn)
            + pltpu.roll(xn, half, 1) * tile(s_up))


def _nsa_prep_kernel(q_ref, kv_ref, c_ref, sd_ref, su_ref, gq_ref, gs_ref, gw_ref, bdq_ref, bdk_ref,
                     o_qt, o_ks, o_kw, o_vst, o_vsd, o_vwt):
    c, sd, su = c_ref[...], sd_ref[...], su_ref[...]
    q = _norm_rope(q_ref[...], bdq_ref[...], gq_ref[...], c, sd, su) * Q_SCALE
    qt = q.T
    ts = q.shape[0]
    kv = kv_ref[...]
    bdk = bdk_ref[...]
    pos = pl.program_id(1) * ts + lax.broadcasted_iota(jnp.int32, (ts, LANES), 0)
    blk_onehot = jnp.where((pos >> SEL_SHIFT) == lax.broadcasted_iota(jnp.int32, (ts, LANES), 1), 1.0, 0.0)
    ks = _norm_rope(kv[:, 2 * LANES:3 * LANES], bdk, gs_ref[...], c, sd, su)
    o_ks[...] = jnp.concatenate([ks, blk_onehot], axis=1).astype(BF16)
    o_kw[...] = _norm_rope(kv[:, 4 * LANES:5 * LANES], bdk, gw_ref[...], c, sd, su).astype(BF16)
    ones_rows = jnp.where(lax.broadcasted_iota(jnp.int32, (V_ROWS - HEAD_DIM, q.shape[0]), 0) == 0, 1.0, 0.0)

    def values_t(x):
        xt = x.T
        return jnp.concatenate([xt[0:HEAD_DIM], ones_rows, xt[HEAD_DIM:2 * HEAD_DIM], ones_rows], axis=0)

    vst = values_t(kv[:, 3 * LANES:4 * LANES])
    vwt = values_t(kv[:, 5 * LANES:6 * LANES])
    for j in range(q.shape[0] // KEY_TILE):
        sl = slice(j * KEY_TILE, (j + 1) * KEY_TILE)
        o_qt[0, j] = qt[:, sl].astype(BF16)
        o_vsd[0, j] = vst[:, sl].astype(BF16)
        o_vwt[0, j] = vwt[:, sl].astype(BF16)
    for j in range(q.shape[0] // SEL_TILE):
        o_vst[0, j] = vst[:, j * SEL_TILE:(j + 1) * SEL_TILE].astype(BF16)


def _nsa_prep(q, kv, tables, qk_g, B, T, ts=512):
    N = q.shape[0]
    nt = T // ts
    nk = ts // KEY_TILE
    ns = ts // SEL_TILE
    row = lambda b, t: (b * nt + t, 0)
    full = lambda shape: pl.BlockSpec(shape, lambda b, t: (0,) * len(shape))
    tab = pl.BlockSpec((ts, LANES), lambda b, t: (t, 0))
    gq = jnp.tile(qk_g[0], NSA_Q_HEADS).reshape(1, NSA_WIDTH)
    gs = jnp.tile(qk_g[2], NSA_KV_HEADS).reshape(1, LANES)
    gw = jnp.tile(qk_g[3], NSA_KV_HEADS).reshape(1, LANES)
    tiled = lambda rows: pl.BlockSpec((1, nk, rows, KEY_TILE), lambda b, t: (b, t, 0, 0))
    return pl.pallas_call(
        _nsa_prep_kernel,
        grid=(B, nt),
        in_specs=[pl.BlockSpec((ts, NSA_WIDTH), row), pl.BlockSpec((ts, KV_COLS), row), tab, tab, tab,
                  full((1, NSA_WIDTH)), full((1, LANES)), full((1, LANES)),
                  full((NSA_WIDTH, NSA_WIDTH)), full((LANES, LANES))],
        out_specs=[tiled(NSA_WIDTH), pl.BlockSpec((ts, 2 * LANES), row), pl.BlockSpec((ts, LANES), row),
                   pl.BlockSpec((1, ns, NSA_KV_HEADS * V_ROWS, SEL_TILE), lambda b, t: (b, t, 0, 0)),
                   tiled(NSA_KV_HEADS * V_ROWS), tiled(NSA_KV_HEADS * V_ROWS)],
        out_shape=[jax.ShapeDtypeStruct((B, T // KEY_TILE, NSA_WIDTH, KEY_TILE), BF16),
                   jax.ShapeDtypeStruct((N, 2 * LANES), BF16), jax.ShapeDtypeStruct((N, LANES), BF16),
                   jax.ShapeDtypeStruct((B, T // SEL_TILE, NSA_KV_HEADS * V_ROWS, SEL_TILE), BF16),
                   jax.ShapeDtypeStruct((B, T // KEY_TILE, NSA_KV_HEADS * V_ROWS, KEY_TILE), BF16),
                   jax.ShapeDtypeStruct((B, T // KEY_TILE, NSA_KV_HEADS * V_ROWS, KEY_TILE), BF16)],
        compiler_params=_params("arbitrary", "arbitrary"),
        name="nsa_prep",
    )(q, kv, *tables, gq, gs, gw, _head_block_diag(NSA_WIDTH, 1.0 / HEAD_DIM),
      _head_block_diag(LANES, 1.0 / HEAD_DIM))


def _gelu_tanh(x):
    return 0.5 * x * (1.0 + jnp.tanh(0.7978845608028654 * (x + 0.044715 * x * x * x)))


def _nsa_cmp_kernel(x_ref, pos_ref, w1_ref, w2_ref, *rest, is_key):
    if is_key:
        g_ref, c_ref, sd_ref, su_ref, bd_ref, o_ref, xs_ref = rest
    else:
        o_ref, xs_ref = rest
    nch = xs_ref.shape[0]
    S = CMP_STRIDE
    for j in range(S):
        xs_ref[:, j * LANES:(j + 1) * LANES] = x_ref[0, pl.ds(j, nch, stride=S), :]
    xs = xs_ref[...]
    first = _dot((xs + pos_ref[0:1, :]).astype(BF16), w1_ref[0])
    second = _dot((xs + pos_ref[1:2, :]).astype(BF16), w1_ref[1])
    hid = first + pltpu.roll(second, nch - 1, 0)
    out = _dot(_gelu_tanh(hid).astype(BF16), w2_ref[...])
    rows = lax.broadcasted_iota(jnp.int32, out.shape, 0)
    if is_key:
        out = _norm_rope(out, bd_ref[...], g_ref[...], c_ref[...], sd_ref[...], su_ref[...])
        o_ref[0] = jnp.where(rows < nch - 1, out, 0.0).astype(BF16)
    else:
        o_ref[0] = jnp.where(rows < nch - 1, out, 0.0).T.astype(BF16)


def _nsa_cmp(kv3, which, cmp_pos, cmp_w1, cmp_w2, g_k, tables_cmp):
    B, T, _ = kv3.shape
    S = CMP_STRIDE
    nch = T // S
    is_key = which == 0
    eye2 = jnp.eye(NSA_KV_HEADS, dtype=F32)
    w1 = cmp_w1[which].reshape(CMP_BLOCK, HEAD_DIM, CMP_HIDDEN)
    w1 = jnp.einsum('jdh,ge->jgdeh', w1, eye2).reshape(2, S * LANES, NSA_KV_HEADS * CMP_HIDDEN)
    w2 = jnp.einsum('hd,ge->ghed', cmp_w2[which], eye2).reshape(NSA_KV_HEADS * CMP_HIDDEN, LANES)
    pos = jnp.tile(cmp_pos[which].reshape(2, S, 1, HEAD_DIM), (1, 1, NSA_KV_HEADS, 1)).reshape(2, S * LANES)
    full = lambda shape: pl.BlockSpec(shape, lambda b: (0,) * len(shape))
    in_specs = [pl.BlockSpec((1, T, LANES), lambda b: (b, 0, which)), full(pos.shape), full(w1.shape),
                full(w2.shape)]
    args = [kv3, pos, w1.astype(BF16), w2.astype(BF16)]
    if is_key:
        in_specs += [full((1, LANES)), full((nch, LANES)), full((nch, LANES)), full((nch, LANES)),
                     full((LANES, LANES))]
        args += [jnp.tile(g_k, NSA_KV_HEADS).reshape(1, LANES), *tables_cmp,
                 _head_block_diag(LANES, 1.0 / HEAD_DIM)]
        out_spec = pl.BlockSpec((1, nch, LANES), lambda b: (b, 0, 0))
        out_shape = jax.ShapeDtypeStruct((B, nch, LANES), BF16)
    else:
        out_spec = pl.BlockSpec((1, LANES, nch), lambda b: (b, 0, 0))
        out_shape = jax.ShapeDtypeStruct((B, LANES, nch), BF16)
    return pl.pallas_call(
        functools.partial(_nsa_cmp_kernel, is_key=is_key),
        grid=(B,),
        in_specs=in_specs,
        out_specs=out_spec,
        out_shape=out_shape,
        scratch_shapes=[pltpu.VMEM((nch, S * LANES), F32)],
        compiler_params=_params("arbitrary"),
        name="nsa_cmp_k" if is_key else "nsa_cmp_v",
    )(*args)


def _nsa_attn_kernel(qt_ref, kc_ref, vct_ref, ks_ref, vst_ref, vsd_ref, kw_ref, vwt_ref, gt_ref, ov_ref, o_ref,
                     rhs_ref, s0_ref, s1_ref, s2_ref, s3_ref, p0_ref, p1_ref):
    g = pl.program_id(1)
    qb = pl.program_id(2)
    R = NSA_GROUP
    QT = Q_TILE
    KT = KEY_TILE
    NQ = R * QT
    t0 = qb * QT
    n_cmp_pad = kc_ref.shape[1]
    n_sel = ov_ref.shape[0]

    q_g = jnp.concatenate([qt_ref[0, 0, r * HEAD_DIM:(r + 1) * HEAD_DIM, :] for r in range(R)], axis=1)
    q2 = jnp.concatenate([q_g, q_g], axis=0)
    row_grp = lax.broadcasted_iota(jnp.int32, q2.shape, 0) // HEAD_DIM
    qpad = jnp.where(row_grp == g, q2, jnp.zeros_like(q2))

    tq_row = t0 + (lax.broadcasted_iota(jnp.int32, (1, NQ), 1) & (QT - 1))
    tile4 = lambda z: jnp.concatenate([z] * R, axis=1)

    sc = _dot(kc_ref[0], qpad)
    n_i = lax.broadcasted_iota(jnp.int32, (n_cmp_pad, 1), 0)
    cend = jnp.where(n_i < n_cmp_pad - 1, n_i * CMP_STRIDE + (CMP_BLOCK - 1), jnp.int32(2 ** 30))
    cvalid = cend <= tq_row
    sc = jnp.where(cvalid, sc, NEG_INF)
    mc = jnp.max(sc, axis=0, keepdims=True)
    ec = jnp.where(cvalid, jnp.exp2(sc - mc), 0.0)
    pc = ec / jnp.maximum(jnp.sum(ec, axis=0, keepdims=True), F32_TINY)
    o_c = _dot(vct_ref[0], pc.astype(BF16))
    pc_sum = pc[:, 0:QT]
    for r in range(1, R):
        pc_sum = pc_sum + pc[:, r * QT:(r + 1) * QT]
    imp = _dot_split_rhs(ov_ref[...], pc_sum)

    ji = lax.broadcasted_iota(jnp.int32, (n_sel, QT), 0)
    jf = ji.astype(F32)
    tq_sel = t0 + lax.broadcasted_iota(jnp.int32, (n_sel, QT), 1)
    cur = tq_sel >> SEL_SHIFT
    forced = (ji == 0) | (ji == cur) | (ji == cur - 1)
    valid = ji * SEL_BLOCK <= tq_sel
    score = jnp.where(valid, jnp.where(forced, FORCE_SCORE, imp), -1.0)
    sel = jnp.zeros((n_sel, QT), F32)
    for _ in range(min(SEL_TOPK, n_sel)):
        mx = jnp.max(score, axis=0, keepdims=True)
        jmin = jnp.min(jnp.where(score == mx, jf, 1e9), axis=0, keepdims=True)
        hit = jf == jmin
        sel = jnp.where(hit, 1.0, sel)
        score = jnp.where(hit, -3e38, score)

    ST = SEL_TILE
    bias_all = (jnp.where(valid, sel, 0.0) - 1.0) * (-NEG_INF)
    first_own = t0 // SEL_BLOCK

    def with_bias_rows(bias):
        rows = tile4(bias).astype(BF16)
        if n_sel < LANES:
            rows = jnp.concatenate([rows, jnp.zeros((LANES - n_sel, NQ), BF16)], axis=0)
        return jnp.concatenate([qpad, rows], axis=0)

    rhs_ref[...] = with_bias_rows(jnp.where(ji < first_own, bias_all, NEG_INF))
    n_tiles = (t0 + ST - 1) // ST
    last_tile = ks_ref.shape[1] // ST - 1
    p_bufs = (p0_ref, p1_ref)

    def sel_scores(kt, s_ref):
        k0 = pl.multiple_of(jnp.minimum(kt, last_tile) * ST, ST)
        s_ref[...] = _dot(ks_ref[0, pl.ds(k0, ST), :], rhs_ref[...])

    def sel_values(kt, slot, acc, alpha):
        return acc * alpha + _dot(vst_ref[0, jnp.clip(kt, 0, last_tile)], p_bufs[slot][...])

    def sel_softmax(s_ref, slot, m):
        s = s_ref[...]
        m_new = jnp.maximum(m, jnp.max(s, axis=0, keepdims=True))
        p_bufs[slot][...] = jnp.exp2(s - m_new).astype(BF16)
        return m_new, jnp.exp2(m - m_new)

    def sel_pair(a, carry, s_now, s_next):
        m, acc, alpha0, alpha1 = carry
        acc = sel_values(a - 2, 0, acc, alpha0)
        acc = sel_values(a - 1, 1, acc, alpha1)
        sel_scores(a + 2, s_next[0])
        sel_scores(a + 3, s_next[1])
        m, alpha0 = sel_softmax(s_now[0], 0, m)
        m, alpha1 = sel_softmax(s_now[1], 1, m)
        return m, acc, alpha0, alpha1

    bufs_a, bufs_b = (s0_ref, s1_ref), (s2_ref, s3_ref)
    sel_scores(0, s0_ref)
    sel_scores(1, s1_ref)
    p0_ref[...] = jnp.zeros_like(p0_ref)
    p1_ref[...] = jnp.zeros_like(p1_ref)
    n_pairs = (n_tiles + 1) // 2
    one = jnp.ones((1, NQ), F32)
    m_s, acc_s, alpha0, alpha1 = lax.fori_loop(
        0, n_pairs,
        lambda j, carry: lax.cond(j % 2 == 0,
                                  lambda c: sel_pair(2 * j, c, bufs_a, bufs_b),
                                  lambda c: sel_pair(2 * j, c, bufs_b, bufs_a), carry),
        (jnp.full((1, NQ), NEG_INF, F32), jnp.zeros((V_ROWS, NQ), F32), one, one))
    acc_s = sel_values(2 * n_pairs - 2, 0, acc_s, alpha0)
    acc_s = sel_values(2 * n_pairs - 1, 1, acc_s, alpha1)
    own = _dot(ks_ref[0, pl.ds(pl.multiple_of(t0, QT), QT), :], with_bias_rows(bias_all))
    seen = lax.broadcasted_iota(jnp.int32, (QT, QT), 0) <= lax.broadcasted_iota(jnp.int32, (QT, QT), 1)
    own = jnp.where(tile4(seen), own, NEG_INF)
    m_new = jnp.maximum(m_s, jnp.max(own, axis=0, keepdims=True))
    acc_s = acc_s * jnp.exp2(m_s - m_new) + _dot(vsd_ref[0, qb], jnp.exp2(own - m_new).astype(BF16))

    n_wt = (WINDOW + QT) // KT
    k0w = pl.multiple_of(jnp.maximum(t0 - WINDOW, 0), KT)
    kt_w = k0w // KT
    keys_w = kw_ref[0, pl.ds(k0w, WINDOW + QT), :]
    dw = (t0 + lax.broadcasted_iota(jnp.int32, (WINDOW + QT, QT), 1)
          - (k0w + lax.broadcasted_iota(jnp.int32, (WINDOW + QT, QT), 0)))
    sw = _dot(keys_w, qpad) + tile4(jnp.where(dw >= 0, jnp.where(dw < WINDOW, 0.0, NEG_INF), NEG_INF))
    pw = jnp.exp2(sw - jnp.max(sw, axis=0, keepdims=True)).astype(BF16)
    acc_w = _dot(vwt_ref[0, kt_w], pw[0:KT])
    for j in range(1, n_wt):
        acc_w = acc_w + _dot(vwt_ref[0, kt_w + j], pw[j * KT:(j + 1) * KT])

    gates = _sigmoid(gt_ref[0, 0])
    grow = lambda j: jnp.concatenate([gates[j, r:r + 1, :] for r in range(R)], axis=1)
    D = HEAD_DIM
    o = (grow(0) * o_c + grow(1) * (acc_s[0:D] / acc_s[D:D + 1])
         + grow(2) * (acc_w[0:D] / acc_w[D:D + 1]))
    halves = []
    for h in range(R // 2):
        pair = jnp.concatenate([o[:, (2 * h) * QT:(2 * h + 1) * QT],
                                o[:, (2 * h + 1) * QT:(2 * h + 2) * QT]], axis=0)
        halves.append(pair.T)
    o_ref[...] = jnp.concatenate(halves, axis=1)


def _nsa_attn(qt, kcmp, vct, ks3, vst, vsd, kw3, vwt, gt, ov_t, B, T):
    G, R = NSA_KV_HEADS, NSA_GROUP
    nq = T // Q_TILE
    nk = T // KEY_TILE
    nch = kcmp.shape[1]
    n_sel = ov_t.shape[0]
    assert (T // SEL_TILE) % 2 == 0 and n_sel <= LANES and Q_TILE == KEY_TILE
    return pl.pallas_call(
        _nsa_attn_kernel,
        grid=(B, G, nq),
        in_specs=[pl.BlockSpec((1, 1, R * HEAD_DIM, Q_TILE), lambda b, g, q: (b, q, g, 0)),
                  pl.BlockSpec((1, nch, LANES), lambda b, g, q: (b, 0, 0)),
                  pl.BlockSpec((1, HEAD_DIM, nch), lambda b, g, q: (b, g, 0)),
                  pl.BlockSpec((1, T, 2 * LANES), lambda b, g, q: (b, 0, 0)),
                  pl.BlockSpec((1, T // SEL_TILE, V_ROWS, SEL_TILE), lambda b, g, q: (b, 0, g, 0)),
                  pl.BlockSpec((1, nk, V_ROWS, KEY_TILE), lambda b, g, q: (b, 0, g, 0)),
                  pl.BlockSpec((1, T, LANES), lambda b, g, q: (b, 0, 0)),
                  pl.BlockSpec((1, nk, V_ROWS, KEY_TILE), lambda b, g, q: (b, 0, g, 0)),
                  pl.BlockSpec((1, 1, 3, R, Q_TILE), lambda b, g, q: (b, g, 0, 0, q)),
                  pl.BlockSpec((n_sel, nch), lambda b, g, q: (0, 0))],
        out_specs=pl.BlockSpec((Q_TILE, R * HEAD_DIM), lambda b, g, q: (b * nq + q, g)),
        out_shape=jax.ShapeDtypeStruct((B * T, NSA_WIDTH), F32),
        scratch_shapes=[pltpu.VMEM((2 * LANES, R * Q_TILE), BF16),
                        *[pltpu.VMEM((SEL_TILE, R * Q_TILE), F32)] * 4,
                        *[pltpu.VMEM((SEL_TILE, R * Q_TILE), BF16)] * 2],
        compiler_params=_params("arbitrary", "arbitrary", "arbitrary"),
        name="nsa_attn",
    )(qt, kcmp, vct, ks3, vst, vsd, kw3, vwt, gt, ov_t)


def _first_index_of(vals, target):
    idx = jnp.full_like(target, float(len(vals) - 1))
    for i in range(len(vals) - 2, -1, -1):
        idx = jnp.where(vals[i] == target, float(i), idx)
    return idx


def _pick(vals, idx):
    out = vals[-1]
    for i in range(len(vals) - 2, -1, -1):
        out = jnp.where(idx == float(i), vals[i], out)
    return out


def _route_rows(score, bias):
    E, G, P = N_EXPERTS, N_GROUPS, EXPERTS_PER_GROUP
    sel = score + bias
    s = [sel[e:e + 1, :] for e in range(E)]
    raw = [score[e:e + 1, :] for e in range(E)]
    grp = []
    for gi in range(G):
        a = s[gi * P:(gi + 1) * P]
        best = None
        for i in range(P):
            for j in range(i + 1, P):
                pair = a[i] + a[j]
                best = pair if best is None else jnp.maximum(best, pair)
        grp.append(best)
    gmax = functools.reduce(jnp.maximum, grp)
    g_star = _first_index_of(grp, gmax)
    v = [_pick([s[gi * P + i] for gi in range(G)], g_star) for i in range(P)]
    w = [_pick([raw[gi * P + i] for gi in range(G)], g_star) for i in range(P)]
    i1 = _first_index_of(v, functools.reduce(jnp.maximum, v))
    v2 = [jnp.where(i1 == float(i), -jnp.inf, v[i]) for i in range(P)]
    i2 = _first_index_of(v2, functools.reduce(jnp.maximum, v2))
    w1, w2 = _pick(w, i1), _pick(w, i2)
    tot = w1 + w2
    zero = jnp.zeros_like(tot)
    return jnp.concatenate([g_star * P + i1, g_star * P + i2, w1 / tot, w2 / tot, zero, zero, zero, zero],
                           axis=0)


def _merge_kernel(ys_ref, g_ref, bonus_ref, gng_ref, gnb_ref, bd_ref, yb_ref, pm_ref, x_ref, mod_ref,
                  ng_ref, wa_ref, wb_ref, wo_ref, rw_ref, rb_ref, o_x, o_h, o_route):
    m = mod_ref[pl.program_id(0)]
    bd = bd_ref[...]
    y = ys_ref[...]
    mean = _dot_split_lhs(y, bd)
    yc = y - mean
    var = _dot_split_lhs(yc * yc, bd)
    ya = (yc * lax.rsqrt(var + RWKV_GN_EPS) * gng_ref[...] + gnb_ref[...] + bonus_ref[...]) * g_ref[...]
    pm = pm_ref[...]
    D = x_ref.shape[1]
    mix = (_sigmoid(pm[:, 0:D]) * _dot(ya.astype(BF16), wa_ref[...])
           + _sigmoid(pm[:, D:2 * D]) * _dot(yb_ref[...].astype(BF16), wb_ref[...]))
    x = x_ref[...] + m[2:3] * _dot(mix.astype(BF16), wo_ref[...])
    o_x[...] = x
    ms = jnp.mean(x * x, axis=-1, keepdims=True)
    h = x * lax.rsqrt(ms + NORM_EPS) * ng_ref[...]
    h = h * (1.0 + m[4:5]) + m[3:4]
    o_h[...] = h
    score = _sigmoid(_dot_3pass(h, rw_ref).T[0:N_EXPERTS, :])
    o_route[...] = _route_rows(score, rb_ref[...])


def _merge(ys, g, bonus, gn_g, gn_b, yb, pm, x2, mod, ng, wa, wb, wo, router_w, router_b, B, T, tm=256):
    N, D = x2.shape
    W = RWKV_WIDTH
    nt = T // tm
    row = lambda b, t: (b * nt + t, 0)
    full = lambda shape: pl.BlockSpec(shape, lambda b, t: (0,) * len(shape))
    return pl.pallas_call(
        _merge_kernel,
        grid=(B, nt),
        in_specs=[pl.BlockSpec((tm, W), row), pl.BlockSpec((tm, W), row), pl.BlockSpec((tm, W), row),
                  full((1, W)), full((1, W)), full((W, W)),
                  pl.BlockSpec((tm, NSA_WIDTH), row), pl.BlockSpec((tm, 2 * D), row),
                  pl.BlockSpec((tm, D), row), full((B, 6, D)), full((1, D)),
                  full((W, D)), full((NSA_WIDTH, D)), full((D, D)), full((2, D, LANES)),
                  full((N_EXPERTS, 1))],
        out_specs=[pl.BlockSpec((tm, D), row), pl.BlockSpec((tm, D), row),
                   pl.BlockSpec((8, tm), lambda b, t: (0, b * nt + t))],
        out_shape=[jax.ShapeDtypeStruct((N, D), F32), jax.ShapeDtypeStruct((N, D), F32),
                   jax.ShapeDtypeStruct((8, N), F32)],
        compiler_params=_params("arbitrary", "arbitrary"),
        name="merge_out",
    )(ys, g, bonus, gn_g.reshape(1, W), gn_b.reshape(1, W), _head_block_diag(W, 1.0 / HEAD_DIM),
      yb, pm, x2, mod, ng, wa, wb, wo,
      _hi_lo(jnp.zeros((D, LANES), F32).at[:, :N_EXPERTS].set(router_w)), router_b.reshape(N_EXPERTS, 1))


def _route(route, N):
    wts = route[TOP_K:2 * TOP_K].T
    NK = N * TOP_K
    e_flat = route[0:TOP_K].astype(jnp.int32).reshape(-1)
    onehot = (e_flat[:, None] == jnp.arange(N_EXPERTS, dtype=jnp.int32)[None, :]).astype(jnp.int32)
    csum = jnp.cumsum(onehot, axis=0)
    counts = csum[-1]
    rank = jnp.take_along_axis(csum, e_flat[:, None], axis=1)[:, 0] - 1
    padded = (counts + MOE_BLOCK - 1) // MOE_BLOCK * MOE_BLOCK
    pad_end = jnp.cumsum(padded)
    pad_start = pad_end - padded
    dest = pad_start[e_flat] + rank
    n_blk = -(-NK // MOE_BLOCK) + N_EXPERTS
    blk_start = jnp.arange(n_blk, dtype=jnp.int32) * MOE_BLOCK
    blk_expert = jnp.clip(jnp.searchsorted(pad_end, blk_start, side='right'), 0, N_EXPERTS - 1).astype(jnp.int32)
    blk_valid = jnp.clip((pad_start + counts)[blk_expert] - blk_start, 0, MOE_BLOCK).astype(jnp.int32)
    dest = jnp.pad(dest.astype(jnp.int32).reshape(NK // SC_WINDOW, SC_WINDOW), ((0, 0), (0, LANES - SC_WINDOW)))
    return wts, dest, blk_expert, blk_valid, n_blk


SC_WINDOW = 32


def _sc_mesh():
    return plsc.VectorSubcoreMesh(core_axis_name="c", subcore_axis_name="s")


def _sc_dispatch(h, dest, n_slots):
    N, D = h.shape
    W = SC_WINDOW
    nw = N // W

    @pl.kernel(out_type=jax.ShapeDtypeStruct((n_slots, D), h.dtype), mesh=_sc_mesh(), scratch_types=[])
    def dispatch(h_hbm, i_hbm, o_hbm):
        def body(x_vmem, i_vmem):
            pltpu.sync_copy(x_vmem, o_hbm.at[i_vmem.at[0, pl.ds(0, W)]])

        pltpu.emit_pipeline(
            body, grid=(TOP_K, nw),
            in_specs=[pl.BlockSpec((W, D), lambda k, i: (i, 0)),
                      pl.BlockSpec((1, LANES), lambda k, i: (k * nw + i, 0))],
            out_specs=[], core_axis_name=("c", "s"),
            dimension_semantics=(pltpu.PARALLEL, pltpu.PARALLEL))(h_hbm, i_hbm)

    return dispatch(h, dest)


def _sc_collect(ys, dest):
    W = SC_WINDOW
    NK = dest.shape[0] * W
    D = ys.shape[1]
    half = NK // TOP_K // W

    @pl.kernel(out_type=jax.ShapeDtypeStruct((NK, D), ys.dtype), mesh=_sc_mesh(), scratch_types=[])
    def collect(y_hbm, i_hbm, o_hbm):
        def body(i_vmem, o_vmem):
            pltpu.sync_copy(y_hbm.at[i_vmem.at[0, pl.ds(0, W)]], o_vmem)

        pltpu.emit_pipeline(
            body, grid=(TOP_K, half),
            in_specs=[pl.BlockSpec((1, LANES), lambda k, i: (k * half + i, 0))],
            out_specs=[pl.BlockSpec((W, D), lambda k, i: (k * half + i, 0))],
            core_axis_name=("c", "s"),
            dimension_semantics=(pltpu.PARALLEL, pltpu.PARALLEL))(i_hbm, o_hbm)

    return collect(ys, dest)


def _moe_dense_kernel(be_ref, nv_ref, x_ref, wg_ref, wu_ref, wd_ref, o_ref):
    nv = nv_ref[pl.program_id(0)]

    @pl.when(nv > 0)
    def _():
        x = x_ref[...].astype(BF16)
        gate = _dot(x, wg_ref[0])
        up = _dot(x, wu_ref[0])
        o_ref[...] = _dot((gate * _sigmoid(gate) * up).astype(BF16), wd_ref[0])

    @pl.when(nv == 0)
    def _():
        o_ref[...] = jnp.zeros_like(o_ref)


def _moe_dense(xs, blk_expert, blk_valid, n_blk, wg, wu, wd):
    P, D = xs.shape
    DE = wg.shape[2]
    wmap = lambda i, be, nv: (be[i], 0, 0)
    grid_spec = pltpu.PrefetchScalarGridSpec(
        num_scalar_prefetch=2,
        grid=(n_blk,),
        in_specs=[pl.BlockSpec((MOE_BLOCK, D), lambda i, be, nv: (i, 0)), pl.BlockSpec((1, D, DE), wmap),
                  pl.BlockSpec((1, D, DE), wmap), pl.BlockSpec((1, DE, D), wmap)],
        out_specs=pl.BlockSpec((MOE_BLOCK, D), lambda i, be, nv: (i, 0)),
    )
    return pl.pallas_call(
        _moe_dense_kernel,
        grid_spec=grid_spec,
        out_shape=jax.ShapeDtypeStruct((P, D), F32),
        compiler_params=_params("arbitrary"),
        name="moe_experts",
    )(blk_expert, blk_valid, xs, wg, wu, wd)


MOE_ISSUE_UNROLL = 8


def _moe_kernel(be_ref, nv_ref, tok_ref, dst_ref, h_hbm, wg_ref, wu_ref, wd_ref, o_hbm, xbuf, ybuf, sem_in,
                sem_out):
    i = pl.program_id(0)
    MB = MOE_BLOCK
    U = MOE_ISSUE_UNROLL
    base = i * MB
    nv = nv_ref[i]

    @pl.when(i == 0)
    def _():
        xbuf[...] = jnp.zeros_like(xbuf)

    def gather_row(r, priority):
        pltpu.make_async_copy(h_hbm.at[pl.ds(tok_ref[base + r], 1), :], xbuf.at[pl.ds(r, 1), :],
                              sem_in).start(priority=priority)

    def scatter_row(r, priority):
        pltpu.make_async_copy(ybuf.at[pl.ds(r, 1), :], o_hbm.at[pl.ds(dst_ref[base + r], 1), :],
                              sem_out).start(priority=priority)

    def issue_rows(row_fn):
        def group(gi, _):
            for u in range(U):
                row_fn(gi * U + u, u % 2)
            return 0

        lax.fori_loop(0, nv // U, group, 0)

        def tail(r, _):
            row_fn(r, 0)
            return 0

        lax.fori_loop((nv // U) * U, nv, tail, 0)

    def wait_rows(descriptor):
        for bit in range(MB.bit_length()):
            n = 1 << bit

            @pl.when((nv & n) != 0)
            def _():
                descriptor(n).wait()

    @pl.when(nv > 0)
    def _():
        issue_rows(gather_row)
        wait_rows(lambda n: pltpu.make_async_copy(h_hbm.at[pl.ds(0, n), :], xbuf.at[pl.ds(0, n), :], sem_in))
        x = xbuf[...].astype(BF16)
        gate = _dot(x, wg_ref[0])
        up = _dot(x, wu_ref[0])
        act = (gate * _sigmoid(gate) * up).astype(BF16)
        ybuf[...] = _dot(act, wd_ref[0])
        issue_rows(scatter_row)
        wait_rows(lambda n: pltpu.make_async_copy(ybuf.at[pl.ds(0, n), :], o_hbm.at[pl.ds(0, n), :], sem_out))


def _moe(h, slot_tok, slot_assign, blk_expert, blk_valid, n_blk, wg, wu, wd):
    N, D = h.shape
    DE = wg.shape[2]
    wmap = lambda i, be, nv, tok, dst: (be[i], 0, 0)
    grid_spec = pltpu.PrefetchScalarGridSpec(
        num_scalar_prefetch=4,
        grid=(n_blk,),
        in_specs=[pl.BlockSpec(memory_space=pl.ANY), pl.BlockSpec((1, D, DE), wmap),
                  pl.BlockSpec((1, D, DE), wmap), pl.BlockSpec((1, DE, D), wmap)],
        out_specs=pl.BlockSpec(memory_space=pl.ANY),
        scratch_shapes=[pltpu.VMEM((MOE_BLOCK, D), F32), pltpu.VMEM((MOE_BLOCK, D), F32),
                        pltpu.SemaphoreType.DMA(()), pltpu.SemaphoreType.DMA(())],
    )
    return pl.pallas_call(
        _moe_kernel,
        grid_spec=grid_spec,
        out_shape=jax.ShapeDtypeStruct((TOP_K * N, D), F32),
        compiler_params=_params("arbitrary"),
        name="moe_experts",
    )(blk_expert, blk_valid, slot_tok, slot_assign, h, wg, wu, wd)


def _final_kernel(x_ref, y0_ref, y1_ref, w_ref, mod_ref, o_ref):
    m = mod_ref[pl.program_id(0)]
    w = w_ref[...]
    o_ref[...] = x_ref[...] + m[5:6] * (w[:, 0:1] * y0_ref[...] + w[:, 1:2] * y1_ref[...])


def _final(x2, ybuf, wts, mod, B, T, tm=512):
    N, D = x2.shape
    nt = T // tm
    row = lambda b, t: (b * nt + t, 0)
    return pl.pallas_call(
        _final_kernel,
        grid=(B, nt),
        in_specs=[pl.BlockSpec((tm, D), row), pl.BlockSpec((tm, D), row),
                  pl.BlockSpec((tm, D), lambda b, t: (N // tm + b * nt + t, 0)),
                  pl.BlockSpec((tm, TOP_K), row), pl.BlockSpec((B, 6, D), lambda b, t: (0, 0, 0))],
        out_specs=pl.BlockSpec((tm, D), row),
        out_shape=jax.ShapeDtypeStruct((N, D), F32),
        compiler_params=_params("arbitrary", "arbitrary"),
        name="moe_combine",
    )(x2, ybuf, ybuf, wts, mod)


def _overlap_t(n_sel, n_cmp_pad):
    ci = jnp.arange(n_cmp_pad)[None, :] * CMP_STRIDE
    sj = jnp.arange(n_sel)[:, None] * SEL_BLOCK
    ov = (ci <= sj + SEL_BLOCK - 1) & (ci + CMP_BLOCK - 1 >= sj) & (jnp.arange(n_cmp_pad)[None, :] < n_cmp_pad - 1)
    return ov.astype(BF16)


def kernel(x, c, w_ada, b_ada, norm_g, w_in, b_in, rwkv_mu, rwkv_w0, rwkv_w2, rwkv_a0, rwkv_a2, rwkv_g2,
           rwkv_k_k, rwkv_k_a, rwkv_r_k, rwkv_gn_g, rwkv_gn_b, qk_norm_g, cmp_pos, cmp_w1, cmp_w2,
           w_up_rwkv, w_up_nsa, w_out, router_w, router_b, exp_w_gate, exp_w_up, exp_w_down):
    B, T, D = x.shape
    L = w_ada.shape[0]
    N = B * T
    mods = _ada(c, w_ada, b_ada)
    tables = _rope_tables(jnp.arange(T, dtype=jnp.int32))
    nch = T // CMP_STRIDE
    tables_cmp = _rope_tables(jnp.arange(nch, dtype=jnp.int32) * CMP_STRIDE + CMP_BLOCK - 1)
    ov_t = _overlap_t(T // SEL_BLOCK, nch)
    n_gate = NSA_GATE_COLS
    x2 = x.reshape(N, D)
    for l in range(L):
        g0 = _SEG_KV[1] + n_gate
        w_pad = jnp.concatenate([w_in[l][:, :g0], jnp.zeros((D, GATE_PAD - n_gate), F32), w_in[l][:, g0:]],
                                axis=1).astype(BF16)
        b_pad = jnp.concatenate([b_in[l][:g0], jnp.zeros((GATE_PAD - n_gate,), F32), b_in[l][g0:]]).reshape(1, -1)
        p_rw, p_q, p_kv, p_gate, p_merge = _inproj(x2, mods[l], norm_g[l, 0].reshape(1, D), w_pad, b_pad, B, T)
        r, k, v, al, bb, ld, g, bonus = _rwkv_pre(p_rw, rwkv_mu[l], rwkv_w0[l], rwkv_w2[l], rwkv_a0[l],
                                                  rwkv_a2[l], rwkv_g2[l], rwkv_k_k[l], rwkv_k_a[l],
                                                  rwkv_r_k[l], B, T)
        ys = _rwkv_scan(r, k, v, al, bb, ld, B, T)
        qt, ks, kw, vst, vsd, vwt = _nsa_prep(p_q, p_kv, tables, qk_norm_g[l], B, T)
        kv3 = p_kv.reshape(B, T, KV_COLS)
        kcmp = _nsa_cmp(kv3, 0, cmp_pos[l], cmp_w1[l], cmp_w2[l], qk_norm_g[l, 1], tables_cmp)
        vct = _nsa_cmp(kv3, 1, cmp_pos[l], cmp_w1[l], cmp_w2[l], None, None)
        gt = p_gate[:, :n_gate].reshape(B, T, NSA_KV_HEADS, NSA_GROUP, 3).transpose(0, 2, 4, 3, 1)
        yb = _nsa_attn(qt, kcmp, vct, ks.reshape(B, T, 2 * LANES), vst, vsd, kw.reshape(B, T, LANES), vwt, gt, ov_t,
                       B, T)
        x2, h2, route = _merge(ys, g, bonus, rwkv_gn_g[l], rwkv_gn_b[l], yb, p_merge, x2, mods[l],
                               norm_g[l, 1].reshape(1, D), w_up_rwkv[l].astype(BF16),
                               w_up_nsa[l].astype(BF16), w_out[l].astype(BF16), router_w, router_b, B, T)
        wts, dest, blk_expert, blk_valid, n_blk = _route(route, N)
        xs = _sc_dispatch(h2, dest, n_blk * MOE_BLOCK)
        ys = _moe_dense(xs, blk_expert, blk_valid, n_blk, exp_w_gate[l].astype(BF16),
                        exp_w_up[l].astype(BF16), exp_w_down[l].astype(BF16))
        ybuf = _sc_collect(ys, dest)
        x2 = _final(x2, ybuf, wts, mods[l], B, T)
    return x2.reshape(B, T, D)
```

```python
import functools
import math

import jax
import jax.numpy as jnp
from jax import lax
from jax.experimental import pallas as pl
from jax.experimental.pallas import tpu as pltpu
from jax.experimental.pallas import tpu_sc as plsc

F32 = jnp.float32
BF16 = jnp.bfloat16
HI = lax.Precision.HIGHEST

D_MODEL = 1024
RWKV_HEADS = 8
HEAD_DIM = 64
RWKV_WIDTH = RWKV_HEADS * HEAD_DIM
DECAY_LORA = 64
ICLR_LORA = 64
GATE_LORA = 128
RWKV_GN_EPS = 64e-5
RWKV_COLS = 3 * RWKV_WIDTH + DECAY_LORA + ICLR_LORA + GATE_LORA

NSA_Q_HEADS = 8
NSA_KV_HEADS = 2
NSA_GROUP = NSA_Q_HEADS // NSA_KV_HEADS
NSA_WIDTH = NSA_Q_HEADS * HEAD_DIM
CMP_STRIDE = 16
CMP_BLOCK = 2 * CMP_STRIDE
CMP_HIDDEN = 256
SEL_BLOCK = 64
SEL_SHIFT = 6
SEL_TOPK = 16
WINDOW = 512
FORCE_SCORE = 1e4
NEG_INF = -1e30
ROPE_THETA = 500000.0
ROPE_DIM = HEAD_DIM // 4
KV_COLS = 6 * NSA_KV_HEADS * HEAD_DIM
NSA_GATE_COLS = 3 * NSA_Q_HEADS
GATE_PAD = 128

N_EXPERTS = 16
N_GROUPS = 4
EXPERTS_PER_GROUP = N_EXPERTS // N_GROUPS
TOP_K = 2
D_EXPERT = 512
MOE_BLOCK = 256
NORM_EPS = 1e-6

LANES = 128
CHUNK = 64
KEY_TILE = 128
SEL_TILE = 512
CMP_VARIANTS = 4
V_ROWS = 80
Q_SCALE = HEAD_DIM ** -0.5 * math.log2(math.e)
Q_TILE = 128
F32_TINY = float(jnp.finfo(jnp.float32).tiny)

_SEG_RW = (0, RWKV_COLS)
_SEG_Q = (_SEG_RW[1], _SEG_RW[1] + NSA_WIDTH)
_SEG_KV = (_SEG_Q[1], _SEG_Q[1] + KV_COLS)
_SEG_GATE = (_SEG_KV[1], _SEG_KV[1] + GATE_PAD)
_SEG_MERGE = (_SEG_GATE[1], _SEG_GATE[1] + 2 * D_MODEL)
IN_COLS_PAD = _SEG_MERGE[1]

_VMEM_LIMIT = 56 * 1024 * 1024


def _dot(a, b, precision=None):
    return jnp.dot(a, b, preferred_element_type=F32, precision=precision)


def _dot_tb(a, b, precision=None):
    return lax.dot_general(a, b, (((1,), (1,)), ((), ())), preferred_element_type=F32,
                           precision=precision)


def _dot_ta(a, b, precision=None):
    return lax.dot_general(a, b, (((0,), (0,)), ((), ())), preferred_element_type=F32,
                           precision=precision)


def _split_bf16(x, terms):
    parts = []
    for _ in range(terms - 1):
        parts.append(x.astype(BF16))
        x = x - parts[-1].astype(F32)
    parts.append(x.astype(BF16))
    return parts


def _dot_split_lhs(x, w_bf, terms=2):
    return functools.reduce(jnp.add, [_dot(p, w_bf) for p in _split_bf16(x, terms)])


def _dot_split_rhs(w_bf, x, terms=2):
    return functools.reduce(jnp.add, [_dot(w_bf, p) for p in _split_bf16(x, terms)])


def _dot_3pass(x, w_hl_ref):
    x_hi, x_lo = _split_bf16(x, 2)
    w_hi = w_hl_ref[0]
    return _dot(x_hi, w_hi) + _dot(x_lo, w_hi) + _dot(x_hi, w_hl_ref[1])


def _hi_lo(w):
    hi = w.astype(BF16)
    return jnp.stack([hi, (w - hi.astype(F32)).astype(BF16)])


def _params(*sem):
    return pltpu.CompilerParams(dimension_semantics=sem, vmem_limit_bytes=_VMEM_LIMIT)


def _sigmoid(x):
    return 1.0 / (1.0 + jnp.exp(-x))


def _ada_kernel(c_ref, w_ref, b_ref, o_ref):
    c = c_ref[...]
    s = c * _sigmoid(c)
    o_ref[0] = _dot(s, w_ref[0], HI) + b_ref[0]


def _ada(c, w_ada, b_ada):
    L, D, D6 = w_ada.shape
    B = c.shape[0]
    rows = 8
    cp = jnp.zeros((rows, D), F32).at[:B].set(c)
    tn = 1536
    out = pl.pallas_call(
        _ada_kernel,
        grid=(L, D6 // tn),
        in_specs=[pl.BlockSpec((rows, D), lambda l, j: (0, 0)),
                  pl.BlockSpec((1, D, tn), lambda l, j: (l, 0, j)),
                  pl.BlockSpec((1, 1, tn), lambda l, j: (l, 0, j))],
        out_specs=pl.BlockSpec((1, rows, tn), lambda l, j: (l, 0, j)),
        out_shape=jax.ShapeDtypeStruct((L, rows, D6), F32),
        compiler_params=_params("arbitrary", "arbitrary"),
        name="ada_mod",
    )(cp, w_ada, b_ada.reshape(L, 1, D6))
    return out[:, :B].reshape(L, B, 6, D)


def _inproj_kernel(x_ref, mod_ref, g_ref, w_ref, b_ref, o_rw, o_q, o_kv, o_gate, o_merge):
    m = mod_ref[pl.program_id(0)]
    x = x_ref[...]
    ms = jnp.mean(x * x, axis=-1, keepdims=True)
    h = x * lax.rsqrt(ms + NORM_EPS) * g_ref[...]
    h = h * (1.0 + m[1:2]) + m[0:1]
    hb = h.astype(BF16)
    for o, (a, e) in ((o_rw, _SEG_RW), (o_q, _SEG_Q), (o_kv, _SEG_KV), (o_gate, _SEG_GATE),
                      (o_merge, _SEG_MERGE)):
        o[...] = _dot(hb, w_ref[:, a:e]) + b_ref[:, a:e]


def _inproj(x2, mod, g, w_pad, b_pad, B, T, tm=256):
    N, D = x2.shape
    nt = T // tm
    row = lambda b, t: (b * nt + t, 0)
    widths = [e - a for a, e in (_SEG_RW, _SEG_Q, _SEG_KV, _SEG_GATE, _SEG_MERGE)]
    return pl.pallas_call(
        _inproj_kernel,
        grid=(B, nt),
        in_specs=[pl.BlockSpec((tm, D), row),
                  pl.BlockSpec((B, 6, D), lambda b, t: (0, 0, 0)),
                  pl.BlockSpec((1, D), lambda b, t: (0, 0)),
                  pl.BlockSpec((D, IN_COLS_PAD), lambda b, t: (0, 0)),
                  pl.BlockSpec((1, IN_COLS_PAD), lambda b, t: (0, 0))],
        out_specs=[pl.BlockSpec((tm, w), row) for w in widths],
        out_shape=[jax.ShapeDtypeStruct((N, w), F32) for w in widths],
        compiler_params=_params("arbitrary", "arbitrary"),
        name="in_proj",
    )(x2, mod, g, w_pad, b_pad)


def _rwkv_pre_kernel(p_ref, mu_ref, w0_ref, w2_ref, a0_ref, a2_ref, g2_ref, kk_ref, ka_ref, rk_ref,
                     bd_ref, o_r, o_k, o_v, o_al, o_b, o_ld, o_g, o_bonus, carry_ref):
    W = RWKV_WIDTH

    @pl.when(pl.program_id(1) == 0)
    def _():
        carry_ref[...] = jnp.zeros_like(carry_ref)

    p = p_ref[...]
    ts = p.shape[0]
    rows = lax.broadcasted_iota(jnp.int32, p.shape, 0)
    shifted = jnp.where(rows == 0, carry_ref[0:1, :], pltpu.roll(p, 1, 0))
    carry_ref[0:1, :] = p[ts - 1:ts, :]
    pm = p + (shifted - p) * mu_ref[...]
    r = pm[:, 0:W]
    k = pm[:, W:2 * W]
    v = pm[:, 2 * W:3 * W]
    wa = pm[:, 3 * W:3 * W + DECAY_LORA + ICLR_LORA]
    gl = pm[:, 3 * W + DECAY_LORA + ICLR_LORA:]
    xw = w0_ref[...] + _dot_3pass(jnp.tanh(wa), w2_ref)
    ld = -math.exp(-0.5) * _sigmoid(xw)
    a = _sigmoid(a0_ref[...] + _dot_3pass(wa, a2_ref))
    g = _dot_3pass(_sigmoid(gl), g2_ref)
    bd = bd_ref[...]
    kk = k * kk_ref[...]
    nrm = jnp.sqrt(_dot_split_lhs(kk * kk, bd))
    kk = kk / jnp.maximum(nrm, 1e-12)
    k2 = k * (1.0 + (a - 1.0) * ka_ref[...])
    bonus = _dot_split_lhs(r * k2 * rk_ref[...], bd) * v
    o_r[...] = r
    o_k[...] = k2
    o_v[...] = v
    o_al[...] = kk
    o_b[...] = -kk * a
    o_ld[...] = ld
    o_g[...] = g
    o_bonus[...] = bonus


def _head_block_diag(width, scale=1.0):
    i = jnp.arange(width) // HEAD_DIM
    return ((i[:, None] == i[None, :]).astype(F32) * scale).astype(BF16)


def _rwkv_pre(p_rw, mu, w0, w2, a0, a2, g2, k_k, k_a, r_k, B, T, ts=256):
    N = p_rw.shape[0]
    W = RWKV_WIDTH
    nt = T // ts
    row = lambda b, t: (b * nt + t, 0)
    zl = jnp.zeros((DECAY_LORA, W), F32)
    w2p = jnp.concatenate([w2, zl], axis=0)
    a2p = jnp.concatenate([zl, a2], axis=0)
    full = lambda shape: pl.BlockSpec(shape, lambda b, t: (0,) * len(shape))
    vec = lambda z: z.reshape(1, -1)
    return pl.pallas_call(
        _rwkv_pre_kernel,
        grid=(B, nt),
        in_specs=[pl.BlockSpec((ts, RWKV_COLS), row), full((1, RWKV_COLS)), full((1, W)),
                  full((2, 2 * DECAY_LORA, W)), full((1, W)), full((2, 2 * DECAY_LORA, W)),
                  full((2, GATE_LORA, W)), full((1, W)), full((1, W)), full((1, W)), full((W, W))],
        out_specs=[pl.BlockSpec((ts, W), row)] * 8,
        out_shape=[jax.ShapeDtypeStruct((N, W), F32)] * 8,
        scratch_shapes=[pltpu.VMEM((8, RWKV_COLS), F32)],
        compiler_params=_params("arbitrary", "arbitrary"),
        name="rwkv_pre",
    )(p_rw, vec(mu), vec(w0), _hi_lo(w2p), vec(a0), _hi_lo(a2p), _hi_lo(g2), vec(k_k), vec(k_a), vec(r_k),
      _head_block_diag(W))


def _bf(x):
    return x.astype(BF16)


def _scan_local(chunks, eye, strict, incl, m0, m1):
    C = CHUNK
    n = range(len(chunks))
    st = lambda z: jnp.concatenate([z * m0, z * m1], axis=0)
    zero = jnp.zeros((2 * C, 2 * C), F32)
    at_b, rt_s, vs, vs_b, lhs_a, rhs_a, bk_t, dcol = [], [], [], [], [], [], [], []
    for r, k, v, al, bb, ld, cum in chunks:
        tot = cum[C - 1:C, :]
        dinv = jnp.exp(-cum)
        dend = jnp.exp(tot - cum)
        at_b.append(_bf(st(al * jnp.exp(cum - ld))))
        rt_s.append(st(r * jnp.exp(cum)))
        vs.append(st(v))
        vs_b.append(_bf(vs[-1]))
        lhs_a.append(jnp.concatenate([at_b[-1], _bf(rt_s[-1])], axis=0))
        rhs_a.append(_bf(jnp.concatenate([st(bb * dinv), st(k * dinv)], axis=0)))
        bk_t.append(_bf(jnp.concatenate([st(bb * dend).T, st(k * dend).T], axis=1)))
        dcol.append(jnp.sum(eye * jnp.exp(tot), axis=1, keepdims=True))
    A = [_dot_tb(lhs_a[i], rhs_a[i]) for i in n]
    a_ab = [jnp.where(strict, A[i][0:2 * C, 0:2 * C], zero) for i in n]
    a_ak = [_bf(jnp.where(strict, A[i][0:2 * C, 2 * C:4 * C], zero)) for i in n]
    a_r = [_bf(jnp.concatenate([jnp.where(incl, A[i][2 * C:4 * C, 0:2 * C], zero),
                                jnp.where(incl, A[i][2 * C:4 * C, 2 * C:4 * C], zero)], axis=1)) for i in n]
    akv = [_bf(_dot(a_ak[i], vs_b[i])) for i in n]
    pw = a_ab
    tinv = [eye + pw[i] for i in n]
    for _ in range(5):
        pw_b = [_bf(pw[i]) for i in n]
        pw = [_dot(pw_b[i], pw_b[i]) for i in n]
        tinv = [tinv[i] + _dot(_bf(pw[i]), _bf(tinv[i])) for i in n]
    X = [_dot(_bf(tinv[i]), jnp.concatenate([at_b[i], akv[i]], axis=1)) for i in n]
    w_b = [_bf(X[i][:, 0:LANES]) for i in n]
    uv0 = [jnp.concatenate([_bf(X[i][:, LANES:2 * LANES]), vs_b[i]], axis=0) for i in n]
    m_h = [_bf(_dot(bk_t[i][:, 0:2 * C], w_b[i])) for i in n]
    g_h = [_dot(bk_t[i], uv0[i]) for i in n]
    q_h = [_bf(rt_s[i] + _dot(a_r[i][:, 0:2 * C], w_b[i])) for i in n]
    y0 = [_dot(a_r[i], uv0[i]) for i in n]
    return [(m_h[i], g_h[i], dcol[i], q_h[i], y0[i]) for i in n]


def _scan_steps(local, H):
    C = CHUNK
    ys = []
    for m_h, g_h, dcol, q_h, y0 in local:
        h_b = _bf(H)
        Y = _dot(q_h, h_b) + y0
        ys.append(Y[0:C] + Y[C:2 * C])
        H = dcol * H + _dot(m_h, h_b) + g_h
    return ys, H


def _rwkv_scan_kernel(r_ref, k_ref, v_ref, al_ref, b_ref, ld_ref, o_ref, h_ref):
    C = CHUNK
    tc = r_ref.shape[0]

    @pl.when(pl.program_id(2) == 0)
    def _():
        h_ref[...] = jnp.zeros_like(h_ref)

    tri = jnp.where(lax.broadcasted_iota(jnp.int32, (C, C), 1) <= lax.broadcasted_iota(jnp.int32, (C, C), 0),
                    1.0, 0.0).astype(BF16)
    r2 = lax.broadcasted_iota(jnp.int32, (2 * C, 2 * C), 0)
    c2 = lax.broadcasted_iota(jnp.int32, (2 * C, 2 * C), 1)
    eye = (r2 == c2).astype(F32)
    strict = (c2 & (C - 1)) < (r2 & (C - 1))
    incl = (c2 & (C - 1)) <= (r2 & (C - 1))
    lane = lax.broadcasted_iota(jnp.int32, (C, LANES), 1)
    m0 = (lane < HEAD_DIM).astype(F32)
    m1 = 1.0 - m0
    nc = tc // C
    cum = _dot_split_rhs(tri, jnp.concatenate([ld_ref[c * C:(c + 1) * C, :] for c in range(nc)], axis=1), 3)
    chunks = []
    for c in range(nc):
        sl = slice(c * C, (c + 1) * C)
        chunks.append((r_ref[sl, :], k_ref[sl, :], v_ref[sl, :], al_ref[sl, :], b_ref[sl, :], ld_ref[sl, :],
                       cum[:, c * LANES:(c + 1) * LANES]))
    ys, H = _scan_steps(_scan_local(chunks, eye, strict, incl, m0, m1), h_ref[...])
    for c in range(nc):
        o_ref[c * C:(c + 1) * C, :] = ys[c]
    h_ref[...] = H


def _rwkv_scan(r, k, v, al, bb, ld, B, T, tc=512):
    N, W = r.shape
    nt = T // tc
    spec = pl.BlockSpec((tc, LANES), lambda b, h, t: (b * nt + t, h))
    return pl.pallas_call(
        _rwkv_scan_kernel,
        grid=(B, W // LANES, nt),
        in_specs=[spec] * 6,
        out_specs=spec,
        out_shape=jax.ShapeDtypeStruct((N, W), F32),
        scratch_shapes=[pltpu.VMEM((LANES, LANES), F32)],
        compiler_params=_params("arbitrary", "arbitrary", "arbitrary"),
        name="rwkv_scan",
    )(r, k, v, al, bb, ld)


def _rope_tables(pos):
    half = ROPE_DIM // 2
    inv = jnp.power(ROPE_THETA, -jnp.arange(half, dtype=F32) * 2.0 / ROPE_DIM)
    ang = pos.astype(F32)[:, None] * inv[None, :]
    cos, sin = jnp.cos(ang), jnp.sin(ang)
    n = pos.shape[0]
    rest = HEAD_DIM - ROPE_DIM
    c = jnp.concatenate([cos, cos, jnp.ones((n, rest), F32)], axis=1)
    s_dn = jnp.concatenate([-sin, jnp.zeros((n, half + rest), F32)], axis=1)
    s_up = jnp.concatenate([jnp.zeros((n, half), F32), sin, jnp.zeros((n, rest), F32)], axis=1)
    rep = LANES // HEAD_DIM
    return jnp.tile(c, (1, rep)), jnp.tile(s_dn, (1, rep)), jnp.tile(s_up, (1, rep))


def _norm_rope(x, bd, g, c, s_dn, s_up):
    width = x.shape[1]
    half = ROPE_DIM // 2
    rep = width // LANES
    tile = (lambda z: jnp.concatenate([z] * rep, axis=1)) if rep > 1 else (lambda z: z)
    ms = _dot_split_lhs(x * x, bd)
    xn = x * lax.rsqrt(ms + NORM_EPS) * g
    return (xn * tile(c) + pltpu.roll(xn, width - half, 1) * tile(s_dn)
            + pltpu.roll(xn, half, 1) * tile(s_up))


def _nsa_prep_kernel(q_ref, kv_ref, c_ref, sd_ref, su_ref, gq_ref, gs_ref, gw_ref, bdq_ref, bdk_ref,
                     o_qt, o_ks, o_kw, o_vst, o_vsd, o_vwt):
    c, sd, su = c_ref[...], sd_ref[...], su_ref[...]
    q = _norm_rope(q_ref[...], bdq_ref[...], gq_ref[...], c, sd, su) * Q_SCALE
    qt = q.T
    ts = q.shape[0]
    kv = kv_ref[...]
    bdk = bdk_ref[...]
    pos = pl.program_id(1) * ts + lax.broadcasted_iota(jnp.int32, (ts, LANES), 0)
    blk_onehot = jnp.where((pos >> SEL_SHIFT) == lax.broadcasted_iota(jnp.int32, (ts, LANES), 1), 1.0, 0.0)
    ks = _norm_rope(kv[:, 2 * LANES:3 * LANES], bdk, gs_ref[...], c, sd, su)
    o_ks[...] = jnp.concatenate([ks, blk_onehot], axis=1).astype(BF16)
    o_kw[...] = _norm_rope(kv[:, 4 * LANES:5 * LANES], bdk, gw_ref[...], c, sd, su).astype(BF16)
    ones_rows = jnp.where(lax.broadcasted_iota(jnp.int32, (V_ROWS - HEAD_DIM, q.shape[0]), 0) == 0, 1.0, 0.0)

    def values_t(x):
        xt = x.T
        return jnp.concatenate([xt[0:HEAD_DIM], ones_rows, xt[HEAD_DIM:2 * HEAD_DIM], ones_rows], axis=0)

    vst = values_t(kv[:, 3 * LANES:4 * LANES])
    vwt = values_t(kv[:, 5 * LANES:6 * LANES])
    for j in range(q.shape[0] // KEY_TILE):
        sl = slice(j * KEY_TILE, (j + 1) * KEY_TILE)
        o_qt[0, j] = qt[:, sl].astype(BF16)
        o_vsd[0, j] = vst[:, sl].astype(BF16)
        o_vwt[0, j] = vwt[:, sl].astype(BF16)
    for j in range(q.shape[0] // SEL_TILE):
        o_vst[0, j] = vst[:, j * SEL_TILE:(j + 1) * SEL_TILE].astype(BF16)


def _nsa_prep(q, kv, tables, qk_g, B, T, ts=512):
    N = q.shape[0]
    nt = T // ts
    nk = ts // KEY_TILE
    ns = ts // SEL_TILE
    row = lambda b, t: (b * nt + t, 0)
    full = lambda shape: pl.BlockSpec(shape, lambda b, t: (0,) * len(shape))
    tab = pl.BlockSpec((ts, LANES), lambda b, t: (t, 0))
    gq = jnp.tile(qk_g[0], NSA_Q_HEADS).reshape(1, NSA_WIDTH)
    gs = jnp.tile(qk_g[2], NSA_KV_HEADS).reshape(1, LANES)
    gw = jnp.tile(qk_g[3], NSA_KV_HEADS).reshape(1, LANES)
    tiled = lambda rows: pl.BlockSpec((1, nk, rows, KEY_TILE), lambda b, t: (b, t, 0, 0))
    return pl.pallas_call(
        _nsa_prep_kernel,
        grid=(B, nt),
        in_specs=[pl.BlockSpec((ts, NSA_WIDTH), row), pl.BlockSpec((ts, KV_COLS), row), tab, tab, tab,
                  full((1, NSA_WIDTH)), full((1, LANES)), full((1, LANES)),
                  full((NSA_WIDTH, NSA_WIDTH)), full((LANES, LANES))],
        out_specs=[tiled(NSA_WIDTH), pl.BlockSpec((ts, 2 * LANES), row), pl.BlockSpec((ts, LANES), row),
                   pl.BlockSpec((1, ns, NSA_KV_HEADS * V_ROWS, SEL_TILE), lambda b, t: (b, t, 0, 0)),
                   tiled(NSA_KV_HEADS * V_ROWS), tiled(NSA_KV_HEADS * V_ROWS)],
        out_shape=[jax.ShapeDtypeStruct((B, T // KEY_TILE, NSA_WIDTH, KEY_TILE), BF16),
                   jax.ShapeDtypeStruct((N, 2 * LANES), BF16), jax.ShapeDtypeStruct((N, LANES), BF16),
                   jax.ShapeDtypeStruct((B, T // SEL_TILE, NSA_KV_HEADS * V_ROWS, SEL_TILE), BF16),
                   jax.ShapeDtypeStruct((B, T // KEY_TILE, NSA_KV_HEADS * V_ROWS, KEY_TILE), BF16),
                   jax.ShapeDtypeStruct((B, T // KEY_TILE, NSA_KV_HEADS * V_ROWS, KEY_TILE), BF16)],
        compiler_params=_params("arbitrary", "arbitrary"),
        name="nsa_prep",
    )(q, kv, *tables, gq, gs, gw, _head_block_diag(NSA_WIDTH, 1.0 / HEAD_DIM),
      _head_block_diag(LANES, 1.0 / HEAD_DIM))


def _gelu_tanh(x):
    return 0.5 * x * (1.0 + jnp.tanh(0.7978845608028654 * (x + 0.044715 * x * x * x)))


def _nsa_cmp_kernel(x_ref, pos_ref, w1_ref, w2_ref, *rest, is_key):
    if is_key:
        g_ref, c_ref, sd_ref, su_ref, bd_ref, o_ref, xs_ref = rest
    else:
        o_ref, xs_ref = rest
    nch = xs_ref.shape[0]
    S = CMP_STRIDE
    for j in range(S):
        xs_ref[:, j * LANES:(j + 1) * LANES] = x_ref[0, pl.ds(j, nch, stride=S), :]
    xs = xs_ref[...]
    first = _dot((xs + pos_ref[0:1, :]).astype(BF16), w1_ref[0])
    second = _dot((xs + pos_ref[1:2, :]).astype(BF16), w1_ref[1])
    hid = first + pltpu.roll(second, nch - 1, 0)
    out = _dot(_gelu_tanh(hid).astype(BF16), w2_ref[...])
    rows = lax.broadcasted_iota(jnp.int32, out.shape, 0)
    if is_key:
        out = _norm_rope(out, bd_ref[...], g_ref[...], c_ref[...], sd_ref[...], su_ref[...])
        o_ref[0] = jnp.where(rows < nch - 1, out, 0.0).astype(BF16)
    else:
        o_ref[0] = jnp.where(rows < nch - 1, out, 0.0).T.astype(BF16)


def _nsa_cmp(kv3, which, cmp_pos, cmp_w1, cmp_w2, g_k, tables_cmp):
    B, T, _ = kv3.shape
    S = CMP_STRIDE
    nch = T // S
    is_key = which == 0
    eye2 = jnp.eye(NSA_KV_HEADS, dtype=F32)
    w1 = cmp_w1[which].reshape(CMP_BLOCK, HEAD_DIM, CMP_HIDDEN)
    w1 = jnp.einsum('jdh,ge->jgdeh', w1, eye2).reshape(2, S * LANES, NSA_KV_HEADS * CMP_HIDDEN)
    w2 = jnp.einsum('hd,ge->ghed', cmp_w2[which], eye2).reshape(NSA_KV_HEADS * CMP_HIDDEN, LANES)
    pos = jnp.tile(cmp_pos[which].reshape(2, S, 1, HEAD_DIM), (1, 1, NSA_KV_HEADS, 1)).reshape(2, S * LANES)
    full = lambda shape: pl.BlockSpec(shape, lambda b: (0,) * len(shape))
    in_specs = [pl.BlockSpec((1, T, LANES), lambda b: (b, 0, which)), full(pos.shape), full(w1.shape),
                full(w2.shape)]
    args = [kv3, pos, w1.astype(BF16), w2.astype(BF16)]
    if is_key:
        in_specs += [full((1, LANES)), full((nch, LANES)), full((nch, LANES)), full((nch, LANES)),
                     full((LANES, LANES))]
        args += [jnp.tile(g_k, NSA_KV_HEADS).reshape(1, LANES), *tables_cmp,
                 _head_block_diag(LANES, 1.0 / HEAD_DIM)]
        out_spec = pl.BlockSpec((1, nch, LANES), lambda b: (b, 0, 0))
        out_shape = jax.ShapeDtypeStruct((B, nch, LANES), BF16)
    else:
        out_spec = pl.BlockSpec((1, LANES, nch), lambda b: (b, 0, 0))
        out_shape = jax.ShapeDtypeStruct((B, LANES, nch), BF16)
    return pl.pallas_call(
        functools.partial(_nsa_cmp_kernel, is_key=is_key),
        grid=(B,),
        in_specs=in_specs,
        out_specs=out_spec,
        out_shape=out_shape,
        scratch_shapes=[pltpu.VMEM((nch, S * LANES), F32)],
        compiler_params=_params("arbitrary"),
        name="nsa_cmp_k" if is_key else "nsa_cmp_v",
    )(*args)


def _nsa_attn_kernel(qt_ref, kc_ref, vct_ref, ks_ref, vst_ref, vsd_ref, kw_ref, vwt_ref, gt_ref, ov_ref, o_ref,
                     rhs_ref, oc_ref, keep_ref, s0_ref, s1_ref, s2_ref, s3_ref, p0_ref, p1_ref):
    g = pl.program_id(1)
    qb = pl.program_id(2)
    R = NSA_GROUP
    QT = Q_TILE
    KT = KEY_TILE
    NQ = R * QT
    t0 = qb * QT
    n_cmp_pad = kc_ref.shape[1]
    n_sel = ov_ref.shape[0]

    q_g = jnp.concatenate([qt_ref[0, 0, r * HEAD_DIM:(r + 1) * HEAD_DIM, :] for r in range(R)], axis=1)
    q2 = jnp.concatenate([q_g, q_g], axis=0)
    row_grp = lax.broadcasted_iota(jnp.int32, q2.shape, 0) // HEAD_DIM
    qpad = jnp.where(row_grp == g, q2, jnp.zeros_like(q2))

    tq_row = t0 + (lax.broadcasted_iota(jnp.int32, (1, NQ), 1) & (QT - 1))
    tile4 = lambda z: jnp.concatenate([z] * R, axis=1)

    NV = CMP_VARIANTS
    nq = ks_ref.shape[1] // QT

    def compressed_and_select(n_c, n_b):
        sc = _dot(kc_ref[0, 0:n_c, :], qpad)
        n_i = lax.broadcasted_iota(jnp.int32, (n_c, 1), 0)
        cend = jnp.where(n_i < n_cmp_pad - 1, n_i * CMP_STRIDE + (CMP_BLOCK - 1), jnp.int32(2 ** 30))
        cvalid = cend <= tq_row
        sc = jnp.where(cvalid, sc, NEG_INF)
        mc = jnp.max(sc, axis=0, keepdims=True)
        ec = jnp.where(cvalid, jnp.exp2(sc - mc), 0.0)
        pc = ec / jnp.maximum(jnp.sum(ec, axis=0, keepdims=True), F32_TINY)
        oc_ref[...] = _dot(vct_ref[0, :, 0:n_c], pc.astype(BF16))
        pc_sum = pc[:, 0:QT]
        for r in range(1, R):
            pc_sum = pc_sum + pc[:, r * QT:(r + 1) * QT]
        imp = _dot_split_rhs(ov_ref[0:n_b, 0:n_c], pc_sum)
        jb = lax.broadcasted_iota(jnp.int32, (n_b, QT), 0)
        jf = jb.astype(F32)
        tq_b = t0 + lax.broadcasted_iota(jnp.int32, (n_b, QT), 1)
        cur = tq_b >> SEL_SHIFT
        forced = (jb == 0) | (jb == cur) | (jb == cur - 1)
        visible = jb * SEL_BLOCK <= tq_b
        score = jnp.where(visible, jnp.where(forced, FORCE_SCORE, imp), -1.0)
        sel = jnp.zeros((n_b, QT), F32)
        for _ in range(min(SEL_TOPK, n_b)):
            mx = jnp.max(score, axis=0, keepdims=True)
            jmin = jnp.min(jnp.where(score == mx, jf, 1e9), axis=0, keepdims=True)
            hit = jf == jmin
            sel = jnp.where(hit, 1.0, sel)
            score = jnp.where(hit, -3e38, score)
        keep_ref[0:n_b, :] = jnp.where(visible, sel, 0.0)
        if n_b < n_sel:
            keep_ref[n_b:n_sel, :] = jnp.zeros((n_sel - n_b, QT), F32)

    for v in range(NV):
        @pl.when((qb * NV) // nq == v)
        def _():
            compressed_and_select((v + 1) * n_cmp_pad // NV, (v + 1) * n_sel // NV)

    o_c = oc_ref[...]
    ji = lax.broadcasted_iota(jnp.int32, (n_sel, QT), 0)

    ST = SEL_TILE
    bias_all = (keep_ref[...] - 1.0) * (-NEG_INF)
    first_own = t0 // SEL_BLOCK

    def with_bias_rows(bias):
        rows = tile4(bias).astype(BF16)
        if n_sel < LANES:
            rows = jnp.concatenate([rows, jnp.zeros((LANES - n_sel, NQ), BF16)], axis=0)
        return jnp.concatenate([qpad, rows], axis=0)

    rhs_ref[...] = with_bias_rows(jnp.where(ji < first_own, bias_all, NEG_INF))
    n_tiles = (t0 + ST - 1) // ST
    last_tile = ks_ref.shape[1] // ST - 1
    p_bufs = (p0_ref, p1_ref)

    def sel_scores(kt, s_ref):
        k0 = pl.multiple_of(jnp.minimum(kt, last_tile) * ST, ST)
        s_ref[...] = _dot(ks_ref[0, pl.ds(k0, ST), :], rhs_ref[...])

    def sel_values(kt, slot, acc, alpha):
        return acc * alpha + _dot(vst_ref[0, jnp.clip(kt, 0, last_tile)], p_bufs[slot][...])

    def sel_softmax(s_ref, slot, m):
        s = s_ref[...]
        m_new = jnp.maximum(m, jnp.max(s, axis=0, keepdims=True))
        p_bufs[slot][...] = jnp.exp2(s - m_new).astype(BF16)
        return m_new, jnp.exp2(m - m_new)

    def sel_pair(a, carry, s_now, s_next):
        m, acc, alpha0, alpha1 = carry
        acc = sel_values(a - 2, 0, acc, alpha0)
        acc = sel_values(a - 1, 1, acc, alpha1)
        sel_scores(a + 2, s_next[0])
        sel_scores(a + 3, s_next[1])
        m, alpha0 = sel_softmax(s_now[0], 0, m)
        m, alpha1 = sel_softmax(s_now[1], 1, m)
        return m, acc, alpha0, alpha1

    bufs_a, bufs_b = (s0_ref, s1_ref), (s2_ref, s3_ref)
    sel_scores(0, s0_ref)
    sel_scores(1, s1_ref)
    p0_ref[...] = jnp.zeros_like(p0_ref)
    p1_ref[...] = jnp.zeros_like(p1_ref)
    n_pairs = (n_tiles + 1) // 2
    one = jnp.ones((1, NQ), F32)
    m_s, acc_s, alpha0, alpha1 = lax.fori_loop(
        0, n_pairs,
        lambda j, carry: lax.cond(j % 2 == 0,
                                  lambda c: sel_pair(2 * j, c, bufs_a, bufs_b),
                                  lambda c: sel_pair(2 * j, c, bufs_b, bufs_a), carry),
        (jnp.full((1, NQ), NEG_INF, F32), jnp.zeros((V_ROWS, NQ), F32), one, one))
    acc_s = sel_values(2 * n_pairs - 2, 0, acc_s, alpha0)
    acc_s = sel_values(2 * n_pairs - 1, 1, acc_s, alpha1)
    own = _dot(ks_ref[0, pl.ds(pl.multiple_of(t0, QT), QT), :], with_bias_rows(bias_all))
    seen = lax.broadcasted_iota(jnp.int32, (QT, QT), 0) <= lax.broadcasted_iota(jnp.int32, (QT, QT), 1)
    own = jnp.where(tile4(seen), own, NEG_INF)
    m_new = jnp.maximum(m_s, jnp.max(own, axis=0, keepdims=True))
    acc_s = acc_s * jnp.exp2(m_s - m_new) + _dot(vsd_ref[0, qb], jnp.exp2(own - m_new).astype(BF16))

    n_wt = (WINDOW + QT) // KT
    k0w = pl.multiple_of(jnp.maximum(t0 - WINDOW, 0), KT)
    kt_w = k0w // KT
    keys_w = kw_ref[0, pl.ds(k0w, WINDOW + QT), :]
    dw = (t0 + lax.broadcasted_iota(jnp.int32, (WINDOW + QT, QT), 1)
          - (k0w + lax.broadcasted_iota(jnp.int32, (WINDOW + QT, QT), 0)))
    sw = _dot(keys_w, qpad) + tile4(jnp.where(dw >= 0, jnp.where(dw < WINDOW, 0.0, NEG_INF), NEG_INF))
    pw = jnp.exp2(sw - jnp.max(sw, axis=0, keepdims=True)).astype(BF16)
    acc_w = _dot(vwt_ref[0, kt_w], pw[0:KT])
    for j in range(1, n_wt):
        acc_w = acc_w + _dot(vwt_ref[0, kt_w + j], pw[j * KT:(j + 1) * KT])

    gates = _sigmoid(gt_ref[0, 0])
    grow = lambda j: jnp.concatenate([gates[j, r:r + 1, :] for r in range(R)], axis=1)
    D = HEAD_DIM
    o = (grow(0) * o_c + grow(1) * (acc_s[0:D] / acc_s[D:D + 1])
         + grow(2) * (acc_w[0:D] / acc_w[D:D + 1]))
    halves = []
    for h in range(R // 2):
        pair = jnp.concatenate([o[:, (2 * h) * QT:(2 * h + 1) * QT],
                                o[:, (2 * h + 1) * QT:(2 * h + 2) * QT]], axis=0)
        halves.append(pair.T)
    o_ref[...] = jnp.concatenate(halves, axis=1)


def _nsa_attn(qt, kcmp, vct, ks3, vst, vsd, kw3, vwt, gt, ov_t, B, T):
    G, R = NSA_KV_HEADS, NSA_GROUP
    nq = T // Q_TILE
    nk = T // KEY_TILE
    nch = kcmp.shape[1]
    n_sel = ov_t.shape[0]
    assert (T // SEL_TILE) % 2 == 0 and n_sel <= LANES and Q_TILE == KEY_TILE
    assert nq % CMP_VARIANTS == 0 and n_sel % (8 * CMP_VARIANTS) == 0 and nch % (8 * CMP_VARIANTS) == 0
    return pl.pallas_call(
        _nsa_attn_kernel,
        grid=(B, G, nq),
        in_specs=[pl.BlockSpec((1, 1, R * HEAD_DIM, Q_TILE), lambda b, g, q: (b, q, g, 0)),
                  pl.BlockSpec((1, nch, LANES), lambda b, g, q: (b, 0, 0)),
                  pl.BlockSpec((1, HEAD_DIM, nch), lambda b, g, q: (b, g, 0)),
                  pl.BlockSpec((1, T, 2 * LANES), lambda b, g, q: (b, 0, 0)),
                  pl.BlockSpec((1, T // SEL_TILE, V_ROWS, SEL_TILE), lambda b, g, q: (b, 0, g, 0)),
                  pl.BlockSpec((1, nk, V_ROWS, KEY_TILE), lambda b, g, q: (b, 0, g, 0)),
                  pl.BlockSpec((1, T, LANES), lambda b, g, q: (b, 0, 0)),
                  pl.BlockSpec((1, nk, V_ROWS, KEY_TILE), lambda b, g, q: (b, 0, g, 0)),
                  pl.BlockSpec((1, 1, 3, R, Q_TILE), lambda b, g, q: (b, g, 0, 0, q)),
                  pl.BlockSpec((n_sel, nch), lambda b, g, q: (0, 0))],
        out_specs=pl.BlockSpec((Q_TILE, R * HEAD_DIM), lambda b, g, q: (b * nq + q, g)),
        out_shape=jax.ShapeDtypeStruct((B * T, NSA_WIDTH), F32),
        scratch_shapes=[pltpu.VMEM((2 * LANES, R * Q_TILE), BF16), pltpu.VMEM((HEAD_DIM, R * Q_TILE), F32),
                        pltpu.VMEM((n_sel, Q_TILE), F32),
                        *[pltpu.VMEM((SEL_TILE, R * Q_TILE), F32)] * 4,
                        *[pltpu.VMEM((SEL_TILE, R * Q_TILE), BF16)] * 2],
        compiler_params=_params("arbitrary", "arbitrary", "arbitrary"),
        name="nsa_attn",
    )(qt, kcmp, vct, ks3, vst, vsd, kw3, vwt, gt, ov_t)


def _first_index_of(vals, target):
    idx = jnp.full_like(target, float(len(vals) - 1))
    for i in range(len(vals) - 2, -1, -1):
        idx = jnp.where(vals[i] == target, float(i), idx)
    return idx


def _pick(vals, idx):
    out = vals[-1]
    for i in range(len(vals) - 2, -1, -1):
        out = jnp.where(idx == float(i), vals[i], out)
    return out


def _route_rows(score, bias):
    E, G, P = N_EXPERTS, N_GROUPS, EXPERTS_PER_GROUP
    sel = score + bias
    s = [sel[e:e + 1, :] for e in range(E)]
    raw = [score[e:e + 1, :] for e in range(E)]
    grp = []
    for gi in range(G):
        a = s[gi * P:(gi + 1) * P]
        best = None
        for i in range(P):
            for j in range(i + 1, P):
                pair = a[i] + a[j]
                best = pair if best is None else jnp.maximum(best, pair)
        grp.append(best)
    gmax = functools.reduce(jnp.maximum, grp)
    g_star = _first_index_of(grp, gmax)
    v = [_pick([s[gi * P + i] for gi in range(G)], g_star) for i in range(P)]
    w = [_pick([raw[gi * P + i] for gi in range(G)], g_star) for i in range(P)]
    i1 = _first_index_of(v, functools.reduce(jnp.maximum, v))
    v2 = [jnp.where(i1 == float(i), -jnp.inf, v[i]) for i in range(P)]
    i2 = _first_index_of(v2, functools.reduce(jnp.maximum, v2))
    w1, w2 = _pick(w, i1), _pick(w, i2)
    tot = w1 + w2
    zero = jnp.zeros_like(tot)
    e1, e2 = g_star * P + i1, g_star * P + i2
    n = score.shape[1]
    eidx = lax.broadcasted_iota(jnp.int32, (E, n), 0).astype(F32)
    oh1, oh2 = jnp.where(eidx == e1, 1.0, 0.0), jnp.where(eidx == e2, 1.0, 0.0)
    earlier = jnp.where(lax.broadcasted_iota(jnp.int32, (n, n), 0) < lax.broadcasted_iota(jnp.int32, (n, n), 1),
                        1.0, 0.0).astype(BF16)
    cnt = _dot(jnp.concatenate([oh1, oh2], axis=0).astype(BF16), earlier)
    rank1 = jnp.sum(oh1 * cnt[0:E], axis=0, keepdims=True)
    rank2 = jnp.sum(oh2 * cnt[E:2 * E], axis=0, keepdims=True)
    lane = lax.broadcasted_iota(jnp.int32, (E, LANES), 1)
    totals = jnp.where(lane == 0, jnp.sum(oh1, axis=1, keepdims=True),
                       jnp.where(lane == 1, jnp.sum(oh2, axis=1, keepdims=True), 0.0))
    return jnp.concatenate([e1, e2, w1 / tot, w2 / tot, rank1, rank2, zero, zero], axis=0), totals


def _merge_kernel(ys_ref, g_ref, bonus_ref, gng_ref, gnb_ref, bd_ref, yb_ref, pm_ref, x_ref, mod_ref,
                  ng_ref, wa_ref, wb_ref, wo_ref, rw_ref, rb_ref, o_x, o_h, o_route, o_tot):
    m = mod_ref[pl.program_id(0)]
    bd = bd_ref[...]
    y = ys_ref[...]
    mean = _dot_split_lhs(y, bd)
    yc = y - mean
    var = _dot_split_lhs(yc * yc, bd)
    ya = (yc * lax.rsqrt(var + RWKV_GN_EPS) * gng_ref[...] + gnb_ref[...] + bonus_ref[...]) * g_ref[...]
    pm = pm_ref[...]
    D = x_ref.shape[1]
    mix = (_sigmoid(pm[:, 0:D]) * _dot(ya.astype(BF16), wa_ref[...])
           + _sigmoid(pm[:, D:2 * D]) * _dot(yb_ref[...].astype(BF16), wb_ref[...]))
    x = x_ref[...] + m[2:3] * _dot(mix.astype(BF16), wo_ref[...])
    o_x[...] = x
    ms = jnp.mean(x * x, axis=-1, keepdims=True)
    h = x * lax.rsqrt(ms + NORM_EPS) * ng_ref[...]
    h = h * (1.0 + m[4:5]) + m[3:4]
    o_h[...] = h
    score = _sigmoid(_dot_3pass(h, rw_ref).T[0:N_EXPERTS, :])
    o_route[...], o_tot[...] = _route_rows(score, rb_ref[...])


def _merge(ys, g, bonus, gn_g, gn_b, yb, pm, x2, mod, ng, wa, wb, wo, router_w, router_b, B, T, tm=256):
    N, D = x2.shape
    W = RWKV_WIDTH
    nt = T // tm
    row = lambda b, t: (b * nt + t, 0)
    full = lambda shape: pl.BlockSpec(shape, lambda b, t: (0,) * len(shape))
    return pl.pallas_call(
        _merge_kernel,
        grid=(B, nt),
        in_specs=[pl.BlockSpec((tm, W), row), pl.BlockSpec((tm, W), row), pl.BlockSpec((tm, W), row),
                  full((1, W)), full((1, W)), full((W, W)),
                  pl.BlockSpec((tm, NSA_WIDTH), row), pl.BlockSpec((tm, 2 * D), row),
                  pl.BlockSpec((tm, D), row), full((B, 6, D)), full((1, D)),
                  full((W, D)), full((NSA_WIDTH, D)), full((D, D)), full((2, D, LANES)),
                  full((N_EXPERTS, 1))],
        out_specs=[pl.BlockSpec((tm, D), row), pl.BlockSpec((tm, D), row),
                   pl.BlockSpec((8, tm), lambda b, t: (0, b * nt + t)),
                   pl.BlockSpec((N_EXPERTS, LANES), lambda b, t: (b * nt + t, 0))],
        out_shape=[jax.ShapeDtypeStruct((N, D), F32), jax.ShapeDtypeStruct((N, D), F32),
                   jax.ShapeDtypeStruct((8, N), F32), jax.ShapeDtypeStruct((N // tm * N_EXPERTS, LANES), F32)],
        compiler_params=_params("arbitrary", "arbitrary"),
        name="merge_out",
    )(ys, g, bonus, gn_g.reshape(1, W), gn_b.reshape(1, W), _head_block_diag(W, 1.0 / HEAD_DIM),
      yb, pm, x2, mod, ng, wa, wb, wo,
      _hi_lo(jnp.zeros((D, LANES), F32).at[:, :N_EXPERTS].set(router_w)), router_b.reshape(N_EXPERTS, 1))


def _route(route, totals, N):
    wts = route[TOP_K:2 * TOP_K].T
    NK = N * TOP_K
    E = N_EXPERTS
    n_tiles = totals.shape[0] // E
    expert = route[0:TOP_K].astype(jnp.int32)
    rank = route[2 * TOP_K:3 * TOP_K].astype(jnp.int32)
    per = totals.reshape(n_tiles, E, LANES)[:, :, 0:TOP_K].astype(jnp.int32).transpose(0, 2, 1)
    per = per.reshape(n_tiles * TOP_K, E)
    csum = jnp.cumsum(per, axis=0)
    counts = csum[-1]
    padded = (counts + MOE_BLOCK - 1) // MOE_BLOCK * MOE_BLOCK
    pad_end = jnp.cumsum(padded)
    pad_start = pad_end - padded
    first = (pad_start[None, :] + csum - per).reshape(-1)
    tile = jnp.arange(N, dtype=jnp.int32) // (N // n_tiles)
    group = (tile[None, :] * TOP_K + jnp.arange(TOP_K, dtype=jnp.int32)[:, None]) * E + expert
    dest = (jnp.take(first, group) + rank).reshape(-1)
    n_blk = -(-NK // MOE_BLOCK) + N_EXPERTS
    blk_start = jnp.arange(n_blk, dtype=jnp.int32) * MOE_BLOCK
    blk_expert = jnp.clip(jnp.searchsorted(pad_end, blk_start, side='right'), 0, N_EXPERTS - 1).astype(jnp.int32)
    blk_valid = jnp.clip((pad_start + counts)[blk_expert] - blk_start, 0, MOE_BLOCK).astype(jnp.int32)
    dest = jnp.pad(dest.astype(jnp.int32).reshape(NK // SC_WINDOW, SC_WINDOW), ((0, 0), (0, LANES - SC_WINDOW)))
    return wts, dest, blk_expert, blk_valid, n_blk


SC_WINDOW = 32


def _sc_mesh():
    return plsc.VectorSubcoreMesh(core_axis_name="c", subcore_axis_name="s")


def _sc_dispatch(h, dest, n_slots):
    N, D = h.shape
    W = SC_WINDOW
    nw = N // W

    @pl.kernel(out_type=jax.ShapeDtypeStruct((n_slots, D), h.dtype), mesh=_sc_mesh(), scratch_types=[])
    def dispatch(h_hbm, i_hbm, o_hbm):
        def body(x_vmem, i_vmem):
            pltpu.sync_copy(x_vmem, o_hbm.at[i_vmem.at[0, pl.ds(0, W)]])

        pltpu.emit_pipeline(
            body, grid=(TOP_K, nw),
            in_specs=[pl.BlockSpec((W, D), lambda k, i: (i, 0)),
                      pl.BlockSpec((1, LANES), lambda k, i: (k * nw + i, 0))],
            out_specs=[], core_axis_name=("c", "s"),
            dimension_semantics=(pltpu.PARALLEL, pltpu.PARALLEL))(h_hbm, i_hbm)

    return dispatch(h, dest)


def _sc_collect(ys, dest):
    W = SC_WINDOW
    NK = dest.shape[0] * W
    D = ys.shape[1]
    half = NK // TOP_K // W

    @pl.kernel(out_type=jax.ShapeDtypeStruct((NK, D), ys.dtype), mesh=_sc_mesh(), scratch_types=[])
    def collect(y_hbm, i_hbm, o_hbm):
        def body(i_vmem, o_vmem):
            pltpu.sync_copy(y_hbm.at[i_vmem.at[0, pl.ds(0, W)]], o_vmem)

        pltpu.emit_pipeline(
            body, grid=(TOP_K, half),
            in_specs=[pl.BlockSpec((1, LANES), lambda k, i: (k * half + i, 0))],
            out_specs=[pl.BlockSpec((W, D), lambda k, i: (k * half + i, 0))],
            core_axis_name=("c", "s"),
            dimension_semantics=(pltpu.PARALLEL, pltpu.PARALLEL))(i_hbm, o_hbm)

    return collect(ys, dest)


def _moe_dense_kernel(be_ref, nv_ref, x_ref, wg_ref, wu_ref, wd_ref, o_ref, wg_b, wu_b, wd_b):
    i = pl.program_id(0)
    nv = nv_ref[i]

    @pl.when((i == 0) | (be_ref[i] != be_ref[jnp.maximum(i - 1, 0)]))
    def _():
        wg_b[...] = wg_ref[0].astype(BF16)
        wu_b[...] = wu_ref[0].astype(BF16)
        wd_b[...] = wd_ref[0].astype(BF16)

    @pl.when(nv > 0)
    def _():
        x = x_ref[...].astype(BF16)
        gate = _dot(x, wg_b[...])
        up = _dot(x, wu_b[...])
        o_ref[...] = _dot((gate * _sigmoid(gate) * up).astype(BF16), wd_b[...])

    @pl.when(nv == 0)
    def _():
        o_ref[...] = jnp.zeros_like(o_ref)


def _moe_dense(xs, blk_expert, blk_valid, n_blk, wg, wu, wd):
    P, D = xs.shape
    DE = wg.shape[2]
    wmap = lambda i, be, nv: (be[i], 0, 0)
    grid_spec = pltpu.PrefetchScalarGridSpec(
        num_scalar_prefetch=2,
        grid=(n_blk,),
        in_specs=[pl.BlockSpec((MOE_BLOCK, D), lambda i, be, nv: (i, 0)), pl.BlockSpec((1, D, DE), wmap),
                  pl.BlockSpec((1, D, DE), wmap), pl.BlockSpec((1, DE, D), wmap)],
        out_specs=pl.BlockSpec((MOE_BLOCK, D), lambda i, be, nv: (i, 0)),
        scratch_shapes=[pltpu.VMEM((D, DE), BF16), pltpu.VMEM((D, DE), BF16), pltpu.VMEM((DE, D), BF16)],
    )
    return pl.pallas_call(
        _moe_dense_kernel,
        grid_spec=grid_spec,
        out_shape=jax.ShapeDtypeStruct((P, D), F32),
        compiler_params=_params("arbitrary"),
        name="moe_experts",
    )(blk_expert, blk_valid, xs, wg, wu, wd)


def _final_kernel(x_ref, y0_ref, y1_ref, w_ref, mod_ref, o_ref):
    m = mod_ref[pl.program_id(0)]
    w = w_ref[...]
    o_ref[...] = x_ref[...] + m[5:6] * (w[:, 0:1] * y0_ref[...] + w[:, 1:2] * y1_ref[...])


def _final(x2, ybuf, wts, mod, B, T, tm=512):
    N, D = x2.shape
    nt = T // tm
    row = lambda b, t: (b * nt + t, 0)
    return pl.pallas_call(
        _final_kernel,
        grid=(B, nt),
        in_specs=[pl.BlockSpec((tm, D), row), pl.BlockSpec((tm, D), row),
                  pl.BlockSpec((tm, D), lambda b, t: (N // tm + b * nt + t, 0)),
                  pl.BlockSpec((tm, TOP_K), row), pl.BlockSpec((B, 6, D), lambda b, t: (0, 0, 0))],
        out_specs=pl.BlockSpec((tm, D), row),
        out_shape=jax.ShapeDtypeStruct((N, D), F32),
        compiler_params=_params("arbitrary", "arbitrary"),
        name="moe_combine",
    )(x2, ybuf, ybuf, wts, mod)


def _overlap_t(n_sel, n_cmp_pad):
    ci = jnp.arange(n_cmp_pad)[None, :] * CMP_STRIDE
    sj = jnp.arange(n_sel)[:, None] * SEL_BLOCK
    ov = (ci <= sj + SEL_BLOCK - 1) & (ci + CMP_BLOCK - 1 >= sj) & (jnp.arange(n_cmp_pad)[None, :] < n_cmp_pad - 1)
    return ov.astype(BF16)


def kernel(x, c, w_ada, b_ada, norm_g, w_in, b_in, rwkv_mu, rwkv_w0, rwkv_w2, rwkv_a0, rwkv_a2, rwkv_g2,
           rwkv_k_k, rwkv_k_a, rwkv_r_k, rwkv_gn_g, rwkv_gn_b, qk_norm_g, cmp_pos, cmp_w1, cmp_w2,
           w_up_rwkv, w_up_nsa, w_out, router_w, router_b, exp_w_gate, exp_w_up, exp_w_down):
    B, T, D = x.shape
    L = w_ada.shape[0]
    N = B * T
    mods = _ada(c, w_ada, b_ada)
    tables = _rope_tables(jnp.arange(T, dtype=jnp.int32))
    nch = T // CMP_STRIDE
    tables_cmp = _rope_tables(jnp.arange(nch, dtype=jnp.int32) * CMP_STRIDE + CMP_BLOCK - 1)
    ov_t = _overlap_t(T // SEL_BLOCK, nch)
    n_gate = NSA_GATE_COLS
    x2 = x.reshape(N, D)
    for l in range(L):
        g0 = _SEG_KV[1] + n_gate
        w_pad = jnp.concatenate([w_in[l][:, :g0], jnp.zeros((D, GATE_PAD - n_gate), F32), w_in[l][:, g0:]],
                                axis=1).astype(BF16)
        b_pad = jnp.concatenate([b_in[l][:g0], jnp.zeros((GATE_PAD - n_gate,), F32), b_in[l][g0:]]).reshape(1, -1)
        p_rw, p_q, p_kv, p_gate, p_merge = _inproj(x2, mods[l], norm_g[l, 0].reshape(1, D), w_pad, b_pad, B, T)
        r, k, v, al, bb, ld, g, bonus = _rwkv_pre(p_rw, rwkv_mu[l], rwkv_w0[l], rwkv_w2[l], rwkv_a0[l],
                                                  rwkv_a2[l], rwkv_g2[l], rwkv_k_k[l], rwkv_k_a[l],
                                                  rwkv_r_k[l], B, T)
        ys = _rwkv_scan(r, k, v, al, bb, ld, B, T)
        qt, ks, kw, vst, vsd, vwt = _nsa_prep(p_q, p_kv, tables, qk_norm_g[l], B, T)
        kv3 = p_kv.reshape(B, T, KV_COLS)
        kcmp = _nsa_cmp(kv3, 0, cmp_pos[l], cmp_w1[l], cmp_w2[l], qk_norm_g[l, 1], tables_cmp)
        vct = _nsa_cmp(kv3, 1, cmp_pos[l], cmp_w1[l], cmp_w2[l], None, None)
        gt = p_gate[:, :n_gate].reshape(B, T, NSA_KV_HEADS, NSA_GROUP, 3).transpose(0, 2, 4, 3, 1)
        yb = _nsa_attn(qt, kcmp, vct, ks.reshape(B, T, 2 * LANES), vst, vsd, kw.reshape(B, T, LANES), vwt, gt, ov_t,
                       B, T)
        x2, h2, route, totals = _merge(ys, g, bonus, rwkv_gn_g[l], rwkv_gn_b[l], yb, p_merge, x2, mods[l],
                               norm_g[l, 1].reshape(1, D), w_up_rwkv[l].astype(BF16),
                               w_up_nsa[l].astype(BF16), w_out[l].astype(BF16), router_w, router_b, B, T)
        wts, dest, blk_expert, blk_valid, n_blk = _route(route, totals, N)
        xs = _sc_dispatch(h2, dest, n_blk * MOE_BLOCK)
        ys = _moe_dense(xs, blk_expert, blk_valid, n_blk, exp_w_gate[l], exp_w_up[l], exp_w_down[l])
        ybuf = _sc_collect(ys, dest)
        x2 = _final(x2, ybuf, wts, mods[l], B, T)
    return x2.reshape(B, T, D)
```

```python
import functools
import math

import jax
import jax.numpy as jnp
from jax import lax
from jax.experimental import pallas as pl
from jax.experimental.pallas import tpu as pltpu
from jax.experimental.pallas import tpu_sc as plsc

F32 = jnp.float32
BF16 = jnp.bfloat16
HI = lax.Precision.HIGHEST

D_MODEL = 1024
RWKV_HEADS = 8
HEAD_DIM = 64
RWKV_WIDTH = RWKV_HEADS * HEAD_DIM
DECAY_LORA = 64
ICLR_LORA = 64
GATE_LORA = 128
RWKV_GN_EPS = 64e-5
RWKV_COLS = 3 * RWKV_WIDTH + DECAY_LORA + ICLR_LORA + GATE_LORA

NSA_Q_HEADS = 8
NSA_KV_HEADS = 2
NSA_GROUP = NSA_Q_HEADS // NSA_KV_HEADS
NSA_WIDTH = NSA_Q_HEADS * HEAD_DIM
CMP_STRIDE = 16
CMP_BLOCK = 2 * CMP_STRIDE
CMP_HIDDEN = 256
SEL_BLOCK = 64
SEL_SHIFT = 6
SEL_TOPK = 16
WINDOW = 512
FORCE_SCORE = 1e4
NEG_INF = -1e30
ROPE_THETA = 500000.0
ROPE_DIM = HEAD_DIM // 4
KV_COLS = 6 * NSA_KV_HEADS * HEAD_DIM
NSA_GATE_COLS = 3 * NSA_Q_HEADS
GATE_PAD = 128

N_EXPERTS = 16
N_GROUPS = 4
EXPERTS_PER_GROUP = N_EXPERTS // N_GROUPS
TOP_K = 2
D_EXPERT = 512
MOE_BLOCK = 256
NORM_EPS = 1e-6

LANES = 128
CHUNK = 64
KEY_TILE = 128
SEL_TILE = 512
CMP_VARIANTS = 4
V_ROWS = 80
Q_SCALE = HEAD_DIM ** -0.5 * math.log2(math.e)
Q_TILE = 128
F32_TINY = float(jnp.finfo(jnp.float32).tiny)

_SEG_RW = (0, RWKV_COLS)
_SEG_Q = (_SEG_RW[1], _SEG_RW[1] + NSA_WIDTH)
_SEG_KV = (_SEG_Q[1], _SEG_Q[1] + KV_COLS)
_SEG_GATE = (_SEG_KV[1], _SEG_KV[1] + GATE_PAD)
_SEG_MERGE = (_SEG_GATE[1], _SEG_GATE[1] + 2 * D_MODEL)
IN_COLS_PAD = _SEG_MERGE[1]

_VMEM_LIMIT = 56 * 1024 * 1024


def _dot(a, b, precision=None):
    return jnp.dot(a, b, preferred_element_type=F32, precision=precision)


def _dot_tb(a, b, precision=None):
    return lax.dot_general(a, b, (((1,), (1,)), ((), ())), preferred_element_type=F32,
                           precision=precision)


def _dot_ta(a, b, precision=None):
    return lax.dot_general(a, b, (((0,), (0,)), ((), ())), preferred_element_type=F32,
                           precision=precision)


def _split_bf16(x, terms):
    parts = []
    for _ in range(terms - 1):
        parts.append(x.astype(BF16))
        x = x - parts[-1].astype(F32)
    parts.append(x.astype(BF16))
    return parts


def _dot_split_lhs(x, w_bf, terms=2):
    return functools.reduce(jnp.add, [_dot(p, w_bf) for p in _split_bf16(x, terms)])


def _dot_split_rhs(w_bf, x, terms=2):
    return functools.reduce(jnp.add, [_dot(w_bf, p) for p in _split_bf16(x, terms)])


def _dot_3pass(x, w_hl_ref):
    x_hi, x_lo = _split_bf16(x, 2)
    w_hi = w_hl_ref[0]
    return _dot(x_hi, w_hi) + _dot(x_lo, w_hi) + _dot(x_hi, w_hl_ref[1])


def _hi_lo(w):
    hi = w.astype(BF16)
    return jnp.stack([hi, (w - hi.astype(F32)).astype(BF16)])


def _params(*sem):
    return pltpu.CompilerParams(dimension_semantics=sem, vmem_limit_bytes=_VMEM_LIMIT)


def _sigmoid(x):
    return 1.0 / (1.0 + jnp.exp(-x))


def _ada_kernel(c_ref, w_ref, b_ref, o_ref):
    c = c_ref[...]
    s = c * _sigmoid(c)
    o_ref[0] = _dot(s, w_ref[0], HI) + b_ref[0]


def _ada(c, w_ada, b_ada):
    L, D, D6 = w_ada.shape
    B = c.shape[0]
    rows = 8
    cp = jnp.zeros((rows, D), F32).at[:B].set(c)
    tn = 1536
    out = pl.pallas_call(
        _ada_kernel,
        grid=(L, D6 // tn),
        in_specs=[pl.BlockSpec((rows, D), lambda l, j: (0, 0)),
                  pl.BlockSpec((1, D, tn), lambda l, j: (l, 0, j)),
                  pl.BlockSpec((1, 1, tn), lambda l, j: (l, 0, j))],
        out_specs=pl.BlockSpec((1, rows, tn), lambda l, j: (l, 0, j)),
        out_shape=jax.ShapeDtypeStruct((L, rows, D6), F32),
        compiler_params=_params("arbitrary", "arbitrary"),
        name="ada_mod",
    )(cp, w_ada, b_ada.reshape(L, 1, D6))
    return out[:, :B].reshape(L, B, 6, D)


def _inproj_kernel(x_ref, mod_ref, g_ref, w_ref, b_ref, o_rw, o_q, o_kv, o_gate, o_merge):
    m = mod_ref[pl.program_id(0)]
    x = x_ref[...]
    ms = jnp.mean(x * x, axis=-1, keepdims=True)
    h = x * lax.rsqrt(ms + NORM_EPS) * g_ref[...]
    h = h * (1.0 + m[1:2]) + m[0:1]
    hb = h.astype(BF16)
    for o, (a, e) in ((o_rw, _SEG_RW), (o_q, _SEG_Q), (o_kv, _SEG_KV), (o_gate, _SEG_GATE),
                      (o_merge, _SEG_MERGE)):
        o[...] = _dot(hb, w_ref[:, a:e]) + b_ref[:, a:e]


def _inproj(x2, mod, g, w_pad, b_pad, B, T, tm=256):
    N, D = x2.shape
    nt = T // tm
    row = lambda b, t: (b * nt + t, 0)
    widths = [e - a for a, e in (_SEG_RW, _SEG_Q, _SEG_KV, _SEG_GATE, _SEG_MERGE)]
    return pl.pallas_call(
        _inproj_kernel,
        grid=(B, nt),
        in_specs=[pl.BlockSpec((tm, D), row),
                  pl.BlockSpec((B, 6, D), lambda b, t: (0, 0, 0)),
                  pl.BlockSpec((1, D), lambda b, t: (0, 0)),
                  pl.BlockSpec((D, IN_COLS_PAD), lambda b, t: (0, 0)),
                  pl.BlockSpec((1, IN_COLS_PAD), lambda b, t: (0, 0))],
        out_specs=[pl.BlockSpec((tm, w), row) for w in widths],
        out_shape=[jax.ShapeDtypeStruct((N, w), F32) for w in widths],
        compiler_params=_params("arbitrary", "arbitrary"),
        name="in_proj",
    )(x2, mod, g, w_pad, b_pad)


def _rwkv_pre_kernel(p_ref, mu_ref, w0_ref, w2_ref, a0_ref, a2_ref, g2_ref, kk_ref, ka_ref, rk_ref,
                     bd_ref, o_r, o_k, o_v, o_al, o_b, o_ld, o_g, o_bonus, carry_ref):
    W = RWKV_WIDTH

    @pl.when(pl.program_id(1) == 0)
    def _():
        carry_ref[...] = jnp.zeros_like(carry_ref)

    p = p_ref[...]
    ts = p.shape[0]
    rows = lax.broadcasted_iota(jnp.int32, p.shape, 0)
    shifted = jnp.where(rows == 0, carry_ref[0:1, :], pltpu.roll(p, 1, 0))
    carry_ref[0:1, :] = p[ts - 1:ts, :]
    pm = p + (shifted - p) * mu_ref[...]
    r = pm[:, 0:W]
    k = pm[:, W:2 * W]
    v = pm[:, 2 * W:3 * W]
    wa = pm[:, 3 * W:3 * W + DECAY_LORA + ICLR_LORA]
    gl = pm[:, 3 * W + DECAY_LORA + ICLR_LORA:]
    xw = w0_ref[...] + _dot_3pass(jnp.tanh(wa), w2_ref)
    ld = -math.exp(-0.5) * _sigmoid(xw)
    a = _sigmoid(a0_ref[...] + _dot_3pass(wa, a2_ref))
    g = _dot_3pass(_sigmoid(gl), g2_ref)
    bd = bd_ref[...]
    kk = k * kk_ref[...]
    nrm = jnp.sqrt(_dot_split_lhs(kk * kk, bd))
    kk = kk / jnp.maximum(nrm, 1e-12)
    k2 = k * (1.0 + (a - 1.0) * ka_ref[...])
    bonus = _dot_split_lhs(r * k2 * rk_ref[...], bd) * v
    o_r[...] = r
    o_k[...] = k2
    o_v[...] = v
    o_al[...] = kk
    o_b[...] = -kk * a
    o_ld[...] = ld
    o_g[...] = g
    o_bonus[...] = bonus


def _head_block_diag(width, scale=1.0):
    i = jnp.arange(width) // HEAD_DIM
    return ((i[:, None] == i[None, :]).astype(F32) * scale).astype(BF16)


def _rwkv_pre(p_rw, mu, w0, w2, a0, a2, g2, k_k, k_a, r_k, B, T, ts=256):
    N = p_rw.shape[0]
    W = RWKV_WIDTH
    nt = T // ts
    row = lambda b, t: (b * nt + t, 0)
    zl = jnp.zeros((DECAY_LORA, W), F32)
    w2p = jnp.concatenate([w2, zl], axis=0)
    a2p = jnp.concatenate([zl, a2], axis=0)
    full = lambda shape: pl.BlockSpec(shape, lambda b, t: (0,) * len(shape))
    vec = lambda z: z.reshape(1, -1)
    return pl.pallas_call(
        _rwkv_pre_kernel,
        grid=(B, nt),
        in_specs=[pl.BlockSpec((ts, RWKV_COLS), row), full((1, RWKV_COLS)), full((1, W)),
                  full((2, 2 * DECAY_LORA, W)), full((1, W)), full((2, 2 * DECAY_LORA, W)),
                  full((2, GATE_LORA, W)), full((1, W)), full((1, W)), full((1, W)), full((W, W))],
        out_specs=[pl.BlockSpec((ts, W), row)] * 8,
        out_shape=[jax.ShapeDtypeStruct((N, W), F32)] * 8,
        scratch_shapes=[pltpu.VMEM((8, RWKV_COLS), F32)],
        compiler_params=_params("arbitrary", "arbitrary"),
        name="rwkv_pre",
    )(p_rw, vec(mu), vec(w0), _hi_lo(w2p), vec(a0), _hi_lo(a2p), _hi_lo(g2), vec(k_k), vec(k_a), vec(r_k),
      _head_block_diag(W))


def _bf(x):
    return x.astype(BF16)


def _scan_local(chunks, eye, strict, incl, m0, m1):
    C = CHUNK
    n = range(len(chunks))
    st = lambda z: jnp.concatenate([z * m0, z * m1], axis=0)
    zero = jnp.zeros((2 * C, 2 * C), F32)
    at_b, rt_s, vs, vs_b, lhs_a, rhs_a, bk_t, dcol = [], [], [], [], [], [], [], []
    for r, k, v, al, bb, ld, cum in chunks:
        tot = cum[C - 1:C, :]
        dinv = jnp.exp(-cum)
        dend = jnp.exp(tot - cum)
        at_b.append(_bf(st(al * jnp.exp(cum - ld))))
        rt_s.append(st(r * jnp.exp(cum)))
        vs.append(st(v))
        vs_b.append(_bf(vs[-1]))
        lhs_a.append(jnp.concatenate([at_b[-1], _bf(rt_s[-1])], axis=0))
        rhs_a.append(_bf(jnp.concatenate([st(bb * dinv), st(k * dinv)], axis=0)))
        bk_t.append(_bf(jnp.concatenate([st(bb * dend).T, st(k * dend).T], axis=1)))
        dcol.append(jnp.sum(eye * jnp.exp(tot), axis=1, keepdims=True))
    A = [_dot_tb(lhs_a[i], rhs_a[i]) for i in n]
    a_ab = [jnp.where(strict, A[i][0:2 * C, 0:2 * C], zero) for i in n]
    a_ak = [_bf(jnp.where(strict, A[i][0:2 * C, 2 * C:4 * C], zero)) for i in n]
    a_r = [_bf(jnp.concatenate([jnp.where(incl, A[i][2 * C:4 * C, 0:2 * C], zero),
                                jnp.where(incl, A[i][2 * C:4 * C, 2 * C:4 * C], zero)], axis=1)) for i in n]
    akv = [_bf(_dot(a_ak[i], vs_b[i])) for i in n]
    pw = a_ab
    tinv = [eye + pw[i] for i in n]
    for _ in range(5):
        pw_b = [_bf(pw[i]) for i in n]
        pw = [_dot(pw_b[i], pw_b[i]) for i in n]
        tinv = [tinv[i] + _dot(_bf(pw[i]), _bf(tinv[i])) for i in n]
    X = [_dot(_bf(tinv[i]), jnp.concatenate([at_b[i], akv[i]], axis=1)) for i in n]
    w_b = [_bf(X[i][:, 0:LANES]) for i in n]
    uv0 = [jnp.concatenate([_bf(X[i][:, LANES:2 * LANES]), vs_b[i]], axis=0) for i in n]
    m_h = [_bf(_dot(bk_t[i][:, 0:2 * C], w_b[i])) for i in n]
    g_h = [_dot(bk_t[i], uv0[i]) for i in n]
    q_h = [_bf(rt_s[i] + _dot(a_r[i][:, 0:2 * C], w_b[i])) for i in n]
    y0 = [_dot(a_r[i], uv0[i]) for i in n]
    return [(m_h[i], g_h[i], dcol[i], q_h[i], y0[i]) for i in n]


def _scan_steps(local, H):
    C = CHUNK
    ys = []
    for m_h, g_h, dcol, q_h, y0 in local:
        h_b = _bf(H)
        Y = _dot(q_h, h_b) + y0
        ys.append(Y[0:C] + Y[C:2 * C])
        H = dcol * H + _dot(m_h, h_b) + g_h
    return ys, H


def _rwkv_scan_kernel(r_ref, k_ref, v_ref, al_ref, b_ref, ld_ref, o_ref, h_ref):
    C = CHUNK
    tc = r_ref.shape[0]

    @pl.when(pl.program_id(2) == 0)
    def _():
        h_ref[...] = jnp.zeros_like(h_ref)

    tri = jnp.where(lax.broadcasted_iota(jnp.int32, (C, C), 1) <= lax.broadcasted_iota(jnp.int32, (C, C), 0),
                    1.0, 0.0).astype(BF16)
    r2 = lax.broadcasted_iota(jnp.int32, (2 * C, 2 * C), 0)
    c2 = lax.broadcasted_iota(jnp.int32, (2 * C, 2 * C), 1)
    eye = (r2 == c2).astype(F32)
    strict = (c2 & (C - 1)) < (r2 & (C - 1))
    incl = (c2 & (C - 1)) <= (r2 & (C - 1))
    lane = lax.broadcasted_iota(jnp.int32, (C, LANES), 1)
    m0 = (lane < HEAD_DIM).astype(F32)
    m1 = 1.0 - m0
    nc = tc // C
    cum = _dot_split_rhs(tri, jnp.concatenate([ld_ref[c * C:(c + 1) * C, :] for c in range(nc)], axis=1), 3)
    chunks = []
    for c in range(nc):
        sl = slice(c * C, (c + 1) * C)
        chunks.append((r_ref[sl, :], k_ref[sl, :], v_ref[sl, :], al_ref[sl, :], b_ref[sl, :], ld_ref[sl, :],
                       cum[:, c * LANES:(c + 1) * LANES]))
    ys, H = _scan_steps(_scan_local(chunks, eye, strict, incl, m0, m1), h_ref[...])
    for c in range(nc):
        o_ref[c * C:(c + 1) * C, :] = ys[c]
    h_ref[...] = H


def _rwkv_scan(r, k, v, al, bb, ld, B, T, tc=512):
    N, W = r.shape
    nt = T // tc
    spec = pl.BlockSpec((tc, LANES), lambda b, h, t: (b * nt + t, h))
    return pl.pallas_call(
        _rwkv_scan_kernel,
        grid=(B, W // LANES, nt),
        in_specs=[spec] * 6,
        out_specs=spec,
        out_shape=jax.ShapeDtypeStruct((N, W), F32),
        scratch_shapes=[pltpu.VMEM((LANES, LANES), F32)],
        compiler_params=_params("arbitrary", "arbitrary", "arbitrary"),
        name="rwkv_scan",
    )(r, k, v, al, bb, ld)


def _rope_tables(pos):
    half = ROPE_DIM // 2
    inv = jnp.power(ROPE_THETA, -jnp.arange(half, dtype=F32) * 2.0 / ROPE_DIM)
    ang = pos.astype(F32)[:, None] * inv[None, :]
    cos, sin = jnp.cos(ang), jnp.sin(ang)
    n = pos.shape[0]
    rest = HEAD_DIM - ROPE_DIM
    c = jnp.concatenate([cos, cos, jnp.ones((n, rest), F32)], axis=1)
    s_dn = jnp.concatenate([-sin, jnp.zeros((n, half + rest), F32)], axis=1)
    s_up = jnp.concatenate([jnp.zeros((n, half), F32), sin, jnp.zeros((n, rest), F32)], axis=1)
    rep = LANES // HEAD_DIM
    return jnp.tile(c, (1, rep)), jnp.tile(s_dn, (1, rep)), jnp.tile(s_up, (1, rep))


def _norm_rope(x, bd, g, c, s_dn, s_up):
    width = x.shape[1]
    half = ROPE_DIM // 2
    rep = width // LANES
    tile = (lambda z: jnp.concatenate([z] * rep, axis=1)) if rep > 1 else (lambda z: z)
    ms = _dot_split_lhs(x * x, bd)
    xn = x * lax.rsqrt(ms + NORM_EPS) * g
    return (xn * tile(c) + pltpu.roll(xn, width - half, 1) * tile(s_dn)
            + pltpu.roll(xn, half, 1) * tile(s_up))


def _nsa_prep_kernel(q_ref, kv_ref, c_ref, sd_ref, su_ref, gq_ref, gs_ref, gw_ref, bdq_ref, bdk_ref,
                     o_qt, o_ks, o_kw, o_vst, o_vsd, o_vwt):
    c, sd, su = c_ref[...], sd_ref[...], su_ref[...]
    q = _norm_rope(q_ref[...], bdq_ref[...], gq_ref[...], c, sd, su) * Q_SCALE
    qt = q.T
    ts = q.shape[0]
    kv = kv_ref[...]
    bdk = bdk_ref[...]
    pos = pl.program_id(1) * ts + lax.broadcasted_iota(jnp.int32, (ts, LANES), 0)
    blk_onehot = jnp.where((pos >> SEL_SHIFT) == lax.broadcasted_iota(jnp.int32, (ts, LANES), 1), 1.0, 0.0)
    ks = _norm_rope(kv[:, 2 * LANES:3 * LANES], bdk, gs_ref[...], c, sd, su)
    o_ks[...] = jnp.concatenate([ks, blk_onehot], axis=1).astype(BF16)
    o_kw[...] = _norm_rope(kv[:, 4 * LANES:5 * LANES], bdk, gw_ref[...], c, sd, su).astype(BF16)
    ones_rows = jnp.where(lax.broadcasted_iota(jnp.int32, (V_ROWS - HEAD_DIM, q.shape[0]), 0) == 0, 1.0, 0.0)

    def values_t(x):
        xt = x.T
        return jnp.concatenate([xt[0:HEAD_DIM], ones_rows, xt[HEAD_DIM:2 * HEAD_DIM], ones_rows], axis=0)

    vst = values_t(kv[:, 3 * LANES:4 * LANES])
    vwt = values_t(kv[:, 5 * LANES:6 * LANES])
    for j in range(q.shape[0] // KEY_TILE):
        sl = slice(j * KEY_TILE, (j + 1) * KEY_TILE)
        o_qt[0, j] = qt[:, sl].astype(BF16)
        o_vsd[0, j] = vst[:, sl].astype(BF16)
        o_vwt[0, j] = vwt[:, sl].astype(BF16)
    for j in range(q.shape[0] // SEL_TILE):
        o_vst[0, j] = vst[:, j * SEL_TILE:(j + 1) * SEL_TILE].astype(BF16)


def _nsa_prep(q, kv, tables, qk_g, B, T, ts=512):
    N = q.shape[0]
    nt = T // ts
    nk = ts // KEY_TILE
    ns = ts // SEL_TILE
    row = lambda b, t: (b * nt + t, 0)
    full = lambda shape: pl.BlockSpec(shape, lambda b, t: (0,) * len(shape))
    tab = pl.BlockSpec((ts, LANES), lambda b, t: (t, 0))
    gq = jnp.tile(qk_g[0], NSA_Q_HEADS).reshape(1, NSA_WIDTH)
    gs = jnp.tile(qk_g[2], NSA_KV_HEADS).reshape(1, LANES)
    gw = jnp.tile(qk_g[3], NSA_KV_HEADS).reshape(1, LANES)
    tiled = lambda rows: pl.BlockSpec((1, nk, rows, KEY_TILE), lambda b, t: (b, t, 0, 0))
    return pl.pallas_call(
        _nsa_prep_kernel,
        grid=(B, nt),
        in_specs=[pl.BlockSpec((ts, NSA_WIDTH), row), pl.BlockSpec((ts, KV_COLS), row), tab, tab, tab,
                  full((1, NSA_WIDTH)), full((1, LANES)), full((1, LANES)),
                  full((NSA_WIDTH, NSA_WIDTH)), full((LANES, LANES))],
        out_specs=[tiled(NSA_WIDTH), pl.BlockSpec((ts, 2 * LANES), row), pl.BlockSpec((ts, LANES), row),
                   pl.BlockSpec((1, ns, NSA_KV_HEADS * V_ROWS, SEL_TILE), lambda b, t: (b, t, 0, 0)),
                   tiled(NSA_KV_HEADS * V_ROWS), tiled(NSA_KV_HEADS * V_ROWS)],
        out_shape=[jax.ShapeDtypeStruct((B, T // KEY_TILE, NSA_WIDTH, KEY_TILE), BF16),
                   jax.ShapeDtypeStruct((N, 2 * LANES), BF16), jax.ShapeDtypeStruct((N, LANES), BF16),
                   jax.ShapeDtypeStruct((B, T // SEL_TILE, NSA_KV_HEADS * V_ROWS, SEL_TILE), BF16),
                   jax.ShapeDtypeStruct((B, T // KEY_TILE, NSA_KV_HEADS * V_ROWS, KEY_TILE), BF16),
                   jax.ShapeDtypeStruct((B, T // KEY_TILE, NSA_KV_HEADS * V_ROWS, KEY_TILE), BF16)],
        compiler_params=_params("arbitrary", "arbitrary"),
        name="nsa_prep",
    )(q, kv, *tables, gq, gs, gw, _head_block_diag(NSA_WIDTH, 1.0 / HEAD_DIM),
      _head_block_diag(LANES, 1.0 / HEAD_DIM))


def _gelu_tanh(x):
    return 0.5 * x * (1.0 + jnp.tanh(0.7978845608028654 * (x + 0.044715 * x * x * x)))


def _nsa_cmp_kernel(x_ref, pos_ref, w1_ref, w2_ref, *rest, is_key):
    if is_key:
        g_ref, c_ref, sd_ref, su_ref, bd_ref, o_ref, xs_ref = rest
    else:
        o_ref, xs_ref = rest
    nch = xs_ref.shape[0]
    S = CMP_STRIDE
    for j in range(S):
        xs_ref[:, j * LANES:(j + 1) * LANES] = x_ref[0, pl.ds(j, nch, stride=S), :]
    xs = xs_ref[...]
    first = _dot((xs + pos_ref[0:1, :]).astype(BF16), w1_ref[0])
    second = _dot((xs + pos_ref[1:2, :]).astype(BF16), w1_ref[1])
    hid = first + pltpu.roll(second, nch - 1, 0)
    out = _dot(_gelu_tanh(hid).astype(BF16), w2_ref[...])
    rows = lax.broadcasted_iota(jnp.int32, out.shape, 0)
    if is_key:
        out = _norm_rope(out, bd_ref[...], g_ref[...], c_ref[...], sd_ref[...], su_ref[...])
        o_ref[0] = jnp.where(rows < nch - 1, out, 0.0).astype(BF16)
    else:
        o_ref[0] = jnp.where(rows < nch - 1, out, 0.0).T.astype(BF16)


def _nsa_cmp(kv3, which, cmp_pos, cmp_w1, cmp_w2, g_k, tables_cmp):
    B, T, _ = kv3.shape
    S = CMP_STRIDE
    nch = T // S
    is_key = which == 0
    eye2 = jnp.eye(NSA_KV_HEADS, dtype=F32)
    w1 = cmp_w1[which].reshape(CMP_BLOCK, HEAD_DIM, CMP_HIDDEN)
    w1 = jnp.einsum('jdh,ge->jgdeh', w1, eye2).reshape(2, S * LANES, NSA_KV_HEADS * CMP_HIDDEN)
    w2 = jnp.einsum('hd,ge->ghed', cmp_w2[which], eye2).reshape(NSA_KV_HEADS * CMP_HIDDEN, LANES)
    pos = jnp.tile(cmp_pos[which].reshape(2, S, 1, HEAD_DIM), (1, 1, NSA_KV_HEADS, 1)).reshape(2, S * LANES)
    full = lambda shape: pl.BlockSpec(shape, lambda b: (0,) * len(shape))
    in_specs = [pl.BlockSpec((1, T, LANES), lambda b: (b, 0, which)), full(pos.shape), full(w1.shape),
                full(w2.shape)]
    args = [kv3, pos, w1.astype(BF16), w2.astype(BF16)]
    if is_key:
        in_specs += [full((1, LANES)), full((nch, LANES)), full((nch, LANES)), full((nch, LANES)),
                     full((LANES, LANES))]
        args += [jnp.tile(g_k, NSA_KV_HEADS).reshape(1, LANES), *tables_cmp,
                 _head_block_diag(LANES, 1.0 / HEAD_DIM)]
        out_spec = pl.BlockSpec((1, nch, LANES), lambda b: (b, 0, 0))
        out_shape = jax.ShapeDtypeStruct((B, nch, LANES), BF16)
    else:
        out_spec = pl.BlockSpec((1, LANES, nch), lambda b: (b, 0, 0))
        out_shape = jax.ShapeDtypeStruct((B, LANES, nch), BF16)
    return pl.pallas_call(
        functools.partial(_nsa_cmp_kernel, is_key=is_key),
        grid=(B,),
        in_specs=in_specs,
        out_specs=out_spec,
        out_shape=out_shape,
        scratch_shapes=[pltpu.VMEM((nch, S * LANES), F32)],
        compiler_params=_params("arbitrary"),
        name="nsa_cmp_k" if is_key else "nsa_cmp_v",
    )(*args)


def _nsa_attn_kernel(qt_ref, kc_ref, vct_ref, ks_ref, vst_ref, vsd_ref, kw_ref, vwt_ref, gt_ref, ov_ref, o_ref,
                     rhs_ref, oc_ref, keep_ref, s0_ref, s1_ref, s2_ref, s3_ref, p0_ref, p1_ref):
    g = pl.program_id(1)
    qb = pl.program_id(2)
    R = NSA_GROUP
    QT = Q_TILE
    KT = KEY_TILE
    NQ = R * QT
    t0 = qb * QT
    n_cmp_pad = kc_ref.shape[1]
    n_sel = ov_ref.shape[0]

    q_g = jnp.concatenate([qt_ref[0, 0, r * HEAD_DIM:(r + 1) * HEAD_DIM, :] for r in range(R)], axis=1)
    q2 = jnp.concatenate([q_g, q_g], axis=0)
    row_grp = lax.broadcasted_iota(jnp.int32, q2.shape, 0) // HEAD_DIM
    qpad = jnp.where(row_grp == g, q2, jnp.zeros_like(q2))

    tq_row = t0 + (lax.broadcasted_iota(jnp.int32, (1, NQ), 1) & (QT - 1))
    tile4 = lambda z: jnp.concatenate([z] * R, axis=1)

    NV = CMP_VARIANTS
    nq = ks_ref.shape[1] // QT

    def compressed_and_select(n_c, n_b):
        sc = _dot(kc_ref[0, 0:n_c, :], qpad)
        n_i = lax.broadcasted_iota(jnp.int32, (n_c, 1), 0)
        cend = jnp.where(n_i < n_cmp_pad - 1, n_i * CMP_STRIDE + (CMP_BLOCK - 1), jnp.int32(2 ** 30))
        cvalid = cend <= tq_row
        sc = jnp.where(cvalid, sc, NEG_INF)
        mc = jnp.max(sc, axis=0, keepdims=True)
        ec = jnp.where(cvalid, jnp.exp2(sc - mc), 0.0)
        pc = ec / jnp.maximum(jnp.sum(ec, axis=0, keepdims=True), F32_TINY)
        oc_ref[...] = _dot(vct_ref[0, :, 0:n_c], pc.astype(BF16))
        pc_sum = pc[:, 0:QT]
        for r in range(1, R):
            pc_sum = pc_sum + pc[:, r * QT:(r + 1) * QT]
        imp = _dot_split_rhs(ov_ref[0:n_b, 0:n_c], pc_sum)
        jb = lax.broadcasted_iota(jnp.int32, (n_b, QT), 0)
        jf = jb.astype(F32)
        tq_b = t0 + lax.broadcasted_iota(jnp.int32, (n_b, QT), 1)
        cur = tq_b >> SEL_SHIFT
        forced = (jb == 0) | (jb == cur) | (jb == cur - 1)
        visible = jb * SEL_BLOCK <= tq_b
        score = jnp.where(visible, jnp.where(forced, FORCE_SCORE, imp), -1.0)
        sel = jnp.zeros((n_b, QT), F32)
        for _ in range(min(SEL_TOPK, n_b)):
            mx = jnp.max(score, axis=0, keepdims=True)
            jmin = jnp.min(jnp.where(score == mx, jf, 1e9), axis=0, keepdims=True)
            hit = jf == jmin
            sel = jnp.where(hit, 1.0, sel)
            score = jnp.where(hit, -3e38, score)
        keep_ref[0:n_b, :] = jnp.where(visible, sel, 0.0)
        if n_b < n_sel:
            keep_ref[n_b:n_sel, :] = jnp.zeros((n_sel - n_b, QT), F32)

    for v in range(NV):
        @pl.when((qb * NV) // nq == v)
        def _():
            compressed_and_select((v + 1) * n_cmp_pad // NV, (v + 1) * n_sel // NV)

    o_c = oc_ref[...]
    ji = lax.broadcasted_iota(jnp.int32, (n_sel, QT), 0)

    ST = SEL_TILE
    bias_all = (keep_ref[...] - 1.0) * (-NEG_INF)
    first_own = t0 // SEL_BLOCK

    def with_bias_rows(bias):
        rows = tile4(bias).astype(BF16)
        if n_sel < LANES:
            rows = jnp.concatenate([rows, jnp.zeros((LANES - n_sel, NQ), BF16)], axis=0)
        return jnp.concatenate([qpad, rows], axis=0)

    rhs_ref[...] = with_bias_rows(jnp.where(ji < first_own, bias_all, NEG_INF))
    n_tiles = (t0 + ST - 1) // ST
    last_tile = ks_ref.shape[1] // ST - 1
    p_bufs = (p0_ref, p1_ref)

    def sel_scores(kt, s_ref):
        k0 = pl.multiple_of(jnp.minimum(kt, last_tile) * ST, ST)
        s_ref[...] = _dot(ks_ref[0, pl.ds(k0, ST), :], rhs_ref[...])

    def sel_values(kt, slot, acc, alpha):
        return acc * alpha + _dot(vst_ref[0, jnp.clip(kt, 0, last_tile)], p_bufs[slot][...])

    def sel_softmax(s_ref, slot, m):
        s = s_ref[...]
        m_new = jnp.maximum(m, jnp.max(s, axis=0, keepdims=True))
        p_bufs[slot][...] = jnp.exp2(s - m_new).astype(BF16)
        return m_new, jnp.exp2(m - m_new)

    def sel_pair(a, carry, s_now, s_next):
        m, acc, alpha0, alpha1 = carry
        acc = sel_values(a - 2, 0, acc, alpha0)
        acc = sel_values(a - 1, 1, acc, alpha1)
        sel_scores(a + 2, s_next[0])
        sel_scores(a + 3, s_next[1])
        m, alpha0 = sel_softmax(s_now[0], 0, m)
        m, alpha1 = sel_softmax(s_now[1], 1, m)
        return m, acc, alpha0, alpha1

    bufs_a, bufs_b = (s0_ref, s1_ref), (s2_ref, s3_ref)
    sel_scores(0, s0_ref)
    sel_scores(1, s1_ref)
    p0_ref[...] = jnp.zeros_like(p0_ref)
    p1_ref[...] = jnp.zeros_like(p1_ref)
    n_pairs = (n_tiles + 1) // 2
    one = jnp.ones((1, NQ), F32)
    m_s, acc_s, alpha0, alpha1 = lax.fori_loop(
        0, n_pairs,
        lambda j, carry: lax.cond(j % 2 == 0,
                                  lambda c: sel_pair(2 * j, c, bufs_a, bufs_b),
                                  lambda c: sel_pair(2 * j, c, bufs_b, bufs_a), carry),
        (jnp.full((1, NQ), NEG_INF, F32), jnp.zeros((V_ROWS, NQ), F32), one, one))
    acc_s = sel_values(2 * n_pairs - 2, 0, acc_s, alpha0)
    acc_s = sel_values(2 * n_pairs - 1, 1, acc_s, alpha1)
    own = _dot(ks_ref[0, pl.ds(pl.multiple_of(t0, QT), QT), :], with_bias_rows(bias_all))
    seen = lax.broadcasted_iota(jnp.int32, (QT, QT), 0) <= lax.broadcasted_iota(jnp.int32, (QT, QT), 1)
    own = jnp.where(tile4(seen), own, NEG_INF)
    m_new = jnp.maximum(m_s, jnp.max(own, axis=0, keepdims=True))
    acc_s = acc_s * jnp.exp2(m_s - m_new) + _dot(vsd_ref[0, qb], jnp.exp2(own - m_new).astype(BF16))

    n_wt = (WINDOW + QT) // KT
    k0w = pl.multiple_of(jnp.maximum(t0 - WINDOW, 0), KT)
    kt_w = k0w // KT
    keys_w = kw_ref[0, pl.ds(k0w, WINDOW + QT), :]
    dw = (t0 + lax.broadcasted_iota(jnp.int32, (WINDOW + QT, QT), 1)
          - (k0w + lax.broadcasted_iota(jnp.int32, (WINDOW + QT, QT), 0)))
    sw = _dot(keys_w, qpad) + tile4(jnp.where(dw >= 0, jnp.where(dw < WINDOW, 0.0, NEG_INF), NEG_INF))
    pw = jnp.exp2(sw - jnp.max(sw, axis=0, keepdims=True)).astype(BF16)
    acc_w = _dot(vwt_ref[0, kt_w], pw[0:KT])
    for j in range(1, n_wt):
        acc_w = acc_w + _dot(vwt_ref[0, kt_w + j], pw[j * KT:(j + 1) * KT])

    gates = _sigmoid(gt_ref[0, 0])
    grow = lambda j: jnp.concatenate([gates[j, r:r + 1, :] for r in range(R)], axis=1)
    D = HEAD_DIM
    o = (grow(0) * o_c + grow(1) * (acc_s[0:D] / acc_s[D:D + 1])
         + grow(2) * (acc_w[0:D] / acc_w[D:D + 1]))
    halves = []
    for h in range(R // 2):
        pair = jnp.concatenate([o[:, (2 * h) * QT:(2 * h + 1) * QT],
                                o[:, (2 * h + 1) * QT:(2 * h + 2) * QT]], axis=0)
        halves.append(pair.T)
    o_ref[...] = jnp.concatenate(halves, axis=1)


def _nsa_attn(qt, kcmp, vct, ks3, vst, vsd, kw3, vwt, gt, ov_t, B, T):
    G, R = NSA_KV_HEADS, NSA_GROUP
    nq = T // Q_TILE
    nk = T // KEY_TILE
    nch = kcmp.shape[1]
    n_sel = ov_t.shape[0]
    assert (T // SEL_TILE) % 2 == 0 and n_sel <= LANES and Q_TILE == KEY_TILE
    assert nq % CMP_VARIANTS == 0 and n_sel % (8 * CMP_VARIANTS) == 0 and nch % (8 * CMP_VARIANTS) == 0
    return pl.pallas_call(
        _nsa_attn_kernel,
        grid=(B, G, nq),
        in_specs=[pl.BlockSpec((1, 1, R * HEAD_DIM, Q_TILE), lambda b, g, q: (b, q, g, 0)),
                  pl.BlockSpec((1, nch, LANES), lambda b, g, q: (b, 0, 0)),
                  pl.BlockSpec((1, HEAD_DIM, nch), lambda b, g, q: (b, g, 0)),
                  pl.BlockSpec((1, T, 2 * LANES), lambda b, g, q: (b, 0, 0)),
                  pl.BlockSpec((1, T // SEL_TILE, V_ROWS, SEL_TILE), lambda b, g, q: (b, 0, g, 0)),
                  pl.BlockSpec((1, nk, V_ROWS, KEY_TILE), lambda b, g, q: (b, 0, g, 0)),
                  pl.BlockSpec((1, T, LANES), lambda b, g, q: (b, 0, 0)),
                  pl.BlockSpec((1, nk, V_ROWS, KEY_TILE), lambda b, g, q: (b, 0, g, 0)),
                  pl.BlockSpec((1, 1, 3, R, Q_TILE), lambda b, g, q: (b, g, 0, 0, q)),
                  pl.BlockSpec((n_sel, nch), lambda b, g, q: (0, 0))],
        out_specs=pl.BlockSpec((Q_TILE, R * HEAD_DIM), lambda b, g, q: (b * nq + q, g)),
        out_shape=jax.ShapeDtypeStruct((B * T, NSA_WIDTH), F32),
        scratch_shapes=[pltpu.VMEM((2 * LANES, R * Q_TILE), BF16), pltpu.VMEM((HEAD_DIM, R * Q_TILE), F32),
                        pltpu.VMEM((n_sel, Q_TILE), F32),
                        *[pltpu.VMEM((SEL_TILE, R * Q_TILE), F32)] * 4,
                        *[pltpu.VMEM((SEL_TILE, R * Q_TILE), BF16)] * 2],
        compiler_params=_params("arbitrary", "arbitrary", "arbitrary"),
        name="nsa_attn",
    )(qt, kcmp, vct, ks3, vst, vsd, kw3, vwt, gt, ov_t)


def _first_index_of(vals, target):
    idx = jnp.full_like(target, float(len(vals) - 1))
    for i in range(len(vals) - 2, -1, -1):
        idx = jnp.where(vals[i] == target, float(i), idx)
    return idx


def _pick(vals, idx):
    out = vals[-1]
    for i in range(len(vals) - 2, -1, -1):
        out = jnp.where(idx == float(i), vals[i], out)
    return out


def _route_rows(score, bias):
    E, G, P = N_EXPERTS, N_GROUPS, EXPERTS_PER_GROUP
    sel = score + bias
    s = [sel[e:e + 1, :] for e in range(E)]
    raw = [score[e:e + 1, :] for e in range(E)]
    grp = []
    for gi in range(G):
        a = s[gi * P:(gi + 1) * P]
        best = None
        for i in range(P):
            for j in range(i + 1, P):
                pair = a[i] + a[j]
                best = pair if best is None else jnp.maximum(best, pair)
        grp.append(best)
    gmax = functools.reduce(jnp.maximum, grp)
    g_star = _first_index_of(grp, gmax)
    v = [_pick([s[gi * P + i] for gi in range(G)], g_star) for i in range(P)]
    w = [_pick([raw[gi * P + i] for gi in range(G)], g_star) for i in range(P)]
    i1 = _first_index_of(v, functools.reduce(jnp.maximum, v))
    v2 = [jnp.where(i1 == float(i), -jnp.inf, v[i]) for i in range(P)]
    i2 = _first_index_of(v2, functools.reduce(jnp.maximum, v2))
    w1, w2 = _pick(w, i1), _pick(w, i2)
    tot = w1 + w2
    zero = jnp.zeros_like(tot)
    e1, e2 = g_star * P + i1, g_star * P + i2
    n = score.shape[1]
    eidx = lax.broadcasted_iota(jnp.int32, (E, n), 0).astype(F32)
    oh1, oh2 = jnp.where(eidx == e1, 1.0, 0.0), jnp.where(eidx == e2, 1.0, 0.0)
    earlier = jnp.where(lax.broadcasted_iota(jnp.int32, (n, n), 0) < lax.broadcasted_iota(jnp.int32, (n, n), 1),
                        1.0, 0.0).astype(BF16)
    cnt = _dot(jnp.concatenate([oh1, oh2], axis=0).astype(BF16), earlier)
    rank1 = jnp.sum(oh1 * cnt[0:E], axis=0, keepdims=True)
    rank2 = jnp.sum(oh2 * cnt[E:2 * E], axis=0, keepdims=True)
    lane = lax.broadcasted_iota(jnp.int32, (E, LANES), 1)
    totals = jnp.where(lane == 0, jnp.sum(oh1, axis=1, keepdims=True),
                       jnp.where(lane == 1, jnp.sum(oh2, axis=1, keepdims=True), 0.0))
    return jnp.concatenate([e1, e2, w1 / tot, w2 / tot, rank1, rank2, zero, zero], axis=0), totals


def _merge_kernel(ys_ref, g_ref, bonus_ref, gng_ref, gnb_ref, bd_ref, yb_ref, pm_ref, x_ref, mod_ref,
                  ng_ref, wa_ref, wb_ref, wo_ref, rw_ref, rb_ref, o_x, o_h, o_route, o_tot):
    m = mod_ref[pl.program_id(0)]
    bd = bd_ref[...]
    y = ys_ref[...]
    mean = _dot_split_lhs(y, bd)
    yc = y - mean
    var = _dot_split_lhs(yc * yc, bd)
    ya = (yc * lax.rsqrt(var + RWKV_GN_EPS) * gng_ref[...] + gnb_ref[...] + bonus_ref[...]) * g_ref[...]
    pm = pm_ref[...]
    D = x_ref.shape[1]
    mix = (_sigmoid(pm[:, 0:D]) * _dot(ya.astype(BF16), wa_ref[...])
           + _sigmoid(pm[:, D:2 * D]) * _dot(yb_ref[...].astype(BF16), wb_ref[...]))
    x = x_ref[...] + m[2:3] * _dot(mix.astype(BF16), wo_ref[...])
    o_x[...] = x
    ms = jnp.mean(x * x, axis=-1, keepdims=True)
    h = x * lax.rsqrt(ms + NORM_EPS) * ng_ref[...]
    h = h * (1.0 + m[4:5]) + m[3:4]
    o_h[...] = h
    score = _sigmoid(_dot_3pass(h, rw_ref).T[0:N_EXPERTS, :])
    o_route[...], o_tot[...] = _route_rows(score, rb_ref[...])


def _merge(ys, g, bonus, gn_g, gn_b, yb, pm, x2, mod, ng, wa, wb, wo, router_w, router_b, B, T, tm=256):
    N, D = x2.shape
    W = RWKV_WIDTH
    nt = T // tm
    row = lambda b, t: (b * nt + t, 0)
    full = lambda shape: pl.BlockSpec(shape, lambda b, t: (0,) * len(shape))
    return pl.pallas_call(
        _merge_kernel,
        grid=(B, nt),
        in_specs=[pl.BlockSpec((tm, W), row), pl.BlockSpec((tm, W), row), pl.BlockSpec((tm, W), row),
                  full((1, W)), full((1, W)), full((W, W)),
                  pl.BlockSpec((tm, NSA_WIDTH), row), pl.BlockSpec((tm, 2 * D), row),
                  pl.BlockSpec((tm, D), row), full((B, 6, D)), full((1, D)),
                  full((W, D)), full((NSA_WIDTH, D)), full((D, D)), full((2, D, LANES)),
                  full((N_EXPERTS, 1))],
        out_specs=[pl.BlockSpec((tm, D), row), pl.BlockSpec((tm, D), row),
                   pl.BlockSpec((8, tm), lambda b, t: (0, b * nt + t)),
                   pl.BlockSpec((N_EXPERTS, LANES), lambda b, t: (b * nt + t, 0))],
        out_shape=[jax.ShapeDtypeStruct((N, D), F32), jax.ShapeDtypeStruct((N, D), F32),
                   jax.ShapeDtypeStruct((8, N), F32), jax.ShapeDtypeStruct((N // tm * N_EXPERTS, LANES), F32)],
        compiler_params=_params("arbitrary", "arbitrary"),
        name="merge_out",
    )(ys, g, bonus, gn_g.reshape(1, W), gn_b.reshape(1, W), _head_block_diag(W, 1.0 / HEAD_DIM),
      yb, pm, x2, mod, ng, wa, wb, wo,
      _hi_lo(jnp.zeros((D, LANES), F32).at[:, :N_EXPERTS].set(router_w)), router_b.reshape(N_EXPERTS, 1))


def _route(route, totals, N):
    wts = route[TOP_K:2 * TOP_K].T
    NK = N * TOP_K
    E = N_EXPERTS
    n_tiles = totals.shape[0] // E
    expert = route[0:TOP_K].astype(jnp.int32)
    rank = route[2 * TOP_K:3 * TOP_K].astype(jnp.int32)
    per = totals.reshape(n_tiles, E, LANES)[:, :, 0:TOP_K].astype(jnp.int32).transpose(0, 2, 1)
    per = per.reshape(n_tiles * TOP_K, E)
    csum = jnp.cumsum(per, axis=0)
    counts = csum[-1]
    padded = (counts + MOE_BLOCK - 1) // MOE_BLOCK * MOE_BLOCK
    pad_end = jnp.cumsum(padded)
    pad_start = pad_end - padded
    first = (pad_start[None, :] + csum - per).reshape(n_tiles, TOP_K, E).transpose(1, 0, 2)
    first = jnp.repeat(first, N // n_tiles, axis=1)
    mine = expert[:, :, None] == jnp.arange(E, dtype=jnp.int32)[None, None, :]
    dest = (jnp.sum(jnp.where(mine, first, 0), axis=-1) + rank).reshape(-1)
    n_blk = -(-NK // MOE_BLOCK) + N_EXPERTS
    blk_start = jnp.arange(n_blk, dtype=jnp.int32) * MOE_BLOCK
    blk_expert = jnp.sum((pad_end[None, :] <= blk_start[:, None]).astype(jnp.int32), axis=1)
    blk_expert = jnp.clip(blk_expert, 0, N_EXPERTS - 1)
    blk_valid = jnp.clip((pad_start + counts)[blk_expert] - blk_start, 0, MOE_BLOCK).astype(jnp.int32)
    dest = jnp.pad(dest.astype(jnp.int32).reshape(NK // SC_WINDOW, SC_WINDOW), ((0, 0), (0, LANES - SC_WINDOW)))
    return wts, dest, blk_expert, blk_valid, n_blk


SC_WINDOW = 32


def _sc_mesh():
    return plsc.VectorSubcoreMesh(core_axis_name="c", subcore_axis_name="s")


def _sc_dispatch(h, dest, n_slots):
    N, D = h.shape
    W = SC_WINDOW
    nw = N // W

    @pl.kernel(out_type=jax.ShapeDtypeStruct((n_slots, D), h.dtype), mesh=_sc_mesh(), scratch_types=[])
    def dispatch(h_hbm, i_hbm, o_hbm):
        def body(x_vmem, i_vmem):
            pltpu.sync_copy(x_vmem, o_hbm.at[i_vmem.at[0, pl.ds(0, W)]])

        pltpu.emit_pipeline(
            body, grid=(TOP_K, nw),
            in_specs=[pl.BlockSpec((W, D), lambda k, i: (i, 0)),
                      pl.BlockSpec((1, LANES), lambda k, i: (k * nw + i, 0))],
            out_specs=[], core_axis_name=("c", "s"),
            dimension_semantics=(pltpu.PARALLEL, pltpu.PARALLEL))(h_hbm, i_hbm)

    return dispatch(h, dest)


def _sc_collect(ys, dest):
    W = SC_WINDOW
    NK = dest.shape[0] * W
    D = ys.shape[1]
    half = NK // TOP_K // W

    @pl.kernel(out_type=jax.ShapeDtypeStruct((NK, D), ys.dtype), mesh=_sc_mesh(), scratch_types=[])
    def collect(y_hbm, i_hbm, o_hbm):
        def body(i_vmem, o_vmem):
            pltpu.sync_copy(y_hbm.at[i_vmem.at[0, pl.ds(0, W)]], o_vmem)

        pltpu.emit_pipeline(
            body, grid=(TOP_K, half),
            in_specs=[pl.BlockSpec((1, LANES), lambda k, i: (k * half + i, 0))],
            out_specs=[pl.BlockSpec((W, D), lambda k, i: (k * half + i, 0))],
            core_axis_name=("c", "s"),
            dimension_semantics=(pltpu.PARALLEL, pltpu.PARALLEL))(i_hbm, o_hbm)

    return collect(ys, dest)


def _moe_dense_kernel(be_ref, nv_ref, x_ref, wg_ref, wu_ref, wd_ref, o_ref, wg_b, wu_b, wd_b):
    i = pl.program_id(0)
    nv = nv_ref[i]

    @pl.when((i == 0) | (be_ref[i] != be_ref[jnp.maximum(i - 1, 0)]))
    def _():
        wg_b[...] = wg_ref[0, 0].astype(BF16)
        wu_b[...] = wu_ref[0, 0].astype(BF16)
        wd_b[...] = wd_ref[0, 0].astype(BF16)

    @pl.when(nv > 0)
    def _():
        x = x_ref[...].astype(BF16)
        gate = _dot(x, wg_b[...])
        up = _dot(x, wu_b[...])
        o_ref[...] = _dot((gate * _sigmoid(gate) * up).astype(BF16), wd_b[...])

    @pl.when(nv == 0)
    def _():
        o_ref[...] = jnp.zeros_like(o_ref)


def _moe_dense(xs, blk_expert, blk_valid, n_blk, layer, wg, wu, wd):
    P, D = xs.shape
    DE = wg.shape[3]
    wmap = lambda i, be, nv: (layer, be[i], 0, 0)
    grid_spec = pltpu.PrefetchScalarGridSpec(
        num_scalar_prefetch=2,
        grid=(n_blk,),
        in_specs=[pl.BlockSpec((MOE_BLOCK, D), lambda i, be, nv: (i, 0)), pl.BlockSpec((1, 1, D, DE), wmap),
                  pl.BlockSpec((1, 1, D, DE), wmap), pl.BlockSpec((1, 1, DE, D), wmap)],
        out_specs=pl.BlockSpec((MOE_BLOCK, D), lambda i, be, nv: (i, 0)),
        scratch_shapes=[pltpu.VMEM((D, DE), BF16), pltpu.VMEM((D, DE), BF16), pltpu.VMEM((DE, D), BF16)],
    )
    return pl.pallas_call(
        _moe_dense_kernel,
        grid_spec=grid_spec,
        out_shape=jax.ShapeDtypeStruct((P, D), F32),
        compiler_params=_params("arbitrary"),
        name="moe_experts",
    )(blk_expert, blk_valid, xs, wg, wu, wd)


def _final_kernel(x_ref, y0_ref, y1_ref, w_ref, mod_ref, o_ref):
    m = mod_ref[pl.program_id(0)]
    w = w_ref[...]
    o_ref[...] = x_ref[...] + m[5:6] * (w[:, 0:1] * y0_ref[...] + w[:, 1:2] * y1_ref[...])


def _final(x2, ybuf, wts, mod, B, T, tm=512):
    N, D = x2.shape
    nt = T // tm
    row = lambda b, t: (b * nt + t, 0)
    return pl.pallas_call(
        _final_kernel,
        grid=(B, nt),
        in_specs=[pl.BlockSpec((tm, D), row), pl.BlockSpec((tm, D), row),
                  pl.BlockSpec((tm, D), lambda b, t: (N // tm + b * nt + t, 0)),
                  pl.BlockSpec((tm, TOP_K), row), pl.BlockSpec((B, 6, D), lambda b, t: (0, 0, 0))],
        out_specs=pl.BlockSpec((tm, D), row),
        out_shape=jax.ShapeDtypeStruct((N, D), F32),
        compiler_params=_params("arbitrary", "arbitrary"),
        name="moe_combine",
    )(x2, ybuf, ybuf, wts, mod)


def _overlap_t(n_sel, n_cmp_pad):
    ci = jnp.arange(n_cmp_pad)[None, :] * CMP_STRIDE
    sj = jnp.arange(n_sel)[:, None] * SEL_BLOCK
    ov = (ci <= sj + SEL_BLOCK - 1) & (ci + CMP_BLOCK - 1 >= sj) & (jnp.arange(n_cmp_pad)[None, :] < n_cmp_pad - 1)
    return ov.astype(BF16)


def kernel(x, c, w_ada, b_ada, norm_g, w_in, b_in, rwkv_mu, rwkv_w0, rwkv_w2, rwkv_a0, rwkv_a2, rwkv_g2,
           rwkv_k_k, rwkv_k_a, rwkv_r_k, rwkv_gn_g, rwkv_gn_b, qk_norm_g, cmp_pos, cmp_w1, cmp_w2,
           w_up_rwkv, w_up_nsa, w_out, router_w, router_b, exp_w_gate, exp_w_up, exp_w_down):
    B, T, D = x.shape
    L = w_ada.shape[0]
    N = B * T
    mods = _ada(c, w_ada, b_ada)
    tables = _rope_tables(jnp.arange(T, dtype=jnp.int32))
    nch = T // CMP_STRIDE
    tables_cmp = _rope_tables(jnp.arange(nch, dtype=jnp.int32) * CMP_STRIDE + CMP_BLOCK - 1)
    ov_t = _overlap_t(T // SEL_BLOCK, nch)
    n_gate = NSA_GATE_COLS
    x2 = x.reshape(N, D)
    for l in range(L):
        g0 = _SEG_KV[1] + n_gate
        w_pad = jnp.concatenate([w_in[l][:, :g0], jnp.zeros((D, GATE_PAD - n_gate), F32), w_in[l][:, g0:]],
                                axis=1).astype(BF16)
        b_pad = jnp.concatenate([b_in[l][:g0], jnp.zeros((GATE_PAD - n_gate,), F32), b_in[l][g0:]]).reshape(1, -1)
        p_rw, p_q, p_kv, p_gate, p_merge = _inproj(x2, mods[l], norm_g[l, 0].reshape(1, D), w_pad, b_pad, B, T)
        r, k, v, al, bb, ld, g, bonus = _rwkv_pre(p_rw, rwkv_mu[l], rwkv_w0[l], rwkv_w2[l], rwkv_a0[l],
                                                  rwkv_a2[l], rwkv_g2[l], rwkv_k_k[l], rwkv_k_a[l],
                                                  rwkv_r_k[l], B, T)
        ys = _rwkv_scan(r, k, v, al, bb, ld, B, T)
        qt, ks, kw, vst, vsd, vwt = _nsa_prep(p_q, p_kv, tables, qk_norm_g[l], B, T)
        kv3 = p_kv.reshape(B, T, KV_COLS)
        kcmp = _nsa_cmp(kv3, 0, cmp_pos[l], cmp_w1[l], cmp_w2[l], qk_norm_g[l, 1], tables_cmp)
        vct = _nsa_cmp(kv3, 1, cmp_pos[l], cmp_w1[l], cmp_w2[l], None, None)
        gt = p_gate[:, :n_gate].reshape(B, T, NSA_KV_HEADS, NSA_GROUP, 3).transpose(0, 2, 4, 3, 1)
        yb = _nsa_attn(qt, kcmp, vct, ks.reshape(B, T, 2 * LANES), vst, vsd, kw.reshape(B, T, LANES), vwt, gt, ov_t,
                       B, T)
        x2, h2, route, totals = _merge(ys, g, bonus, rwkv_gn_g[l], rwkv_gn_b[l], yb, p_merge, x2, mods[l],
                               norm_g[l, 1].reshape(1, D), w_up_rwkv[l].astype(BF16),
                               w_up_nsa[l].astype(BF16), w_out[l].astype(BF16), router_w, router_b, B, T)
        wts, dest, blk_expert, blk_valid, n_blk = _route(route, totals, N)
        xs = _sc_dispatch(h2, dest, n_blk * MOE_BLOCK)
        ys = _moe_dense(xs, blk_expert, blk_valid, n_blk, l, exp_w_gate, exp_w_up, exp_w_down)
        ybuf = _sc_collect(ys, dest)
        x2 = _final(x2, ybuf, wts, mods[l], B, T)
    return x2.reshape(B, T, D)
```

```python
import functools
import math

import jax
import jax.numpy as jnp
from jax import lax
from jax.experimental import pallas as pl
from jax.experimental.pallas import tpu as pltpu
from jax.experimental.pallas import tpu_sc as plsc

F32 = jnp.float32
BF16 = jnp.bfloat16
HI = lax.Precision.HIGHEST

D_MODEL = 1024
RWKV_HEADS = 8
HEAD_DIM = 64
RWKV_WIDTH = RWKV_HEADS * HEAD_DIM
DECAY_LORA = 64
ICLR_LORA = 64
GATE_LORA = 128
RWKV_GN_EPS = 64e-5
RWKV_COLS = 3 * RWKV_WIDTH + DECAY_LORA + ICLR_LORA + GATE_LORA

NSA_Q_HEADS = 8
NSA_KV_HEADS = 2
NSA_GROUP = NSA_Q_HEADS // NSA_KV_HEADS
NSA_WIDTH = NSA_Q_HEADS * HEAD_DIM
CMP_STRIDE = 16
CMP_BLOCK = 2 * CMP_STRIDE
CMP_HIDDEN = 256
SEL_BLOCK = 64
SEL_SHIFT = 6
SEL_TOPK = 16
WINDOW = 512
FORCE_SCORE = 1e4
NEG_INF = -1e30
ROPE_THETA = 500000.0
ROPE_DIM = HEAD_DIM // 4
KV_COLS = 6 * NSA_KV_HEADS * HEAD_DIM
NSA_GATE_COLS = 3 * NSA_Q_HEADS
GATE_PAD = 128

N_EXPERTS = 16
N_GROUPS = 4
EXPERTS_PER_GROUP = N_EXPERTS // N_GROUPS
TOP_K = 2
D_EXPERT = 512
MOE_BLOCK = 256
NORM_EPS = 1e-6

LANES = 128
CHUNK = 64
KEY_TILE = 128
SEL_TILE = 512
CMP_VARIANTS = 4
V_ROWS = 80
Q_SCALE = HEAD_DIM ** -0.5 * math.log2(math.e)
Q_TILE = 128
F32_TINY = float(jnp.finfo(jnp.float32).tiny)

_SEG_RW = (0, RWKV_COLS)
_SEG_Q = (_SEG_RW[1], _SEG_RW[1] + NSA_WIDTH)
_SEG_KV = (_SEG_Q[1], _SEG_Q[1] + KV_COLS)
_SEG_GATE = (_SEG_KV[1], _SEG_KV[1] + GATE_PAD)
_SEG_MERGE = (_SEG_GATE[1], _SEG_GATE[1] + 2 * D_MODEL)
IN_COLS_PAD = _SEG_MERGE[1]

_VMEM_LIMIT = 56 * 1024 * 1024


def _dot(a, b, precision=None):
    return jnp.dot(a, b, preferred_element_type=F32, precision=precision)


def _dot_tb(a, b, precision=None):
    return lax.dot_general(a, b, (((1,), (1,)), ((), ())), preferred_element_type=F32,
                           precision=precision)


def _dot_ta(a, b, precision=None):
    return lax.dot_general(a, b, (((0,), (0,)), ((), ())), preferred_element_type=F32,
                           precision=precision)


def _split_bf16(x, terms):
    parts = []
    for _ in range(terms - 1):
        parts.append(x.astype(BF16))
        x = x - parts[-1].astype(F32)
    parts.append(x.astype(BF16))
    return parts


def _dot_split_lhs(x, w_bf, terms=2):
    return functools.reduce(jnp.add, [_dot(p, w_bf) for p in _split_bf16(x, terms)])


def _dot_split_rhs(w_bf, x, terms=2):
    return functools.reduce(jnp.add, [_dot(w_bf, p) for p in _split_bf16(x, terms)])


def _dot_3pass(x, w_hl_ref):
    x_hi, x_lo = _split_bf16(x, 2)
    w_hi = w_hl_ref[0]
    return _dot(x_hi, w_hi) + _dot(x_lo, w_hi) + _dot(x_hi, w_hl_ref[1])


def _hi_lo(w):
    hi = w.astype(BF16)
    return jnp.stack([hi, (w - hi.astype(F32)).astype(BF16)])


def _params(*sem):
    return pltpu.CompilerParams(dimension_semantics=sem, vmem_limit_bytes=_VMEM_LIMIT)


def _sigmoid(x):
    return 1.0 / (1.0 + jnp.exp(-x))


def _ada_kernel(c_ref, w_ref, b_ref, o_ref):
    c = c_ref[...]
    s = c * _sigmoid(c)
    o_ref[0] = _dot(s, w_ref[0], HI) + b_ref[0]


def _ada(c, w_ada, b_ada):
    L, D, D6 = w_ada.shape
    B = c.shape[0]
    rows = 8
    cp = jnp.zeros((rows, D), F32).at[:B].set(c)
    tn = 1536
    out = pl.pallas_call(
        _ada_kernel,
        grid=(L, D6 // tn),
        in_specs=[pl.BlockSpec((rows, D), lambda l, j: (0, 0)),
                  pl.BlockSpec((1, D, tn), lambda l, j: (l, 0, j)),
                  pl.BlockSpec((1, 1, tn), lambda l, j: (l, 0, j))],
        out_specs=pl.BlockSpec((1, rows, tn), lambda l, j: (l, 0, j)),
        out_shape=jax.ShapeDtypeStruct((L, rows, D6), F32),
        compiler_params=_params("arbitrary", "arbitrary"),
        name="ada_mod",
    )(cp, w_ada, b_ada.reshape(L, 1, D6))
    return out[:, :B].reshape(L, B, 6, D)


def _inproj_kernel(x_ref, mod_ref, g_ref, w_ref, b_ref, o_rw, o_q, o_kv, o_gate, o_merge):
    m = mod_ref[pl.program_id(0)]
    x = x_ref[...]
    ms = jnp.mean(x * x, axis=-1, keepdims=True)
    h = x * lax.rsqrt(ms + NORM_EPS) * g_ref[...]
    h = h * (1.0 + m[1:2]) + m[0:1]
    hb = h.astype(BF16)
    for o, (a, e) in ((o_rw, _SEG_RW), (o_q, _SEG_Q), (o_kv, _SEG_KV), (o_gate, _SEG_GATE),
                      (o_merge, _SEG_MERGE)):
        o[...] = _dot(hb, w_ref[:, a:e]) + b_ref[:, a:e]


def _inproj(x2, mod, g, w_pad, b_pad, B, T, tm=256):
    N, D = x2.shape
    nt = T // tm
    row = lambda b, t: (b * nt + t, 0)
    widths = [e - a for a, e in (_SEG_RW, _SEG_Q, _SEG_KV, _SEG_GATE, _SEG_MERGE)]
    return pl.pallas_call(
        _inproj_kernel,
        grid=(B, nt),
        in_specs=[pl.BlockSpec((tm, D), row),
                  pl.BlockSpec((B, 6, D), lambda b, t: (0, 0, 0)),
                  pl.BlockSpec((1, D), lambda b, t: (0, 0)),
                  pl.BlockSpec((D, IN_COLS_PAD), lambda b, t: (0, 0)),
                  pl.BlockSpec((1, IN_COLS_PAD), lambda b, t: (0, 0))],
        out_specs=[pl.BlockSpec((tm, w), row) for w in widths],
        out_shape=[jax.ShapeDtypeStruct((N, w), F32) for w in widths],
        compiler_params=_params("arbitrary", "arbitrary"),
        name="in_proj",
    )(x2, mod, g, w_pad, b_pad)


def _rwkv_pre_kernel(p_ref, mu_ref, w0_ref, w2_ref, a0_ref, a2_ref, g2_ref, kk_ref, ka_ref, rk_ref,
                     bd_ref, o_r, o_k, o_v, o_al, o_b, o_ld, o_g, o_bonus, carry_ref):
    W = RWKV_WIDTH

    @pl.when(pl.program_id(1) == 0)
    def _():
        carry_ref[...] = jnp.zeros_like(carry_ref)

    p = p_ref[...]
    ts = p.shape[0]
    rows = lax.broadcasted_iota(jnp.int32, p.shape, 0)
    shifted = jnp.where(rows == 0, carry_ref[0:1, :], pltpu.roll(p, 1, 0))
    carry_ref[0:1, :] = p[ts - 1:ts, :]
    pm = p + (shifted - p) * mu_ref[...]
    r = pm[:, 0:W]
    k = pm[:, W:2 * W]
    v = pm[:, 2 * W:3 * W]
    wa = pm[:, 3 * W:3 * W + DECAY_LORA + ICLR_LORA]
    gl = pm[:, 3 * W + DECAY_LORA + ICLR_LORA:]
    xw = w0_ref[...] + _dot_3pass(jnp.tanh(wa), w2_ref)
    ld = -math.exp(-0.5) * _sigmoid(xw)
    a = _sigmoid(a0_ref[...] + _dot_3pass(wa, a2_ref))
    g = _dot_3pass(_sigmoid(gl), g2_ref)
    bd = bd_ref[...]
    kk = k * kk_ref[...]
    nrm = jnp.sqrt(_dot_split_lhs(kk * kk, bd))
    kk = kk / jnp.maximum(nrm, 1e-12)
    k2 = k * (1.0 + (a - 1.0) * ka_ref[...])
    bonus = _dot_split_lhs(r * k2 * rk_ref[...], bd) * v
    o_r[...] = r
    o_k[...] = k2
    o_v[...] = v
    o_al[...] = kk
    o_b[...] = -kk * a
    o_ld[...] = ld
    o_g[...] = g
    o_bonus[...] = bonus


def _head_block_diag(width, scale=1.0):
    i = jnp.arange(width) // HEAD_DIM
    return ((i[:, None] == i[None, :]).astype(F32) * scale).astype(BF16)


def _rwkv_pre(p_rw, mu, w0, w2, a0, a2, g2, k_k, k_a, r_k, B, T, ts=256):
    N = p_rw.shape[0]
    W = RWKV_WIDTH
    nt = T // ts
    row = lambda b, t: (b * nt + t, 0)
    zl = jnp.zeros((DECAY_LORA, W), F32)
    w2p = jnp.concatenate([w2, zl], axis=0)
    a2p = jnp.concatenate([zl, a2], axis=0)
    full = lambda shape: pl.BlockSpec(shape, lambda b, t: (0,) * len(shape))
    vec = lambda z: z.reshape(1, -1)
    return pl.pallas_call(
        _rwkv_pre_kernel,
        grid=(B, nt),
        in_specs=[pl.BlockSpec((ts, RWKV_COLS), row), full((1, RWKV_COLS)), full((1, W)),
                  full((2, 2 * DECAY_LORA, W)), full((1, W)), full((2, 2 * DECAY_LORA, W)),
                  full((2, GATE_LORA, W)), full((1, W)), full((1, W)), full((1, W)), full((W, W))],
        out_specs=[pl.BlockSpec((ts, W), row)] * 8,
        out_shape=[jax.ShapeDtypeStruct((N, W), F32)] * 8,
        scratch_shapes=[pltpu.VMEM((8, RWKV_COLS), F32)],
        compiler_params=_params("arbitrary", "arbitrary"),
        name="rwkv_pre",
    )(p_rw, vec(mu), vec(w0), _hi_lo(w2p), vec(a0), _hi_lo(a2p), _hi_lo(g2), vec(k_k), vec(k_a), vec(r_k),
      _head_block_diag(W))


def _bf(x):
    return x.astype(BF16)


def _scan_local(chunks, eye, strict, incl, m0, m1, between_stages=lambda: None):
    C = CHUNK
    n = range(len(chunks))
    st = lambda z: jnp.concatenate([z * m0, z * m1], axis=0)
    zero = jnp.zeros((2 * C, 2 * C), F32)
    at_b, rt_s, vs, vs_b, lhs_a, rhs_a, bk_t, dcol = [], [], [], [], [], [], [], []
    for r, k, v, al, bb, ld, cum in chunks:
        tot = cum[C - 1:C, :]
        dinv = jnp.exp(-cum)
        dend = jnp.exp(tot - cum)
        at_b.append(_bf(st(al * jnp.exp(cum - ld))))
        rt_s.append(st(r * jnp.exp(cum)))
        vs.append(st(v))
        vs_b.append(_bf(vs[-1]))
        lhs_a.append(jnp.concatenate([at_b[-1], _bf(rt_s[-1])], axis=0))
        rhs_a.append(_bf(jnp.concatenate([st(bb * dinv), st(k * dinv)], axis=0)))
        bk_t.append(_bf(jnp.concatenate([st(bb * dend).T, st(k * dend).T], axis=1)))
        dcol.append(jnp.sum(eye * jnp.exp(tot), axis=1, keepdims=True))
    between_stages()
    A = [_dot_tb(lhs_a[i], rhs_a[i]) for i in n]
    between_stages()
    a_ab = [jnp.where(strict, A[i][0:2 * C, 0:2 * C], zero) for i in n]
    a_ak = [_bf(jnp.where(strict, A[i][0:2 * C, 2 * C:4 * C], zero)) for i in n]
    a_r = [_bf(jnp.concatenate([jnp.where(incl, A[i][2 * C:4 * C, 0:2 * C], zero),
                                jnp.where(incl, A[i][2 * C:4 * C, 2 * C:4 * C], zero)], axis=1)) for i in n]
    akv = [_bf(_dot(a_ak[i], vs_b[i])) for i in n]
    between_stages()
    pw = a_ab
    tinv = [eye + pw[i] for i in n]
    for _ in range(5):
        pw_b = [_bf(pw[i]) for i in n]
        pw = [_dot(pw_b[i], pw_b[i]) for i in n]
        tinv = [tinv[i] + _dot(_bf(pw[i]), _bf(tinv[i])) for i in n]
        between_stages()
    X = [_dot(_bf(tinv[i]), jnp.concatenate([at_b[i], akv[i]], axis=1)) for i in n]
    between_stages()
    w_b = [_bf(X[i][:, 0:LANES]) for i in n]
    uv0 = [jnp.concatenate([_bf(X[i][:, LANES:2 * LANES]), vs_b[i]], axis=0) for i in n]
    m_h = [_bf(_dot(bk_t[i][:, 0:2 * C], w_b[i])) for i in n]
    g_h = [_dot(bk_t[i], uv0[i]) for i in n]
    between_stages()
    q_h = [_bf(rt_s[i] + _dot(a_r[i][:, 0:2 * C], w_b[i])) for i in n]
    y0 = [_dot(a_r[i], uv0[i]) for i in n]
    return [(m_h[i], g_h[i], dcol[i], q_h[i], y0[i]) for i in n]


def _rwkv_scan_kernel(r_ref, k_ref, v_ref, al_ref, b_ref, ld_ref, o_ref, h_ref, *local_refs):
    C = CHUNK
    tc = r_ref.shape[0]
    nc = tc // C

    @pl.when(pl.program_id(2) == 0)
    def _():
        h_ref[...] = jnp.zeros_like(h_ref)
        for ref in local_refs:
            ref[...] = jnp.zeros_like(ref)

    seq = {"H": h_ref[...], "c": 0}

    def one_step():
        c = seq["c"]
        if c < nc:
            m_h, g_h, dcol, q_h, y0 = (ref[c] for ref in local_refs)
            h_b = _bf(seq["H"])
            Y = _dot(q_h, h_b) + y0
            o_ref[c * C:(c + 1) * C, :] = Y[0:C] + Y[C:2 * C]
            seq["H"] = dcol * seq["H"] + _dot(m_h, h_b) + g_h
            seq["c"] = c + 1

    tri = jnp.where(lax.broadcasted_iota(jnp.int32, (C, C), 1) <= lax.broadcasted_iota(jnp.int32, (C, C), 0),
                    1.0, 0.0).astype(BF16)
    r2 = lax.broadcasted_iota(jnp.int32, (2 * C, 2 * C), 0)
    c2 = lax.broadcasted_iota(jnp.int32, (2 * C, 2 * C), 1)
    eye = (r2 == c2).astype(F32)
    strict = (c2 & (C - 1)) < (r2 & (C - 1))
    incl = (c2 & (C - 1)) <= (r2 & (C - 1))
    lane = lax.broadcasted_iota(jnp.int32, (C, LANES), 1)
    m0 = (lane < HEAD_DIM).astype(F32)
    m1 = 1.0 - m0
    cum = _dot_split_rhs(tri, jnp.concatenate([ld_ref[c * C:(c + 1) * C, :] for c in range(nc)], axis=1), 3)
    chunks = []
    for c in range(nc):
        sl = slice(c * C, (c + 1) * C)
        chunks.append((r_ref[sl, :], k_ref[sl, :], v_ref[sl, :], al_ref[sl, :], b_ref[sl, :], ld_ref[sl, :],
                       cum[:, c * LANES:(c + 1) * LANES]))
    local = _scan_local(chunks, eye, strict, incl, m0, m1, between_stages=one_step)
    while seq["c"] < nc:
        one_step()
    h_ref[...] = seq["H"]
    for c, parts in enumerate(local):
        for ref, part in zip(local_refs, parts):
            ref[c] = part


def _rwkv_scan(r, k, v, al, bb, ld, B, T, tc=512):
    N, W = r.shape
    nt = T // tc
    nc = tc // CHUNK
    in_spec = pl.BlockSpec((tc, LANES), lambda b, h, t: (b * nt + jnp.minimum(t, nt - 1), h))
    out_spec = pl.BlockSpec((tc, LANES), lambda b, h, t: (b * nt + jnp.maximum(t - 1, 0), h))
    sq = (nc, LANES, LANES)
    return pl.pallas_call(
        _rwkv_scan_kernel,
        grid=(B, W // LANES, nt + 1),
        in_specs=[in_spec] * 6,
        out_specs=out_spec,
        out_shape=jax.ShapeDtypeStruct((N, W), F32),
        scratch_shapes=[pltpu.VMEM((LANES, LANES), F32), pltpu.VMEM(sq, BF16), pltpu.VMEM(sq, F32),
                        pltpu.VMEM((nc, LANES, 1), F32), pltpu.VMEM(sq, BF16), pltpu.VMEM(sq, F32)],
        compiler_params=_params("arbitrary", "arbitrary", "arbitrary"),
        name="rwkv_scan",
    )(r, k, v, al, bb, ld)


def _rope_tables(pos):
    half = ROPE_DIM // 2
    inv = jnp.power(ROPE_THETA, -jnp.arange(half, dtype=F32) * 2.0 / ROPE_DIM)
    ang = pos.astype(F32)[:, None] * inv[None, :]
    cos, sin = jnp.cos(ang), jnp.sin(ang)
    n = pos.shape[0]
    rest = HEAD_DIM - ROPE_DIM
    c = jnp.concatenate([cos, cos, jnp.ones((n, rest), F32)], axis=1)
    s_dn = jnp.concatenate([-sin, jnp.zeros((n, half + rest), F32)], axis=1)
    s_up = jnp.concatenate([jnp.zeros((n, half), F32), sin, jnp.zeros((n, rest), F32)], axis=1)
    rep = LANES // HEAD_DIM
    return jnp.tile(c, (1, rep)), jnp.tile(s_dn, (1, rep)), jnp.tile(s_up, (1, rep))


def _norm_rope(x, bd, g, c, s_dn, s_up):
    width = x.shape[1]
    half = ROPE_DIM // 2
    rep = width // LANES
    tile = (lambda z: jnp.concatenate([z] * rep, axis=1)) if rep > 1 else (lambda z: z)
    ms = _dot_split_lhs(x * x, bd)
    xn = x * lax.rsqrt(ms + NORM_EPS) * g
    return (xn * tile(c) + pltpu.roll(xn, width - half, 1) * tile(s_dn)
            + pltpu.roll(xn, half, 1) * tile(s_up))


def _nsa_prep_kernel(q_ref, kv_ref, c_ref, sd_ref, su_ref, gq_ref, gs_ref, gw_ref, bdq_ref, bdk_ref,
                     o_qt, o_ks, o_kw, o_vst, o_vsd, o_vwt):
    c, sd, su = c_ref[...], sd_ref[...], su_ref[...]
    q = _norm_rope(q_ref[...], bdq_ref[...], gq_ref[...], c, sd, su) * Q_SCALE
    qt = q.T
    ts = q.shape[0]
    kv = kv_ref[...]
    bdk = bdk_ref[...]
    pos = pl.program_id(1) * ts + lax.broadcasted_iota(jnp.int32, (ts, LANES), 0)
    blk_onehot = jnp.where((pos >> SEL_SHIFT) == lax.broadcasted_iota(jnp.int32, (ts, LANES), 1), 1.0, 0.0)
    ks = _norm_rope(kv[:, 2 * LANES:3 * LANES], bdk, gs_ref[...], c, sd, su)
    o_ks[...] = jnp.concatenate([ks, blk_onehot], axis=1).astype(BF16)
    o_kw[...] = _norm_rope(kv[:, 4 * LANES:5 * LANES], bdk, gw_ref[...], c, sd, su).astype(BF16)
    ones_rows = jnp.where(lax.broadcasted_iota(jnp.int32, (V_ROWS - HEAD_DIM, q.shape[0]), 0) == 0, 1.0, 0.0)

    def values_t(x):
        xt = x.T
        return jnp.concatenate([xt[0:HEAD_DIM], ones_rows, xt[HEAD_DIM:2 * HEAD_DIM], ones_rows], axis=0)

    vst = values_t(kv[:, 3 * LANES:4 * LANES])
    vwt = values_t(kv[:, 5 * LANES:6 * LANES])
    for j in range(q.shape[0] // KEY_TILE):
        sl = slice(j * KEY_TILE, (j + 1) * KEY_TILE)
        o_qt[0, j] = qt[:, sl].astype(BF16)
        o_vsd[0, j] = vst[:, sl].astype(BF16)
        o_vwt[0, j] = vwt[:, sl].astype(BF16)
    for j in range(q.shape[0] // SEL_TILE):
        o_vst[0, j] = vst[:, j * SEL_TILE:(j + 1) * SEL_TILE].astype(BF16)


def _nsa_prep(q, kv, tables, qk_g, B, T, ts=512):
    N = q.shape[0]
    nt = T // ts
    nk = ts // KEY_TILE
    ns = ts // SEL_TILE
    row = lambda b, t: (b * nt + t, 0)
    full = lambda shape: pl.BlockSpec(shape, lambda b, t: (0,) * len(shape))
    tab = pl.BlockSpec((ts, LANES), lambda b, t: (t, 0))
    gq = jnp.tile(qk_g[0], NSA_Q_HEADS).reshape(1, NSA_WIDTH)
    gs = jnp.tile(qk_g[2], NSA_KV_HEADS).reshape(1, LANES)
    gw = jnp.tile(qk_g[3], NSA_KV_HEADS).reshape(1, LANES)
    tiled = lambda rows: pl.BlockSpec((1, nk, rows, KEY_TILE), lambda b, t: (b, t, 0, 0))
    return pl.pallas_call(
        _nsa_prep_kernel,
        grid=(B, nt),
        in_specs=[pl.BlockSpec((ts, NSA_WIDTH), row), pl.BlockSpec((ts, KV_COLS), row), tab, tab, tab,
                  full((1, NSA_WIDTH)), full((1, LANES)), full((1, LANES)),
                  full((NSA_WIDTH, NSA_WIDTH)), full((LANES, LANES))],
        out_specs=[tiled(NSA_WIDTH), pl.BlockSpec((ts, 2 * LANES), row), pl.BlockSpec((ts, LANES), row),
                   pl.BlockSpec((1, ns, NSA_KV_HEADS * V_ROWS, SEL_TILE), lambda b, t: (b, t, 0, 0)),
                   tiled(NSA_KV_HEADS * V_ROWS), tiled(NSA_KV_HEADS * V_ROWS)],
        out_shape=[jax.ShapeDtypeStruct((B, T // KEY_TILE, NSA_WIDTH, KEY_TILE), BF16),
                   jax.ShapeDtypeStruct((N, 2 * LANES), BF16), jax.ShapeDtypeStruct((N, LANES), BF16),
                   jax.ShapeDtypeStruct((B, T // SEL_TILE, NSA_KV_HEADS * V_ROWS, SEL_TILE), BF16),
                   jax.ShapeDtypeStruct((B, T // KEY_TILE, NSA_KV_HEADS * V_ROWS, KEY_TILE), BF16),
                   jax.ShapeDtypeStruct((B, T // KEY_TILE, NSA_KV_HEADS * V_ROWS, KEY_TILE), BF16)],
        compiler_params=_params("arbitrary", "arbitrary"),
        name="nsa_prep",
    )(q, kv, *tables, gq, gs, gw, _head_block_diag(NSA_WIDTH, 1.0 / HEAD_DIM),
      _head_block_diag(LANES, 1.0 / HEAD_DIM))


def _gelu_tanh(x):
    return 0.5 * x * (1.0 + jnp.tanh(0.7978845608028654 * (x + 0.044715 * x * x * x)))


def _nsa_cmp_kernel(x_ref, pos_ref, w1_ref, w2_ref, *rest, is_key):
    if is_key:
        g_ref, c_ref, sd_ref, su_ref, bd_ref, o_ref, xs_ref = rest
    else:
        o_ref, xs_ref = rest
    nch = xs_ref.shape[0]
    S = CMP_STRIDE
    for j in range(S):
        xs_ref[:, j * LANES:(j + 1) * LANES] = x_ref[0, pl.ds(j, nch, stride=S), :]
    xs = xs_ref[...]
    first = _dot((xs + pos_ref[0:1, :]).astype(BF16), w1_ref[0])
    second = _dot((xs + pos_ref[1:2, :]).astype(BF16), w1_ref[1])
    hid = first + pltpu.roll(second, nch - 1, 0)
    out = _dot(_gelu_tanh(hid).astype(BF16), w2_ref[...])
    rows = lax.broadcasted_iota(jnp.int32, out.shape, 0)
    if is_key:
        out = _norm_rope(out, bd_ref[...], g_ref[...], c_ref[...], sd_ref[...], su_ref[...])
        o_ref[0] = jnp.where(rows < nch - 1, out, 0.0).astype(BF16)
    else:
        o_ref[0] = jnp.where(rows < nch - 1, out, 0.0).T.astype(BF16)


def _nsa_cmp(kv3, which, cmp_pos, cmp_w1, cmp_w2, g_k, tables_cmp):
    B, T, _ = kv3.shape
    S = CMP_STRIDE
    nch = T // S
    is_key = which == 0
    eye2 = jnp.eye(NSA_KV_HEADS, dtype=F32)
    w1 = cmp_w1[which].reshape(CMP_BLOCK, HEAD_DIM, CMP_HIDDEN)
    w1 = jnp.einsum('jdh,ge->jgdeh', w1, eye2).reshape(2, S * LANES, NSA_KV_HEADS * CMP_HIDDEN)
    w2 = jnp.einsum('hd,ge->ghed', cmp_w2[which], eye2).reshape(NSA_KV_HEADS * CMP_HIDDEN, LANES)
    pos = jnp.tile(cmp_pos[which].reshape(2, S, 1, HEAD_DIM), (1, 1, NSA_KV_HEADS, 1)).reshape(2, S * LANES)
    full = lambda shape: pl.BlockSpec(shape, lambda b: (0,) * len(shape))
    in_specs = [pl.BlockSpec((1, T, LANES), lambda b: (b, 0, which)), full(pos.shape), full(w1.shape),
                full(w2.shape)]
    args = [kv3, pos, w1.astype(BF16), w2.astype(BF16)]
    if is_key:
        in_specs += [full((1, LANES)), full((nch, LANES)), full((nch, LANES)), full((nch, LANES)),
                     full((LANES, LANES))]
        args += [jnp.tile(g_k, NSA_KV_HEADS).reshape(1, LANES), *tables_cmp,
                 _head_block_diag(LANES, 1.0 / HEAD_DIM)]
        out_spec = pl.BlockSpec((1, nch, LANES), lambda b: (b, 0, 0))
        out_shape = jax.ShapeDtypeStruct((B, nch, LANES), BF16)
    else:
        out_spec = pl.BlockSpec((1, LANES, nch), lambda b: (b, 0, 0))
        out_shape = jax.ShapeDtypeStruct((B, LANES, nch), BF16)
    return pl.pallas_call(
        functools.partial(_nsa_cmp_kernel, is_key=is_key),
        grid=(B,),
        in_specs=in_specs,
        out_specs=out_spec,
        out_shape=out_shape,
        scratch_shapes=[pltpu.VMEM((nch, S * LANES), F32)],
        compiler_params=_params("arbitrary"),
        name="nsa_cmp_k" if is_key else "nsa_cmp_v",
    )(*args)


def _nsa_attn_kernel(qt_ref, kc_ref, vct_ref, ks_ref, vst_ref, vsd_ref, kw_ref, vwt_ref, gt_ref, ov_ref, o_ref,
                     rhs_ref, oc_ref, keep_ref, s0_ref, s1_ref, s2_ref, s3_ref, p0_ref, p1_ref):
    g = pl.program_id(1)
    qb = pl.program_id(2)
    R = NSA_GROUP
    QT = Q_TILE
    KT = KEY_TILE
    NQ = R * QT
    t0 = qb * QT
    n_cmp_pad = kc_ref.shape[1]
    n_sel = ov_ref.shape[0]

    q_g = jnp.concatenate([qt_ref[0, 0, r * HEAD_DIM:(r + 1) * HEAD_DIM, :] for r in range(R)], axis=1)
    q2 = jnp.concatenate([q_g, q_g], axis=0)
    row_grp = lax.broadcasted_iota(jnp.int32, q2.shape, 0) // HEAD_DIM
    qpad = jnp.where(row_grp == g, q2, jnp.zeros_like(q2))

    tq_row = t0 + (lax.broadcasted_iota(jnp.int32, (1, NQ), 1) & (QT - 1))
    tile4 = lambda z: jnp.concatenate([z] * R, axis=1)

    NV = CMP_VARIANTS
    nq = ks_ref.shape[1] // QT

    def compressed_and_select(n_c, n_b):
        sc = _dot(kc_ref[0, 0:n_c, :], qpad)
        n_i = lax.broadcasted_iota(jnp.int32, (n_c, 1), 0)
        cend = jnp.where(n_i < n_cmp_pad - 1, n_i * CMP_STRIDE + (CMP_BLOCK - 1), jnp.int32(2 ** 30))
        cvalid = cend <= tq_row
        sc = jnp.where(cvalid, sc, NEG_INF)
        mc = jnp.max(sc, axis=0, keepdims=True)
        ec = jnp.where(cvalid, jnp.exp2(sc - mc), 0.0)
        pc = ec / jnp.maximum(jnp.sum(ec, axis=0, keepdims=True), F32_TINY)
        oc_ref[...] = _dot(vct_ref[0, :, 0:n_c], pc.astype(BF16))
        pc_sum = pc[:, 0:QT]
        for r in range(1, R):
            pc_sum = pc_sum + pc[:, r * QT:(r + 1) * QT]
        imp = _dot_split_rhs(ov_ref[0:n_b, 0:n_c], pc_sum)
        jb = lax.broadcasted_iota(jnp.int32, (n_b, QT), 0)
        jf = jb.astype(F32)
        tq_b = t0 + lax.broadcasted_iota(jnp.int32, (n_b, QT), 1)
        cur = tq_b >> SEL_SHIFT
        forced = (jb == 0) | (jb == cur) | (jb == cur - 1)
        visible = jb * SEL_BLOCK <= tq_b
        score = jnp.where(visible, jnp.where(forced, FORCE_SCORE, imp), -1.0)
        sel = jnp.zeros((n_b, QT), F32)
        for _ in range(min(SEL_TOPK, n_b)):
            mx = jnp.max(score, axis=0, keepdims=True)
            jmin = jnp.min(jnp.where(score == mx, jf, 1e9), axis=0, keepdims=True)
            hit = jf == jmin
            sel = jnp.where(hit, 1.0, sel)
            score = jnp.where(hit, -3e38, score)
        keep_ref[0:n_b, :] = jnp.where(visible, sel, 0.0)
        if n_b < n_sel:
            keep_ref[n_b:n_sel, :] = jnp.zeros((n_sel - n_b, QT), F32)

    for v in range(NV):
        @pl.when((qb * NV) // nq == v)
        def _():
            compressed_and_select((v + 1) * n_cmp_pad // NV, (v + 1) * n_sel // NV)

    o_c = oc_ref[...]
    ji = lax.broadcasted_iota(jnp.int32, (n_sel, QT), 0)

    ST = SEL_TILE
    bias_all = (keep_ref[...] - 1.0) * (-NEG_INF)
    first_own = t0 // SEL_BLOCK

    def with_bias_rows(bias):
        rows = tile4(bias).astype(BF16)
        if n_sel < LANES:
            rows = jnp.concatenate([rows, jnp.zeros((LANES - n_sel, NQ), BF16)], axis=0)
        return jnp.concatenate([qpad, rows], axis=0)

    rhs_ref[...] = with_bias_rows(jnp.where(ji < first_own, bias_all, NEG_INF))
    n_tiles = (t0 + ST - 1) // ST
    last_tile = ks_ref.shape[1] // ST - 1
    p_bufs = (p0_ref, p1_ref)

    def sel_scores(kt, s_ref):
        k0 = pl.multiple_of(jnp.minimum(kt, last_tile) * ST, ST)
        s_ref[...] = _dot(ks_ref[0, pl.ds(k0, ST), :], rhs_ref[...])

    def sel_values(kt, slot, acc, alpha):
        return acc * alpha + _dot(vst_ref[0, jnp.clip(kt, 0, last_tile)], p_bufs[slot][...])

    def sel_softmax(s_ref, slot, m):
        s = s_ref[...]
        m_new = jnp.maximum(m, jnp.max(s, axis=0, keepdims=True))
        p_bufs[slot][...] = jnp.exp2(s - m_new).astype(BF16)
        return m_new, jnp.exp2(m - m_new)

    def sel_pair(a, carry, s_now, s_next):
        m, acc, alpha0, alpha1 = carry
        acc = sel_values(a - 2, 0, acc, alpha0)
        acc = sel_values(a - 1, 1, acc, alpha1)
        sel_scores(a + 2, s_next[0])
        sel_scores(a + 3, s_next[1])
        m, alpha0 = sel_softmax(s_now[0], 0, m)
        m, alpha1 = sel_softmax(s_now[1], 1, m)
        return m, acc, alpha0, alpha1

    bufs_a, bufs_b = (s0_ref, s1_ref), (s2_ref, s3_ref)
    sel_scores(0, s0_ref)
    sel_scores(1, s1_ref)
    p0_ref[...] = jnp.zeros_like(p0_ref)
    p1_ref[...] = jnp.zeros_like(p1_ref)
    own = _dot(ks_ref[0, pl.ds(pl.multiple_of(t0, QT), QT), :], with_bias_rows(bias_all))

    n_wt = (WINDOW + QT) // KT
    k0w = pl.multiple_of(jnp.maximum(t0 - WINDOW, 0), KT)
    kt_w = k0w // KT
    keys_w = kw_ref[0, pl.ds(k0w, WINDOW + QT), :]
    dw = (t0 + lax.broadcasted_iota(jnp.int32, (WINDOW + QT, QT), 1)
          - (k0w + lax.broadcasted_iota(jnp.int32, (WINDOW + QT, QT), 0)))
    sw = _dot(keys_w, qpad) + tile4(jnp.where(dw >= 0, jnp.where(dw < WINDOW, 0.0, NEG_INF), NEG_INF))
    pw = jnp.exp2(sw - jnp.max(sw, axis=0, keepdims=True)).astype(BF16)
    acc_w = _dot(vwt_ref[0, kt_w], pw[0:KT])
    for j in range(1, n_wt):
        acc_w = acc_w + _dot(vwt_ref[0, kt_w + j], pw[j * KT:(j + 1) * KT])

    n_pairs = (n_tiles + 1) // 2
    one = jnp.ones((1, NQ), F32)
    m_s, acc_s, alpha0, alpha1 = lax.fori_loop(
        0, n_pairs,
        lambda j, carry: lax.cond(j % 2 == 0,
                                  lambda c: sel_pair(2 * j, c, bufs_a, bufs_b),
                                  lambda c: sel_pair(2 * j, c, bufs_b, bufs_a), carry),
        (jnp.full((1, NQ), NEG_INF, F32), jnp.zeros((V_ROWS, NQ), F32), one, one))
    acc_s = sel_values(2 * n_pairs - 2, 0, acc_s, alpha0)
    acc_s = sel_values(2 * n_pairs - 1, 1, acc_s, alpha1)
    seen = lax.broadcasted_iota(jnp.int32, (QT, QT), 0) <= lax.broadcasted_iota(jnp.int32, (QT, QT), 1)
    own = jnp.where(tile4(seen), own, NEG_INF)
    m_new = jnp.maximum(m_s, jnp.max(own, axis=0, keepdims=True))
    acc_s = acc_s * jnp.exp2(m_s - m_new) + _dot(vsd_ref[0, qb], jnp.exp2(own - m_new).astype(BF16))

    gates = _sigmoid(gt_ref[0, 0])
    grow = lambda j: jnp.concatenate([gates[j, r:r + 1, :] for r in range(R)], axis=1)
    D = HEAD_DIM
    o = (grow(0) * o_c + grow(1) * (acc_s[0:D] / acc_s[D:D + 1])
         + grow(2) * (acc_w[0:D] / acc_w[D:D + 1]))
    halves = []
    for h in range(R // 2):
        pair = jnp.concatenate([o[:, (2 * h) * QT:(2 * h + 1) * QT],
                                o[:, (2 * h + 1) * QT:(2 * h + 2) * QT]], axis=0)
        halves.append(pair.T)
    o_ref[...] = jnp.concatenate(halves, axis=1)


def _nsa_attn(qt, kcmp, vct, ks3, vst, vsd, kw3, vwt, gt, ov_t, B, T):
    G, R = NSA_KV_HEADS, NSA_GROUP
    nq = T // Q_TILE
    nk = T // KEY_TILE
    nch = kcmp.shape[1]
    n_sel = ov_t.shape[0]
    assert (T // SEL_TILE) % 2 == 0 and n_sel <= LANES and Q_TILE == KEY_TILE
    assert nq % CMP_VARIANTS == 0 and n_sel % (8 * CMP_VARIANTS) == 0 and nch % (8 * CMP_VARIANTS) == 0
    return pl.pallas_call(
        _nsa_attn_kernel,
        grid=(B, G, nq),
        in_specs=[pl.BlockSpec((1, 1, R * HEAD_DIM, Q_TILE), lambda b, g, q: (b, q, g, 0)),
                  pl.BlockSpec((1, nch, LANES), lambda b, g, q: (b, 0, 0)),
                  pl.BlockSpec((1, HEAD_DIM, nch), lambda b, g, q: (b, g, 0)),
                  pl.BlockSpec((1, T, 2 * LANES), lambda b, g, q: (b, 0, 0)),
                  pl.BlockSpec((1, T // SEL_TILE, V_ROWS, SEL_TILE), lambda b, g, q: (b, 0, g, 0)),
                  pl.BlockSpec((1, nk, V_ROWS, KEY_TILE), lambda b, g, q: (b, 0, g, 0)),
                  pl.BlockSpec((1, T, LANES), lambda b, g, q: (b, 0, 0)),
                  pl.BlockSpec((1, nk, V_ROWS, KEY_TILE), lambda b, g, q: (b, 0, g, 0)),
                  pl.BlockSpec((1, 1, 3, R, Q_TILE), lambda b, g, q: (b, g, 0, 0, q)),
                  pl.BlockSpec((n_sel, nch), lambda b, g, q: (0, 0))],
        out_specs=pl.BlockSpec((Q_TILE, R * HEAD_DIM), lambda b, g, q: (b * nq + q, g)),
        out_shape=jax.ShapeDtypeStruct((B * T, NSA_WIDTH), F32),
        scratch_shapes=[pltpu.VMEM((2 * LANES, R * Q_TILE), BF16), pltpu.VMEM((HEAD_DIM, R * Q_TILE), F32),
                        pltpu.VMEM((n_sel, Q_TILE), F32),
                        *[pltpu.VMEM((SEL_TILE, R * Q_TILE), F32)] * 4,
                        *[pltpu.VMEM((SEL_TILE, R * Q_TILE), BF16)] * 2],
        compiler_params=_params("arbitrary", "arbitrary", "arbitrary"),
        name="nsa_attn",
    )(qt, kcmp, vct, ks3, vst, vsd, kw3, vwt, gt, ov_t)


def _first_index_of(vals, target):
    idx = jnp.full_like(target, float(len(vals) - 1))
    for i in range(len(vals) - 2, -1, -1):
        idx = jnp.where(vals[i] == target, float(i), idx)
    return idx


def _pick(vals, idx):
    out = vals[-1]
    for i in range(len(vals) - 2, -1, -1):
        out = jnp.where(idx == float(i), vals[i], out)
    return out


def _route_rows(score, bias):
    E, G, P = N_EXPERTS, N_GROUPS, EXPERTS_PER_GROUP
    sel = score + bias
    s = [sel[e:e + 1, :] for e in range(E)]
    raw = [score[e:e + 1, :] for e in range(E)]
    grp = []
    for gi in range(G):
        a = s[gi * P:(gi + 1) * P]
        best = None
        for i in range(P):
            for j in range(i + 1, P):
                pair = a[i] + a[j]
                best = pair if best is None else jnp.maximum(best, pair)
        grp.append(best)
    gmax = functools.reduce(jnp.maximum, grp)
    g_star = _first_index_of(grp, gmax)
    v = [_pick([s[gi * P + i] for gi in range(G)], g_star) for i in range(P)]
    w = [_pick([raw[gi * P + i] for gi in range(G)], g_star) for i in range(P)]
    i1 = _first_index_of(v, functools.reduce(jnp.maximum, v))
    v2 = [jnp.where(i1 == float(i), -jnp.inf, v[i]) for i in range(P)]
    i2 = _first_index_of(v2, functools.reduce(jnp.maximum, v2))
    w1, w2 = _pick(w, i1), _pick(w, i2)
    tot = w1 + w2
    zero = jnp.zeros_like(tot)
    e1, e2 = g_star * P + i1, g_star * P + i2
    n = score.shape[1]
    eidx = lax.broadcasted_iota(jnp.int32, (E, n), 0).astype(F32)
    oh1, oh2 = jnp.where(eidx == e1, 1.0, 0.0), jnp.where(eidx == e2, 1.0, 0.0)
    earlier = jnp.where(lax.broadcasted_iota(jnp.int32, (n, n), 0) < lax.broadcasted_iota(jnp.int32, (n, n), 1),
                        1.0, 0.0).astype(BF16)
    cnt = _dot(jnp.concatenate([oh1, oh2], axis=0).astype(BF16), earlier)
    rank1 = jnp.sum(oh1 * cnt[0:E], axis=0, keepdims=True)
    rank2 = jnp.sum(oh2 * cnt[E:2 * E], axis=0, keepdims=True)
    lane = lax.broadcasted_iota(jnp.int32, (E, LANES), 1)
    totals = jnp.where(lane == 0, jnp.sum(oh1, axis=1, keepdims=True),
                       jnp.where(lane == 1, jnp.sum(oh2, axis=1, keepdims=True), 0.0))
    return jnp.concatenate([e1, e2, w1 / tot, w2 / tot, rank1, rank2, zero, zero], axis=0), totals


def _merge_kernel(ys_ref, g_ref, bonus_ref, gng_ref, gnb_ref, bd_ref, yb_ref, pm_ref, x_ref, mod_ref,
                  ng_ref, wa_ref, wb_ref, wo_ref, rw_ref, rb_ref, o_x, o_h, o_route, o_tot):
    m = mod_ref[pl.program_id(0)]
    bd = bd_ref[...]
    y = ys_ref[...]
    mean = _dot_split_lhs(y, bd)
    yc = y - mean
    var = _dot_split_lhs(yc * yc, bd)
    ya = (yc * lax.rsqrt(var + RWKV_GN_EPS) * gng_ref[...] + gnb_ref[...] + bonus_ref[...]) * g_ref[...]
    pm = pm_ref[...]
    D = x_ref.shape[1]
    mix = (_sigmoid(pm[:, 0:D]) * _dot(ya.astype(BF16), wa_ref[...])
           + _sigmoid(pm[:, D:2 * D]) * _dot(yb_ref[...].astype(BF16), wb_ref[...]))
    x = x_ref[...] + m[2:3] * _dot(mix.astype(BF16), wo_ref[...])
    o_x[...] = x
    ms = jnp.mean(x * x, axis=-1, keepdims=True)
    h = x * lax.rsqrt(ms + NORM_EPS) * ng_ref[...]
    h = h * (1.0 + m[4:5]) + m[3:4]
    o_h[...] = h
    score = _sigmoid(_dot_3pass(h, rw_ref).T[0:N_EXPERTS, :])
    o_route[...], o_tot[...] = _route_rows(score, rb_ref[...])


def _merge(ys, g, bonus, gn_g, gn_b, yb, pm, x2, mod, ng, wa, wb, wo, router_w, router_b, B, T, tm=256):
    N, D = x2.shape
    W = RWKV_WIDTH
    nt = T // tm
    row = lambda b, t: (b * nt + t, 0)
    full = lambda shape: pl.BlockSpec(shape, lambda b, t: (0,) * len(shape))
    return pl.pallas_call(
        _merge_kernel,
        grid=(B, nt),
        in_specs=[pl.BlockSpec((tm, W), row), pl.BlockSpec((tm, W), row), pl.BlockSpec((tm, W), row),
                  full((1, W)), full((1, W)), full((W, W)),
                  pl.BlockSpec((tm, NSA_WIDTH), row), pl.BlockSpec((tm, 2 * D), row),
                  pl.BlockSpec((tm, D), row), full((B, 6, D)), full((1, D)),
                  full((W, D)), full((NSA_WIDTH, D)), full((D, D)), full((2, D, LANES)),
                  full((N_EXPERTS, 1))],
        out_specs=[pl.BlockSpec((tm, D), row), pl.BlockSpec((tm, D), row),
                   pl.BlockSpec((8, tm), lambda b, t: (0, b * nt + t)),
                   pl.BlockSpec((N_EXPERTS, LANES), lambda b, t: (b * nt + t, 0))],
        out_shape=[jax.ShapeDtypeStruct((N, D), F32), jax.ShapeDtypeStruct((N, D), F32),
                   jax.ShapeDtypeStruct((8, N), F32), jax.ShapeDtypeStruct((N // tm * N_EXPERTS, LANES), F32)],
        compiler_params=_params("arbitrary", "arbitrary"),
        name="merge_out",
    )(ys, g, bonus, gn_g.reshape(1, W), gn_b.reshape(1, W), _head_block_diag(W, 1.0 / HEAD_DIM),
      yb, pm, x2, mod, ng, wa, wb, wo,
      _hi_lo(jnp.zeros((D, LANES), F32).at[:, :N_EXPERTS].set(router_w)), router_b.reshape(N_EXPERTS, 1))


def _route(route, totals, N):
    wts = route[TOP_K:2 * TOP_K].T
    NK = N * TOP_K
    E = N_EXPERTS
    n_tiles = totals.shape[0] // E
    expert = route[0:TOP_K].astype(jnp.int32)
    rank = route[2 * TOP_K:3 * TOP_K].astype(jnp.int32)
    per = totals.reshape(n_tiles, E, LANES)[:, :, 0:TOP_K].astype(jnp.int32).transpose(0, 2, 1)
    per = per.reshape(n_tiles * TOP_K, E)
    csum = jnp.cumsum(per, axis=0)
    counts = csum[-1]
    padded = (counts + MOE_BLOCK - 1) // MOE_BLOCK * MOE_BLOCK
    pad_end = jnp.cumsum(padded)
    pad_start = pad_end - padded
    first = (pad_start[None, :] + csum - per).reshape(n_tiles, TOP_K, E).transpose(1, 0, 2)
    first = jnp.repeat(first, N // n_tiles, axis=1)
    mine = expert[:, :, None] == jnp.arange(E, dtype=jnp.int32)[None, None, :]
    dest = (jnp.sum(jnp.where(mine, first, 0), axis=-1) + rank).reshape(-1)
    n_blk = -(-NK // MOE_BLOCK) + N_EXPERTS
    blk_start = jnp.arange(n_blk, dtype=jnp.int32) * MOE_BLOCK
    blk_expert = jnp.sum((pad_end[None, :] <= blk_start[:, None]).astype(jnp.int32), axis=1)
    blk_expert = jnp.clip(blk_expert, 0, N_EXPERTS - 1)
    blk_valid = jnp.clip((pad_start + counts)[blk_expert] - blk_start, 0, MOE_BLOCK).astype(jnp.int32)
    dest = jnp.pad(dest.astype(jnp.int32).reshape(NK // SC_WINDOW, SC_WINDOW), ((0, 0), (0, LANES - SC_WINDOW)))
    return wts, dest, blk_expert, blk_valid, n_blk


SC_WINDOW = 32


def _sc_mesh():
    return plsc.VectorSubcoreMesh(core_axis_name="c", subcore_axis_name="s")


def _sc_dispatch(h, dest, n_slots):
    N, D = h.shape
    W = SC_WINDOW
    nw = N // W

    @pl.kernel(out_type=jax.ShapeDtypeStruct((n_slots, D), h.dtype), mesh=_sc_mesh(), scratch_types=[])
    def dispatch(h_hbm, i_hbm, o_hbm):
        def body(x_vmem, i_vmem):
            pltpu.sync_copy(x_vmem, o_hbm.at[i_vmem.at[0, pl.ds(0, W)]])

        pltpu.emit_pipeline(
            body, grid=(TOP_K, nw),
            in_specs=[pl.BlockSpec((W, D), lambda k, i: (i, 0)),
                      pl.BlockSpec((1, LANES), lambda k, i: (k * nw + i, 0))],
            out_specs=[], core_axis_name=("c", "s"),
            dimension_semantics=(pltpu.PARALLEL, pltpu.PARALLEL))(h_hbm, i_hbm)

    return dispatch(h, dest)


def _sc_collect(ys, dest):
    W = SC_WINDOW
    NK = dest.shape[0] * W
    D = ys.shape[1]
    half = NK // TOP_K // W

    @pl.kernel(out_type=jax.ShapeDtypeStruct((NK, D), ys.dtype), mesh=_sc_mesh(), scratch_types=[])
    def collect(y_hbm, i_hbm, o_hbm):
        def body(i_vmem, o_vmem):
            pltpu.sync_copy(y_hbm.at[i_vmem.at[0, pl.ds(0, W)]], o_vmem)

        pltpu.emit_pipeline(
            body, grid=(TOP_K, half),
            in_specs=[pl.BlockSpec((1, LANES), lambda k, i: (k * half + i, 0))],
            out_specs=[pl.BlockSpec((W, D), lambda k, i: (k * half + i, 0))],
            core_axis_name=("c", "s"),
            dimension_semantics=(pltpu.PARALLEL, pltpu.PARALLEL))(i_hbm, o_hbm)

    return collect(ys, dest)


def _moe_dense_kernel(be_ref, nv_ref, x_ref, wg_ref, wu_ref, wd_ref, o_ref, wg_b, wu_b, wd_b):
    i = pl.program_id(0)
    nv = nv_ref[i]

    @pl.when((i == 0) | (be_ref[i] != be_ref[jnp.maximum(i - 1, 0)]))
    def _():
        wg_b[...] = wg_ref[0, 0].astype(BF16)
        wu_b[...] = wu_ref[0, 0].astype(BF16)
        wd_b[...] = wd_ref[0, 0].astype(BF16)

    @pl.when(nv > 0)
    def _():
        x = x_ref[...].astype(BF16)
        gate = _dot(x, wg_b[...])
        up = _dot(x, wu_b[...])
        o_ref[...] = _dot((gate * _sigmoid(gate) * up).astype(BF16), wd_b[...])

    @pl.when(nv == 0)
    def _():
        o_ref[...] = jnp.zeros_like(o_ref)


def _moe_dense(xs, blk_expert, blk_valid, n_blk, layer, wg, wu, wd):
    P, D = xs.shape
    DE = wg.shape[3]
    wmap = lambda i, be, nv: (layer, be[i], 0, 0)
    grid_spec = pltpu.PrefetchScalarGridSpec(
        num_scalar_prefetch=2,
        grid=(n_blk,),
        in_specs=[pl.BlockSpec((MOE_BLOCK, D), lambda i, be, nv: (i, 0)), pl.BlockSpec((1, 1, D, DE), wmap),
                  pl.BlockSpec((1, 1, D, DE), wmap), pl.BlockSpec((1, 1, DE, D), wmap)],
        out_specs=pl.BlockSpec((MOE_BLOCK, D), lambda i, be, nv: (i, 0)),
        scratch_shapes=[pltpu.VMEM((D, DE), BF16), pltpu.VMEM((D, DE), BF16), pltpu.VMEM((DE, D), BF16)],
    )
    return pl.pallas_call(
        _moe_dense_kernel,
        grid_spec=grid_spec,
        out_shape=jax.ShapeDtypeStruct((P, D), F32),
        compiler_params=_params("arbitrary"),
        name="moe_experts",
    )(blk_expert, blk_valid, xs, wg, wu, wd)


def _final_kernel(x_ref, y0_ref, y1_ref, w_ref, mod_ref, o_ref):
    m = mod_ref[pl.program_id(0)]
    w = w_ref[...]
    o_ref[...] = x_ref[...] + m[5:6] * (w[:, 0:1] * y0_ref[...] + w[:, 1:2] * y1_ref[...])


def _final(x2, ybuf, wts, mod, B, T, tm=512):
    N, D = x2.shape
    nt = T // tm
    row = lambda b, t: (b * nt + t, 0)
    return pl.pallas_call(
        _final_kernel,
        grid=(B, nt),
        in_specs=[pl.BlockSpec((tm, D), row), pl.BlockSpec((tm, D), row),
                  pl.BlockSpec((tm, D), lambda b, t: (N // tm + b * nt + t, 0)),
                  pl.BlockSpec((tm, TOP_K), row), pl.BlockSpec((B, 6, D), lambda b, t: (0, 0, 0))],
        out_specs=pl.BlockSpec((tm, D), row),
        out_shape=jax.ShapeDtypeStruct((N, D), F32),
        compiler_params=_params("arbitrary", "arbitrary"),
        name="moe_combine",
    )(x2, ybuf, ybuf, wts, mod)


def _overlap_t(n_sel, n_cmp_pad):
    ci = jnp.arange(n_cmp_pad)[None, :] * CMP_STRIDE
    sj = jnp.arange(n_sel)[:, None] * SEL_BLOCK
    ov = (ci <= sj + SEL_BLOCK - 1) & (ci + CMP_BLOCK - 1 >= sj) & (jnp.arange(n_cmp_pad)[None, :] < n_cmp_pad - 1)
    return ov.astype(BF16)


def kernel(x, c, w_ada, b_ada, norm_g, w_in, b_in, rwkv_mu, rwkv_w0, rwkv_w2, rwkv_a0, rwkv_a2, rwkv_g2,
           rwkv_k_k, rwkv_k_a, rwkv_r_k, rwkv_gn_g, rwkv_gn_b, qk_norm_g, cmp_pos, cmp_w1, cmp_w2,
           w_up_rwkv, w_up_nsa, w_out, router_w, router_b, exp_w_gate, exp_w_up, exp_w_down):
    B, T, D = x.shape
    L = w_ada.shape[0]
    N = B * T
    mods = _ada(c, w_ada, b_ada)
    tables = _rope_tables(jnp.arange(T, dtype=jnp.int32))
    nch = T // CMP_STRIDE
    tables_cmp = _rope_tables(jnp.arange(nch, dtype=jnp.int32) * CMP_STRIDE + CMP_BLOCK - 1)
    ov_t = _overlap_t(T // SEL_BLOCK, nch)
    n_gate = NSA_GATE_COLS
    x2 = x.reshape(N, D)
    for l in range(L):
        g0 = _SEG_KV[1] + n_gate
        w_pad = jnp.concatenate([w_in[l][:, :g0], jnp.zeros((D, GATE_PAD - n_gate), F32), w_in[l][:, g0:]],
                                axis=1).astype(BF16)
        b_pad = jnp.concatenate([b_in[l][:g0], jnp.zeros((GATE_PAD - n_gate,), F32), b_in[l][g0:]]).reshape(1, -1)
        p_rw, p_q, p_kv, p_gate, p_merge = _inproj(x2, mods[l], norm_g[l, 0].reshape(1, D), w_pad, b_pad, B, T)
        r, k, v, al, bb, ld, g, bonus = _rwkv_pre(p_rw, rwkv_mu[l], rwkv_w0[l], rwkv_w2[l], rwkv_a0[l],
                                                  rwkv_a2[l], rwkv_g2[l], rwkv_k_k[l], rwkv_k_a[l],
                                                  rwkv_r_k[l], B, T)
        ys = _rwkv_scan(r, k, v, al, bb, ld, B, T)
        qt, ks, kw, vst, vsd, vwt = _nsa_prep(p_q, p_kv, tables, qk_norm_g[l], B, T)
        kv3 = p_kv.reshape(B, T, KV_COLS)
        kcmp = _nsa_cmp(kv3, 0, cmp_pos[l], cmp_w1[l], cmp_w2[l], qk_norm_g[l, 1], tables_cmp)
        vct = _nsa_cmp(kv3, 1, cmp_pos[l], cmp_w1[l], cmp_w2[l], None, None)
        gt = p_gate[:, :n_gate].reshape(B, T, NSA_KV_HEADS, NSA_GROUP, 3).transpose(0, 2, 4, 3, 1)
        yb = _nsa_attn(qt, kcmp, vct, ks.reshape(B, T, 2 * LANES), vst, vsd, kw.reshape(B, T, LANES), vwt, gt, ov_t,
                       B, T)
        x2, h2, route, totals = _merge(ys, g, bonus, rwkv_gn_g[l], rwkv_gn_b[l], yb, p_merge, x2, mods[l],
                               norm_g[l, 1].reshape(1, D), w_up_rwkv[l].astype(BF16),
                               w_up_nsa[l].astype(BF16), w_out[l].astype(BF16), router_w, router_b, B, T)
        wts, dest, blk_expert, blk_valid, n_blk = _route(route, totals, N)
        xs = _sc_dispatch(h2, dest, n_blk * MOE_BLOCK)
        ys = _moe_dense(xs, blk_expert, blk_valid, n_blk, l, exp_w_gate, exp_w_up, exp_w_down)
        ybuf = _sc_collect(ys, dest)
        x2 = _final(x2, ybuf, wts, mods[l], B, T)
    return x2.reshape(B, T, D)
```

```python
import functools
import math

import jax
import jax.numpy as jnp
from jax import lax
from jax.experimental import pallas as pl
from jax.experimental.pallas import tpu as pltpu
from jax.experimental.pallas import tpu_sc as plsc

F32 = jnp.float32
BF16 = jnp.bfloat16
HI = lax.Precision.HIGHEST

D_MODEL = 1024
RWKV_HEADS = 8
HEAD_DIM = 64
RWKV_WIDTH = RWKV_HEADS * HEAD_DIM
DECAY_LORA = 64
ICLR_LORA = 64
GATE_LORA = 128
RWKV_GN_EPS = 64e-5
RWKV_COLS = 3 * RWKV_WIDTH + DECAY_LORA + ICLR_LORA + GATE_LORA

NSA_Q_HEADS = 8
NSA_KV_HEADS = 2
NSA_GROUP = NSA_Q_HEADS // NSA_KV_HEADS
NSA_WIDTH = NSA_Q_HEADS * HEAD_DIM
CMP_STRIDE = 16
CMP_BLOCK = 2 * CMP_STRIDE
CMP_HIDDEN = 256
SEL_BLOCK = 64
SEL_SHIFT = 6
SEL_TOPK = 16
WINDOW = 512
FORCE_SCORE = 1e4
NEG_INF = -1e30
ROPE_THETA = 500000.0
ROPE_DIM = HEAD_DIM // 4
KV_COLS = 6 * NSA_KV_HEADS * HEAD_DIM
NSA_GATE_COLS = 3 * NSA_Q_HEADS
GATE_PAD = 128

N_EXPERTS = 16
N_GROUPS = 4
EXPERTS_PER_GROUP = N_EXPERTS // N_GROUPS
TOP_K = 2
D_EXPERT = 512
MOE_BLOCK = 256
NORM_EPS = 1e-6

LANES = 128
CHUNK = 64
KEY_TILE = 128
SEL_TILE = 512
CMP_VARIANTS = 4
V_ROWS = 80
Q_SCALE = HEAD_DIM ** -0.5 * math.log2(math.e)
Q_TILE = 128
F32_TINY = float(jnp.finfo(jnp.float32).tiny)

_SEG_RW = (0, RWKV_COLS)
_SEG_Q = (_SEG_RW[1], _SEG_RW[1] + NSA_WIDTH)
_SEG_KV = (_SEG_Q[1], _SEG_Q[1] + KV_COLS)
_SEG_GATE = (_SEG_KV[1], _SEG_KV[1] + GATE_PAD)
_SEG_MERGE = (_SEG_GATE[1], _SEG_GATE[1] + 2 * D_MODEL)
IN_COLS_PAD = _SEG_MERGE[1]

_VMEM_LIMIT = 56 * 1024 * 1024


def _dot(a, b, precision=None):
    return jnp.dot(a, b, preferred_element_type=F32, precision=precision)


def _dot_tb(a, b, precision=None):
    return lax.dot_general(a, b, (((1,), (1,)), ((), ())), preferred_element_type=F32,
                           precision=precision)


def _dot_ta(a, b, precision=None):
    return lax.dot_general(a, b, (((0,), (0,)), ((), ())), preferred_element_type=F32,
                           precision=precision)


def _split_bf16(x, terms):
    parts = []
    for _ in range(terms - 1):
        parts.append(x.astype(BF16))
        x = x - parts[-1].astype(F32)
    parts.append(x.astype(BF16))
    return parts


def _dot_split_lhs(x, w_bf, terms=2):
    return functools.reduce(jnp.add, [_dot(p, w_bf) for p in _split_bf16(x, terms)])


def _dot_split_rhs(w_bf, x, terms=2):
    return functools.reduce(jnp.add, [_dot(w_bf, p) for p in _split_bf16(x, terms)])


def _dot_3pass(x, w_hl_ref):
    x_hi, x_lo = _split_bf16(x, 2)
    w_hi = w_hl_ref[0]
    return _dot(x_hi, w_hi) + _dot(x_lo, w_hi) + _dot(x_hi, w_hl_ref[1])


def _hi_lo(w):
    hi = w.astype(BF16)
    return jnp.stack([hi, (w - hi.astype(F32)).astype(BF16)])


def _params(*sem):
    return pltpu.CompilerParams(dimension_semantics=sem, vmem_limit_bytes=_VMEM_LIMIT)


def _sigmoid(x):
    return 1.0 / (1.0 + jnp.exp(-x))


def _ada_kernel(c_ref, w_ref, b_ref, o_ref):
    c = c_ref[...]
    s = c * _sigmoid(c)
    o_ref[0] = _dot(s, w_ref[0], HI) + b_ref[0]


def _ada(c, w_ada, b_ada):
    L, D, D6 = w_ada.shape
    B = c.shape[0]
    rows = 8
    cp = jnp.zeros((rows, D), F32).at[:B].set(c)
    tn = 1536
    out = pl.pallas_call(
        _ada_kernel,
        grid=(L, D6 // tn),
        in_specs=[pl.BlockSpec((rows, D), lambda l, j: (0, 0)),
                  pl.BlockSpec((1, D, tn), lambda l, j: (l, 0, j)),
                  pl.BlockSpec((1, 1, tn), lambda l, j: (l, 0, j))],
        out_specs=pl.BlockSpec((1, rows, tn), lambda l, j: (l, 0, j)),
        out_shape=jax.ShapeDtypeStruct((L, rows, D6), F32),
        compiler_params=_params("arbitrary", "arbitrary"),
        name="ada_mod",
    )(cp, w_ada, b_ada.reshape(L, 1, D6))
    return out[:, :B].reshape(L, B, 6, D)


def _inproj_kernel(x_ref, mod_ref, g_ref, w_ref, b_ref, o_rw, o_q, o_kv, o_gate, o_merge):
    m = mod_ref[pl.program_id(0)]
    x = x_ref[...]
    ms = jnp.mean(x * x, axis=-1, keepdims=True)
    h = x * lax.rsqrt(ms + NORM_EPS) * g_ref[...]
    h = h * (1.0 + m[1:2]) + m[0:1]
    hb = h.astype(BF16)
    for o, (a, e) in ((o_rw, _SEG_RW), (o_q, _SEG_Q), (o_kv, _SEG_KV), (o_gate, _SEG_GATE),
                      (o_merge, _SEG_MERGE)):
        o[...] = _dot(hb, w_ref[:, a:e]) + b_ref[:, a:e]


def _inproj(x2, mod, g, w_pad, b_pad, B, T, tm=256):
    N, D = x2.shape
    nt = T // tm
    row = lambda b, t: (b * nt + t, 0)
    widths = [e - a for a, e in (_SEG_RW, _SEG_Q, _SEG_KV, _SEG_GATE, _SEG_MERGE)]
    return pl.pallas_call(
        _inproj_kernel,
        grid=(B, nt),
        in_specs=[pl.BlockSpec((tm, D), row),
                  pl.BlockSpec((B, 6, D), lambda b, t: (0, 0, 0)),
                  pl.BlockSpec((1, D), lambda b, t: (0, 0)),
                  pl.BlockSpec((D, IN_COLS_PAD), lambda b, t: (0, 0)),
                  pl.BlockSpec((1, IN_COLS_PAD), lambda b, t: (0, 0))],
        out_specs=[pl.BlockSpec((tm, w), row) for w in widths],
        out_shape=[jax.ShapeDtypeStruct((N, w), F32) for w in widths],
        compiler_params=_params("arbitrary", "arbitrary"),
        name="in_proj",
    )(x2, mod, g, w_pad, b_pad)


def _rwkv_pre_kernel(p_ref, mu_ref, w0_ref, w2_ref, a0_ref, a2_ref, g2_ref, kk_ref, ka_ref, rk_ref,
                     bd_ref, o_r, o_k, o_v, o_al, o_b, o_ld, o_g, o_bonus, carry_ref):
    W = RWKV_WIDTH

    @pl.when(pl.program_id(1) == 0)
    def _():
        carry_ref[...] = jnp.zeros_like(carry_ref)

    p = p_ref[...]
    ts = p.shape[0]
    rows = lax.broadcasted_iota(jnp.int32, p.shape, 0)
    shifted = jnp.where(rows == 0, carry_ref[0:1, :], pltpu.roll(p, 1, 0))
    carry_ref[0:1, :] = p[ts - 1:ts, :]
    pm = p + (shifted - p) * mu_ref[...]
    r = pm[:, 0:W]
    k = pm[:, W:2 * W]
    v = pm[:, 2 * W:3 * W]
    wa = pm[:, 3 * W:3 * W + DECAY_LORA + ICLR_LORA]
    gl = pm[:, 3 * W + DECAY_LORA + ICLR_LORA:]
    xw = w0_ref[...] + _dot_3pass(jnp.tanh(wa), w2_ref)
    ld = -math.exp(-0.5) * _sigmoid(xw)
    a = _sigmoid(a0_ref[...] + _dot_3pass(wa, a2_ref))
    g = _dot_3pass(_sigmoid(gl), g2_ref)
    bd = bd_ref[...]
    kk = k * kk_ref[...]
    nrm = jnp.sqrt(_dot_split_lhs(kk * kk, bd))
    kk = kk / jnp.maximum(nrm, 1e-12)
    k2 = k * (1.0 + (a - 1.0) * ka_ref[...])
    bonus = _dot_split_lhs(r * k2 * rk_ref[...], bd) * v
    o_r[...] = r
    o_k[...] = k2
    o_v[...] = v
    o_al[...] = kk
    o_b[...] = -kk * a
    o_ld[...] = ld
    o_g[...] = g
    o_bonus[...] = bonus


def _head_block_diag(width, scale=1.0):
    i = jnp.arange(width) // HEAD_DIM
    return ((i[:, None] == i[None, :]).astype(F32) * scale).astype(BF16)


def _rwkv_pre(p_rw, mu, w0, w2, a0, a2, g2, k_k, k_a, r_k, B, T, ts=256):
    N = p_rw.shape[0]
    W = RWKV_WIDTH
    nt = T // ts
    row = lambda b, t: (b * nt + t, 0)
    zl = jnp.zeros((DECAY_LORA, W), F32)
    w2p = jnp.concatenate([w2, zl], axis=0)
    a2p = jnp.concatenate([zl, a2], axis=0)
    full = lambda shape: pl.BlockSpec(shape, lambda b, t: (0,) * len(shape))
    vec = lambda z: z.reshape(1, -1)
    return pl.pallas_call(
        _rwkv_pre_kernel,
        grid=(B, nt),
        in_specs=[pl.BlockSpec((ts, RWKV_COLS), row), full((1, RWKV_COLS)), full((1, W)),
                  full((2, 2 * DECAY_LORA, W)), full((1, W)), full((2, 2 * DECAY_LORA, W)),
                  full((2, GATE_LORA, W)), full((1, W)), full((1, W)), full((1, W)), full((W, W))],
        out_specs=[pl.BlockSpec((ts, W), row)] * 8,
        out_shape=[jax.ShapeDtypeStruct((N, W), F32)] * 8,
        scratch_shapes=[pltpu.VMEM((8, RWKV_COLS), F32)],
        compiler_params=_params("arbitrary", "arbitrary"),
        name="rwkv_pre",
    )(p_rw, vec(mu), vec(w0), _hi_lo(w2p), vec(a0), _hi_lo(a2p), _hi_lo(g2), vec(k_k), vec(k_a), vec(r_k),
      _head_block_diag(W))


def _bf(x):
    return x.astype(BF16)


def _scan_local(chunks, eye, strict, incl, m0, m1, between_stages=lambda: None):
    C = CHUNK
    n = range(len(chunks))
    st = lambda z: jnp.concatenate([z * m0, z * m1], axis=0)
    zero = jnp.zeros((2 * C, 2 * C), F32)
    at_b, rt_s, vs, vs_b, lhs_a, rhs_a, bk_t, dcol = [], [], [], [], [], [], [], []
    for r, k, v, al, bb, ld, cum in chunks:
        tot = cum[C - 1:C, :]
        dinv = jnp.exp(-cum)
        dend = jnp.exp(tot - cum)
        at_b.append(_bf(st(al * jnp.exp(cum - ld))))
        rt_s.append(st(r * jnp.exp(cum)))
        vs.append(st(v))
        vs_b.append(_bf(vs[-1]))
        lhs_a.append(jnp.concatenate([at_b[-1], _bf(rt_s[-1])], axis=0))
        rhs_a.append(_bf(jnp.concatenate([st(bb * dinv), st(k * dinv)], axis=0)))
        bk_t.append(_bf(jnp.concatenate([st(bb * dend).T, st(k * dend).T], axis=1)))
        dcol.append(jnp.sum(eye * jnp.exp(tot), axis=1, keepdims=True))
    between_stages()
    A = [_dot_tb(lhs_a[i], rhs_a[i]) for i in n]
    between_stages()
    a_ab = [jnp.where(strict, A[i][0:2 * C, 0:2 * C], zero) for i in n]
    a_ak = [_bf(jnp.where(strict, A[i][0:2 * C, 2 * C:4 * C], zero)) for i in n]
    a_r = [_bf(jnp.concatenate([jnp.where(incl, A[i][2 * C:4 * C, 0:2 * C], zero),
                                jnp.where(incl, A[i][2 * C:4 * C, 2 * C:4 * C], zero)], axis=1)) for i in n]
    akv = [_bf(_dot(a_ak[i], vs_b[i])) for i in n]
    between_stages()
    pw = a_ab
    tinv = [eye + pw[i] for i in n]
    for _ in range(5):
        pw_b = [_bf(pw[i]) for i in n]
        pw = [_dot(pw_b[i], pw_b[i]) for i in n]
        tinv = [tinv[i] + _dot(_bf(pw[i]), _bf(tinv[i])) for i in n]
        between_stages()
    X = [_dot(_bf(tinv[i]), jnp.concatenate([at_b[i], akv[i]], axis=1)) for i in n]
    between_stages()
    w_b = [_bf(X[i][:, 0:LANES]) for i in n]
    uv0 = [jnp.concatenate([_bf(X[i][:, LANES:2 * LANES]), vs_b[i]], axis=0) for i in n]
    m_h = [_bf(_dot(bk_t[i][:, 0:2 * C], w_b[i])) for i in n]
    g_h = [_dot(bk_t[i], uv0[i]) for i in n]
    between_stages()
    q_h = [_bf(rt_s[i] + _dot(a_r[i][:, 0:2 * C], w_b[i])) for i in n]
    y0 = [_dot(a_r[i], uv0[i]) for i in n]
    return [(m_h[i], g_h[i], dcol[i], q_h[i], y0[i]) for i in n]


def _rwkv_scan_kernel(r_ref, k_ref, v_ref, al_ref, b_ref, ld_ref, o_ref, h_ref, *local_refs):
    C = CHUNK
    tc = r_ref.shape[0]
    nc = tc // C

    @pl.when(pl.program_id(2) == 0)
    def _():
        h_ref[...] = jnp.zeros_like(h_ref)
        for ref in local_refs:
            ref[...] = jnp.zeros_like(ref)

    seq = {"H": h_ref[...], "c": 0}

    def one_step():
        c = seq["c"]
        if c < nc:
            m_h, g_h, dcol, q_h, y0 = (ref[c] for ref in local_refs)
            h_b = _bf(seq["H"])
            Y = _dot(q_h, h_b) + y0
            o_ref[c * C:(c + 1) * C, :] = Y[0:C] + Y[C:2 * C]
            seq["H"] = dcol * seq["H"] + _dot(m_h, h_b) + g_h
            seq["c"] = c + 1

    tri = jnp.where(lax.broadcasted_iota(jnp.int32, (C, C), 1) <= lax.broadcasted_iota(jnp.int32, (C, C), 0),
                    1.0, 0.0).astype(BF16)
    r2 = lax.broadcasted_iota(jnp.int32, (2 * C, 2 * C), 0)
    c2 = lax.broadcasted_iota(jnp.int32, (2 * C, 2 * C), 1)
    eye = (r2 == c2).astype(F32)
    strict = (c2 & (C - 1)) < (r2 & (C - 1))
    incl = (c2 & (C - 1)) <= (r2 & (C - 1))
    lane = lax.broadcasted_iota(jnp.int32, (C, LANES), 1)
    m0 = (lane < HEAD_DIM).astype(F32)
    m1 = 1.0 - m0
    cum = _dot_split_rhs(tri, jnp.concatenate([ld_ref[c * C:(c + 1) * C, :] for c in range(nc)], axis=1), 3)
    chunks = []
    for c in range(nc):
        sl = slice(c * C, (c + 1) * C)
        chunks.append((r_ref[sl, :], k_ref[sl, :], v_ref[sl, :], al_ref[sl, :], b_ref[sl, :], ld_ref[sl, :],
                       cum[:, c * LANES:(c + 1) * LANES]))
    local = _scan_local(chunks, eye, strict, incl, m0, m1, between_stages=one_step)
    while seq["c"] < nc:
        one_step()
    h_ref[...] = seq["H"]
    for c, parts in enumerate(local):
        for ref, part in zip(local_refs, parts):
            ref[c] = part


def _rwkv_scan(r, k, v, al, bb, ld, B, T, tc=512):
    N, W = r.shape
    nt = T // tc
    nc = tc // CHUNK
    in_spec = pl.BlockSpec((tc, LANES), lambda b, h, t: (b * nt + jnp.minimum(t, nt - 1), h))
    out_spec = pl.BlockSpec((tc, LANES), lambda b, h, t: (b * nt + jnp.maximum(t - 1, 0), h))
    sq = (nc, LANES, LANES)
    return pl.pallas_call(
        _rwkv_scan_kernel,
        grid=(B, W // LANES, nt + 1),
        in_specs=[in_spec] * 6,
        out_specs=out_spec,
        out_shape=jax.ShapeDtypeStruct((N, W), F32),
        scratch_shapes=[pltpu.VMEM((LANES, LANES), F32), pltpu.VMEM(sq, BF16), pltpu.VMEM(sq, F32),
                        pltpu.VMEM((nc, LANES, 1), F32), pltpu.VMEM(sq, BF16), pltpu.VMEM(sq, F32)],
        compiler_params=_params("arbitrary", "arbitrary", "arbitrary"),
        name="rwkv_scan",
    )(r, k, v, al, bb, ld)


def _rope_tables(pos):
    half = ROPE_DIM // 2
    inv = jnp.power(ROPE_THETA, -jnp.arange(half, dtype=F32) * 2.0 / ROPE_DIM)
    ang = pos.astype(F32)[:, None] * inv[None, :]
    cos, sin = jnp.cos(ang), jnp.sin(ang)
    n = pos.shape[0]
    rest = HEAD_DIM - ROPE_DIM
    c = jnp.concatenate([cos, cos, jnp.ones((n, rest), F32)], axis=1)
    s_dn = jnp.concatenate([-sin, jnp.zeros((n, half + rest), F32)], axis=1)
    s_up = jnp.concatenate([jnp.zeros((n, half), F32), sin, jnp.zeros((n, rest), F32)], axis=1)
    rep = LANES // HEAD_DIM
    return jnp.tile(c, (1, rep)), jnp.tile(s_dn, (1, rep)), jnp.tile(s_up, (1, rep))


def _norm_rope(x, bd, g, c, s_dn, s_up):
    width = x.shape[1]
    half = ROPE_DIM // 2
    rep = width // LANES
    tile = (lambda z: jnp.concatenate([z] * rep, axis=1)) if rep > 1 else (lambda z: z)
    ms = _dot_split_lhs(x * x, bd)
    xn = x * lax.rsqrt(ms + NORM_EPS) * g
    return (xn * tile(c) + pltpu.roll(xn, width - half, 1) * tile(s_dn)
            + pltpu.roll(xn, half, 1) * tile(s_up))


def _nsa_prep_kernel(q_ref, kv_ref, c_ref, sd_ref, su_ref, gq_ref, gs_ref, gw_ref, bdq_ref, bdk_ref,
                     o_qt, o_ks, o_kw, o_vst, o_vsd, o_vwt):
    c, sd, su = c_ref[...], sd_ref[...], su_ref[...]
    q = _norm_rope(q_ref[...], bdq_ref[...], gq_ref[...], c, sd, su) * Q_SCALE
    qt = q.T
    ts = q.shape[0]
    kv = kv_ref[...]
    bdk = bdk_ref[...]
    pos = pl.program_id(1) * ts + lax.broadcasted_iota(jnp.int32, (ts, LANES), 0)
    blk_onehot = jnp.where((pos >> SEL_SHIFT) == lax.broadcasted_iota(jnp.int32, (ts, LANES), 1), 1.0, 0.0)
    ks = _norm_rope(kv[:, 2 * LANES:3 * LANES], bdk, gs_ref[...], c, sd, su)
    o_ks[...] = jnp.concatenate([ks, blk_onehot], axis=1).astype(BF16)
    o_kw[...] = _norm_rope(kv[:, 4 * LANES:5 * LANES], bdk, gw_ref[...], c, sd, su).astype(BF16)
    ones_rows = jnp.where(lax.broadcasted_iota(jnp.int32, (V_ROWS - HEAD_DIM, q.shape[0]), 0) == 0, 1.0, 0.0)

    def values_t(x):
        xt = x.T
        return jnp.concatenate([xt[0:HEAD_DIM], ones_rows, xt[HEAD_DIM:2 * HEAD_DIM], ones_rows], axis=0)

    vst = values_t(kv[:, 3 * LANES:4 * LANES])
    vwt = values_t(kv[:, 5 * LANES:6 * LANES])
    for j in range(q.shape[0] // KEY_TILE):
        sl = slice(j * KEY_TILE, (j + 1) * KEY_TILE)
        o_qt[0, j] = qt[:, sl].astype(BF16)
        o_vsd[0, j] = vst[:, sl].astype(BF16)
        o_vwt[0, j] = vwt[:, sl].astype(BF16)
    for j in range(q.shape[0] // SEL_TILE):
        o_vst[0, j] = vst[:, j * SEL_TILE:(j + 1) * SEL_TILE].astype(BF16)


def _nsa_prep(q, kv, tables, qk_g, B, T, ts=512):
    N = q.shape[0]
    nt = T // ts
    nk = ts // KEY_TILE
    ns = ts // SEL_TILE
    row = lambda b, t: (b * nt + t, 0)
    full = lambda shape: pl.BlockSpec(shape, lambda b, t: (0,) * len(shape))
    tab = pl.BlockSpec((ts, LANES), lambda b, t: (t, 0))
    gq = jnp.tile(qk_g[0], NSA_Q_HEADS).reshape(1, NSA_WIDTH)
    gs = jnp.tile(qk_g[2], NSA_KV_HEADS).reshape(1, LANES)
    gw = jnp.tile(qk_g[3], NSA_KV_HEADS).reshape(1, LANES)
    tiled = lambda rows: pl.BlockSpec((1, nk, rows, KEY_TILE), lambda b, t: (b, t, 0, 0))
    return pl.pallas_call(
        _nsa_prep_kernel,
        grid=(B, nt),
        in_specs=[pl.BlockSpec((ts, NSA_WIDTH), row), pl.BlockSpec((ts, KV_COLS), row), tab, tab, tab,
                  full((1, NSA_WIDTH)), full((1, LANES)), full((1, LANES)),
                  full((NSA_WIDTH, NSA_WIDTH)), full((LANES, LANES))],
        out_specs=[tiled(NSA_WIDTH), pl.BlockSpec((ts, 2 * LANES), row), pl.BlockSpec((ts, LANES), row),
                   pl.BlockSpec((1, ns, NSA_KV_HEADS * V_ROWS, SEL_TILE), lambda b, t: (b, t, 0, 0)),
                   tiled(NSA_KV_HEADS * V_ROWS), tiled(NSA_KV_HEADS * V_ROWS)],
        out_shape=[jax.ShapeDtypeStruct((B, T // KEY_TILE, NSA_WIDTH, KEY_TILE), BF16),
                   jax.ShapeDtypeStruct((N, 2 * LANES), BF16), jax.ShapeDtypeStruct((N, LANES), BF16),
                   jax.ShapeDtypeStruct((B, T // SEL_TILE, NSA_KV_HEADS * V_ROWS, SEL_TILE), BF16),
                   jax.ShapeDtypeStruct((B, T // KEY_TILE, NSA_KV_HEADS * V_ROWS, KEY_TILE), BF16),
                   jax.ShapeDtypeStruct((B, T // KEY_TILE, NSA_KV_HEADS * V_ROWS, KEY_TILE), BF16)],
        compiler_params=_params("arbitrary", "arbitrary"),
        name="nsa_prep",
    )(q, kv, *tables, gq, gs, gw, _head_block_diag(NSA_WIDTH, 1.0 / HEAD_DIM),
      _head_block_diag(LANES, 1.0 / HEAD_DIM))


def _gelu_tanh(x):
    return 0.5 * x * (1.0 + jnp.tanh(0.7978845608028654 * (x + 0.044715 * x * x * x)))


def _nsa_cmp_kernel(x_ref, pos_ref, w1_ref, w2_ref, *rest, is_key):
    if is_key:
        g_ref, c_ref, sd_ref, su_ref, bd_ref, o_ref, xs_ref = rest
    else:
        o_ref, xs_ref = rest
    nch = xs_ref.shape[0]
    S = CMP_STRIDE
    for j in range(S):
        xs_ref[:, j * LANES:(j + 1) * LANES] = x_ref[0, pl.ds(j, nch, stride=S), :]
    xs = xs_ref[...]
    first = _dot((xs + pos_ref[0:1, :]).astype(BF16), w1_ref[0])
    second = _dot((xs + pos_ref[1:2, :]).astype(BF16), w1_ref[1])
    hid = first + pltpu.roll(second, nch - 1, 0)
    out = _dot(_gelu_tanh(hid).astype(BF16), w2_ref[...])
    rows = lax.broadcasted_iota(jnp.int32, out.shape, 0)
    if is_key:
        out = _norm_rope(out, bd_ref[...], g_ref[...], c_ref[...], sd_ref[...], su_ref[...])
        o_ref[0] = jnp.where(rows < nch - 1, out, 0.0).astype(BF16)
    else:
        o_ref[0] = jnp.where(rows < nch - 1, out, 0.0).T.astype(BF16)


def _nsa_cmp(kv3, which, cmp_pos, cmp_w1, cmp_w2, g_k, tables_cmp):
    B, T, _ = kv3.shape
    S = CMP_STRIDE
    nch = T // S
    is_key = which == 0
    eye2 = jnp.eye(NSA_KV_HEADS, dtype=F32)
    w1 = cmp_w1[which].reshape(CMP_BLOCK, HEAD_DIM, CMP_HIDDEN)
    w1 = jnp.einsum('jdh,ge->jgdeh', w1, eye2).reshape(2, S * LANES, NSA_KV_HEADS * CMP_HIDDEN)
    w2 = jnp.einsum('hd,ge->ghed', cmp_w2[which], eye2).reshape(NSA_KV_HEADS * CMP_HIDDEN, LANES)
    pos = jnp.tile(cmp_pos[which].reshape(2, S, 1, HEAD_DIM), (1, 1, NSA_KV_HEADS, 1)).reshape(2, S * LANES)
    full = lambda shape: pl.BlockSpec(shape, lambda b: (0,) * len(shape))
    in_specs = [pl.BlockSpec((1, T, LANES), lambda b: (b, 0, which)), full(pos.shape), full(w1.shape),
                full(w2.shape)]
    args = [kv3, pos, w1.astype(BF16), w2.astype(BF16)]
    if is_key:
        in_specs += [full((1, LANES)), full((nch, LANES)), full((nch, LANES)), full((nch, LANES)),
                     full((LANES, LANES))]
        args += [jnp.tile(g_k, NSA_KV_HEADS).reshape(1, LANES), *tables_cmp,
                 _head_block_diag(LANES, 1.0 / HEAD_DIM)]
        out_spec = pl.BlockSpec((1, nch, LANES), lambda b: (b, 0, 0))
        out_shape = jax.ShapeDtypeStruct((B, nch, LANES), BF16)
    else:
        out_spec = pl.BlockSpec((1, LANES, nch), lambda b: (b, 0, 0))
        out_shape = jax.ShapeDtypeStruct((B, LANES, nch), BF16)
    return pl.pallas_call(
        functools.partial(_nsa_cmp_kernel, is_key=is_key),
        grid=(B,),
        in_specs=in_specs,
        out_specs=out_spec,
        out_shape=out_shape,
        scratch_shapes=[pltpu.VMEM((nch, S * LANES), F32)],
        compiler_params=_params("arbitrary"),
        name="nsa_cmp_k" if is_key else "nsa_cmp_v",
    )(*args)


def _nsa_attn_kernel(qt_ref, kc_ref, vct_ref, ks_ref, vst_ref, vsd_ref, kw_ref, vwt_ref, gt_ref, ov_ref, o_ref,
                     rhs_ref, oc_ref, keep_ref, s0_ref, s1_ref, s2_ref, s3_ref, p0_ref, p1_ref):
    qb = pl.program_id(1)
    G = NSA_KV_HEADS
    R = NSA_GROUP
    QT = Q_TILE
    KT = KEY_TILE
    CG = R * QT
    NQ = G * CG
    D = HEAD_DIM
    t0 = qb * QT
    n_cmp_pad = kc_ref.shape[1]
    n_sel = ov_ref.shape[0]
    cols = lambda g: slice(g * CG, (g + 1) * CG)

    q_cols = []
    for g in range(G):
        q_g = jnp.concatenate([qt_ref[0, 0, (g * R + r) * D:(g * R + r + 1) * D, :] for r in range(R)], axis=1)
        q_cols.append(jnp.concatenate([q_g if gg == g else jnp.zeros_like(q_g) for gg in range(G)], axis=0))
    qpad = jnp.concatenate(q_cols, axis=1)

    tq_row = t0 + (lax.broadcasted_iota(jnp.int32, (1, NQ), 1) & (QT - 1))
    spread = lambda z: jnp.concatenate([z[:, g * QT:(g + 1) * QT] for g in range(G) for _ in range(R)], axis=1)
    tile_all = lambda z: jnp.concatenate([z] * (G * R), axis=1)

    def values_dot(v_of_group, p):
        return jnp.concatenate([_dot(v_of_group(g), p[:, cols(g)]) for g in range(G)], axis=1)

    NV = CMP_VARIANTS
    nq = ks_ref.shape[1] // QT

    def compressed_and_select(n_c, n_b):
        sc = _dot(kc_ref[0, 0:n_c, :], qpad)
        n_i = lax.broadcasted_iota(jnp.int32, (n_c, 1), 0)
        cend = jnp.where(n_i < n_cmp_pad - 1, n_i * CMP_STRIDE + (CMP_BLOCK - 1), jnp.int32(2 ** 30))
        cvalid = cend <= tq_row
        sc = jnp.where(cvalid, sc, NEG_INF)
        mc = jnp.max(sc, axis=0, keepdims=True)
        ec = jnp.where(cvalid, jnp.exp2(sc - mc), 0.0)
        pc = ec / jnp.maximum(jnp.sum(ec, axis=0, keepdims=True), F32_TINY)
        pc_b = pc.astype(BF16)
        oc_ref[...] = values_dot(lambda g: vct_ref[0, g * D:(g + 1) * D, 0:n_c], pc_b)
        sums = []
        for g in range(G):
            acc = pc[:, g * CG:g * CG + QT]
            for r in range(1, R):
                acc = acc + pc[:, g * CG + r * QT:g * CG + (r + 1) * QT]
            sums.append(acc)
        imp = _dot_split_rhs(ov_ref[0:n_b, 0:n_c], jnp.concatenate(sums, axis=1))
        jb = lax.broadcasted_iota(jnp.int32, (n_b, G * QT), 0)
        jf = jb.astype(F32)
        tq_b = t0 + (lax.broadcasted_iota(jnp.int32, (n_b, G * QT), 1) & (QT - 1))
        cur = tq_b >> SEL_SHIFT
        forced = (jb == 0) | (jb == cur) | (jb == cur - 1)
        visible = jb * SEL_BLOCK <= tq_b
        score = jnp.where(visible, jnp.where(forced, FORCE_SCORE, imp), -1.0)
        sel = jnp.zeros((n_b, G * QT), F32)
        for _ in range(min(SEL_TOPK, n_b)):
            mx = jnp.max(score, axis=0, keepdims=True)
            jmin = jnp.min(jnp.where(score == mx, jf, 1e9), axis=0, keepdims=True)
            hit = jf == jmin
            sel = jnp.where(hit, 1.0, sel)
            score = jnp.where(hit, -3e38, score)
        keep_ref[0:n_b, :] = jnp.where(visible, sel, 0.0)
        if n_b < n_sel:
            keep_ref[n_b:n_sel, :] = jnp.zeros((n_sel - n_b, G * QT), F32)

    for v in range(NV):
        @pl.when((qb * NV) // nq == v)
        def _():
            compressed_and_select((v + 1) * n_cmp_pad // NV, (v + 1) * n_sel // NV)

    o_c = oc_ref[...]
    ji = lax.broadcasted_iota(jnp.int32, (n_sel, G * QT), 0)

    ST = SEL_TILE
    bias_all = (keep_ref[...] - 1.0) * (-NEG_INF)
    first_own = t0 // SEL_BLOCK
    vrows = lambda g: slice(g * V_ROWS, (g + 1) * V_ROWS)

    def with_bias_rows(bias):
        rows = spread(bias).astype(BF16)
        if n_sel < LANES:
            rows = jnp.concatenate([rows, jnp.zeros((LANES - n_sel, NQ), BF16)], axis=0)
        return jnp.concatenate([qpad, rows], axis=0)

    rhs_ref[...] = with_bias_rows(jnp.where(ji < first_own, bias_all, NEG_INF))
    n_tiles = (t0 + ST - 1) // ST
    last_tile = ks_ref.shape[1] // ST - 1
    p_bufs = (p0_ref, p1_ref)

    def sel_scores(kt, s_ref):
        k0 = pl.multiple_of(jnp.minimum(kt, last_tile) * ST, ST)
        s_ref[...] = _dot(ks_ref[0, pl.ds(k0, ST), :], rhs_ref[...])

    def sel_values(kt, slot, acc, alpha):
        kt = jnp.clip(kt, 0, last_tile)
        return acc * alpha + values_dot(lambda g: vst_ref[0, kt, vrows(g), :], p_bufs[slot][...])

    def sel_softmax(s_ref, slot, m):
        s = s_ref[...]
        m_new = jnp.maximum(m, jnp.max(s, axis=0, keepdims=True))
        p_bufs[slot][...] = jnp.exp2(s - m_new).astype(BF16)
        return m_new, jnp.exp2(m - m_new)

    def sel_pair(a, carry, s_now, s_next):
        m, acc, alpha0, alpha1 = carry
        acc = sel_values(a - 2, 0, acc, alpha0)
        acc = sel_values(a - 1, 1, acc, alpha1)
        sel_scores(a + 2, s_next[0])
        sel_scores(a + 3, s_next[1])
        m, alpha0 = sel_softmax(s_now[0], 0, m)
        m, alpha1 = sel_softmax(s_now[1], 1, m)
        return m, acc, alpha0, alpha1

    bufs_a, bufs_b = (s0_ref, s1_ref), (s2_ref, s3_ref)
    sel_scores(0, s0_ref)
    sel_scores(1, s1_ref)
    p0_ref[...] = jnp.zeros_like(p0_ref)
    p1_ref[...] = jnp.zeros_like(p1_ref)
    own = _dot(ks_ref[0, pl.ds(pl.multiple_of(t0, QT), QT), :], with_bias_rows(bias_all))

    n_wt = (WINDOW + QT) // KT
    k0w = pl.multiple_of(jnp.maximum(t0 - WINDOW, 0), KT)
    kt_w = k0w // KT
    keys_w = kw_ref[0, pl.ds(k0w, WINDOW + QT), :]
    dw = (t0 + lax.broadcasted_iota(jnp.int32, (WINDOW + QT, QT), 1)
          - (k0w + lax.broadcasted_iota(jnp.int32, (WINDOW + QT, QT), 0)))
    sw = _dot(keys_w, qpad) + tile_all(jnp.where(dw >= 0, jnp.where(dw < WINDOW, 0.0, NEG_INF), NEG_INF))
    pw = jnp.exp2(sw - jnp.max(sw, axis=0, keepdims=True)).astype(BF16)
    acc_w = values_dot(lambda g: vwt_ref[0, kt_w, vrows(g), :], pw[0:KT])
    for j in range(1, n_wt):
        acc_w = acc_w + values_dot(lambda g: vwt_ref[0, kt_w + j, vrows(g), :], pw[j * KT:(j + 1) * KT])

    n_pairs = (n_tiles + 1) // 2
    one = jnp.ones((1, NQ), F32)
    m_s, acc_s, alpha0, alpha1 = lax.fori_loop(
        0, n_pairs,
        lambda j, carry: lax.cond(j % 2 == 0,
                                  lambda c: sel_pair(2 * j, c, bufs_a, bufs_b),
                                  lambda c: sel_pair(2 * j, c, bufs_b, bufs_a), carry),
        (jnp.full((1, NQ), NEG_INF, F32), jnp.zeros((V_ROWS, NQ), F32), one, one))
    acc_s = sel_values(2 * n_pairs - 2, 0, acc_s, alpha0)
    acc_s = sel_values(2 * n_pairs - 1, 1, acc_s, alpha1)
    seen = lax.broadcasted_iota(jnp.int32, (QT, QT), 0) <= lax.broadcasted_iota(jnp.int32, (QT, QT), 1)
    own = jnp.where(tile_all(seen), own, NEG_INF)
    m_new = jnp.maximum(m_s, jnp.max(own, axis=0, keepdims=True))
    acc_s = acc_s * jnp.exp2(m_s - m_new) + values_dot(lambda g: vsd_ref[0, qb, vrows(g), :],
                                                       jnp.exp2(own - m_new).astype(BF16))

    gates = _sigmoid(gt_ref[0])
    grow = lambda j: jnp.concatenate([gates[g, j, r:r + 1, :] for g in range(G) for r in range(R)], axis=1)
    o = (grow(0) * o_c + grow(1) * (acc_s[0:D] / acc_s[D:D + 1])
         + grow(2) * (acc_w[0:D] / acc_w[D:D + 1]))
    halves = []
    for h in range(G * R // 2):
        pair = jnp.concatenate([o[:, (2 * h) * QT:(2 * h + 1) * QT],
                                o[:, (2 * h + 1) * QT:(2 * h + 2) * QT]], axis=0)
        halves.append(pair.T)
    o_ref[...] = jnp.concatenate(halves, axis=1)


def _nsa_attn(qt, kcmp, vct, ks3, vst, vsd, kw3, vwt, gt, ov_t, B, T):
    G, R = NSA_KV_HEADS, NSA_GROUP
    nq = T // Q_TILE
    nk = T // KEY_TILE
    nch = kcmp.shape[1]
    n_sel = ov_t.shape[0]
    NQ = G * R * Q_TILE
    assert (T // SEL_TILE) % 2 == 0 and n_sel <= LANES and Q_TILE == KEY_TILE
    assert nq % CMP_VARIANTS == 0 and n_sel % (8 * CMP_VARIANTS) == 0 and nch % (8 * CMP_VARIANTS) == 0
    return pl.pallas_call(
        _nsa_attn_kernel,
        grid=(B, nq),
        in_specs=[pl.BlockSpec((1, 1, NSA_WIDTH, Q_TILE), lambda b, q: (b, q, 0, 0)),
                  pl.BlockSpec((1, nch, LANES), lambda b, q: (b, 0, 0)),
                  pl.BlockSpec((1, G * HEAD_DIM, nch), lambda b, q: (b, 0, 0)),
                  pl.BlockSpec((1, T, 2 * LANES), lambda b, q: (b, 0, 0)),
                  pl.BlockSpec((1, T // SEL_TILE, G * V_ROWS, SEL_TILE), lambda b, q: (b, 0, 0, 0)),
                  pl.BlockSpec((1, nk, G * V_ROWS, KEY_TILE), lambda b, q: (b, 0, 0, 0)),
                  pl.BlockSpec((1, T, LANES), lambda b, q: (b, 0, 0)),
                  pl.BlockSpec((1, nk, G * V_ROWS, KEY_TILE), lambda b, q: (b, 0, 0, 0)),
                  pl.BlockSpec((1, G, 3, R, Q_TILE), lambda b, q: (b, 0, 0, 0, q)),
                  pl.BlockSpec((n_sel, nch), lambda b, q: (0, 0))],
        out_specs=pl.BlockSpec((Q_TILE, NSA_WIDTH), lambda b, q: (b * nq + q, 0)),
        out_shape=jax.ShapeDtypeStruct((B * T, NSA_WIDTH), F32),
        scratch_shapes=[pltpu.VMEM((2 * LANES, NQ), BF16), pltpu.VMEM((HEAD_DIM, NQ), F32),
                        pltpu.VMEM((n_sel, G * Q_TILE), F32),
                        *[pltpu.VMEM((SEL_TILE, NQ), F32)] * 4,
                        *[pltpu.VMEM((SEL_TILE, NQ), BF16)] * 2],
        compiler_params=_params("arbitrary", "arbitrary"),
        name="nsa_attn",
    )(qt, kcmp, vct, ks3, vst, vsd, kw3, vwt, gt, ov_t)


def _first_index_of(vals, target):
    idx = jnp.full_like(target, float(len(vals) - 1))
    for i in range(len(vals) - 2, -1, -1):
        idx = jnp.where(vals[i] == target, float(i), idx)
    return idx


def _pick(vals, idx):
    out = vals[-1]
    for i in range(len(vals) - 2, -1, -1):
        out = jnp.where(idx == float(i), vals[i], out)
    return out


def _route_rows(score, bias):
    E, G, P = N_EXPERTS, N_GROUPS, EXPERTS_PER_GROUP
    sel = score + bias
    s = [sel[e:e + 1, :] for e in range(E)]
    raw = [score[e:e + 1, :] for e in range(E)]
    grp = []
    for gi in range(G):
        a = s[gi * P:(gi + 1) * P]
        best = None
        for i in range(P):
            for j in range(i + 1, P):
                pair = a[i] + a[j]
                best = pair if best is None else jnp.maximum(best, pair)
        grp.append(best)
    gmax = functools.reduce(jnp.maximum, grp)
    g_star = _first_index_of(grp, gmax)
    v = [_pick([s[gi * P + i] for gi in range(G)], g_star) for i in range(P)]
    w = [_pick([raw[gi * P + i] for gi in range(G)], g_star) for i in range(P)]
    i1 = _first_index_of(v, functools.reduce(jnp.maximum, v))
    v2 = [jnp.where(i1 == float(i), -jnp.inf, v[i]) for i in range(P)]
    i2 = _first_index_of(v2, functools.reduce(jnp.maximum, v2))
    w1, w2 = _pick(w, i1), _pick(w, i2)
    tot = w1 + w2
    zero = jnp.zeros_like(tot)
    e1, e2 = g_star * P + i1, g_star * P + i2
    n = score.shape[1]
    eidx = lax.broadcasted_iota(jnp.int32, (E, n), 0).astype(F32)
    oh1, oh2 = jnp.where(eidx == e1, 1.0, 0.0), jnp.where(eidx == e2, 1.0, 0.0)
    earlier = jnp.where(lax.broadcasted_iota(jnp.int32, (n, n), 0) < lax.broadcasted_iota(jnp.int32, (n, n), 1),
                        1.0, 0.0).astype(BF16)
    cnt = _dot(jnp.concatenate([oh1, oh2], axis=0).astype(BF16), earlier)
    rank1 = jnp.sum(oh1 * cnt[0:E], axis=0, keepdims=True)
    rank2 = jnp.sum(oh2 * cnt[E:2 * E], axis=0, keepdims=True)
    lane = lax.broadcasted_iota(jnp.int32, (E, LANES), 1)
    totals = jnp.where(lane == 0, jnp.sum(oh1, axis=1, keepdims=True),
                       jnp.where(lane == 1, jnp.sum(oh2, axis=1, keepdims=True), 0.0))
    return jnp.concatenate([e1, e2, w1 / tot, w2 / tot, rank1, rank2, zero, zero], axis=0), totals


def _merge_kernel(ys_ref, g_ref, bonus_ref, gng_ref, gnb_ref, bd_ref, yb_ref, pm_ref, x_ref, mod_ref,
                  ng_ref, wa_ref, wb_ref, wo_ref, rw_ref, rb_ref, o_x, o_h, o_route, o_tot):
    m = mod_ref[pl.program_id(0)]
    bd = bd_ref[...]
    y = ys_ref[...]
    mean = _dot_split_lhs(y, bd)
    yc = y - mean
    var = _dot_split_lhs(yc * yc, bd)
    ya = (yc * lax.rsqrt(var + RWKV_GN_EPS) * gng_ref[...] + gnb_ref[...] + bonus_ref[...]) * g_ref[...]
    pm = pm_ref[...]
    D = x_ref.shape[1]
    mix = (_sigmoid(pm[:, 0:D]) * _dot(ya.astype(BF16), wa_ref[...])
           + _sigmoid(pm[:, D:2 * D]) * _dot(yb_ref[...].astype(BF16), wb_ref[...]))
    x = x_ref[...] + m[2:3] * _dot(mix.astype(BF16), wo_ref[...])
    o_x[...] = x
    ms = jnp.mean(x * x, axis=-1, keepdims=True)
    h = x * lax.rsqrt(ms + NORM_EPS) * ng_ref[...]
    h = h * (1.0 + m[4:5]) + m[3:4]
    o_h[...] = h
    score = _sigmoid(_dot_3pass(h, rw_ref).T[0:N_EXPERTS, :])
    o_route[...], o_tot[...] = _route_rows(score, rb_ref[...])


def _merge(ys, g, bonus, gn_g, gn_b, yb, pm, x2, mod, ng, wa, wb, wo, router_w, router_b, B, T, tm=256):
    N, D = x2.shape
    W = RWKV_WIDTH
    nt = T // tm
    row = lambda b, t: (b * nt + t, 0)
    full = lambda shape: pl.BlockSpec(shape, lambda b, t: (0,) * len(shape))
    return pl.pallas_call(
        _merge_kernel,
        grid=(B, nt),
        in_specs=[pl.BlockSpec((tm, W), row), pl.BlockSpec((tm, W), row), pl.BlockSpec((tm, W), row),
                  full((1, W)), full((1, W)), full((W, W)),
                  pl.BlockSpec((tm, NSA_WIDTH), row), pl.BlockSpec((tm, 2 * D), row),
                  pl.BlockSpec((tm, D), row), full((B, 6, D)), full((1, D)),
                  full((W, D)), full((NSA_WIDTH, D)), full((D, D)), full((2, D, LANES)),
                  full((N_EXPERTS, 1))],
        out_specs=[pl.BlockSpec((tm, D), row), pl.BlockSpec((tm, D), row),
                   pl.BlockSpec((8, tm), lambda b, t: (0, b * nt + t)),
                   pl.BlockSpec((N_EXPERTS, LANES), lambda b, t: (b * nt + t, 0))],
        out_shape=[jax.ShapeDtypeStruct((N, D), F32), jax.ShapeDtypeStruct((N, D), F32),
                   jax.ShapeDtypeStruct((8, N), F32), jax.ShapeDtypeStruct((N // tm * N_EXPERTS, LANES), F32)],
        compiler_params=_params("arbitrary", "arbitrary"),
        name="merge_out",
    )(ys, g, bonus, gn_g.reshape(1, W), gn_b.reshape(1, W), _head_block_diag(W, 1.0 / HEAD_DIM),
      yb, pm, x2, mod, ng, wa, wb, wo,
      _hi_lo(jnp.zeros((D, LANES), F32).at[:, :N_EXPERTS].set(router_w)), router_b.reshape(N_EXPERTS, 1))


def _route(route, totals, N):
    wts = route[TOP_K:2 * TOP_K].T
    NK = N * TOP_K
    E = N_EXPERTS
    n_tiles = totals.shape[0] // E
    expert = route[0:TOP_K].astype(jnp.int32)
    rank = route[2 * TOP_K:3 * TOP_K].astype(jnp.int32)
    per = totals.reshape(n_tiles, E, LANES)[:, :, 0:TOP_K].astype(jnp.int32).transpose(0, 2, 1)
    per = per.reshape(n_tiles * TOP_K, E)
    csum = jnp.cumsum(per, axis=0)
    counts = csum[-1]
    padded = (counts + MOE_BLOCK - 1) // MOE_BLOCK * MOE_BLOCK
    pad_end = jnp.cumsum(padded)
    pad_start = pad_end - padded
    first = (pad_start[None, :] + csum - per).reshape(n_tiles, TOP_K, E).transpose(1, 0, 2)
    first = jnp.repeat(first, N // n_tiles, axis=1)
    mine = expert[:, :, None] == jnp.arange(E, dtype=jnp.int32)[None, None, :]
    dest = (jnp.sum(jnp.where(mine, first, 0), axis=-1) + rank).reshape(-1)
    n_blk = -(-NK // MOE_BLOCK) + N_EXPERTS
    blk_start = jnp.arange(n_blk, dtype=jnp.int32) * MOE_BLOCK
    blk_expert = jnp.sum((pad_end[None, :] <= blk_start[:, None]).astype(jnp.int32), axis=1)
    blk_expert = jnp.clip(blk_expert, 0, N_EXPERTS - 1)
    blk_valid = jnp.clip((pad_start + counts)[blk_expert] - blk_start, 0, MOE_BLOCK).astype(jnp.int32)
    dest = jnp.pad(dest.astype(jnp.int32).reshape(NK // SC_WINDOW, SC_WINDOW), ((0, 0), (0, LANES - SC_WINDOW)))
    return wts, dest, blk_expert, blk_valid, n_blk


SC_WINDOW = 32


def _sc_mesh():
    return plsc.VectorSubcoreMesh(core_axis_name="c", subcore_axis_name="s")


def _sc_dispatch(h, dest, n_slots):
    N, D = h.shape
    W = SC_WINDOW
    nw = N // W

    @pl.kernel(out_type=jax.ShapeDtypeStruct((n_slots, D), h.dtype), mesh=_sc_mesh(), scratch_types=[])
    def dispatch(h_hbm, i_hbm, o_hbm):
        def body(x_vmem, i_vmem):
            pltpu.sync_copy(x_vmem, o_hbm.at[i_vmem.at[0, pl.ds(0, W)]])

        pltpu.emit_pipeline(
            body, grid=(TOP_K, nw),
            in_specs=[pl.BlockSpec((W, D), lambda k, i: (i, 0)),
                      pl.BlockSpec((1, LANES), lambda k, i: (k * nw + i, 0))],
            out_specs=[], core_axis_name=("c", "s"),
            dimension_semantics=(pltpu.PARALLEL, pltpu.PARALLEL))(h_hbm, i_hbm)

    return dispatch(h, dest)


def _sc_collect(ys, dest):
    W = SC_WINDOW
    NK = dest.shape[0] * W
    D = ys.shape[1]
    half = NK // TOP_K // W

    @pl.kernel(out_type=jax.ShapeDtypeStruct((NK, D), ys.dtype), mesh=_sc_mesh(), scratch_types=[])
    def collect(y_hbm, i_hbm, o_hbm):
        def body(i_vmem, o_vmem):
            pltpu.sync_copy(y_hbm.at[i_vmem.at[0, pl.ds(0, W)]], o_vmem)

        pltpu.emit_pipeline(
            body, grid=(TOP_K, half),
            in_specs=[pl.BlockSpec((1, LANES), lambda k, i: (k * half + i, 0))],
            out_specs=[pl.BlockSpec((W, D), lambda k, i: (k * half + i, 0))],
            core_axis_name=("c", "s"),
            dimension_semantics=(pltpu.PARALLEL, pltpu.PARALLEL))(i_hbm, o_hbm)

    return collect(ys, dest)


def _moe_dense_kernel(be_ref, nv_ref, x_ref, wg_ref, wu_ref, wd_ref, o_ref, wg_b, wu_b, wd_b):
    i = pl.program_id(0)
    nv = nv_ref[i]

    @pl.when((i == 0) | (be_ref[i] != be_ref[jnp.maximum(i - 1, 0)]))
    def _():
        wg_b[...] = wg_ref[0, 0].astype(BF16)
        wu_b[...] = wu_ref[0, 0].astype(BF16)
        wd_b[...] = wd_ref[0, 0].astype(BF16)

    @pl.when(nv > 0)
    def _():
        x = x_ref[...].astype(BF16)
        gate = _dot(x, wg_b[...])
        up = _dot(x, wu_b[...])
        o_ref[...] = _dot((gate * _sigmoid(gate) * up).astype(BF16), wd_b[...])

    @pl.when(nv == 0)
    def _():
        o_ref[...] = jnp.zeros_like(o_ref)


def _moe_dense(xs, blk_expert, blk_valid, n_blk, layer, wg, wu, wd):
    P, D = xs.shape
    DE = wg.shape[3]
    wmap = lambda i, be, nv: (layer, be[i], 0, 0)
    grid_spec = pltpu.PrefetchScalarGridSpec(
        num_scalar_prefetch=2,
        grid=(n_blk,),
        in_specs=[pl.BlockSpec((MOE_BLOCK, D), lambda i, be, nv: (i, 0)), pl.BlockSpec((1, 1, D, DE), wmap),
                  pl.BlockSpec((1, 1, D, DE), wmap), pl.BlockSpec((1, 1, DE, D), wmap)],
        out_specs=pl.BlockSpec((MOE_BLOCK, D), lambda i, be, nv: (i, 0)),
        scratch_shapes=[pltpu.VMEM((D, DE), BF16), pltpu.VMEM((D, DE), BF16), pltpu.VMEM((DE, D), BF16)],
    )
    return pl.pallas_call(
        _moe_dense_kernel,
        grid_spec=grid_spec,
        out_shape=jax.ShapeDtypeStruct((P, D), F32),
        compiler_params=_params("arbitrary"),
        name="moe_experts",
    )(blk_expert, blk_valid, xs, wg, wu, wd)


def _final_kernel(x_ref, y0_ref, y1_ref, w_ref, mod_ref, o_ref):
    m = mod_ref[pl.program_id(0)]
    w = w_ref[...]
    o_ref[...] = x_ref[...] + m[5:6] * (w[:, 0:1] * y0_ref[...] + w[:, 1:2] * y1_ref[...])


def _final(x2, ybuf, wts, mod, B, T, tm=512):
    N, D = x2.shape
    nt = T // tm
    row = lambda b, t: (b * nt + t, 0)
    return pl.pallas_call(
        _final_kernel,
        grid=(B, nt),
        in_specs=[pl.BlockSpec((tm, D), row), pl.BlockSpec((tm, D), row),
                  pl.BlockSpec((tm, D), lambda b, t: (N // tm + b * nt + t, 0)),
                  pl.BlockSpec((tm, TOP_K), row), pl.BlockSpec((B, 6, D), lambda b, t: (0, 0, 0))],
        out_specs=pl.BlockSpec((tm, D), row),
        out_shape=jax.ShapeDtypeStruct((N, D), F32),
        compiler_params=_params("arbitrary", "arbitrary"),
        name="moe_combine",
    )(x2, ybuf, ybuf, wts, mod)


def _overlap_t(n_sel, n_cmp_pad):
    ci = jnp.arange(n_cmp_pad)[None, :] * CMP_STRIDE
    sj = jnp.arange(n_sel)[:, None] * SEL_BLOCK
    ov = (ci <= sj + SEL_BLOCK - 1) & (ci + CMP_BLOCK - 1 >= sj) & (jnp.arange(n_cmp_pad)[None, :] < n_cmp_pad - 1)
    return ov.astype(BF16)


def kernel(x, c, w_ada, b_ada, norm_g, w_in, b_in, rwkv_mu, rwkv_w0, rwkv_w2, rwkv_a0, rwkv_a2, rwkv_g2,
           rwkv_k_k, rwkv_k_a, rwkv_r_k, rwkv_gn_g, rwkv_gn_b, qk_norm_g, cmp_pos, cmp_w1, cmp_w2,
           w_up_rwkv, w_up_nsa, w_out, router_w, router_b, exp_w_gate, exp_w_up, exp_w_down):
    B, T, D = x.shape
    L = w_ada.shape[0]
    N = B * T
    mods = _ada(c, w_ada, b_ada)
    tables = _rope_tables(jnp.arange(T, dtype=jnp.int32))
    nch = T // CMP_STRIDE
    tables_cmp = _rope_tables(jnp.arange(nch, dtype=jnp.int32) * CMP_STRIDE + CMP_BLOCK - 1)
    ov_t = _overlap_t(T // SEL_BLOCK, nch)
    n_gate = NSA_GATE_COLS
    x2 = x.reshape(N, D)
    for l in range(L):
        g0 = _SEG_KV[1] + n_gate
        w_pad = jnp.concatenate([w_in[l][:, :g0], jnp.zeros((D, GATE_PAD - n_gate), F32), w_in[l][:, g0:]],
                                axis=1).astype(BF16)
        b_pad = jnp.concatenate([b_in[l][:g0], jnp.zeros((GATE_PAD - n_gate,), F32), b_in[l][g0:]]).reshape(1, -1)
        p_rw, p_q, p_kv, p_gate, p_merge = _inproj(x2, mods[l], norm_g[l, 0].reshape(1, D), w_pad, b_pad, B, T)
        r, k, v, al, bb, ld, g, bonus = _rwkv_pre(p_rw, rwkv_mu[l], rwkv_w0[l], rwkv_w2[l], rwkv_a0[l],
                                                  rwkv_a2[l], rwkv_g2[l], rwkv_k_k[l], rwkv_k_a[l],
                                                  rwkv_r_k[l], B, T)
        ys = _rwkv_scan(r, k, v, al, bb, ld, B, T)
        qt, ks, kw, vst, vsd, vwt = _nsa_prep(p_q, p_kv, tables, qk_norm_g[l], B, T)
        kv3 = p_kv.reshape(B, T, KV_COLS)
        kcmp = _nsa_cmp(kv3, 0, cmp_pos[l], cmp_w1[l], cmp_w2[l], qk_norm_g[l, 1], tables_cmp)
        vct = _nsa_cmp(kv3, 1, cmp_pos[l], cmp_w1[l], cmp_w2[l], None, None)
        gt = p_gate[:, :n_gate].reshape(B, T, NSA_KV_HEADS, NSA_GROUP, 3).transpose(0, 2, 4, 3, 1)
        yb = _nsa_attn(qt, kcmp, vct, ks.reshape(B, T, 2 * LANES), vst, vsd, kw.reshape(B, T, LANES), vwt, gt, ov_t,
                       B, T)
        x2, h2, route, totals = _merge(ys, g, bonus, rwkv_gn_g[l], rwkv_gn_b[l], yb, p_merge, x2, mods[l],
                               norm_g[l, 1].reshape(1, D), w_up_rwkv[l].astype(BF16),
                               w_up_nsa[l].astype(BF16), w_out[l].astype(BF16), router_w, router_b, B, T)
        wts, dest, blk_expert, blk_valid, n_blk = _route(route, totals, N)
        xs = _sc_dispatch(h2, dest, n_blk * MOE_BLOCK)
        ys = _moe_dense(xs, blk_expert, blk_valid, n_blk, l, exp_w_gate, exp_w_up, exp_w_down)
        ybuf = _sc_collect(ys, dest)
        x2 = _final(x2, ybuf, wts, mods[l], B, T)
    return x2.reshape(B, T, D)
```

```python
import functools
import math

import jax
import jax.numpy as jnp
from jax import lax
from jax.experimental import pallas as pl
from jax.experimental.pallas import tpu as pltpu
from jax.experimental.pallas import tpu_sc as plsc

F32 = jnp.float32
BF16 = jnp.bfloat16
HI = lax.Precision.HIGHEST

D_MODEL = 1024
RWKV_HEADS = 8
HEAD_DIM = 64
RWKV_WIDTH = RWKV_HEADS * HEAD_DIM
DECAY_LORA = 64
ICLR_LORA = 64
GATE_LORA = 128
RWKV_GN_EPS = 64e-5
RWKV_COLS = 3 * RWKV_WIDTH + DECAY_LORA + ICLR_LORA + GATE_LORA

NSA_Q_HEADS = 8
NSA_KV_HEADS = 2
NSA_GROUP = NSA_Q_HEADS // NSA_KV_HEADS
NSA_WIDTH = NSA_Q_HEADS * HEAD_DIM
CMP_STRIDE = 16
CMP_BLOCK = 2 * CMP_STRIDE
CMP_HIDDEN = 256
SEL_BLOCK = 64
SEL_SHIFT = 6
SEL_TOPK = 16
WINDOW = 512
FORCE_SCORE = 1e4
NEG_INF = -1e30
ROPE_THETA = 500000.0
ROPE_DIM = HEAD_DIM // 4
KV_COLS = 6 * NSA_KV_HEADS * HEAD_DIM
NSA_GATE_COLS = 3 * NSA_Q_HEADS
GATE_PAD = 128

N_EXPERTS = 16
N_GROUPS = 4
EXPERTS_PER_GROUP = N_EXPERTS // N_GROUPS
TOP_K = 2
D_EXPERT = 512
MOE_BLOCK = 256
NORM_EPS = 1e-6

LANES = 128
CHUNK = 64
KEY_TILE = 128
SEL_TILE = 512
CMP_VARIANTS = 4
V_ROWS = 80
Q_SCALE = HEAD_DIM ** -0.5 * math.log2(math.e)
Q_TILE = 128
F32_TINY = float(jnp.finfo(jnp.float32).tiny)

_SEG_RW = (0, RWKV_COLS)
_SEG_Q = (_SEG_RW[1], _SEG_RW[1] + NSA_WIDTH)
_SEG_KV = (_SEG_Q[1], _SEG_Q[1] + KV_COLS)
_SEG_GATE = (_SEG_KV[1], _SEG_KV[1] + GATE_PAD)
_SEG_MERGE = (_SEG_GATE[1], _SEG_GATE[1] + 2 * D_MODEL)
IN_COLS_PAD = _SEG_MERGE[1]

_VMEM_LIMIT = 56 * 1024 * 1024


def _dot(a, b, precision=None):
    return jnp.dot(a, b, preferred_element_type=F32, precision=precision)


def _dot_tb(a, b, precision=None):
    return lax.dot_general(a, b, (((1,), (1,)), ((), ())), preferred_element_type=F32,
                           precision=precision)


def _dot_ta(a, b, precision=None):
    return lax.dot_general(a, b, (((0,), (0,)), ((), ())), preferred_element_type=F32,
                           precision=precision)


def _split_bf16(x, terms):
    parts = []
    for _ in range(terms - 1):
        parts.append(x.astype(BF16))
        x = x - parts[-1].astype(F32)
    parts.append(x.astype(BF16))
    return parts


def _dot_split_lhs(x, w_bf, terms=2):
    return functools.reduce(jnp.add, [_dot(p, w_bf) for p in _split_bf16(x, terms)])


def _dot_split_rhs(w_bf, x, terms=2):
    return functools.reduce(jnp.add, [_dot(w_bf, p) for p in _split_bf16(x, terms)])


def _dot_3pass(x, w_hl_ref):
    x_hi, x_lo = _split_bf16(x, 2)
    w_hi = w_hl_ref[0]
    return _dot(x_hi, w_hi) + _dot(x_lo, w_hi) + _dot(x_hi, w_hl_ref[1])


def _hi_lo(w):
    hi = w.astype(BF16)
    return jnp.stack([hi, (w - hi.astype(F32)).astype(BF16)])


def _params(*sem):
    return pltpu.CompilerParams(dimension_semantics=sem, vmem_limit_bytes=_VMEM_LIMIT)


def _sigmoid(x):
    return 1.0 / (1.0 + jnp.exp(-x))


def _ada_kernel(c_ref, w_ref, b_ref, o_ref):
    c = c_ref[...]
    s = c * _sigmoid(c)
    o_ref[0] = _dot(s, w_ref[0], HI) + b_ref[0]


def _ada(c, w_ada, b_ada):
    L, D, D6 = w_ada.shape
    B = c.shape[0]
    rows = 8
    cp = jnp.zeros((rows, D), F32).at[:B].set(c)
    tn = 1536
    out = pl.pallas_call(
        _ada_kernel,
        grid=(L, D6 // tn),
        in_specs=[pl.BlockSpec((rows, D), lambda l, j: (0, 0)),
                  pl.BlockSpec((1, D, tn), lambda l, j: (l, 0, j)),
                  pl.BlockSpec((1, 1, tn), lambda l, j: (l, 0, j))],
        out_specs=pl.BlockSpec((1, rows, tn), lambda l, j: (l, 0, j)),
        out_shape=jax.ShapeDtypeStruct((L, rows, D6), F32),
        compiler_params=_params("arbitrary", "arbitrary"),
        name="ada_mod",
    )(cp, w_ada, b_ada.reshape(L, 1, D6))
    return out[:, :B].reshape(L, B, 6, D)


def _inproj_kernel(x_ref, mod_ref, g_ref, w_ref, b_ref, o_rw, o_q, o_kv, o_gate, o_merge):
    m = mod_ref[pl.program_id(0)]
    x = x_ref[...]
    ms = jnp.mean(x * x, axis=-1, keepdims=True)
    h = x * lax.rsqrt(ms + NORM_EPS) * g_ref[...]
    h = h * (1.0 + m[1:2]) + m[0:1]
    hb = h.astype(BF16)
    for o, (a, e) in ((o_rw, _SEG_RW), (o_q, _SEG_Q), (o_kv, _SEG_KV), (o_gate, _SEG_GATE),
                      (o_merge, _SEG_MERGE)):
        o[...] = _dot(hb, w_ref[:, a:e]) + b_ref[:, a:e]


def _inproj(x2, mod, g, w_pad, b_pad, B, T, tm=512):
    N, D = x2.shape
    nt = T // tm
    row = lambda b, t: (b * nt + t, 0)
    widths = [e - a for a, e in (_SEG_RW, _SEG_Q, _SEG_KV, _SEG_GATE, _SEG_MERGE)]
    return pl.pallas_call(
        _inproj_kernel,
        grid=(B, nt),
        in_specs=[pl.BlockSpec((tm, D), row),
                  pl.BlockSpec((B, 6, D), lambda b, t: (0, 0, 0)),
                  pl.BlockSpec((1, D), lambda b, t: (0, 0)),
                  pl.BlockSpec((D, IN_COLS_PAD), lambda b, t: (0, 0)),
                  pl.BlockSpec((1, IN_COLS_PAD), lambda b, t: (0, 0))],
        out_specs=[pl.BlockSpec((tm, w), row) for w in widths],
        out_shape=[jax.ShapeDtypeStruct((N, w), F32) for w in widths],
        compiler_params=_params("arbitrary", "arbitrary"),
        name="in_proj",
    )(x2, mod, g, w_pad, b_pad)


def _rwkv_pre_kernel(p_ref, mu_ref, w0_ref, w2_ref, a0_ref, a2_ref, g2_ref, kk_ref, ka_ref, rk_ref,
                     bd_ref, o_r, o_k, o_v, o_al, o_b, o_ld, o_g, o_bonus, carry_ref):
    W = RWKV_WIDTH

    @pl.when(pl.program_id(1) == 0)
    def _():
        carry_ref[...] = jnp.zeros_like(carry_ref)

    p = p_ref[...]
    ts = p.shape[0]
    rows = lax.broadcasted_iota(jnp.int32, p.shape, 0)
    shifted = jnp.where(rows == 0, carry_ref[0:1, :], pltpu.roll(p, 1, 0))
    carry_ref[0:1, :] = p[ts - 1:ts, :]
    pm = p + (shifted - p) * mu_ref[...]
    r = pm[:, 0:W]
    k = pm[:, W:2 * W]
    v = pm[:, 2 * W:3 * W]
    wa = pm[:, 3 * W:3 * W + DECAY_LORA + ICLR_LORA]
    gl = pm[:, 3 * W + DECAY_LORA + ICLR_LORA:]
    xw = w0_ref[...] + _dot_3pass(jnp.tanh(wa), w2_ref)
    ld = -math.exp(-0.5) * _sigmoid(xw)
    a = _sigmoid(a0_ref[...] + _dot_3pass(wa, a2_ref))
    g = _dot_3pass(_sigmoid(gl), g2_ref)
    bd = bd_ref[...]
    kk = k * kk_ref[...]
    nrm = jnp.sqrt(_dot_split_lhs(kk * kk, bd))
    kk = kk / jnp.maximum(nrm, 1e-12)
    k2 = k * (1.0 + (a - 1.0) * ka_ref[...])
    bonus = _dot_split_lhs(r * k2 * rk_ref[...], bd) * v
    o_r[...] = r
    o_k[...] = k2
    o_v[...] = v
    o_al[...] = kk
    o_b[...] = -kk * a
    o_ld[...] = ld
    o_g[...] = g
    o_bonus[...] = bonus


def _head_block_diag(width, scale=1.0):
    i = jnp.arange(width) // HEAD_DIM
    return ((i[:, None] == i[None, :]).astype(F32) * scale).astype(BF16)


def _rwkv_pre(p_rw, mu, w0, w2, a0, a2, g2, k_k, k_a, r_k, B, T, ts=512):
    N = p_rw.shape[0]
    W = RWKV_WIDTH
    nt = T // ts
    row = lambda b, t: (b * nt + t, 0)
    zl = jnp.zeros((DECAY_LORA, W), F32)
    w2p = jnp.concatenate([w2, zl], axis=0)
    a2p = jnp.concatenate([zl, a2], axis=0)
    full = lambda shape: pl.BlockSpec(shape, lambda b, t: (0,) * len(shape))
    vec = lambda z: z.reshape(1, -1)
    return pl.pallas_call(
        _rwkv_pre_kernel,
        grid=(B, nt),
        in_specs=[pl.BlockSpec((ts, RWKV_COLS), row), full((1, RWKV_COLS)), full((1, W)),
                  full((2, 2 * DECAY_LORA, W)), full((1, W)), full((2, 2 * DECAY_LORA, W)),
                  full((2, GATE_LORA, W)), full((1, W)), full((1, W)), full((1, W)), full((W, W))],
        out_specs=[pl.BlockSpec((ts, W), row)] * 8,
        out_shape=[jax.ShapeDtypeStruct((N, W), F32)] * 8,
        scratch_shapes=[pltpu.VMEM((8, RWKV_COLS), F32)],
        compiler_params=_params("arbitrary", "arbitrary"),
        name="rwkv_pre",
    )(p_rw, vec(mu), vec(w0), _hi_lo(w2p), vec(a0), _hi_lo(a2p), _hi_lo(g2), vec(k_k), vec(k_a), vec(r_k),
      _head_block_diag(W))


def _bf(x):
    return x.astype(BF16)


def _scan_local(chunks, eye, strict, incl, m0, m1, between_stages=lambda: None):
    C = CHUNK
    n = range(len(chunks))
    st = lambda z: jnp.concatenate([z * m0, z * m1], axis=0)
    zero = jnp.zeros((2 * C, 2 * C), F32)
    at_b, rt_s, vs, vs_b, lhs_a, rhs_a, bk_t, dcol = [], [], [], [], [], [], [], []
    for r, k, v, al, bb, ld, cum in chunks:
        tot = cum[C - 1:C, :]
        dinv = jnp.exp(-cum)
        dend = jnp.exp(tot - cum)
        at_b.append(_bf(st(al * jnp.exp(cum - ld))))
        rt_s.append(st(r * jnp.exp(cum)))
        vs.append(st(v))
        vs_b.append(_bf(vs[-1]))
        lhs_a.append(jnp.concatenate([at_b[-1], _bf(rt_s[-1])], axis=0))
        rhs_a.append(_bf(jnp.concatenate([st(bb * dinv), st(k * dinv)], axis=0)))
        bk_t.append(_bf(jnp.concatenate([st(bb * dend).T, st(k * dend).T], axis=1)))
        dcol.append(jnp.sum(eye * jnp.exp(tot), axis=1, keepdims=True))
    between_stages()
    A = [_dot_tb(lhs_a[i], rhs_a[i]) for i in n]
    between_stages()
    a_ab = [jnp.where(strict, A[i][0:2 * C, 0:2 * C], zero) for i in n]
    a_ak = [_bf(jnp.where(strict, A[i][0:2 * C, 2 * C:4 * C], zero)) for i in n]
    a_r = [_bf(jnp.concatenate([jnp.where(incl, A[i][2 * C:4 * C, 0:2 * C], zero),
                                jnp.where(incl, A[i][2 * C:4 * C, 2 * C:4 * C], zero)], axis=1)) for i in n]
    akv = [_bf(_dot(a_ak[i], vs_b[i])) for i in n]
    between_stages()
    pw = a_ab
    tinv = [eye + pw[i] for i in n]
    for _ in range(5):
        pw_b = [_bf(pw[i]) for i in n]
        pw = [_dot(pw_b[i], pw_b[i]) for i in n]
        tinv = [tinv[i] + _dot(_bf(pw[i]), _bf(tinv[i])) for i in n]
        between_stages()
    X = [_dot(_bf(tinv[i]), jnp.concatenate([at_b[i], akv[i]], axis=1)) for i in n]
    between_stages()
    w_b = [_bf(X[i][:, 0:LANES]) for i in n]
    uv0 = [jnp.concatenate([_bf(X[i][:, LANES:2 * LANES]), vs_b[i]], axis=0) for i in n]
    m_h = [_bf(_dot(bk_t[i][:, 0:2 * C], w_b[i])) for i in n]
    g_h = [_dot(bk_t[i], uv0[i]) for i in n]
    between_stages()
    q_h = [_bf(rt_s[i] + _dot(a_r[i][:, 0:2 * C], w_b[i])) for i in n]
    y0 = [_dot(a_r[i], uv0[i]) for i in n]
    return [(m_h[i], g_h[i], dcol[i], q_h[i], y0[i]) for i in n]


def _rwkv_scan_kernel(r_ref, k_ref, v_ref, al_ref, b_ref, ld_ref, o_ref, h_ref, *local_refs):
    C = CHUNK
    tc = r_ref.shape[0]
    nc = tc // C

    @pl.when(pl.program_id(2) == 0)
    def _():
        h_ref[...] = jnp.zeros_like(h_ref)
        for ref in local_refs:
            ref[...] = jnp.zeros_like(ref)

    seq = {"H": h_ref[...], "c": 0}

    def one_step():
        c = seq["c"]
        if c < nc:
            m_h, g_h, dcol, q_h, y0 = (ref[c] for ref in local_refs)
            h_b = _bf(seq["H"])
            Y = _dot(q_h, h_b) + y0
            o_ref[c * C:(c + 1) * C, :] = Y[0:C] + Y[C:2 * C]
            seq["H"] = dcol * seq["H"] + _dot(m_h, h_b) + g_h
            seq["c"] = c + 1

    tri = jnp.where(lax.broadcasted_iota(jnp.int32, (C, C), 1) <= lax.broadcasted_iota(jnp.int32, (C, C), 0),
                    1.0, 0.0).astype(BF16)
    r2 = lax.broadcasted_iota(jnp.int32, (2 * C, 2 * C), 0)
    c2 = lax.broadcasted_iota(jnp.int32, (2 * C, 2 * C), 1)
    eye = (r2 == c2).astype(F32)
    strict = (c2 & (C - 1)) < (r2 & (C - 1))
    incl = (c2 & (C - 1)) <= (r2 & (C - 1))
    lane = lax.broadcasted_iota(jnp.int32, (C, LANES), 1)
    m0 = (lane < HEAD_DIM).astype(F32)
    m1 = 1.0 - m0
    cum = _dot_split_rhs(tri, jnp.concatenate([ld_ref[c * C:(c + 1) * C, :] for c in range(nc)], axis=1), 3)
    chunks = []
    for c in range(nc):
        sl = slice(c * C, (c + 1) * C)
        chunks.append((r_ref[sl, :], k_ref[sl, :], v_ref[sl, :], al_ref[sl, :], b_ref[sl, :], ld_ref[sl, :],
                       cum[:, c * LANES:(c + 1) * LANES]))
    local = _scan_local(chunks, eye, strict, incl, m0, m1, between_stages=one_step)
    while seq["c"] < nc:
        one_step()
    h_ref[...] = seq["H"]
    for c, parts in enumerate(local):
        for ref, part in zip(local_refs, parts):
            ref[c] = part


def _rwkv_scan(r, k, v, al, bb, ld, B, T, tc=512):
    N, W = r.shape
    nt = T // tc
    nc = tc // CHUNK
    in_spec = pl.BlockSpec((tc, LANES), lambda b, h, t: (b * nt + jnp.minimum(t, nt - 1), h))
    out_spec = pl.BlockSpec((tc, LANES), lambda b, h, t: (b * nt + jnp.maximum(t - 1, 0), h))
    sq = (nc, LANES, LANES)
    return pl.pallas_call(
        _rwkv_scan_kernel,
        grid=(B, W // LANES, nt + 1),
        in_specs=[in_spec] * 6,
        out_specs=out_spec,
        out_shape=jax.ShapeDtypeStruct((N, W), F32),
        scratch_shapes=[pltpu.VMEM((LANES, LANES), F32), pltpu.VMEM(sq, BF16), pltpu.VMEM(sq, F32),
                        pltpu.VMEM((nc, LANES, 1), F32), pltpu.VMEM(sq, BF16), pltpu.VMEM(sq, F32)],
        compiler_params=_params("arbitrary", "arbitrary", "arbitrary"),
        name="rwkv_scan",
    )(r, k, v, al, bb, ld)


def _rope_tables(pos):
    half = ROPE_DIM // 2
    inv = jnp.power(ROPE_THETA, -jnp.arange(half, dtype=F32) * 2.0 / ROPE_DIM)
    ang = pos.astype(F32)[:, None] * inv[None, :]
    cos, sin = jnp.cos(ang), jnp.sin(ang)
    n = pos.shape[0]
    rest = HEAD_DIM - ROPE_DIM
    c = jnp.concatenate([cos, cos, jnp.ones((n, rest), F32)], axis=1)
    s_dn = jnp.concatenate([-sin, jnp.zeros((n, half + rest), F32)], axis=1)
    s_up = jnp.concatenate([jnp.zeros((n, half), F32), sin, jnp.zeros((n, rest), F32)], axis=1)
    rep = LANES // HEAD_DIM
    return jnp.tile(c, (1, rep)), jnp.tile(s_dn, (1, rep)), jnp.tile(s_up, (1, rep))


def _norm_rope(x, bd, g, c, s_dn, s_up):
    width = x.shape[1]
    half = ROPE_DIM // 2
    rep = width // LANES
    tile = (lambda z: jnp.concatenate([z] * rep, axis=1)) if rep > 1 else (lambda z: z)
    ms = _dot_split_lhs(x * x, bd)
    xn = x * lax.rsqrt(ms + NORM_EPS) * g
    return (xn * tile(c) + pltpu.roll(xn, width - half, 1) * tile(s_dn)
            + pltpu.roll(xn, half, 1) * tile(s_up))


def _nsa_prep_kernel(q_ref, kv_ref, c_ref, sd_ref, su_ref, gq_ref, gs_ref, gw_ref, bdq_ref, bdk_ref,
                     o_qt, o_ks, o_kw, o_vst, o_vsd, o_vwt):
    c, sd, su = c_ref[...], sd_ref[...], su_ref[...]
    q = _norm_rope(q_ref[...], bdq_ref[...], gq_ref[...], c, sd, su) * Q_SCALE
    qt = q.T
    ts = q.shape[0]
    kv = kv_ref[...]
    bdk = bdk_ref[...]
    pos = pl.program_id(1) * ts + lax.broadcasted_iota(jnp.int32, (ts, LANES), 0)
    blk_onehot = jnp.where((pos >> SEL_SHIFT) == lax.broadcasted_iota(jnp.int32, (ts, LANES), 1), 1.0, 0.0)
    ks = _norm_rope(kv[:, 2 * LANES:3 * LANES], bdk, gs_ref[...], c, sd, su)
    o_ks[...] = jnp.concatenate([ks, blk_onehot], axis=1).astype(BF16)
    o_kw[...] = _norm_rope(kv[:, 4 * LANES:5 * LANES], bdk, gw_ref[...], c, sd, su).astype(BF16)
    ones_rows = jnp.where(lax.broadcasted_iota(jnp.int32, (V_ROWS - HEAD_DIM, q.shape[0]), 0) == 0, 1.0, 0.0)

    def values_t(x):
        xt = x.T
        return jnp.concatenate([xt[0:HEAD_DIM], ones_rows, xt[HEAD_DIM:2 * HEAD_DIM], ones_rows], axis=0)

    vst = values_t(kv[:, 3 * LANES:4 * LANES])
    vwt = values_t(kv[:, 5 * LANES:6 * LANES])
    for j in range(q.shape[0] // KEY_TILE):
        sl = slice(j * KEY_TILE, (j + 1) * KEY_TILE)
        o_qt[0, j] = qt[:, sl].astype(BF16)
        o_vsd[0, j] = vst[:, sl].astype(BF16)
        o_vwt[0, j] = vwt[:, sl].astype(BF16)
    for j in range(q.shape[0] // SEL_TILE):
        o_vst[0, j] = vst[:, j * SEL_TILE:(j + 1) * SEL_TILE].astype(BF16)


def _nsa_prep(q, kv, tables, qk_g, B, T, ts=512):
    N = q.shape[0]
    nt = T // ts
    nk = ts // KEY_TILE
    ns = ts // SEL_TILE
    row = lambda b, t: (b * nt + t, 0)
    full = lambda shape: pl.BlockSpec(shape, lambda b, t: (0,) * len(shape))
    tab = pl.BlockSpec((ts, LANES), lambda b, t: (t, 0))
    gq = jnp.tile(qk_g[0], NSA_Q_HEADS).reshape(1, NSA_WIDTH)
    gs = jnp.tile(qk_g[2], NSA_KV_HEADS).reshape(1, LANES)
    gw = jnp.tile(qk_g[3], NSA_KV_HEADS).reshape(1, LANES)
    tiled = lambda rows: pl.BlockSpec((1, nk, rows, KEY_TILE), lambda b, t: (b, t, 0, 0))
    return pl.pallas_call(
        _nsa_prep_kernel,
        grid=(B, nt),
        in_specs=[pl.BlockSpec((ts, NSA_WIDTH), row), pl.BlockSpec((ts, KV_COLS), row), tab, tab, tab,
                  full((1, NSA_WIDTH)), full((1, LANES)), full((1, LANES)),
                  full((NSA_WIDTH, NSA_WIDTH)), full((LANES, LANES))],
        out_specs=[tiled(NSA_WIDTH), pl.BlockSpec((ts, 2 * LANES), row), pl.BlockSpec((ts, LANES), row),
                   pl.BlockSpec((1, ns, NSA_KV_HEADS * V_ROWS, SEL_TILE), lambda b, t: (b, t, 0, 0)),
                   tiled(NSA_KV_HEADS * V_ROWS), tiled(NSA_KV_HEADS * V_ROWS)],
        out_shape=[jax.ShapeDtypeStruct((B, T // KEY_TILE, NSA_WIDTH, KEY_TILE), BF16),
                   jax.ShapeDtypeStruct((N, 2 * LANES), BF16), jax.ShapeDtypeStruct((N, LANES), BF16),
                   jax.ShapeDtypeStruct((B, T // SEL_TILE, NSA_KV_HEADS * V_ROWS, SEL_TILE), BF16),
                   jax.ShapeDtypeStruct((B, T // KEY_TILE, NSA_KV_HEADS * V_ROWS, KEY_TILE), BF16),
                   jax.ShapeDtypeStruct((B, T // KEY_TILE, NSA_KV_HEADS * V_ROWS, KEY_TILE), BF16)],
        compiler_params=_params("arbitrary", "arbitrary"),
        name="nsa_prep",
    )(q, kv, *tables, gq, gs, gw, _head_block_diag(NSA_WIDTH, 1.0 / HEAD_DIM),
      _head_block_diag(LANES, 1.0 / HEAD_DIM))


def _gelu_tanh(x):
    return 0.5 * x * (1.0 + jnp.tanh(0.7978845608028654 * (x + 0.044715 * x * x * x)))


def _nsa_cmp_kernel(x_ref, pos_ref, w1_ref, w2_ref, *rest, is_key):
    if is_key:
        g_ref, c_ref, sd_ref, su_ref, bd_ref, o_ref, xs_ref = rest
    else:
        o_ref, xs_ref = rest
    nch = xs_ref.shape[0]
    S = CMP_STRIDE
    for j in range(S):
        xs_ref[:, j * LANES:(j + 1) * LANES] = x_ref[0, pl.ds(j, nch, stride=S), :]
    xs = xs_ref[...]
    first = _dot((xs + pos_ref[0:1, :]).astype(BF16), w1_ref[0])
    second = _dot((xs + pos_ref[1:2, :]).astype(BF16), w1_ref[1])
    hid = first + pltpu.roll(second, nch - 1, 0)
    out = _dot(_gelu_tanh(hid).astype(BF16), w2_ref[...])
    rows = lax.broadcasted_iota(jnp.int32, out.shape, 0)
    if is_key:
        out = _norm_rope(out, bd_ref[...], g_ref[...], c_ref[...], sd_ref[...], su_ref[...])
        o_ref[0] = jnp.where(rows < nch - 1, out, 0.0).astype(BF16)
    else:
        o_ref[0] = jnp.where(rows < nch - 1, out, 0.0).T.astype(BF16)


def _nsa_cmp(kv3, which, cmp_pos, cmp_w1, cmp_w2, g_k, tables_cmp):
    B, T, _ = kv3.shape
    S = CMP_STRIDE
    nch = T // S
    is_key = which == 0
    eye2 = jnp.eye(NSA_KV_HEADS, dtype=F32)
    w1 = cmp_w1[which].reshape(CMP_BLOCK, HEAD_DIM, CMP_HIDDEN)
    w1 = jnp.einsum('jdh,ge->jgdeh', w1, eye2).reshape(2, S * LANES, NSA_KV_HEADS * CMP_HIDDEN)
    w2 = jnp.einsum('hd,ge->ghed', cmp_w2[which], eye2).reshape(NSA_KV_HEADS * CMP_HIDDEN, LANES)
    pos = jnp.tile(cmp_pos[which].reshape(2, S, 1, HEAD_DIM), (1, 1, NSA_KV_HEADS, 1)).reshape(2, S * LANES)
    full = lambda shape: pl.BlockSpec(shape, lambda b: (0,) * len(shape))
    in_specs = [pl.BlockSpec((1, T, LANES), lambda b: (b, 0, which)), full(pos.shape), full(w1.shape),
                full(w2.shape)]
    args = [kv3, pos, w1.astype(BF16), w2.astype(BF16)]
    if is_key:
        in_specs += [full((1, LANES)), full((nch, LANES)), full((nch, LANES)), full((nch, LANES)),
                     full((LANES, LANES))]
        args += [jnp.tile(g_k, NSA_KV_HEADS).reshape(1, LANES), *tables_cmp,
                 _head_block_diag(LANES, 1.0 / HEAD_DIM)]
        out_spec = pl.BlockSpec((1, nch, LANES), lambda b: (b, 0, 0))
        out_shape = jax.ShapeDtypeStruct((B, nch, LANES), BF16)
    else:
        out_spec = pl.BlockSpec((1, LANES, nch), lambda b: (b, 0, 0))
        out_shape = jax.ShapeDtypeStruct((B, LANES, nch), BF16)
    return pl.pallas_call(
        functools.partial(_nsa_cmp_kernel, is_key=is_key),
        grid=(B,),
        in_specs=in_specs,
        out_specs=out_spec,
        out_shape=out_shape,
        scratch_shapes=[pltpu.VMEM((nch, S * LANES), F32)],
        compiler_params=_params("arbitrary"),
        name="nsa_cmp_k" if is_key else "nsa_cmp_v",
    )(*args)


def _nsa_attn_kernel(qt_ref, kc_ref, vct_ref, ks_ref, vst_ref, vsd_ref, kw_ref, vwt_ref, gt_ref, ov_ref, o_ref,
                     rhs_ref, oc_ref, keep_ref, s0_ref, s1_ref, s2_ref, s3_ref, p0_ref, p1_ref):
    qb = pl.program_id(1)
    G = NSA_KV_HEADS
    R = NSA_GROUP
    QT = Q_TILE
    KT = KEY_TILE
    CG = R * QT
    NQ = G * CG
    D = HEAD_DIM
    t0 = qb * QT
    n_cmp_pad = kc_ref.shape[1]
    n_sel = ov_ref.shape[0]
    cols = lambda g: slice(g * CG, (g + 1) * CG)

    q_cols = []
    for g in range(G):
        q_g = jnp.concatenate([qt_ref[0, 0, (g * R + r) * D:(g * R + r + 1) * D, :] for r in range(R)], axis=1)
        q_cols.append(jnp.concatenate([q_g if gg == g else jnp.zeros_like(q_g) for gg in range(G)], axis=0))
    qpad = jnp.concatenate(q_cols, axis=1)

    tq_row = t0 + (lax.broadcasted_iota(jnp.int32, (1, NQ), 1) & (QT - 1))
    spread = lambda z: jnp.concatenate([z[:, g * QT:(g + 1) * QT] for g in range(G) for _ in range(R)], axis=1)
    tile_all = lambda z: jnp.concatenate([z] * (G * R), axis=1)

    def values_dot(v_of_group, p):
        return jnp.concatenate([_dot(v_of_group(g), p[:, cols(g)]) for g in range(G)], axis=1)

    NV = CMP_VARIANTS
    nq = ks_ref.shape[1] // QT

    def compressed_and_select(n_c, n_b):
        sc = _dot(kc_ref[0, 0:n_c, :], qpad)
        n_i = lax.broadcasted_iota(jnp.int32, (n_c, 1), 0)
        cend = jnp.where(n_i < n_cmp_pad - 1, n_i * CMP_STRIDE + (CMP_BLOCK - 1), jnp.int32(2 ** 30))
        cvalid = cend <= tq_row
        sc = jnp.where(cvalid, sc, NEG_INF)
        mc = jnp.max(sc, axis=0, keepdims=True)
        ec = jnp.where(cvalid, jnp.exp2(sc - mc), 0.0)
        pc = ec / jnp.maximum(jnp.sum(ec, axis=0, keepdims=True), F32_TINY)
        pc_b = pc.astype(BF16)
        oc_ref[...] = values_dot(lambda g: vct_ref[0, g * D:(g + 1) * D, 0:n_c], pc_b)
        sums = []
        for g in range(G):
            acc = pc[:, g * CG:g * CG + QT]
            for r in range(1, R):
                acc = acc + pc[:, g * CG + r * QT:g * CG + (r + 1) * QT]
            sums.append(acc)
        imp = _dot_split_rhs(ov_ref[0:n_b, 0:n_c], jnp.concatenate(sums, axis=1))
        jb = lax.broadcasted_iota(jnp.int32, (n_b, G * QT), 0)
        jf = jb.astype(F32)
        tq_b = t0 + (lax.broadcasted_iota(jnp.int32, (n_b, G * QT), 1) & (QT - 1))
        cur = tq_b >> SEL_SHIFT
        forced = (jb == 0) | (jb == cur) | (jb == cur - 1)
        visible = jb * SEL_BLOCK <= tq_b
        score = jnp.where(visible, jnp.where(forced, FORCE_SCORE, imp), -1.0)
        sel = jnp.zeros((n_b, G * QT), F32)
        for _ in range(min(SEL_TOPK, n_b)):
            mx = jnp.max(score, axis=0, keepdims=True)
            jmin = jnp.min(jnp.where(score == mx, jf, 1e9), axis=0, keepdims=True)
            hit = jf == jmin
            sel = jnp.where(hit, 1.0, sel)
            score = jnp.where(hit, -3e38, score)
        keep_ref[0:n_b, :] = jnp.where(visible, sel, 0.0)
        if n_b < n_sel:
            keep_ref[n_b:n_sel, :] = jnp.zeros((n_sel - n_b, G * QT), F32)

    for v in range(NV):
        @pl.when((qb * NV) // nq == v)
        def _():
            compressed_and_select((v + 1) * n_cmp_pad // NV, (v + 1) * n_sel // NV)

    o_c = oc_ref[...]
    ji = lax.broadcasted_iota(jnp.int32, (n_sel, G * QT), 0)

    ST = SEL_TILE
    bias_all = (keep_ref[...] - 1.0) * (-NEG_INF)
    first_own = t0 // SEL_BLOCK
    vrows = lambda g: slice(g * V_ROWS, (g + 1) * V_ROWS)

    def with_bias_rows(bias):
        rows = spread(bias).astype(BF16)
        if n_sel < LANES:
            rows = jnp.concatenate([rows, jnp.zeros((LANES - n_sel, NQ), BF16)], axis=0)
        return jnp.concatenate([qpad, rows], axis=0)

    rhs_ref[...] = with_bias_rows(jnp.where(ji < first_own, bias_all, NEG_INF))
    n_tiles = (t0 + ST - 1) // ST
    last_tile = ks_ref.shape[1] // ST - 1
    p_bufs = (p0_ref, p1_ref)

    def sel_scores(kt, s_ref):
        k0 = pl.multiple_of(jnp.minimum(kt, last_tile) * ST, ST)
        s_ref[...] = _dot(ks_ref[0, pl.ds(k0, ST), :], rhs_ref[...])

    def sel_values(kt, slot, acc, alpha):
        kt = jnp.clip(kt, 0, last_tile)
        return acc * alpha + values_dot(lambda g: vst_ref[0, kt, vrows(g), :], p_bufs[slot][...])

    def sel_softmax(s_ref, slot, m):
        s = s_ref[...]
        m_new = jnp.maximum(m, jnp.max(s, axis=0, keepdims=True))
        p_bufs[slot][...] = jnp.exp2(s - m_new).astype(BF16)
        return m_new, jnp.exp2(m - m_new)

    def sel_pair(a, carry, s_now, s_next):
        m, acc, alpha0, alpha1 = carry
        acc = sel_values(a - 2, 0, acc, alpha0)
        acc = sel_values(a - 1, 1, acc, alpha1)
        sel_scores(a + 2, s_next[0])
        sel_scores(a + 3, s_next[1])
        m, alpha0 = sel_softmax(s_now[0], 0, m)
        m, alpha1 = sel_softmax(s_now[1], 1, m)
        return m, acc, alpha0, alpha1

    bufs_a, bufs_b = (s0_ref, s1_ref), (s2_ref, s3_ref)
    sel_scores(0, s0_ref)
    sel_scores(1, s1_ref)
    p0_ref[...] = jnp.zeros_like(p0_ref)
    p1_ref[...] = jnp.zeros_like(p1_ref)
    own = _dot(ks_ref[0, pl.ds(pl.multiple_of(t0, QT), QT), :], with_bias_rows(bias_all))

    n_wt = (WINDOW + QT) // KT
    k0w = pl.multiple_of(jnp.maximum(t0 - WINDOW, 0), KT)
    kt_w = k0w // KT
    keys_w = kw_ref[0, pl.ds(k0w, WINDOW + QT), :]
    dw = (t0 + lax.broadcasted_iota(jnp.int32, (WINDOW + QT, QT), 1)
          - (k0w + lax.broadcasted_iota(jnp.int32, (WINDOW + QT, QT), 0)))
    sw = _dot(keys_w, qpad) + tile_all(jnp.where(dw >= 0, jnp.where(dw < WINDOW, 0.0, NEG_INF), NEG_INF))
    pw = jnp.exp2(sw - jnp.max(sw, axis=0, keepdims=True)).astype(BF16)
    acc_w = values_dot(lambda g: vwt_ref[0, kt_w, vrows(g), :], pw[0:KT])
    for j in range(1, n_wt):
        acc_w = acc_w + values_dot(lambda g: vwt_ref[0, kt_w + j, vrows(g), :], pw[j * KT:(j + 1) * KT])

    n_pairs = (n_tiles + 1) // 2
    one = jnp.ones((1, NQ), F32)
    m_s, acc_s, alpha0, alpha1 = lax.fori_loop(
        0, n_pairs,
        lambda j, carry: lax.cond(j % 2 == 0,
                                  lambda c: sel_pair(2 * j, c, bufs_a, bufs_b),
                                  lambda c: sel_pair(2 * j, c, bufs_b, bufs_a), carry),
        (jnp.full((1, NQ), NEG_INF, F32), jnp.zeros((V_ROWS, NQ), F32), one, one))
    acc_s = sel_values(2 * n_pairs - 2, 0, acc_s, alpha0)
    acc_s = sel_values(2 * n_pairs - 1, 1, acc_s, alpha1)
    seen = lax.broadcasted_iota(jnp.int32, (QT, QT), 0) <= lax.broadcasted_iota(jnp.int32, (QT, QT), 1)
    own = jnp.where(tile_all(seen), own, NEG_INF)
    m_new = jnp.maximum(m_s, jnp.max(own, axis=0, keepdims=True))
    acc_s = acc_s * jnp.exp2(m_s - m_new) + values_dot(lambda g: vsd_ref[0, qb, vrows(g), :],
                                                       jnp.exp2(own - m_new).astype(BF16))

    gates = _sigmoid(gt_ref[0])
    grow = lambda j: jnp.concatenate([gates[g, j, r:r + 1, :] for g in range(G) for r in range(R)], axis=1)
    o = (grow(0) * o_c + grow(1) * (acc_s[0:D] / acc_s[D:D + 1])
         + grow(2) * (acc_w[0:D] / acc_w[D:D + 1]))
    halves = []
    for h in range(G * R // 2):
        pair = jnp.concatenate([o[:, (2 * h) * QT:(2 * h + 1) * QT],
                                o[:, (2 * h + 1) * QT:(2 * h + 2) * QT]], axis=0)
        halves.append(pair.T)
    o_ref[...] = jnp.concatenate(halves, axis=1)


def _nsa_attn(qt, kcmp, vct, ks3, vst, vsd, kw3, vwt, gt, ov_t, B, T):
    G, R = NSA_KV_HEADS, NSA_GROUP
    nq = T // Q_TILE
    nk = T // KEY_TILE
    nch = kcmp.shape[1]
    n_sel = ov_t.shape[0]
    NQ = G * R * Q_TILE
    assert (T // SEL_TILE) % 2 == 0 and n_sel <= LANES and Q_TILE == KEY_TILE
    assert nq % CMP_VARIANTS == 0 and n_sel % (8 * CMP_VARIANTS) == 0 and nch % (8 * CMP_VARIANTS) == 0
    return pl.pallas_call(
        _nsa_attn_kernel,
        grid=(B, nq),
        in_specs=[pl.BlockSpec((1, 1, NSA_WIDTH, Q_TILE), lambda b, q: (b, q, 0, 0)),
                  pl.BlockSpec((1, nch, LANES), lambda b, q: (b, 0, 0)),
                  pl.BlockSpec((1, G * HEAD_DIM, nch), lambda b, q: (b, 0, 0)),
                  pl.BlockSpec((1, T, 2 * LANES), lambda b, q: (b, 0, 0)),
                  pl.BlockSpec((1, T // SEL_TILE, G * V_ROWS, SEL_TILE), lambda b, q: (b, 0, 0, 0)),
                  pl.BlockSpec((1, nk, G * V_ROWS, KEY_TILE), lambda b, q: (b, 0, 0, 0)),
                  pl.BlockSpec((1, T, LANES), lambda b, q: (b, 0, 0)),
                  pl.BlockSpec((1, nk, G * V_ROWS, KEY_TILE), lambda b, q: (b, 0, 0, 0)),
                  pl.BlockSpec((1, G, 3, R, Q_TILE), lambda b, q: (b, 0, 0, 0, q)),
                  pl.BlockSpec((n_sel, nch), lambda b, q: (0, 0))],
        out_specs=pl.BlockSpec((Q_TILE, NSA_WIDTH), lambda b, q: (b * nq + q, 0)),
        out_shape=jax.ShapeDtypeStruct((B * T, NSA_WIDTH), F32),
        scratch_shapes=[pltpu.VMEM((2 * LANES, NQ), BF16), pltpu.VMEM((HEAD_DIM, NQ), F32),
                        pltpu.VMEM((n_sel, G * Q_TILE), F32),
                        *[pltpu.VMEM((SEL_TILE, NQ), F32)] * 4,
                        *[pltpu.VMEM((SEL_TILE, NQ), BF16)] * 2],
        compiler_params=_params("arbitrary", "arbitrary"),
        name="nsa_attn",
    )(qt, kcmp, vct, ks3, vst, vsd, kw3, vwt, gt, ov_t)


def _first_index_of(vals, target):
    idx = jnp.full_like(target, float(len(vals) - 1))
    for i in range(len(vals) - 2, -1, -1):
        idx = jnp.where(vals[i] == target, float(i), idx)
    return idx


def _pick(vals, idx):
    out = vals[-1]
    for i in range(len(vals) - 2, -1, -1):
        out = jnp.where(idx == float(i), vals[i], out)
    return out


def _route_rows(score, bias):
    E, G, P = N_EXPERTS, N_GROUPS, EXPERTS_PER_GROUP
    sel = score + bias
    s = [sel[e:e + 1, :] for e in range(E)]
    raw = [score[e:e + 1, :] for e in range(E)]
    grp = []
    for gi in range(G):
        a = s[gi * P:(gi + 1) * P]
        best = None
        for i in range(P):
            for j in range(i + 1, P):
                pair = a[i] + a[j]
                best = pair if best is None else jnp.maximum(best, pair)
        grp.append(best)
    gmax = functools.reduce(jnp.maximum, grp)
    g_star = _first_index_of(grp, gmax)
    v = [_pick([s[gi * P + i] for gi in range(G)], g_star) for i in range(P)]
    w = [_pick([raw[gi * P + i] for gi in range(G)], g_star) for i in range(P)]
    i1 = _first_index_of(v, functools.reduce(jnp.maximum, v))
    v2 = [jnp.where(i1 == float(i), -jnp.inf, v[i]) for i in range(P)]
    i2 = _first_index_of(v2, functools.reduce(jnp.maximum, v2))
    w1, w2 = _pick(w, i1), _pick(w, i2)
    tot = w1 + w2
    zero = jnp.zeros_like(tot)
    e1, e2 = g_star * P + i1, g_star * P + i2
    n = score.shape[1]
    eidx = lax.broadcasted_iota(jnp.int32, (E, n), 0).astype(F32)
    oh1, oh2 = jnp.where(eidx == e1, 1.0, 0.0), jnp.where(eidx == e2, 1.0, 0.0)
    earlier = jnp.where(lax.broadcasted_iota(jnp.int32, (n, n), 0) < lax.broadcasted_iota(jnp.int32, (n, n), 1),
                        1.0, 0.0).astype(BF16)
    cnt = _dot(jnp.concatenate([oh1, oh2], axis=0).astype(BF16), earlier)
    rank1 = jnp.sum(oh1 * cnt[0:E], axis=0, keepdims=True)
    rank2 = jnp.sum(oh2 * cnt[E:2 * E], axis=0, keepdims=True)
    lane = lax.broadcasted_iota(jnp.int32, (E, LANES), 1)
    totals = jnp.where(lane == 0, jnp.sum(oh1, axis=1, keepdims=True),
                       jnp.where(lane == 1, jnp.sum(oh2, axis=1, keepdims=True), 0.0))
    return jnp.concatenate([e1, e2, w1 / tot, w2 / tot, rank1, rank2, zero, zero], axis=0), totals


def _merge_kernel(ys_ref, g_ref, bonus_ref, gng_ref, gnb_ref, bd_ref, yb_ref, pm_ref, x_ref, mod_ref,
                  ng_ref, wa_ref, wb_ref, wo_ref, rw_ref, rb_ref, o_x, o_h, o_route, o_tot):
    m = mod_ref[pl.program_id(0)]
    bd = bd_ref[...]
    y = ys_ref[...]
    mean = _dot_split_lhs(y, bd)
    yc = y - mean
    var = _dot_split_lhs(yc * yc, bd)
    ya = (yc * lax.rsqrt(var + RWKV_GN_EPS) * gng_ref[...] + gnb_ref[...] + bonus_ref[...]) * g_ref[...]
    pm = pm_ref[...]
    D = x_ref.shape[1]
    mix = (_sigmoid(pm[:, 0:D]) * _dot(ya.astype(BF16), wa_ref[...])
           + _sigmoid(pm[:, D:2 * D]) * _dot(yb_ref[...].astype(BF16), wb_ref[...]))
    x = x_ref[...] + m[2:3] * _dot(mix.astype(BF16), wo_ref[...])
    o_x[...] = x
    ms = jnp.mean(x * x, axis=-1, keepdims=True)
    h = x * lax.rsqrt(ms + NORM_EPS) * ng_ref[...]
    h = h * (1.0 + m[4:5]) + m[3:4]
    o_h[...] = h
    score = _sigmoid(_dot_3pass(h, rw_ref).T[0:N_EXPERTS, :])
    o_route[...], o_tot[...] = _route_rows(score, rb_ref[...])


def _merge(ys, g, bonus, gn_g, gn_b, yb, pm, x2, mod, ng, wa, wb, wo, router_w, router_b, B, T, tm=256):
    N, D = x2.shape
    W = RWKV_WIDTH
    nt = T // tm
    row = lambda b, t: (b * nt + t, 0)
    full = lambda shape: pl.BlockSpec(shape, lambda b, t: (0,) * len(shape))
    return pl.pallas_call(
        _merge_kernel,
        grid=(B, nt),
        in_specs=[pl.BlockSpec((tm, W), row), pl.BlockSpec((tm, W), row), pl.BlockSpec((tm, W), row),
                  full((1, W)), full((1, W)), full((W, W)),
                  pl.BlockSpec((tm, NSA_WIDTH), row), pl.BlockSpec((tm, 2 * D), row),
                  pl.BlockSpec((tm, D), row), full((B, 6, D)), full((1, D)),
                  full((W, D)), full((NSA_WIDTH, D)), full((D, D)), full((2, D, LANES)),
                  full((N_EXPERTS, 1))],
        out_specs=[pl.BlockSpec((tm, D), row), pl.BlockSpec((tm, D), row),
                   pl.BlockSpec((8, tm), lambda b, t: (0, b * nt + t)),
                   pl.BlockSpec((N_EXPERTS, LANES), lambda b, t: (b * nt + t, 0))],
        out_shape=[jax.ShapeDtypeStruct((N, D), F32), jax.ShapeDtypeStruct((N, D), F32),
                   jax.ShapeDtypeStruct((8, N), F32), jax.ShapeDtypeStruct((N // tm * N_EXPERTS, LANES), F32)],
        compiler_params=_params("arbitrary", "arbitrary"),
        name="merge_out",
    )(ys, g, bonus, gn_g.reshape(1, W), gn_b.reshape(1, W), _head_block_diag(W, 1.0 / HEAD_DIM),
      yb, pm, x2, mod, ng, wa, wb, wo,
      _hi_lo(jnp.zeros((D, LANES), F32).at[:, :N_EXPERTS].set(router_w)), router_b.reshape(N_EXPERTS, 1))


def _route(route, totals, N):
    wts = route[TOP_K:2 * TOP_K].T
    NK = N * TOP_K
    E = N_EXPERTS
    n_tiles = totals.shape[0] // E
    expert = route[0:TOP_K].astype(jnp.int32)
    rank = route[2 * TOP_K:3 * TOP_K].astype(jnp.int32)
    per = totals.reshape(n_tiles, E, LANES)[:, :, 0:TOP_K].astype(jnp.int32).transpose(0, 2, 1)
    per = per.reshape(n_tiles * TOP_K, E)
    csum = jnp.cumsum(per, axis=0)
    counts = csum[-1]
    padded = (counts + MOE_BLOCK - 1) // MOE_BLOCK * MOE_BLOCK
    pad_end = jnp.cumsum(padded)
    pad_start = pad_end - padded
    first = (pad_start[None, :] + csum - per).reshape(n_tiles, TOP_K, E).transpose(1, 0, 2)
    first = jnp.repeat(first, N // n_tiles, axis=1)
    mine = expert[:, :, None] == jnp.arange(E, dtype=jnp.int32)[None, None, :]
    dest = (jnp.sum(jnp.where(mine, first, 0), axis=-1) + rank).reshape(-1)
    n_blk = -(-NK // MOE_BLOCK) + N_EXPERTS
    blk_start = jnp.arange(n_blk, dtype=jnp.int32) * MOE_BLOCK
    blk_expert = jnp.sum((pad_end[None, :] <= blk_start[:, None]).astype(jnp.int32), axis=1)
    blk_expert = jnp.clip(blk_expert, 0, N_EXPERTS - 1)
    blk_valid = jnp.clip((pad_start + counts)[blk_expert] - blk_start, 0, MOE_BLOCK).astype(jnp.int32)
    dest = jnp.pad(dest.astype(jnp.int32).reshape(NK // SC_WINDOW, SC_WINDOW), ((0, 0), (0, LANES - SC_WINDOW)))
    return wts, dest, blk_expert, blk_valid, n_blk


SC_WINDOW = 32


def _sc_mesh():
    return plsc.VectorSubcoreMesh(core_axis_name="c", subcore_axis_name="s")


def _sc_dispatch(h, dest, n_slots):
    N, D = h.shape
    W = SC_WINDOW
    nw = N // W

    @pl.kernel(out_type=jax.ShapeDtypeStruct((n_slots, D), h.dtype), mesh=_sc_mesh(), scratch_types=[])
    def dispatch(h_hbm, i_hbm, o_hbm):
        def body(x_vmem, i_vmem):
            pltpu.sync_copy(x_vmem, o_hbm.at[i_vmem.at[0, pl.ds(0, W)]])

        pltpu.emit_pipeline(
            body, grid=(TOP_K, nw),
            in_specs=[pl.BlockSpec((W, D), lambda k, i: (i, 0)),
                      pl.BlockSpec((1, LANES), lambda k, i: (k * nw + i, 0))],
            out_specs=[], core_axis_name=("c", "s"),
            dimension_semantics=(pltpu.PARALLEL, pltpu.PARALLEL))(h_hbm, i_hbm)

    return dispatch(h, dest)


def _sc_collect(ys, dest):
    W = SC_WINDOW
    NK = dest.shape[0] * W
    D = ys.shape[1]
    half = NK // TOP_K // W

    @pl.kernel(out_type=jax.ShapeDtypeStruct((NK, D), ys.dtype), mesh=_sc_mesh(), scratch_types=[])
    def collect(y_hbm, i_hbm, o_hbm):
        def body(i_vmem, o_vmem):
            pltpu.sync_copy(y_hbm.at[i_vmem.at[0, pl.ds(0, W)]], o_vmem)

        pltpu.emit_pipeline(
            body, grid=(TOP_K, half),
            in_specs=[pl.BlockSpec((1, LANES), lambda k, i: (k * half + i, 0))],
            out_specs=[pl.BlockSpec((W, D), lambda k, i: (k * half + i, 0))],
            core_axis_name=("c", "s"),
            dimension_semantics=(pltpu.PARALLEL, pltpu.PARALLEL))(i_hbm, o_hbm)

    return collect(ys, dest)


def _moe_dense_kernel(be_ref, nv_ref, x_ref, wg_ref, wu_ref, wd_ref, o_ref, wg_b, wu_b, wd_b):
    i = pl.program_id(0)
    nv = nv_ref[i]

    @pl.when((i == 0) | (be_ref[i] != be_ref[jnp.maximum(i - 1, 0)]))
    def _():
        wg_b[...] = wg_ref[0, 0].astype(BF16)
        wu_b[...] = wu_ref[0, 0].astype(BF16)
        wd_b[...] = wd_ref[0, 0].astype(BF16)

    @pl.when(nv > 0)
    def _():
        x = x_ref[...].astype(BF16)
        gate = _dot(x, wg_b[...])
        up = _dot(x, wu_b[...])
        o_ref[...] = _dot((gate * _sigmoid(gate) * up).astype(BF16), wd_b[...])

    @pl.when(nv == 0)
    def _():
        o_ref[...] = jnp.zeros_like(o_ref)


def _moe_dense(xs, blk_expert, blk_valid, n_blk, layer, wg, wu, wd):
    P, D = xs.shape
    DE = wg.shape[3]
    wmap = lambda i, be, nv: (layer, be[i], 0, 0)
    grid_spec = pltpu.PrefetchScalarGridSpec(
        num_scalar_prefetch=2,
        grid=(n_blk,),
        in_specs=[pl.BlockSpec((MOE_BLOCK, D), lambda i, be, nv: (i, 0)), pl.BlockSpec((1, 1, D, DE), wmap),
                  pl.BlockSpec((1, 1, D, DE), wmap), pl.BlockSpec((1, 1, DE, D), wmap)],
        out_specs=pl.BlockSpec((MOE_BLOCK, D), lambda i, be, nv: (i, 0)),
        scratch_shapes=[pltpu.VMEM((D, DE), BF16), pltpu.VMEM((D, DE), BF16), pltpu.VMEM((DE, D), BF16)],
    )
    return pl.pallas_call(
        _moe_dense_kernel,
        grid_spec=grid_spec,
        out_shape=jax.ShapeDtypeStruct((P, D), F32),
        compiler_params=_params("arbitrary"),
        name="moe_experts",
    )(blk_expert, blk_valid, xs, wg, wu, wd)


def _final_kernel(x_ref, y0_ref, y1_ref, w_ref, mod_ref, o_ref):
    m = mod_ref[pl.program_id(0)]
    w = w_ref[...]
    o_ref[...] = x_ref[...] + m[5:6] * (w[:, 0:1] * y0_ref[...] + w[:, 1:2] * y1_ref[...])


def _final(x2, ybuf, wts, mod, B, T, tm=512):
    N, D = x2.shape
    nt = T // tm
    row = lambda b, t: (b * nt + t, 0)
    return pl.pallas_call(
        _final_kernel,
        grid=(B, nt),
        in_specs=[pl.BlockSpec((tm, D), row), pl.BlockSpec((tm, D), row),
                  pl.BlockSpec((tm, D), lambda b, t: (N // tm + b * nt + t, 0)),
                  pl.BlockSpec((tm, TOP_K), row), pl.BlockSpec((B, 6, D), lambda b, t: (0, 0, 0))],
        out_specs=pl.BlockSpec((tm, D), row),
        out_shape=jax.ShapeDtypeStruct((N, D), F32),
        compiler_params=_params("arbitrary", "arbitrary"),
        name="moe_combine",
    )(x2, ybuf, ybuf, wts, mod)


def _overlap_t(n_sel, n_cmp_pad):
    ci = jnp.arange(n_cmp_pad)[None, :] * CMP_STRIDE
    sj = jnp.arange(n_sel)[:, None] * SEL_BLOCK
    ov = (ci <= sj + SEL_BLOCK - 1) & (ci + CMP_BLOCK - 1 >= sj) & (jnp.arange(n_cmp_pad)[None, :] < n_cmp_pad - 1)
    return ov.astype(BF16)


def kernel(x, c, w_ada, b_ada, norm_g, w_in, b_in, rwkv_mu, rwkv_w0, rwkv_w2, rwkv_a0, rwkv_a2, rwkv_g2,
           rwkv_k_k, rwkv_k_a, rwkv_r_k, rwkv_gn_g, rwkv_gn_b, qk_norm_g, cmp_pos, cmp_w1, cmp_w2,
           w_up_rwkv, w_up_nsa, w_out, router_w, router_b, exp_w_gate, exp_w_up, exp_w_down):
    B, T, D = x.shape
    L = w_ada.shape[0]
    N = B * T
    mods = _ada(c, w_ada, b_ada)
    tables = _rope_tables(jnp.arange(T, dtype=jnp.int32))
    nch = T // CMP_STRIDE
    tables_cmp = _rope_tables(jnp.arange(nch, dtype=jnp.int32) * CMP_STRIDE + CMP_BLOCK - 1)
    ov_t = _overlap_t(T // SEL_BLOCK, nch)
    n_gate = NSA_GATE_COLS
    x2 = x.reshape(N, D)
    for l in range(L):
        g0 = _SEG_KV[1] + n_gate
        w_pad = jnp.concatenate([w_in[l][:, :g0], jnp.zeros((D, GATE_PAD - n_gate), F32), w_in[l][:, g0:]],
                                axis=1).astype(BF16)
        b_pad = jnp.concatenate([b_in[l][:g0], jnp.zeros((GATE_PAD - n_gate,), F32), b_in[l][g0:]]).reshape(1, -1)
        p_rw, p_q, p_kv, p_gate, p_merge = _inproj(x2, mods[l], norm_g[l, 0].reshape(1, D), w_pad, b_pad, B, T)
        r, k, v, al, bb, ld, g, bonus = _rwkv_pre(p_rw, rwkv_mu[l], rwkv_w0[l], rwkv_w2[l], rwkv_a0[l],
                                                  rwkv_a2[l], rwkv_g2[l], rwkv_k_k[l], rwkv_k_a[l],
                                                  rwkv_r_k[l], B, T)
        ys = _rwkv_scan(r, k, v, al, bb, ld, B, T)
        qt, ks, kw, vst, vsd, vwt = _nsa_prep(p_q, p_kv, tables, qk_norm_g[l], B, T)
        kv3 = p_kv.reshape(B, T, KV_COLS)
        kcmp = _nsa_cmp(kv3, 0, cmp_pos[l], cmp_w1[l], cmp_w2[l], qk_norm_g[l, 1], tables_cmp)
        vct = _nsa_cmp(kv3, 1, cmp_pos[l], cmp_w1[l], cmp_w2[l], None, None)
        gt = p_gate[:, :n_gate].reshape(B, T, NSA_KV_HEADS, NSA_GROUP, 3).transpose(0, 2, 4, 3, 1)
        yb = _nsa_attn(qt, kcmp, vct, ks.reshape(B, T, 2 * LANES), vst, vsd, kw.reshape(B, T, LANES), vwt, gt, ov_t,
                       B, T)
        x2, h2, route, totals = _merge(ys, g, bonus, rwkv_gn_g[l], rwkv_gn_b[l], yb, p_merge, x2, mods[l],
                               norm_g[l, 1].reshape(1, D), w_up_rwkv[l].astype(BF16),
                               w_up_nsa[l].astype(BF16), w_out[l].astype(BF16), router_w, router_b, B, T)
        wts, dest, blk_expert, blk_valid, n_blk = _route(route, totals, N)
        xs = _sc_dispatch(h2, dest, n_blk * MOE_BLOCK)
        ys = _moe_dense(xs, blk_expert, blk_valid, n_blk, l, exp_w_gate, exp_w_up, exp_w_down)
        ybuf = _sc_collect(ys, dest)
        x2 = _final(x2, ybuf, wts, mods[l], B, T)
    return x2.reshape(B, T, D)
```

```python
import functools
import math

import jax
import jax.numpy as jnp
from jax import lax
from jax.experimental import pallas as pl
from jax.experimental.pallas import tpu as pltpu
from jax.experimental.pallas import tpu_sc as plsc

F32 = jnp.float32
BF16 = jnp.bfloat16
HI = lax.Precision.HIGHEST

D_MODEL = 1024
RWKV_HEADS = 8
HEAD_DIM = 64
RWKV_WIDTH = RWKV_HEADS * HEAD_DIM
DECAY_LORA = 64
ICLR_LORA = 64
GATE_LORA = 128
RWKV_GN_EPS = 64e-5
RWKV_COLS = 3 * RWKV_WIDTH + DECAY_LORA + ICLR_LORA + GATE_LORA

NSA_Q_HEADS = 8
NSA_KV_HEADS = 2
NSA_GROUP = NSA_Q_HEADS // NSA_KV_HEADS
NSA_WIDTH = NSA_Q_HEADS * HEAD_DIM
CMP_STRIDE = 16
CMP_BLOCK = 2 * CMP_STRIDE
CMP_HIDDEN = 256
SEL_BLOCK = 64
SEL_SHIFT = 6
SEL_TOPK = 16
WINDOW = 512
FORCE_SCORE = 1e4
NEG_INF = -1e30
ROPE_THETA = 500000.0
ROPE_DIM = HEAD_DIM // 4
KV_COLS = 6 * NSA_KV_HEADS * HEAD_DIM
NSA_GATE_COLS = 3 * NSA_Q_HEADS
GATE_PAD = 128

N_EXPERTS = 16
N_GROUPS = 4
EXPERTS_PER_GROUP = N_EXPERTS // N_GROUPS
TOP_K = 2
D_EXPERT = 512
MOE_BLOCK = 256
NORM_EPS = 1e-6

LANES = 128
CHUNK = 64
KEY_TILE = 128
SEL_TILE = 512
CMP_VARIANTS = 4
V_ROWS = 80
Q_SCALE = HEAD_DIM ** -0.5 * math.log2(math.e)
Q_TILE = 128
F32_TINY = float(jnp.finfo(jnp.float32).tiny)

_SEG_RW = (0, RWKV_COLS)
_SEG_Q = (_SEG_RW[1], _SEG_RW[1] + NSA_WIDTH)
_SEG_KV = (_SEG_Q[1], _SEG_Q[1] + KV_COLS)
_SEG_GATE = (_SEG_KV[1], _SEG_KV[1] + GATE_PAD)
_SEG_MERGE = (_SEG_GATE[1], _SEG_GATE[1] + 2 * D_MODEL)
IN_COLS_PAD = _SEG_MERGE[1]

_VMEM_LIMIT = 56 * 1024 * 1024


def _dot(a, b, precision=None):
    return jnp.dot(a, b, preferred_element_type=F32, precision=precision)


def _dot_tb(a, b, precision=None):
    return lax.dot_general(a, b, (((1,), (1,)), ((), ())), preferred_element_type=F32,
                           precision=precision)


def _dot_ta(a, b, precision=None):
    return lax.dot_general(a, b, (((0,), (0,)), ((), ())), preferred_element_type=F32,
                           precision=precision)


def _split_bf16(x, terms):
    parts = []
    for _ in range(terms - 1):
        parts.append(x.astype(BF16))
        x = x - parts[-1].astype(F32)
    parts.append(x.astype(BF16))
    return parts


def _dot_split_lhs(x, w_bf, terms=2):
    return functools.reduce(jnp.add, [_dot(p, w_bf) for p in _split_bf16(x, terms)])


def _dot_split_rhs(w_bf, x, terms=2):
    return functools.reduce(jnp.add, [_dot(w_bf, p) for p in _split_bf16(x, terms)])


def _dot_3pass(x, w_hl_ref):
    x_hi, x_lo = _split_bf16(x, 2)
    w_hi = w_hl_ref[0]
    return _dot(x_hi, w_hi) + _dot(x_lo, w_hi) + _dot(x_hi, w_hl_ref[1])


def _hi_lo(w):
    hi = w.astype(BF16)
    return jnp.stack([hi, (w - hi.astype(F32)).astype(BF16)])


def _params(*sem):
    return pltpu.CompilerParams(dimension_semantics=sem, vmem_limit_bytes=_VMEM_LIMIT)


def _sigmoid(x):
    return 1.0 / (1.0 + jnp.exp(-x))


def _ada_kernel(c_ref, w_ref, b_ref, o_ref):
    c = c_ref[...]
    s = c * _sigmoid(c)
    o_ref[0] = _dot(s, w_ref[0], HI) + b_ref[0]


def _ada(c, w_ada, b_ada):
    L, D, D6 = w_ada.shape
    B = c.shape[0]
    rows = 8
    cp = jnp.zeros((rows, D), F32).at[:B].set(c)
    tn = 1536
    out = pl.pallas_call(
        _ada_kernel,
        grid=(L, D6 // tn),
        in_specs=[pl.BlockSpec((rows, D), lambda l, j: (0, 0)),
                  pl.BlockSpec((1, D, tn), lambda l, j: (l, 0, j)),
                  pl.BlockSpec((1, 1, tn), lambda l, j: (l, 0, j))],
        out_specs=pl.BlockSpec((1, rows, tn), lambda l, j: (l, 0, j)),
        out_shape=jax.ShapeDtypeStruct((L, rows, D6), F32),
        compiler_params=_params("arbitrary", "arbitrary"),
        name="ada_mod",
    )(cp, w_ada, b_ada.reshape(L, 1, D6))
    return out[:, :B].reshape(L, B, 6, D)


def _moe_residual(x_ref, y0_ref, y1_ref, w_ref, m_prev):
    w = w_ref[...]
    return x_ref[...] + m_prev[5:6] * (w[:, 0:1] * y0_ref[...] + w[:, 1:2] * y1_ref[...])


def _inproj_kernel(*refs, after_moe):
    if after_moe:
        x_ref, y0_ref, y1_ref, wts_ref, modp_ref, mod_ref, g_ref, w_ref, b_ref, o_x = refs[:10]
        x = _moe_residual(x_ref, y0_ref, y1_ref, wts_ref, modp_ref[pl.program_id(0)])
        o_x[...] = x
    else:
        x_ref, mod_ref, g_ref, w_ref, b_ref = refs[:5]
        x = x_ref[...]
    o_rw, o_q, o_kv, o_gate, o_merge = refs[-5:]
    m = mod_ref[pl.program_id(0)]
    ms = jnp.mean(x * x, axis=-1, keepdims=True)
    h = x * lax.rsqrt(ms + NORM_EPS) * g_ref[...]
    h = h * (1.0 + m[1:2]) + m[0:1]
    hb = h.astype(BF16)
    for o, (a, e) in ((o_rw, _SEG_RW), (o_q, _SEG_Q), (o_kv, _SEG_KV), (o_gate, _SEG_GATE),
                      (o_merge, _SEG_MERGE)):
        o[...] = _dot(hb, w_ref[:, a:e]) + b_ref[:, a:e]


def _inproj(x2, pending_moe, mod, g, w_pad, b_pad, B, T, tm=512):
    N, D = x2.shape
    nt = T // tm
    row = lambda b, t: (b * nt + t, 0)
    mods_spec = pl.BlockSpec((B, 6, D), lambda b, t: (0, 0, 0))
    widths = [e - a for a, e in (_SEG_RW, _SEG_Q, _SEG_KV, _SEG_GATE, _SEG_MERGE)]
    in_specs = [pl.BlockSpec((tm, D), row)]
    args = [x2]
    if pending_moe is not None:
        ybuf, wts, mod_prev = pending_moe
        in_specs += [pl.BlockSpec((tm, D), row), pl.BlockSpec((tm, D), lambda b, t: (N // tm + b * nt + t, 0)),
                     pl.BlockSpec((tm, TOP_K), row), mods_spec]
        args += [ybuf, ybuf, wts, mod_prev]
        widths = [D] + widths
    in_specs += [mods_spec, pl.BlockSpec((1, D), lambda b, t: (0, 0)),
                 pl.BlockSpec((D, IN_COLS_PAD), lambda b, t: (0, 0)),
                 pl.BlockSpec((1, IN_COLS_PAD), lambda b, t: (0, 0))]
    return pl.pallas_call(
        functools.partial(_inproj_kernel, after_moe=pending_moe is not None),
        grid=(B, nt),
        in_specs=in_specs,
        out_specs=[pl.BlockSpec((tm, w), row) for w in widths],
        out_shape=[jax.ShapeDtypeStruct((N, w), F32) for w in widths],
        compiler_params=_params("arbitrary", "arbitrary"),
        name="in_proj",
    )(*args, mod, g, w_pad, b_pad)


def _rwkv_pre_kernel(p_ref, mu_ref, w0_ref, w2_ref, a0_ref, a2_ref, g2_ref, kk_ref, ka_ref, rk_ref,
                     bd_ref, o_r, o_k, o_v, o_al, o_b, o_ld, o_g, o_bonus, carry_ref):
    W = RWKV_WIDTH

    @pl.when(pl.program_id(1) == 0)
    def _():
        carry_ref[...] = jnp.zeros_like(carry_ref)

    p = p_ref[...]
    ts = p.shape[0]
    rows = lax.broadcasted_iota(jnp.int32, p.shape, 0)
    shifted = jnp.where(rows == 0, carry_ref[0:1, :], pltpu.roll(p, 1, 0))
    carry_ref[0:1, :] = p[ts - 1:ts, :]
    pm = p + (shifted - p) * mu_ref[...]
    r = pm[:, 0:W]
    k = pm[:, W:2 * W]
    v = pm[:, 2 * W:3 * W]
    wa = pm[:, 3 * W:3 * W + DECAY_LORA + ICLR_LORA]
    gl = pm[:, 3 * W + DECAY_LORA + ICLR_LORA:]
    xw = w0_ref[...] + _dot_3pass(jnp.tanh(wa), w2_ref)
    ld = -math.exp(-0.5) * _sigmoid(xw)
    a = _sigmoid(a0_ref[...] + _dot_3pass(wa, a2_ref))
    g = _dot_3pass(_sigmoid(gl), g2_ref)
    bd = bd_ref[...]
    kk = k * kk_ref[...]
    nrm = jnp.sqrt(_dot_split_lhs(kk * kk, bd))
    kk = kk / jnp.maximum(nrm, 1e-12)
    k2 = k * (1.0 + (a - 1.0) * ka_ref[...])
    bonus = _dot_split_lhs(r * k2 * rk_ref[...], bd) * v
    o_r[...] = r
    o_k[...] = k2
    o_v[...] = v
    o_al[...] = kk
    o_b[...] = -kk * a
    o_ld[...] = ld
    o_g[...] = g
    o_bonus[...] = bonus


def _head_block_diag(width, scale=1.0):
    i = jnp.arange(width) // HEAD_DIM
    return ((i[:, None] == i[None, :]).astype(F32) * scale).astype(BF16)


def _rwkv_pre(p_rw, mu, w0, w2, a0, a2, g2, k_k, k_a, r_k, B, T, ts=512):
    N = p_rw.shape[0]
    W = RWKV_WIDTH
    nt = T // ts
    row = lambda b, t: (b * nt + t, 0)
    zl = jnp.zeros((DECAY_LORA, W), F32)
    w2p = jnp.concatenate([w2, zl], axis=0)
    a2p = jnp.concatenate([zl, a2], axis=0)
    full = lambda shape: pl.BlockSpec(shape, lambda b, t: (0,) * len(shape))
    vec = lambda z: z.reshape(1, -1)
    return pl.pallas_call(
        _rwkv_pre_kernel,
        grid=(B, nt),
        in_specs=[pl.BlockSpec((ts, RWKV_COLS), row), full((1, RWKV_COLS)), full((1, W)),
                  full((2, 2 * DECAY_LORA, W)), full((1, W)), full((2, 2 * DECAY_LORA, W)),
                  full((2, GATE_LORA, W)), full((1, W)), full((1, W)), full((1, W)), full((W, W))],
        out_specs=[pl.BlockSpec((ts, W), row)] * 8,
        out_shape=[jax.ShapeDtypeStruct((N, W), F32)] * 8,
        scratch_shapes=[pltpu.VMEM((8, RWKV_COLS), F32)],
        compiler_params=_params("arbitrary", "arbitrary"),
        name="rwkv_pre",
    )(p_rw, vec(mu), vec(w0), _hi_lo(w2p), vec(a0), _hi_lo(a2p), _hi_lo(g2), vec(k_k), vec(k_a), vec(r_k),
      _head_block_diag(W))


def _bf(x):
    return x.astype(BF16)


def _scan_local(chunks, eye, strict, incl, m0, m1, between_stages=lambda: None):
    C = CHUNK
    n = range(len(chunks))
    st = lambda z: jnp.concatenate([z * m0, z * m1], axis=0)
    zero = jnp.zeros((2 * C, 2 * C), F32)
    at_b, rt_s, vs, vs_b, lhs_a, rhs_a, bk_t, dcol = [], [], [], [], [], [], [], []
    for r, k, v, al, bb, ld, cum in chunks:
        tot = cum[C - 1:C, :]
        dinv = jnp.exp(-cum)
        dend = jnp.exp(tot - cum)
        at_b.append(_bf(st(al * jnp.exp(cum - ld))))
        rt_s.append(st(r * jnp.exp(cum)))
        vs.append(st(v))
        vs_b.append(_bf(vs[-1]))
        lhs_a.append(jnp.concatenate([at_b[-1], _bf(rt_s[-1])], axis=0))
        rhs_a.append(_bf(jnp.concatenate([st(bb * dinv), st(k * dinv)], axis=0)))
        bk_t.append(_bf(jnp.concatenate([st(bb * dend).T, st(k * dend).T], axis=1)))
        dcol.append(jnp.sum(eye * jnp.exp(tot), axis=1, keepdims=True))
    between_stages()
    A = [_dot_tb(lhs_a[i], rhs_a[i]) for i in n]
    between_stages()
    a_ab = [jnp.where(strict, A[i][0:2 * C, 0:2 * C], zero) for i in n]
    a_ak = [_bf(jnp.where(strict, A[i][0:2 * C, 2 * C:4 * C], zero)) for i in n]
    a_r = [_bf(jnp.concatenate([jnp.where(incl, A[i][2 * C:4 * C, 0:2 * C], zero),
                                jnp.where(incl, A[i][2 * C:4 * C, 2 * C:4 * C], zero)], axis=1)) for i in n]
    akv = [_bf(_dot(a_ak[i], vs_b[i])) for i in n]
    between_stages()
    tinv = [eye + a_ab[i] for i in n]
    pw_b = [_bf(a_ab[i]) for i in n]
    pw_b = [_bf(_dot(pw_b[i], pw_b[i])) for i in n]
    between_stages()
    for step in range(5):
        rhs = [jnp.concatenate([pw_b[i], _bf(tinv[i])], axis=1) for i in n]
        if step == 4:
            rhs = [_bf(tinv[i]) for i in n]
        prod = [_dot(pw_b[i], rhs[i]) for i in n]
        tinv = [tinv[i] + prod[i][:, -2 * C:] for i in n]
        pw_b = [_bf(prod[i][:, 0:2 * C]) for i in n]
        between_stages()
    X = [_dot(_bf(tinv[i]), jnp.concatenate([at_b[i], akv[i]], axis=1)) for i in n]
    between_stages()
    w_b = [_bf(X[i][:, 0:LANES]) for i in n]
    uv0 = [jnp.concatenate([_bf(X[i][:, LANES:2 * LANES]), vs_b[i]], axis=0) for i in n]
    m_h = [_bf(_dot(bk_t[i][:, 0:2 * C], w_b[i])) for i in n]
    g_h = [_dot(bk_t[i], uv0[i]) for i in n]
    between_stages()
    q_h = [_bf(rt_s[i] + _dot(a_r[i][:, 0:2 * C], w_b[i])) for i in n]
    y0 = [_dot(a_r[i], uv0[i]) for i in n]
    return [(m_h[i], g_h[i], dcol[i], q_h[i], y0[i]) for i in n]


def _rwkv_scan_kernel(r_ref, k_ref, v_ref, al_ref, b_ref, ld_ref, o_ref, h_ref, *local_refs):
    C = CHUNK
    tc = r_ref.shape[0]
    nc = tc // C

    @pl.when(pl.program_id(2) == 0)
    def _():
        h_ref[...] = jnp.zeros_like(h_ref)
        for ref in local_refs:
            ref[...] = jnp.zeros_like(ref)

    seq = {"H": h_ref[...], "c": 0}

    def one_step():
        c = seq["c"]
        if c < nc:
            m_h, g_h, dcol, q_h, y0 = (ref[c] for ref in local_refs)
            h_b = _bf(seq["H"])
            Y = _dot(q_h, h_b) + y0
            o_ref[c * C:(c + 1) * C, :] = Y[0:C] + Y[C:2 * C]
            seq["H"] = dcol * seq["H"] + _dot(m_h, h_b) + g_h
            seq["c"] = c + 1

    tri = jnp.where(lax.broadcasted_iota(jnp.int32, (C, C), 1) <= lax.broadcasted_iota(jnp.int32, (C, C), 0),
                    1.0, 0.0).astype(BF16)
    r2 = lax.broadcasted_iota(jnp.int32, (2 * C, 2 * C), 0)
    c2 = lax.broadcasted_iota(jnp.int32, (2 * C, 2 * C), 1)
    eye = (r2 == c2).astype(F32)
    strict = (c2 & (C - 1)) < (r2 & (C - 1))
    incl = (c2 & (C - 1)) <= (r2 & (C - 1))
    lane = lax.broadcasted_iota(jnp.int32, (C, LANES), 1)
    m0 = (lane < HEAD_DIM).astype(F32)
    m1 = 1.0 - m0
    cum = _dot_split_rhs(tri, jnp.concatenate([ld_ref[c * C:(c + 1) * C, :] for c in range(nc)], axis=1), 3)
    chunks = []
    for c in range(nc):
        sl = slice(c * C, (c + 1) * C)
        chunks.append((r_ref[sl, :], k_ref[sl, :], v_ref[sl, :], al_ref[sl, :], b_ref[sl, :], ld_ref[sl, :],
                       cum[:, c * LANES:(c + 1) * LANES]))
    local = _scan_local(chunks, eye, strict, incl, m0, m1, between_stages=one_step)
    while seq["c"] < nc:
        one_step()
    h_ref[...] = seq["H"]
    for c, parts in enumerate(local):
        for ref, part in zip(local_refs, parts):
            ref[c] = part


def _rwkv_scan(r, k, v, al, bb, ld, B, T, tc=512):
    N, W = r.shape
    nt = T // tc
    nc = tc // CHUNK
    in_spec = pl.BlockSpec((tc, LANES), lambda b, h, t: (b * nt + jnp.minimum(t, nt - 1), h))
    out_spec = pl.BlockSpec((tc, LANES), lambda b, h, t: (b * nt + jnp.maximum(t - 1, 0), h))
    sq = (nc, LANES, LANES)
    return pl.pallas_call(
        _rwkv_scan_kernel,
        grid=(B, W // LANES, nt + 1),
        in_specs=[in_spec] * 6,
        out_specs=out_spec,
        out_shape=jax.ShapeDtypeStruct((N, W), F32),
        scratch_shapes=[pltpu.VMEM((LANES, LANES), F32), pltpu.VMEM(sq, BF16), pltpu.VMEM(sq, F32),
                        pltpu.VMEM((nc, LANES, 1), F32), pltpu.VMEM(sq, BF16), pltpu.VMEM(sq, F32)],
        compiler_params=_params("arbitrary", "arbitrary", "arbitrary"),
        name="rwkv_scan",
    )(r, k, v, al, bb, ld)


def _rope_tables(pos):
    half = ROPE_DIM // 2
    inv = jnp.power(ROPE_THETA, -jnp.arange(half, dtype=F32) * 2.0 / ROPE_DIM)
    ang = pos.astype(F32)[:, None] * inv[None, :]
    cos, sin = jnp.cos(ang), jnp.sin(ang)
    n = pos.shape[0]
    rest = HEAD_DIM - ROPE_DIM
    c = jnp.concatenate([cos, cos, jnp.ones((n, rest), F32)], axis=1)
    s_dn = jnp.concatenate([-sin, jnp.zeros((n, half + rest), F32)], axis=1)
    s_up = jnp.concatenate([jnp.zeros((n, half), F32), sin, jnp.zeros((n, rest), F32)], axis=1)
    rep = LANES // HEAD_DIM
    return jnp.tile(c, (1, rep)), jnp.tile(s_dn, (1, rep)), jnp.tile(s_up, (1, rep))


def _norm_rope(x, bd, g, c, s_dn, s_up):
    width = x.shape[1]
    half = ROPE_DIM // 2
    rep = width // LANES
    tile = (lambda z: jnp.concatenate([z] * rep, axis=1)) if rep > 1 else (lambda z: z)
    ms = _dot_split_lhs(x * x, bd)
    xn = x * lax.rsqrt(ms + NORM_EPS) * g
    return (xn * tile(c) + pltpu.roll(xn, width - half, 1) * tile(s_dn)
            + pltpu.roll(xn, half, 1) * tile(s_up))


def _nsa_prep_kernel(q_ref, kv_ref, c_ref, sd_ref, su_ref, gq_ref, gs_ref, gw_ref, bdq_ref, bdk_ref,
                     o_qt, o_ks, o_kw, o_vst, o_vsd, o_vwt):
    c, sd, su = c_ref[...], sd_ref[...], su_ref[...]
    q = _norm_rope(q_ref[...], bdq_ref[...], gq_ref[...], c, sd, su) * Q_SCALE
    qt = q.T
    ts = q.shape[0]
    kv = kv_ref[...]
    bdk = bdk_ref[...]
    pos = pl.program_id(1) * ts + lax.broadcasted_iota(jnp.int32, (ts, LANES), 0)
    blk_onehot = jnp.where((pos >> SEL_SHIFT) == lax.broadcasted_iota(jnp.int32, (ts, LANES), 1), 1.0, 0.0)
    ks = _norm_rope(kv[:, 2 * LANES:3 * LANES], bdk, gs_ref[...], c, sd, su)
    o_ks[...] = jnp.concatenate([ks, blk_onehot], axis=1).astype(BF16)
    o_kw[...] = _norm_rope(kv[:, 4 * LANES:5 * LANES], bdk, gw_ref[...], c, sd, su).astype(BF16)
    ones_rows = jnp.where(lax.broadcasted_iota(jnp.int32, (V_ROWS - HEAD_DIM, q.shape[0]), 0) == 0, 1.0, 0.0)

    def values_t(x):
        xt = x.T
        return jnp.concatenate([xt[0:HEAD_DIM], ones_rows, xt[HEAD_DIM:2 * HEAD_DIM], ones_rows], axis=0)

    vst = values_t(kv[:, 3 * LANES:4 * LANES])
    vwt = values_t(kv[:, 5 * LANES:6 * LANES])
    for j in range(q.shape[0] // KEY_TILE):
        sl = slice(j * KEY_TILE, (j + 1) * KEY_TILE)
        o_qt[0, j] = qt[:, sl].astype(BF16)
        o_vsd[0, j] = vst[:, sl].astype(BF16)
        o_vwt[0, j] = vwt[:, sl].astype(BF16)
    for j in range(q.shape[0] // SEL_TILE):
        o_vst[0, j] = vst[:, j * SEL_TILE:(j + 1) * SEL_TILE].astype(BF16)


def _nsa_prep(q, kv, tables, qk_g, B, T, ts=512):
    N = q.shape[0]
    nt = T // ts
    nk = ts // KEY_TILE
    ns = ts // SEL_TILE
    row = lambda b, t: (b * nt + t, 0)
    full = lambda shape: pl.BlockSpec(shape, lambda b, t: (0,) * len(shape))
    tab = pl.BlockSpec((ts, LANES), lambda b, t: (t, 0))
    gq = jnp.tile(qk_g[0], NSA_Q_HEADS).reshape(1, NSA_WIDTH)
    gs = jnp.tile(qk_g[2], NSA_KV_HEADS).reshape(1, LANES)
    gw = jnp.tile(qk_g[3], NSA_KV_HEADS).reshape(1, LANES)
    tiled = lambda rows: pl.BlockSpec((1, nk, rows, KEY_TILE), lambda b, t: (b, t, 0, 0))
    return pl.pallas_call(
        _nsa_prep_kernel,
        grid=(B, nt),
        in_specs=[pl.BlockSpec((ts, NSA_WIDTH), row), pl.BlockSpec((ts, KV_COLS), row), tab, tab, tab,
                  full((1, NSA_WIDTH)), full((1, LANES)), full((1, LANES)),
                  full((NSA_WIDTH, NSA_WIDTH)), full((LANES, LANES))],
        out_specs=[tiled(NSA_WIDTH), pl.BlockSpec((ts, 2 * LANES), row), pl.BlockSpec((ts, LANES), row),
                   pl.BlockSpec((1, ns, NSA_KV_HEADS * V_ROWS, SEL_TILE), lambda b, t: (b, t, 0, 0)),
                   tiled(NSA_KV_HEADS * V_ROWS), tiled(NSA_KV_HEADS * V_ROWS)],
        out_shape=[jax.ShapeDtypeStruct((B, T // KEY_TILE, NSA_WIDTH, KEY_TILE), BF16),
                   jax.ShapeDtypeStruct((N, 2 * LANES), BF16), jax.ShapeDtypeStruct((N, LANES), BF16),
                   jax.ShapeDtypeStruct((B, T // SEL_TILE, NSA_KV_HEADS * V_ROWS, SEL_TILE), BF16),
                   jax.ShapeDtypeStruct((B, T // KEY_TILE, NSA_KV_HEADS * V_ROWS, KEY_TILE), BF16),
                   jax.ShapeDtypeStruct((B, T // KEY_TILE, NSA_KV_HEADS * V_ROWS, KEY_TILE), BF16)],
        compiler_params=_params("arbitrary", "arbitrary"),
        name="nsa_prep",
    )(q, kv, *tables, gq, gs, gw, _head_block_diag(NSA_WIDTH, 1.0 / HEAD_DIM),
      _head_block_diag(LANES, 1.0 / HEAD_DIM))


def _gelu_tanh(x):
    return 0.5 * x * (1.0 + jnp.tanh(0.7978845608028654 * (x + 0.044715 * x * x * x)))


def _nsa_cmp_kernel(x_ref, pos_ref, w1_ref, w2_ref, *rest, is_key):
    if is_key:
        g_ref, c_ref, sd_ref, su_ref, bd_ref, o_ref, xs_ref = rest
    else:
        o_ref, xs_ref = rest
    nch = xs_ref.shape[0]
    S = CMP_STRIDE
    for j in range(S):
        xs_ref[:, j * LANES:(j + 1) * LANES] = x_ref[0, pl.ds(j, nch, stride=S), :]
    xs = xs_ref[...]
    first = _dot((xs + pos_ref[0:1, :]).astype(BF16), w1_ref[0])
    second = _dot((xs + pos_ref[1:2, :]).astype(BF16), w1_ref[1])
    hid = first + pltpu.roll(second, nch - 1, 0)
    out = _dot(_gelu_tanh(hid).astype(BF16), w2_ref[...])
    rows = lax.broadcasted_iota(jnp.int32, out.shape, 0)
    if is_key:
        out = _norm_rope(out, bd_ref[...], g_ref[...], c_ref[...], sd_ref[...], su_ref[...])
        o_ref[0] = jnp.where(rows < nch - 1, out, 0.0).astype(BF16)
    else:
        o_ref[0] = jnp.where(rows < nch - 1, out, 0.0).T.astype(BF16)


def _nsa_cmp(kv3, which, cmp_pos, cmp_w1, cmp_w2, g_k, tables_cmp):
    B, T, _ = kv3.shape
    S = CMP_STRIDE
    nch = T // S
    is_key = which == 0
    eye2 = jnp.eye(NSA_KV_HEADS, dtype=F32)
    w1 = cmp_w1[which].reshape(CMP_BLOCK, HEAD_DIM, CMP_HIDDEN)
    w1 = jnp.einsum('jdh,ge->jgdeh', w1, eye2).reshape(2, S * LANES, NSA_KV_HEADS * CMP_HIDDEN)
    w2 = jnp.einsum('hd,ge->ghed', cmp_w2[which], eye2).reshape(NSA_KV_HEADS * CMP_HIDDEN, LANES)
    pos = jnp.tile(cmp_pos[which].reshape(2, S, 1, HEAD_DIM), (1, 1, NSA_KV_HEADS, 1)).reshape(2, S * LANES)
    full = lambda shape: pl.BlockSpec(shape, lambda b: (0,) * len(shape))
    in_specs = [pl.BlockSpec((1, T, LANES), lambda b: (b, 0, which)), full(pos.shape), full(w1.shape),
                full(w2.shape)]
    args = [kv3, pos, w1.astype(BF16), w2.astype(BF16)]
    if is_key:
        in_specs += [full((1, LANES)), full((nch, LANES)), full((nch, LANES)), full((nch, LANES)),
                     full((LANES, LANES))]
        args += [jnp.tile(g_k, NSA_KV_HEADS).reshape(1, LANES), *tables_cmp,
                 _head_block_diag(LANES, 1.0 / HEAD_DIM)]
        out_spec = pl.BlockSpec((1, nch, LANES), lambda b: (b, 0, 0))
        out_shape = jax.ShapeDtypeStruct((B, nch, LANES), BF16)
    else:
        out_spec = pl.BlockSpec((1, LANES, nch), lambda b: (b, 0, 0))
        out_shape = jax.ShapeDtypeStruct((B, LANES, nch), BF16)
    return pl.pallas_call(
        functools.partial(_nsa_cmp_kernel, is_key=is_key),
        grid=(B,),
        in_specs=in_specs,
        out_specs=out_spec,
        out_shape=out_shape,
        scratch_shapes=[pltpu.VMEM((nch, S * LANES), F32)],
        compiler_params=_params("arbitrary"),
        name="nsa_cmp_k" if is_key else "nsa_cmp_v",
    )(*args)


def _nsa_attn_kernel(qt_ref, kc_ref, vct_ref, ks_ref, vst_ref, vsd_ref, kw_ref, vwt_ref, gt_ref, ov_ref, o_ref,
                     rhs_ref, oc_ref, keep_ref, s0_ref, s1_ref, s2_ref, s3_ref, p0_ref, p1_ref):
    qb = pl.program_id(1)
    G = NSA_KV_HEADS
    R = NSA_GROUP
    QT = Q_TILE
    KT = KEY_TILE
    CG = R * QT
    NQ = G * CG
    D = HEAD_DIM
    t0 = qb * QT
    n_cmp_pad = kc_ref.shape[1]
    n_sel = ov_ref.shape[0]
    cols = lambda g: slice(g * CG, (g + 1) * CG)

    q_cols = []
    for g in range(G):
        q_g = jnp.concatenate([qt_ref[0, 0, (g * R + r) * D:(g * R + r + 1) * D, :] for r in range(R)], axis=1)
        q_cols.append(jnp.concatenate([q_g if gg == g else jnp.zeros_like(q_g) for gg in range(G)], axis=0))
    qpad = jnp.concatenate(q_cols, axis=1)

    tq_row = t0 + (lax.broadcasted_iota(jnp.int32, (1, NQ), 1) & (QT - 1))
    spread = lambda z: jnp.concatenate([z[:, g * QT:(g + 1) * QT] for g in range(G) for _ in range(R)], axis=1)
    tile_all = lambda z: jnp.concatenate([z] * (G * R), axis=1)

    def values_dot(v_of_group, p):
        return jnp.concatenate([_dot(v_of_group(g), p[:, cols(g)]) for g in range(G)], axis=1)

    NV = CMP_VARIANTS
    nq = ks_ref.shape[1] // QT

    def compressed_and_select(n_c, n_b):
        sc = _dot(kc_ref[0, 0:n_c, :], qpad)
        n_i = lax.broadcasted_iota(jnp.int32, (n_c, 1), 0)
        cend = jnp.where(n_i < n_cmp_pad - 1, n_i * CMP_STRIDE + (CMP_BLOCK - 1), jnp.int32(2 ** 30))
        cvalid = cend <= tq_row
        sc = jnp.where(cvalid, sc, NEG_INF)
        mc = jnp.max(sc, axis=0, keepdims=True)
        ec = jnp.where(cvalid, jnp.exp2(sc - mc), 0.0)
        pc = ec / jnp.maximum(jnp.sum(ec, axis=0, keepdims=True), F32_TINY)
        pc_b = pc.astype(BF16)
        oc_ref[...] = values_dot(lambda g: vct_ref[0, g * D:(g + 1) * D, 0:n_c], pc_b)
        sums = []
        for g in range(G):
            acc = pc[:, g * CG:g * CG + QT]
            for r in range(1, R):
                acc = acc + pc[:, g * CG + r * QT:g * CG + (r + 1) * QT]
            sums.append(acc)
        imp = _dot_split_rhs(ov_ref[0:n_b, 0:n_c], jnp.concatenate(sums, axis=1))
        jb = lax.broadcasted_iota(jnp.int32, (n_b, G * QT), 0)
        jf = jb.astype(F32)
        tq_b = t0 + (lax.broadcasted_iota(jnp.int32, (n_b, G * QT), 1) & (QT - 1))
        cur = tq_b >> SEL_SHIFT
        forced = (jb == 0) | (jb == cur) | (jb == cur - 1)
        visible = jb * SEL_BLOCK <= tq_b
        score = jnp.where(visible, jnp.where(forced, FORCE_SCORE, imp), -1.0)
        sel = jnp.zeros((n_b, G * QT), F32)
        for _ in range(min(SEL_TOPK, n_b)):
            mx = jnp.max(score, axis=0, keepdims=True)
            jmin = jnp.min(jnp.where(score == mx, jf, 1e9), axis=0, keepdims=True)
            hit = jf == jmin
            sel = jnp.where(hit, 1.0, sel)
            score = jnp.where(hit, -3e38, score)
        keep_ref[0:n_b, :] = jnp.where(visible, sel, 0.0)
        if n_b < n_sel:
            keep_ref[n_b:n_sel, :] = jnp.zeros((n_sel - n_b, G * QT), F32)

    for v in range(NV):
        @pl.when((qb * NV) // nq == v)
        def _():
            compressed_and_select((v + 1) * n_cmp_pad // NV, (v + 1) * n_sel // NV)

    o_c = oc_ref[...]
    ji = lax.broadcasted_iota(jnp.int32, (n_sel, G * QT), 0)

    ST = SEL_TILE
    bias_all = (keep_ref[...] - 1.0) * (-NEG_INF)
    first_own = t0 // SEL_BLOCK
    vrows = lambda g: slice(g * V_ROWS, (g + 1) * V_ROWS)

    def with_bias_rows(bias):
        rows = spread(bias).astype(BF16)
        if n_sel < LANES:
            rows = jnp.concatenate([rows, jnp.zeros((LANES - n_sel, NQ), BF16)], axis=0)
        return jnp.concatenate([qpad, rows], axis=0)

    rhs_ref[...] = with_bias_rows(jnp.where(ji < first_own, bias_all, NEG_INF))
    n_tiles = (t0 + ST - 1) // ST
    last_tile = ks_ref.shape[1] // ST - 1
    p_bufs = (p0_ref, p1_ref)

    def sel_scores(kt, s_ref):
        k0 = pl.multiple_of(jnp.minimum(kt, last_tile) * ST, ST)
        s_ref[...] = _dot(ks_ref[0, pl.ds(k0, ST), :], rhs_ref[...])

    def sel_values(kt, slot, acc, alpha):
        kt = jnp.clip(kt, 0, last_tile)
        return acc * alpha + values_dot(lambda g: vst_ref[0, kt, vrows(g), :], p_bufs[slot][...])

    def sel_softmax(s_ref, slot, m):
        s = s_ref[...]
        m_new = jnp.maximum(m, jnp.max(s, axis=0, keepdims=True))
        p_bufs[slot][...] = jnp.exp2(s - m_new).astype(BF16)
        return m_new, jnp.exp2(m - m_new)

    def sel_pair(a, carry, s_now, s_next):
        m, acc, alpha0, alpha1 = carry
        acc = sel_values(a - 2, 0, acc, alpha0)
        acc = sel_values(a - 1, 1, acc, alpha1)
        sel_scores(a + 2, s_next[0])
        sel_scores(a + 3, s_next[1])
        m, alpha0 = sel_softmax(s_now[0], 0, m)
        m, alpha1 = sel_softmax(s_now[1], 1, m)
        return m, acc, alpha0, alpha1

    bufs_a, bufs_b = (s0_ref, s1_ref), (s2_ref, s3_ref)
    sel_scores(0, s0_ref)
    sel_scores(1, s1_ref)
    p0_ref[...] = jnp.zeros_like(p0_ref)
    p1_ref[...] = jnp.zeros_like(p1_ref)
    own = _dot(ks_ref[0, pl.ds(pl.multiple_of(t0, QT), QT), :], with_bias_rows(bias_all))

    n_wt = (WINDOW + QT) // KT
    k0w = pl.multiple_of(jnp.maximum(t0 - WINDOW, 0), KT)
    kt_w = k0w // KT
    keys_w = kw_ref[0, pl.ds(k0w, WINDOW + QT), :]
    dw = (t0 + lax.broadcasted_iota(jnp.int32, (WINDOW + QT, QT), 1)
          - (k0w + lax.broadcasted_iota(jnp.int32, (WINDOW + QT, QT), 0)))
    sw = _dot(keys_w, qpad) + tile_all(jnp.where(dw >= 0, jnp.where(dw < WINDOW, 0.0, NEG_INF), NEG_INF))
    pw = jnp.exp2(sw - jnp.max(sw, axis=0, keepdims=True)).astype(BF16)
    acc_w = values_dot(lambda g: vwt_ref[0, kt_w, vrows(g), :], pw[0:KT])
    for j in range(1, n_wt):
        acc_w = acc_w + values_dot(lambda g: vwt_ref[0, kt_w + j, vrows(g), :], pw[j * KT:(j + 1) * KT])

    n_pairs = (n_tiles + 1) // 2
    one = jnp.ones((1, NQ), F32)
    m_s, acc_s, alpha0, alpha1 = lax.fori_loop(
        0, n_pairs,
        lambda j, carry: lax.cond(j % 2 == 0,
                                  lambda c: sel_pair(2 * j, c, bufs_a, bufs_b),
                                  lambda c: sel_pair(2 * j, c, bufs_b, bufs_a), carry),
        (jnp.full((1, NQ), NEG_INF, F32), jnp.zeros((V_ROWS, NQ), F32), one, one))
    acc_s = sel_values(2 * n_pairs - 2, 0, acc_s, alpha0)
    acc_s = sel_values(2 * n_pairs - 1, 1, acc_s, alpha1)
    seen = lax.broadcasted_iota(jnp.int32, (QT, QT), 0) <= lax.broadcasted_iota(jnp.int32, (QT, QT), 1)
    own = jnp.where(tile_all(seen), own, NEG_INF)
    m_new = jnp.maximum(m_s, jnp.max(own, axis=0, keepdims=True))
    acc_s = acc_s * jnp.exp2(m_s - m_new) + values_dot(lambda g: vsd_ref[0, qb, vrows(g), :],
                                                       jnp.exp2(own - m_new).astype(BF16))

    gates = _sigmoid(gt_ref[0])
    grow = lambda j: jnp.concatenate([gates[g, j, r:r + 1, :] for g in range(G) for r in range(R)], axis=1)
    o = (grow(0) * o_c + grow(1) * (acc_s[0:D] / acc_s[D:D + 1])
         + grow(2) * (acc_w[0:D] / acc_w[D:D + 1]))
    halves = []
    for h in range(G * R // 2):
        pair = jnp.concatenate([o[:, (2 * h) * QT:(2 * h + 1) * QT],
                                o[:, (2 * h + 1) * QT:(2 * h + 2) * QT]], axis=0)
        halves.append(pair.T)
    o_ref[...] = jnp.concatenate(halves, axis=1)


def _nsa_attn(qt, kcmp, vct, ks3, vst, vsd, kw3, vwt, gt, ov_t, B, T):
    G, R = NSA_KV_HEADS, NSA_GROUP
    nq = T // Q_TILE
    nk = T // KEY_TILE
    nch = kcmp.shape[1]
    n_sel = ov_t.shape[0]
    NQ = G * R * Q_TILE
    assert (T // SEL_TILE) % 2 == 0 and n_sel <= LANES and Q_TILE == KEY_TILE
    assert nq % CMP_VARIANTS == 0 and n_sel % (8 * CMP_VARIANTS) == 0 and nch % (8 * CMP_VARIANTS) == 0
    return pl.pallas_call(
        _nsa_attn_kernel,
        grid=(B, nq),
        in_specs=[pl.BlockSpec((1, 1, NSA_WIDTH, Q_TILE), lambda b, q: (b, q, 0, 0)),
                  pl.BlockSpec((1, nch, LANES), lambda b, q: (b, 0, 0)),
                  pl.BlockSpec((1, G * HEAD_DIM, nch), lambda b, q: (b, 0, 0)),
                  pl.BlockSpec((1, T, 2 * LANES), lambda b, q: (b, 0, 0)),
                  pl.BlockSpec((1, T // SEL_TILE, G * V_ROWS, SEL_TILE), lambda b, q: (b, 0, 0, 0)),
                  pl.BlockSpec((1, nk, G * V_ROWS, KEY_TILE), lambda b, q: (b, 0, 0, 0)),
                  pl.BlockSpec((1, T, LANES), lambda b, q: (b, 0, 0)),
                  pl.BlockSpec((1, nk, G * V_ROWS, KEY_TILE), lambda b, q: (b, 0, 0, 0)),
                  pl.BlockSpec((1, G, 3, R, Q_TILE), lambda b, q: (b, 0, 0, 0, q)),
                  pl.BlockSpec((n_sel, nch), lambda b, q: (0, 0))],
        out_specs=pl.BlockSpec((Q_TILE, NSA_WIDTH), lambda b, q: (b * nq + q, 0)),
        out_shape=jax.ShapeDtypeStruct((B * T, NSA_WIDTH), F32),
        scratch_shapes=[pltpu.VMEM((2 * LANES, NQ), BF16), pltpu.VMEM((HEAD_DIM, NQ), F32),
                        pltpu.VMEM((n_sel, G * Q_TILE), F32),
                        *[pltpu.VMEM((SEL_TILE, NQ), F32)] * 4,
                        *[pltpu.VMEM((SEL_TILE, NQ), BF16)] * 2],
        compiler_params=_params("arbitrary", "arbitrary"),
        name="nsa_attn",
    )(qt, kcmp, vct, ks3, vst, vsd, kw3, vwt, gt, ov_t)


def _first_index_of(vals, target):
    idx = jnp.full_like(target, float(len(vals) - 1))
    for i in range(len(vals) - 2, -1, -1):
        idx = jnp.where(vals[i] == target, float(i), idx)
    return idx


def _pick(vals, idx):
    out = vals[-1]
    for i in range(len(vals) - 2, -1, -1):
        out = jnp.where(idx == float(i), vals[i], out)
    return out


def _route_rows(score, bias):
    E, G, P = N_EXPERTS, N_GROUPS, EXPERTS_PER_GROUP
    sel = score + bias
    s = [sel[e:e + 1, :] for e in range(E)]
    raw = [score[e:e + 1, :] for e in range(E)]
    grp = []
    for gi in range(G):
        a = s[gi * P:(gi + 1) * P]
        best = None
        for i in range(P):
            for j in range(i + 1, P):
                pair = a[i] + a[j]
                best = pair if best is None else jnp.maximum(best, pair)
        grp.append(best)
    gmax = functools.reduce(jnp.maximum, grp)
    g_star = _first_index_of(grp, gmax)
    v = [_pick([s[gi * P + i] for gi in range(G)], g_star) for i in range(P)]
    w = [_pick([raw[gi * P + i] for gi in range(G)], g_star) for i in range(P)]
    i1 = _first_index_of(v, functools.reduce(jnp.maximum, v))
    v2 = [jnp.where(i1 == float(i), -jnp.inf, v[i]) for i in range(P)]
    i2 = _first_index_of(v2, functools.reduce(jnp.maximum, v2))
    w1, w2 = _pick(w, i1), _pick(w, i2)
    tot = w1 + w2
    zero = jnp.zeros_like(tot)
    e1, e2 = g_star * P + i1, g_star * P + i2
    n = score.shape[1]
    eidx = lax.broadcasted_iota(jnp.int32, (E, n), 0).astype(F32)
    oh1, oh2 = jnp.where(eidx == e1, 1.0, 0.0), jnp.where(eidx == e2, 1.0, 0.0)
    earlier = jnp.where(lax.broadcasted_iota(jnp.int32, (n, n), 0) < lax.broadcasted_iota(jnp.int32, (n, n), 1),
                        1.0, 0.0).astype(BF16)
    cnt = _dot(jnp.concatenate([oh1, oh2], axis=0).astype(BF16), earlier)
    rank1 = jnp.sum(oh1 * cnt[0:E], axis=0, keepdims=True)
    rank2 = jnp.sum(oh2 * cnt[E:2 * E], axis=0, keepdims=True)
    lane = lax.broadcasted_iota(jnp.int32, (E, LANES), 1)
    totals = jnp.where(lane == 0, jnp.sum(oh1, axis=1, keepdims=True),
                       jnp.where(lane == 1, jnp.sum(oh2, axis=1, keepdims=True), 0.0))
    return jnp.concatenate([e1, e2, w1 / tot, w2 / tot, rank1, rank2, zero, zero], axis=0), totals


def _merge_kernel(ys_ref, g_ref, bonus_ref, gng_ref, gnb_ref, bd_ref, yb_ref, pm_ref, x_ref, mod_ref,
                  ng_ref, wa_ref, wb_ref, wo_ref, rw_ref, rb_ref, o_x, o_h, o_route, o_tot):
    m = mod_ref[pl.program_id(0)]
    bd = bd_ref[...]
    y = ys_ref[...]
    mean = _dot_split_lhs(y, bd)
    yc = y - mean
    var = _dot_split_lhs(yc * yc, bd)
    ya = (yc * lax.rsqrt(var + RWKV_GN_EPS) * gng_ref[...] + gnb_ref[...] + bonus_ref[...]) * g_ref[...]
    pm = pm_ref[...]
    D = x_ref.shape[1]
    mix = (_sigmoid(pm[:, 0:D]) * _dot(ya.astype(BF16), wa_ref[...])
           + _sigmoid(pm[:, D:2 * D]) * _dot(yb_ref[...].astype(BF16), wb_ref[...]))
    x = x_ref[...] + m[2:3] * _dot(mix.astype(BF16), wo_ref[...])
    o_x[...] = x
    ms = jnp.mean(x * x, axis=-1, keepdims=True)
    h = x * lax.rsqrt(ms + NORM_EPS) * ng_ref[...]
    h = h * (1.0 + m[4:5]) + m[3:4]
    o_h[...] = h
    score = _sigmoid(_dot_3pass(h, rw_ref).T[0:N_EXPERTS, :])
    o_route[...], o_tot[...] = _route_rows(score, rb_ref[...])


def _merge(ys, g, bonus, gn_g, gn_b, yb, pm, x2, mod, ng, wa, wb, wo, router_w, router_b, B, T, tm=256):
    N, D = x2.shape
    W = RWKV_WIDTH
    nt = T // tm
    row = lambda b, t: (b * nt + t, 0)
    full = lambda shape: pl.BlockSpec(shape, lambda b, t: (0,) * len(shape))
    return pl.pallas_call(
        _merge_kernel,
        grid=(B, nt),
        in_specs=[pl.BlockSpec((tm, W), row), pl.BlockSpec((tm, W), row), pl.BlockSpec((tm, W), row),
                  full((1, W)), full((1, W)), full((W, W)),
                  pl.BlockSpec((tm, NSA_WIDTH), row), pl.BlockSpec((tm, 2 * D), row),
                  pl.BlockSpec((tm, D), row), full((B, 6, D)), full((1, D)),
                  full((W, D)), full((NSA_WIDTH, D)), full((D, D)), full((2, D, LANES)),
                  full((N_EXPERTS, 1))],
        out_specs=[pl.BlockSpec((tm, D), row), pl.BlockSpec((tm, D), row),
                   pl.BlockSpec((8, tm), lambda b, t: (0, b * nt + t)),
                   pl.BlockSpec((N_EXPERTS, LANES), lambda b, t: (b * nt + t, 0))],
        out_shape=[jax.ShapeDtypeStruct((N, D), F32), jax.ShapeDtypeStruct((N, D), F32),
                   jax.ShapeDtypeStruct((8, N), F32), jax.ShapeDtypeStruct((N // tm * N_EXPERTS, LANES), F32)],
        compiler_params=_params("arbitrary", "arbitrary"),
        name="merge_out",
    )(ys, g, bonus, gn_g.reshape(1, W), gn_b.reshape(1, W), _head_block_diag(W, 1.0 / HEAD_DIM),
      yb, pm, x2, mod, ng, wa, wb, wo,
      _hi_lo(jnp.zeros((D, LANES), F32).at[:, :N_EXPERTS].set(router_w)), router_b.reshape(N_EXPERTS, 1))


def _route(route, totals, N):
    wts = route[TOP_K:2 * TOP_K].T
    NK = N * TOP_K
    E = N_EXPERTS
    n_tiles = totals.shape[0] // E
    expert = route[0:TOP_K].astype(jnp.int32)
    rank = route[2 * TOP_K:3 * TOP_K].astype(jnp.int32)
    per = totals.reshape(n_tiles, E, LANES)[:, :, 0:TOP_K].astype(jnp.int32).transpose(0, 2, 1)
    per = per.reshape(n_tiles * TOP_K, E)
    csum = jnp.cumsum(per, axis=0)
    counts = csum[-1]
    padded = (counts + MOE_BLOCK - 1) // MOE_BLOCK * MOE_BLOCK
    pad_end = jnp.cumsum(padded)
    pad_start = pad_end - padded
    first = (pad_start[None, :] + csum - per).reshape(n_tiles, TOP_K, E).transpose(1, 0, 2)
    first = jnp.repeat(first, N // n_tiles, axis=1)
    mine = expert[:, :, None] == jnp.arange(E, dtype=jnp.int32)[None, None, :]
    dest = (jnp.sum(jnp.where(mine, first, 0), axis=-1) + rank).reshape(-1)
    n_blk = -(-NK // MOE_BLOCK) + N_EXPERTS
    blk_start = jnp.arange(n_blk, dtype=jnp.int32) * MOE_BLOCK
    blk_expert = jnp.sum((pad_end[None, :] <= blk_start[:, None]).astype(jnp.int32), axis=1)
    blk_expert = jnp.clip(blk_expert, 0, N_EXPERTS - 1)
    blk_valid = jnp.clip((pad_start + counts)[blk_expert] - blk_start, 0, MOE_BLOCK).astype(jnp.int32)
    dest = jnp.pad(dest.astype(jnp.int32).reshape(NK // SC_WINDOW, SC_WINDOW), ((0, 0), (0, LANES - SC_WINDOW)))
    return wts, dest, blk_expert, blk_valid, n_blk


SC_WINDOW = 32


def _sc_mesh():
    return plsc.VectorSubcoreMesh(core_axis_name="c", subcore_axis_name="s")


def _sc_dispatch(h, dest, n_slots):
    N, D = h.shape
    W = SC_WINDOW
    nw = N // W

    @pl.kernel(out_type=jax.ShapeDtypeStruct((n_slots, D), h.dtype), mesh=_sc_mesh(), scratch_types=[])
    def dispatch(h_hbm, i_hbm, o_hbm):
        def body(x_vmem, i_vmem):
            pltpu.sync_copy(x_vmem, o_hbm.at[i_vmem.at[0, pl.ds(0, W)]])

        pltpu.emit_pipeline(
            body, grid=(TOP_K, nw),
            in_specs=[pl.BlockSpec((W, D), lambda k, i: (i, 0)),
                      pl.BlockSpec((1, LANES), lambda k, i: (k * nw + i, 0))],
            out_specs=[], core_axis_name=("c", "s"),
            dimension_semantics=(pltpu.PARALLEL, pltpu.PARALLEL))(h_hbm, i_hbm)

    return dispatch(h, dest)


def _sc_collect(ys, dest):
    W = SC_WINDOW
    NK = dest.shape[0] * W
    D = ys.shape[1]
    half = NK // TOP_K // W

    @pl.kernel(out_type=jax.ShapeDtypeStruct((NK, D), ys.dtype), mesh=_sc_mesh(), scratch_types=[])
    def collect(y_hbm, i_hbm, o_hbm):
        def body(i_vmem, o_vmem):
            pltpu.sync_copy(y_hbm.at[i_vmem.at[0, pl.ds(0, W)]], o_vmem)

        pltpu.emit_pipeline(
            body, grid=(TOP_K, half),
            in_specs=[pl.BlockSpec((1, LANES), lambda k, i: (k * half + i, 0))],
            out_specs=[pl.BlockSpec((W, D), lambda k, i: (k * half + i, 0))],
            core_axis_name=("c", "s"),
            dimension_semantics=(pltpu.PARALLEL, pltpu.PARALLEL))(i_hbm, o_hbm)

    return collect(ys, dest)


def _moe_dense_kernel(be_ref, nv_ref, x_ref, wg_ref, wu_ref, wd_ref, o_ref, wg_b, wu_b, wd_b):
    i = pl.program_id(0)
    nv = nv_ref[i]

    @pl.when((i == 0) | (be_ref[i] != be_ref[jnp.maximum(i - 1, 0)]))
    def _():
        wg_b[...] = wg_ref[0, 0].astype(BF16)
        wu_b[...] = wu_ref[0, 0].astype(BF16)
        wd_b[...] = wd_ref[0, 0].astype(BF16)

    @pl.when(nv > 0)
    def _():
        x = x_ref[...].astype(BF16)
        gate = _dot(x, wg_b[...])
        up = _dot(x, wu_b[...])
        o_ref[...] = _dot((gate * _sigmoid(gate) * up).astype(BF16), wd_b[...])

    @pl.when(nv == 0)
    def _():
        o_ref[...] = jnp.zeros_like(o_ref)


def _moe_dense(xs, blk_expert, blk_valid, n_blk, layer, wg, wu, wd):
    P, D = xs.shape
    DE = wg.shape[3]
    wmap = lambda i, be, nv: (layer, be[i], 0, 0)
    grid_spec = pltpu.PrefetchScalarGridSpec(
        num_scalar_prefetch=2,
        grid=(n_blk,),
        in_specs=[pl.BlockSpec((MOE_BLOCK, D), lambda i, be, nv: (i, 0)), pl.BlockSpec((1, 1, D, DE), wmap),
                  pl.BlockSpec((1, 1, D, DE), wmap), pl.BlockSpec((1, 1, DE, D), wmap)],
        out_specs=pl.BlockSpec((MOE_BLOCK, D), lambda i, be, nv: (i, 0)),
        scratch_shapes=[pltpu.VMEM((D, DE), BF16), pltpu.VMEM((D, DE), BF16), pltpu.VMEM((DE, D), BF16)],
    )
    return pl.pallas_call(
        _moe_dense_kernel,
        grid_spec=grid_spec,
        out_shape=jax.ShapeDtypeStruct((P, D), F32),
        compiler_params=_params("arbitrary"),
        name="moe_experts",
    )(blk_expert, blk_valid, xs, wg, wu, wd)


def _final_kernel(x_ref, y0_ref, y1_ref, w_ref, mod_ref, o_ref):
    o_ref[...] = _moe_residual(x_ref, y0_ref, y1_ref, w_ref, mod_ref[pl.program_id(0)])


def _final(x2, ybuf, wts, mod, B, T, tm=512):
    N, D = x2.shape
    nt = T // tm
    row = lambda b, t: (b * nt + t, 0)
    return pl.pallas_call(
        _final_kernel,
        grid=(B, nt),
        in_specs=[pl.BlockSpec((tm, D), row), pl.BlockSpec((tm, D), row),
                  pl.BlockSpec((tm, D), lambda b, t: (N // tm + b * nt + t, 0)),
                  pl.BlockSpec((tm, TOP_K), row), pl.BlockSpec((B, 6, D), lambda b, t: (0, 0, 0))],
        out_specs=pl.BlockSpec((tm, D), row),
        out_shape=jax.ShapeDtypeStruct((N, D), F32),
        compiler_params=_params("arbitrary", "arbitrary"),
        name="moe_combine",
    )(x2, ybuf, ybuf, wts, mod)


def _overlap_t(n_sel, n_cmp_pad):
    ci = jnp.arange(n_cmp_pad)[None, :] * CMP_STRIDE
    sj = jnp.arange(n_sel)[:, None] * SEL_BLOCK
    ov = (ci <= sj + SEL_BLOCK - 1) & (ci + CMP_BLOCK - 1 >= sj) & (jnp.arange(n_cmp_pad)[None, :] < n_cmp_pad - 1)
    return ov.astype(BF16)


def kernel(x, c, w_ada, b_ada, norm_g, w_in, b_in, rwkv_mu, rwkv_w0, rwkv_w2, rwkv_a0, rwkv_a2, rwkv_g2,
           rwkv_k_k, rwkv_k_a, rwkv_r_k, rwkv_gn_g, rwkv_gn_b, qk_norm_g, cmp_pos, cmp_w1, cmp_w2,
           w_up_rwkv, w_up_nsa, w_out, router_w, router_b, exp_w_gate, exp_w_up, exp_w_down):
    B, T, D = x.shape
    L = w_ada.shape[0]
    N = B * T
    mods = _ada(c, w_ada, b_ada)
    tables = _rope_tables(jnp.arange(T, dtype=jnp.int32))
    nch = T // CMP_STRIDE
    tables_cmp = _rope_tables(jnp.arange(nch, dtype=jnp.int32) * CMP_STRIDE + CMP_BLOCK - 1)
    ov_t = _overlap_t(T // SEL_BLOCK, nch)
    n_gate = NSA_GATE_COLS
    x2 = x.reshape(N, D)
    pending_moe = None
    for l in range(L):
        g0 = _SEG_KV[1] + n_gate
        w_pad = jnp.concatenate([w_in[l][:, :g0], jnp.zeros((D, GATE_PAD - n_gate), F32), w_in[l][:, g0:]],
                                axis=1).astype(BF16)
        b_pad = jnp.concatenate([b_in[l][:g0], jnp.zeros((GATE_PAD - n_gate,), F32), b_in[l][g0:]]).reshape(1, -1)
        outs = _inproj(x2, pending_moe, mods[l], norm_g[l, 0].reshape(1, D), w_pad, b_pad, B, T)
        if pending_moe is not None:
            x2, outs = outs[0], outs[1:]
        p_rw, p_q, p_kv, p_gate, p_merge = outs
        r, k, v, al, bb, ld, g, bonus = _rwkv_pre(p_rw, rwkv_mu[l], rwkv_w0[l], rwkv_w2[l], rwkv_a0[l],
                                                  rwkv_a2[l], rwkv_g2[l], rwkv_k_k[l], rwkv_k_a[l],
                                                  rwkv_r_k[l], B, T)
        ys = _rwkv_scan(r, k, v, al, bb, ld, B, T)
        qt, ks, kw, vst, vsd, vwt = _nsa_prep(p_q, p_kv, tables, qk_norm_g[l], B, T)
        kv3 = p_kv.reshape(B, T, KV_COLS)
        kcmp = _nsa_cmp(kv3, 0, cmp_pos[l], cmp_w1[l], cmp_w2[l], qk_norm_g[l, 1], tables_cmp)
        vct = _nsa_cmp(kv3, 1, cmp_pos[l], cmp_w1[l], cmp_w2[l], None, None)
        gt = p_gate[:, :n_gate].reshape(B, T, NSA_KV_HEADS, NSA_GROUP, 3).transpose(0, 2, 4, 3, 1)
        yb = _nsa_attn(qt, kcmp, vct, ks.reshape(B, T, 2 * LANES), vst, vsd, kw.reshape(B, T, LANES), vwt, gt, ov_t,
                       B, T)
        x2, h2, route, totals = _merge(ys, g, bonus, rwkv_gn_g[l], rwkv_gn_b[l], yb, p_merge, x2, mods[l],
                               norm_g[l, 1].reshape(1, D), w_up_rwkv[l].astype(BF16),
                               w_up_nsa[l].astype(BF16), w_out[l].astype(BF16), router_w, router_b, B, T)
        wts, dest, blk_expert, blk_valid, n_blk = _route(route, totals, N)
        xs = _sc_dispatch(h2, dest, n_blk * MOE_BLOCK)
        ys = _moe_dense(xs, blk_expert, blk_valid, n_blk, l, exp_w_gate, exp_w_up, exp_w_down)
        ybuf = _sc_collect(ys, dest)
        pending_moe = (ybuf, wts, mods[l])
    return _final(x2, *pending_moe, B, T).reshape(B, T, D)
```

```python
import functools
import math

import jax
import jax.numpy as jnp
from jax import lax
from jax.experimental import pallas as pl
from jax.experimental.pallas import tpu as pltpu
from jax.experimental.pallas import tpu_sc as plsc

F32 = jnp.float32
BF16 = jnp.bfloat16
HI = lax.Precision.HIGHEST

D_MODEL = 1024
RWKV_HEADS = 8
HEAD_DIM = 64
RWKV_WIDTH = RWKV_HEADS * HEAD_DIM
DECAY_LORA = 64
ICLR_LORA = 64
GATE_LORA = 128
RWKV_GN_EPS = 64e-5
RWKV_COLS = 3 * RWKV_WIDTH + DECAY_LORA + ICLR_LORA + GATE_LORA

NSA_Q_HEADS = 8
NSA_KV_HEADS = 2
NSA_GROUP = NSA_Q_HEADS // NSA_KV_HEADS
NSA_WIDTH = NSA_Q_HEADS * HEAD_DIM
CMP_STRIDE = 16
CMP_BLOCK = 2 * CMP_STRIDE
CMP_HIDDEN = 256
SEL_BLOCK = 64
SEL_SHIFT = 6
SEL_TOPK = 16
WINDOW = 512
FORCE_SCORE = 1e4
NEG_INF = -1e30
ROPE_THETA = 500000.0
ROPE_DIM = HEAD_DIM // 4
KV_COLS = 6 * NSA_KV_HEADS * HEAD_DIM
NSA_GATE_COLS = 3 * NSA_Q_HEADS
GATE_PAD = 128

N_EXPERTS = 16
N_GROUPS = 4
EXPERTS_PER_GROUP = N_EXPERTS // N_GROUPS
TOP_K = 2
D_EXPERT = 512
MOE_BLOCK = 256
NORM_EPS = 1e-6

LANES = 128
CHUNK = 64
KEY_TILE = 128
SEL_TILE = 512
CMP_VARIANTS = 4
V_ROWS = 80
Q_SCALE = HEAD_DIM ** -0.5 * math.log2(math.e)
Q_TILE = 128
F32_TINY = float(jnp.finfo(jnp.float32).tiny)

_SEG_RW = (0, RWKV_COLS)
_SEG_Q = (_SEG_RW[1], _SEG_RW[1] + NSA_WIDTH)
_SEG_KV = (_SEG_Q[1], _SEG_Q[1] + KV_COLS)
_SEG_GATE = (_SEG_KV[1], _SEG_KV[1] + GATE_PAD)
_SEG_MERGE = (_SEG_GATE[1], _SEG_GATE[1] + 2 * D_MODEL)
IN_COLS_PAD = _SEG_MERGE[1]

_VMEM_LIMIT = 56 * 1024 * 1024


def _dot(a, b, precision=None):
    return jnp.dot(a, b, preferred_element_type=F32, precision=precision)


def _dot_tb(a, b, precision=None):
    return lax.dot_general(a, b, (((1,), (1,)), ((), ())), preferred_element_type=F32,
                           precision=precision)


def _dot_ta(a, b, precision=None):
    return lax.dot_general(a, b, (((0,), (0,)), ((), ())), preferred_element_type=F32,
                           precision=precision)


def _split_bf16(x, terms):
    parts = []
    for _ in range(terms - 1):
        parts.append(x.astype(BF16))
        x = x - parts[-1].astype(F32)
    parts.append(x.astype(BF16))
    return parts


def _dot_split_lhs(x, w_bf, terms=2):
    return functools.reduce(jnp.add, [_dot(p, w_bf) for p in _split_bf16(x, terms)])


def _dot_split_rhs(w_bf, x, terms=2):
    return functools.reduce(jnp.add, [_dot(w_bf, p) for p in _split_bf16(x, terms)])


def _dot_3pass(x, w_hl_ref):
    x_hi, x_lo = _split_bf16(x, 2)
    w_hi = w_hl_ref[0]
    return _dot(x_hi, w_hi) + _dot(x_lo, w_hi) + _dot(x_hi, w_hl_ref[1])


def _hi_lo(w):
    hi = w.astype(BF16)
    return jnp.stack([hi, (w - hi.astype(F32)).astype(BF16)])


def _params(*sem):
    return pltpu.CompilerParams(dimension_semantics=sem, vmem_limit_bytes=_VMEM_LIMIT)


def _sigmoid(x):
    return 1.0 / (1.0 + jnp.exp(-x))


def _ada_kernel(c_ref, w_ref, b_ref, o_ref):
    c = c_ref[...]
    s = c * _sigmoid(c)
    o_ref[0] = _dot(s, w_ref[0], HI) + b_ref[0]


def _ada(c, w_ada, b_ada):
    L, D, D6 = w_ada.shape
    B = c.shape[0]
    rows = 8
    cp = jnp.zeros((rows, D), F32).at[:B].set(c)
    tn = 1536
    out = pl.pallas_call(
        _ada_kernel,
        grid=(L, D6 // tn),
        in_specs=[pl.BlockSpec((rows, D), lambda l, j: (0, 0)),
                  pl.BlockSpec((1, D, tn), lambda l, j: (l, 0, j)),
                  pl.BlockSpec((1, 1, tn), lambda l, j: (l, 0, j))],
        out_specs=pl.BlockSpec((1, rows, tn), lambda l, j: (l, 0, j)),
        out_shape=jax.ShapeDtypeStruct((L, rows, D6), F32),
        compiler_params=_params("arbitrary", "arbitrary"),
        name="ada_mod",
    )(cp, w_ada, b_ada.reshape(L, 1, D6))
    return out[:, :B].reshape(L, B, 6, D)


def _moe_residual(x_ref, y0_ref, y1_ref, w_ref, m_prev):
    w = w_ref[...]
    return x_ref[...] + m_prev[5:6] * (w[:, 0:1] * y0_ref[...] + w[:, 1:2] * y1_ref[...])


def _inproj_kernel(*refs, after_moe):
    if after_moe:
        x_ref, y0_ref, y1_ref, wts_ref, modp_ref, mod_ref, g_ref, w_ref, b_ref, o_x = refs[:10]
        x = _moe_residual(x_ref, y0_ref, y1_ref, wts_ref, modp_ref[pl.program_id(0)])
        o_x[...] = x
    else:
        x_ref, mod_ref, g_ref, w_ref, b_ref = refs[:5]
        x = x_ref[...]
    o_rw, o_q, o_kv, o_gate, o_merge = refs[-5:]
    m = mod_ref[pl.program_id(0)]
    ms = jnp.mean(x * x, axis=-1, keepdims=True)
    h = x * lax.rsqrt(ms + NORM_EPS) * g_ref[...]
    h = h * (1.0 + m[1:2]) + m[0:1]
    hb = h.astype(BF16)
    for o, (a, e) in ((o_rw, _SEG_RW), (o_q, _SEG_Q), (o_kv, _SEG_KV), (o_gate, _SEG_GATE),
                      (o_merge, _SEG_MERGE)):
        o[...] = _dot(hb, w_ref[:, a:e]) + b_ref[:, a:e]


def _inproj(x2, pending_moe, mod, g, w_pad, b_pad, B, T, tm=512):
    N, D = x2.shape
    nt = T // tm
    row = lambda b, t: (b * nt + t, 0)
    mods_spec = pl.BlockSpec((B, 6, D), lambda b, t: (0, 0, 0))
    widths = [e - a for a, e in (_SEG_RW, _SEG_Q, _SEG_KV, _SEG_GATE, _SEG_MERGE)]
    in_specs = [pl.BlockSpec((tm, D), row)]
    args = [x2]
    if pending_moe is not None:
        ybuf, wts, mod_prev = pending_moe
        in_specs += [pl.BlockSpec((tm, D), row), pl.BlockSpec((tm, D), lambda b, t: (N // tm + b * nt + t, 0)),
                     pl.BlockSpec((tm, TOP_K), row), mods_spec]
        args += [ybuf, ybuf, wts, mod_prev]
        widths = [D] + widths
    in_specs += [mods_spec, pl.BlockSpec((1, D), lambda b, t: (0, 0)),
                 pl.BlockSpec((D, IN_COLS_PAD), lambda b, t: (0, 0)),
                 pl.BlockSpec((1, IN_COLS_PAD), lambda b, t: (0, 0))]
    return pl.pallas_call(
        functools.partial(_inproj_kernel, after_moe=pending_moe is not None),
        grid=(B, nt),
        in_specs=in_specs,
        out_specs=[pl.BlockSpec((tm, w), row) for w in widths],
        out_shape=[jax.ShapeDtypeStruct((N, w), F32) for w in widths],
        compiler_params=_params("arbitrary", "arbitrary"),
        name="in_proj",
    )(*args, mod, g, w_pad, b_pad)


def _rwkv_pre_kernel(p_ref, mu_ref, w0_ref, w2_ref, a0_ref, a2_ref, g2_ref, kk_ref, ka_ref, rk_ref,
                     bd_ref, o_r, o_k, o_v, o_al, o_b, o_ld, o_g, o_bonus, carry_ref):
    W = RWKV_WIDTH

    @pl.when(pl.program_id(1) == 0)
    def _():
        carry_ref[...] = jnp.zeros_like(carry_ref)

    p = p_ref[...]
    ts = p.shape[0]
    rows = lax.broadcasted_iota(jnp.int32, p.shape, 0)
    shifted = jnp.where(rows == 0, carry_ref[0:1, :], pltpu.roll(p, 1, 0))
    carry_ref[0:1, :] = p[ts - 1:ts, :]
    pm = p + (shifted - p) * mu_ref[...]
    r = pm[:, 0:W]
    k = pm[:, W:2 * W]
    v = pm[:, 2 * W:3 * W]
    wa = pm[:, 3 * W:3 * W + DECAY_LORA + ICLR_LORA]
    gl = pm[:, 3 * W + DECAY_LORA + ICLR_LORA:]
    xw = w0_ref[...] + _dot_3pass(jnp.tanh(wa), w2_ref)
    ld = -math.exp(-0.5) * _sigmoid(xw)
    a = _sigmoid(a0_ref[...] + _dot_3pass(wa, a2_ref))
    g = _dot_3pass(_sigmoid(gl), g2_ref)
    bd = bd_ref[...]
    kk = k * kk_ref[...]
    nrm = jnp.sqrt(_head_sums(kk * kk, bd))
    kk = kk / jnp.maximum(nrm, 1e-12)
    k2 = k * (1.0 + (a - 1.0) * ka_ref[...])
    bonus = _head_sums(r * k2 * rk_ref[...], bd) * v
    o_r[...] = r
    o_k[...] = k2
    o_v[...] = v
    o_al[...] = kk
    o_b[...] = -kk * a
    o_ld[...] = ld
    o_g[...] = g
    o_bonus[...] = bonus


HEAD_SUM_WIDTH = 256


def _head_sums(x, bd):
    w = bd.shape[0]
    parts = [_dot_split_lhs(x[:, j:j + w], bd) for j in range(0, x.shape[1], w)]
    return parts[0] if len(parts) == 1 else jnp.concatenate(parts, axis=1)


def _head_block_diag(width, scale=1.0):
    i = jnp.arange(width) // HEAD_DIM
    return ((i[:, None] == i[None, :]).astype(F32) * scale).astype(BF16)


def _rwkv_pre(p_rw, mu, w0, w2, a0, a2, g2, k_k, k_a, r_k, B, T, ts=512):
    N = p_rw.shape[0]
    W = RWKV_WIDTH
    nt = T // ts
    row = lambda b, t: (b * nt + t, 0)
    zl = jnp.zeros((DECAY_LORA, W), F32)
    w2p = jnp.concatenate([w2, zl], axis=0)
    a2p = jnp.concatenate([zl, a2], axis=0)
    full = lambda shape: pl.BlockSpec(shape, lambda b, t: (0,) * len(shape))
    vec = lambda z: z.reshape(1, -1)
    return pl.pallas_call(
        _rwkv_pre_kernel,
        grid=(B, nt),
        in_specs=[pl.BlockSpec((ts, RWKV_COLS), row), full((1, RWKV_COLS)), full((1, W)),
                  full((2, 2 * DECAY_LORA, W)), full((1, W)), full((2, 2 * DECAY_LORA, W)),
                  full((2, GATE_LORA, W)), full((1, W)), full((1, W)), full((1, W)),
                  full((HEAD_SUM_WIDTH, HEAD_SUM_WIDTH))],
        out_specs=[pl.BlockSpec((ts, W), row)] * 8,
        out_shape=[jax.ShapeDtypeStruct((N, W), F32)] * 8,
        scratch_shapes=[pltpu.VMEM((8, RWKV_COLS), F32)],
        compiler_params=_params("arbitrary", "arbitrary"),
        name="rwkv_pre",
    )(p_rw, vec(mu), vec(w0), _hi_lo(w2p), vec(a0), _hi_lo(a2p), _hi_lo(g2), vec(k_k), vec(k_a), vec(r_k),
      _head_block_diag(HEAD_SUM_WIDTH))


def _bf(x):
    return x.astype(BF16)


def _scan_local(chunks, eye, strict, incl, m0, m1, between_stages=lambda: None):
    C = CHUNK
    n = range(len(chunks))
    st = lambda z: jnp.concatenate([z * m0, z * m1], axis=0)
    zero = jnp.zeros((2 * C, 2 * C), F32)
    at_b, rt_s, vs, vs_b, lhs_a, rhs_a, bk_t, dcol = [], [], [], [], [], [], [], []
    for r, k, v, al, bb, ld, cum in chunks:
        tot = cum[C - 1:C, :]
        dinv = jnp.exp(-cum)
        dend = jnp.exp(tot - cum)
        at_b.append(_bf(st(al * jnp.exp(cum - ld))))
        rt_s.append(st(r * jnp.exp(cum)))
        vs.append(st(v))
        vs_b.append(_bf(vs[-1]))
        lhs_a.append(jnp.concatenate([at_b[-1], _bf(rt_s[-1])], axis=0))
        rhs_a.append(_bf(jnp.concatenate([st(bb * dinv), st(k * dinv)], axis=0)))
        bk_t.append(_bf(jnp.concatenate([st(bb * dend).T, st(k * dend).T], axis=1)))
        dcol.append(jnp.sum(eye * jnp.exp(tot), axis=1, keepdims=True))
    between_stages()
    A = [_dot_tb(lhs_a[i], rhs_a[i]) for i in n]
    between_stages()
    a_ab = [jnp.where(strict, A[i][0:2 * C, 0:2 * C], zero) for i in n]
    a_ak = [_bf(jnp.where(strict, A[i][0:2 * C, 2 * C:4 * C], zero)) for i in n]
    a_r = [_bf(jnp.concatenate([jnp.where(incl, A[i][2 * C:4 * C, 0:2 * C], zero),
                                jnp.where(incl, A[i][2 * C:4 * C, 2 * C:4 * C], zero)], axis=1)) for i in n]
    akv = [_bf(_dot(a_ak[i], vs_b[i])) for i in n]
    between_stages()
    tinv = [eye + a_ab[i] for i in n]
    pw_b = [_bf(a_ab[i]) for i in n]
    pw_b = [_bf(_dot(pw_b[i], pw_b[i])) for i in n]
    between_stages()
    for step in range(5):
        rhs = [jnp.concatenate([pw_b[i], _bf(tinv[i])], axis=1) for i in n]
        if step == 4:
            rhs = [_bf(tinv[i]) for i in n]
        prod = [_dot(pw_b[i], rhs[i]) for i in n]
        tinv = [tinv[i] + prod[i][:, -2 * C:] for i in n]
        pw_b = [_bf(prod[i][:, 0:2 * C]) for i in n]
        between_stages()
    X = [_dot(_bf(tinv[i]), jnp.concatenate([at_b[i], akv[i]], axis=1)) for i in n]
    between_stages()
    w_b = [_bf(X[i][:, 0:LANES]) for i in n]
    uv0 = [jnp.concatenate([_bf(X[i][:, LANES:2 * LANES]), vs_b[i]], axis=0) for i in n]
    m_h = [_bf(_dot(bk_t[i][:, 0:2 * C], w_b[i])) for i in n]
    g_h = [_dot(bk_t[i], uv0[i]) for i in n]
    between_stages()
    q_h = [_bf(rt_s[i] + _dot(a_r[i][:, 0:2 * C], w_b[i])) for i in n]
    y0 = [_dot(a_r[i], uv0[i]) for i in n]
    return [(m_h[i], g_h[i], dcol[i], q_h[i], y0[i]) for i in n]


def _rwkv_scan_kernel(r_ref, k_ref, v_ref, al_ref, b_ref, ld_ref, o_ref, h_ref, *local_refs):
    C = CHUNK
    tc = r_ref.shape[0]
    nc = tc // C

    @pl.when(pl.program_id(2) == 0)
    def _():
        h_ref[...] = jnp.zeros_like(h_ref)
        for ref in local_refs:
            ref[...] = jnp.zeros_like(ref)

    seq = {"H": h_ref[...], "c": 0}

    def one_step():
        c = seq["c"]
        if c < nc:
            m_h, g_h, dcol, q_h, y0 = (ref[c] for ref in local_refs)
            h_b = _bf(seq["H"])
            Y = _dot(q_h, h_b) + y0
            o_ref[c * C:(c + 1) * C, :] = Y[0:C] + Y[C:2 * C]
            seq["H"] = dcol * seq["H"] + _dot(m_h, h_b) + g_h
            seq["c"] = c + 1

    tri = jnp.where(lax.broadcasted_iota(jnp.int32, (C, C), 1) <= lax.broadcasted_iota(jnp.int32, (C, C), 0),
                    1.0, 0.0).astype(BF16)
    r2 = lax.broadcasted_iota(jnp.int32, (2 * C, 2 * C), 0)
    c2 = lax.broadcasted_iota(jnp.int32, (2 * C, 2 * C), 1)
    eye = (r2 == c2).astype(F32)
    strict = (c2 & (C - 1)) < (r2 & (C - 1))
    incl = (c2 & (C - 1)) <= (r2 & (C - 1))
    lane = lax.broadcasted_iota(jnp.int32, (C, LANES), 1)
    m0 = (lane < HEAD_DIM).astype(F32)
    m1 = 1.0 - m0
    cum = _dot_split_rhs(tri, jnp.concatenate([ld_ref[c * C:(c + 1) * C, :] for c in range(nc)], axis=1), 3)
    chunks = []
    for c in range(nc):
        sl = slice(c * C, (c + 1) * C)
        chunks.append((r_ref[sl, :], k_ref[sl, :], v_ref[sl, :], al_ref[sl, :], b_ref[sl, :], ld_ref[sl, :],
                       cum[:, c * LANES:(c + 1) * LANES]))
    local = _scan_local(chunks, eye, strict, incl, m0, m1, between_stages=one_step)
    while seq["c"] < nc:
        one_step()
    h_ref[...] = seq["H"]
    for c, parts in enumerate(local):
        for ref, part in zip(local_refs, parts):
            ref[c] = part


def _rwkv_scan(r, k, v, al, bb, ld, B, T, tc=512):
    N, W = r.shape
    nt = T // tc
    nc = tc // CHUNK
    in_spec = pl.BlockSpec((tc, LANES), lambda b, h, t: (b * nt + jnp.minimum(t, nt - 1), h))
    out_spec = pl.BlockSpec((tc, LANES), lambda b, h, t: (b * nt + jnp.maximum(t - 1, 0), h))
    sq = (nc, LANES, LANES)
    return pl.pallas_call(
        _rwkv_scan_kernel,
        grid=(B, W // LANES, nt + 1),
        in_specs=[in_spec] * 6,
        out_specs=out_spec,
        out_shape=jax.ShapeDtypeStruct((N, W), F32),
        scratch_shapes=[pltpu.VMEM((LANES, LANES), F32), pltpu.VMEM(sq, BF16), pltpu.VMEM(sq, F32),
                        pltpu.VMEM((nc, LANES, 1), F32), pltpu.VMEM(sq, BF16), pltpu.VMEM(sq, F32)],
        compiler_params=_params("arbitrary", "arbitrary", "arbitrary"),
        name="rwkv_scan",
    )(r, k, v, al, bb, ld)


def _rope_tables(pos):
    half = ROPE_DIM // 2
    inv = jnp.power(ROPE_THETA, -jnp.arange(half, dtype=F32) * 2.0 / ROPE_DIM)
    ang = pos.astype(F32)[:, None] * inv[None, :]
    cos, sin = jnp.cos(ang), jnp.sin(ang)
    n = pos.shape[0]
    rest = HEAD_DIM - ROPE_DIM
    c = jnp.concatenate([cos, cos, jnp.ones((n, rest), F32)], axis=1)
    s_dn = jnp.concatenate([-sin, jnp.zeros((n, half + rest), F32)], axis=1)
    s_up = jnp.concatenate([jnp.zeros((n, half), F32), sin, jnp.zeros((n, rest), F32)], axis=1)
    rep = LANES // HEAD_DIM
    return jnp.tile(c, (1, rep)), jnp.tile(s_dn, (1, rep)), jnp.tile(s_up, (1, rep))


def _norm_rope(x, bd, g, c, s_dn, s_up):
    width = x.shape[1]
    half = ROPE_DIM // 2
    rep = width // LANES
    tile = (lambda z: jnp.concatenate([z] * rep, axis=1)) if rep > 1 else (lambda z: z)
    ms = _head_sums(x * x, bd)
    xn = x * lax.rsqrt(ms + NORM_EPS) * g
    return (xn * tile(c) + pltpu.roll(xn, width - half, 1) * tile(s_dn)
            + pltpu.roll(xn, half, 1) * tile(s_up))


def _nsa_prep_kernel(q_ref, kv_ref, c_ref, sd_ref, su_ref, gq_ref, gs_ref, gw_ref, bdq_ref, bdk_ref,
                     o_qt, o_ks, o_kw, o_vst, o_vsd, o_vwt):
    c, sd, su = c_ref[...], sd_ref[...], su_ref[...]
    q = _norm_rope(q_ref[...], bdq_ref[...], gq_ref[...], c, sd, su) * Q_SCALE
    qt = q.T
    ts = q.shape[0]
    kv = kv_ref[...]
    bdk = bdk_ref[...]
    pos = pl.program_id(1) * ts + lax.broadcasted_iota(jnp.int32, (ts, LANES), 0)
    blk_onehot = jnp.where((pos >> SEL_SHIFT) == lax.broadcasted_iota(jnp.int32, (ts, LANES), 1), 1.0, 0.0)
    ks = _norm_rope(kv[:, 2 * LANES:3 * LANES], bdk, gs_ref[...], c, sd, su)
    o_ks[...] = jnp.concatenate([ks, blk_onehot], axis=1).astype(BF16)
    o_kw[...] = _norm_rope(kv[:, 4 * LANES:5 * LANES], bdk, gw_ref[...], c, sd, su).astype(BF16)
    ones_rows = jnp.where(lax.broadcasted_iota(jnp.int32, (V_ROWS - HEAD_DIM, q.shape[0]), 0) == 0, 1.0, 0.0)

    def values_t(x):
        xt = x.T
        return jnp.concatenate([xt[0:HEAD_DIM], ones_rows, xt[HEAD_DIM:2 * HEAD_DIM], ones_rows], axis=0)

    vst = values_t(kv[:, 3 * LANES:4 * LANES])
    vwt = values_t(kv[:, 5 * LANES:6 * LANES])
    for j in range(q.shape[0] // KEY_TILE):
        sl = slice(j * KEY_TILE, (j + 1) * KEY_TILE)
        o_qt[0, j] = qt[:, sl].astype(BF16)
        o_vsd[0, j] = vst[:, sl].astype(BF16)
        o_vwt[0, j] = vwt[:, sl].astype(BF16)
    for j in range(q.shape[0] // SEL_TILE):
        o_vst[0, j] = vst[:, j * SEL_TILE:(j + 1) * SEL_TILE].astype(BF16)


def _nsa_prep(q, kv, tables, qk_g, B, T, ts=512):
    N = q.shape[0]
    nt = T // ts
    nk = ts // KEY_TILE
    ns = ts // SEL_TILE
    row = lambda b, t: (b * nt + t, 0)
    full = lambda shape: pl.BlockSpec(shape, lambda b, t: (0,) * len(shape))
    tab = pl.BlockSpec((ts, LANES), lambda b, t: (t, 0))
    gq = jnp.tile(qk_g[0], NSA_Q_HEADS).reshape(1, NSA_WIDTH)
    gs = jnp.tile(qk_g[2], NSA_KV_HEADS).reshape(1, LANES)
    gw = jnp.tile(qk_g[3], NSA_KV_HEADS).reshape(1, LANES)
    tiled = lambda rows: pl.BlockSpec((1, nk, rows, KEY_TILE), lambda b, t: (b, t, 0, 0))
    return pl.pallas_call(
        _nsa_prep_kernel,
        grid=(B, nt),
        in_specs=[pl.BlockSpec((ts, NSA_WIDTH), row), pl.BlockSpec((ts, KV_COLS), row), tab, tab, tab,
                  full((1, NSA_WIDTH)), full((1, LANES)), full((1, LANES)),
                  full((HEAD_SUM_WIDTH, HEAD_SUM_WIDTH)), full((LANES, LANES))],
        out_specs=[tiled(NSA_WIDTH), pl.BlockSpec((ts, 2 * LANES), row), pl.BlockSpec((ts, LANES), row),
                   pl.BlockSpec((1, ns, NSA_KV_HEADS * V_ROWS, SEL_TILE), lambda b, t: (b, t, 0, 0)),
                   tiled(NSA_KV_HEADS * V_ROWS), tiled(NSA_KV_HEADS * V_ROWS)],
        out_shape=[jax.ShapeDtypeStruct((B, T // KEY_TILE, NSA_WIDTH, KEY_TILE), BF16),
                   jax.ShapeDtypeStruct((N, 2 * LANES), BF16), jax.ShapeDtypeStruct((N, LANES), BF16),
                   jax.ShapeDtypeStruct((B, T // SEL_TILE, NSA_KV_HEADS * V_ROWS, SEL_TILE), BF16),
                   jax.ShapeDtypeStruct((B, T // KEY_TILE, NSA_KV_HEADS * V_ROWS, KEY_TILE), BF16),
                   jax.ShapeDtypeStruct((B, T // KEY_TILE, NSA_KV_HEADS * V_ROWS, KEY_TILE), BF16)],
        compiler_params=_params("arbitrary", "arbitrary"),
        name="nsa_prep",
    )(q, kv, *tables, gq, gs, gw, _head_block_diag(HEAD_SUM_WIDTH, 1.0 / HEAD_DIM),
      _head_block_diag(LANES, 1.0 / HEAD_DIM))


def _gelu_tanh(x):
    return 0.5 * x * (1.0 + jnp.tanh(0.7978845608028654 * (x + 0.044715 * x * x * x)))


def _nsa_cmp_kernel(x_ref, pos_ref, w1_ref, w2_ref, *rest, is_key):
    if is_key:
        g_ref, c_ref, sd_ref, su_ref, bd_ref, o_ref, xs_ref = rest
    else:
        o_ref, xs_ref = rest
    nch = xs_ref.shape[0]
    S = CMP_STRIDE
    for j in range(S):
        xs_ref[:, j * LANES:(j + 1) * LANES] = x_ref[0, pl.ds(j, nch, stride=S), :]
    xs = xs_ref[...]
    first = _dot((xs + pos_ref[0:1, :]).astype(BF16), w1_ref[0])
    second = _dot((xs + pos_ref[1:2, :]).astype(BF16), w1_ref[1])
    hid = first + pltpu.roll(second, nch - 1, 0)
    out = _dot(_gelu_tanh(hid).astype(BF16), w2_ref[...])
    rows = lax.broadcasted_iota(jnp.int32, out.shape, 0)
    if is_key:
        out = _norm_rope(out, bd_ref[...], g_ref[...], c_ref[...], sd_ref[...], su_ref[...])
        o_ref[0] = jnp.where(rows < nch - 1, out, 0.0).astype(BF16)
    else:
        o_ref[0] = jnp.where(rows < nch - 1, out, 0.0).T.astype(BF16)


def _nsa_cmp(kv3, which, cmp_pos, cmp_w1, cmp_w2, g_k, tables_cmp):
    B, T, _ = kv3.shape
    S = CMP_STRIDE
    nch = T // S
    is_key = which == 0
    eye2 = jnp.eye(NSA_KV_HEADS, dtype=F32)
    w1 = cmp_w1[which].reshape(CMP_BLOCK, HEAD_DIM, CMP_HIDDEN)
    w1 = jnp.einsum('jdh,ge->jgdeh', w1, eye2).reshape(2, S * LANES, NSA_KV_HEADS * CMP_HIDDEN)
    w2 = jnp.einsum('hd,ge->ghed', cmp_w2[which], eye2).reshape(NSA_KV_HEADS * CMP_HIDDEN, LANES)
    pos = jnp.tile(cmp_pos[which].reshape(2, S, 1, HEAD_DIM), (1, 1, NSA_KV_HEADS, 1)).reshape(2, S * LANES)
    full = lambda shape: pl.BlockSpec(shape, lambda b: (0,) * len(shape))
    in_specs = [pl.BlockSpec((1, T, LANES), lambda b: (b, 0, which)), full(pos.shape), full(w1.shape),
                full(w2.shape)]
    args = [kv3, pos, w1.astype(BF16), w2.astype(BF16)]
    if is_key:
        in_specs += [full((1, LANES)), full((nch, LANES)), full((nch, LANES)), full((nch, LANES)),
                     full((LANES, LANES))]
        args += [jnp.tile(g_k, NSA_KV_HEADS).reshape(1, LANES), *tables_cmp,
                 _head_block_diag(LANES, 1.0 / HEAD_DIM)]
        out_spec = pl.BlockSpec((1, nch, LANES), lambda b: (b, 0, 0))
        out_shape = jax.ShapeDtypeStruct((B, nch, LANES), BF16)
    else:
        out_spec = pl.BlockSpec((1, LANES, nch), lambda b: (b, 0, 0))
        out_shape = jax.ShapeDtypeStruct((B, LANES, nch), BF16)
    return pl.pallas_call(
        functools.partial(_nsa_cmp_kernel, is_key=is_key),
        grid=(B,),
        in_specs=in_specs,
        out_specs=out_spec,
        out_shape=out_shape,
        scratch_shapes=[pltpu.VMEM((nch, S * LANES), F32)],
        compiler_params=_params("arbitrary"),
        name="nsa_cmp_k" if is_key else "nsa_cmp_v",
    )(*args)


def _nsa_attn_kernel(qt_ref, kc_ref, vct_ref, ks_ref, vst_ref, vsd_ref, kw_ref, vwt_ref, gt_ref, ov_ref, o_ref,
                     rhs_ref, oc_ref, keep_ref, s0_ref, s1_ref, s2_ref, s3_ref, p0_ref, p1_ref):
    qb = pl.program_id(1)
    G = NSA_KV_HEADS
    R = NSA_GROUP
    QT = Q_TILE
    KT = KEY_TILE
    CG = R * QT
    NQ = G * CG
    D = HEAD_DIM
    t0 = qb * QT
    n_cmp_pad = kc_ref.shape[1]
    n_sel = ov_ref.shape[0]
    cols = lambda g: slice(g * CG, (g + 1) * CG)

    q_cols = []
    for g in range(G):
        q_g = jnp.concatenate([qt_ref[0, 0, (g * R + r) * D:(g * R + r + 1) * D, :] for r in range(R)], axis=1)
        q_cols.append(jnp.concatenate([q_g if gg == g else jnp.zeros_like(q_g) for gg in range(G)], axis=0))
    qpad = jnp.concatenate(q_cols, axis=1)

    tq_row = t0 + (lax.broadcasted_iota(jnp.int32, (1, NQ), 1) & (QT - 1))
    spread = lambda z: jnp.concatenate([z[:, g * QT:(g + 1) * QT] for g in range(G) for _ in range(R)], axis=1)
    tile_all = lambda z: jnp.concatenate([z] * (G * R), axis=1)

    def values_dot(v_of_group, p):
        return jnp.concatenate([_dot(v_of_group(g), p[:, cols(g)]) for g in range(G)], axis=1)

    NV = CMP_VARIANTS
    nq = ks_ref.shape[1] // QT

    def compressed_and_select(n_c, n_b):
        sc = _dot(kc_ref[0, 0:n_c, :], qpad)
        n_i = lax.broadcasted_iota(jnp.int32, (n_c, 1), 0)
        cend = jnp.where(n_i < n_cmp_pad - 1, n_i * CMP_STRIDE + (CMP_BLOCK - 1), jnp.int32(2 ** 30))
        cvalid = cend <= tq_row
        sc = jnp.where(cvalid, sc, NEG_INF)
        mc = jnp.max(sc, axis=0, keepdims=True)
        ec = jnp.where(cvalid, jnp.exp2(sc - mc), 0.0)
        pc = ec / jnp.maximum(jnp.sum(ec, axis=0, keepdims=True), F32_TINY)
        pc_b = pc.astype(BF16)
        oc_ref[...] = values_dot(lambda g: vct_ref[0, g * D:(g + 1) * D, 0:n_c], pc_b)
        sums = []
        for g in range(G):
            acc = pc[:, g * CG:g * CG + QT]
            for r in range(1, R):
                acc = acc + pc[:, g * CG + r * QT:g * CG + (r + 1) * QT]
            sums.append(acc)
        imp = _dot_split_rhs(ov_ref[0:n_b, 0:n_c], jnp.concatenate(sums, axis=1))
        jb = lax.broadcasted_iota(jnp.int32, (n_b, G * QT), 0)
        jf = jb.astype(F32)
        tq_b = t0 + (lax.broadcasted_iota(jnp.int32, (n_b, G * QT), 1) & (QT - 1))
        cur = tq_b >> SEL_SHIFT
        forced = (jb == 0) | (jb == cur) | (jb == cur - 1)
        visible = jb * SEL_BLOCK <= tq_b
        score = jnp.where(visible, jnp.where(forced, FORCE_SCORE, imp), -1.0)
        sel = jnp.zeros((n_b, G * QT), F32)
        for _ in range(min(SEL_TOPK, n_b)):
            mx = jnp.max(score, axis=0, keepdims=True)
            jmin = jnp.min(jnp.where(score == mx, jf, 1e9), axis=0, keepdims=True)
            hit = jf == jmin
            sel = jnp.where(hit, 1.0, sel)
            score = jnp.where(hit, -3e38, score)
        keep_ref[0:n_b, :] = jnp.where(visible, sel, 0.0)
        if n_b < n_sel:
            keep_ref[n_b:n_sel, :] = jnp.zeros((n_sel - n_b, G * QT), F32)

    for v in range(NV):
        @pl.when((qb * NV) // nq == v)
        def _():
            compressed_and_select((v + 1) * n_cmp_pad // NV, (v + 1) * n_sel // NV)

    o_c = oc_ref[...]
    ji = lax.broadcasted_iota(jnp.int32, (n_sel, G * QT), 0)

    ST = SEL_TILE
    bias_all = (keep_ref[...] - 1.0) * (-NEG_INF)
    first_own = t0 // SEL_BLOCK
    vrows = lambda g: slice(g * V_ROWS, (g + 1) * V_ROWS)

    def with_bias_rows(bias):
        rows = spread(bias).astype(BF16)
        if n_sel < LANES:
            rows = jnp.concatenate([rows, jnp.zeros((LANES - n_sel, NQ), BF16)], axis=0)
        return jnp.concatenate([qpad, rows], axis=0)

    rhs_ref[...] = with_bias_rows(jnp.where(ji < first_own, bias_all, NEG_INF))
    n_tiles = (t0 + ST - 1) // ST
    last_tile = ks_ref.shape[1] // ST - 1
    p_bufs = (p0_ref, p1_ref)

    def sel_scores(kt, s_ref):
        k0 = pl.multiple_of(jnp.minimum(kt, last_tile) * ST, ST)
        s_ref[...] = _dot(ks_ref[0, pl.ds(k0, ST), :], rhs_ref[...])

    def sel_values(kt, slot, acc, alpha):
        kt = jnp.clip(kt, 0, last_tile)
        return acc * alpha + values_dot(lambda g: vst_ref[0, kt, vrows(g), :], p_bufs[slot][...])

    def sel_softmax(s_ref, slot, m):
        s = s_ref[...]
        m_new = jnp.maximum(m, jnp.max(s, axis=0, keepdims=True))
        p_bufs[slot][...] = jnp.exp2(s - m_new).astype(BF16)
        return m_new, jnp.exp2(m - m_new)

    def sel_pair(a, carry, s_now, s_next):
        m, acc, alpha0, alpha1 = carry
        acc = sel_values(a - 2, 0, acc, alpha0)
        acc = sel_values(a - 1, 1, acc, alpha1)
        sel_scores(a + 2, s_next[0])
        sel_scores(a + 3, s_next[1])
        m, alpha0 = sel_softmax(s_now[0], 0, m)
        m, alpha1 = sel_softmax(s_now[1], 1, m)
        return m, acc, alpha0, alpha1

    bufs_a, bufs_b = (s0_ref, s1_ref), (s2_ref, s3_ref)
    sel_scores(0, s0_ref)
    sel_scores(1, s1_ref)
    p0_ref[...] = jnp.zeros_like(p0_ref)
    p1_ref[...] = jnp.zeros_like(p1_ref)
    own = _dot(ks_ref[0, pl.ds(pl.multiple_of(t0, QT), QT), :], with_bias_rows(bias_all))

    n_wt = (WINDOW + QT) // KT
    k0w = pl.multiple_of(jnp.maximum(t0 - WINDOW, 0), KT)
    kt_w = k0w // KT
    keys_w = kw_ref[0, pl.ds(k0w, WINDOW + QT), :]
    dw = (t0 + lax.broadcasted_iota(jnp.int32, (WINDOW + QT, QT), 1)
          - (k0w + lax.broadcasted_iota(jnp.int32, (WINDOW + QT, QT), 0)))
    sw = _dot(keys_w, qpad) + tile_all(jnp.where(dw >= 0, jnp.where(dw < WINDOW, 0.0, NEG_INF), NEG_INF))
    pw = jnp.exp2(sw - jnp.max(sw, axis=0, keepdims=True)).astype(BF16)
    acc_w = values_dot(lambda g: vwt_ref[0, kt_w, vrows(g), :], pw[0:KT])
    for j in range(1, n_wt):
        acc_w = acc_w + values_dot(lambda g: vwt_ref[0, kt_w + j, vrows(g), :], pw[j * KT:(j + 1) * KT])

    n_pairs = (n_tiles + 1) // 2
    one = jnp.ones((1, NQ), F32)
    m_s, acc_s, alpha0, alpha1 = lax.fori_loop(
        0, n_pairs,
        lambda j, carry: lax.cond(j % 2 == 0,
                                  lambda c: sel_pair(2 * j, c, bufs_a, bufs_b),
                                  lambda c: sel_pair(2 * j, c, bufs_b, bufs_a), carry),
        (jnp.full((1, NQ), NEG_INF, F32), jnp.zeros((V_ROWS, NQ), F32), one, one))
    acc_s = sel_values(2 * n_pairs - 2, 0, acc_s, alpha0)
    acc_s = sel_values(2 * n_pairs - 1, 1, acc_s, alpha1)
    seen = lax.broadcasted_iota(jnp.int32, (QT, QT), 0) <= lax.broadcasted_iota(jnp.int32, (QT, QT), 1)
    own = jnp.where(tile_all(seen), own, NEG_INF)
    m_new = jnp.maximum(m_s, jnp.max(own, axis=0, keepdims=True))
    acc_s = acc_s * jnp.exp2(m_s - m_new) + values_dot(lambda g: vsd_ref[0, qb, vrows(g), :],
                                                       jnp.exp2(own - m_new).astype(BF16))

    gates = _sigmoid(gt_ref[0])
    grow = lambda j: jnp.concatenate([gates[g, j, r:r + 1, :] for g in range(G) for r in range(R)], axis=1)
    o = (grow(0) * o_c + grow(1) * (acc_s[0:D] / acc_s[D:D + 1])
         + grow(2) * (acc_w[0:D] / acc_w[D:D + 1]))
    halves = []
    for h in range(G * R // 2):
        pair = jnp.concatenate([o[:, (2 * h) * QT:(2 * h + 1) * QT],
                                o[:, (2 * h + 1) * QT:(2 * h + 2) * QT]], axis=0)
        halves.append(pair.T)
    o_ref[...] = jnp.concatenate(halves, axis=1)


def _nsa_attn(qt, kcmp, vct, ks3, vst, vsd, kw3, vwt, gt, ov_t, B, T):
    G, R = NSA_KV_HEADS, NSA_GROUP
    nq = T // Q_TILE
    nk = T // KEY_TILE
    nch = kcmp.shape[1]
    n_sel = ov_t.shape[0]
    NQ = G * R * Q_TILE
    assert (T // SEL_TILE) % 2 == 0 and n_sel <= LANES and Q_TILE == KEY_TILE
    assert nq % CMP_VARIANTS == 0 and n_sel % (8 * CMP_VARIANTS) == 0 and nch % (8 * CMP_VARIANTS) == 0
    return pl.pallas_call(
        _nsa_attn_kernel,
        grid=(B, nq),
        in_specs=[pl.BlockSpec((1, 1, NSA_WIDTH, Q_TILE), lambda b, q: (b, q, 0, 0)),
                  pl.BlockSpec((1, nch, LANES), lambda b, q: (b, 0, 0)),
                  pl.BlockSpec((1, G * HEAD_DIM, nch), lambda b, q: (b, 0, 0)),
                  pl.BlockSpec((1, T, 2 * LANES), lambda b, q: (b, 0, 0)),
                  pl.BlockSpec((1, T // SEL_TILE, G * V_ROWS, SEL_TILE), lambda b, q: (b, 0, 0, 0)),
                  pl.BlockSpec((1, nk, G * V_ROWS, KEY_TILE), lambda b, q: (b, 0, 0, 0)),
                  pl.BlockSpec((1, T, LANES), lambda b, q: (b, 0, 0)),
                  pl.BlockSpec((1, nk, G * V_ROWS, KEY_TILE), lambda b, q: (b, 0, 0, 0)),
                  pl.BlockSpec((1, G, 3, R, Q_TILE), lambda b, q: (b, 0, 0, 0, q)),
                  pl.BlockSpec((n_sel, nch), lambda b, q: (0, 0))],
        out_specs=pl.BlockSpec((Q_TILE, NSA_WIDTH), lambda b, q: (b * nq + q, 0)),
        out_shape=jax.ShapeDtypeStruct((B * T, NSA_WIDTH), F32),
        scratch_shapes=[pltpu.VMEM((2 * LANES, NQ), BF16), pltpu.VMEM((HEAD_DIM, NQ), F32),
                        pltpu.VMEM((n_sel, G * Q_TILE), F32),
                        *[pltpu.VMEM((SEL_TILE, NQ), F32)] * 4,
                        *[pltpu.VMEM((SEL_TILE, NQ), BF16)] * 2],
        compiler_params=_params("arbitrary", "arbitrary"),
        name="nsa_attn",
    )(qt, kcmp, vct, ks3, vst, vsd, kw3, vwt, gt, ov_t)


def _first_index_of(vals, target):
    idx = jnp.full_like(target, float(len(vals) - 1))
    for i in range(len(vals) - 2, -1, -1):
        idx = jnp.where(vals[i] == target, float(i), idx)
    return idx


def _pick(vals, idx):
    out = vals[-1]
    for i in range(len(vals) - 2, -1, -1):
        out = jnp.where(idx == float(i), vals[i], out)
    return out


def _route_rows(score, bias):
    E, G, P = N_EXPERTS, N_GROUPS, EXPERTS_PER_GROUP
    sel = score + bias
    s = [sel[e:e + 1, :] for e in range(E)]
    raw = [score[e:e + 1, :] for e in range(E)]
    grp = []
    for gi in range(G):
        a = s[gi * P:(gi + 1) * P]
        best = None
        for i in range(P):
            for j in range(i + 1, P):
                pair = a[i] + a[j]
                best = pair if best is None else jnp.maximum(best, pair)
        grp.append(best)
    gmax = functools.reduce(jnp.maximum, grp)
    g_star = _first_index_of(grp, gmax)
    v = [_pick([s[gi * P + i] for gi in range(G)], g_star) for i in range(P)]
    w = [_pick([raw[gi * P + i] for gi in range(G)], g_star) for i in range(P)]
    i1 = _first_index_of(v, functools.reduce(jnp.maximum, v))
    v2 = [jnp.where(i1 == float(i), -jnp.inf, v[i]) for i in range(P)]
    i2 = _first_index_of(v2, functools.reduce(jnp.maximum, v2))
    w1, w2 = _pick(w, i1), _pick(w, i2)
    tot = w1 + w2
    zero = jnp.zeros_like(tot)
    e1, e2 = g_star * P + i1, g_star * P + i2
    n = score.shape[1]
    eidx = lax.broadcasted_iota(jnp.int32, (E, n), 0).astype(F32)
    oh1, oh2 = jnp.where(eidx == e1, 1.0, 0.0), jnp.where(eidx == e2, 1.0, 0.0)
    earlier = jnp.where(lax.broadcasted_iota(jnp.int32, (n, n), 0) < lax.broadcasted_iota(jnp.int32, (n, n), 1),
                        1.0, 0.0).astype(BF16)
    cnt = _dot(jnp.concatenate([oh1, oh2], axis=0).astype(BF16), earlier)
    rank1 = jnp.sum(oh1 * cnt[0:E], axis=0, keepdims=True)
    rank2 = jnp.sum(oh2 * cnt[E:2 * E], axis=0, keepdims=True)
    lane = lax.broadcasted_iota(jnp.int32, (E, LANES), 1)
    totals = jnp.where(lane == 0, jnp.sum(oh1, axis=1, keepdims=True),
                       jnp.where(lane == 1, jnp.sum(oh2, axis=1, keepdims=True), 0.0))
    return jnp.concatenate([e1, e2, w1 / tot, w2 / tot, rank1, rank2, zero, zero], axis=0), totals


def _merge_kernel(ys_ref, g_ref, bonus_ref, gng_ref, gnb_ref, bd_ref, yb_ref, pm_ref, x_ref, mod_ref,
                  ng_ref, wa_ref, wb_ref, wo_ref, rw_ref, rb_ref, o_x, o_h, o_route, o_tot):
    m = mod_ref[pl.program_id(0)]
    bd = bd_ref[...]
    y = ys_ref[...]
    mean = _head_sums(y, bd)
    yc = y - mean
    var = _head_sums(yc * yc, bd)
    ya = (yc * lax.rsqrt(var + RWKV_GN_EPS) * gng_ref[...] + gnb_ref[...] + bonus_ref[...]) * g_ref[...]
    pm = pm_ref[...]
    D = x_ref.shape[1]
    mix = (_sigmoid(pm[:, 0:D]) * _dot(ya.astype(BF16), wa_ref[...])
           + _sigmoid(pm[:, D:2 * D]) * _dot(yb_ref[...].astype(BF16), wb_ref[...]))
    x = x_ref[...] + m[2:3] * _dot(mix.astype(BF16), wo_ref[...])
    o_x[...] = x
    ms = jnp.mean(x * x, axis=-1, keepdims=True)
    h = x * lax.rsqrt(ms + NORM_EPS) * ng_ref[...]
    h = h * (1.0 + m[4:5]) + m[3:4]
    o_h[...] = h
    score = _sigmoid(_dot_3pass(h, rw_ref).T[0:N_EXPERTS, :])
    o_route[...], o_tot[...] = _route_rows(score, rb_ref[...])


def _merge(ys, g, bonus, gn_g, gn_b, yb, pm, x2, mod, ng, wa, wb, wo, router_w, router_b, B, T, tm=256):
    N, D = x2.shape
    W = RWKV_WIDTH
    nt = T // tm
    row = lambda b, t: (b * nt + t, 0)
    full = lambda shape: pl.BlockSpec(shape, lambda b, t: (0,) * len(shape))
    return pl.pallas_call(
        _merge_kernel,
        grid=(B, nt),
        in_specs=[pl.BlockSpec((tm, W), row), pl.BlockSpec((tm, W), row), pl.BlockSpec((tm, W), row),
                  full((1, W)), full((1, W)), full((HEAD_SUM_WIDTH, HEAD_SUM_WIDTH)),
                  pl.BlockSpec((tm, NSA_WIDTH), row), pl.BlockSpec((tm, 2 * D), row),
                  pl.BlockSpec((tm, D), row), full((B, 6, D)), full((1, D)),
                  full((W, D)), full((NSA_WIDTH, D)), full((D, D)), full((2, D, LANES)),
                  full((N_EXPERTS, 1))],
        out_specs=[pl.BlockSpec((tm, D), row), pl.BlockSpec((tm, D), row),
                   pl.BlockSpec((8, tm), lambda b, t: (0, b * nt + t)),
                   pl.BlockSpec((N_EXPERTS, LANES), lambda b, t: (b * nt + t, 0))],
        out_shape=[jax.ShapeDtypeStruct((N, D), F32), jax.ShapeDtypeStruct((N, D), F32),
                   jax.ShapeDtypeStruct((8, N), F32), jax.ShapeDtypeStruct((N // tm * N_EXPERTS, LANES), F32)],
        compiler_params=_params("arbitrary", "arbitrary"),
        name="merge_out",
    )(ys, g, bonus, gn_g.reshape(1, W), gn_b.reshape(1, W), _head_block_diag(HEAD_SUM_WIDTH, 1.0 / HEAD_DIM),
      yb, pm, x2, mod, ng, wa, wb, wo,
      _hi_lo(jnp.zeros((D, LANES), F32).at[:, :N_EXPERTS].set(router_w)), router_b.reshape(N_EXPERTS, 1))


def _route(route, totals, N):
    wts = route[TOP_K:2 * TOP_K].T
    NK = N * TOP_K
    E = N_EXPERTS
    n_tiles = totals.shape[0] // E
    expert = route[0:TOP_K].astype(jnp.int32)
    rank = route[2 * TOP_K:3 * TOP_K].astype(jnp.int32)
    per = totals.reshape(n_tiles, E, LANES)[:, :, 0:TOP_K].astype(jnp.int32).transpose(0, 2, 1)
    per = per.reshape(n_tiles * TOP_K, E)
    csum = jnp.cumsum(per, axis=0)
    counts = csum[-1]
    padded = (counts + MOE_BLOCK - 1) // MOE_BLOCK * MOE_BLOCK
    pad_end = jnp.cumsum(padded)
    pad_start = pad_end - padded
    first = (pad_start[None, :] + csum - per).reshape(n_tiles, TOP_K, E).transpose(1, 0, 2)
    first = jnp.repeat(first, N // n_tiles, axis=1)
    mine = expert[:, :, None] == jnp.arange(E, dtype=jnp.int32)[None, None, :]
    dest = (jnp.sum(jnp.where(mine, first, 0), axis=-1) + rank).reshape(-1)
    n_blk = -(-NK // MOE_BLOCK) + N_EXPERTS
    blk_start = jnp.arange(n_blk, dtype=jnp.int32) * MOE_BLOCK
    blk_expert = jnp.sum((pad_end[None, :] <= blk_start[:, None]).astype(jnp.int32), axis=1)
    blk_expert = jnp.clip(blk_expert, 0, N_EXPERTS - 1)
    blk_valid = jnp.clip((pad_start + counts)[blk_expert] - blk_start, 0, MOE_BLOCK).astype(jnp.int32)
    dest = jnp.pad(dest.astype(jnp.int32).reshape(NK // SC_WINDOW, SC_WINDOW), ((0, 0), (0, LANES - SC_WINDOW)))
    return wts, dest, blk_expert, blk_valid, n_blk


SC_WINDOW = 32


def _sc_mesh():
    return plsc.VectorSubcoreMesh(core_axis_name="c", subcore_axis_name="s")


def _sc_dispatch(h, dest, n_slots):
    N, D = h.shape
    W = SC_WINDOW
    nw = N // W

    @pl.kernel(out_type=jax.ShapeDtypeStruct((n_slots, D), h.dtype), mesh=_sc_mesh(), scratch_types=[])
    def dispatch(h_hbm, i_hbm, o_hbm):
        def body(x_vmem, i_vmem):
            pltpu.sync_copy(x_vmem, o_hbm.at[i_vmem.at[0, pl.ds(0, W)]])

        pltpu.emit_pipeline(
            body, grid=(TOP_K, nw),
            in_specs=[pl.BlockSpec((W, D), lambda k, i: (i, 0)),
                      pl.BlockSpec((1, LANES), lambda k, i: (k * nw + i, 0))],
            out_specs=[], core_axis_name=("c", "s"),
            dimension_semantics=(pltpu.PARALLEL, pltpu.PARALLEL))(h_hbm, i_hbm)

    return dispatch(h, dest)


def _sc_collect(ys, dest):
    W = SC_WINDOW
    NK = dest.shape[0] * W
    D = ys.shape[1]
    half = NK // TOP_K // W

    @pl.kernel(out_type=jax.ShapeDtypeStruct((NK, D), ys.dtype), mesh=_sc_mesh(), scratch_types=[])
    def collect(y_hbm, i_hbm, o_hbm):
        def body(i_vmem, o_vmem):
            pltpu.sync_copy(y_hbm.at[i_vmem.at[0, pl.ds(0, W)]], o_vmem)

        pltpu.emit_pipeline(
            body, grid=(TOP_K, half),
            in_specs=[pl.BlockSpec((1, LANES), lambda k, i: (k * half + i, 0))],
            out_specs=[pl.BlockSpec((W, D), lambda k, i: (k * half + i, 0))],
            core_axis_name=("c", "s"),
            dimension_semantics=(pltpu.PARALLEL, pltpu.PARALLEL))(i_hbm, o_hbm)

    return collect(ys, dest)


def _moe_dense_kernel(be_ref, nv_ref, x_ref, wg_ref, wu_ref, wd_ref, o_ref, wg_b, wu_b, wd_b):
    i = pl.program_id(0)
    nv = nv_ref[i]

    @pl.when((i == 0) | (be_ref[i] != be_ref[jnp.maximum(i - 1, 0)]))
    def _():
        wg_b[...] = wg_ref[0, 0].astype(BF16)
        wu_b[...] = wu_ref[0, 0].astype(BF16)
        wd_b[...] = wd_ref[0, 0].astype(BF16)

    @pl.when(nv > 0)
    def _():
        x = x_ref[...].astype(BF16)
        gate = _dot(x, wg_b[...])
        up = _dot(x, wu_b[...])
        o_ref[...] = _dot((gate * _sigmoid(gate) * up).astype(BF16), wd_b[...])

    @pl.when(nv == 0)
    def _():
        o_ref[...] = jnp.zeros_like(o_ref)


def _moe_dense(xs, blk_expert, blk_valid, n_blk, layer, wg, wu, wd):
    P, D = xs.shape
    DE = wg.shape[3]
    wmap = lambda i, be, nv: (layer, be[i], 0, 0)
    grid_spec = pltpu.PrefetchScalarGridSpec(
        num_scalar_prefetch=2,
        grid=(n_blk,),
        in_specs=[pl.BlockSpec((MOE_BLOCK, D), lambda i, be, nv: (i, 0)), pl.BlockSpec((1, 1, D, DE), wmap),
                  pl.BlockSpec((1, 1, D, DE), wmap), pl.BlockSpec((1, 1, DE, D), wmap)],
        out_specs=pl.BlockSpec((MOE_BLOCK, D), lambda i, be, nv: (i, 0)),
        scratch_shapes=[pltpu.VMEM((D, DE), BF16), pltpu.VMEM((D, DE), BF16), pltpu.VMEM((DE, D), BF16)],
    )
    return pl.pallas_call(
        _moe_dense_kernel,
        grid_spec=grid_spec,
        out_shape=jax.ShapeDtypeStruct((P, D), F32),
        compiler_params=_params("arbitrary"),
        name="moe_experts",
    )(blk_expert, blk_valid, xs, wg, wu, wd)


def _final_kernel(x_ref, y0_ref, y1_ref, w_ref, mod_ref, o_ref):
    o_ref[...] = _moe_residual(x_ref, y0_ref, y1_ref, w_ref, mod_ref[pl.program_id(0)])


def _final(x2, ybuf, wts, mod, B, T, tm=512):
    N, D = x2.shape
    nt = T // tm
    row = lambda b, t: (b * nt + t, 0)
    return pl.pallas_call(
        _final_kernel,
        grid=(B, nt),
        in_specs=[pl.BlockSpec((tm, D), row), pl.BlockSpec((tm, D), row),
                  pl.BlockSpec((tm, D), lambda b, t: (N // tm + b * nt + t, 0)),
                  pl.BlockSpec((tm, TOP_K), row), pl.BlockSpec((B, 6, D), lambda b, t: (0, 0, 0))],
        out_specs=pl.BlockSpec((tm, D), row),
        out_shape=jax.ShapeDtypeStruct((N, D), F32),
        compiler_params=_params("arbitrary", "arbitrary"),
        name="moe_combine",
    )(x2, ybuf, ybuf, wts, mod)


def _overlap_t(n_sel, n_cmp_pad):
    ci = jnp.arange(n_cmp_pad)[None, :] * CMP_STRIDE
    sj = jnp.arange(n_sel)[:, None] * SEL_BLOCK
    ov = (ci <= sj + SEL_BLOCK - 1) & (ci + CMP_BLOCK - 1 >= sj) & (jnp.arange(n_cmp_pad)[None, :] < n_cmp_pad - 1)
    return ov.astype(BF16)


def kernel(x, c, w_ada, b_ada, norm_g, w_in, b_in, rwkv_mu, rwkv_w0, rwkv_w2, rwkv_a0, rwkv_a2, rwkv_g2,
           rwkv_k_k, rwkv_k_a, rwkv_r_k, rwkv_gn_g, rwkv_gn_b, qk_norm_g, cmp_pos, cmp_w1, cmp_w2,
           w_up_rwkv, w_up_nsa, w_out, router_w, router_b, exp_w_gate, exp_w_up, exp_w_down):
    B, T, D = x.shape
    L = w_ada.shape[0]
    N = B * T
    mods = _ada(c, w_ada, b_ada)
    tables = _rope_tables(jnp.arange(T, dtype=jnp.int32))
    nch = T // CMP_STRIDE
    tables_cmp = _rope_tables(jnp.arange(nch, dtype=jnp.int32) * CMP_STRIDE + CMP_BLOCK - 1)
    ov_t = _overlap_t(T // SEL_BLOCK, nch)
    n_gate = NSA_GATE_COLS
    x2 = x.reshape(N, D)
    pending_moe = None
    for l in range(L):
        g0 = _SEG_KV[1] + n_gate
        w_pad = jnp.concatenate([w_in[l][:, :g0], jnp.zeros((D, GATE_PAD - n_gate), F32), w_in[l][:, g0:]],
                                axis=1).astype(BF16)
        b_pad = jnp.concatenate([b_in[l][:g0], jnp.zeros((GATE_PAD - n_gate,), F32), b_in[l][g0:]]).reshape(1, -1)
        outs = _inproj(x2, pending_moe, mods[l], norm_g[l, 0].reshape(1, D), w_pad, b_pad, B, T)
        if pending_moe is not None:
            x2, outs = outs[0], outs[1:]
        p_rw, p_q, p_kv, p_gate, p_merge = outs
        r, k, v, al, bb, ld, g, bonus = _rwkv_pre(p_rw, rwkv_mu[l], rwkv_w0[l], rwkv_w2[l], rwkv_a0[l],
                                                  rwkv_a2[l], rwkv_g2[l], rwkv_k_k[l], rwkv_k_a[l],
                                                  rwkv_r_k[l], B, T)
        ys = _rwkv_scan(r, k, v, al, bb, ld, B, T)
        qt, ks, kw, vst, vsd, vwt = _nsa_prep(p_q, p_kv, tables, qk_norm_g[l], B, T)
        kv3 = p_kv.reshape(B, T, KV_COLS)
        kcmp = _nsa_cmp(kv3, 0, cmp_pos[l], cmp_w1[l], cmp_w2[l], qk_norm_g[l, 1], tables_cmp)
        vct = _nsa_cmp(kv3, 1, cmp_pos[l], cmp_w1[l], cmp_w2[l], None, None)
        gt = p_gate[:, :n_gate].reshape(B, T, NSA_KV_HEADS, NSA_GROUP, 3).transpose(0, 2, 4, 3, 1)
        yb = _nsa_attn(qt, kcmp, vct, ks.reshape(B, T, 2 * LANES), vst, vsd, kw.reshape(B, T, LANES), vwt, gt, ov_t,
                       B, T)
        x2, h2, route, totals = _merge(ys, g, bonus, rwkv_gn_g[l], rwkv_gn_b[l], yb, p_merge, x2, mods[l],
                               norm_g[l, 1].reshape(1, D), w_up_rwkv[l].astype(BF16),
                               w_up_nsa[l].astype(BF16), w_out[l].astype(BF16), router_w, router_b, B, T)
        wts, dest, blk_expert, blk_valid, n_blk = _route(route, totals, N)
        xs = _sc_dispatch(h2, dest, n_blk * MOE_BLOCK)
        ys = _moe_dense(xs, blk_expert, blk_valid, n_blk, l, exp_w_gate, exp_w_up, exp_w_down)
        ybuf = _sc_collect(ys, dest)
        pending_moe = (ybuf, wts, mods[l])
    return _final(x2, *pending_moe, B, T).reshape(B, T, D)
```

```python
import functools
import math

import jax
import jax.numpy as jnp
from jax import lax
from jax.experimental import pallas as pl
from jax.experimental.pallas import tpu as pltpu
from jax.experimental.pallas import tpu_sc as plsc

F32 = jnp.float32
BF16 = jnp.bfloat16
HI = lax.Precision.HIGHEST

D_MODEL = 1024
RWKV_HEADS = 8
HEAD_DIM = 64
RWKV_WIDTH = RWKV_HEADS * HEAD_DIM
DECAY_LORA = 64
ICLR_LORA = 64
GATE_LORA = 128
RWKV_GN_EPS = 64e-5
RWKV_COLS = 3 * RWKV_WIDTH + DECAY_LORA + ICLR_LORA + GATE_LORA

NSA_Q_HEADS = 8
NSA_KV_HEADS = 2
NSA_GROUP = NSA_Q_HEADS // NSA_KV_HEADS
NSA_WIDTH = NSA_Q_HEADS * HEAD_DIM
CMP_STRIDE = 16
CMP_BLOCK = 2 * CMP_STRIDE
CMP_HIDDEN = 256
SEL_BLOCK = 64
SEL_SHIFT = 6
SEL_TOPK = 16
WINDOW = 512
FORCE_SCORE = 1e4
NEG_INF = -1e30
ROPE_THETA = 500000.0
ROPE_DIM = HEAD_DIM // 4
KV_COLS = 6 * NSA_KV_HEADS * HEAD_DIM
NSA_GATE_COLS = 3 * NSA_Q_HEADS
GATE_PAD = 128

N_EXPERTS = 16
N_GROUPS = 4
EXPERTS_PER_GROUP = N_EXPERTS // N_GROUPS
TOP_K = 2
D_EXPERT = 512
MOE_BLOCK = 256
NORM_EPS = 1e-6

LANES = 128
CHUNK = 64
KEY_TILE = 128
SEL_TILE = 512
CMP_VARIANTS = 4
V_ROWS = 80
Q_SCALE = HEAD_DIM ** -0.5 * math.log2(math.e)
Q_TILE = 128
F32_TINY = float(jnp.finfo(jnp.float32).tiny)

_SEG_RW = (0, RWKV_COLS)
_SEG_Q = (_SEG_RW[1], _SEG_RW[1] + NSA_WIDTH)
_SEG_KV = (_SEG_Q[1], _SEG_Q[1] + KV_COLS)
_SEG_GATE = (_SEG_KV[1], _SEG_KV[1] + GATE_PAD)
_SEG_MERGE = (_SEG_GATE[1], _SEG_GATE[1] + 2 * D_MODEL)
IN_COLS_PAD = _SEG_MERGE[1]

_VMEM_LIMIT = 56 * 1024 * 1024


def _dot(a, b, precision=None):
    return jnp.dot(a, b, preferred_element_type=F32, precision=precision)


def _dot_tb(a, b, precision=None):
    return lax.dot_general(a, b, (((1,), (1,)), ((), ())), preferred_element_type=F32,
                           precision=precision)


def _dot_ta(a, b, precision=None):
    return lax.dot_general(a, b, (((0,), (0,)), ((), ())), preferred_element_type=F32,
                           precision=precision)


def _split_bf16(x, terms):
    parts = []
    for _ in range(terms - 1):
        parts.append(x.astype(BF16))
        x = x - parts[-1].astype(F32)
    parts.append(x.astype(BF16))
    return parts


def _dot_split_lhs(x, w_bf, terms=2):
    return functools.reduce(jnp.add, [_dot(p, w_bf) for p in _split_bf16(x, terms)])


def _dot_split_rhs(w_bf, x, terms=2):
    return functools.reduce(jnp.add, [_dot(w_bf, p) for p in _split_bf16(x, terms)])


def _dot_3pass(x, w_hl_ref):
    x_hi, x_lo = _split_bf16(x, 2)
    w_hi = w_hl_ref[0]
    return _dot(x_hi, w_hi) + _dot(x_lo, w_hi) + _dot(x_hi, w_hl_ref[1])


def _pack_bf16_pairs(x):
    n = x.shape[1] // 2
    bits = lax.bitcast_convert_type(x.astype(BF16).astype(F32), jnp.uint32)
    return lax.bitcast_convert_type(bits[:, 0:n] | (bits[:, n:] >> 16), F32)


def _unpack_bf16_pairs(packed):
    bits = lax.bitcast_convert_type(packed, jnp.uint32)
    lo = lax.bitcast_convert_type(bits & jnp.uint32(0xFFFF0000), F32)
    hi = lax.bitcast_convert_type(bits << 16, F32)
    return lo.astype(BF16), hi.astype(BF16)


def _hi_lo(w):
    hi = w.astype(BF16)
    return jnp.stack([hi, (w - hi.astype(F32)).astype(BF16)])


def _params(*sem):
    return pltpu.CompilerParams(dimension_semantics=sem, vmem_limit_bytes=_VMEM_LIMIT)


def _sigmoid(x):
    return 1.0 / (1.0 + jnp.exp(-x))


def _ada_kernel(c_ref, w_ref, b_ref, o_ref):
    c = c_ref[...]
    s = c * _sigmoid(c)
    o_ref[0] = _dot(s, w_ref[0], HI) + b_ref[0]


def _ada(c, w_ada, b_ada):
    L, D, D6 = w_ada.shape
    B = c.shape[0]
    rows = 8
    cp = jnp.zeros((rows, D), F32).at[:B].set(c)
    tn = 1536
    out = pl.pallas_call(
        _ada_kernel,
        grid=(L, D6 // tn),
        in_specs=[pl.BlockSpec((rows, D), lambda l, j: (0, 0)),
                  pl.BlockSpec((1, D, tn), lambda l, j: (l, 0, j)),
                  pl.BlockSpec((1, 1, tn), lambda l, j: (l, 0, j))],
        out_specs=pl.BlockSpec((1, rows, tn), lambda l, j: (l, 0, j)),
        out_shape=jax.ShapeDtypeStruct((L, rows, D6), F32),
        compiler_params=_params("arbitrary", "arbitrary"),
        name="ada_mod",
    )(cp, w_ada, b_ada.reshape(L, 1, D6))
    return out[:, :B].reshape(L, B, 6, D)


def _moe_residual(x_ref, y0_ref, y1_ref, w_ref, m_prev):
    w = w_ref[...]
    return x_ref[...] + m_prev[5:6] * (w[:, 0:1] * y0_ref[...] + w[:, 1:2] * y1_ref[...])


def _inproj_kernel(*refs, after_moe):
    if after_moe:
        x_ref, y0_ref, y1_ref, wts_ref, modp_ref, mod_ref, g_ref, w_ref, b_ref, o_x = refs[:10]
        x = _moe_residual(x_ref, y0_ref, y1_ref, wts_ref, modp_ref[pl.program_id(0)])
        o_x[...] = x
    else:
        x_ref, mod_ref, g_ref, w_ref, b_ref = refs[:5]
        x = x_ref[...]
    o_rw, o_q, o_kv, o_gate, o_merge = refs[-5:]
    m = mod_ref[pl.program_id(0)]
    ms = jnp.mean(x * x, axis=-1, keepdims=True)
    h = x * lax.rsqrt(ms + NORM_EPS) * g_ref[...]
    h = h * (1.0 + m[1:2]) + m[0:1]
    hb = h.astype(BF16)
    for o, (a, e) in ((o_rw, _SEG_RW), (o_q, _SEG_Q), (o_kv, _SEG_KV), (o_gate, _SEG_GATE),
                      (o_merge, _SEG_MERGE)):
        o[...] = _dot(hb, w_ref[:, a:e]) + b_ref[:, a:e]


def _inproj(x2, pending_moe, mod, g, w_pad, b_pad, B, T, tm=512):
    N, D = x2.shape
    nt = T // tm
    row = lambda b, t: (b * nt + t, 0)
    mods_spec = pl.BlockSpec((B, 6, D), lambda b, t: (0, 0, 0))
    widths = [e - a for a, e in (_SEG_RW, _SEG_Q, _SEG_KV, _SEG_GATE, _SEG_MERGE)]
    in_specs = [pl.BlockSpec((tm, D), row)]
    args = [x2]
    if pending_moe is not None:
        ybuf, wts, mod_prev = pending_moe
        in_specs += [pl.BlockSpec((tm, D), row), pl.BlockSpec((tm, D), lambda b, t: (N // tm + b * nt + t, 0)),
                     pl.BlockSpec((tm, TOP_K), row), mods_spec]
        args += [ybuf, ybuf, wts, mod_prev]
        widths = [D] + widths
    in_specs += [mods_spec, pl.BlockSpec((1, D), lambda b, t: (0, 0)),
                 pl.BlockSpec((D, IN_COLS_PAD), lambda b, t: (0, 0)),
                 pl.BlockSpec((1, IN_COLS_PAD), lambda b, t: (0, 0))]
    return pl.pallas_call(
        functools.partial(_inproj_kernel, after_moe=pending_moe is not None),
        grid=(B, nt),
        in_specs=in_specs,
        out_specs=[pl.BlockSpec((tm, w), row) for w in widths],
        out_shape=[jax.ShapeDtypeStruct((N, w), F32) for w in widths],
        compiler_params=_params("arbitrary", "arbitrary"),
        name="in_proj",
    )(*args, mod, g, w_pad, b_pad)


def _rwkv_pre_kernel(p_ref, mu_ref, w0_ref, w2_ref, a0_ref, a2_ref, g2_ref, kk_ref, ka_ref, rk_ref,
                     bd_ref, o_r, o_k, o_v, o_al, o_b, o_ld, o_g, o_bonus, carry_ref):
    W = RWKV_WIDTH

    @pl.when(pl.program_id(1) == 0)
    def _():
        carry_ref[...] = jnp.zeros_like(carry_ref)

    p = p_ref[...]
    ts = p.shape[0]
    rows = lax.broadcasted_iota(jnp.int32, p.shape, 0)
    shifted = jnp.where(rows == 0, carry_ref[0:1, :], pltpu.roll(p, 1, 0))
    carry_ref[0:1, :] = p[ts - 1:ts, :]
    pm = p + (shifted - p) * mu_ref[...]
    r = pm[:, 0:W]
    k = pm[:, W:2 * W]
    v = pm[:, 2 * W:3 * W]
    wa = pm[:, 3 * W:3 * W + DECAY_LORA + ICLR_LORA]
    gl = pm[:, 3 * W + DECAY_LORA + ICLR_LORA:]
    xw = w0_ref[...] + _dot_3pass(jnp.tanh(wa), w2_ref)
    ld = -math.exp(-0.5) * _sigmoid(xw)
    a = _sigmoid(a0_ref[...] + _dot_3pass(wa, a2_ref))
    g = _dot_3pass(_sigmoid(gl), g2_ref)
    bd = bd_ref[...]
    kk = k * kk_ref[...]
    nrm = jnp.sqrt(_head_sums(kk * kk, bd))
    kk = kk / jnp.maximum(nrm, 1e-12)
    k2 = k * (1.0 + (a - 1.0) * ka_ref[...])
    bonus = _head_sums(r * k2 * rk_ref[...], bd) * v
    o_r[...] = r
    o_k[...] = k2
    o_v[...] = v
    o_al[...] = kk
    o_b[...] = -kk * a
    o_ld[...] = ld
    o_g[...] = g
    o_bonus[...] = bonus


HEAD_SUM_WIDTH = 256


def _head_sums(x, bd):
    w = bd.shape[0]
    parts = [_dot_split_lhs(x[:, j:j + w], bd) for j in range(0, x.shape[1], w)]
    return parts[0] if len(parts) == 1 else jnp.concatenate(parts, axis=1)


def _head_block_diag(width, scale=1.0):
    i = jnp.arange(width) // HEAD_DIM
    return ((i[:, None] == i[None, :]).astype(F32) * scale).astype(BF16)


def _rwkv_pre(p_rw, mu, w0, w2, a0, a2, g2, k_k, k_a, r_k, B, T, ts=512):
    N = p_rw.shape[0]
    W = RWKV_WIDTH
    nt = T // ts
    row = lambda b, t: (b * nt + t, 0)
    zl = jnp.zeros((DECAY_LORA, W), F32)
    w2p = jnp.concatenate([w2, zl], axis=0)
    a2p = jnp.concatenate([zl, a2], axis=0)
    full = lambda shape: pl.BlockSpec(shape, lambda b, t: (0,) * len(shape))
    vec = lambda z: z.reshape(1, -1)
    return pl.pallas_call(
        _rwkv_pre_kernel,
        grid=(B, nt),
        in_specs=[pl.BlockSpec((ts, RWKV_COLS), row), full((1, RWKV_COLS)), full((1, W)),
                  full((2, 2 * DECAY_LORA, W)), full((1, W)), full((2, 2 * DECAY_LORA, W)),
                  full((2, GATE_LORA, W)), full((1, W)), full((1, W)), full((1, W)),
                  full((HEAD_SUM_WIDTH, HEAD_SUM_WIDTH))],
        out_specs=[pl.BlockSpec((ts, W), row)] * 8,
        out_shape=[jax.ShapeDtypeStruct((N, W), F32)] * 8,
        scratch_shapes=[pltpu.VMEM((8, RWKV_COLS), F32)],
        compiler_params=_params("arbitrary", "arbitrary"),
        name="rwkv_pre",
    )(p_rw, vec(mu), vec(w0), _hi_lo(w2p), vec(a0), _hi_lo(a2p), _hi_lo(g2), vec(k_k), vec(k_a), vec(r_k),
      _head_block_diag(HEAD_SUM_WIDTH))


def _bf(x):
    return x.astype(BF16)


def _scan_local(chunks, eye, strict, incl, m0, m1, between_stages=lambda: None):
    C = CHUNK
    n = range(len(chunks))
    st = lambda z: jnp.concatenate([z * m0, z * m1], axis=0)
    zero = jnp.zeros((2 * C, 2 * C), F32)
    at_b, rt_s, vs, vs_b, lhs_a, rhs_a, bk_t, dcol = [], [], [], [], [], [], [], []
    for r, k, v, al, bb, ld, cum in chunks:
        tot = cum[C - 1:C, :]
        dinv = jnp.exp(-cum)
        dend = jnp.exp(tot - cum)
        at_b.append(_bf(st(al * jnp.exp(cum - ld))))
        rt_s.append(st(r * jnp.exp(cum)))
        vs.append(st(v))
        vs_b.append(_bf(vs[-1]))
        lhs_a.append(jnp.concatenate([at_b[-1], _bf(rt_s[-1])], axis=0))
        rhs_a.append(_bf(jnp.concatenate([st(bb * dinv), st(k * dinv)], axis=0)))
        bk_t.append(_bf(jnp.concatenate([st(bb * dend).T, st(k * dend).T], axis=1)))
        dcol.append(jnp.sum(eye * jnp.exp(tot), axis=1, keepdims=True))
    between_stages()
    A = [_dot_tb(lhs_a[i], rhs_a[i]) for i in n]
    between_stages()
    a_ab = [jnp.where(strict, A[i][0:2 * C, 0:2 * C], zero) for i in n]
    a_ak = [_bf(jnp.where(strict, A[i][0:2 * C, 2 * C:4 * C], zero)) for i in n]
    a_r = [_bf(jnp.concatenate([jnp.where(incl, A[i][2 * C:4 * C, 0:2 * C], zero),
                                jnp.where(incl, A[i][2 * C:4 * C, 2 * C:4 * C], zero)], axis=1)) for i in n]
    akv = [_bf(_dot(a_ak[i], vs_b[i])) for i in n]
    between_stages()
    tinv = [eye + a_ab[i] for i in n]
    pw_b = [_bf(a_ab[i]) for i in n]
    pw_b = [_bf(_dot(pw_b[i], pw_b[i])) for i in n]
    between_stages()
    for step in range(5):
        rhs = [jnp.concatenate([pw_b[i], _bf(tinv[i])], axis=1) for i in n]
        if step == 4:
            rhs = [_bf(tinv[i]) for i in n]
        prod = [_dot(pw_b[i], rhs[i]) for i in n]
        tinv = [tinv[i] + prod[i][:, -2 * C:] for i in n]
        pw_b = [_bf(prod[i][:, 0:2 * C]) for i in n]
        between_stages()
    X = [_dot(_bf(tinv[i]), jnp.concatenate([at_b[i], akv[i]], axis=1)) for i in n]
    between_stages()
    w_b = [_bf(X[i][:, 0:LANES]) for i in n]
    uv0 = [jnp.concatenate([_bf(X[i][:, LANES:2 * LANES]), vs_b[i]], axis=0) for i in n]
    m_h = [_bf(_dot(bk_t[i][:, 0:2 * C], w_b[i])) for i in n]
    g_h = [_dot(bk_t[i], uv0[i]) for i in n]
    between_stages()
    q_h = [_bf(rt_s[i] + _dot(a_r[i][:, 0:2 * C], w_b[i])) for i in n]
    y0 = [_dot(a_r[i], uv0[i]) for i in n]
    return [(m_h[i], g_h[i], dcol[i], q_h[i], y0[i]) for i in n]


def _rwkv_scan_kernel(r_ref, k_ref, v_ref, al_ref, b_ref, ld_ref, o_ref, h_ref, *local_refs):
    C = CHUNK
    tc = r_ref.shape[0]
    nc = tc // C

    @pl.when(pl.program_id(2) == 0)
    def _():
        h_ref[...] = jnp.zeros_like(h_ref)
        for ref in local_refs:
            ref[...] = jnp.zeros_like(ref)

    seq = {"H": h_ref[...], "c": 0}

    def one_step():
        c = seq["c"]
        if c < nc:
            m_h, g_h, dcol, q_h, y0 = (ref[c] for ref in local_refs)
            h_b = _bf(seq["H"])
            Y = _dot(q_h, h_b) + y0
            o_ref[c * C:(c + 1) * C, :] = Y[0:C] + Y[C:2 * C]
            seq["H"] = dcol * seq["H"] + _dot(m_h, h_b) + g_h
            seq["c"] = c + 1

    tri = jnp.where(lax.broadcasted_iota(jnp.int32, (C, C), 1) <= lax.broadcasted_iota(jnp.int32, (C, C), 0),
                    1.0, 0.0).astype(BF16)
    r2 = lax.broadcasted_iota(jnp.int32, (2 * C, 2 * C), 0)
    c2 = lax.broadcasted_iota(jnp.int32, (2 * C, 2 * C), 1)
    eye = (r2 == c2).astype(F32)
    strict = (c2 & (C - 1)) < (r2 & (C - 1))
    incl = (c2 & (C - 1)) <= (r2 & (C - 1))
    lane = lax.broadcasted_iota(jnp.int32, (C, LANES), 1)
    m0 = (lane < HEAD_DIM).astype(F32)
    m1 = 1.0 - m0
    cum = _dot_split_rhs(tri, jnp.concatenate([ld_ref[c * C:(c + 1) * C, :] for c in range(nc)], axis=1), 3)
    chunks = []
    for c in range(nc):
        sl = slice(c * C, (c + 1) * C)
        chunks.append((r_ref[sl, :], k_ref[sl, :], v_ref[sl, :], al_ref[sl, :], b_ref[sl, :], ld_ref[sl, :],
                       cum[:, c * LANES:(c + 1) * LANES]))
    local = _scan_local(chunks, eye, strict, incl, m0, m1, between_stages=one_step)
    while seq["c"] < nc:
        one_step()
    h_ref[...] = seq["H"]
    for c, parts in enumerate(local):
        for ref, part in zip(local_refs, parts):
            ref[c] = part


def _rwkv_scan(r, k, v, al, bb, ld, B, T, tc=512):
    N, W = r.shape
    nt = T // tc
    nc = tc // CHUNK
    in_spec = pl.BlockSpec((tc, LANES), lambda b, h, t: (b * nt + jnp.minimum(t, nt - 1), h))
    out_spec = pl.BlockSpec((tc, LANES), lambda b, h, t: (b * nt + jnp.maximum(t - 1, 0), h))
    sq = (nc, LANES, LANES)
    return pl.pallas_call(
        _rwkv_scan_kernel,
        grid=(B, W // LANES, nt + 1),
        in_specs=[in_spec] * 6,
        out_specs=out_spec,
        out_shape=jax.ShapeDtypeStruct((N, W), F32),
        scratch_shapes=[pltpu.VMEM((LANES, LANES), F32), pltpu.VMEM(sq, BF16), pltpu.VMEM(sq, F32),
                        pltpu.VMEM((nc, LANES, 1), F32), pltpu.VMEM(sq, BF16), pltpu.VMEM(sq, F32)],
        compiler_params=_params("arbitrary", "arbitrary", "arbitrary"),
        name="rwkv_scan",
    )(r, k, v, al, bb, ld)


def _rope_tables(pos):
    half = ROPE_DIM // 2
    inv = jnp.power(ROPE_THETA, -jnp.arange(half, dtype=F32) * 2.0 / ROPE_DIM)
    ang = pos.astype(F32)[:, None] * inv[None, :]
    cos, sin = jnp.cos(ang), jnp.sin(ang)
    n = pos.shape[0]
    rest = HEAD_DIM - ROPE_DIM
    c = jnp.concatenate([cos, cos, jnp.ones((n, rest), F32)], axis=1)
    s_dn = jnp.concatenate([-sin, jnp.zeros((n, half + rest), F32)], axis=1)
    s_up = jnp.concatenate([jnp.zeros((n, half), F32), sin, jnp.zeros((n, rest), F32)], axis=1)
    rep = LANES // HEAD_DIM
    return jnp.tile(c, (1, rep)), jnp.tile(s_dn, (1, rep)), jnp.tile(s_up, (1, rep))


def _norm_rope(x, bd, g, c, s_dn, s_up):
    width = x.shape[1]
    half = ROPE_DIM // 2
    rep = width // LANES
    tile = (lambda z: jnp.concatenate([z] * rep, axis=1)) if rep > 1 else (lambda z: z)
    ms = _head_sums(x * x, bd)
    xn = x * lax.rsqrt(ms + NORM_EPS) * g
    return (xn * tile(c) + pltpu.roll(xn, width - half, 1) * tile(s_dn)
            + pltpu.roll(xn, half, 1) * tile(s_up))


def _nsa_prep_kernel(q_ref, kv_ref, c_ref, sd_ref, su_ref, gq_ref, gs_ref, gw_ref, bdq_ref, bdk_ref,
                     o_qt, o_ks, o_kw, o_vst, o_vsd, o_vwt):
    c, sd, su = c_ref[...], sd_ref[...], su_ref[...]
    q = _norm_rope(q_ref[...], bdq_ref[...], gq_ref[...], c, sd, su) * Q_SCALE
    qt = q.T
    ts = q.shape[0]
    kv = kv_ref[...]
    bdk = bdk_ref[...]
    pos = pl.program_id(1) * ts + lax.broadcasted_iota(jnp.int32, (ts, LANES), 0)
    blk_onehot = jnp.where((pos >> SEL_SHIFT) == lax.broadcasted_iota(jnp.int32, (ts, LANES), 1), 1.0, 0.0)
    ks = _norm_rope(kv[:, 2 * LANES:3 * LANES], bdk, gs_ref[...], c, sd, su)
    o_ks[...] = jnp.concatenate([ks, blk_onehot], axis=1).astype(BF16)
    o_kw[...] = _norm_rope(kv[:, 4 * LANES:5 * LANES], bdk, gw_ref[...], c, sd, su).astype(BF16)
    ones_rows = jnp.where(lax.broadcasted_iota(jnp.int32, (V_ROWS - HEAD_DIM, q.shape[0]), 0) == 0, 1.0, 0.0)

    def values_t(x):
        xt = x.T
        return jnp.concatenate([xt[0:HEAD_DIM], ones_rows, xt[HEAD_DIM:2 * HEAD_DIM], ones_rows], axis=0)

    vst = values_t(kv[:, 3 * LANES:4 * LANES])
    vwt = values_t(kv[:, 5 * LANES:6 * LANES])
    for j in range(q.shape[0] // KEY_TILE):
        sl = slice(j * KEY_TILE, (j + 1) * KEY_TILE)
        o_qt[0, j] = qt[:, sl].astype(BF16)
        o_vsd[0, j] = vst[:, sl].astype(BF16)
        o_vwt[0, j] = vwt[:, sl].astype(BF16)
    for j in range(q.shape[0] // SEL_TILE):
        o_vst[0, j] = vst[:, j * SEL_TILE:(j + 1) * SEL_TILE].astype(BF16)


def _nsa_prep(q, kv, tables, qk_g, B, T, ts=512):
    N = q.shape[0]
    nt = T // ts
    nk = ts // KEY_TILE
    ns = ts // SEL_TILE
    row = lambda b, t: (b * nt + t, 0)
    full = lambda shape: pl.BlockSpec(shape, lambda b, t: (0,) * len(shape))
    tab = pl.BlockSpec((ts, LANES), lambda b, t: (t, 0))
    gq = jnp.tile(qk_g[0], NSA_Q_HEADS).reshape(1, NSA_WIDTH)
    gs = jnp.tile(qk_g[2], NSA_KV_HEADS).reshape(1, LANES)
    gw = jnp.tile(qk_g[3], NSA_KV_HEADS).reshape(1, LANES)
    tiled = lambda rows: pl.BlockSpec((1, nk, rows, KEY_TILE), lambda b, t: (b, t, 0, 0))
    return pl.pallas_call(
        _nsa_prep_kernel,
        grid=(B, nt),
        in_specs=[pl.BlockSpec((ts, NSA_WIDTH), row), pl.BlockSpec((ts, KV_COLS), row), tab, tab, tab,
                  full((1, NSA_WIDTH)), full((1, LANES)), full((1, LANES)),
                  full((HEAD_SUM_WIDTH, HEAD_SUM_WIDTH)), full((LANES, LANES))],
        out_specs=[tiled(NSA_WIDTH), pl.BlockSpec((ts, 2 * LANES), row), pl.BlockSpec((ts, LANES), row),
                   pl.BlockSpec((1, ns, NSA_KV_HEADS * V_ROWS, SEL_TILE), lambda b, t: (b, t, 0, 0)),
                   tiled(NSA_KV_HEADS * V_ROWS), tiled(NSA_KV_HEADS * V_ROWS)],
        out_shape=[jax.ShapeDtypeStruct((B, T // KEY_TILE, NSA_WIDTH, KEY_TILE), BF16),
                   jax.ShapeDtypeStruct((N, 2 * LANES), BF16), jax.ShapeDtypeStruct((N, LANES), BF16),
                   jax.ShapeDtypeStruct((B, T // SEL_TILE, NSA_KV_HEADS * V_ROWS, SEL_TILE), BF16),
                   jax.ShapeDtypeStruct((B, T // KEY_TILE, NSA_KV_HEADS * V_ROWS, KEY_TILE), BF16),
                   jax.ShapeDtypeStruct((B, T // KEY_TILE, NSA_KV_HEADS * V_ROWS, KEY_TILE), BF16)],
        compiler_params=_params("arbitrary", "arbitrary"),
        name="nsa_prep",
    )(q, kv, *tables, gq, gs, gw, _head_block_diag(HEAD_SUM_WIDTH, 1.0 / HEAD_DIM),
      _head_block_diag(LANES, 1.0 / HEAD_DIM))


def _gelu_tanh(x):
    return 0.5 * x * (1.0 + jnp.tanh(0.7978845608028654 * (x + 0.044715 * x * x * x)))


def _nsa_cmp_kernel(x_ref, pos_ref, w1_ref, w2_ref, *rest, is_key):
    if is_key:
        g_ref, c_ref, sd_ref, su_ref, bd_ref, o_ref, xs_ref = rest
    else:
        o_ref, xs_ref = rest
    nch = xs_ref.shape[0]
    S = CMP_STRIDE
    for j in range(S):
        xs_ref[:, j * LANES:(j + 1) * LANES] = x_ref[0, pl.ds(j, nch, stride=S), :]
    xs = xs_ref[...]
    first = _dot((xs + pos_ref[0:1, :]).astype(BF16), w1_ref[0])
    second = _dot((xs + pos_ref[1:2, :]).astype(BF16), w1_ref[1])
    hid = first + pltpu.roll(second, nch - 1, 0)
    out = _dot(_gelu_tanh(hid).astype(BF16), w2_ref[...])
    rows = lax.broadcasted_iota(jnp.int32, out.shape, 0)
    if is_key:
        out = _norm_rope(out, bd_ref[...], g_ref[...], c_ref[...], sd_ref[...], su_ref[...])
        o_ref[0] = jnp.where(rows < nch - 1, out, 0.0).astype(BF16)
    else:
        o_ref[0] = jnp.where(rows < nch - 1, out, 0.0).T.astype(BF16)


def _nsa_cmp(kv3, which, cmp_pos, cmp_w1, cmp_w2, g_k, tables_cmp):
    B, T, _ = kv3.shape
    S = CMP_STRIDE
    nch = T // S
    is_key = which == 0
    eye2 = jnp.eye(NSA_KV_HEADS, dtype=F32)
    w1 = cmp_w1[which].reshape(CMP_BLOCK, HEAD_DIM, CMP_HIDDEN)
    w1 = jnp.einsum('jdh,ge->jgdeh', w1, eye2).reshape(2, S * LANES, NSA_KV_HEADS * CMP_HIDDEN)
    w2 = jnp.einsum('hd,ge->ghed', cmp_w2[which], eye2).reshape(NSA_KV_HEADS * CMP_HIDDEN, LANES)
    pos = jnp.tile(cmp_pos[which].reshape(2, S, 1, HEAD_DIM), (1, 1, NSA_KV_HEADS, 1)).reshape(2, S * LANES)
    full = lambda shape: pl.BlockSpec(shape, lambda b: (0,) * len(shape))
    in_specs = [pl.BlockSpec((1, T, LANES), lambda b: (b, 0, which)), full(pos.shape), full(w1.shape),
                full(w2.shape)]
    args = [kv3, pos, w1.astype(BF16), w2.astype(BF16)]
    if is_key:
        in_specs += [full((1, LANES)), full((nch, LANES)), full((nch, LANES)), full((nch, LANES)),
                     full((LANES, LANES))]
        args += [jnp.tile(g_k, NSA_KV_HEADS).reshape(1, LANES), *tables_cmp,
                 _head_block_diag(LANES, 1.0 / HEAD_DIM)]
        out_spec = pl.BlockSpec((1, nch, LANES), lambda b: (b, 0, 0))
        out_shape = jax.ShapeDtypeStruct((B, nch, LANES), BF16)
    else:
        out_spec = pl.BlockSpec((1, LANES, nch), lambda b: (b, 0, 0))
        out_shape = jax.ShapeDtypeStruct((B, LANES, nch), BF16)
    return pl.pallas_call(
        functools.partial(_nsa_cmp_kernel, is_key=is_key),
        grid=(B,),
        in_specs=in_specs,
        out_specs=out_spec,
        out_shape=out_shape,
        scratch_shapes=[pltpu.VMEM((nch, S * LANES), F32)],
        compiler_params=_params("arbitrary"),
        name="nsa_cmp_k" if is_key else "nsa_cmp_v",
    )(*args)


def _nsa_attn_kernel(qt_ref, kc_ref, vct_ref, ks_ref, vst_ref, vsd_ref, kw_ref, vwt_ref, gt_ref, ov_ref, o_ref,
                     rhs_ref, oc_ref, keep_ref, s0_ref, s1_ref, s2_ref, s3_ref, p0_ref, p1_ref):
    qb = pl.program_id(1)
    G = NSA_KV_HEADS
    R = NSA_GROUP
    QT = Q_TILE
    KT = KEY_TILE
    CG = R * QT
    NQ = G * CG
    D = HEAD_DIM
    t0 = qb * QT
    n_cmp_pad = kc_ref.shape[1]
    n_sel = ov_ref.shape[0]
    cols = lambda g: slice(g * CG, (g + 1) * CG)

    q_cols = []
    for g in range(G):
        q_g = jnp.concatenate([qt_ref[0, 0, (g * R + r) * D:(g * R + r + 1) * D, :] for r in range(R)], axis=1)
        q_cols.append(jnp.concatenate([q_g if gg == g else jnp.zeros_like(q_g) for gg in range(G)], axis=0))
    qpad = jnp.concatenate(q_cols, axis=1)

    tq_row = t0 + (lax.broadcasted_iota(jnp.int32, (1, NQ), 1) & (QT - 1))
    spread = lambda z: jnp.concatenate([z[:, g * QT:(g + 1) * QT] for g in range(G) for _ in range(R)], axis=1)
    tile_all = lambda z: jnp.concatenate([z] * (G * R), axis=1)

    def values_dot(v_of_group, p):
        return jnp.concatenate([_dot(v_of_group(g), p[:, cols(g)]) for g in range(G)], axis=1)

    NV = CMP_VARIANTS
    nq = ks_ref.shape[1] // QT

    def compressed_and_select(n_c, n_b):
        sc = _dot(kc_ref[0, 0:n_c, :], qpad)
        n_i = lax.broadcasted_iota(jnp.int32, (n_c, 1), 0)
        cend = jnp.where(n_i < n_cmp_pad - 1, n_i * CMP_STRIDE + (CMP_BLOCK - 1), jnp.int32(2 ** 30))
        cvalid = cend <= tq_row
        sc = jnp.where(cvalid, sc, NEG_INF)
        mc = jnp.max(sc, axis=0, keepdims=True)
        ec = jnp.where(cvalid, jnp.exp2(sc - mc), 0.0)
        pc = ec / jnp.maximum(jnp.sum(ec, axis=0, keepdims=True), F32_TINY)
        pc_b = pc.astype(BF16)
        oc_ref[...] = values_dot(lambda g: vct_ref[0, g * D:(g + 1) * D, 0:n_c], pc_b)
        sums = []
        for g in range(G):
            acc = pc[:, g * CG:g * CG + QT]
            for r in range(1, R):
                acc = acc + pc[:, g * CG + r * QT:g * CG + (r + 1) * QT]
            sums.append(acc)
        imp = _dot_split_rhs(ov_ref[0:n_b, 0:n_c], jnp.concatenate(sums, axis=1))
        jb = lax.broadcasted_iota(jnp.int32, (n_b, G * QT), 0)
        jf = jb.astype(F32)
        tq_b = t0 + (lax.broadcasted_iota(jnp.int32, (n_b, G * QT), 1) & (QT - 1))
        cur = tq_b >> SEL_SHIFT
        forced = (jb == 0) | (jb == cur) | (jb == cur - 1)
        visible = jb * SEL_BLOCK <= tq_b
        score = jnp.where(visible, jnp.where(forced, FORCE_SCORE, imp), -1.0)
        sel = jnp.zeros((n_b, G * QT), F32)
        for _ in range(min(SEL_TOPK, n_b)):
            mx = jnp.max(score, axis=0, keepdims=True)
            jmin = jnp.min(jnp.where(score == mx, jf, 1e9), axis=0, keepdims=True)
            hit = jf == jmin
            sel = jnp.where(hit, 1.0, sel)
            score = jnp.where(hit, -3e38, score)
        keep_ref[0:n_b, :] = jnp.where(visible, sel, 0.0)
        if n_b < n_sel:
            keep_ref[n_b:n_sel, :] = jnp.zeros((n_sel - n_b, G * QT), F32)

    for v in range(NV):
        @pl.when((qb * NV) // nq == v)
        def _():
            compressed_and_select((v + 1) * n_cmp_pad // NV, (v + 1) * n_sel // NV)

    o_c = oc_ref[...]
    ji = lax.broadcasted_iota(jnp.int32, (n_sel, G * QT), 0)

    ST = SEL_TILE
    bias_all = (keep_ref[...] - 1.0) * (-NEG_INF)
    first_own = t0 // SEL_BLOCK
    vrows = lambda g: slice(g * V_ROWS, (g + 1) * V_ROWS)

    def with_bias_rows(bias):
        rows = spread(bias).astype(BF16)
        if n_sel < LANES:
            rows = jnp.concatenate([rows, jnp.zeros((LANES - n_sel, NQ), BF16)], axis=0)
        return jnp.concatenate([qpad, rows], axis=0)

    rhs_ref[...] = with_bias_rows(jnp.where(ji < first_own, bias_all, NEG_INF))
    n_tiles = (t0 + ST - 1) // ST
    last_tile = ks_ref.shape[1] // ST - 1
    p_bufs = (p0_ref, p1_ref)

    def sel_scores(kt, s_ref):
        k0 = pl.multiple_of(jnp.minimum(kt, last_tile) * ST, ST)
        s_ref[...] = _dot(ks_ref[0, pl.ds(k0, ST), :], rhs_ref[...])

    def sel_values(kt, slot, acc, alpha):
        kt = jnp.clip(kt, 0, last_tile)
        return acc * alpha + values_dot(lambda g: vst_ref[0, kt, vrows(g), :], p_bufs[slot][...])

    def sel_softmax(s_ref, slot, m):
        s = s_ref[...]
        m_new = jnp.maximum(m, jnp.max(s, axis=0, keepdims=True))
        p_bufs[slot][...] = jnp.exp2(s - m_new).astype(BF16)
        return m_new, jnp.exp2(m - m_new)

    def sel_pair(a, carry, s_now, s_next):
        m, acc, alpha0, alpha1 = carry
        acc = sel_values(a - 2, 0, acc, alpha0)
        acc = sel_values(a - 1, 1, acc, alpha1)
        sel_scores(a + 2, s_next[0])
        sel_scores(a + 3, s_next[1])
        m, alpha0 = sel_softmax(s_now[0], 0, m)
        m, alpha1 = sel_softmax(s_now[1], 1, m)
        return m, acc, alpha0, alpha1

    bufs_a, bufs_b = (s0_ref, s1_ref), (s2_ref, s3_ref)
    sel_scores(0, s0_ref)
    sel_scores(1, s1_ref)
    p0_ref[...] = jnp.zeros_like(p0_ref)
    p1_ref[...] = jnp.zeros_like(p1_ref)
    own = _dot(ks_ref[0, pl.ds(pl.multiple_of(t0, QT), QT), :], with_bias_rows(bias_all))

    n_wt = (WINDOW + QT) // KT
    k0w = pl.multiple_of(jnp.maximum(t0 - WINDOW, 0), KT)
    kt_w = k0w // KT
    keys_w = kw_ref[0, pl.ds(k0w, WINDOW + QT), :]
    dw = (t0 + lax.broadcasted_iota(jnp.int32, (WINDOW + QT, QT), 1)
          - (k0w + lax.broadcasted_iota(jnp.int32, (WINDOW + QT, QT), 0)))
    sw = _dot(keys_w, qpad) + tile_all(jnp.where(dw >= 0, jnp.where(dw < WINDOW, 0.0, NEG_INF), NEG_INF))
    pw = jnp.exp2(sw - jnp.max(sw, axis=0, keepdims=True)).astype(BF16)
    acc_w = values_dot(lambda g: vwt_ref[0, kt_w, vrows(g), :], pw[0:KT])
    for j in range(1, n_wt):
        acc_w = acc_w + values_dot(lambda g: vwt_ref[0, kt_w + j, vrows(g), :], pw[j * KT:(j + 1) * KT])

    n_pairs = (n_tiles + 1) // 2
    one = jnp.ones((1, NQ), F32)
    m_s, acc_s, alpha0, alpha1 = lax.fori_loop(
        0, n_pairs,
        lambda j, carry: lax.cond(j % 2 == 0,
                                  lambda c: sel_pair(2 * j, c, bufs_a, bufs_b),
                                  lambda c: sel_pair(2 * j, c, bufs_b, bufs_a), carry),
        (jnp.full((1, NQ), NEG_INF, F32), jnp.zeros((V_ROWS, NQ), F32), one, one))
    acc_s = sel_values(2 * n_pairs - 2, 0, acc_s, alpha0)
    acc_s = sel_values(2 * n_pairs - 1, 1, acc_s, alpha1)
    seen = lax.broadcasted_iota(jnp.int32, (QT, QT), 0) <= lax.broadcasted_iota(jnp.int32, (QT, QT), 1)
    own = jnp.where(tile_all(seen), own, NEG_INF)
    m_new = jnp.maximum(m_s, jnp.max(own, axis=0, keepdims=True))
    acc_s = acc_s * jnp.exp2(m_s - m_new) + values_dot(lambda g: vsd_ref[0, qb, vrows(g), :],
                                                       jnp.exp2(own - m_new).astype(BF16))

    gates = _sigmoid(gt_ref[0])
    grow = lambda j: jnp.concatenate([gates[g, j, r:r + 1, :] for g in range(G) for r in range(R)], axis=1)
    o = (grow(0) * o_c + grow(1) * (acc_s[0:D] / acc_s[D:D + 1])
         + grow(2) * (acc_w[0:D] / acc_w[D:D + 1]))
    halves = []
    for h in range(G * R // 2):
        pair = jnp.concatenate([o[:, (2 * h) * QT:(2 * h + 1) * QT],
                                o[:, (2 * h + 1) * QT:(2 * h + 2) * QT]], axis=0)
        halves.append(pair.T)
    o_ref[...] = jnp.concatenate(halves, axis=1)


def _nsa_attn(qt, kcmp, vct, ks3, vst, vsd, kw3, vwt, gt, ov_t, B, T):
    G, R = NSA_KV_HEADS, NSA_GROUP
    nq = T // Q_TILE
    nk = T // KEY_TILE
    nch = kcmp.shape[1]
    n_sel = ov_t.shape[0]
    NQ = G * R * Q_TILE
    assert (T // SEL_TILE) % 2 == 0 and n_sel <= LANES and Q_TILE == KEY_TILE
    assert nq % CMP_VARIANTS == 0 and n_sel % (8 * CMP_VARIANTS) == 0 and nch % (8 * CMP_VARIANTS) == 0
    return pl.pallas_call(
        _nsa_attn_kernel,
        grid=(B, nq),
        in_specs=[pl.BlockSpec((1, 1, NSA_WIDTH, Q_TILE), lambda b, q: (b, q, 0, 0)),
                  pl.BlockSpec((1, nch, LANES), lambda b, q: (b, 0, 0)),
                  pl.BlockSpec((1, G * HEAD_DIM, nch), lambda b, q: (b, 0, 0)),
                  pl.BlockSpec((1, T, 2 * LANES), lambda b, q: (b, 0, 0)),
                  pl.BlockSpec((1, T // SEL_TILE, G * V_ROWS, SEL_TILE), lambda b, q: (b, 0, 0, 0)),
                  pl.BlockSpec((1, nk, G * V_ROWS, KEY_TILE), lambda b, q: (b, 0, 0, 0)),
                  pl.BlockSpec((1, T, LANES), lambda b, q: (b, 0, 0)),
                  pl.BlockSpec((1, nk, G * V_ROWS, KEY_TILE), lambda b, q: (b, 0, 0, 0)),
                  pl.BlockSpec((1, G, 3, R, Q_TILE), lambda b, q: (b, 0, 0, 0, q)),
                  pl.BlockSpec((n_sel, nch), lambda b, q: (0, 0))],
        out_specs=pl.BlockSpec((Q_TILE, NSA_WIDTH), lambda b, q: (b * nq + q, 0)),
        out_shape=jax.ShapeDtypeStruct((B * T, NSA_WIDTH), F32),
        scratch_shapes=[pltpu.VMEM((2 * LANES, NQ), BF16), pltpu.VMEM((HEAD_DIM, NQ), F32),
                        pltpu.VMEM((n_sel, G * Q_TILE), F32),
                        *[pltpu.VMEM((SEL_TILE, NQ), F32)] * 4,
                        *[pltpu.VMEM((SEL_TILE, NQ), BF16)] * 2],
        compiler_params=_params("arbitrary", "arbitrary"),
        name="nsa_attn",
    )(qt, kcmp, vct, ks3, vst, vsd, kw3, vwt, gt, ov_t)


def _first_index_of(vals, target):
    idx = jnp.full_like(target, float(len(vals) - 1))
    for i in range(len(vals) - 2, -1, -1):
        idx = jnp.where(vals[i] == target, float(i), idx)
    return idx


def _pick(vals, idx):
    out = vals[-1]
    for i in range(len(vals) - 2, -1, -1):
        out = jnp.where(idx == float(i), vals[i], out)
    return out


def _route_rows(score, bias):
    E, G, P = N_EXPERTS, N_GROUPS, EXPERTS_PER_GROUP
    sel = score + bias
    s = [sel[e:e + 1, :] for e in range(E)]
    raw = [score[e:e + 1, :] for e in range(E)]
    grp = []
    for gi in range(G):
        a = s[gi * P:(gi + 1) * P]
        best = None
        for i in range(P):
            for j in range(i + 1, P):
                pair = a[i] + a[j]
                best = pair if best is None else jnp.maximum(best, pair)
        grp.append(best)
    gmax = functools.reduce(jnp.maximum, grp)
    g_star = _first_index_of(grp, gmax)
    v = [_pick([s[gi * P + i] for gi in range(G)], g_star) for i in range(P)]
    w = [_pick([raw[gi * P + i] for gi in range(G)], g_star) for i in range(P)]
    i1 = _first_index_of(v, functools.reduce(jnp.maximum, v))
    v2 = [jnp.where(i1 == float(i), -jnp.inf, v[i]) for i in range(P)]
    i2 = _first_index_of(v2, functools.reduce(jnp.maximum, v2))
    w1, w2 = _pick(w, i1), _pick(w, i2)
    tot = w1 + w2
    zero = jnp.zeros_like(tot)
    e1, e2 = g_star * P + i1, g_star * P + i2
    n = score.shape[1]
    eidx = lax.broadcasted_iota(jnp.int32, (E, n), 0).astype(F32)
    oh1, oh2 = jnp.where(eidx == e1, 1.0, 0.0), jnp.where(eidx == e2, 1.0, 0.0)
    earlier = jnp.where(lax.broadcasted_iota(jnp.int32, (n, n), 0) < lax.broadcasted_iota(jnp.int32, (n, n), 1),
                        1.0, 0.0).astype(BF16)
    cnt = _dot(jnp.concatenate([oh1, oh2], axis=0).astype(BF16), earlier)
    rank1 = jnp.sum(oh1 * cnt[0:E], axis=0, keepdims=True)
    rank2 = jnp.sum(oh2 * cnt[E:2 * E], axis=0, keepdims=True)
    lane = lax.broadcasted_iota(jnp.int32, (E, LANES), 1)
    totals = jnp.where(lane == 0, jnp.sum(oh1, axis=1, keepdims=True),
                       jnp.where(lane == 1, jnp.sum(oh2, axis=1, keepdims=True), 0.0))
    return jnp.concatenate([e1, e2, w1 / tot, w2 / tot, rank1, rank2, zero, zero], axis=0), totals


def _merge_kernel(ys_ref, g_ref, bonus_ref, gng_ref, gnb_ref, bd_ref, yb_ref, pm_ref, x_ref, mod_ref,
                  ng_ref, wa_ref, wb_ref, wo_ref, rw_ref, rb_ref, o_x, o_h, o_route, o_tot):
    m = mod_ref[pl.program_id(0)]
    bd = bd_ref[...]
    y = ys_ref[...]
    mean = _head_sums(y, bd)
    yc = y - mean
    var = _head_sums(yc * yc, bd)
    ya = (yc * lax.rsqrt(var + RWKV_GN_EPS) * gng_ref[...] + gnb_ref[...] + bonus_ref[...]) * g_ref[...]
    pm = pm_ref[...]
    D = x_ref.shape[1]
    mix = (_sigmoid(pm[:, 0:D]) * _dot(ya.astype(BF16), wa_ref[...])
           + _sigmoid(pm[:, D:2 * D]) * _dot(yb_ref[...].astype(BF16), wb_ref[...]))
    x = x_ref[...] + m[2:3] * _dot(mix.astype(BF16), wo_ref[...])
    o_x[...] = x
    ms = jnp.mean(x * x, axis=-1, keepdims=True)
    h = x * lax.rsqrt(ms + NORM_EPS) * ng_ref[...]
    h = h * (1.0 + m[4:5]) + m[3:4]
    o_h[...] = _pack_bf16_pairs(h)
    score = _sigmoid(_dot_3pass(h, rw_ref).T[0:N_EXPERTS, :])
    o_route[...], o_tot[...] = _route_rows(score, rb_ref[...])


def _merge(ys, g, bonus, gn_g, gn_b, yb, pm, x2, mod, ng, wa, wb, wo, router_w, router_b, B, T, tm=256):
    N, D = x2.shape
    W = RWKV_WIDTH
    nt = T // tm
    row = lambda b, t: (b * nt + t, 0)
    full = lambda shape: pl.BlockSpec(shape, lambda b, t: (0,) * len(shape))
    return pl.pallas_call(
        _merge_kernel,
        grid=(B, nt),
        in_specs=[pl.BlockSpec((tm, W), row), pl.BlockSpec((tm, W), row), pl.BlockSpec((tm, W), row),
                  full((1, W)), full((1, W)), full((HEAD_SUM_WIDTH, HEAD_SUM_WIDTH)),
                  pl.BlockSpec((tm, NSA_WIDTH), row), pl.BlockSpec((tm, 2 * D), row),
                  pl.BlockSpec((tm, D), row), full((B, 6, D)), full((1, D)),
                  full((W, D)), full((NSA_WIDTH, D)), full((D, D)), full((2, D, LANES)),
                  full((N_EXPERTS, 1))],
        out_specs=[pl.BlockSpec((tm, D), row), pl.BlockSpec((tm, D // 2), row),
                   pl.BlockSpec((8, tm), lambda b, t: (0, b * nt + t)),
                   pl.BlockSpec((N_EXPERTS, LANES), lambda b, t: (b * nt + t, 0))],
        out_shape=[jax.ShapeDtypeStruct((N, D), F32), jax.ShapeDtypeStruct((N, D // 2), F32),
                   jax.ShapeDtypeStruct((8, N), F32), jax.ShapeDtypeStruct((N // tm * N_EXPERTS, LANES), F32)],
        compiler_params=_params("arbitrary", "arbitrary"),
        name="merge_out",
    )(ys, g, bonus, gn_g.reshape(1, W), gn_b.reshape(1, W), _head_block_diag(HEAD_SUM_WIDTH, 1.0 / HEAD_DIM),
      yb, pm, x2, mod, ng, wa, wb, wo,
      _hi_lo(jnp.zeros((D, LANES), F32).at[:, :N_EXPERTS].set(router_w)), router_b.reshape(N_EXPERTS, 1))


def _route(route, totals, N):
    wts = route[TOP_K:2 * TOP_K].T
    NK = N * TOP_K
    E = N_EXPERTS
    n_tiles = totals.shape[0] // E
    expert = route[0:TOP_K].astype(jnp.int32)
    rank = route[2 * TOP_K:3 * TOP_K].astype(jnp.int32)
    per = totals.reshape(n_tiles, E, LANES)[:, :, 0:TOP_K].astype(jnp.int32).transpose(0, 2, 1)
    per = per.reshape(n_tiles * TOP_K, E)
    csum = jnp.cumsum(per, axis=0)
    counts = csum[-1]
    padded = (counts + MOE_BLOCK - 1) // MOE_BLOCK * MOE_BLOCK
    pad_end = jnp.cumsum(padded)
    pad_start = pad_end - padded
    first = (pad_start[None, :] + csum - per).reshape(n_tiles, TOP_K, E).transpose(1, 0, 2)
    first = jnp.repeat(first, N // n_tiles, axis=1)
    mine = expert[:, :, None] == jnp.arange(E, dtype=jnp.int32)[None, None, :]
    dest = (jnp.sum(jnp.where(mine, first, 0), axis=-1) + rank).reshape(-1)
    n_blk = -(-NK // MOE_BLOCK) + N_EXPERTS
    blk_start = jnp.arange(n_blk, dtype=jnp.int32) * MOE_BLOCK
    blk_expert = jnp.sum((pad_end[None, :] <= blk_start[:, None]).astype(jnp.int32), axis=1)
    blk_expert = jnp.clip(blk_expert, 0, N_EXPERTS - 1)
    blk_valid = jnp.clip((pad_start + counts)[blk_expert] - blk_start, 0, MOE_BLOCK).astype(jnp.int32)
    dest = jnp.pad(dest.astype(jnp.int32).reshape(NK // SC_WINDOW, SC_WINDOW), ((0, 0), (0, LANES - SC_WINDOW)))
    return wts, dest, blk_expert, blk_valid, n_blk


SC_WINDOW = 32


def _sc_mesh():
    return plsc.VectorSubcoreMesh(core_axis_name="c", subcore_axis_name="s")


def _sc_dispatch(h, dest, n_slots):
    N, D = h.shape
    W = SC_WINDOW
    nw = N // W

    @pl.kernel(out_type=jax.ShapeDtypeStruct((n_slots, D), h.dtype), mesh=_sc_mesh(), scratch_types=[])
    def dispatch(h_hbm, i_hbm, o_hbm):
        def body(x_vmem, i_vmem):
            pltpu.sync_copy(x_vmem, o_hbm.at[i_vmem.at[0, pl.ds(0, W)]])

        pltpu.emit_pipeline(
            body, grid=(TOP_K, nw),
            in_specs=[pl.BlockSpec((W, D), lambda k, i: (i, 0)),
                      pl.BlockSpec((1, LANES), lambda k, i: (k * nw + i, 0))],
            out_specs=[], core_axis_name=("c", "s"),
            dimension_semantics=(pltpu.PARALLEL, pltpu.PARALLEL))(h_hbm, i_hbm)

    return dispatch(h, dest)


def _sc_collect(ys, dest):
    W = SC_WINDOW
    NK = dest.shape[0] * W
    D = ys.shape[1]
    half = NK // TOP_K // W

    @pl.kernel(out_type=jax.ShapeDtypeStruct((NK, D), ys.dtype), mesh=_sc_mesh(), scratch_types=[])
    def collect(y_hbm, i_hbm, o_hbm):
        def body(i_vmem, o_vmem):
            pltpu.sync_copy(y_hbm.at[i_vmem.at[0, pl.ds(0, W)]], o_vmem)

        pltpu.emit_pipeline(
            body, grid=(TOP_K, half),
            in_specs=[pl.BlockSpec((1, LANES), lambda k, i: (k * half + i, 0))],
            out_specs=[pl.BlockSpec((W, D), lambda k, i: (k * half + i, 0))],
            core_axis_name=("c", "s"),
            dimension_semantics=(pltpu.PARALLEL, pltpu.PARALLEL))(i_hbm, o_hbm)

    return collect(ys, dest)


def _moe_dense_kernel(be_ref, nv_ref, x_ref, wg_ref, wu_ref, wd_ref, o_ref, wg_b, wu_b, wd_b):
    i = pl.program_id(0)
    nv = nv_ref[i]

    @pl.when((i == 0) | (be_ref[i] != be_ref[jnp.maximum(i - 1, 0)]))
    def _():
        wg_b[...] = wg_ref[0, 0].astype(BF16)
        wu_b[...] = wu_ref[0, 0].astype(BF16)
        wd_b[...] = wd_ref[0, 0].astype(BF16)

    @pl.when(nv > 0)
    def _():
        x_lo, x_hi = _unpack_bf16_pairs(x_ref[...])
        half = x_lo.shape[1]
        gate = _dot(x_lo, wg_b[0:half, :]) + _dot(x_hi, wg_b[half:, :])
        up = _dot(x_lo, wu_b[0:half, :]) + _dot(x_hi, wu_b[half:, :])
        o_ref[...] = _dot((gate * _sigmoid(gate) * up).astype(BF16), wd_b[...])

    @pl.when(nv == 0)
    def _():
        o_ref[...] = jnp.zeros_like(o_ref)


def _moe_dense(xs, blk_expert, blk_valid, n_blk, layer, wg, wu, wd):
    P = xs.shape[0]
    D, DE = wg.shape[2:]
    wmap = lambda i, be, nv: (layer, be[i], 0, 0)
    grid_spec = pltpu.PrefetchScalarGridSpec(
        num_scalar_prefetch=2,
        grid=(n_blk,),
        in_specs=[pl.BlockSpec((MOE_BLOCK, D // 2), lambda i, be, nv: (i, 0)), pl.BlockSpec((1, 1, D, DE), wmap),
                  pl.BlockSpec((1, 1, D, DE), wmap), pl.BlockSpec((1, 1, DE, D), wmap)],
        out_specs=pl.BlockSpec((MOE_BLOCK, D), lambda i, be, nv: (i, 0)),
        scratch_shapes=[pltpu.VMEM((D, DE), BF16), pltpu.VMEM((D, DE), BF16), pltpu.VMEM((DE, D), BF16)],
    )
    return pl.pallas_call(
        _moe_dense_kernel,
        grid_spec=grid_spec,
        out_shape=jax.ShapeDtypeStruct((P, D), F32),
        compiler_params=_params("arbitrary"),
        name="moe_experts",
    )(blk_expert, blk_valid, xs, wg, wu, wd)


def _final_kernel(x_ref, y0_ref, y1_ref, w_ref, mod_ref, o_ref):
    o_ref[...] = _moe_residual(x_ref, y0_ref, y1_ref, w_ref, mod_ref[pl.program_id(0)])


def _final(x2, ybuf, wts, mod, B, T, tm=512):
    N, D = x2.shape
    nt = T // tm
    row = lambda b, t: (b * nt + t, 0)
    return pl.pallas_call(
        _final_kernel,
        grid=(B, nt),
        in_specs=[pl.BlockSpec((tm, D), row), pl.BlockSpec((tm, D), row),
                  pl.BlockSpec((tm, D), lambda b, t: (N // tm + b * nt + t, 0)),
                  pl.BlockSpec((tm, TOP_K), row), pl.BlockSpec((B, 6, D), lambda b, t: (0, 0, 0))],
        out_specs=pl.BlockSpec((tm, D), row),
        out_shape=jax.ShapeDtypeStruct((N, D), F32),
        compiler_params=_params("arbitrary", "arbitrary"),
        name="moe_combine",
    )(x2, ybuf, ybuf, wts, mod)


def _overlap_t(n_sel, n_cmp_pad):
    ci = jnp.arange(n_cmp_pad)[None, :] * CMP_STRIDE
    sj = jnp.arange(n_sel)[:, None] * SEL_BLOCK
    ov = (ci <= sj + SEL_BLOCK - 1) & (ci + CMP_BLOCK - 1 >= sj) & (jnp.arange(n_cmp_pad)[None, :] < n_cmp_pad - 1)
    return ov.astype(BF16)


def kernel(x, c, w_ada, b_ada, norm_g, w_in, b_in, rwkv_mu, rwkv_w0, rwkv_w2, rwkv_a0, rwkv_a2, rwkv_g2,
           rwkv_k_k, rwkv_k_a, rwkv_r_k, rwkv_gn_g, rwkv_gn_b, qk_norm_g, cmp_pos, cmp_w1, cmp_w2,
           w_up_rwkv, w_up_nsa, w_out, router_w, router_b, exp_w_gate, exp_w_up, exp_w_down):
    B, T, D = x.shape
    L = w_ada.shape[0]
    N = B * T
    mods = _ada(c, w_ada, b_ada)
    tables = _rope_tables(jnp.arange(T, dtype=jnp.int32))
    nch = T // CMP_STRIDE
    tables_cmp = _rope_tables(jnp.arange(nch, dtype=jnp.int32) * CMP_STRIDE + CMP_BLOCK - 1)
    ov_t = _overlap_t(T // SEL_BLOCK, nch)
    n_gate = NSA_GATE_COLS
    x2 = x.reshape(N, D)
    pending_moe = None
    for l in range(L):
        g0 = _SEG_KV[1] + n_gate
        w_pad = jnp.concatenate([w_in[l][:, :g0], jnp.zeros((D, GATE_PAD - n_gate), F32), w_in[l][:, g0:]],
                                axis=1).astype(BF16)
        b_pad = jnp.concatenate([b_in[l][:g0], jnp.zeros((GATE_PAD - n_gate,), F32), b_in[l][g0:]]).reshape(1, -1)
        outs = _inproj(x2, pending_moe, mods[l], norm_g[l, 0].reshape(1, D), w_pad, b_pad, B, T)
        if pending_moe is not None:
            x2, outs = outs[0], outs[1:]
        p_rw, p_q, p_kv, p_gate, p_merge = outs
        r, k, v, al, bb, ld, g, bonus = _rwkv_pre(p_rw, rwkv_mu[l], rwkv_w0[l], rwkv_w2[l], rwkv_a0[l],
                                                  rwkv_a2[l], rwkv_g2[l], rwkv_k_k[l], rwkv_k_a[l],
                                                  rwkv_r_k[l], B, T)
        ys = _rwkv_scan(r, k, v, al, bb, ld, B, T)
        qt, ks, kw, vst, vsd, vwt = _nsa_prep(p_q, p_kv, tables, qk_norm_g[l], B, T)
        kv3 = p_kv.reshape(B, T, KV_COLS)
        kcmp = _nsa_cmp(kv3, 0, cmp_pos[l], cmp_w1[l], cmp_w2[l], qk_norm_g[l, 1], tables_cmp)
        vct = _nsa_cmp(kv3, 1, cmp_pos[l], cmp_w1[l], cmp_w2[l], None, None)
        gt = p_gate[:, :n_gate].reshape(B, T, NSA_KV_HEADS, NSA_GROUP, 3).transpose(0, 2, 4, 3, 1)
        yb = _nsa_attn(qt, kcmp, vct, ks.reshape(B, T, 2 * LANES), vst, vsd, kw.reshape(B, T, LANES), vwt, gt, ov_t,
                       B, T)
        x2, h2, route, totals = _merge(ys, g, bonus, rwkv_gn_g[l], rwkv_gn_b[l], yb, p_merge, x2, mods[l],
                               norm_g[l, 1].reshape(1, D), w_up_rwkv[l].astype(BF16),
                               w_up_nsa[l].astype(BF16), w_out[l].astype(BF16), router_w, router_b, B, T)
        wts, dest, blk_expert, blk_valid, n_blk = _route(route, totals, N)
        xs = _sc_dispatch(h2, dest, n_blk * MOE_BLOCK)
        ys = _moe_dense(xs, blk_expert, blk_valid, n_blk, l, exp_w_gate, exp_w_up, exp_w_down)
        ybuf = _sc_collect(ys, dest)
        pending_moe = (ybuf, wts, mods[l])
    return _final(x2, *pending_moe, B, T).reshape(B, T, D)
```

```python
import functools
import math

import jax
import jax.numpy as jnp
from jax import lax
from jax.experimental import pallas as pl
from jax.experimental.pallas import tpu as pltpu
from jax.experimental.pallas import tpu_sc as plsc

F32 = jnp.float32
BF16 = jnp.bfloat16
HI = lax.Precision.HIGHEST

D_MODEL = 1024
RWKV_HEADS = 8
HEAD_DIM = 64
RWKV_WIDTH = RWKV_HEADS * HEAD_DIM
DECAY_LORA = 64
ICLR_LORA = 64
GATE_LORA = 128
RWKV_GN_EPS = 64e-5
RWKV_COLS = 3 * RWKV_WIDTH + DECAY_LORA + ICLR_LORA + GATE_LORA

NSA_Q_HEADS = 8
NSA_KV_HEADS = 2
NSA_GROUP = NSA_Q_HEADS // NSA_KV_HEADS
NSA_WIDTH = NSA_Q_HEADS * HEAD_DIM
CMP_STRIDE = 16
CMP_BLOCK = 2 * CMP_STRIDE
CMP_HIDDEN = 256
SEL_BLOCK = 64
SEL_SHIFT = 6
SEL_TOPK = 16
WINDOW = 512
FORCE_SCORE = 1e4
NEG_INF = -1e30
ROPE_THETA = 500000.0
ROPE_DIM = HEAD_DIM // 4
KV_COLS = 6 * NSA_KV_HEADS * HEAD_DIM
NSA_GATE_COLS = 3 * NSA_Q_HEADS
GATE_PAD = 128

N_EXPERTS = 16
N_GROUPS = 4
EXPERTS_PER_GROUP = N_EXPERTS // N_GROUPS
TOP_K = 2
D_EXPERT = 512
MOE_BLOCK = 256
NORM_EPS = 1e-6

LANES = 128
CHUNK = 64
KEY_TILE = 128
SEL_TILE = 512
CMP_VARIANTS = 4
V_ROWS = 80
Q_SCALE = HEAD_DIM ** -0.5 * math.log2(math.e)
Q_TILE = 128
F32_TINY = float(jnp.finfo(jnp.float32).tiny)

_SEG_RW = (0, RWKV_COLS)
_SEG_Q = (_SEG_RW[1], _SEG_RW[1] + NSA_WIDTH)
_SEG_KV = (_SEG_Q[1], _SEG_Q[1] + KV_COLS)
_SEG_GATE = (_SEG_KV[1], _SEG_KV[1] + GATE_PAD)
_SEG_MERGE = (_SEG_GATE[1], _SEG_GATE[1] + 2 * D_MODEL)
IN_COLS_PAD = _SEG_MERGE[1]

_VMEM_LIMIT = 56 * 1024 * 1024


def _dot(a, b, precision=None):
    return jnp.dot(a, b, preferred_element_type=F32, precision=precision)


def _dot_tb(a, b, precision=None):
    return lax.dot_general(a, b, (((1,), (1,)), ((), ())), preferred_element_type=F32,
                           precision=precision)


def _dot_ta(a, b, precision=None):
    return lax.dot_general(a, b, (((0,), (0,)), ((), ())), preferred_element_type=F32,
                           precision=precision)


def _split_bf16(x, terms):
    parts = []
    for _ in range(terms - 1):
        parts.append(x.astype(BF16))
        x = x - parts[-1].astype(F32)
    parts.append(x.astype(BF16))
    return parts


def _dot_split_lhs(x, w_bf, terms=2):
    return functools.reduce(jnp.add, [_dot(p, w_bf) for p in _split_bf16(x, terms)])


def _dot_split_rhs(w_bf, x, terms=2):
    return functools.reduce(jnp.add, [_dot(w_bf, p) for p in _split_bf16(x, terms)])


def _dot_3pass(x, w_hl_ref):
    x_hi, x_lo = _split_bf16(x, 2)
    w_hi = w_hl_ref[0]
    return _dot(x_hi, w_hi) + _dot(x_lo, w_hi) + _dot(x_hi, w_hl_ref[1])


def _pack_bf16_pairs(x):
    n = x.shape[1] // 2
    bits = lax.bitcast_convert_type(x.astype(BF16).astype(F32), jnp.uint32)
    return lax.bitcast_convert_type(bits[:, 0:n] | (bits[:, n:] >> 16), F32)


def _unpack_bf16_pairs(packed):
    bits = lax.bitcast_convert_type(packed, jnp.uint32)
    lo = lax.bitcast_convert_type(bits & jnp.uint32(0xFFFF0000), F32)
    hi = lax.bitcast_convert_type(bits << 16, F32)
    return lo.astype(BF16), hi.astype(BF16)


def _hi_lo(w):
    hi = w.astype(BF16)
    return jnp.stack([hi, (w - hi.astype(F32)).astype(BF16)])


def _params(*sem):
    return pltpu.CompilerParams(dimension_semantics=sem, vmem_limit_bytes=_VMEM_LIMIT)


def _sigmoid(x):
    return 1.0 / (1.0 + jnp.exp(-x))


def _ada_kernel(c_ref, w_ref, b_ref, o_ref):
    c = c_ref[...]
    s = c * _sigmoid(c)
    o_ref[0] = _dot(s, w_ref[0], HI) + b_ref[0]


def _ada(c, w_ada, b_ada):
    L, D, D6 = w_ada.shape
    B = c.shape[0]
    rows = 8
    cp = jnp.zeros((rows, D), F32).at[:B].set(c)
    tn = 1536
    out = pl.pallas_call(
        _ada_kernel,
        grid=(L, D6 // tn),
        in_specs=[pl.BlockSpec((rows, D), lambda l, j: (0, 0)),
                  pl.BlockSpec((1, D, tn), lambda l, j: (l, 0, j)),
                  pl.BlockSpec((1, 1, tn), lambda l, j: (l, 0, j))],
        out_specs=pl.BlockSpec((1, rows, tn), lambda l, j: (l, 0, j)),
        out_shape=jax.ShapeDtypeStruct((L, rows, D6), F32),
        compiler_params=_params("arbitrary", "arbitrary"),
        name="ada_mod",
    )(cp, w_ada, b_ada.reshape(L, 1, D6))
    return out[:, :B].reshape(L, B, 6, D)


def _moe_residual(x_ref, y0_ref, y1_ref, w_ref, m_prev):
    w = w_ref[...]
    rows = lambda y_ref: jnp.concatenate([h.astype(F32) for h in _unpack_bf16_pairs(y_ref[...])], axis=1)
    return x_ref[...] + m_prev[5:6] * (w[:, 0:1] * rows(y0_ref) + w[:, 1:2] * rows(y1_ref))


def _inproj_kernel(*refs, after_moe):
    if after_moe:
        x_ref, y0_ref, y1_ref, wts_ref, modp_ref, mod_ref, g_ref, w_ref, b_ref, o_x = refs[:10]
        x = _moe_residual(x_ref, y0_ref, y1_ref, wts_ref, modp_ref[pl.program_id(0)])
        o_x[...] = x
    else:
        x_ref, mod_ref, g_ref, w_ref, b_ref = refs[:5]
        x = x_ref[...]
    o_rw, o_q, o_kv, o_gate, o_merge = refs[-5:]
    m = mod_ref[pl.program_id(0)]
    ms = jnp.mean(x * x, axis=-1, keepdims=True)
    h = x * lax.rsqrt(ms + NORM_EPS) * g_ref[...]
    h = h * (1.0 + m[1:2]) + m[0:1]
    hb = h.astype(BF16)
    for o, (a, e) in ((o_rw, _SEG_RW), (o_q, _SEG_Q), (o_kv, _SEG_KV), (o_gate, _SEG_GATE),
                      (o_merge, _SEG_MERGE)):
        o[...] = _dot(hb, w_ref[:, a:e]) + b_ref[:, a:e]


def _inproj(x2, pending_moe, mod, g, w_pad, b_pad, B, T, tm=512):
    N, D = x2.shape
    nt = T // tm
    row = lambda b, t: (b * nt + t, 0)
    mods_spec = pl.BlockSpec((B, 6, D), lambda b, t: (0, 0, 0))
    widths = [e - a for a, e in (_SEG_RW, _SEG_Q, _SEG_KV, _SEG_GATE, _SEG_MERGE)]
    in_specs = [pl.BlockSpec((tm, D), row)]
    args = [x2]
    if pending_moe is not None:
        ybuf, wts, mod_prev = pending_moe
        in_specs += [pl.BlockSpec((tm, D // 2), row),
                     pl.BlockSpec((tm, D // 2), lambda b, t: (N // tm + b * nt + t, 0)),
                     pl.BlockSpec((tm, TOP_K), row), mods_spec]
        args += [ybuf, ybuf, wts, mod_prev]
        widths = [D] + widths
    in_specs += [mods_spec, pl.BlockSpec((1, D), lambda b, t: (0, 0)),
                 pl.BlockSpec((D, IN_COLS_PAD), lambda b, t: (0, 0)),
                 pl.BlockSpec((1, IN_COLS_PAD), lambda b, t: (0, 0))]
    return pl.pallas_call(
        functools.partial(_inproj_kernel, after_moe=pending_moe is not None),
        grid=(B, nt),
        in_specs=in_specs,
        out_specs=[pl.BlockSpec((tm, w), row) for w in widths],
        out_shape=[jax.ShapeDtypeStruct((N, w), F32) for w in widths],
        compiler_params=_params("arbitrary", "arbitrary"),
        name="in_proj",
    )(*args, mod, g, w_pad, b_pad)


def _rwkv_pre_kernel(p_ref, mu_ref, w0_ref, w2_ref, a0_ref, a2_ref, g2_ref, kk_ref, ka_ref, rk_ref,
                     bd_ref, o_r, o_k, o_v, o_al, o_b, o_ld, o_g, o_bonus, carry_ref):
    W = RWKV_WIDTH

    @pl.when(pl.program_id(1) == 0)
    def _():
        carry_ref[...] = jnp.zeros_like(carry_ref)

    p = p_ref[...]
    ts = p.shape[0]
    rows = lax.broadcasted_iota(jnp.int32, p.shape, 0)
    shifted = jnp.where(rows == 0, carry_ref[0:1, :], pltpu.roll(p, 1, 0))
    carry_ref[0:1, :] = p[ts - 1:ts, :]
    pm = p + (shifted - p) * mu_ref[...]
    r = pm[:, 0:W]
    k = pm[:, W:2 * W]
    v = pm[:, 2 * W:3 * W]
    wa = pm[:, 3 * W:3 * W + DECAY_LORA + ICLR_LORA]
    gl = pm[:, 3 * W + DECAY_LORA + ICLR_LORA:]
    xw = w0_ref[...] + _dot_3pass(jnp.tanh(wa), w2_ref)
    ld = -math.exp(-0.5) * _sigmoid(xw)
    a = _sigmoid(a0_ref[...] + _dot_3pass(wa, a2_ref))
    g = _dot_3pass(_sigmoid(gl), g2_ref)
    bd = bd_ref[...]
    kk = k * kk_ref[...]
    nrm = jnp.sqrt(_head_sums(kk * kk, bd))
    kk = kk / jnp.maximum(nrm, 1e-12)
    k2 = k * (1.0 + (a - 1.0) * ka_ref[...])
    bonus = _head_sums(r * k2 * rk_ref[...], bd) * v
    o_r[...] = r
    o_k[...] = k2
    o_v[...] = v
    o_al[...] = kk
    o_b[...] = -kk * a
    o_ld[...] = ld
    o_g[...] = g
    o_bonus[...] = bonus


HEAD_SUM_WIDTH = 256


def _head_sums(x, bd):
    w = bd.shape[0]
    parts = [_dot_split_lhs(x[:, j:j + w], bd) for j in range(0, x.shape[1], w)]
    return parts[0] if len(parts) == 1 else jnp.concatenate(parts, axis=1)


def _head_block_diag(width, scale=1.0):
    i = jnp.arange(width) // HEAD_DIM
    return ((i[:, None] == i[None, :]).astype(F32) * scale).astype(BF16)


def _rwkv_pre(p_rw, mu, w0, w2, a0, a2, g2, k_k, k_a, r_k, B, T, ts=512):
    N = p_rw.shape[0]
    W = RWKV_WIDTH
    nt = T // ts
    row = lambda b, t: (b * nt + t, 0)
    zl = jnp.zeros((DECAY_LORA, W), F32)
    w2p = jnp.concatenate([w2, zl], axis=0)
    a2p = jnp.concatenate([zl, a2], axis=0)
    full = lambda shape: pl.BlockSpec(shape, lambda b, t: (0,) * len(shape))
    vec = lambda z: z.reshape(1, -1)
    return pl.pallas_call(
        _rwkv_pre_kernel,
        grid=(B, nt),
        in_specs=[pl.BlockSpec((ts, RWKV_COLS), row), full((1, RWKV_COLS)), full((1, W)),
                  full((2, 2 * DECAY_LORA, W)), full((1, W)), full((2, 2 * DECAY_LORA, W)),
                  full((2, GATE_LORA, W)), full((1, W)), full((1, W)), full((1, W)),
                  full((HEAD_SUM_WIDTH, HEAD_SUM_WIDTH))],
        out_specs=[pl.BlockSpec((ts, W), row)] * 8,
        out_shape=[jax.ShapeDtypeStruct((N, W), F32)] * 8,
        scratch_shapes=[pltpu.VMEM((8, RWKV_COLS), F32)],
        compiler_params=_params("arbitrary", "arbitrary"),
        name="rwkv_pre",
    )(p_rw, vec(mu), vec(w0), _hi_lo(w2p), vec(a0), _hi_lo(a2p), _hi_lo(g2), vec(k_k), vec(k_a), vec(r_k),
      _head_block_diag(HEAD_SUM_WIDTH))


def _bf(x):
    return x.astype(BF16)


def _scan_local(chunks, eye, strict, incl, m0, m1, between_stages=lambda: None):
    C = CHUNK
    n = range(len(chunks))
    st = lambda z: jnp.concatenate([z * m0, z * m1], axis=0)
    zero = jnp.zeros((2 * C, 2 * C), F32)
    at_b, rt_s, vs, vs_b, lhs_a, rhs_a, bk_t, dcol = [], [], [], [], [], [], [], []
    for r, k, v, al, bb, ld, cum in chunks:
        tot = cum[C - 1:C, :]
        dinv = jnp.exp(-cum)
        dend = jnp.exp(tot - cum)
        at_b.append(_bf(st(al * jnp.exp(cum - ld))))
        rt_s.append(st(r * jnp.exp(cum)))
        vs.append(st(v))
        vs_b.append(_bf(vs[-1]))
        lhs_a.append(jnp.concatenate([at_b[-1], _bf(rt_s[-1])], axis=0))
        rhs_a.append(_bf(jnp.concatenate([st(bb * dinv), st(k * dinv)], axis=0)))
        bk_t.append(_bf(jnp.concatenate([st(bb * dend).T, st(k * dend).T], axis=1)))
        dcol.append(jnp.sum(eye * jnp.exp(tot), axis=1, keepdims=True))
    between_stages()
    A = [_dot_tb(lhs_a[i], rhs_a[i]) for i in n]
    between_stages()
    a_ab = [jnp.where(strict, A[i][0:2 * C, 0:2 * C], zero) for i in n]
    a_ak = [_bf(jnp.where(strict, A[i][0:2 * C, 2 * C:4 * C], zero)) for i in n]
    a_r = [_bf(jnp.concatenate([jnp.where(incl, A[i][2 * C:4 * C, 0:2 * C], zero),
                                jnp.where(incl, A[i][2 * C:4 * C, 2 * C:4 * C], zero)], axis=1)) for i in n]
    akv = [_bf(_dot(a_ak[i], vs_b[i])) for i in n]
    between_stages()
    tinv = [eye + a_ab[i] for i in n]
    pw_b = [_bf(a_ab[i]) for i in n]
    pw_b = [_bf(_dot(pw_b[i], pw_b[i])) for i in n]
    between_stages()
    for step in range(5):
        rhs = [jnp.concatenate([pw_b[i], _bf(tinv[i])], axis=1) for i in n]
        if step == 4:
            rhs = [_bf(tinv[i]) for i in n]
        prod = [_dot(pw_b[i], rhs[i]) for i in n]
        tinv = [tinv[i] + prod[i][:, -2 * C:] for i in n]
        pw_b = [_bf(prod[i][:, 0:2 * C]) for i in n]
        between_stages()
    X = [_dot(_bf(tinv[i]), jnp.concatenate([at_b[i], akv[i]], axis=1)) for i in n]
    between_stages()
    w_b = [_bf(X[i][:, 0:LANES]) for i in n]
    uv0 = [jnp.concatenate([_bf(X[i][:, LANES:2 * LANES]), vs_b[i]], axis=0) for i in n]
    m_h = [_bf(_dot(bk_t[i][:, 0:2 * C], w_b[i])) for i in n]
    g_h = [_dot(bk_t[i], uv0[i]) for i in n]
    between_stages()
    q_h = [_bf(rt_s[i] + _dot(a_r[i][:, 0:2 * C], w_b[i])) for i in n]
    y0 = [_dot(a_r[i], uv0[i]) for i in n]
    return [(m_h[i], g_h[i], dcol[i], q_h[i], y0[i]) for i in n]


def _rwkv_scan_kernel(r_ref, k_ref, v_ref, al_ref, b_ref, ld_ref, o_ref, h_ref, *local_refs):
    C = CHUNK
    tc = r_ref.shape[0]
    nc = tc // C

    @pl.when(pl.program_id(2) == 0)
    def _():
        h_ref[...] = jnp.zeros_like(h_ref)
        for ref in local_refs:
            ref[...] = jnp.zeros_like(ref)

    seq = {"H": h_ref[...], "c": 0}

    def one_step():
        c = seq["c"]
        if c < nc:
            m_h, g_h, dcol, q_h, y0 = (ref[c] for ref in local_refs)
            h_b = _bf(seq["H"])
            Y = _dot(q_h, h_b) + y0
            o_ref[c * C:(c + 1) * C, :] = Y[0:C] + Y[C:2 * C]
            seq["H"] = dcol * seq["H"] + _dot(m_h, h_b) + g_h
            seq["c"] = c + 1

    tri = jnp.where(lax.broadcasted_iota(jnp.int32, (C, C), 1) <= lax.broadcasted_iota(jnp.int32, (C, C), 0),
                    1.0, 0.0).astype(BF16)
    r2 = lax.broadcasted_iota(jnp.int32, (2 * C, 2 * C), 0)
    c2 = lax.broadcasted_iota(jnp.int32, (2 * C, 2 * C), 1)
    eye = (r2 == c2).astype(F32)
    strict = (c2 & (C - 1)) < (r2 & (C - 1))
    incl = (c2 & (C - 1)) <= (r2 & (C - 1))
    lane = lax.broadcasted_iota(jnp.int32, (C, LANES), 1)
    m0 = (lane < HEAD_DIM).astype(F32)
    m1 = 1.0 - m0
    cum = _dot_split_rhs(tri, jnp.concatenate([ld_ref[c * C:(c + 1) * C, :] for c in range(nc)], axis=1), 3)
    chunks = []
    for c in range(nc):
        sl = slice(c * C, (c + 1) * C)
        chunks.append((r_ref[sl, :], k_ref[sl, :], v_ref[sl, :], al_ref[sl, :], b_ref[sl, :], ld_ref[sl, :],
                       cum[:, c * LANES:(c + 1) * LANES]))
    local = _scan_local(chunks, eye, strict, incl, m0, m1, between_stages=one_step)
    while seq["c"] < nc:
        one_step()
    h_ref[...] = seq["H"]
    for c, parts in enumerate(local):
        for ref, part in zip(local_refs, parts):
            ref[c] = part


def _rwkv_scan(r, k, v, al, bb, ld, B, T, tc=512):
    N, W = r.shape
    nt = T // tc
    nc = tc // CHUNK
    in_spec = pl.BlockSpec((tc, LANES), lambda b, h, t: (b * nt + jnp.minimum(t, nt - 1), h))
    out_spec = pl.BlockSpec((tc, LANES), lambda b, h, t: (b * nt + jnp.maximum(t - 1, 0), h))
    sq = (nc, LANES, LANES)
    return pl.pallas_call(
        _rwkv_scan_kernel,
        grid=(B, W // LANES, nt + 1),
        in_specs=[in_spec] * 6,
        out_specs=out_spec,
        out_shape=jax.ShapeDtypeStruct((N, W), F32),
        scratch_shapes=[pltpu.VMEM((LANES, LANES), F32), pltpu.VMEM(sq, BF16), pltpu.VMEM(sq, F32),
                        pltpu.VMEM((nc, LANES, 1), F32), pltpu.VMEM(sq, BF16), pltpu.VMEM(sq, F32)],
        compiler_params=_params("arbitrary", "arbitrary", "arbitrary"),
        name="rwkv_scan",
    )(r, k, v, al, bb, ld)


def _rope_tables(pos):
    half = ROPE_DIM // 2
    inv = jnp.power(ROPE_THETA, -jnp.arange(half, dtype=F32) * 2.0 / ROPE_DIM)
    ang = pos.astype(F32)[:, None] * inv[None, :]
    cos, sin = jnp.cos(ang), jnp.sin(ang)
    n = pos.shape[0]
    rest = HEAD_DIM - ROPE_DIM
    c = jnp.concatenate([cos, cos, jnp.ones((n, rest), F32)], axis=1)
    s_dn = jnp.concatenate([-sin, jnp.zeros((n, half + rest), F32)], axis=1)
    s_up = jnp.concatenate([jnp.zeros((n, half), F32), sin, jnp.zeros((n, rest), F32)], axis=1)
    rep = LANES // HEAD_DIM
    return jnp.tile(c, (1, rep)), jnp.tile(s_dn, (1, rep)), jnp.tile(s_up, (1, rep))


def _norm_rope(x, bd, g, c, s_dn, s_up):
    width = x.shape[1]
    half = ROPE_DIM // 2
    rep = width // LANES
    tile = (lambda z: jnp.concatenate([z] * rep, axis=1)) if rep > 1 else (lambda z: z)
    ms = _head_sums(x * x, bd)
    xn = x * lax.rsqrt(ms + NORM_EPS) * g
    return (xn * tile(c) + pltpu.roll(xn, width - half, 1) * tile(s_dn)
            + pltpu.roll(xn, half, 1) * tile(s_up))


def _nsa_prep_kernel(q_ref, kv_ref, c_ref, sd_ref, su_ref, gq_ref, gs_ref, gw_ref, bdq_ref, bdk_ref,
                     o_qt, o_ks, o_kw, o_vst, o_vsd, o_vwt):
    c, sd, su = c_ref[...], sd_ref[...], su_ref[...]
    q = _norm_rope(q_ref[...], bdq_ref[...], gq_ref[...], c, sd, su) * Q_SCALE
    qt = q.T
    ts = q.shape[0]
    kv = kv_ref[...]
    bdk = bdk_ref[...]
    pos = pl.program_id(1) * ts + lax.broadcasted_iota(jnp.int32, (ts, LANES), 0)
    blk_onehot = jnp.where((pos >> SEL_SHIFT) == lax.broadcasted_iota(jnp.int32, (ts, LANES), 1), 1.0, 0.0)
    ks = _norm_rope(kv[:, 2 * LANES:3 * LANES], bdk, gs_ref[...], c, sd, su)
    o_ks[...] = jnp.concatenate([ks, blk_onehot], axis=1).astype(BF16)
    o_kw[...] = _norm_rope(kv[:, 4 * LANES:5 * LANES], bdk, gw_ref[...], c, sd, su).astype(BF16)
    ones_rows = jnp.where(lax.broadcasted_iota(jnp.int32, (V_ROWS - HEAD_DIM, q.shape[0]), 0) == 0, 1.0, 0.0)

    def values_t(x):
        xt = x.T
        return jnp.concatenate([xt[0:HEAD_DIM], ones_rows, xt[HEAD_DIM:2 * HEAD_DIM], ones_rows], axis=0)

    vst = values_t(kv[:, 3 * LANES:4 * LANES])
    vwt = values_t(kv[:, 5 * LANES:6 * LANES])
    for j in range(q.shape[0] // KEY_TILE):
        sl = slice(j * KEY_TILE, (j + 1) * KEY_TILE)
        o_qt[0, j] = qt[:, sl].astype(BF16)
        o_vsd[0, j] = vst[:, sl].astype(BF16)
        o_vwt[0, j] = vwt[:, sl].astype(BF16)
    for j in range(q.shape[0] // SEL_TILE):
        o_vst[0, j] = vst[:, j * SEL_TILE:(j + 1) * SEL_TILE].astype(BF16)


def _nsa_prep(q, kv, tables, qk_g, B, T, ts=512):
    N = q.shape[0]
    nt = T // ts
    nk = ts // KEY_TILE
    ns = ts // SEL_TILE
    row = lambda b, t: (b * nt + t, 0)
    full = lambda shape: pl.BlockSpec(shape, lambda b, t: (0,) * len(shape))
    tab = pl.BlockSpec((ts, LANES), lambda b, t: (t, 0))
    gq = jnp.tile(qk_g[0], NSA_Q_HEADS).reshape(1, NSA_WIDTH)
    gs = jnp.tile(qk_g[2], NSA_KV_HEADS).reshape(1, LANES)
    gw = jnp.tile(qk_g[3], NSA_KV_HEADS).reshape(1, LANES)
    tiled = lambda rows: pl.BlockSpec((1, nk, rows, KEY_TILE), lambda b, t: (b, t, 0, 0))
    return pl.pallas_call(
        _nsa_prep_kernel,
        grid=(B, nt),
        in_specs=[pl.BlockSpec((ts, NSA_WIDTH), row), pl.BlockSpec((ts, KV_COLS), row), tab, tab, tab,
                  full((1, NSA_WIDTH)), full((1, LANES)), full((1, LANES)),
                  full((HEAD_SUM_WIDTH, HEAD_SUM_WIDTH)), full((LANES, LANES))],
        out_specs=[tiled(NSA_WIDTH), pl.BlockSpec((ts, 2 * LANES), row), pl.BlockSpec((ts, LANES), row),
                   pl.BlockSpec((1, ns, NSA_KV_HEADS * V_ROWS, SEL_TILE), lambda b, t: (b, t, 0, 0)),
                   tiled(NSA_KV_HEADS * V_ROWS), tiled(NSA_KV_HEADS * V_ROWS)],
        out_shape=[jax.ShapeDtypeStruct((B, T // KEY_TILE, NSA_WIDTH, KEY_TILE), BF16),
                   jax.ShapeDtypeStruct((N, 2 * LANES), BF16), jax.ShapeDtypeStruct((N, LANES), BF16),
                   jax.ShapeDtypeStruct((B, T // SEL_TILE, NSA_KV_HEADS * V_ROWS, SEL_TILE), BF16),
                   jax.ShapeDtypeStruct((B, T // KEY_TILE, NSA_KV_HEADS * V_ROWS, KEY_TILE), BF16),
                   jax.ShapeDtypeStruct((B, T // KEY_TILE, NSA_KV_HEADS * V_ROWS, KEY_TILE), BF16)],
        compiler_params=_params("arbitrary", "arbitrary"),
        name="nsa_prep",
    )(q, kv, *tables, gq, gs, gw, _head_block_diag(HEAD_SUM_WIDTH, 1.0 / HEAD_DIM),
      _head_block_diag(LANES, 1.0 / HEAD_DIM))


def _gelu_tanh(x):
    return 0.5 * x * (1.0 + jnp.tanh(0.7978845608028654 * (x + 0.044715 * x * x * x)))


def _nsa_cmp_kernel(x_ref, pos_ref, w1_ref, w2_ref, *rest, is_key):
    if is_key:
        g_ref, c_ref, sd_ref, su_ref, bd_ref, o_ref, xs_ref = rest
    else:
        o_ref, xs_ref = rest
    nch = xs_ref.shape[0]
    S = CMP_STRIDE
    for j in range(S):
        xs_ref[:, j * LANES:(j + 1) * LANES] = x_ref[0, pl.ds(j, nch, stride=S), :]
    xs = xs_ref[...]
    first = _dot((xs + pos_ref[0:1, :]).astype(BF16), w1_ref[0])
    second = _dot((xs + pos_ref[1:2, :]).astype(BF16), w1_ref[1])
    hid = first + pltpu.roll(second, nch - 1, 0)
    out = _dot(_gelu_tanh(hid).astype(BF16), w2_ref[...])
    rows = lax.broadcasted_iota(jnp.int32, out.shape, 0)
    if is_key:
        out = _norm_rope(out, bd_ref[...], g_ref[...], c_ref[...], sd_ref[...], su_ref[...])
        o_ref[0] = jnp.where(rows < nch - 1, out, 0.0).astype(BF16)
    else:
        o_ref[0] = jnp.where(rows < nch - 1, out, 0.0).T.astype(BF16)


def _nsa_cmp(kv3, which, cmp_pos, cmp_w1, cmp_w2, g_k, tables_cmp):
    B, T, _ = kv3.shape
    S = CMP_STRIDE
    nch = T // S
    is_key = which == 0
    eye2 = jnp.eye(NSA_KV_HEADS, dtype=F32)
    w1 = cmp_w1[which].reshape(CMP_BLOCK, HEAD_DIM, CMP_HIDDEN)
    w1 = jnp.einsum('jdh,ge->jgdeh', w1, eye2).reshape(2, S * LANES, NSA_KV_HEADS * CMP_HIDDEN)
    w2 = jnp.einsum('hd,ge->ghed', cmp_w2[which], eye2).reshape(NSA_KV_HEADS * CMP_HIDDEN, LANES)
    pos = jnp.tile(cmp_pos[which].reshape(2, S, 1, HEAD_DIM), (1, 1, NSA_KV_HEADS, 1)).reshape(2, S * LANES)
    full = lambda shape: pl.BlockSpec(shape, lambda b: (0,) * len(shape))
    in_specs = [pl.BlockSpec((1, T, LANES), lambda b: (b, 0, which)), full(pos.shape), full(w1.shape),
                full(w2.shape)]
    args = [kv3, pos, w1.astype(BF16), w2.astype(BF16)]
    if is_key:
        in_specs += [full((1, LANES)), full((nch, LANES)), full((nch, LANES)), full((nch, LANES)),
                     full((LANES, LANES))]
        args += [jnp.tile(g_k, NSA_KV_HEADS).reshape(1, LANES), *tables_cmp,
                 _head_block_diag(LANES, 1.0 / HEAD_DIM)]
        out_spec = pl.BlockSpec((1, nch, LANES), lambda b: (b, 0, 0))
        out_shape = jax.ShapeDtypeStruct((B, nch, LANES), BF16)
    else:
        out_spec = pl.BlockSpec((1, LANES, nch), lambda b: (b, 0, 0))
        out_shape = jax.ShapeDtypeStruct((B, LANES, nch), BF16)
    return pl.pallas_call(
        functools.partial(_nsa_cmp_kernel, is_key=is_key),
        grid=(B,),
        in_specs=in_specs,
        out_specs=out_spec,
        out_shape=out_shape,
        scratch_shapes=[pltpu.VMEM((nch, S * LANES), F32)],
        compiler_params=_params("arbitrary"),
        name="nsa_cmp_k" if is_key else "nsa_cmp_v",
    )(*args)


def _nsa_attn_kernel(qt_ref, kc_ref, vct_ref, ks_ref, vst_ref, vsd_ref, kw_ref, vwt_ref, gt_ref, ov_ref, o_ref,
                     rhs_ref, oc_ref, keep_ref, s0_ref, s1_ref, s2_ref, s3_ref, p0_ref, p1_ref):
    qb = pl.program_id(1)
    G = NSA_KV_HEADS
    R = NSA_GROUP
    QT = Q_TILE
    KT = KEY_TILE
    CG = R * QT
    NQ = G * CG
    D = HEAD_DIM
    t0 = qb * QT
    n_cmp_pad = kc_ref.shape[1]
    n_sel = ov_ref.shape[0]
    cols = lambda g: slice(g * CG, (g + 1) * CG)

    q_cols = []
    for g in range(G):
        q_g = jnp.concatenate([qt_ref[0, 0, (g * R + r) * D:(g * R + r + 1) * D, :] for r in range(R)], axis=1)
        q_cols.append(jnp.concatenate([q_g if gg == g else jnp.zeros_like(q_g) for gg in range(G)], axis=0))
    qpad = jnp.concatenate(q_cols, axis=1)

    tq_row = t0 + (lax.broadcasted_iota(jnp.int32, (1, NQ), 1) & (QT - 1))
    spread = lambda z: jnp.concatenate([z[:, g * QT:(g + 1) * QT] for g in range(G) for _ in range(R)], axis=1)
    tile_all = lambda z: jnp.concatenate([z] * (G * R), axis=1)

    def values_dot(v_of_group, p):
        return jnp.concatenate([_dot(v_of_group(g), p[:, cols(g)]) for g in range(G)], axis=1)

    NV = CMP_VARIANTS
    nq = ks_ref.shape[1] // QT

    def compressed_and_select(n_c, n_b):
        sc = _dot(kc_ref[0, 0:n_c, :], qpad)
        n_i = lax.broadcasted_iota(jnp.int32, (n_c, 1), 0)
        cend = jnp.where(n_i < n_cmp_pad - 1, n_i * CMP_STRIDE + (CMP_BLOCK - 1), jnp.int32(2 ** 30))
        cvalid = cend <= tq_row
        sc = jnp.where(cvalid, sc, NEG_INF)
        mc = jnp.max(sc, axis=0, keepdims=True)
        ec = jnp.where(cvalid, jnp.exp2(sc - mc), 0.0)
        pc = ec / jnp.maximum(jnp.sum(ec, axis=0, keepdims=True), F32_TINY)
        pc_b = pc.astype(BF16)
        oc_ref[...] = values_dot(lambda g: vct_ref[0, g * D:(g + 1) * D, 0:n_c], pc_b)
        sums = []
        for g in range(G):
            acc = pc[:, g * CG:g * CG + QT]
            for r in range(1, R):
                acc = acc + pc[:, g * CG + r * QT:g * CG + (r + 1) * QT]
            sums.append(acc)
        imp = _dot_split_rhs(ov_ref[0:n_b, 0:n_c], jnp.concatenate(sums, axis=1))
        jb = lax.broadcasted_iota(jnp.int32, (n_b, G * QT), 0)
        jf = jb.astype(F32)
        tq_b = t0 + (lax.broadcasted_iota(jnp.int32, (n_b, G * QT), 1) & (QT - 1))
        cur = tq_b >> SEL_SHIFT
        forced = (jb == 0) | (jb == cur) | (jb == cur - 1)
        visible = jb * SEL_BLOCK <= tq_b
        score = jnp.where(visible, jnp.where(forced, FORCE_SCORE, imp), -1.0)
        sel = jnp.zeros((n_b, G * QT), F32)
        for _ in range(min(SEL_TOPK, n_b)):
            mx = jnp.max(score, axis=0, keepdims=True)
            jmin = jnp.min(jnp.where(score == mx, jf, 1e9), axis=0, keepdims=True)
            hit = jf == jmin
            sel = jnp.where(hit, 1.0, sel)
            score = jnp.where(hit, -3e38, score)
        keep_ref[0:n_b, :] = jnp.where(visible, sel, 0.0)
        if n_b < n_sel:
            keep_ref[n_b:n_sel, :] = jnp.zeros((n_sel - n_b, G * QT), F32)

    for v in range(NV):
        @pl.when((qb * NV) // nq == v)
        def _():
            compressed_and_select((v + 1) * n_cmp_pad // NV, (v + 1) * n_sel // NV)

    o_c = oc_ref[...]
    ji = lax.broadcasted_iota(jnp.int32, (n_sel, G * QT), 0)

    ST = SEL_TILE
    bias_all = (keep_ref[...] - 1.0) * (-NEG_INF)
    first_own = t0 // SEL_BLOCK
    vrows = lambda g: slice(g * V_ROWS, (g + 1) * V_ROWS)

    def with_bias_rows(bias):
        rows = spread(bias).astype(BF16)
        if n_sel < LANES:
            rows = jnp.concatenate([rows, jnp.zeros((LANES - n_sel, NQ), BF16)], axis=0)
        return jnp.concatenate([qpad, rows], axis=0)

    rhs_ref[...] = with_bias_rows(jnp.where(ji < first_own, bias_all, NEG_INF))
    n_tiles = (t0 + ST - 1) // ST
    last_tile = ks_ref.shape[1] // ST - 1
    p_bufs = (p0_ref, p1_ref)

    def sel_scores(kt, s_ref):
        k0 = pl.multiple_of(jnp.minimum(kt, last_tile) * ST, ST)
        s_ref[...] = _dot(ks_ref[0, pl.ds(k0, ST), :], rhs_ref[...])

    def sel_values(kt, slot, acc, alpha):
        kt = jnp.clip(kt, 0, last_tile)
        return acc * alpha + values_dot(lambda g: vst_ref[0, kt, vrows(g), :], p_bufs[slot][...])

    def sel_softmax(s_ref, slot, m):
        s = s_ref[...]
        m_new = jnp.maximum(m, jnp.max(s, axis=0, keepdims=True))
        p_bufs[slot][...] = jnp.exp2(s - m_new).astype(BF16)
        return m_new, jnp.exp2(m - m_new)

    def sel_pair(a, carry, s_now, s_next):
        m, acc, alpha0, alpha1 = carry
        acc = sel_values(a - 2, 0, acc, alpha0)
        acc = sel_values(a - 1, 1, acc, alpha1)
        sel_scores(a + 2, s_next[0])
        sel_scores(a + 3, s_next[1])
        m, alpha0 = sel_softmax(s_now[0], 0, m)
        m, alpha1 = sel_softmax(s_now[1], 1, m)
        return m, acc, alpha0, alpha1

    bufs_a, bufs_b = (s0_ref, s1_ref), (s2_ref, s3_ref)
    sel_scores(0, s0_ref)
    sel_scores(1, s1_ref)
    p0_ref[...] = jnp.zeros_like(p0_ref)
    p1_ref[...] = jnp.zeros_like(p1_ref)
    own = _dot(ks_ref[0, pl.ds(pl.multiple_of(t0, QT), QT), :], with_bias_rows(bias_all))

    n_wt = (WINDOW + QT) // KT
    k0w = pl.multiple_of(jnp.maximum(t0 - WINDOW, 0), KT)
    kt_w = k0w // KT
    keys_w = kw_ref[0, pl.ds(k0w, WINDOW + QT), :]
    dw = (t0 + lax.broadcasted_iota(jnp.int32, (WINDOW + QT, QT), 1)
          - (k0w + lax.broadcasted_iota(jnp.int32, (WINDOW + QT, QT), 0)))
    sw = _dot(keys_w, qpad) + tile_all(jnp.where(dw >= 0, jnp.where(dw < WINDOW, 0.0, NEG_INF), NEG_INF))
    pw = jnp.exp2(sw - jnp.max(sw, axis=0, keepdims=True)).astype(BF16)
    acc_w = values_dot(lambda g: vwt_ref[0, kt_w, vrows(g), :], pw[0:KT])
    for j in range(1, n_wt):
        acc_w = acc_w + values_dot(lambda g: vwt_ref[0, kt_w + j, vrows(g), :], pw[j * KT:(j + 1) * KT])

    n_pairs = (n_tiles + 1) // 2
    one = jnp.ones((1, NQ), F32)
    m_s, acc_s, alpha0, alpha1 = lax.fori_loop(
        0, n_pairs,
        lambda j, carry: lax.cond(j % 2 == 0,
                                  lambda c: sel_pair(2 * j, c, bufs_a, bufs_b),
                                  lambda c: sel_pair(2 * j, c, bufs_b, bufs_a), carry),
        (jnp.full((1, NQ), NEG_INF, F32), jnp.zeros((V_ROWS, NQ), F32), one, one))
    acc_s = sel_values(2 * n_pairs - 2, 0, acc_s, alpha0)
    acc_s = sel_values(2 * n_pairs - 1, 1, acc_s, alpha1)
    seen = lax.broadcasted_iota(jnp.int32, (QT, QT), 0) <= lax.broadcasted_iota(jnp.int32, (QT, QT), 1)
    own = jnp.where(tile_all(seen), own, NEG_INF)
    m_new = jnp.maximum(m_s, jnp.max(own, axis=0, keepdims=True))
    acc_s = acc_s * jnp.exp2(m_s - m_new) + values_dot(lambda g: vsd_ref[0, qb, vrows(g), :],
                                                       jnp.exp2(own - m_new).astype(BF16))

    gates = _sigmoid(gt_ref[0])
    grow = lambda j: jnp.concatenate([gates[g, j, r:r + 1, :] for g in range(G) for r in range(R)], axis=1)
    o = (grow(0) * o_c + grow(1) * (acc_s[0:D] / acc_s[D:D + 1])
         + grow(2) * (acc_w[0:D] / acc_w[D:D + 1]))
    halves = []
    for h in range(G * R // 2):
        pair = jnp.concatenate([o[:, (2 * h) * QT:(2 * h + 1) * QT],
                                o[:, (2 * h + 1) * QT:(2 * h + 2) * QT]], axis=0)
        halves.append(pair.T)
    o_ref[...] = jnp.concatenate(halves, axis=1)


def _nsa_attn(qt, kcmp, vct, ks3, vst, vsd, kw3, vwt, gt, ov_t, B, T):
    G, R = NSA_KV_HEADS, NSA_GROUP
    nq = T // Q_TILE
    nk = T // KEY_TILE
    nch = kcmp.shape[1]
    n_sel = ov_t.shape[0]
    NQ = G * R * Q_TILE
    assert (T // SEL_TILE) % 2 == 0 and n_sel <= LANES and Q_TILE == KEY_TILE
    assert nq % CMP_VARIANTS == 0 and n_sel % (8 * CMP_VARIANTS) == 0 and nch % (8 * CMP_VARIANTS) == 0
    return pl.pallas_call(
        _nsa_attn_kernel,
        grid=(B, nq),
        in_specs=[pl.BlockSpec((1, 1, NSA_WIDTH, Q_TILE), lambda b, q: (b, q, 0, 0)),
                  pl.BlockSpec((1, nch, LANES), lambda b, q: (b, 0, 0)),
                  pl.BlockSpec((1, G * HEAD_DIM, nch), lambda b, q: (b, 0, 0)),
                  pl.BlockSpec((1, T, 2 * LANES), lambda b, q: (b, 0, 0)),
                  pl.BlockSpec((1, T // SEL_TILE, G * V_ROWS, SEL_TILE), lambda b, q: (b, 0, 0, 0)),
                  pl.BlockSpec((1, nk, G * V_ROWS, KEY_TILE), lambda b, q: (b, 0, 0, 0)),
                  pl.BlockSpec((1, T, LANES), lambda b, q: (b, 0, 0)),
                  pl.BlockSpec((1, nk, G * V_ROWS, KEY_TILE), lambda b, q: (b, 0, 0, 0)),
                  pl.BlockSpec((1, G, 3, R, Q_TILE), lambda b, q: (b, 0, 0, 0, q)),
                  pl.BlockSpec((n_sel, nch), lambda b, q: (0, 0))],
        out_specs=pl.BlockSpec((Q_TILE, NSA_WIDTH), lambda b, q: (b * nq + q, 0)),
        out_shape=jax.ShapeDtypeStruct((B * T, NSA_WIDTH), F32),
        scratch_shapes=[pltpu.VMEM((2 * LANES, NQ), BF16), pltpu.VMEM((HEAD_DIM, NQ), F32),
                        pltpu.VMEM((n_sel, G * Q_TILE), F32),
                        *[pltpu.VMEM((SEL_TILE, NQ), F32)] * 4,
                        *[pltpu.VMEM((SEL_TILE, NQ), BF16)] * 2],
        compiler_params=_params("arbitrary", "arbitrary"),
        name="nsa_attn",
    )(qt, kcmp, vct, ks3, vst, vsd, kw3, vwt, gt, ov_t)


def _first_index_of(vals, target):
    idx = jnp.full_like(target, float(len(vals) - 1))
    for i in range(len(vals) - 2, -1, -1):
        idx = jnp.where(vals[i] == target, float(i), idx)
    return idx


def _pick(vals, idx):
    out = vals[-1]
    for i in range(len(vals) - 2, -1, -1):
        out = jnp.where(idx == float(i), vals[i], out)
    return out


def _route_rows(score, bias):
    E, G, P = N_EXPERTS, N_GROUPS, EXPERTS_PER_GROUP
    sel = score + bias
    s = [sel[e:e + 1, :] for e in range(E)]
    raw = [score[e:e + 1, :] for e in range(E)]
    grp = []
    for gi in range(G):
        a = s[gi * P:(gi + 1) * P]
        best = None
        for i in range(P):
            for j in range(i + 1, P):
                pair = a[i] + a[j]
                best = pair if best is None else jnp.maximum(best, pair)
        grp.append(best)
    gmax = functools.reduce(jnp.maximum, grp)
    g_star = _first_index_of(grp, gmax)
    v = [_pick([s[gi * P + i] for gi in range(G)], g_star) for i in range(P)]
    w = [_pick([raw[gi * P + i] for gi in range(G)], g_star) for i in range(P)]
    i1 = _first_index_of(v, functools.reduce(jnp.maximum, v))
    v2 = [jnp.where(i1 == float(i), -jnp.inf, v[i]) for i in range(P)]
    i2 = _first_index_of(v2, functools.reduce(jnp.maximum, v2))
    w1, w2 = _pick(w, i1), _pick(w, i2)
    tot = w1 + w2
    zero = jnp.zeros_like(tot)
    e1, e2 = g_star * P + i1, g_star * P + i2
    n = score.shape[1]
    eidx = lax.broadcasted_iota(jnp.int32, (E, n), 0).astype(F32)
    oh1, oh2 = jnp.where(eidx == e1, 1.0, 0.0), jnp.where(eidx == e2, 1.0, 0.0)
    earlier = jnp.where(lax.broadcasted_iota(jnp.int32, (n, n), 0) < lax.broadcasted_iota(jnp.int32, (n, n), 1),
                        1.0, 0.0).astype(BF16)
    cnt = _dot(jnp.concatenate([oh1, oh2], axis=0).astype(BF16), earlier)
    rank1 = jnp.sum(oh1 * cnt[0:E], axis=0, keepdims=True)
    rank2 = jnp.sum(oh2 * cnt[E:2 * E], axis=0, keepdims=True)
    lane = lax.broadcasted_iota(jnp.int32, (E, LANES), 1)
    totals = jnp.where(lane == 0, jnp.sum(oh1, axis=1, keepdims=True),
                       jnp.where(lane == 1, jnp.sum(oh2, axis=1, keepdims=True), 0.0))
    return jnp.concatenate([e1, e2, w1 / tot, w2 / tot, rank1, rank2, zero, zero], axis=0), totals


def _merge_kernel(ys_ref, g_ref, bonus_ref, gng_ref, gnb_ref, bd_ref, yb_ref, pm_ref, x_ref, mod_ref,
                  ng_ref, wa_ref, wb_ref, wo_ref, rw_ref, rb_ref, o_x, o_h, o_route, o_tot):
    m = mod_ref[pl.program_id(0)]
    bd = bd_ref[...]
    y = ys_ref[...]
    mean = _head_sums(y, bd)
    yc = y - mean
    var = _head_sums(yc * yc, bd)
    ya = (yc * lax.rsqrt(var + RWKV_GN_EPS) * gng_ref[...] + gnb_ref[...] + bonus_ref[...]) * g_ref[...]
    pm = pm_ref[...]
    D = x_ref.shape[1]
    mix = (_sigmoid(pm[:, 0:D]) * _dot(ya.astype(BF16), wa_ref[...])
           + _sigmoid(pm[:, D:2 * D]) * _dot(yb_ref[...].astype(BF16), wb_ref[...]))
    x = x_ref[...] + m[2:3] * _dot(mix.astype(BF16), wo_ref[...])
    o_x[...] = x
    ms = jnp.mean(x * x, axis=-1, keepdims=True)
    h = x * lax.rsqrt(ms + NORM_EPS) * ng_ref[...]
    h = h * (1.0 + m[4:5]) + m[3:4]
    o_h[...] = _pack_bf16_pairs(h)
    score = _sigmoid(_dot_3pass(h, rw_ref).T[0:N_EXPERTS, :])
    o_route[...], o_tot[...] = _route_rows(score, rb_ref[...])


def _merge(ys, g, bonus, gn_g, gn_b, yb, pm, x2, mod, ng, wa, wb, wo, router_w, router_b, B, T, tm=256):
    N, D = x2.shape
    W = RWKV_WIDTH
    nt = T // tm
    row = lambda b, t: (b * nt + t, 0)
    full = lambda shape: pl.BlockSpec(shape, lambda b, t: (0,) * len(shape))
    return pl.pallas_call(
        _merge_kernel,
        grid=(B, nt),
        in_specs=[pl.BlockSpec((tm, W), row), pl.BlockSpec((tm, W), row), pl.BlockSpec((tm, W), row),
                  full((1, W)), full((1, W)), full((HEAD_SUM_WIDTH, HEAD_SUM_WIDTH)),
                  pl.BlockSpec((tm, NSA_WIDTH), row), pl.BlockSpec((tm, 2 * D), row),
                  pl.BlockSpec((tm, D), row), full((B, 6, D)), full((1, D)),
                  full((W, D)), full((NSA_WIDTH, D)), full((D, D)), full((2, D, LANES)),
                  full((N_EXPERTS, 1))],
        out_specs=[pl.BlockSpec((tm, D), row), pl.BlockSpec((tm, D // 2), row),
                   pl.BlockSpec((8, tm), lambda b, t: (0, b * nt + t)),
                   pl.BlockSpec((N_EXPERTS, LANES), lambda b, t: (b * nt + t, 0))],
        out_shape=[jax.ShapeDtypeStruct((N, D), F32), jax.ShapeDtypeStruct((N, D // 2), F32),
                   jax.ShapeDtypeStruct((8, N), F32), jax.ShapeDtypeStruct((N // tm * N_EXPERTS, LANES), F32)],
        compiler_params=_params("arbitrary", "arbitrary"),
        name="merge_out",
    )(ys, g, bonus, gn_g.reshape(1, W), gn_b.reshape(1, W), _head_block_diag(HEAD_SUM_WIDTH, 1.0 / HEAD_DIM),
      yb, pm, x2, mod, ng, wa, wb, wo,
      _hi_lo(jnp.zeros((D, LANES), F32).at[:, :N_EXPERTS].set(router_w)), router_b.reshape(N_EXPERTS, 1))


def _route(route, totals, N):
    wts = route[TOP_K:2 * TOP_K].T
    NK = N * TOP_K
    E = N_EXPERTS
    n_tiles = totals.shape[0] // E
    expert = route[0:TOP_K].astype(jnp.int32)
    rank = route[2 * TOP_K:3 * TOP_K].astype(jnp.int32)
    per = totals.reshape(n_tiles, E, LANES)[:, :, 0:TOP_K].astype(jnp.int32).transpose(0, 2, 1)
    per = per.reshape(n_tiles * TOP_K, E)
    csum = jnp.cumsum(per, axis=0)
    counts = csum[-1]
    padded = (counts + MOE_BLOCK - 1) // MOE_BLOCK * MOE_BLOCK
    pad_end = jnp.cumsum(padded)
    pad_start = pad_end - padded
    first = (pad_start[None, :] + csum - per).reshape(n_tiles, TOP_K, E).transpose(1, 0, 2)
    first = jnp.repeat(first, N // n_tiles, axis=1)
    mine = expert[:, :, None] == jnp.arange(E, dtype=jnp.int32)[None, None, :]
    dest = (jnp.sum(jnp.where(mine, first, 0), axis=-1) + rank).reshape(-1)
    n_blk = -(-NK // MOE_BLOCK) + N_EXPERTS
    blk_start = jnp.arange(n_blk, dtype=jnp.int32) * MOE_BLOCK
    blk_expert = jnp.sum((pad_end[None, :] <= blk_start[:, None]).astype(jnp.int32), axis=1)
    blk_expert = jnp.clip(blk_expert, 0, N_EXPERTS - 1)
    blk_valid = jnp.clip((pad_start + counts)[blk_expert] - blk_start, 0, MOE_BLOCK).astype(jnp.int32)
    dest = jnp.pad(dest.astype(jnp.int32).reshape(NK // SC_WINDOW, SC_WINDOW), ((0, 0), (0, LANES - SC_WINDOW)))
    return wts, dest, blk_expert, blk_valid, n_blk


SC_WINDOW = 32


def _sc_mesh():
    return plsc.VectorSubcoreMesh(core_axis_name="c", subcore_axis_name="s")


def _sc_dispatch(h, dest, n_slots):
    N, D = h.shape
    W = SC_WINDOW
    nw = N // W

    @pl.kernel(out_type=jax.ShapeDtypeStruct((n_slots, D), h.dtype), mesh=_sc_mesh(), scratch_types=[])
    def dispatch(h_hbm, i_hbm, o_hbm):
        def body(x_vmem, i_vmem):
            pltpu.sync_copy(x_vmem, o_hbm.at[i_vmem.at[0, pl.ds(0, W)]])

        pltpu.emit_pipeline(
            body, grid=(TOP_K, nw),
            in_specs=[pl.BlockSpec((W, D), lambda k, i: (i, 0)),
                      pl.BlockSpec((1, LANES), lambda k, i: (k * nw + i, 0))],
            out_specs=[], core_axis_name=("c", "s"),
            dimension_semantics=(pltpu.PARALLEL, pltpu.PARALLEL))(h_hbm, i_hbm)

    return dispatch(h, dest)


def _sc_collect(ys, dest):
    W = SC_WINDOW
    NK = dest.shape[0] * W
    D = ys.shape[1]
    half = NK // TOP_K // W

    @pl.kernel(out_type=jax.ShapeDtypeStruct((NK, D), ys.dtype), mesh=_sc_mesh(), scratch_types=[])
    def collect(y_hbm, i_hbm, o_hbm):
        def body(i_vmem, o_vmem):
            pltpu.sync_copy(y_hbm.at[i_vmem.at[0, pl.ds(0, W)]], o_vmem)

        pltpu.emit_pipeline(
            body, grid=(TOP_K, half),
            in_specs=[pl.BlockSpec((1, LANES), lambda k, i: (k * half + i, 0))],
            out_specs=[pl.BlockSpec((W, D), lambda k, i: (k * half + i, 0))],
            core_axis_name=("c", "s"),
            dimension_semantics=(pltpu.PARALLEL, pltpu.PARALLEL))(i_hbm, o_hbm)

    return collect(ys, dest)


def _moe_dense_kernel(be_ref, nv_ref, x_ref, wg_ref, wu_ref, wd_ref, o_ref, wg_b, wu_b, wd_b):
    i = pl.program_id(0)
    nv = nv_ref[i]

    @pl.when((i == 0) | (be_ref[i] != be_ref[jnp.maximum(i - 1, 0)]))
    def _():
        wg_b[...] = wg_ref[0, 0].astype(BF16)
        wu_b[...] = wu_ref[0, 0].astype(BF16)
        wd_b[...] = wd_ref[0, 0].astype(BF16)

    @pl.when(nv > 0)
    def _():
        x_lo, x_hi = _unpack_bf16_pairs(x_ref[...])
        half = x_lo.shape[1]
        gate = _dot(x_lo, wg_b[0:half, :]) + _dot(x_hi, wg_b[half:, :])
        up = _dot(x_lo, wu_b[0:half, :]) + _dot(x_hi, wu_b[half:, :])
        o_ref[...] = _pack_bf16_pairs(_dot((gate * _sigmoid(gate) * up).astype(BF16), wd_b[...]))

    @pl.when(nv == 0)
    def _():
        o_ref[...] = jnp.zeros_like(o_ref)


def _moe_dense(xs, blk_expert, blk_valid, n_blk, layer, wg, wu, wd):
    P = xs.shape[0]
    D, DE = wg.shape[2:]
    wmap = lambda i, be, nv: (layer, be[i], 0, 0)
    grid_spec = pltpu.PrefetchScalarGridSpec(
        num_scalar_prefetch=2,
        grid=(n_blk,),
        in_specs=[pl.BlockSpec((MOE_BLOCK, D // 2), lambda i, be, nv: (i, 0)), pl.BlockSpec((1, 1, D, DE), wmap),
                  pl.BlockSpec((1, 1, D, DE), wmap), pl.BlockSpec((1, 1, DE, D), wmap)],
        out_specs=pl.BlockSpec((MOE_BLOCK, D // 2), lambda i, be, nv: (i, 0)),
        scratch_shapes=[pltpu.VMEM((D, DE), BF16), pltpu.VMEM((D, DE), BF16), pltpu.VMEM((DE, D), BF16)],
    )
    return pl.pallas_call(
        _moe_dense_kernel,
        grid_spec=grid_spec,
        out_shape=jax.ShapeDtypeStruct((P, D // 2), F32),
        compiler_params=_params("arbitrary"),
        name="moe_experts",
    )(blk_expert, blk_valid, xs, wg, wu, wd)


def _final_kernel(x_ref, y0_ref, y1_ref, w_ref, mod_ref, o_ref):
    o_ref[...] = _moe_residual(x_ref, y0_ref, y1_ref, w_ref, mod_ref[pl.program_id(0)])


def _final(x2, ybuf, wts, mod, B, T, tm=512):
    N, D = x2.shape
    nt = T // tm
    row = lambda b, t: (b * nt + t, 0)
    return pl.pallas_call(
        _final_kernel,
        grid=(B, nt),
        in_specs=[pl.BlockSpec((tm, D), row), pl.BlockSpec((tm, D // 2), row),
                  pl.BlockSpec((tm, D // 2), lambda b, t: (N // tm + b * nt + t, 0)),
                  pl.BlockSpec((tm, TOP_K), row), pl.BlockSpec((B, 6, D), lambda b, t: (0, 0, 0))],
        out_specs=pl.BlockSpec((tm, D), row),
        out_shape=jax.ShapeDtypeStruct((N, D), F32),
        compiler_params=_params("arbitrary", "arbitrary"),
        name="moe_combine",
    )(x2, ybuf, ybuf, wts, mod)


def _overlap_t(n_sel, n_cmp_pad):
    ci = jnp.arange(n_cmp_pad)[None, :] * CMP_STRIDE
    sj = jnp.arange(n_sel)[:, None] * SEL_BLOCK
    ov = (ci <= sj + SEL_BLOCK - 1) & (ci + CMP_BLOCK - 1 >= sj) & (jnp.arange(n_cmp_pad)[None, :] < n_cmp_pad - 1)
    return ov.astype(BF16)


def kernel(x, c, w_ada, b_ada, norm_g, w_in, b_in, rwkv_mu, rwkv_w0, rwkv_w2, rwkv_a0, rwkv_a2, rwkv_g2,
           rwkv_k_k, rwkv_k_a, rwkv_r_k, rwkv_gn_g, rwkv_gn_b, qk_norm_g, cmp_pos, cmp_w1, cmp_w2,
           w_up_rwkv, w_up_nsa, w_out, router_w, router_b, exp_w_gate, exp_w_up, exp_w_down):
    B, T, D = x.shape
    L = w_ada.shape[0]
    N = B * T
    mods = _ada(c, w_ada, b_ada)
    tables = _rope_tables(jnp.arange(T, dtype=jnp.int32))
    nch = T // CMP_STRIDE
    tables_cmp = _rope_tables(jnp.arange(nch, dtype=jnp.int32) * CMP_STRIDE + CMP_BLOCK - 1)
    ov_t = _overlap_t(T // SEL_BLOCK, nch)
    n_gate = NSA_GATE_COLS
    x2 = x.reshape(N, D)
    pending_moe = None
    for l in range(L):
        g0 = _SEG_KV[1] + n_gate
        w_pad = jnp.concatenate([w_in[l][:, :g0], jnp.zeros((D, GATE_PAD - n_gate), F32), w_in[l][:, g0:]],
                                axis=1).astype(BF16)
        b_pad = jnp.concatenate([b_in[l][:g0], jnp.zeros((GATE_PAD - n_gate,), F32), b_in[l][g0:]]).reshape(1, -1)
        outs = _inproj(x2, pending_moe, mods[l], norm_g[l, 0].reshape(1, D), w_pad, b_pad, B, T)
        if pending_moe is not None:
            x2, outs = outs[0], outs[1:]
        p_rw, p_q, p_kv, p_gate, p_merge = outs
        r, k, v, al, bb, ld, g, bonus = _rwkv_pre(p_rw, rwkv_mu[l], rwkv_w0[l], rwkv_w2[l], rwkv_a0[l],
                                                  rwkv_a2[l], rwkv_g2[l], rwkv_k_k[l], rwkv_k_a[l],
                                                  rwkv_r_k[l], B, T)
        ys = _rwkv_scan(r, k, v, al, bb, ld, B, T)
        qt, ks, kw, vst, vsd, vwt = _nsa_prep(p_q, p_kv, tables, qk_norm_g[l], B, T)
        kv3 = p_kv.reshape(B, T, KV_COLS)
        kcmp = _nsa_cmp(kv3, 0, cmp_pos[l], cmp_w1[l], cmp_w2[l], qk_norm_g[l, 1], tables_cmp)
        vct = _nsa_cmp(kv3, 1, cmp_pos[l], cmp_w1[l], cmp_w2[l], None, None)
        gt = p_gate[:, :n_gate].reshape(B, T, NSA_KV_HEADS, NSA_GROUP, 3).transpose(0, 2, 4, 3, 1)
        yb = _nsa_attn(qt, kcmp, vct, ks.reshape(B, T, 2 * LANES), vst, vsd, kw.reshape(B, T, LANES), vwt, gt, ov_t,
                       B, T)
        x2, h2, route, totals = _merge(ys, g, bonus, rwkv_gn_g[l], rwkv_gn_b[l], yb, p_merge, x2, mods[l],
                               norm_g[l, 1].reshape(1, D), w_up_rwkv[l].astype(BF16),
                               w_up_nsa[l].astype(BF16), w_out[l].astype(BF16), router_w, router_b, B, T)
        wts, dest, blk_expert, blk_valid, n_blk = _route(route, totals, N)
        xs = _sc_dispatch(h2, dest, n_blk * MOE_BLOCK)
        ys = _moe_dense(xs, blk_expert, blk_valid, n_blk, l, exp_w_gate, exp_w_up, exp_w_down)
        ybuf = _sc_collect(ys, dest)
        pending_moe = (ybuf, wts, mods[l])
    return _final(x2, *pending_moe, B, T).reshape(B, T, D)
```

```python
import functools
import math

import jax
import jax.numpy as jnp
from jax import lax
from jax.experimental import pallas as pl
from jax.experimental.pallas import tpu as pltpu
from jax.experimental.pallas import tpu_sc as plsc

F32 = jnp.float32
BF16 = jnp.bfloat16
HI = lax.Precision.HIGHEST

D_MODEL = 1024
RWKV_HEADS = 8
HEAD_DIM = 64
RWKV_WIDTH = RWKV_HEADS * HEAD_DIM
DECAY_LORA = 64
ICLR_LORA = 64
GATE_LORA = 128
RWKV_GN_EPS = 64e-5
RWKV_COLS = 3 * RWKV_WIDTH + DECAY_LORA + ICLR_LORA + GATE_LORA

NSA_Q_HEADS = 8
NSA_KV_HEADS = 2
NSA_GROUP = NSA_Q_HEADS // NSA_KV_HEADS
NSA_WIDTH = NSA_Q_HEADS * HEAD_DIM
CMP_STRIDE = 16
CMP_BLOCK = 2 * CMP_STRIDE
CMP_HIDDEN = 256
SEL_BLOCK = 64
SEL_SHIFT = 6
SEL_TOPK = 16
WINDOW = 512
FORCE_SCORE = 1e4
NEG_INF = -1e30
ROPE_THETA = 500000.0
ROPE_DIM = HEAD_DIM // 4
KV_COLS = 6 * NSA_KV_HEADS * HEAD_DIM
NSA_GATE_COLS = 3 * NSA_Q_HEADS
GATE_PAD = 128

N_EXPERTS = 16
N_GROUPS = 4
EXPERTS_PER_GROUP = N_EXPERTS // N_GROUPS
TOP_K = 2
D_EXPERT = 512
MOE_BLOCK = 256
NORM_EPS = 1e-6

LANES = 128
CHUNK = 64
KEY_TILE = 128
SEL_TILE = 512
CMP_VARIANTS = 8
V_ROWS = 80
Q_SCALE = HEAD_DIM ** -0.5 * math.log2(math.e)
Q_TILE = 128
F32_TINY = float(jnp.finfo(jnp.float32).tiny)

_SEG_RW = (0, RWKV_COLS)
_SEG_Q = (_SEG_RW[1], _SEG_RW[1] + NSA_WIDTH)
_SEG_KV = (_SEG_Q[1], _SEG_Q[1] + KV_COLS)
_SEG_GATE = (_SEG_KV[1], _SEG_KV[1] + GATE_PAD)
_SEG_MERGE = (_SEG_GATE[1], _SEG_GATE[1] + 2 * D_MODEL)
IN_COLS_PAD = _SEG_MERGE[1]

_VMEM_LIMIT = 56 * 1024 * 1024


def _dot(a, b, precision=None):
    return jnp.dot(a, b, preferred_element_type=F32, precision=precision)


def _dot_tb(a, b, precision=None):
    return lax.dot_general(a, b, (((1,), (1,)), ((), ())), preferred_element_type=F32,
                           precision=precision)


def _dot_ta(a, b, precision=None):
    return lax.dot_general(a, b, (((0,), (0,)), ((), ())), preferred_element_type=F32,
                           precision=precision)


def _split_bf16(x, terms):
    parts = []
    for _ in range(terms - 1):
        parts.append(x.astype(BF16))
        x = x - parts[-1].astype(F32)
    parts.append(x.astype(BF16))
    return parts


def _dot_split_lhs(x, w_bf, terms=2):
    return functools.reduce(jnp.add, [_dot(p, w_bf) for p in _split_bf16(x, terms)])


def _dot_split_rhs(w_bf, x, terms=2):
    return functools.reduce(jnp.add, [_dot(w_bf, p) for p in _split_bf16(x, terms)])


def _dot_3pass(x, w_hl_ref):
    x_hi, x_lo = _split_bf16(x, 2)
    w_hi = w_hl_ref[0]
    return _dot(x_hi, w_hi) + _dot(x_lo, w_hi) + _dot(x_hi, w_hl_ref[1])


def _pack_bf16_pairs(x):
    n = x.shape[1] // 2
    bits = lax.bitcast_convert_type(x.astype(BF16).astype(F32), jnp.uint32)
    return lax.bitcast_convert_type(bits[:, 0:n] | (bits[:, n:] >> 16), F32)


def _unpack_bf16_pairs(packed):
    bits = lax.bitcast_convert_type(packed, jnp.uint32)
    lo = lax.bitcast_convert_type(bits & jnp.uint32(0xFFFF0000), F32)
    hi = lax.bitcast_convert_type(bits << 16, F32)
    return lo.astype(BF16), hi.astype(BF16)


def _hi_lo(w):
    hi = w.astype(BF16)
    return jnp.stack([hi, (w - hi.astype(F32)).astype(BF16)])


def _params(*sem):
    return pltpu.CompilerParams(dimension_semantics=sem, vmem_limit_bytes=_VMEM_LIMIT)


def _sigmoid(x):
    return 1.0 / (1.0 + jnp.exp(-x))


def _ada_kernel(c_ref, w_ref, b_ref, o_ref):
    c = c_ref[...]
    s = c * _sigmoid(c)
    o_ref[0] = _dot(s, w_ref[0], HI) + b_ref[0]


def _ada(c, w_ada, b_ada):
    L, D, D6 = w_ada.shape
    B = c.shape[0]
    rows = 8
    cp = jnp.zeros((rows, D), F32).at[:B].set(c)
    tn = 1536
    out = pl.pallas_call(
        _ada_kernel,
        grid=(L, D6 // tn),
        in_specs=[pl.BlockSpec((rows, D), lambda l, j: (0, 0)),
                  pl.BlockSpec((1, D, tn), lambda l, j: (l, 0, j)),
                  pl.BlockSpec((1, 1, tn), lambda l, j: (l, 0, j))],
        out_specs=pl.BlockSpec((1, rows, tn), lambda l, j: (l, 0, j)),
        out_shape=jax.ShapeDtypeStruct((L, rows, D6), F32),
        compiler_params=_params("arbitrary", "arbitrary"),
        name="ada_mod",
    )(cp, w_ada, b_ada.reshape(L, 1, D6))
    return out[:, :B].reshape(L, B, 6, D)


def _moe_residual(x_ref, y0_ref, y1_ref, w_ref, m_prev):
    w = w_ref[...]
    rows = lambda y_ref: jnp.concatenate([h.astype(F32) for h in _unpack_bf16_pairs(y_ref[...])], axis=1)
    return x_ref[...] + m_prev[5:6] * (w[:, 0:1] * rows(y0_ref) + w[:, 1:2] * rows(y1_ref))


def _inproj_kernel(*refs, after_moe):
    if after_moe:
        x_ref, y0_ref, y1_ref, wts_ref, modp_ref, mod_ref, g_ref, w_ref, b_ref, o_x = refs[:10]
        x = _moe_residual(x_ref, y0_ref, y1_ref, wts_ref, modp_ref[pl.program_id(0)])
        o_x[...] = x
    else:
        x_ref, mod_ref, g_ref, w_ref, b_ref = refs[:5]
        x = x_ref[...]
    o_rw, o_q, o_kv, o_gate, o_merge = refs[-5:]
    m = mod_ref[pl.program_id(0)]
    ms = jnp.mean(x * x, axis=-1, keepdims=True)
    h = x * lax.rsqrt(ms + NORM_EPS) * g_ref[...]
    h = h * (1.0 + m[1:2]) + m[0:1]
    hb = h.astype(BF16)
    for o, (a, e) in ((o_rw, _SEG_RW), (o_q, _SEG_Q), (o_kv, _SEG_KV), (o_gate, _SEG_GATE),
                      (o_merge, _SEG_MERGE)):
        o[...] = _dot(hb, w_ref[:, a:e]) + b_ref[:, a:e]


def _inproj(x2, pending_moe, mod, g, w_pad, b_pad, B, T, tm=512):
    N, D = x2.shape
    nt = T // tm
    row = lambda b, t: (b * nt + t, 0)
    mods_spec = pl.BlockSpec((B, 6, D), lambda b, t: (0, 0, 0))
    widths = [e - a for a, e in (_SEG_RW, _SEG_Q, _SEG_KV, _SEG_GATE, _SEG_MERGE)]
    in_specs = [pl.BlockSpec((tm, D), row)]
    args = [x2]
    if pending_moe is not None:
        ybuf, wts, mod_prev = pending_moe
        in_specs += [pl.BlockSpec((tm, D // 2), row),
                     pl.BlockSpec((tm, D // 2), lambda b, t: (N // tm + b * nt + t, 0)),
                     pl.BlockSpec((tm, TOP_K), row), mods_spec]
        args += [ybuf, ybuf, wts, mod_prev]
        widths = [D] + widths
    in_specs += [mods_spec, pl.BlockSpec((1, D), lambda b, t: (0, 0)),
                 pl.BlockSpec((D, IN_COLS_PAD), lambda b, t: (0, 0)),
                 pl.BlockSpec((1, IN_COLS_PAD), lambda b, t: (0, 0))]
    return pl.pallas_call(
        functools.partial(_inproj_kernel, after_moe=pending_moe is not None),
        grid=(B, nt),
        in_specs=in_specs,
        out_specs=[pl.BlockSpec((tm, w), row) for w in widths],
        out_shape=[jax.ShapeDtypeStruct((N, w), F32) for w in widths],
        compiler_params=_params("arbitrary", "arbitrary"),
        name="in_proj",
    )(*args, mod, g, w_pad, b_pad)


def _rwkv_pre_kernel(p_ref, mu_ref, w0_ref, w2_ref, a0_ref, a2_ref, g2_ref, kk_ref, ka_ref, rk_ref,
                     bd_ref, o_r, o_k, o_v, o_al, o_b, o_ld, o_g, o_bonus, carry_ref):
    W = RWKV_WIDTH

    @pl.when(pl.program_id(1) == 0)
    def _():
        carry_ref[...] = jnp.zeros_like(carry_ref)

    p = p_ref[...]
    ts = p.shape[0]
    rows = lax.broadcasted_iota(jnp.int32, p.shape, 0)
    shifted = jnp.where(rows == 0, carry_ref[0:1, :], pltpu.roll(p, 1, 0))
    carry_ref[0:1, :] = p[ts - 1:ts, :]
    pm = p + (shifted - p) * mu_ref[...]
    r = pm[:, 0:W]
    k = pm[:, W:2 * W]
    v = pm[:, 2 * W:3 * W]
    wa = pm[:, 3 * W:3 * W + DECAY_LORA + ICLR_LORA]
    gl = pm[:, 3 * W + DECAY_LORA + ICLR_LORA:]
    xw = w0_ref[...] + _dot_3pass(jnp.tanh(wa), w2_ref)
    ld = -math.exp(-0.5) * _sigmoid(xw)
    a = _sigmoid(a0_ref[...] + _dot_3pass(wa, a2_ref))
    g = _dot_3pass(_sigmoid(gl), g2_ref)
    bd = bd_ref[...]
    kk = k * kk_ref[...]
    nrm = jnp.sqrt(_head_sums(kk * kk, bd))
    kk = kk / jnp.maximum(nrm, 1e-12)
    k2 = k * (1.0 + (a - 1.0) * ka_ref[...])
    bonus = _head_sums(r * k2 * rk_ref[...], bd) * v
    o_r[...] = r
    o_k[...] = k2
    o_v[...] = v
    o_al[...] = kk
    o_b[...] = -kk * a
    o_ld[...] = ld
    o_g[...] = g
    o_bonus[...] = bonus


HEAD_SUM_WIDTH = 256


def _head_sums(x, bd):
    w = bd.shape[0]
    parts = [_dot_split_lhs(x[:, j:j + w], bd) for j in range(0, x.shape[1], w)]
    return parts[0] if len(parts) == 1 else jnp.concatenate(parts, axis=1)


def _head_block_diag(width, scale=1.0):
    i = jnp.arange(width) // HEAD_DIM
    return ((i[:, None] == i[None, :]).astype(F32) * scale).astype(BF16)


def _rwkv_pre(p_rw, mu, w0, w2, a0, a2, g2, k_k, k_a, r_k, B, T, ts=512):
    N = p_rw.shape[0]
    W = RWKV_WIDTH
    nt = T // ts
    row = lambda b, t: (b * nt + t, 0)
    zl = jnp.zeros((DECAY_LORA, W), F32)
    w2p = jnp.concatenate([w2, zl], axis=0)
    a2p = jnp.concatenate([zl, a2], axis=0)
    full = lambda shape: pl.BlockSpec(shape, lambda b, t: (0,) * len(shape))
    vec = lambda z: z.reshape(1, -1)
    return pl.pallas_call(
        _rwkv_pre_kernel,
        grid=(B, nt),
        in_specs=[pl.BlockSpec((ts, RWKV_COLS), row), full((1, RWKV_COLS)), full((1, W)),
                  full((2, 2 * DECAY_LORA, W)), full((1, W)), full((2, 2 * DECAY_LORA, W)),
                  full((2, GATE_LORA, W)), full((1, W)), full((1, W)), full((1, W)),
                  full((HEAD_SUM_WIDTH, HEAD_SUM_WIDTH))],
        out_specs=[pl.BlockSpec((ts, W), row)] * 8,
        out_shape=[jax.ShapeDtypeStruct((N, W), F32)] * 8,
        scratch_shapes=[pltpu.VMEM((8, RWKV_COLS), F32)],
        compiler_params=_params("arbitrary", "arbitrary"),
        name="rwkv_pre",
    )(p_rw, vec(mu), vec(w0), _hi_lo(w2p), vec(a0), _hi_lo(a2p), _hi_lo(g2), vec(k_k), vec(k_a), vec(r_k),
      _head_block_diag(HEAD_SUM_WIDTH))


def _bf(x):
    return x.astype(BF16)


def _scan_local(chunks, eye, strict, incl, m0, m1, between_stages=lambda: None):
    C = CHUNK
    n = range(len(chunks))
    st = lambda z: jnp.concatenate([z * m0, z * m1], axis=0)
    zero = jnp.zeros((2 * C, 2 * C), F32)
    at_b, rt_s, vs, vs_b, lhs_a, rhs_a, bk_t, dcol = [], [], [], [], [], [], [], []
    for r, k, v, al, bb, ld, cum in chunks:
        tot = cum[C - 1:C, :]
        dinv = jnp.exp(-cum)
        dend = jnp.exp(tot - cum)
        at_b.append(_bf(st(al * jnp.exp(cum - ld))))
        rt_s.append(st(r * jnp.exp(cum)))
        vs.append(st(v))
        vs_b.append(_bf(vs[-1]))
        lhs_a.append(jnp.concatenate([at_b[-1], _bf(rt_s[-1])], axis=0))
        rhs_a.append(_bf(jnp.concatenate([st(bb * dinv), st(k * dinv)], axis=0)))
        bk_t.append(_bf(jnp.concatenate([st(bb * dend).T, st(k * dend).T], axis=1)))
        dcol.append(jnp.sum(eye * jnp.exp(tot), axis=1, keepdims=True))
    between_stages()
    A = [_dot_tb(lhs_a[i], rhs_a[i]) for i in n]
    between_stages()
    a_ab = [jnp.where(strict, A[i][0:2 * C, 0:2 * C], zero) for i in n]
    a_ak = [_bf(jnp.where(strict, A[i][0:2 * C, 2 * C:4 * C], zero)) for i in n]
    a_r = [_bf(jnp.concatenate([jnp.where(incl, A[i][2 * C:4 * C, 0:2 * C], zero),
                                jnp.where(incl, A[i][2 * C:4 * C, 2 * C:4 * C], zero)], axis=1)) for i in n]
    akv = [_bf(_dot(a_ak[i], vs_b[i])) for i in n]
    between_stages()
    tinv = [eye + a_ab[i] for i in n]
    pw_b = [_bf(a_ab[i]) for i in n]
    pw_b = [_bf(_dot(pw_b[i], pw_b[i])) for i in n]
    between_stages()
    for step in range(5):
        rhs = [jnp.concatenate([pw_b[i], _bf(tinv[i])], axis=1) for i in n]
        if step == 4:
            rhs = [_bf(tinv[i]) for i in n]
        prod = [_dot(pw_b[i], rhs[i]) for i in n]
        tinv = [tinv[i] + prod[i][:, -2 * C:] for i in n]
        pw_b = [_bf(prod[i][:, 0:2 * C]) for i in n]
        between_stages()
    X = [_dot(_bf(tinv[i]), jnp.concatenate([at_b[i], akv[i]], axis=1)) for i in n]
    between_stages()
    w_b = [_bf(X[i][:, 0:LANES]) for i in n]
    uv0 = [jnp.concatenate([_bf(X[i][:, LANES:2 * LANES]), vs_b[i]], axis=0) for i in n]
    m_h = [_bf(_dot(bk_t[i][:, 0:2 * C], w_b[i])) for i in n]
    g_h = [_dot(bk_t[i], uv0[i]) for i in n]
    between_stages()
    q_h = [_bf(rt_s[i] + _dot(a_r[i][:, 0:2 * C], w_b[i])) for i in n]
    y0 = [_dot(a_r[i], uv0[i]) for i in n]
    return [(m_h[i], g_h[i], dcol[i], q_h[i], y0[i]) for i in n]


def _rwkv_scan_kernel(r_ref, k_ref, v_ref, al_ref, b_ref, ld_ref, o_ref, h_ref, *local_refs):
    C = CHUNK
    tc = r_ref.shape[0]
    nc = tc // C

    @pl.when(pl.program_id(2) == 0)
    def _():
        h_ref[...] = jnp.zeros_like(h_ref)
        for ref in local_refs:
            ref[...] = jnp.zeros_like(ref)

    seq = {"H": h_ref[...], "c": 0}

    def one_step():
        c = seq["c"]
        if c < nc:
            m_h, g_h, dcol, q_h, y0 = (ref[c] for ref in local_refs)
            h_b = _bf(seq["H"])
            Y = _dot(q_h, h_b) + y0
            o_ref[c * C:(c + 1) * C, :] = Y[0:C] + Y[C:2 * C]
            seq["H"] = dcol * seq["H"] + _dot(m_h, h_b) + g_h
            seq["c"] = c + 1

    tri = jnp.where(lax.broadcasted_iota(jnp.int32, (C, C), 1) <= lax.broadcasted_iota(jnp.int32, (C, C), 0),
                    1.0, 0.0).astype(BF16)
    r2 = lax.broadcasted_iota(jnp.int32, (2 * C, 2 * C), 0)
    c2 = lax.broadcasted_iota(jnp.int32, (2 * C, 2 * C), 1)
    eye = (r2 == c2).astype(F32)
    strict = (c2 & (C - 1)) < (r2 & (C - 1))
    incl = (c2 & (C - 1)) <= (r2 & (C - 1))
    lane = lax.broadcasted_iota(jnp.int32, (C, LANES), 1)
    m0 = (lane < HEAD_DIM).astype(F32)
    m1 = 1.0 - m0
    cum = _dot_split_rhs(tri, jnp.concatenate([ld_ref[c * C:(c + 1) * C, :] for c in range(nc)], axis=1), 3)
    chunks = []
    for c in range(nc):
        sl = slice(c * C, (c + 1) * C)
        chunks.append((r_ref[sl, :], k_ref[sl, :], v_ref[sl, :], al_ref[sl, :], b_ref[sl, :], ld_ref[sl, :],
                       cum[:, c * LANES:(c + 1) * LANES]))
    local = _scan_local(chunks, eye, strict, incl, m0, m1, between_stages=one_step)
    while seq["c"] < nc:
        one_step()
    h_ref[...] = seq["H"]
    for c, parts in enumerate(local):
        for ref, part in zip(local_refs, parts):
            ref[c] = part


def _rwkv_scan(r, k, v, al, bb, ld, B, T, tc=512):
    N, W = r.shape
    nt = T // tc
    nc = tc // CHUNK
    in_spec = pl.BlockSpec((tc, LANES), lambda b, h, t: (b * nt + jnp.minimum(t, nt - 1), h))
    out_spec = pl.BlockSpec((tc, LANES), lambda b, h, t: (b * nt + jnp.maximum(t - 1, 0), h))
    sq = (nc, LANES, LANES)
    return pl.pallas_call(
        _rwkv_scan_kernel,
        grid=(B, W // LANES, nt + 1),
        in_specs=[in_spec] * 6,
        out_specs=out_spec,
        out_shape=jax.ShapeDtypeStruct((N, W), F32),
        scratch_shapes=[pltpu.VMEM((LANES, LANES), F32), pltpu.VMEM(sq, BF16), pltpu.VMEM(sq, F32),
                        pltpu.VMEM((nc, LANES, 1), F32), pltpu.VMEM(sq, BF16), pltpu.VMEM(sq, F32)],
        compiler_params=_params("arbitrary", "arbitrary", "arbitrary"),
        name="rwkv_scan",
    )(r, k, v, al, bb, ld)


def _rope_tables(pos):
    half = ROPE_DIM // 2
    inv = jnp.power(ROPE_THETA, -jnp.arange(half, dtype=F32) * 2.0 / ROPE_DIM)
    ang = pos.astype(F32)[:, None] * inv[None, :]
    cos, sin = jnp.cos(ang), jnp.sin(ang)
    n = pos.shape[0]
    rest = HEAD_DIM - ROPE_DIM
    c = jnp.concatenate([cos, cos, jnp.ones((n, rest), F32)], axis=1)
    s_dn = jnp.concatenate([-sin, jnp.zeros((n, half + rest), F32)], axis=1)
    s_up = jnp.concatenate([jnp.zeros((n, half), F32), sin, jnp.zeros((n, rest), F32)], axis=1)
    rep = LANES // HEAD_DIM
    return jnp.tile(c, (1, rep)), jnp.tile(s_dn, (1, rep)), jnp.tile(s_up, (1, rep))


def _norm_rope(x, bd, g, c, s_dn, s_up):
    width = x.shape[1]
    half = ROPE_DIM // 2
    rep = width // LANES
    tile = (lambda z: jnp.concatenate([z] * rep, axis=1)) if rep > 1 else (lambda z: z)
    ms = _head_sums(x * x, bd)
    xn = x * lax.rsqrt(ms + NORM_EPS) * g
    return (xn * tile(c) + pltpu.roll(xn, width - half, 1) * tile(s_dn)
            + pltpu.roll(xn, half, 1) * tile(s_up))


def _nsa_prep_kernel(q_ref, kv_ref, c_ref, sd_ref, su_ref, gq_ref, gs_ref, gw_ref, bdq_ref, bdk_ref,
                     o_qt, o_ks, o_kw, o_vst, o_vsd, o_vwt):
    c, sd, su = c_ref[...], sd_ref[...], su_ref[...]
    q = _norm_rope(q_ref[...], bdq_ref[...], gq_ref[...], c, sd, su) * Q_SCALE
    qt = q.T
    ts = q.shape[0]
    kv = kv_ref[...]
    bdk = bdk_ref[...]
    pos = pl.program_id(1) * ts + lax.broadcasted_iota(jnp.int32, (ts, LANES), 0)
    blk_onehot = jnp.where((pos >> SEL_SHIFT) == lax.broadcasted_iota(jnp.int32, (ts, LANES), 1), 1.0, 0.0)
    ks = _norm_rope(kv[:, 2 * LANES:3 * LANES], bdk, gs_ref[...], c, sd, su)
    o_ks[...] = jnp.concatenate([ks, blk_onehot], axis=1).astype(BF16)
    o_kw[...] = _norm_rope(kv[:, 4 * LANES:5 * LANES], bdk, gw_ref[...], c, sd, su).astype(BF16)
    ones_rows = jnp.where(lax.broadcasted_iota(jnp.int32, (V_ROWS - HEAD_DIM, q.shape[0]), 0) == 0, 1.0, 0.0)

    def values_t(x):
        xt = x.T
        return jnp.concatenate([xt[0:HEAD_DIM], ones_rows, xt[HEAD_DIM:2 * HEAD_DIM], ones_rows], axis=0)

    vst = values_t(kv[:, 3 * LANES:4 * LANES])
    vwt = values_t(kv[:, 5 * LANES:6 * LANES])
    for j in range(q.shape[0] // KEY_TILE):
        sl = slice(j * KEY_TILE, (j + 1) * KEY_TILE)
        o_qt[0, j] = qt[:, sl].astype(BF16)
        o_vsd[0, j] = vst[:, sl].astype(BF16)
        o_vwt[0, j] = vwt[:, sl].astype(BF16)
    for j in range(q.shape[0] // SEL_TILE):
        o_vst[0, j] = vst[:, j * SEL_TILE:(j + 1) * SEL_TILE].astype(BF16)


def _nsa_prep(q, kv, tables, qk_g, B, T, ts=512):
    N = q.shape[0]
    nt = T // ts
    nk = ts // KEY_TILE
    ns = ts // SEL_TILE
    row = lambda b, t: (b * nt + t, 0)
    full = lambda shape: pl.BlockSpec(shape, lambda b, t: (0,) * len(shape))
    tab = pl.BlockSpec((ts, LANES), lambda b, t: (t, 0))
    gq = jnp.tile(qk_g[0], NSA_Q_HEADS).reshape(1, NSA_WIDTH)
    gs = jnp.tile(qk_g[2], NSA_KV_HEADS).reshape(1, LANES)
    gw = jnp.tile(qk_g[3], NSA_KV_HEADS).reshape(1, LANES)
    tiled = lambda rows: pl.BlockSpec((1, nk, rows, KEY_TILE), lambda b, t: (b, t, 0, 0))
    return pl.pallas_call(
        _nsa_prep_kernel,
        grid=(B, nt),
        in_specs=[pl.BlockSpec((ts, NSA_WIDTH), row), pl.BlockSpec((ts, KV_COLS), row), tab, tab, tab,
                  full((1, NSA_WIDTH)), full((1, LANES)), full((1, LANES)),
                  full((HEAD_SUM_WIDTH, HEAD_SUM_WIDTH)), full((LANES, LANES))],
        out_specs=[tiled(NSA_WIDTH), pl.BlockSpec((ts, 2 * LANES), row), pl.BlockSpec((ts, LANES), row),
                   pl.BlockSpec((1, ns, NSA_KV_HEADS * V_ROWS, SEL_TILE), lambda b, t: (b, t, 0, 0)),
                   tiled(NSA_KV_HEADS * V_ROWS), tiled(NSA_KV_HEADS * V_ROWS)],
        out_shape=[jax.ShapeDtypeStruct((B, T // KEY_TILE, NSA_WIDTH, KEY_TILE), BF16),
                   jax.ShapeDtypeStruct((N, 2 * LANES), BF16), jax.ShapeDtypeStruct((N, LANES), BF16),
                   jax.ShapeDtypeStruct((B, T // SEL_TILE, NSA_KV_HEADS * V_ROWS, SEL_TILE), BF16),
                   jax.ShapeDtypeStruct((B, T // KEY_TILE, NSA_KV_HEADS * V_ROWS, KEY_TILE), BF16),
                   jax.ShapeDtypeStruct((B, T // KEY_TILE, NSA_KV_HEADS * V_ROWS, KEY_TILE), BF16)],
        compiler_params=_params("arbitrary", "arbitrary"),
        name="nsa_prep",
    )(q, kv, *tables, gq, gs, gw, _head_block_diag(HEAD_SUM_WIDTH, 1.0 / HEAD_DIM),
      _head_block_diag(LANES, 1.0 / HEAD_DIM))


def _gelu_tanh(x):
    return 0.5 * x * (1.0 + jnp.tanh(0.7978845608028654 * (x + 0.044715 * x * x * x)))


def _nsa_cmp_kernel(x_ref, pos_ref, w1_ref, w2_ref, *rest, is_key):
    if is_key:
        g_ref, c_ref, sd_ref, su_ref, bd_ref, o_ref, xs_ref = rest
    else:
        o_ref, xs_ref = rest
    nch = xs_ref.shape[0]
    S = CMP_STRIDE
    for j in range(S):
        xs_ref[:, j * LANES:(j + 1) * LANES] = x_ref[0, pl.ds(j, nch, stride=S), :]
    xs = xs_ref[...]
    first = _dot((xs + pos_ref[0:1, :]).astype(BF16), w1_ref[0])
    second = _dot((xs + pos_ref[1:2, :]).astype(BF16), w1_ref[1])
    hid = first + pltpu.roll(second, nch - 1, 0)
    out = _dot(_gelu_tanh(hid).astype(BF16), w2_ref[...])
    rows = lax.broadcasted_iota(jnp.int32, out.shape, 0)
    if is_key:
        out = _norm_rope(out, bd_ref[...], g_ref[...], c_ref[...], sd_ref[...], su_ref[...])
        o_ref[0] = jnp.where(rows < nch - 1, out, 0.0).astype(BF16)
    else:
        o_ref[0] = jnp.where(rows < nch - 1, out, 0.0).T.astype(BF16)


def _nsa_cmp(kv3, which, cmp_pos, cmp_w1, cmp_w2, g_k, tables_cmp):
    B, T, _ = kv3.shape
    S = CMP_STRIDE
    nch = T // S
    is_key = which == 0
    eye2 = jnp.eye(NSA_KV_HEADS, dtype=F32)
    w1 = cmp_w1[which].reshape(CMP_BLOCK, HEAD_DIM, CMP_HIDDEN)
    w1 = jnp.einsum('jdh,ge->jgdeh', w1, eye2).reshape(2, S * LANES, NSA_KV_HEADS * CMP_HIDDEN)
    w2 = jnp.einsum('hd,ge->ghed', cmp_w2[which], eye2).reshape(NSA_KV_HEADS * CMP_HIDDEN, LANES)
    pos = jnp.tile(cmp_pos[which].reshape(2, S, 1, HEAD_DIM), (1, 1, NSA_KV_HEADS, 1)).reshape(2, S * LANES)
    full = lambda shape: pl.BlockSpec(shape, lambda b: (0,) * len(shape))
    in_specs = [pl.BlockSpec((1, T, LANES), lambda b: (b, 0, which)), full(pos.shape), full(w1.shape),
                full(w2.shape)]
    args = [kv3, pos, w1.astype(BF16), w2.astype(BF16)]
    if is_key:
        in_specs += [full((1, LANES)), full((nch, LANES)), full((nch, LANES)), full((nch, LANES)),
                     full((LANES, LANES))]
        args += [jnp.tile(g_k, NSA_KV_HEADS).reshape(1, LANES), *tables_cmp,
                 _head_block_diag(LANES, 1.0 / HEAD_DIM)]
        out_spec = pl.BlockSpec((1, nch, LANES), lambda b: (b, 0, 0))
        out_shape = jax.ShapeDtypeStruct((B, nch, LANES), BF16)
    else:
        out_spec = pl.BlockSpec((1, LANES, nch), lambda b: (b, 0, 0))
        out_shape = jax.ShapeDtypeStruct((B, LANES, nch), BF16)
    return pl.pallas_call(
        functools.partial(_nsa_cmp_kernel, is_key=is_key),
        grid=(B,),
        in_specs=in_specs,
        out_specs=out_spec,
        out_shape=out_shape,
        scratch_shapes=[pltpu.VMEM((nch, S * LANES), F32)],
        compiler_params=_params("arbitrary"),
        name="nsa_cmp_k" if is_key else "nsa_cmp_v",
    )(*args)


def _nsa_attn_kernel(qt_ref, kc_ref, vct_ref, ks_ref, vst_ref, vsd_ref, kw_ref, vwt_ref, gt_ref, ov_ref, o_ref,
                     rhs_ref, oc_ref, keep_ref, s0_ref, s1_ref, s2_ref, s3_ref, p0_ref, p1_ref):
    qb = pl.program_id(1)
    G = NSA_KV_HEADS
    R = NSA_GROUP
    QT = Q_TILE
    KT = KEY_TILE
    CG = R * QT
    NQ = G * CG
    D = HEAD_DIM
    t0 = qb * QT
    n_cmp_pad = kc_ref.shape[1]
    n_sel = ov_ref.shape[0]
    cols = lambda g: slice(g * CG, (g + 1) * CG)

    q_cols = []
    for g in range(G):
        q_g = jnp.concatenate([qt_ref[0, 0, (g * R + r) * D:(g * R + r + 1) * D, :] for r in range(R)], axis=1)
        q_cols.append(jnp.concatenate([q_g if gg == g else jnp.zeros_like(q_g) for gg in range(G)], axis=0))
    qpad = jnp.concatenate(q_cols, axis=1)

    tq_row = t0 + (lax.broadcasted_iota(jnp.int32, (1, NQ), 1) & (QT - 1))
    spread = lambda z: jnp.concatenate([z[:, g * QT:(g + 1) * QT] for g in range(G) for _ in range(R)], axis=1)
    tile_all = lambda z: jnp.concatenate([z] * (G * R), axis=1)

    def values_dot(v_of_group, p):
        return jnp.concatenate([_dot(v_of_group(g), p[:, cols(g)]) for g in range(G)], axis=1)

    NV = CMP_VARIANTS
    nq = ks_ref.shape[1] // QT

    def compressed_and_select(n_c, n_b):
        sc = _dot(kc_ref[0, 0:n_c, :], qpad)
        n_i = lax.broadcasted_iota(jnp.int32, (n_c, 1), 0)
        cend = jnp.where(n_i < n_cmp_pad - 1, n_i * CMP_STRIDE + (CMP_BLOCK - 1), jnp.int32(2 ** 30))
        cvalid = cend <= tq_row
        sc = jnp.where(cvalid, sc, NEG_INF)
        mc = jnp.max(sc, axis=0, keepdims=True)
        ec = jnp.where(cvalid, jnp.exp2(sc - mc), 0.0)
        pc = ec / jnp.maximum(jnp.sum(ec, axis=0, keepdims=True), F32_TINY)
        pc_b = pc.astype(BF16)
        oc_ref[...] = values_dot(lambda g: vct_ref[0, g * D:(g + 1) * D, 0:n_c], pc_b)
        sums = []
        for g in range(G):
            acc = pc[:, g * CG:g * CG + QT]
            for r in range(1, R):
                acc = acc + pc[:, g * CG + r * QT:g * CG + (r + 1) * QT]
            sums.append(acc)
        imp = _dot_split_rhs(ov_ref[0:n_b, 0:n_c], jnp.concatenate(sums, axis=1))
        jb = lax.broadcasted_iota(jnp.int32, (n_b, G * QT), 0)
        jf = jb.astype(F32)
        tq_b = t0 + (lax.broadcasted_iota(jnp.int32, (n_b, G * QT), 1) & (QT - 1))
        cur = tq_b >> SEL_SHIFT
        forced = (jb == 0) | (jb == cur) | (jb == cur - 1)
        visible = jb * SEL_BLOCK <= tq_b
        score = jnp.where(visible, jnp.where(forced, FORCE_SCORE, imp), -1.0)
        sel = jnp.zeros((n_b, G * QT), F32)
        for _ in range(min(SEL_TOPK, n_b)):
            mx = jnp.max(score, axis=0, keepdims=True)
            jmin = jnp.min(jnp.where(score == mx, jf, 1e9), axis=0, keepdims=True)
            hit = jf == jmin
            sel = jnp.where(hit, 1.0, sel)
            score = jnp.where(hit, -3e38, score)
        keep_ref[0:n_b, :] = jnp.where(visible, sel, 0.0)
        if n_b < n_sel:
            keep_ref[n_b:n_sel, :] = jnp.zeros((n_sel - n_b, G * QT), F32)

    for v in range(NV):
        @pl.when((qb * NV) // nq == v)
        def _():
            compressed_and_select((v + 1) * n_cmp_pad // NV, (v + 1) * n_sel // NV)

    o_c = oc_ref[...]
    ji = lax.broadcasted_iota(jnp.int32, (n_sel, G * QT), 0)

    ST = SEL_TILE
    bias_all = (keep_ref[...] - 1.0) * (-NEG_INF)
    first_own = t0 // SEL_BLOCK
    vrows = lambda g: slice(g * V_ROWS, (g + 1) * V_ROWS)

    def with_bias_rows(bias):
        rows = spread(bias).astype(BF16)
        if n_sel < LANES:
            rows = jnp.concatenate([rows, jnp.zeros((LANES - n_sel, NQ), BF16)], axis=0)
        return jnp.concatenate([qpad, rows], axis=0)

    rhs_ref[...] = with_bias_rows(jnp.where(ji < first_own, bias_all, NEG_INF))
    n_tiles = (t0 + ST - 1) // ST
    last_tile = ks_ref.shape[1] // ST - 1
    p_bufs = (p0_ref, p1_ref)

    def sel_scores(kt, s_ref):
        k0 = pl.multiple_of(jnp.minimum(kt, last_tile) * ST, ST)
        s_ref[...] = _dot(ks_ref[0, pl.ds(k0, ST), :], rhs_ref[...])

    def sel_values(kt, slot, acc, alpha):
        kt = jnp.clip(kt, 0, last_tile)
        return acc * alpha + values_dot(lambda g: vst_ref[0, kt, vrows(g), :], p_bufs[slot][...])

    def sel_softmax(s_ref, slot, m):
        s = s_ref[...]
        m_new = jnp.maximum(m, jnp.max(s, axis=0, keepdims=True))
        p_bufs[slot][...] = jnp.exp2(s - m_new).astype(BF16)
        return m_new, jnp.exp2(m - m_new)

    def sel_pair(a, carry, s_now, s_next):
        m, acc, alpha0, alpha1 = carry
        acc = sel_values(a - 2, 0, acc, alpha0)
        acc = sel_values(a - 1, 1, acc, alpha1)
        sel_scores(a + 2, s_next[0])
        sel_scores(a + 3, s_next[1])
        m, alpha0 = sel_softmax(s_now[0], 0, m)
        m, alpha1 = sel_softmax(s_now[1], 1, m)
        return m, acc, alpha0, alpha1

    bufs_a, bufs_b = (s0_ref, s1_ref), (s2_ref, s3_ref)
    sel_scores(0, s0_ref)
    sel_scores(1, s1_ref)
    p0_ref[...] = jnp.zeros_like(p0_ref)
    p1_ref[...] = jnp.zeros_like(p1_ref)
    own = _dot(ks_ref[0, pl.ds(pl.multiple_of(t0, QT), QT), :], with_bias_rows(bias_all))

    n_wt = (WINDOW + QT) // KT
    k0w = pl.multiple_of(jnp.maximum(t0 - WINDOW, 0), KT)
    kt_w = k0w // KT
    keys_w = kw_ref[0, pl.ds(k0w, WINDOW + QT), :]
    dw = (t0 + lax.broadcasted_iota(jnp.int32, (WINDOW + QT, QT), 1)
          - (k0w + lax.broadcasted_iota(jnp.int32, (WINDOW + QT, QT), 0)))
    sw = _dot(keys_w, qpad) + tile_all(jnp.where(dw >= 0, jnp.where(dw < WINDOW, 0.0, NEG_INF), NEG_INF))
    pw = jnp.exp2(sw - jnp.max(sw, axis=0, keepdims=True)).astype(BF16)
    acc_w = values_dot(lambda g: vwt_ref[0, kt_w, vrows(g), :], pw[0:KT])
    for j in range(1, n_wt):
        acc_w = acc_w + values_dot(lambda g: vwt_ref[0, kt_w + j, vrows(g), :], pw[j * KT:(j + 1) * KT])

    n_pairs = (n_tiles + 1) // 2
    one = jnp.ones((1, NQ), F32)
    m_s, acc_s, alpha0, alpha1 = lax.fori_loop(
        0, n_pairs,
        lambda j, carry: lax.cond(j % 2 == 0,
                                  lambda c: sel_pair(2 * j, c, bufs_a, bufs_b),
                                  lambda c: sel_pair(2 * j, c, bufs_b, bufs_a), carry),
        (jnp.full((1, NQ), NEG_INF, F32), jnp.zeros((V_ROWS, NQ), F32), one, one))
    acc_s = sel_values(2 * n_pairs - 2, 0, acc_s, alpha0)
    acc_s = sel_values(2 * n_pairs - 1, 1, acc_s, alpha1)
    seen = lax.broadcasted_iota(jnp.int32, (QT, QT), 0) <= lax.broadcasted_iota(jnp.int32, (QT, QT), 1)
    own = jnp.where(tile_all(seen), own, NEG_INF)
    m_new = jnp.maximum(m_s, jnp.max(own, axis=0, keepdims=True))
    acc_s = acc_s * jnp.exp2(m_s - m_new) + values_dot(lambda g: vsd_ref[0, qb, vrows(g), :],
                                                       jnp.exp2(own - m_new).astype(BF16))

    gates = _sigmoid(gt_ref[0])
    grow = lambda j: jnp.concatenate([gates[g, j, r:r + 1, :] for g in range(G) for r in range(R)], axis=1)
    o = (grow(0) * o_c + grow(1) * (acc_s[0:D] / acc_s[D:D + 1])
         + grow(2) * (acc_w[0:D] / acc_w[D:D + 1]))
    halves = []
    for h in range(G * R // 2):
        pair = jnp.concatenate([o[:, (2 * h) * QT:(2 * h + 1) * QT],
                                o[:, (2 * h + 1) * QT:(2 * h + 2) * QT]], axis=0)
        halves.append(pair.T)
    o_ref[...] = jnp.concatenate(halves, axis=1)


def _nsa_attn(qt, kcmp, vct, ks3, vst, vsd, kw3, vwt, gt, ov_t, B, T):
    G, R = NSA_KV_HEADS, NSA_GROUP
    nq = T // Q_TILE
    nk = T // KEY_TILE
    nch = kcmp.shape[1]
    n_sel = ov_t.shape[0]
    NQ = G * R * Q_TILE
    assert (T // SEL_TILE) % 2 == 0 and n_sel <= LANES and Q_TILE == KEY_TILE
    assert nq % CMP_VARIANTS == 0 and n_sel % (8 * CMP_VARIANTS) == 0 and nch % (8 * CMP_VARIANTS) == 0
    return pl.pallas_call(
        _nsa_attn_kernel,
        grid=(B, nq),
        in_specs=[pl.BlockSpec((1, 1, NSA_WIDTH, Q_TILE), lambda b, q: (b, q, 0, 0)),
                  pl.BlockSpec((1, nch, LANES), lambda b, q: (b, 0, 0)),
                  pl.BlockSpec((1, G * HEAD_DIM, nch), lambda b, q: (b, 0, 0)),
                  pl.BlockSpec((1, T, 2 * LANES), lambda b, q: (b, 0, 0)),
                  pl.BlockSpec((1, T // SEL_TILE, G * V_ROWS, SEL_TILE), lambda b, q: (b, 0, 0, 0)),
                  pl.BlockSpec((1, nk, G * V_ROWS, KEY_TILE), lambda b, q: (b, 0, 0, 0)),
                  pl.BlockSpec((1, T, LANES), lambda b, q: (b, 0, 0)),
                  pl.BlockSpec((1, nk, G * V_ROWS, KEY_TILE), lambda b, q: (b, 0, 0, 0)),
                  pl.BlockSpec((1, G, 3, R, Q_TILE), lambda b, q: (b, 0, 0, 0, q)),
                  pl.BlockSpec((n_sel, nch), lambda b, q: (0, 0))],
        out_specs=pl.BlockSpec((Q_TILE, NSA_WIDTH), lambda b, q: (b * nq + q, 0)),
        out_shape=jax.ShapeDtypeStruct((B * T, NSA_WIDTH), F32),
        scratch_shapes=[pltpu.VMEM((2 * LANES, NQ), BF16), pltpu.VMEM((HEAD_DIM, NQ), F32),
                        pltpu.VMEM((n_sel, G * Q_TILE), F32),
                        *[pltpu.VMEM((SEL_TILE, NQ), F32)] * 4,
                        *[pltpu.VMEM((SEL_TILE, NQ), BF16)] * 2],
        compiler_params=_params("arbitrary", "arbitrary"),
        name="nsa_attn",
    )(qt, kcmp, vct, ks3, vst, vsd, kw3, vwt, gt, ov_t)


def _first_index_of(vals, target):
    idx = jnp.full_like(target, float(len(vals) - 1))
    for i in range(len(vals) - 2, -1, -1):
        idx = jnp.where(vals[i] == target, float(i), idx)
    return idx


def _pick(vals, idx):
    out = vals[-1]
    for i in range(len(vals) - 2, -1, -1):
        out = jnp.where(idx == float(i), vals[i], out)
    return out


def _route_rows(score, bias):
    E, G, P = N_EXPERTS, N_GROUPS, EXPERTS_PER_GROUP
    sel = score + bias
    s = [sel[e:e + 1, :] for e in range(E)]
    raw = [score[e:e + 1, :] for e in range(E)]
    grp = []
    for gi in range(G):
        a = s[gi * P:(gi + 1) * P]
        best = None
        for i in range(P):
            for j in range(i + 1, P):
                pair = a[i] + a[j]
                best = pair if best is None else jnp.maximum(best, pair)
        grp.append(best)
    gmax = functools.reduce(jnp.maximum, grp)
    g_star = _first_index_of(grp, gmax)
    v = [_pick([s[gi * P + i] for gi in range(G)], g_star) for i in range(P)]
    w = [_pick([raw[gi * P + i] for gi in range(G)], g_star) for i in range(P)]
    i1 = _first_index_of(v, functools.reduce(jnp.maximum, v))
    v2 = [jnp.where(i1 == float(i), -jnp.inf, v[i]) for i in range(P)]
    i2 = _first_index_of(v2, functools.reduce(jnp.maximum, v2))
    w1, w2 = _pick(w, i1), _pick(w, i2)
    tot = w1 + w2
    zero = jnp.zeros_like(tot)
    e1, e2 = g_star * P + i1, g_star * P + i2
    n = score.shape[1]
    eidx = lax.broadcasted_iota(jnp.int32, (E, n), 0).astype(F32)
    oh1, oh2 = jnp.where(eidx == e1, 1.0, 0.0), jnp.where(eidx == e2, 1.0, 0.0)
    earlier = jnp.where(lax.broadcasted_iota(jnp.int32, (n, n), 0) < lax.broadcasted_iota(jnp.int32, (n, n), 1),
                        1.0, 0.0).astype(BF16)
    cnt = _dot(jnp.concatenate([oh1, oh2], axis=0).astype(BF16), earlier)
    rank1 = jnp.sum(oh1 * cnt[0:E], axis=0, keepdims=True)
    rank2 = jnp.sum(oh2 * cnt[E:2 * E], axis=0, keepdims=True)
    lane = lax.broadcasted_iota(jnp.int32, (E, LANES), 1)
    totals = jnp.where(lane == 0, jnp.sum(oh1, axis=1, keepdims=True),
                       jnp.where(lane == 1, jnp.sum(oh2, axis=1, keepdims=True), 0.0))
    return jnp.concatenate([e1, e2, w1 / tot, w2 / tot, rank1, rank2, zero, zero], axis=0), totals


def _merge_kernel(ys_ref, g_ref, bonus_ref, gng_ref, gnb_ref, bd_ref, yb_ref, pm_ref, x_ref, mod_ref,
                  ng_ref, wa_ref, wb_ref, wo_ref, rw_ref, rb_ref, o_x, o_h, o_route, o_tot):
    m = mod_ref[pl.program_id(0)]
    bd = bd_ref[...]
    y = ys_ref[...]
    mean = _head_sums(y, bd)
    yc = y - mean
    var = _head_sums(yc * yc, bd)
    ya = (yc * lax.rsqrt(var + RWKV_GN_EPS) * gng_ref[...] + gnb_ref[...] + bonus_ref[...]) * g_ref[...]
    pm = pm_ref[...]
    D = x_ref.shape[1]
    mix = (_sigmoid(pm[:, 0:D]) * _dot(ya.astype(BF16), wa_ref[...])
           + _sigmoid(pm[:, D:2 * D]) * _dot(yb_ref[...].astype(BF16), wb_ref[...]))
    x = x_ref[...] + m[2:3] * _dot(mix.astype(BF16), wo_ref[...])
    o_x[...] = x
    ms = jnp.mean(x * x, axis=-1, keepdims=True)
    h = x * lax.rsqrt(ms + NORM_EPS) * ng_ref[...]
    h = h * (1.0 + m[4:5]) + m[3:4]
    o_h[...] = _pack_bf16_pairs(h)
    score = _sigmoid(_dot_3pass(h, rw_ref).T[0:N_EXPERTS, :])
    o_route[...], o_tot[...] = _route_rows(score, rb_ref[...])


def _merge(ys, g, bonus, gn_g, gn_b, yb, pm, x2, mod, ng, wa, wb, wo, router_w, router_b, B, T, tm=256):
    N, D = x2.shape
    W = RWKV_WIDTH
    nt = T // tm
    row = lambda b, t: (b * nt + t, 0)
    full = lambda shape: pl.BlockSpec(shape, lambda b, t: (0,) * len(shape))
    return pl.pallas_call(
        _merge_kernel,
        grid=(B, nt),
        in_specs=[pl.BlockSpec((tm, W), row), pl.BlockSpec((tm, W), row), pl.BlockSpec((tm, W), row),
                  full((1, W)), full((1, W)), full((HEAD_SUM_WIDTH, HEAD_SUM_WIDTH)),
                  pl.BlockSpec((tm, NSA_WIDTH), row), pl.BlockSpec((tm, 2 * D), row),
                  pl.BlockSpec((tm, D), row), full((B, 6, D)), full((1, D)),
                  full((W, D)), full((NSA_WIDTH, D)), full((D, D)), full((2, D, LANES)),
                  full((N_EXPERTS, 1))],
        out_specs=[pl.BlockSpec((tm, D), row), pl.BlockSpec((tm, D // 2), row),
                   pl.BlockSpec((8, tm), lambda b, t: (0, b * nt + t)),
                   pl.BlockSpec((N_EXPERTS, LANES), lambda b, t: (b * nt + t, 0))],
        out_shape=[jax.ShapeDtypeStruct((N, D), F32), jax.ShapeDtypeStruct((N, D // 2), F32),
                   jax.ShapeDtypeStruct((8, N), F32), jax.ShapeDtypeStruct((N // tm * N_EXPERTS, LANES), F32)],
        compiler_params=_params("arbitrary", "arbitrary"),
        name="merge_out",
    )(ys, g, bonus, gn_g.reshape(1, W), gn_b.reshape(1, W), _head_block_diag(HEAD_SUM_WIDTH, 1.0 / HEAD_DIM),
      yb, pm, x2, mod, ng, wa, wb, wo,
      _hi_lo(jnp.zeros((D, LANES), F32).at[:, :N_EXPERTS].set(router_w)), router_b.reshape(N_EXPERTS, 1))


def _route(route, totals, N):
    wts = route[TOP_K:2 * TOP_K].T
    NK = N * TOP_K
    E = N_EXPERTS
    n_tiles = totals.shape[0] // E
    expert = route[0:TOP_K].astype(jnp.int32)
    rank = route[2 * TOP_K:3 * TOP_K].astype(jnp.int32)
    per = totals.reshape(n_tiles, E, LANES)[:, :, 0:TOP_K].astype(jnp.int32).transpose(0, 2, 1)
    per = per.reshape(n_tiles * TOP_K, E)
    csum = jnp.cumsum(per, axis=0)
    counts = csum[-1]
    padded = (counts + MOE_BLOCK - 1) // MOE_BLOCK * MOE_BLOCK
    pad_end = jnp.cumsum(padded)
    pad_start = pad_end - padded
    first = (pad_start[None, :] + csum - per).reshape(n_tiles, TOP_K, E).transpose(1, 0, 2)
    first = jnp.repeat(first, N // n_tiles, axis=1)
    mine = expert[:, :, None] == jnp.arange(E, dtype=jnp.int32)[None, None, :]
    dest = (jnp.sum(jnp.where(mine, first, 0), axis=-1) + rank).reshape(-1)
    n_blk = -(-NK // MOE_BLOCK) + N_EXPERTS
    blk_start = jnp.arange(n_blk, dtype=jnp.int32) * MOE_BLOCK
    blk_expert = jnp.sum((pad_end[None, :] <= blk_start[:, None]).astype(jnp.int32), axis=1)
    blk_expert = jnp.clip(blk_expert, 0, N_EXPERTS - 1)
    blk_valid = jnp.clip((pad_start + counts)[blk_expert] - blk_start, 0, MOE_BLOCK).astype(jnp.int32)
    dest = jnp.pad(dest.astype(jnp.int32).reshape(NK // SC_WINDOW, SC_WINDOW), ((0, 0), (0, LANES - SC_WINDOW)))
    return wts, dest, blk_expert, blk_valid, n_blk


SC_WINDOW = 64


def _sc_mesh():
    return plsc.VectorSubcoreMesh(core_axis_name="c", subcore_axis_name="s")


def _sc_dispatch(h, dest, n_slots):
    N, D = h.shape
    W = SC_WINDOW
    nw = N // W

    @pl.kernel(out_type=jax.ShapeDtypeStruct((n_slots, D), h.dtype), mesh=_sc_mesh(), scratch_types=[])
    def dispatch(h_hbm, i_hbm, o_hbm):
        def body(x_vmem, i_vmem):
            pltpu.sync_copy(x_vmem, o_hbm.at[i_vmem.at[0, pl.ds(0, W)]])

        pltpu.emit_pipeline(
            body, grid=(TOP_K, nw),
            in_specs=[pl.BlockSpec((W, D), lambda k, i: (i, 0)),
                      pl.BlockSpec((1, LANES), lambda k, i: (k * nw + i, 0))],
            out_specs=[], core_axis_name=("c", "s"),
            dimension_semantics=(pltpu.PARALLEL, pltpu.PARALLEL))(h_hbm, i_hbm)

    return dispatch(h, dest)


def _sc_collect(ys, dest):
    W = SC_WINDOW
    NK = dest.shape[0] * W
    D = ys.shape[1]
    half = NK // TOP_K // W

    @pl.kernel(out_type=jax.ShapeDtypeStruct((NK, D), ys.dtype), mesh=_sc_mesh(), scratch_types=[])
    def collect(y_hbm, i_hbm, o_hbm):
        def body(i_vmem, o_vmem):
            pltpu.sync_copy(y_hbm.at[i_vmem.at[0, pl.ds(0, W)]], o_vmem)

        pltpu.emit_pipeline(
            body, grid=(TOP_K, half),
            in_specs=[pl.BlockSpec((1, LANES), lambda k, i: (k * half + i, 0))],
            out_specs=[pl.BlockSpec((W, D), lambda k, i: (k * half + i, 0))],
            core_axis_name=("c", "s"),
            dimension_semantics=(pltpu.PARALLEL, pltpu.PARALLEL))(i_hbm, o_hbm)

    return collect(ys, dest)


def _moe_dense_kernel(be_ref, nv_ref, x_ref, wg_ref, wu_ref, wd_ref, o_ref, wg_b, wu_b, wd_b):
    i = pl.program_id(0)
    nv = nv_ref[i]

    @pl.when((i == 0) | (be_ref[i] != be_ref[jnp.maximum(i - 1, 0)]))
    def _():
        wg_b[...] = wg_ref[0, 0].astype(BF16)
        wu_b[...] = wu_ref[0, 0].astype(BF16)
        wd_b[...] = wd_ref[0, 0].astype(BF16)

    @pl.when(nv > 0)
    def _():
        x_lo, x_hi = _unpack_bf16_pairs(x_ref[...])
        half = x_lo.shape[1]
        gate = _dot(x_lo, wg_b[0:half, :]) + _dot(x_hi, wg_b[half:, :])
        up = _dot(x_lo, wu_b[0:half, :]) + _dot(x_hi, wu_b[half:, :])
        o_ref[...] = _pack_bf16_pairs(_dot((gate * _sigmoid(gate) * up).astype(BF16), wd_b[...]))

    @pl.when(nv == 0)
    def _():
        o_ref[...] = jnp.zeros_like(o_ref)


def _moe_dense(xs, blk_expert, blk_valid, n_blk, layer, wg, wu, wd):
    P = xs.shape[0]
    D, DE = wg.shape[2:]
    wmap = lambda i, be, nv: (layer, be[i], 0, 0)
    grid_spec = pltpu.PrefetchScalarGridSpec(
        num_scalar_prefetch=2,
        grid=(n_blk,),
        in_specs=[pl.BlockSpec((MOE_BLOCK, D // 2), lambda i, be, nv: (i, 0)), pl.BlockSpec((1, 1, D, DE), wmap),
                  pl.BlockSpec((1, 1, D, DE), wmap), pl.BlockSpec((1, 1, DE, D), wmap)],
        out_specs=pl.BlockSpec((MOE_BLOCK, D // 2), lambda i, be, nv: (i, 0)),
        scratch_shapes=[pltpu.VMEM((D, DE), BF16), pltpu.VMEM((D, DE), BF16), pltpu.VMEM((DE, D), BF16)],
    )
    return pl.pallas_call(
        _moe_dense_kernel,
        grid_spec=grid_spec,
        out_shape=jax.ShapeDtypeStruct((P, D // 2), F32),
        compiler_params=_params("arbitrary"),
        name="moe_experts",
    )(blk_expert, blk_valid, xs, wg, wu, wd)


def _final_kernel(x_ref, y0_ref, y1_ref, w_ref, mod_ref, o_ref):
    o_ref[...] = _moe_residual(x_ref, y0_ref, y1_ref, w_ref, mod_ref[pl.program_id(0)])


def _final(x2, ybuf, wts, mod, B, T, tm=512):
    N, D = x2.shape
    nt = T // tm
    row = lambda b, t: (b * nt + t, 0)
    return pl.pallas_call(
        _final_kernel,
        grid=(B, nt),
        in_specs=[pl.BlockSpec((tm, D), row), pl.BlockSpec((tm, D // 2), row),
                  pl.BlockSpec((tm, D // 2), lambda b, t: (N // tm + b * nt + t, 0)),
                  pl.BlockSpec((tm, TOP_K), row), pl.BlockSpec((B, 6, D), lambda b, t: (0, 0, 0))],
        out_specs=pl.BlockSpec((tm, D), row),
        out_shape=jax.ShapeDtypeStruct((N, D), F32),
        compiler_params=_params("arbitrary", "arbitrary"),
        name="moe_combine",
    )(x2, ybuf, ybuf, wts, mod)


def _overlap_t(n_sel, n_cmp_pad):
    ci = jnp.arange(n_cmp_pad)[None, :] * CMP_STRIDE
    sj = jnp.arange(n_sel)[:, None] * SEL_BLOCK
    ov = (ci <= sj + SEL_BLOCK - 1) & (ci + CMP_BLOCK - 1 >= sj) & (jnp.arange(n_cmp_pad)[None, :] < n_cmp_pad - 1)
    return ov.astype(BF16)


def kernel(x, c, w_ada, b_ada, norm_g, w_in, b_in, rwkv_mu, rwkv_w0, rwkv_w2, rwkv_a0, rwkv_a2, rwkv_g2,
           rwkv_k_k, rwkv_k_a, rwkv_r_k, rwkv_gn_g, rwkv_gn_b, qk_norm_g, cmp_pos, cmp_w1, cmp_w2,
           w_up_rwkv, w_up_nsa, w_out, router_w, router_b, exp_w_gate, exp_w_up, exp_w_down):
    B, T, D = x.shape
    L = w_ada.shape[0]
    N = B * T
    mods = _ada(c, w_ada, b_ada)
    tables = _rope_tables(jnp.arange(T, dtype=jnp.int32))
    nch = T // CMP_STRIDE
    tables_cmp = _rope_tables(jnp.arange(nch, dtype=jnp.int32) * CMP_STRIDE + CMP_BLOCK - 1)
    ov_t = _overlap_t(T // SEL_BLOCK, nch)
    n_gate = NSA_GATE_COLS
    x2 = x.reshape(N, D)
    pending_moe = None
    for l in range(L):
        g0 = _SEG_KV[1] + n_gate
        w_pad = jnp.concatenate([w_in[l][:, :g0], jnp.zeros((D, GATE_PAD - n_gate), F32), w_in[l][:, g0:]],
                                axis=1).astype(BF16)
        b_pad = jnp.concatenate([b_in[l][:g0], jnp.zeros((GATE_PAD - n_gate,), F32), b_in[l][g0:]]).reshape(1, -1)
        outs = _inproj(x2, pending_moe, mods[l], norm_g[l, 0].reshape(1, D), w_pad, b_pad, B, T)
        if pending_moe is not None:
            x2, outs = outs[0], outs[1:]
        p_rw, p_q, p_kv, p_gate, p_merge = outs
        r, k, v, al, bb, ld, g, bonus = _rwkv_pre(p_rw, rwkv_mu[l], rwkv_w0[l], rwkv_w2[l], rwkv_a0[l],
                                                  rwkv_a2[l], rwkv_g2[l], rwkv_k_k[l], rwkv_k_a[l],
                                                  rwkv_r_k[l], B, T)
        ys = _rwkv_scan(r, k, v, al, bb, ld, B, T)
        qt, ks, kw, vst, vsd, vwt = _nsa_prep(p_q, p_kv, tables, qk_norm_g[l], B, T)
        kv3 = p_kv.reshape(B, T, KV_COLS)
        kcmp = _nsa_cmp(kv3, 0, cmp_pos[l], cmp_w1[l], cmp_w2[l], qk_norm_g[l, 1], tables_cmp)
        vct = _nsa_cmp(kv3, 1, cmp_pos[l], cmp_w1[l], cmp_w2[l], None, None)
        gt = p_gate[:, :n_gate].reshape(B, T, NSA_KV_HEADS, NSA_GROUP, 3).transpose(0, 2, 4, 3, 1)
        yb = _nsa_attn(qt, kcmp, vct, ks.reshape(B, T, 2 * LANES), vst, vsd, kw.reshape(B, T, LANES), vwt, gt, ov_t,
                       B, T)
        x2, h2, route, totals = _merge(ys, g, bonus, rwkv_gn_g[l], rwkv_gn_b[l], yb, p_merge, x2, mods[l],
                               norm_g[l, 1].reshape(1, D), w_up_rwkv[l].astype(BF16),
                               w_up_nsa[l].astype(BF16), w_out[l].astype(BF16), router_w, router_b, B, T)
        wts, dest, blk_expert, blk_valid, n_blk = _route(route, totals, N)
        xs = _sc_dispatch(h2, dest, n_blk * MOE_BLOCK)
        ys = _moe_dense(xs, blk_expert, blk_valid, n_blk, l, exp_w_gate, exp_w_up, exp_w_down)
        ybuf = _sc_collect(ys, dest)
        pending_moe = (ybuf, wts, mods[l])
    return _final(x2, *pending_moe, B, T).reshape(B, T, D)
```

```python
import functools
import math

import jax
import jax.numpy as jnp
from jax import lax
from jax.experimental import pallas as pl
from jax.experimental.pallas import tpu as pltpu
from jax.experimental.pallas import tpu_sc as plsc

F32 = jnp.float32
BF16 = jnp.bfloat16
HI = lax.Precision.HIGHEST

D_MODEL = 1024
RWKV_HEADS = 8
HEAD_DIM = 64
RWKV_WIDTH = RWKV_HEADS * HEAD_DIM
DECAY_LORA = 64
ICLR_LORA = 64
GATE_LORA = 128
RWKV_GN_EPS = 64e-5
RWKV_COLS = 3 * RWKV_WIDTH + DECAY_LORA + ICLR_LORA + GATE_LORA

NSA_Q_HEADS = 8
NSA_KV_HEADS = 2
NSA_GROUP = NSA_Q_HEADS // NSA_KV_HEADS
NSA_WIDTH = NSA_Q_HEADS * HEAD_DIM
CMP_STRIDE = 16
CMP_BLOCK = 2 * CMP_STRIDE
CMP_HIDDEN = 256
SEL_BLOCK = 64
SEL_SHIFT = 6
SEL_TOPK = 16
WINDOW = 512
FORCE_SCORE = 1e4
NEG_INF = -1e30
ROPE_THETA = 500000.0
ROPE_DIM = HEAD_DIM // 4
KV_COLS = 6 * NSA_KV_HEADS * HEAD_DIM
NSA_GATE_COLS = 3 * NSA_Q_HEADS
GATE_PAD = 128

N_EXPERTS = 16
N_GROUPS = 4
EXPERTS_PER_GROUP = N_EXPERTS // N_GROUPS
TOP_K = 2
D_EXPERT = 512
MOE_BLOCK = 256
NORM_EPS = 1e-6

LANES = 128
CHUNK = 64
KEY_TILE = 128
SEL_TILE = 512
CMP_VARIANTS = 8
V_ROWS = 80
Q_SCALE = HEAD_DIM ** -0.5 * math.log2(math.e)
Q_TILE = 128
F32_TINY = float(jnp.finfo(jnp.float32).tiny)

_SEG_RW = (0, RWKV_COLS)
_SEG_Q = (_SEG_RW[1], _SEG_RW[1] + NSA_WIDTH)
_SEG_KV = (_SEG_Q[1], _SEG_Q[1] + KV_COLS)
_SEG_GATE = (_SEG_KV[1], _SEG_KV[1] + GATE_PAD)
_SEG_MERGE = (_SEG_GATE[1], _SEG_GATE[1] + 2 * D_MODEL)
IN_COLS_PAD = _SEG_MERGE[1]

_VMEM_LIMIT = 56 * 1024 * 1024


def _dot(a, b, precision=None):
    return jnp.dot(a, b, preferred_element_type=F32, precision=precision)


def _dot_tb(a, b, precision=None):
    return lax.dot_general(a, b, (((1,), (1,)), ((), ())), preferred_element_type=F32,
                           precision=precision)


def _split_bf16(x, terms):
    parts = []
    for _ in range(terms - 1):
        parts.append(x.astype(BF16))
        x = x - parts[-1].astype(F32)
    parts.append(x.astype(BF16))
    return parts


def _dot_split_lhs(x, w_bf, terms=2):
    return functools.reduce(jnp.add, [_dot(p, w_bf) for p in _split_bf16(x, terms)])


def _dot_split_rhs(w_bf, x, terms=2):
    return functools.reduce(jnp.add, [_dot(w_bf, p) for p in _split_bf16(x, terms)])


def _dot_3pass(x, w_hl_ref):
    x_hi, x_lo = _split_bf16(x, 2)
    w_hi = w_hl_ref[0]
    return _dot(x_hi, w_hi) + _dot(x_lo, w_hi) + _dot(x_hi, w_hl_ref[1])


def _pack_bf16_pairs(x):
    n = x.shape[1] // 2
    bits = lax.bitcast_convert_type(x.astype(BF16).astype(F32), jnp.uint32)
    return lax.bitcast_convert_type(bits[:, 0:n] | (bits[:, n:] >> 16), F32)


def _unpack_bf16_pairs(packed):
    bits = lax.bitcast_convert_type(packed, jnp.uint32)
    lo = lax.bitcast_convert_type(bits & jnp.uint32(0xFFFF0000), F32)
    hi = lax.bitcast_convert_type(bits << 16, F32)
    return lo.astype(BF16), hi.astype(BF16)


def _hi_lo(w):
    hi = w.astype(BF16)
    return jnp.stack([hi, (w - hi.astype(F32)).astype(BF16)])


def _params(*sem):
    return pltpu.CompilerParams(dimension_semantics=sem, vmem_limit_bytes=_VMEM_LIMIT)


def _sigmoid(x):
    return 1.0 / (1.0 + jnp.exp(-x))


def _ada_kernel(c_ref, w_ref, b_ref, o_ref):
    c = c_ref[...]
    s = c * _sigmoid(c)
    o_ref[0] = _dot(s, w_ref[0], HI) + b_ref[0]


def _ada(c, w_ada, b_ada):
    L, D, D6 = w_ada.shape
    B = c.shape[0]
    rows = 8
    cp = jnp.zeros((rows, D), F32).at[:B].set(c)
    tn = 1536
    out = pl.pallas_call(
        _ada_kernel,
        grid=(L, D6 // tn),
        in_specs=[pl.BlockSpec((rows, D), lambda l, j: (0, 0)),
                  pl.BlockSpec((1, D, tn), lambda l, j: (l, 0, j)),
                  pl.BlockSpec((1, 1, tn), lambda l, j: (l, 0, j))],
        out_specs=pl.BlockSpec((1, rows, tn), lambda l, j: (l, 0, j)),
        out_shape=jax.ShapeDtypeStruct((L, rows, D6), F32),
        compiler_params=_params("arbitrary", "arbitrary"),
        name="ada_mod",
    )(cp, w_ada, b_ada.reshape(L, 1, D6))
    return out[:, :B].reshape(L, B, 6, D)


def _moe_residual(x_ref, y0_ref, y1_ref, w_ref, m_prev):
    w = w_ref[...]
    rows = lambda y_ref: jnp.concatenate([h.astype(F32) for h in _unpack_bf16_pairs(y_ref[...])], axis=1)
    return x_ref[...] + m_prev[5:6] * (w[:, 0:1] * rows(y0_ref) + w[:, 1:2] * rows(y1_ref))


def _inproj_kernel(*refs, after_moe):
    if after_moe:
        x_ref, y0_ref, y1_ref, wts_ref, modp_ref, mod_ref, g_ref, w_ref, b_ref, o_x = refs[:10]
        x = _moe_residual(x_ref, y0_ref, y1_ref, wts_ref, modp_ref[pl.program_id(0)])
        o_x[...] = x
    else:
        x_ref, mod_ref, g_ref, w_ref, b_ref = refs[:5]
        x = x_ref[...]
    o_rw, o_q, o_kv, o_gate, o_merge = refs[-5:]
    m = mod_ref[pl.program_id(0)]
    ms = jnp.mean(x * x, axis=-1, keepdims=True)
    h = x * lax.rsqrt(ms + NORM_EPS) * g_ref[...]
    h = h * (1.0 + m[1:2]) + m[0:1]
    hb = h.astype(BF16)
    for o, (a, e) in ((o_rw, _SEG_RW), (o_q, _SEG_Q), (o_kv, _SEG_KV), (o_gate, _SEG_GATE),
                      (o_merge, _SEG_MERGE)):
        o[...] = _dot(hb, w_ref[:, a:e]) + b_ref[:, a:e]


def _inproj(x2, pending_moe, mod, g, w_pad, b_pad, B, T, tm=512):
    N, D = x2.shape
    nt = T // tm
    row = lambda b, t: (b * nt + t, 0)
    mods_spec = pl.BlockSpec((B, 6, D), lambda b, t: (0, 0, 0))
    widths = [e - a for a, e in (_SEG_RW, _SEG_Q, _SEG_KV, _SEG_GATE, _SEG_MERGE)]
    in_specs = [pl.BlockSpec((tm, D), row)]
    args = [x2]
    if pending_moe is not None:
        ybuf, wts, mod_prev = pending_moe
        in_specs += [pl.BlockSpec((tm, D // 2), row),
                     pl.BlockSpec((tm, D // 2), lambda b, t: (N // tm + b * nt + t, 0)),
                     pl.BlockSpec((tm, TOP_K), row), mods_spec]
        args += [ybuf, ybuf, wts, mod_prev]
        widths = [D] + widths
    in_specs += [mods_spec, pl.BlockSpec((1, D), lambda b, t: (0, 0)),
                 pl.BlockSpec((D, IN_COLS_PAD), lambda b, t: (0, 0)),
                 pl.BlockSpec((1, IN_COLS_PAD), lambda b, t: (0, 0))]
    return pl.pallas_call(
        functools.partial(_inproj_kernel, after_moe=pending_moe is not None),
        grid=(B, nt),
        in_specs=in_specs,
        out_specs=[pl.BlockSpec((tm, w), row) for w in widths],
        out_shape=[jax.ShapeDtypeStruct((N, w), F32) for w in widths],
        compiler_params=_params("arbitrary", "arbitrary"),
        name="in_proj",
    )(*args, mod, g, w_pad, b_pad)


def _rwkv_pre_kernel(p_ref, mu_ref, w0_ref, w2_ref, a0_ref, a2_ref, g2_ref, kk_ref, ka_ref, rk_ref,
                     bd_ref, o_r, o_k, o_v, o_al, o_b, o_ld, o_g, o_bonus, carry_ref):
    W = RWKV_WIDTH

    @pl.when(pl.program_id(1) == 0)
    def _():
        carry_ref[...] = jnp.zeros_like(carry_ref)

    p = p_ref[...]
    ts = p.shape[0]
    rows = lax.broadcasted_iota(jnp.int32, p.shape, 0)
    shifted = jnp.where(rows == 0, carry_ref[0:1, :], pltpu.roll(p, 1, 0))
    carry_ref[0:1, :] = p[ts - 1:ts, :]
    pm = p + (shifted - p) * mu_ref[...]
    r = pm[:, 0:W]
    k = pm[:, W:2 * W]
    v = pm[:, 2 * W:3 * W]
    wa = pm[:, 3 * W:3 * W + DECAY_LORA + ICLR_LORA]
    gl = pm[:, 3 * W + DECAY_LORA + ICLR_LORA:]
    xw = w0_ref[...] + _dot_3pass(jnp.tanh(wa), w2_ref)
    ld = -math.exp(-0.5) * _sigmoid(xw)
    a = _sigmoid(a0_ref[...] + _dot_3pass(wa, a2_ref))
    g = _dot_3pass(_sigmoid(gl), g2_ref)
    bd = bd_ref[...]
    kk = k * kk_ref[...]
    nrm = jnp.sqrt(_head_sums(kk * kk, bd))
    kk = kk / jnp.maximum(nrm, 1e-12)
    k2 = k * (1.0 + (a - 1.0) * ka_ref[...])
    bonus = _head_sums(r * k2 * rk_ref[...], bd) * v
    o_r[...] = r
    o_k[...] = k2
    o_v[...] = v
    o_al[...] = kk
    o_b[...] = -kk * a
    o_ld[...] = ld
    o_g[...] = g
    o_bonus[...] = bonus


HEAD_SUM_WIDTH = 256


def _head_sums(x, bd):
    w = bd.shape[0]
    parts = [_dot_split_lhs(x[:, j:j + w], bd) for j in range(0, x.shape[1], w)]
    return parts[0] if len(parts) == 1 else jnp.concatenate(parts, axis=1)


def _head_block_diag(width, scale=1.0):
    i = jnp.arange(width) // HEAD_DIM
    return ((i[:, None] == i[None, :]).astype(F32) * scale).astype(BF16)


def _rwkv_pre(p_rw, mu, w0, w2, a0, a2, g2, k_k, k_a, r_k, B, T, ts=512):
    N = p_rw.shape[0]
    W = RWKV_WIDTH
    nt = T // ts
    row = lambda b, t: (b * nt + t, 0)
    zl = jnp.zeros((DECAY_LORA, W), F32)
    w2p = jnp.concatenate([w2, zl], axis=0)
    a2p = jnp.concatenate([zl, a2], axis=0)
    full = lambda shape: pl.BlockSpec(shape, lambda b, t: (0,) * len(shape))
    vec = lambda z: z.reshape(1, -1)
    return pl.pallas_call(
        _rwkv_pre_kernel,
        grid=(B, nt),
        in_specs=[pl.BlockSpec((ts, RWKV_COLS), row), full((1, RWKV_COLS)), full((1, W)),
                  full((2, 2 * DECAY_LORA, W)), full((1, W)), full((2, 2 * DECAY_LORA, W)),
                  full((2, GATE_LORA, W)), full((1, W)), full((1, W)), full((1, W)),
                  full((HEAD_SUM_WIDTH, HEAD_SUM_WIDTH))],
        out_specs=[pl.BlockSpec((ts, W), row)] * 8,
        out_shape=[jax.ShapeDtypeStruct((N, W), F32)] * 8,
        scratch_shapes=[pltpu.VMEM((8, RWKV_COLS), F32)],
        compiler_params=_params("arbitrary", "arbitrary"),
        name="rwkv_pre",
    )(p_rw, vec(mu), vec(w0), _hi_lo(w2p), vec(a0), _hi_lo(a2p), _hi_lo(g2), vec(k_k), vec(k_a), vec(r_k),
      _head_block_diag(HEAD_SUM_WIDTH))


def _bf(x):
    return x.astype(BF16)


def _scan_local(chunks, eye, strict, incl, m0, m1, between_stages=lambda: None):
    C = CHUNK
    n = range(len(chunks))
    st = lambda z: jnp.concatenate([z * m0, z * m1], axis=0)
    zero = jnp.zeros((2 * C, 2 * C), F32)
    at_b, rt_s, vs, vs_b, lhs_a, rhs_a, bk_t, dcol = [], [], [], [], [], [], [], []
    for r, k, v, al, bb, ld, cum in chunks:
        tot = cum[C - 1:C, :]
        dinv = jnp.exp(-cum)
        dend = jnp.exp(tot - cum)
        at_b.append(_bf(st(al * jnp.exp(cum - ld))))
        rt_s.append(st(r * jnp.exp(cum)))
        vs.append(st(v))
        vs_b.append(_bf(vs[-1]))
        lhs_a.append(jnp.concatenate([at_b[-1], _bf(rt_s[-1])], axis=0))
        rhs_a.append(_bf(jnp.concatenate([st(bb * dinv), st(k * dinv)], axis=0)))
        bk_t.append(_bf(jnp.concatenate([st(bb * dend).T, st(k * dend).T], axis=1)))
        dcol.append(jnp.sum(eye * jnp.exp(tot), axis=1, keepdims=True))
    between_stages()
    A = [_dot_tb(lhs_a[i], rhs_a[i]) for i in n]
    between_stages()
    a_ab = [jnp.where(strict, A[i][0:2 * C, 0:2 * C], zero) for i in n]
    a_ak = [_bf(jnp.where(strict, A[i][0:2 * C, 2 * C:4 * C], zero)) for i in n]
    a_r = [_bf(jnp.concatenate([jnp.where(incl, A[i][2 * C:4 * C, 0:2 * C], zero),
                                jnp.where(incl, A[i][2 * C:4 * C, 2 * C:4 * C], zero)], axis=1)) for i in n]
    akv = [_bf(_dot(a_ak[i], vs_b[i])) for i in n]
    between_stages()
    tinv = [eye + a_ab[i] for i in n]
    pw_b = [_bf(a_ab[i]) for i in n]
    pw_b = [_bf(_dot(pw_b[i], pw_b[i])) for i in n]
    between_stages()
    for step in range(5):
        rhs = [jnp.concatenate([pw_b[i], _bf(tinv[i])], axis=1) for i in n]
        if step == 4:
            rhs = [_bf(tinv[i]) for i in n]
        prod = [_dot(pw_b[i], rhs[i]) for i in n]
        tinv = [tinv[i] + prod[i][:, -2 * C:] for i in n]
        pw_b = [_bf(prod[i][:, 0:2 * C]) for i in n]
        between_stages()
    X = [_dot(_bf(tinv[i]), jnp.concatenate([at_b[i], akv[i]], axis=1)) for i in n]
    between_stages()
    w_b = [_bf(X[i][:, 0:LANES]) for i in n]
    uv0 = [jnp.concatenate([_bf(X[i][:, LANES:2 * LANES]), vs_b[i]], axis=0) for i in n]
    m_h = [_bf(_dot(bk_t[i][:, 0:2 * C], w_b[i])) for i in n]
    g_h = [_dot(bk_t[i], uv0[i]) for i in n]
    between_stages()
    q_h = [_bf(rt_s[i] + _dot(a_r[i][:, 0:2 * C], w_b[i])) for i in n]
    y0 = [_dot(a_r[i], uv0[i]) for i in n]
    return [(m_h[i], g_h[i], dcol[i], q_h[i], y0[i]) for i in n]


def _rwkv_scan_kernel(r_ref, k_ref, v_ref, al_ref, b_ref, ld_ref, o_ref, h_ref, *local_refs):
    C = CHUNK
    tc = r_ref.shape[0]
    nc = tc // C

    @pl.when(pl.program_id(2) == 0)
    def _():
        h_ref[...] = jnp.zeros_like(h_ref)
        for ref in local_refs:
            ref[...] = jnp.zeros_like(ref)

    seq = {"H": h_ref[...], "c": 0}

    def one_step():
        c = seq["c"]
        if c < nc:
            m_h, g_h, dcol, q_h, y0 = (ref[c] for ref in local_refs)
            h_b = _bf(seq["H"])
            Y = _dot(q_h, h_b) + y0
            o_ref[c * C:(c + 1) * C, :] = Y[0:C] + Y[C:2 * C]
            seq["H"] = dcol * seq["H"] + _dot(m_h, h_b) + g_h
            seq["c"] = c + 1

    tri = jnp.where(lax.broadcasted_iota(jnp.int32, (C, C), 1) <= lax.broadcasted_iota(jnp.int32, (C, C), 0),
                    1.0, 0.0).astype(BF16)
    r2 = lax.broadcasted_iota(jnp.int32, (2 * C, 2 * C), 0)
    c2 = lax.broadcasted_iota(jnp.int32, (2 * C, 2 * C), 1)
    eye = (r2 == c2).astype(F32)
    strict = (c2 & (C - 1)) < (r2 & (C - 1))
    incl = (c2 & (C - 1)) <= (r2 & (C - 1))
    lane = lax.broadcasted_iota(jnp.int32, (C, LANES), 1)
    m0 = (lane < HEAD_DIM).astype(F32)
    m1 = 1.0 - m0
    cum = _dot_split_rhs(tri, jnp.concatenate([ld_ref[c * C:(c + 1) * C, :] for c in range(nc)], axis=1), 3)
    chunks = []
    for c in range(nc):
        sl = slice(c * C, (c + 1) * C)
        chunks.append((r_ref[sl, :], k_ref[sl, :], v_ref[sl, :], al_ref[sl, :], b_ref[sl, :], ld_ref[sl, :],
                       cum[:, c * LANES:(c + 1) * LANES]))
    local = _scan_local(chunks, eye, strict, incl, m0, m1, between_stages=one_step)
    while seq["c"] < nc:
        one_step()
    h_ref[...] = seq["H"]
    for c, parts in enumerate(local):
        for ref, part in zip(local_refs, parts):
            ref[c] = part


def _rwkv_scan(r, k, v, al, bb, ld, B, T, tc=512):
    N, W = r.shape
    nt = T // tc
    nc = tc // CHUNK
    in_spec = pl.BlockSpec((tc, LANES), lambda b, h, t: (b * nt + jnp.minimum(t, nt - 1), h))
    out_spec = pl.BlockSpec((tc, LANES), lambda b, h, t: (b * nt + jnp.maximum(t - 1, 0), h))
    sq = (nc, LANES, LANES)
    return pl.pallas_call(
        _rwkv_scan_kernel,
        grid=(B, W // LANES, nt + 1),
        in_specs=[in_spec] * 6,
        out_specs=out_spec,
        out_shape=jax.ShapeDtypeStruct((N, W), F32),
        scratch_shapes=[pltpu.VMEM((LANES, LANES), F32), pltpu.VMEM(sq, BF16), pltpu.VMEM(sq, F32),
                        pltpu.VMEM((nc, LANES, 1), F32), pltpu.VMEM(sq, BF16), pltpu.VMEM(sq, F32)],
        compiler_params=_params("arbitrary", "arbitrary", "arbitrary"),
        name="rwkv_scan",
    )(r, k, v, al, bb, ld)


def _rope_tables(pos):
    half = ROPE_DIM // 2
    inv = jnp.power(ROPE_THETA, -jnp.arange(half, dtype=F32) * 2.0 / ROPE_DIM)
    ang = pos.astype(F32)[:, None] * inv[None, :]
    cos, sin = jnp.cos(ang), jnp.sin(ang)
    n = pos.shape[0]
    rest = HEAD_DIM - ROPE_DIM
    c = jnp.concatenate([cos, cos, jnp.ones((n, rest), F32)], axis=1)
    s_dn = jnp.concatenate([-sin, jnp.zeros((n, half + rest), F32)], axis=1)
    s_up = jnp.concatenate([jnp.zeros((n, half), F32), sin, jnp.zeros((n, rest), F32)], axis=1)
    rep = LANES // HEAD_DIM
    return jnp.tile(c, (1, rep)), jnp.tile(s_dn, (1, rep)), jnp.tile(s_up, (1, rep))


def _norm_rope(x, bd, g, c, s_dn, s_up):
    width = x.shape[1]
    half = ROPE_DIM // 2
    rep = width // LANES
    tile = (lambda z: jnp.concatenate([z] * rep, axis=1)) if rep > 1 else (lambda z: z)
    ms = _head_sums(x * x, bd)
    xn = x * lax.rsqrt(ms + NORM_EPS) * g
    return (xn * tile(c) + pltpu.roll(xn, width - half, 1) * tile(s_dn)
            + pltpu.roll(xn, half, 1) * tile(s_up))


def _nsa_prep_kernel(q_ref, kv_ref, c_ref, sd_ref, su_ref, gq_ref, gs_ref, gw_ref, bdq_ref, bdk_ref,
                     o_qt, o_ks, o_kw, o_vst, o_vsd, o_vwt):
    c, sd, su = c_ref[...], sd_ref[...], su_ref[...]
    q = _norm_rope(q_ref[...], bdq_ref[...], gq_ref[...], c, sd, su) * Q_SCALE
    qt = q.T
    ts = q.shape[0]
    kv = kv_ref[...]
    bdk = bdk_ref[...]
    pos = pl.program_id(1) * ts + lax.broadcasted_iota(jnp.int32, (ts, LANES), 0)
    blk_onehot = jnp.where((pos >> SEL_SHIFT) == lax.broadcasted_iota(jnp.int32, (ts, LANES), 1), 1.0, 0.0)
    ks = _norm_rope(kv[:, 2 * LANES:3 * LANES], bdk, gs_ref[...], c, sd, su)
    o_ks[...] = jnp.concatenate([ks, blk_onehot], axis=1).astype(BF16)
    o_kw[...] = _norm_rope(kv[:, 4 * LANES:5 * LANES], bdk, gw_ref[...], c, sd, su).astype(BF16)
    ones_rows = jnp.where(lax.broadcasted_iota(jnp.int32, (V_ROWS - HEAD_DIM, q.shape[0]), 0) == 0, 1.0, 0.0)

    def values_t(x):
        xt = x.T
        return jnp.concatenate([xt[0:HEAD_DIM], ones_rows, xt[HEAD_DIM:2 * HEAD_DIM], ones_rows], axis=0)

    vst = values_t(kv[:, 3 * LANES:4 * LANES])
    vwt = values_t(kv[:, 5 * LANES:6 * LANES])
    for j in range(q.shape[0] // KEY_TILE):
        sl = slice(j * KEY_TILE, (j + 1) * KEY_TILE)
        o_qt[0, j] = qt[:, sl].astype(BF16)
        o_vsd[0, j] = vst[:, sl].astype(BF16)
        o_vwt[0, j] = vwt[:, sl].astype(BF16)
    for j in range(q.shape[0] // SEL_TILE):
        o_vst[0, j] = vst[:, j * SEL_TILE:(j + 1) * SEL_TILE].astype(BF16)


def _nsa_prep(q, kv, tables, qk_g, B, T, ts=512):
    N = q.shape[0]
    nt = T // ts
    nk = ts // KEY_TILE
    ns = ts // SEL_TILE
    row = lambda b, t: (b * nt + t, 0)
    full = lambda shape: pl.BlockSpec(shape, lambda b, t: (0,) * len(shape))
    tab = pl.BlockSpec((ts, LANES), lambda b, t: (t, 0))
    gq = jnp.tile(qk_g[0], NSA_Q_HEADS).reshape(1, NSA_WIDTH)
    gs = jnp.tile(qk_g[2], NSA_KV_HEADS).reshape(1, LANES)
    gw = jnp.tile(qk_g[3], NSA_KV_HEADS).reshape(1, LANES)
    tiled = lambda rows: pl.BlockSpec((1, nk, rows, KEY_TILE), lambda b, t: (b, t, 0, 0))
    return pl.pallas_call(
        _nsa_prep_kernel,
        grid=(B, nt),
        in_specs=[pl.BlockSpec((ts, NSA_WIDTH), row), pl.BlockSpec((ts, KV_COLS), row), tab, tab, tab,
                  full((1, NSA_WIDTH)), full((1, LANES)), full((1, LANES)),
                  full((HEAD_SUM_WIDTH, HEAD_SUM_WIDTH)), full((LANES, LANES))],
        out_specs=[tiled(NSA_WIDTH), pl.BlockSpec((ts, 2 * LANES), row), pl.BlockSpec((ts, LANES), row),
                   pl.BlockSpec((1, ns, NSA_KV_HEADS * V_ROWS, SEL_TILE), lambda b, t: (b, t, 0, 0)),
                   tiled(NSA_KV_HEADS * V_ROWS), tiled(NSA_KV_HEADS * V_ROWS)],
        out_shape=[jax.ShapeDtypeStruct((B, T // KEY_TILE, NSA_WIDTH, KEY_TILE), BF16),
                   jax.ShapeDtypeStruct((N, 2 * LANES), BF16), jax.ShapeDtypeStruct((N, LANES), BF16),
                   jax.ShapeDtypeStruct((B, T // SEL_TILE, NSA_KV_HEADS * V_ROWS, SEL_TILE), BF16),
                   jax.ShapeDtypeStruct((B, T // KEY_TILE, NSA_KV_HEADS * V_ROWS, KEY_TILE), BF16),
                   jax.ShapeDtypeStruct((B, T // KEY_TILE, NSA_KV_HEADS * V_ROWS, KEY_TILE), BF16)],
        compiler_params=_params("arbitrary", "arbitrary"),
        name="nsa_prep",
    )(q, kv, *tables, gq, gs, gw, _head_block_diag(HEAD_SUM_WIDTH, 1.0 / HEAD_DIM),
      _head_block_diag(LANES, 1.0 / HEAD_DIM))


def _gelu_tanh(x):
    return 0.5 * x * (1.0 + jnp.tanh(0.7978845608028654 * (x + 0.044715 * x * x * x)))


def _nsa_cmp_kernel(x_ref, pos_ref, w1_ref, w2_ref, *rest, is_key):
    if is_key:
        g_ref, c_ref, sd_ref, su_ref, bd_ref, o_ref, xs_ref = rest
    else:
        o_ref, xs_ref = rest
    nch = xs_ref.shape[0]
    S = CMP_STRIDE
    for j in range(S):
        xs_ref[:, j * LANES:(j + 1) * LANES] = x_ref[0, pl.ds(j, nch, stride=S), :]
    xs = xs_ref[...]
    first = _dot((xs + pos_ref[0:1, :]).astype(BF16), w1_ref[0])
    second = _dot((xs + pos_ref[1:2, :]).astype(BF16), w1_ref[1])
    hid = first + pltpu.roll(second, nch - 1, 0)
    out = _dot(_gelu_tanh(hid).astype(BF16), w2_ref[...])
    rows = lax.broadcasted_iota(jnp.int32, out.shape, 0)
    if is_key:
        out = _norm_rope(out, bd_ref[...], g_ref[...], c_ref[...], sd_ref[...], su_ref[...])
        o_ref[0] = jnp.where(rows < nch - 1, out, 0.0).astype(BF16)
    else:
        o_ref[0] = jnp.where(rows < nch - 1, out, 0.0).T.astype(BF16)


def _nsa_cmp(kv3, which, cmp_pos, cmp_w1, cmp_w2, g_k, tables_cmp):
    B, T, _ = kv3.shape
    S = CMP_STRIDE
    nch = T // S
    is_key = which == 0
    eye2 = jnp.eye(NSA_KV_HEADS, dtype=F32)
    w1 = cmp_w1[which].reshape(CMP_BLOCK, HEAD_DIM, CMP_HIDDEN)
    w1 = jnp.einsum('jdh,ge->jgdeh', w1, eye2).reshape(2, S * LANES, NSA_KV_HEADS * CMP_HIDDEN)
    w2 = jnp.einsum('hd,ge->ghed', cmp_w2[which], eye2).reshape(NSA_KV_HEADS * CMP_HIDDEN, LANES)
    pos = jnp.tile(cmp_pos[which].reshape(2, S, 1, HEAD_DIM), (1, 1, NSA_KV_HEADS, 1)).reshape(2, S * LANES)
    full = lambda shape: pl.BlockSpec(shape, lambda b: (0,) * len(shape))
    in_specs = [pl.BlockSpec((1, T, LANES), lambda b: (b, 0, which)), full(pos.shape), full(w1.shape),
                full(w2.shape)]
    args = [kv3, pos, w1.astype(BF16), w2.astype(BF16)]
    if is_key:
        in_specs += [full((1, LANES)), full((nch, LANES)), full((nch, LANES)), full((nch, LANES)),
                     full((LANES, LANES))]
        args += [jnp.tile(g_k, NSA_KV_HEADS).reshape(1, LANES), *tables_cmp,
                 _head_block_diag(LANES, 1.0 / HEAD_DIM)]
        out_spec = pl.BlockSpec((1, nch, LANES), lambda b: (b, 0, 0))
        out_shape = jax.ShapeDtypeStruct((B, nch, LANES), BF16)
    else:
        out_spec = pl.BlockSpec((1, LANES, nch), lambda b: (b, 0, 0))
        out_shape = jax.ShapeDtypeStruct((B, LANES, nch), BF16)
    return pl.pallas_call(
        functools.partial(_nsa_cmp_kernel, is_key=is_key),
        grid=(B,),
        in_specs=in_specs,
        out_specs=out_spec,
        out_shape=out_shape,
        scratch_shapes=[pltpu.VMEM((nch, S * LANES), F32)],
        compiler_params=_params("arbitrary"),
        name="nsa_cmp_k" if is_key else "nsa_cmp_v",
    )(*args)


def _nsa_attn_kernel(qt_ref, kc_ref, vct_ref, ks_ref, vst_ref, vsd_ref, kw_ref, vwt_ref, gt_ref, ov_ref, o_ref,
                     rhs_ref, oc_ref, keep_ref, s0_ref, s1_ref, s2_ref, s3_ref, p0_ref, p1_ref):
    qb = pl.program_id(1)
    G = NSA_KV_HEADS
    R = NSA_GROUP
    QT = Q_TILE
    KT = KEY_TILE
    CG = R * QT
    NQ = G * CG
    D = HEAD_DIM
    t0 = qb * QT
    n_cmp_pad = kc_ref.shape[1]
    n_sel = ov_ref.shape[0]
    cols = lambda g: slice(g * CG, (g + 1) * CG)

    q_cols = []
    for g in range(G):
        q_g = jnp.concatenate([qt_ref[0, 0, (g * R + r) * D:(g * R + r + 1) * D, :] for r in range(R)], axis=1)
        q_cols.append(jnp.concatenate([q_g if gg == g else jnp.zeros_like(q_g) for gg in range(G)], axis=0))
    qpad = jnp.concatenate(q_cols, axis=1)

    tq_row = t0 + (lax.broadcasted_iota(jnp.int32, (1, NQ), 1) & (QT - 1))
    spread = lambda z: jnp.concatenate([z[:, g * QT:(g + 1) * QT] for g in range(G) for _ in range(R)], axis=1)
    tile_all = lambda z: jnp.concatenate([z] * (G * R), axis=1)

    def values_dot(v_of_group, p):
        return jnp.concatenate([_dot(v_of_group(g), p[:, cols(g)]) for g in range(G)], axis=1)

    NV = CMP_VARIANTS
    nq = ks_ref.shape[1] // QT

    def compressed_and_select(n_c, n_b):
        sc = _dot(kc_ref[0, 0:n_c, :], qpad)
        n_i = lax.broadcasted_iota(jnp.int32, (n_c, 1), 0)
        cend = jnp.where(n_i < n_cmp_pad - 1, n_i * CMP_STRIDE + (CMP_BLOCK - 1), jnp.int32(2 ** 30))
        cvalid = cend <= tq_row
        sc = jnp.where(cvalid, sc, NEG_INF)
        mc = jnp.max(sc, axis=0, keepdims=True)
        ec = jnp.where(cvalid, jnp.exp2(sc - mc), 0.0)
        pc = ec / jnp.maximum(jnp.sum(ec, axis=0, keepdims=True), F32_TINY)
        pc_b = pc.astype(BF16)
        oc_ref[...] = values_dot(lambda g: vct_ref[0, g * D:(g + 1) * D, 0:n_c], pc_b)
        sums = []
        for g in range(G):
            acc = pc[:, g * CG:g * CG + QT]
            for r in range(1, R):
                acc = acc + pc[:, g * CG + r * QT:g * CG + (r + 1) * QT]
            sums.append(acc)
        imp = _dot_split_rhs(ov_ref[0:n_b, 0:n_c], jnp.concatenate(sums, axis=1))
        jb = lax.broadcasted_iota(jnp.int32, (n_b, G * QT), 0)
        jf = jb.astype(F32)
        tq_b = t0 + (lax.broadcasted_iota(jnp.int32, (n_b, G * QT), 1) & (QT - 1))
        cur = tq_b >> SEL_SHIFT
        forced = (jb == 0) | (jb == cur) | (jb == cur - 1)
        visible = jb * SEL_BLOCK <= tq_b
        score = jnp.where(visible, jnp.where(forced, FORCE_SCORE, imp), -1.0)
        sel = jnp.zeros((n_b, G * QT), F32)
        for _ in range(min(SEL_TOPK, n_b)):
            mx = jnp.max(score, axis=0, keepdims=True)
            jmin = jnp.min(jnp.where(score == mx, jf, 1e9), axis=0, keepdims=True)
            hit = jf == jmin
            sel = jnp.where(hit, 1.0, sel)
            score = jnp.where(hit, -3e38, score)
        keep_ref[0:n_b, :] = jnp.where(visible, sel, 0.0)
        if n_b < n_sel:
            keep_ref[n_b:n_sel, :] = jnp.zeros((n_sel - n_b, G * QT), F32)

    for v in range(NV):
        @pl.when((qb * NV) // nq == v)
        def _():
            compressed_and_select((v + 1) * n_cmp_pad // NV, (v + 1) * n_sel // NV)

    o_c = oc_ref[...]
    ji = lax.broadcasted_iota(jnp.int32, (n_sel, G * QT), 0)

    ST = SEL_TILE
    bias_all = (keep_ref[...] - 1.0) * (-NEG_INF)
    first_own = t0 // SEL_BLOCK
    vrows = lambda g: slice(g * V_ROWS, (g + 1) * V_ROWS)

    def with_bias_rows(bias):
        rows = spread(bias).astype(BF16)
        if n_sel < LANES:
            rows = jnp.concatenate([rows, jnp.zeros((LANES - n_sel, NQ), BF16)], axis=0)
        return jnp.concatenate([qpad, rows], axis=0)

    rhs_ref[...] = with_bias_rows(jnp.where(ji < first_own, bias_all, NEG_INF))
    n_tiles = (t0 + ST - 1) // ST
    last_tile = ks_ref.shape[1] // ST - 1
    p_bufs = (p0_ref, p1_ref)

    def sel_scores(kt, s_ref):
        k0 = pl.multiple_of(jnp.minimum(kt, last_tile) * ST, ST)
        s_ref[...] = _dot(ks_ref[0, pl.ds(k0, ST), :], rhs_ref[...])

    def sel_values(kt, slot, acc, alpha):
        kt = jnp.clip(kt, 0, last_tile)
        return acc * alpha + values_dot(lambda g: vst_ref[0, kt, vrows(g), :], p_bufs[slot][...])

    def sel_softmax(s_ref, slot, m):
        s = s_ref[...]
        m_new = jnp.maximum(m, jnp.max(s, axis=0, keepdims=True))
        p_bufs[slot][...] = jnp.exp2(s - m_new).astype(BF16)
        return m_new, jnp.exp2(m - m_new)

    def sel_pair(a, carry, s_now, s_next):
        m, acc, alpha0, alpha1 = carry
        acc = sel_values(a - 2, 0, acc, alpha0)
        acc = sel_values(a - 1, 1, acc, alpha1)
        sel_scores(a + 2, s_next[0])
        sel_scores(a + 3, s_next[1])
        m, alpha0 = sel_softmax(s_now[0], 0, m)
        m, alpha1 = sel_softmax(s_now[1], 1, m)
        return m, acc, alpha0, alpha1

    bufs_a, bufs_b = (s0_ref, s1_ref), (s2_ref, s3_ref)
    sel_scores(0, s0_ref)
    sel_scores(1, s1_ref)
    p0_ref[...] = jnp.zeros_like(p0_ref)
    p1_ref[...] = jnp.zeros_like(p1_ref)
    own = _dot(ks_ref[0, pl.ds(pl.multiple_of(t0, QT), QT), :], with_bias_rows(bias_all))

    n_wt = (WINDOW + QT) // KT
    k0w = pl.multiple_of(jnp.maximum(t0 - WINDOW, 0), KT)
    kt_w = k0w // KT
    keys_w = kw_ref[0, pl.ds(k0w, WINDOW + QT), :]
    dw = (t0 + lax.broadcasted_iota(jnp.int32, (WINDOW + QT, QT), 1)
          - (k0w + lax.broadcasted_iota(jnp.int32, (WINDOW + QT, QT), 0)))
    sw = _dot(keys_w, qpad) + tile_all(jnp.where(dw >= 0, jnp.where(dw < WINDOW, 0.0, NEG_INF), NEG_INF))
    pw = jnp.exp2(sw - jnp.max(sw, axis=0, keepdims=True)).astype(BF16)
    acc_w = values_dot(lambda g: vwt_ref[0, kt_w, vrows(g), :], pw[0:KT])
    for j in range(1, n_wt):
        acc_w = acc_w + values_dot(lambda g: vwt_ref[0, kt_w + j, vrows(g), :], pw[j * KT:(j + 1) * KT])

    n_pairs = (n_tiles + 1) // 2
    one = jnp.ones((1, NQ), F32)
    m_s, acc_s, alpha0, alpha1 = lax.fori_loop(
        0, n_pairs,
        lambda j, carry: lax.cond(j % 2 == 0,
                                  lambda c: sel_pair(2 * j, c, bufs_a, bufs_b),
                                  lambda c: sel_pair(2 * j, c, bufs_b, bufs_a), carry),
        (jnp.full((1, NQ), NEG_INF, F32), jnp.zeros((V_ROWS, NQ), F32), one, one))
    acc_s = sel_values(2 * n_pairs - 2, 0, acc_s, alpha0)
    acc_s = sel_values(2 * n_pairs - 1, 1, acc_s, alpha1)
    seen = lax.broadcasted_iota(jnp.int32, (QT, QT), 0) <= lax.broadcasted_iota(jnp.int32, (QT, QT), 1)
    own = jnp.where(tile_all(seen), own, NEG_INF)
    m_new = jnp.maximum(m_s, jnp.max(own, axis=0, keepdims=True))
    acc_s = acc_s * jnp.exp2(m_s - m_new) + values_dot(lambda g: vsd_ref[0, qb, vrows(g), :],
                                                       jnp.exp2(own - m_new).astype(BF16))

    gates = _sigmoid(gt_ref[0])
    grow = lambda j: jnp.concatenate([gates[g, j, r:r + 1, :] for g in range(G) for r in range(R)], axis=1)
    o = (grow(0) * o_c + grow(1) * (acc_s[0:D] / acc_s[D:D + 1])
         + grow(2) * (acc_w[0:D] / acc_w[D:D + 1]))
    halves = []
    for h in range(G * R // 2):
        pair = jnp.concatenate([o[:, (2 * h) * QT:(2 * h + 1) * QT],
                                o[:, (2 * h + 1) * QT:(2 * h + 2) * QT]], axis=0)
        halves.append(pair.T)
    o_ref[...] = jnp.concatenate(halves, axis=1)


def _nsa_attn(qt, kcmp, vct, ks3, vst, vsd, kw3, vwt, gt, ov_t, B, T):
    G, R = NSA_KV_HEADS, NSA_GROUP
    nq = T // Q_TILE
    nk = T // KEY_TILE
    nch = kcmp.shape[1]
    n_sel = ov_t.shape[0]
    NQ = G * R * Q_TILE
    assert (T // SEL_TILE) % 2 == 0 and n_sel <= LANES and Q_TILE == KEY_TILE
    assert nq % CMP_VARIANTS == 0 and n_sel % (8 * CMP_VARIANTS) == 0 and nch % (8 * CMP_VARIANTS) == 0
    return pl.pallas_call(
        _nsa_attn_kernel,
        grid=(B, nq),
        in_specs=[pl.BlockSpec((1, 1, NSA_WIDTH, Q_TILE), lambda b, q: (b, q, 0, 0)),
                  pl.BlockSpec((1, nch, LANES), lambda b, q: (b, 0, 0)),
                  pl.BlockSpec((1, G * HEAD_DIM, nch), lambda b, q: (b, 0, 0)),
                  pl.BlockSpec((1, T, 2 * LANES), lambda b, q: (b, 0, 0)),
                  pl.BlockSpec((1, T // SEL_TILE, G * V_ROWS, SEL_TILE), lambda b, q: (b, 0, 0, 0)),
                  pl.BlockSpec((1, nk, G * V_ROWS, KEY_TILE), lambda b, q: (b, 0, 0, 0)),
                  pl.BlockSpec((1, T, LANES), lambda b, q: (b, 0, 0)),
                  pl.BlockSpec((1, nk, G * V_ROWS, KEY_TILE), lambda b, q: (b, 0, 0, 0)),
                  pl.BlockSpec((1, G, 3, R, Q_TILE), lambda b, q: (b, 0, 0, 0, q)),
                  pl.BlockSpec((n_sel, nch), lambda b, q: (0, 0))],
        out_specs=pl.BlockSpec((Q_TILE, NSA_WIDTH), lambda b, q: (b * nq + q, 0)),
        out_shape=jax.ShapeDtypeStruct((B * T, NSA_WIDTH), F32),
        scratch_shapes=[pltpu.VMEM((2 * LANES, NQ), BF16), pltpu.VMEM((HEAD_DIM, NQ), F32),
                        pltpu.VMEM((n_sel, G * Q_TILE), F32),
                        *[pltpu.VMEM((SEL_TILE, NQ), F32)] * 4,
                        *[pltpu.VMEM((SEL_TILE, NQ), BF16)] * 2],
        compiler_params=_params("arbitrary", "arbitrary"),
        name="nsa_attn",
    )(qt, kcmp, vct, ks3, vst, vsd, kw3, vwt, gt, ov_t)


def _first_index_of(vals, target):
    idx = jnp.full_like(target, float(len(vals) - 1))
    for i in range(len(vals) - 2, -1, -1):
        idx = jnp.where(vals[i] == target, float(i), idx)
    return idx


def _pick(vals, idx):
    out = vals[-1]
    for i in range(len(vals) - 2, -1, -1):
        out = jnp.where(idx == float(i), vals[i], out)
    return out


def _route_rows(score, bias):
    E, G, P = N_EXPERTS, N_GROUPS, EXPERTS_PER_GROUP
    sel = score + bias
    s = [sel[e:e + 1, :] for e in range(E)]
    raw = [score[e:e + 1, :] for e in range(E)]
    grp = []
    for gi in range(G):
        a = s[gi * P:(gi + 1) * P]
        best = None
        for i in range(P):
            for j in range(i + 1, P):
                pair = a[i] + a[j]
                best = pair if best is None else jnp.maximum(best, pair)
        grp.append(best)
    gmax = functools.reduce(jnp.maximum, grp)
    g_star = _first_index_of(grp, gmax)
    v = [_pick([s[gi * P + i] for gi in range(G)], g_star) for i in range(P)]
    w = [_pick([raw[gi * P + i] for gi in range(G)], g_star) for i in range(P)]
    i1 = _first_index_of(v, functools.reduce(jnp.maximum, v))
    v2 = [jnp.where(i1 == float(i), -jnp.inf, v[i]) for i in range(P)]
    i2 = _first_index_of(v2, functools.reduce(jnp.maximum, v2))
    w1, w2 = _pick(w, i1), _pick(w, i2)
    tot = w1 + w2
    zero = jnp.zeros_like(tot)
    e1, e2 = g_star * P + i1, g_star * P + i2
    n = score.shape[1]
    eidx = lax.broadcasted_iota(jnp.int32, (E, n), 0).astype(F32)
    oh1, oh2 = jnp.where(eidx == e1, 1.0, 0.0), jnp.where(eidx == e2, 1.0, 0.0)
    earlier = jnp.where(lax.broadcasted_iota(jnp.int32, (n, n), 0) < lax.broadcasted_iota(jnp.int32, (n, n), 1),
                        1.0, 0.0).astype(BF16)
    cnt = _dot(jnp.concatenate([oh1, oh2], axis=0).astype(BF16), earlier)
    rank1 = jnp.sum(oh1 * cnt[0:E], axis=0, keepdims=True)
    rank2 = jnp.sum(oh2 * cnt[E:2 * E], axis=0, keepdims=True)
    lane = lax.broadcasted_iota(jnp.int32, (E, LANES), 1)
    totals = jnp.where(lane == 0, jnp.sum(oh1, axis=1, keepdims=True),
                       jnp.where(lane == 1, jnp.sum(oh2, axis=1, keepdims=True), 0.0))
    return jnp.concatenate([e1, e2, w1 / tot, w2 / tot, rank1, rank2, zero, zero], axis=0), totals


def _merge_kernel(ys_ref, g_ref, bonus_ref, gng_ref, gnb_ref, bd_ref, yb_ref, pm_ref, x_ref, mod_ref,
                  ng_ref, wa_ref, wb_ref, wo_ref, rw_ref, rb_ref, o_x, o_h, o_route, o_tot):
    m = mod_ref[pl.program_id(0)]
    bd = bd_ref[...]
    y = ys_ref[...]
    mean = _head_sums(y, bd)
    yc = y - mean
    var = _head_sums(yc * yc, bd)
    ya = (yc * lax.rsqrt(var + RWKV_GN_EPS) * gng_ref[...] + gnb_ref[...] + bonus_ref[...]) * g_ref[...]
    pm = pm_ref[...]
    D = x_ref.shape[1]
    mix = (_sigmoid(pm[:, 0:D]) * _dot(ya.astype(BF16), wa_ref[...])
           + _sigmoid(pm[:, D:2 * D]) * _dot(yb_ref[...].astype(BF16), wb_ref[...]))
    x = x_ref[...] + m[2:3] * _dot(mix.astype(BF16), wo_ref[...])
    o_x[...] = x
    ms = jnp.mean(x * x, axis=-1, keepdims=True)
    h = x * lax.rsqrt(ms + NORM_EPS) * ng_ref[...]
    h = h * (1.0 + m[4:5]) + m[3:4]
    o_h[...] = _pack_bf16_pairs(h)
    score = _sigmoid(_dot_3pass(h, rw_ref).T[0:N_EXPERTS, :])
    o_route[...], o_tot[...] = _route_rows(score, rb_ref[...])


def _merge(ys, g, bonus, gn_g, gn_b, yb, pm, x2, mod, ng, wa, wb, wo, router_w, router_b, B, T, tm=256):
    N, D = x2.shape
    W = RWKV_WIDTH
    nt = T // tm
    row = lambda b, t: (b * nt + t, 0)
    full = lambda shape: pl.BlockSpec(shape, lambda b, t: (0,) * len(shape))
    return pl.pallas_call(
        _merge_kernel,
        grid=(B, nt),
        in_specs=[pl.BlockSpec((tm, W), row), pl.BlockSpec((tm, W), row), pl.BlockSpec((tm, W), row),
                  full((1, W)), full((1, W)), full((HEAD_SUM_WIDTH, HEAD_SUM_WIDTH)),
                  pl.BlockSpec((tm, NSA_WIDTH), row), pl.BlockSpec((tm, 2 * D), row),
                  pl.BlockSpec((tm, D), row), full((B, 6, D)), full((1, D)),
                  full((W, D)), full((NSA_WIDTH, D)), full((D, D)), full((2, D, LANES)),
                  full((N_EXPERTS, 1))],
        out_specs=[pl.BlockSpec((tm, D), row), pl.BlockSpec((tm, D // 2), row),
                   pl.BlockSpec((8, tm), lambda b, t: (0, b * nt + t)),
                   pl.BlockSpec((N_EXPERTS, LANES), lambda b, t: (b * nt + t, 0))],
        out_shape=[jax.ShapeDtypeStruct((N, D), F32), jax.ShapeDtypeStruct((N, D // 2), F32),
                   jax.ShapeDtypeStruct((8, N), F32), jax.ShapeDtypeStruct((N // tm * N_EXPERTS, LANES), F32)],
        compiler_params=_params("arbitrary", "arbitrary"),
        name="merge_out",
    )(ys, g, bonus, gn_g.reshape(1, W), gn_b.reshape(1, W), _head_block_diag(HEAD_SUM_WIDTH, 1.0 / HEAD_DIM),
      yb, pm, x2, mod, ng, wa, wb, wo,
      _hi_lo(jnp.zeros((D, LANES), F32).at[:, :N_EXPERTS].set(router_w)), router_b.reshape(N_EXPERTS, 1))


def _route(route, totals, N):
    wts = route[TOP_K:2 * TOP_K].T
    NK = N * TOP_K
    E = N_EXPERTS
    n_tiles = totals.shape[0] // E
    expert = route[0:TOP_K].astype(jnp.int32)
    rank = route[2 * TOP_K:3 * TOP_K].astype(jnp.int32)
    per = totals.reshape(n_tiles, E, LANES)[:, :, 0:TOP_K].astype(jnp.int32).transpose(0, 2, 1)
    per = per.reshape(n_tiles * TOP_K, E)
    csum = jnp.cumsum(per, axis=0)
    counts = csum[-1]
    padded = (counts + MOE_BLOCK - 1) // MOE_BLOCK * MOE_BLOCK
    pad_end = jnp.cumsum(padded)
    pad_start = pad_end - padded
    first = (pad_start[None, :] + csum - per).reshape(n_tiles, TOP_K, E).transpose(1, 0, 2)
    first = jnp.repeat(first, N // n_tiles, axis=1)
    mine = expert[:, :, None] == jnp.arange(E, dtype=jnp.int32)[None, None, :]
    dest = (jnp.sum(jnp.where(mine, first, 0), axis=-1) + rank).reshape(-1)
    n_blk = -(-NK // MOE_BLOCK) + N_EXPERTS
    blk_start = jnp.arange(n_blk, dtype=jnp.int32) * MOE_BLOCK
    blk_expert = jnp.sum((pad_end[None, :] <= blk_start[:, None]).astype(jnp.int32), axis=1)
    blk_expert = jnp.clip(blk_expert, 0, N_EXPERTS - 1)
    blk_valid = jnp.clip((pad_start + counts)[blk_expert] - blk_start, 0, MOE_BLOCK).astype(jnp.int32)
    dest = jnp.pad(dest.astype(jnp.int32).reshape(NK // SC_WINDOW, SC_WINDOW), ((0, 0), (0, LANES - SC_WINDOW)))
    return wts, dest, blk_expert, blk_valid, n_blk


SC_WINDOW = 64


def _sc_mesh():
    return plsc.VectorSubcoreMesh(core_axis_name="c", subcore_axis_name="s")


def _sc_dispatch(h, dest, n_slots):
    N, D = h.shape
    W = SC_WINDOW
    nw = N // W

    @pl.kernel(out_type=jax.ShapeDtypeStruct((n_slots, D), h.dtype), mesh=_sc_mesh(), scratch_types=[])
    def dispatch(h_hbm, i_hbm, o_hbm):
        def body(x_vmem, i_vmem):
            pltpu.sync_copy(x_vmem, o_hbm.at[i_vmem.at[0, pl.ds(0, W)]])

        pltpu.emit_pipeline(
            body, grid=(TOP_K, nw),
            in_specs=[pl.BlockSpec((W, D), lambda k, i: (i, 0)),
                      pl.BlockSpec((1, LANES), lambda k, i: (k * nw + i, 0))],
            out_specs=[], core_axis_name=("c", "s"),
            dimension_semantics=(pltpu.PARALLEL, pltpu.PARALLEL))(h_hbm, i_hbm)

    return dispatch(h, dest)


def _sc_collect(ys, dest):
    W = SC_WINDOW
    NK = dest.shape[0] * W
    D = ys.shape[1]
    half = NK // TOP_K // W

    @pl.kernel(out_type=jax.ShapeDtypeStruct((NK, D), ys.dtype), mesh=_sc_mesh(), scratch_types=[])
    def collect(y_hbm, i_hbm, o_hbm):
        def body(i_vmem, o_vmem):
            pltpu.sync_copy(y_hbm.at[i_vmem.at[0, pl.ds(0, W)]], o_vmem)

        pltpu.emit_pipeline(
            body, grid=(TOP_K, half),
            in_specs=[pl.BlockSpec((1, LANES), lambda k, i: (k * half + i, 0))],
            out_specs=[pl.BlockSpec((W, D), lambda k, i: (k * half + i, 0))],
            core_axis_name=("c", "s"),
            dimension_semantics=(pltpu.PARALLEL, pltpu.PARALLEL))(i_hbm, o_hbm)

    return collect(ys, dest)


def _moe_dense_kernel(be_ref, nv_ref, x_ref, wg_ref, wu_ref, wd_ref, o_ref, wg_b, wu_b, wd_b):
    i = pl.program_id(0)
    nv = nv_ref[i]

    @pl.when((i == 0) | (be_ref[i] != be_ref[jnp.maximum(i - 1, 0)]))
    def _():
        wg_b[...] = wg_ref[0, 0].astype(BF16)
        wu_b[...] = wu_ref[0, 0].astype(BF16)
        wd_b[...] = wd_ref[0, 0].astype(BF16)

    @pl.when(nv > 0)
    def _():
        x_lo, x_hi = _unpack_bf16_pairs(x_ref[...])
        half = x_lo.shape[1]
        gate = _dot(x_lo, wg_b[0:half, :]) + _dot(x_hi, wg_b[half:, :])
        up = _dot(x_lo, wu_b[0:half, :]) + _dot(x_hi, wu_b[half:, :])
        o_ref[...] = _pack_bf16_pairs(_dot((gate * _sigmoid(gate) * up).astype(BF16), wd_b[...]))

    @pl.when(nv == 0)
    def _():
        o_ref[...] = jnp.zeros_like(o_ref)


def _moe_dense(xs, blk_expert, blk_valid, n_blk, layer, wg, wu, wd):
    P = xs.shape[0]
    D, DE = wg.shape[2:]
    wmap = lambda i, be, nv: (layer, be[i], 0, 0)
    grid_spec = pltpu.PrefetchScalarGridSpec(
        num_scalar_prefetch=2,
        grid=(n_blk,),
        in_specs=[pl.BlockSpec((MOE_BLOCK, D // 2), lambda i, be, nv: (i, 0)), pl.BlockSpec((1, 1, D, DE), wmap),
                  pl.BlockSpec((1, 1, D, DE), wmap), pl.BlockSpec((1, 1, DE, D), wmap)],
        out_specs=pl.BlockSpec((MOE_BLOCK, D // 2), lambda i, be, nv: (i, 0)),
        scratch_shapes=[pltpu.VMEM((D, DE), BF16), pltpu.VMEM((D, DE), BF16), pltpu.VMEM((DE, D), BF16)],
    )
    return pl.pallas_call(
        _moe_dense_kernel,
        grid_spec=grid_spec,
        out_shape=jax.ShapeDtypeStruct((P, D // 2), F32),
        compiler_params=_params("arbitrary"),
        name="moe_experts",
    )(blk_expert, blk_valid, xs, wg, wu, wd)


def _final_kernel(x_ref, y0_ref, y1_ref, w_ref, mod_ref, o_ref):
    o_ref[...] = _moe_residual(x_ref, y0_ref, y1_ref, w_ref, mod_ref[pl.program_id(0)])


def _final(x2, ybuf, wts, mod, B, T, tm=512):
    N, D = x2.shape
    nt = T // tm
    row = lambda b, t: (b * nt + t, 0)
    return pl.pallas_call(
        _final_kernel,
        grid=(B, nt),
        in_specs=[pl.BlockSpec((tm, D), row), pl.BlockSpec((tm, D // 2), row),
                  pl.BlockSpec((tm, D // 2), lambda b, t: (N // tm + b * nt + t, 0)),
                  pl.BlockSpec((tm, TOP_K), row), pl.BlockSpec((B, 6, D), lambda b, t: (0, 0, 0))],
        out_specs=pl.BlockSpec((tm, D), row),
        out_shape=jax.ShapeDtypeStruct((N, D), F32),
        compiler_params=_params("arbitrary", "arbitrary"),
        name="moe_combine",
    )(x2, ybuf, ybuf, wts, mod)


def _overlap_t(n_sel, n_cmp_pad):
    ci = jnp.arange(n_cmp_pad)[None, :] * CMP_STRIDE
    sj = jnp.arange(n_sel)[:, None] * SEL_BLOCK
    ov = (ci <= sj + SEL_BLOCK - 1) & (ci + CMP_BLOCK - 1 >= sj) & (jnp.arange(n_cmp_pad)[None, :] < n_cmp_pad - 1)
    return ov.astype(BF16)


def kernel(x, c, w_ada, b_ada, norm_g, w_in, b_in, rwkv_mu, rwkv_w0, rwkv_w2, rwkv_a0, rwkv_a2, rwkv_g2,
           rwkv_k_k, rwkv_k_a, rwkv_r_k, rwkv_gn_g, rwkv_gn_b, qk_norm_g, cmp_pos, cmp_w1, cmp_w2,
           w_up_rwkv, w_up_nsa, w_out, router_w, router_b, exp_w_gate, exp_w_up, exp_w_down):
    B, T, D = x.shape
    L = w_ada.shape[0]
    N = B * T
    mods = _ada(c, w_ada, b_ada)
    tables = _rope_tables(jnp.arange(T, dtype=jnp.int32))
    nch = T // CMP_STRIDE
    tables_cmp = _rope_tables(jnp.arange(nch, dtype=jnp.int32) * CMP_STRIDE + CMP_BLOCK - 1)
    ov_t = _overlap_t(T // SEL_BLOCK, nch)
    n_gate = NSA_GATE_COLS
    x2 = x.reshape(N, D)
    pending_moe = None
    for l in range(L):
        g0 = _SEG_KV[1] + n_gate
        w_pad = jnp.concatenate([w_in[l][:, :g0], jnp.zeros((D, GATE_PAD - n_gate), F32), w_in[l][:, g0:]],
                                axis=1).astype(BF16)
        b_pad = jnp.concatenate([b_in[l][:g0], jnp.zeros((GATE_PAD - n_gate,), F32), b_in[l][g0:]]).reshape(1, -1)
        outs = _inproj(x2, pending_moe, mods[l], norm_g[l, 0].reshape(1, D), w_pad, b_pad, B, T)
        if pending_moe is not None:
            x2, outs = outs[0], outs[1:]
        p_rw, p_q, p_kv, p_gate, p_merge = outs
        r, k, v, al, bb, ld, g, bonus = _rwkv_pre(p_rw, rwkv_mu[l], rwkv_w0[l], rwkv_w2[l], rwkv_a0[l],
                                                  rwkv_a2[l], rwkv_g2[l], rwkv_k_k[l], rwkv_k_a[l],
                                                  rwkv_r_k[l], B, T)
        ys = _rwkv_scan(r, k, v, al, bb, ld, B, T)
        qt, ks, kw, vst, vsd, vwt = _nsa_prep(p_q, p_kv, tables, qk_norm_g[l], B, T)
        kv3 = p_kv.reshape(B, T, KV_COLS)
        kcmp = _nsa_cmp(kv3, 0, cmp_pos[l], cmp_w1[l], cmp_w2[l], qk_norm_g[l, 1], tables_cmp)
        vct = _nsa_cmp(kv3, 1, cmp_pos[l], cmp_w1[l], cmp_w2[l], None, None)
        gt = p_gate[:, :n_gate].reshape(B, T, NSA_KV_HEADS, NSA_GROUP, 3).transpose(0, 2, 4, 3, 1)
        yb = _nsa_attn(qt, kcmp, vct, ks.reshape(B, T, 2 * LANES), vst, vsd, kw.reshape(B, T, LANES), vwt, gt, ov_t,
                       B, T)
        x2, h2, route, totals = _merge(ys, g, bonus, rwkv_gn_g[l], rwkv_gn_b[l], yb, p_merge, x2, mods[l],
                               norm_g[l, 1].reshape(1, D), w_up_rwkv[l].astype(BF16),
                               w_up_nsa[l].astype(BF16), w_out[l].astype(BF16), router_w, router_b, B, T)
        wts, dest, blk_expert, blk_valid, n_blk = _route(route, totals, N)
        xs = _sc_dispatch(h2, dest, n_blk * MOE_BLOCK)
        ys = _moe_dense(xs, blk_expert, blk_valid, n_blk, l, exp_w_gate, exp_w_up, exp_w_down)
        ybuf = _sc_collect(ys, dest)
        pending_moe = (ybuf, wts, mods[l])
    return _final(x2, *pending_moe, B, T).reshape(B, T, D)
```

```python
import functools
import math

import jax
import jax.numpy as jnp
from jax import lax
from jax.experimental import pallas as pl
from jax.experimental.pallas import tpu as pltpu
from jax.experimental.pallas import tpu_sc as plsc

F32 = jnp.float32
BF16 = jnp.bfloat16
HI = lax.Precision.HIGHEST

D_MODEL = 1024
RWKV_HEADS = 8
HEAD_DIM = 64
RWKV_WIDTH = RWKV_HEADS * HEAD_DIM
DECAY_LORA = 64
ICLR_LORA = 64
GATE_LORA = 128
RWKV_GN_EPS = 64e-5
RWKV_COLS = 3 * RWKV_WIDTH + DECAY_LORA + ICLR_LORA + GATE_LORA

NSA_Q_HEADS = 8
NSA_KV_HEADS = 2
NSA_GROUP = NSA_Q_HEADS // NSA_KV_HEADS
NSA_WIDTH = NSA_Q_HEADS * HEAD_DIM
CMP_STRIDE = 16
CMP_BLOCK = 2 * CMP_STRIDE
CMP_HIDDEN = 256
SEL_BLOCK = 64
SEL_SHIFT = 6
SEL_TOPK = 16
WINDOW = 512
FORCE_SCORE = 1e4
NEG_INF = -1e30
ROPE_THETA = 500000.0
ROPE_DIM = HEAD_DIM // 4
KV_COLS = 6 * NSA_KV_HEADS * HEAD_DIM
NSA_GATE_COLS = 3 * NSA_Q_HEADS
GATE_PAD = 128

N_EXPERTS = 16
N_GROUPS = 4
EXPERTS_PER_GROUP = N_EXPERTS // N_GROUPS
TOP_K = 2
D_EXPERT = 512
MOE_BLOCK = 512
NORM_EPS = 1e-6

LANES = 128
CHUNK = 64
KEY_TILE = 128
SEL_TILE = 512
CMP_VARIANTS = 8
V_ROWS = 80
Q_SCALE = HEAD_DIM ** -0.5 * math.log2(math.e)
Q_TILE = 128
F32_TINY = float(jnp.finfo(jnp.float32).tiny)

_SEG_RW = (0, RWKV_COLS)
_SEG_Q = (_SEG_RW[1], _SEG_RW[1] + NSA_WIDTH)
_SEG_KV = (_SEG_Q[1], _SEG_Q[1] + KV_COLS)
_SEG_GATE = (_SEG_KV[1], _SEG_KV[1] + GATE_PAD)
_SEG_MERGE = (_SEG_GATE[1], _SEG_GATE[1] + 2 * D_MODEL)
IN_COLS_PAD = _SEG_MERGE[1]

_VMEM_LIMIT = 56 * 1024 * 1024


def _dot(a, b, precision=None):
    return jnp.dot(a, b, preferred_element_type=F32, precision=precision)


def _dot_tb(a, b, precision=None):
    return lax.dot_general(a, b, (((1,), (1,)), ((), ())), preferred_element_type=F32,
                           precision=precision)


def _split_bf16(x, terms):
    parts = []
    for _ in range(terms - 1):
        parts.append(x.astype(BF16))
        x = x - parts[-1].astype(F32)
    parts.append(x.astype(BF16))
    return parts


def _dot_split_lhs(x, w_bf, terms=2):
    return functools.reduce(jnp.add, [_dot(p, w_bf) for p in _split_bf16(x, terms)])


def _dot_split_rhs(w_bf, x, terms=2):
    return functools.reduce(jnp.add, [_dot(w_bf, p) for p in _split_bf16(x, terms)])


def _dot_3pass(x, w_hl_ref):
    x_hi, x_lo = _split_bf16(x, 2)
    w_hi = w_hl_ref[0]
    return _dot(x_hi, w_hi) + _dot(x_lo, w_hi) + _dot(x_hi, w_hl_ref[1])


def _pack_bf16_pairs(x):
    n = x.shape[1] // 2
    bits = lax.bitcast_convert_type(x.astype(BF16).astype(F32), jnp.uint32)
    return lax.bitcast_convert_type(bits[:, 0:n] | (bits[:, n:] >> 16), F32)


def _unpack_bf16_pairs(packed):
    bits = lax.bitcast_convert_type(packed, jnp.uint32)
    lo = lax.bitcast_convert_type(bits & jnp.uint32(0xFFFF0000), F32)
    hi = lax.bitcast_convert_type(bits << 16, F32)
    return lo.astype(BF16), hi.astype(BF16)


def _hi_lo(w):
    hi = w.astype(BF16)
    return jnp.stack([hi, (w - hi.astype(F32)).astype(BF16)])


def _params(*sem):
    return pltpu.CompilerParams(dimension_semantics=sem, vmem_limit_bytes=_VMEM_LIMIT)


def _sigmoid(x):
    return 1.0 / (1.0 + jnp.exp(-x))


def _ada_kernel(c_ref, w_ref, b_ref, o_ref):
    c = c_ref[...]
    s = c * _sigmoid(c)
    o_ref[0] = _dot(s, w_ref[0], HI) + b_ref[0]


def _ada(c, w_ada, b_ada):
    L, D, D6 = w_ada.shape
    B = c.shape[0]
    rows = 8
    cp = jnp.zeros((rows, D), F32).at[:B].set(c)
    tn = 1536
    out = pl.pallas_call(
        _ada_kernel,
        grid=(L, D6 // tn),
        in_specs=[pl.BlockSpec((rows, D), lambda l, j: (0, 0)),
                  pl.BlockSpec((1, D, tn), lambda l, j: (l, 0, j)),
                  pl.BlockSpec((1, 1, tn), lambda l, j: (l, 0, j))],
        out_specs=pl.BlockSpec((1, rows, tn), lambda l, j: (l, 0, j)),
        out_shape=jax.ShapeDtypeStruct((L, rows, D6), F32),
        compiler_params=_params("arbitrary", "arbitrary"),
        name="ada_mod",
    )(cp, w_ada, b_ada.reshape(L, 1, D6))
    return out[:, :B].reshape(L, B, 6, D)


def _moe_residual(x_ref, y0_ref, y1_ref, w_ref, m_prev):
    w = w_ref[...]
    rows = lambda y_ref: jnp.concatenate([h.astype(F32) for h in _unpack_bf16_pairs(y_ref[...])], axis=1)
    return x_ref[...] + m_prev[5:6] * (w[:, 0:1] * rows(y0_ref) + w[:, 1:2] * rows(y1_ref))


def _inproj_kernel(*refs, after_moe):
    if after_moe:
        x_ref, y0_ref, y1_ref, wts_ref, modp_ref, mod_ref, g_ref, w_ref, b_ref, o_x = refs[:10]
        x = _moe_residual(x_ref, y0_ref, y1_ref, wts_ref, modp_ref[pl.program_id(0)])
        o_x[...] = x
    else:
        x_ref, mod_ref, g_ref, w_ref, b_ref = refs[:5]
        x = x_ref[...]
    o_rw, o_q, o_kv, o_gate, o_merge = refs[-5:]
    m = mod_ref[pl.program_id(0)]
    ms = jnp.mean(x * x, axis=-1, keepdims=True)
    h = x * lax.rsqrt(ms + NORM_EPS) * g_ref[...]
    h = h * (1.0 + m[1:2]) + m[0:1]
    hb = h.astype(BF16)
    for o, (a, e) in ((o_rw, _SEG_RW), (o_q, _SEG_Q), (o_kv, _SEG_KV), (o_gate, _SEG_GATE),
                      (o_merge, _SEG_MERGE)):
        o[...] = _dot(hb, w_ref[:, a:e]) + b_ref[:, a:e]


def _inproj(x2, pending_moe, mod, g, w_pad, b_pad, B, T, tm=512):
    N, D = x2.shape
    nt = T // tm
    row = lambda b, t: (b * nt + t, 0)
    mods_spec = pl.BlockSpec((B, 6, D), lambda b, t: (0, 0, 0))
    widths = [e - a for a, e in (_SEG_RW, _SEG_Q, _SEG_KV, _SEG_GATE, _SEG_MERGE)]
    in_specs = [pl.BlockSpec((tm, D), row)]
    args = [x2]
    if pending_moe is not None:
        ybuf, wts, mod_prev = pending_moe
        in_specs += [pl.BlockSpec((tm, D // 2), row),
                     pl.BlockSpec((tm, D // 2), lambda b, t: (N // tm + b * nt + t, 0)),
                     pl.BlockSpec((tm, TOP_K), row), mods_spec]
        args += [ybuf, ybuf, wts, mod_prev]
        widths = [D] + widths
    in_specs += [mods_spec, pl.BlockSpec((1, D), lambda b, t: (0, 0)),
                 pl.BlockSpec((D, IN_COLS_PAD), lambda b, t: (0, 0)),
                 pl.BlockSpec((1, IN_COLS_PAD), lambda b, t: (0, 0))]
    return pl.pallas_call(
        functools.partial(_inproj_kernel, after_moe=pending_moe is not None),
        grid=(B, nt),
        in_specs=in_specs,
        out_specs=[pl.BlockSpec((tm, w), row) for w in widths],
        out_shape=[jax.ShapeDtypeStruct((N, w), F32) for w in widths],
        compiler_params=_params("arbitrary", "arbitrary"),
        name="in_proj",
    )(*args, mod, g, w_pad, b_pad)


def _rwkv_pre_kernel(p_ref, mu_ref, w0_ref, w2_ref, a0_ref, a2_ref, g2_ref, kk_ref, ka_ref, rk_ref,
                     bd_ref, o_r, o_k, o_v, o_al, o_b, o_ld, o_g, o_bonus, carry_ref):
    W = RWKV_WIDTH

    @pl.when(pl.program_id(1) == 0)
    def _():
        carry_ref[...] = jnp.zeros_like(carry_ref)

    p = p_ref[...]
    ts = p.shape[0]
    rows = lax.broadcasted_iota(jnp.int32, p.shape, 0)
    shifted = jnp.where(rows == 0, carry_ref[0:1, :], pltpu.roll(p, 1, 0))
    carry_ref[0:1, :] = p[ts - 1:ts, :]
    pm = p + (shifted - p) * mu_ref[...]
    r = pm[:, 0:W]
    k = pm[:, W:2 * W]
    v = pm[:, 2 * W:3 * W]
    wa = pm[:, 3 * W:3 * W + DECAY_LORA + ICLR_LORA]
    gl = pm[:, 3 * W + DECAY_LORA + ICLR_LORA:]
    xw = w0_ref[...] + _dot_3pass(jnp.tanh(wa), w2_ref)
    ld = -math.exp(-0.5) * _sigmoid(xw)
    a = _sigmoid(a0_ref[...] + _dot_3pass(wa, a2_ref))
    g = _dot_3pass(_sigmoid(gl), g2_ref)
    bd = bd_ref[...]
    kk = k * kk_ref[...]
    nrm = jnp.sqrt(_head_sums(kk * kk, bd))
    kk = kk / jnp.maximum(nrm, 1e-12)
    k2 = k * (1.0 + (a - 1.0) * ka_ref[...])
    bonus = _head_sums(r * k2 * rk_ref[...], bd) * v
    o_r[...] = r
    o_k[...] = k2
    o_v[...] = v
    o_al[...] = kk
    o_b[...] = -kk * a
    o_ld[...] = ld
    o_g[...] = g
    o_bonus[...] = bonus


HEAD_SUM_WIDTH = 256


def _head_sums(x, bd):
    w = bd.shape[0]
    parts = [_dot_split_lhs(x[:, j:j + w], bd) for j in range(0, x.shape[1], w)]
    return parts[0] if len(parts) == 1 else jnp.concatenate(parts, axis=1)


def _head_block_diag(width, scale=1.0):
    i = jnp.arange(width) // HEAD_DIM
    return ((i[:, None] == i[None, :]).astype(F32) * scale).astype(BF16)


def _rwkv_pre(p_rw, mu, w0, w2, a0, a2, g2, k_k, k_a, r_k, B, T, ts=512):
    N = p_rw.shape[0]
    W = RWKV_WIDTH
    nt = T // ts
    row = lambda b, t: (b * nt + t, 0)
    zl = jnp.zeros((DECAY_LORA, W), F32)
    w2p = jnp.concatenate([w2, zl], axis=0)
    a2p = jnp.concatenate([zl, a2], axis=0)
    full = lambda shape: pl.BlockSpec(shape, lambda b, t: (0,) * len(shape))
    vec = lambda z: z.reshape(1, -1)
    return pl.pallas_call(
        _rwkv_pre_kernel,
        grid=(B, nt),
        in_specs=[pl.BlockSpec((ts, RWKV_COLS), row), full((1, RWKV_COLS)), full((1, W)),
                  full((2, 2 * DECAY_LORA, W)), full((1, W)), full((2, 2 * DECAY_LORA, W)),
                  full((2, GATE_LORA, W)), full((1, W)), full((1, W)), full((1, W)),
                  full((HEAD_SUM_WIDTH, HEAD_SUM_WIDTH))],
        out_specs=[pl.BlockSpec((ts, W), row)] * 8,
        out_shape=[jax.ShapeDtypeStruct((N, W), F32)] * 8,
        scratch_shapes=[pltpu.VMEM((8, RWKV_COLS), F32)],
        compiler_params=_params("arbitrary", "arbitrary"),
        name="rwkv_pre",
    )(p_rw, vec(mu), vec(w0), _hi_lo(w2p), vec(a0), _hi_lo(a2p), _hi_lo(g2), vec(k_k), vec(k_a), vec(r_k),
      _head_block_diag(HEAD_SUM_WIDTH))


def _bf(x):
    return x.astype(BF16)


def _scan_local(chunks, eye, strict, incl, m0, m1, between_stages=lambda: None):
    C = CHUNK
    n = range(len(chunks))
    st = lambda z: jnp.concatenate([z * m0, z * m1], axis=0)
    zero = jnp.zeros((2 * C, 2 * C), F32)
    at_b, rt_s, vs, vs_b, lhs_a, rhs_a, bk_t, dcol = [], [], [], [], [], [], [], []
    for r, k, v, al, bb, ld, cum in chunks:
        tot = cum[C - 1:C, :]
        dinv = jnp.exp(-cum)
        dend = jnp.exp(tot - cum)
        at_b.append(_bf(st(al * jnp.exp(cum - ld))))
        rt_s.append(st(r * jnp.exp(cum)))
        vs.append(st(v))
        vs_b.append(_bf(vs[-1]))
        lhs_a.append(jnp.concatenate([at_b[-1], _bf(rt_s[-1])], axis=0))
        rhs_a.append(_bf(jnp.concatenate([st(bb * dinv), st(k * dinv)], axis=0)))
        bk_t.append(_bf(jnp.concatenate([st(bb * dend).T, st(k * dend).T], axis=1)))
        dcol.append(jnp.sum(eye * jnp.exp(tot), axis=1, keepdims=True))
    between_stages()
    A = [_dot_tb(lhs_a[i], rhs_a[i]) for i in n]
    between_stages()
    a_ab = [jnp.where(strict, A[i][0:2 * C, 0:2 * C], zero) for i in n]
    a_ak = [_bf(jnp.where(strict, A[i][0:2 * C, 2 * C:4 * C], zero)) for i in n]
    a_r = [_bf(jnp.concatenate([jnp.where(incl, A[i][2 * C:4 * C, 0:2 * C], zero),
                                jnp.where(incl, A[i][2 * C:4 * C, 2 * C:4 * C], zero)], axis=1)) for i in n]
    akv = [_bf(_dot(a_ak[i], vs_b[i])) for i in n]
    between_stages()
    tinv = [eye + a_ab[i] for i in n]
    pw_b = [_bf(a_ab[i]) for i in n]
    pw_b = [_bf(_dot(pw_b[i], pw_b[i])) for i in n]
    between_stages()
    for step in range(5):
        rhs = [jnp.concatenate([pw_b[i], _bf(tinv[i])], axis=1) for i in n]
        if step == 4:
            rhs = [_bf(tinv[i]) for i in n]
        prod = [_dot(pw_b[i], rhs[i]) for i in n]
        tinv = [tinv[i] + prod[i][:, -2 * C:] for i in n]
        pw_b = [_bf(prod[i][:, 0:2 * C]) for i in n]
        between_stages()
    X = [_dot(_bf(tinv[i]), jnp.concatenate([at_b[i], akv[i]], axis=1)) for i in n]
    between_stages()
    w_b = [_bf(X[i][:, 0:LANES]) for i in n]
    uv0 = [jnp.concatenate([_bf(X[i][:, LANES:2 * LANES]), vs_b[i]], axis=0) for i in n]
    m_h = [_bf(_dot(bk_t[i][:, 0:2 * C], w_b[i])) for i in n]
    g_h = [_dot(bk_t[i], uv0[i]) for i in n]
    between_stages()
    q_h = [_bf(rt_s[i] + _dot(a_r[i][:, 0:2 * C], w_b[i])) for i in n]
    y0 = [_dot(a_r[i], uv0[i]) for i in n]
    return [(m_h[i], g_h[i], dcol[i], q_h[i], y0[i]) for i in n]


def _rwkv_scan_kernel(r_ref, k_ref, v_ref, al_ref, b_ref, ld_ref, o_ref, h_ref, *local_refs):
    C = CHUNK
    tc = r_ref.shape[0]
    nc = tc // C

    @pl.when(pl.program_id(2) == 0)
    def _():
        h_ref[...] = jnp.zeros_like(h_ref)
        for ref in local_refs:
            ref[...] = jnp.zeros_like(ref)

    seq = {"H": h_ref[...], "c": 0}

    def one_step():
        c = seq["c"]
        if c < nc:
            m_h, g_h, dcol, q_h, y0 = (ref[c] for ref in local_refs)
            h_b = _bf(seq["H"])
            Y = _dot(q_h, h_b) + y0
            o_ref[c * C:(c + 1) * C, :] = Y[0:C] + Y[C:2 * C]
            seq["H"] = dcol * seq["H"] + _dot(m_h, h_b) + g_h
            seq["c"] = c + 1

    tri = jnp.where(lax.broadcasted_iota(jnp.int32, (C, C), 1) <= lax.broadcasted_iota(jnp.int32, (C, C), 0),
                    1.0, 0.0).astype(BF16)
    r2 = lax.broadcasted_iota(jnp.int32, (2 * C, 2 * C), 0)
    c2 = lax.broadcasted_iota(jnp.int32, (2 * C, 2 * C), 1)
    eye = (r2 == c2).astype(F32)
    strict = (c2 & (C - 1)) < (r2 & (C - 1))
    incl = (c2 & (C - 1)) <= (r2 & (C - 1))
    lane = lax.broadcasted_iota(jnp.int32, (C, LANES), 1)
    m0 = (lane < HEAD_DIM).astype(F32)
    m1 = 1.0 - m0
    cum = _dot_split_rhs(tri, jnp.concatenate([ld_ref[c * C:(c + 1) * C, :] for c in range(nc)], axis=1), 3)
    chunks = []
    for c in range(nc):
        sl = slice(c * C, (c + 1) * C)
        chunks.append((r_ref[sl, :], k_ref[sl, :], v_ref[sl, :], al_ref[sl, :], b_ref[sl, :], ld_ref[sl, :],
                       cum[:, c * LANES:(c + 1) * LANES]))
    local = _scan_local(chunks, eye, strict, incl, m0, m1, between_stages=one_step)
    while seq["c"] < nc:
        one_step()
    h_ref[...] = seq["H"]
    for c, parts in enumerate(local):
        for ref, part in zip(local_refs, parts):
            ref[c] = part


def _rwkv_scan(r, k, v, al, bb, ld, B, T, tc=512):
    N, W = r.shape
    nt = T // tc
    nc = tc // CHUNK
    in_spec = pl.BlockSpec((tc, LANES), lambda b, h, t: (b * nt + jnp.minimum(t, nt - 1), h))
    out_spec = pl.BlockSpec((tc, LANES), lambda b, h, t: (b * nt + jnp.maximum(t - 1, 0), h))
    sq = (nc, LANES, LANES)
    return pl.pallas_call(
        _rwkv_scan_kernel,
        grid=(B, W // LANES, nt + 1),
        in_specs=[in_spec] * 6,
        out_specs=out_spec,
        out_shape=jax.ShapeDtypeStruct((N, W), F32),
        scratch_shapes=[pltpu.VMEM((LANES, LANES), F32), pltpu.VMEM(sq, BF16), pltpu.VMEM(sq, F32),
                        pltpu.VMEM((nc, LANES, 1), F32), pltpu.VMEM(sq, BF16), pltpu.VMEM(sq, F32)],
        compiler_params=_params("arbitrary", "arbitrary", "arbitrary"),
        name="rwkv_scan",
    )(r, k, v, al, bb, ld)


def _rope_tables(pos):
    half = ROPE_DIM // 2
    inv = jnp.power(ROPE_THETA, -jnp.arange(half, dtype=F32) * 2.0 / ROPE_DIM)
    ang = pos.astype(F32)[:, None] * inv[None, :]
    cos, sin = jnp.cos(ang), jnp.sin(ang)
    n = pos.shape[0]
    rest = HEAD_DIM - ROPE_DIM
    c = jnp.concatenate([cos, cos, jnp.ones((n, rest), F32)], axis=1)
    s_dn = jnp.concatenate([-sin, jnp.zeros((n, half + rest), F32)], axis=1)
    s_up = jnp.concatenate([jnp.zeros((n, half), F32), sin, jnp.zeros((n, rest), F32)], axis=1)
    rep = LANES // HEAD_DIM
    return jnp.tile(c, (1, rep)), jnp.tile(s_dn, (1, rep)), jnp.tile(s_up, (1, rep))


def _norm_rope(x, bd, g, c, s_dn, s_up):
    width = x.shape[1]
    half = ROPE_DIM // 2
    rep = width // LANES
    tile = (lambda z: jnp.concatenate([z] * rep, axis=1)) if rep > 1 else (lambda z: z)
    ms = _head_sums(x * x, bd)
    xn = x * lax.rsqrt(ms + NORM_EPS) * g
    return (xn * tile(c) + pltpu.roll(xn, width - half, 1) * tile(s_dn)
            + pltpu.roll(xn, half, 1) * tile(s_up))


def _nsa_prep_kernel(q_ref, kv_ref, c_ref, sd_ref, su_ref, gq_ref, gs_ref, gw_ref, bdq_ref, bdk_ref,
                     o_qt, o_ks, o_kw, o_vst, o_vsd, o_vwt):
    c, sd, su = c_ref[...], sd_ref[...], su_ref[...]
    q = _norm_rope(q_ref[...], bdq_ref[...], gq_ref[...], c, sd, su) * Q_SCALE
    qt = q.T
    ts = q.shape[0]
    kv = kv_ref[...]
    bdk = bdk_ref[...]
    pos = pl.program_id(1) * ts + lax.broadcasted_iota(jnp.int32, (ts, LANES), 0)
    blk_onehot = jnp.where((pos >> SEL_SHIFT) == lax.broadcasted_iota(jnp.int32, (ts, LANES), 1), 1.0, 0.0)
    ks = _norm_rope(kv[:, 2 * LANES:3 * LANES], bdk, gs_ref[...], c, sd, su)
    o_ks[...] = jnp.concatenate([ks, blk_onehot], axis=1).astype(BF16)
    o_kw[...] = _norm_rope(kv[:, 4 * LANES:5 * LANES], bdk, gw_ref[...], c, sd, su).astype(BF16)
    ones_rows = jnp.where(lax.broadcasted_iota(jnp.int32, (V_ROWS - HEAD_DIM, q.shape[0]), 0) == 0, 1.0, 0.0)

    def values_t(x):
        xt = x.T
        return jnp.concatenate([xt[0:HEAD_DIM], ones_rows, xt[HEAD_DIM:2 * HEAD_DIM], ones_rows], axis=0)

    vst = values_t(kv[:, 3 * LANES:4 * LANES])
    vwt = values_t(kv[:, 5 * LANES:6 * LANES])
    for j in range(q.shape[0] // KEY_TILE):
        sl = slice(j * KEY_TILE, (j + 1) * KEY_TILE)
        o_qt[0, j] = qt[:, sl].astype(BF16)
        o_vsd[0, j] = vst[:, sl].astype(BF16)
        o_vwt[0, j] = vwt[:, sl].astype(BF16)
    for j in range(q.shape[0] // SEL_TILE):
        o_vst[0, j] = vst[:, j * SEL_TILE:(j + 1) * SEL_TILE].astype(BF16)


def _nsa_prep(q, kv, tables, qk_g, B, T, ts=512):
    N = q.shape[0]
    nt = T // ts
    nk = ts // KEY_TILE
    ns = ts // SEL_TILE
    row = lambda b, t: (b * nt + t, 0)
    full = lambda shape: pl.BlockSpec(shape, lambda b, t: (0,) * len(shape))
    tab = pl.BlockSpec((ts, LANES), lambda b, t: (t, 0))
    gq = jnp.tile(qk_g[0], NSA_Q_HEADS).reshape(1, NSA_WIDTH)
    gs = jnp.tile(qk_g[2], NSA_KV_HEADS).reshape(1, LANES)
    gw = jnp.tile(qk_g[3], NSA_KV_HEADS).reshape(1, LANES)
    tiled = lambda rows: pl.BlockSpec((1, nk, rows, KEY_TILE), lambda b, t: (b, t, 0, 0))
    return pl.pallas_call(
        _nsa_prep_kernel,
        grid=(B, nt),
        in_specs=[pl.BlockSpec((ts, NSA_WIDTH), row), pl.BlockSpec((ts, KV_COLS), row), tab, tab, tab,
                  full((1, NSA_WIDTH)), full((1, LANES)), full((1, LANES)),
                  full((HEAD_SUM_WIDTH, HEAD_SUM_WIDTH)), full((LANES, LANES))],
        out_specs=[tiled(NSA_WIDTH), pl.BlockSpec((ts, 2 * LANES), row), pl.BlockSpec((ts, LANES), row),
                   pl.BlockSpec((1, ns, NSA_KV_HEADS * V_ROWS, SEL_TILE), lambda b, t: (b, t, 0, 0)),
                   tiled(NSA_KV_HEADS * V_ROWS), tiled(NSA_KV_HEADS * V_ROWS)],
        out_shape=[jax.ShapeDtypeStruct((B, T // KEY_TILE, NSA_WIDTH, KEY_TILE), BF16),
                   jax.ShapeDtypeStruct((N, 2 * LANES), BF16), jax.ShapeDtypeStruct((N, LANES), BF16),
                   jax.ShapeDtypeStruct((B, T // SEL_TILE, NSA_KV_HEADS * V_ROWS, SEL_TILE), BF16),
                   jax.ShapeDtypeStruct((B, T // KEY_TILE, NSA_KV_HEADS * V_ROWS, KEY_TILE), BF16),
                   jax.ShapeDtypeStruct((B, T // KEY_TILE, NSA_KV_HEADS * V_ROWS, KEY_TILE), BF16)],
        compiler_params=_params("arbitrary", "arbitrary"),
        name="nsa_prep",
    )(q, kv, *tables, gq, gs, gw, _head_block_diag(HEAD_SUM_WIDTH, 1.0 / HEAD_DIM),
      _head_block_diag(LANES, 1.0 / HEAD_DIM))


def _gelu_tanh(x):
    return 0.5 * x * (1.0 + jnp.tanh(0.7978845608028654 * (x + 0.044715 * x * x * x)))


def _nsa_cmp_kernel(x_ref, pos_ref, w1_ref, w2_ref, *rest, is_key):
    if is_key:
        g_ref, c_ref, sd_ref, su_ref, bd_ref, o_ref, xs_ref = rest
    else:
        o_ref, xs_ref = rest
    nch = xs_ref.shape[0]
    S = CMP_STRIDE
    for j in range(S):
        xs_ref[:, j * LANES:(j + 1) * LANES] = x_ref[0, pl.ds(j, nch, stride=S), :]
    xs = xs_ref[...]
    first = _dot((xs + pos_ref[0:1, :]).astype(BF16), w1_ref[0])
    second = _dot((xs + pos_ref[1:2, :]).astype(BF16), w1_ref[1])
    hid = first + pltpu.roll(second, nch - 1, 0)
    out = _dot(_gelu_tanh(hid).astype(BF16), w2_ref[...])
    rows = lax.broadcasted_iota(jnp.int32, out.shape, 0)
    if is_key:
        out = _norm_rope(out, bd_ref[...], g_ref[...], c_ref[...], sd_ref[...], su_ref[...])
        o_ref[0] = jnp.where(rows < nch - 1, out, 0.0).astype(BF16)
    else:
        o_ref[0] = jnp.where(rows < nch - 1, out, 0.0).T.astype(BF16)


def _nsa_cmp(kv3, which, cmp_pos, cmp_w1, cmp_w2, g_k, tables_cmp):
    B, T, _ = kv3.shape
    S = CMP_STRIDE
    nch = T // S
    is_key = which == 0
    eye2 = jnp.eye(NSA_KV_HEADS, dtype=F32)
    w1 = cmp_w1[which].reshape(CMP_BLOCK, HEAD_DIM, CMP_HIDDEN)
    w1 = jnp.einsum('jdh,ge->jgdeh', w1, eye2).reshape(2, S * LANES, NSA_KV_HEADS * CMP_HIDDEN)
    w2 = jnp.einsum('hd,ge->ghed', cmp_w2[which], eye2).reshape(NSA_KV_HEADS * CMP_HIDDEN, LANES)
    pos = jnp.tile(cmp_pos[which].reshape(2, S, 1, HEAD_DIM), (1, 1, NSA_KV_HEADS, 1)).reshape(2, S * LANES)
    full = lambda shape: pl.BlockSpec(shape, lambda b: (0,) * len(shape))
    in_specs = [pl.BlockSpec((1, T, LANES), lambda b: (b, 0, which)), full(pos.shape), full(w1.shape),
                full(w2.shape)]
    args = [kv3, pos, w1.astype(BF16), w2.astype(BF16)]
    if is_key:
        in_specs += [full((1, LANES)), full((nch, LANES)), full((nch, LANES)), full((nch, LANES)),
                     full((LANES, LANES))]
        args += [jnp.tile(g_k, NSA_KV_HEADS).reshape(1, LANES), *tables_cmp,
                 _head_block_diag(LANES, 1.0 / HEAD_DIM)]
        out_spec = pl.BlockSpec((1, nch, LANES), lambda b: (b, 0, 0))
        out_shape = jax.ShapeDtypeStruct((B, nch, LANES), BF16)
    else:
        out_spec = pl.BlockSpec((1, LANES, nch), lambda b: (b, 0, 0))
        out_shape = jax.ShapeDtypeStruct((B, LANES, nch), BF16)
    return pl.pallas_call(
        functools.partial(_nsa_cmp_kernel, is_key=is_key),
        grid=(B,),
        in_specs=in_specs,
        out_specs=out_spec,
        out_shape=out_shape,
        scratch_shapes=[pltpu.VMEM((nch, S * LANES), F32)],
        compiler_params=_params("arbitrary"),
        name="nsa_cmp_k" if is_key else "nsa_cmp_v",
    )(*args)


def _nsa_attn_kernel(qt_ref, kc_ref, vct_ref, ks_ref, vst_ref, vsd_ref, kw_ref, vwt_ref, gt_ref, ov_ref, o_ref,
                     rhs_ref, oc_ref, keep_ref, s0_ref, s1_ref, s2_ref, s3_ref, p0_ref, p1_ref):
    qb = pl.program_id(1)
    G = NSA_KV_HEADS
    R = NSA_GROUP
    QT = Q_TILE
    KT = KEY_TILE
    CG = R * QT
    NQ = G * CG
    D = HEAD_DIM
    t0 = qb * QT
    n_cmp_pad = kc_ref.shape[1]
    n_sel = ov_ref.shape[0]
    cols = lambda g: slice(g * CG, (g + 1) * CG)

    q_cols = []
    for g in range(G):
        q_g = jnp.concatenate([qt_ref[0, 0, (g * R + r) * D:(g * R + r + 1) * D, :] for r in range(R)], axis=1)
        q_cols.append(jnp.concatenate([q_g if gg == g else jnp.zeros_like(q_g) for gg in range(G)], axis=0))
    qpad = jnp.concatenate(q_cols, axis=1)

    tq_row = t0 + (lax.broadcasted_iota(jnp.int32, (1, NQ), 1) & (QT - 1))
    spread = lambda z: jnp.concatenate([z[:, g * QT:(g + 1) * QT] for g in range(G) for _ in range(R)], axis=1)
    tile_all = lambda z: jnp.concatenate([z] * (G * R), axis=1)

    def values_dot(v_of_group, p):
        return jnp.concatenate([_dot(v_of_group(g), p[:, cols(g)]) for g in range(G)], axis=1)

    NV = CMP_VARIANTS
    nq = ks_ref.shape[1] // QT

    def compressed_and_select(n_c, n_b):
        sc = _dot(kc_ref[0, 0:n_c, :], qpad)
        n_i = lax.broadcasted_iota(jnp.int32, (n_c, 1), 0)
        cend = jnp.where(n_i < n_cmp_pad - 1, n_i * CMP_STRIDE + (CMP_BLOCK - 1), jnp.int32(2 ** 30))
        cvalid = cend <= tq_row
        sc = jnp.where(cvalid, sc, NEG_INF)
        mc = jnp.max(sc, axis=0, keepdims=True)
        ec = jnp.where(cvalid, jnp.exp2(sc - mc), 0.0)
        pc = ec / jnp.maximum(jnp.sum(ec, axis=0, keepdims=True), F32_TINY)
        pc_b = pc.astype(BF16)
        oc_ref[...] = values_dot(lambda g: vct_ref[0, g * D:(g + 1) * D, 0:n_c], pc_b)
        sums = []
        for g in range(G):
            acc = pc[:, g * CG:g * CG + QT]
            for r in range(1, R):
                acc = acc + pc[:, g * CG + r * QT:g * CG + (r + 1) * QT]
            sums.append(acc)
        imp = _dot_split_rhs(ov_ref[0:n_b, 0:n_c], jnp.concatenate(sums, axis=1))
        jb = lax.broadcasted_iota(jnp.int32, (n_b, G * QT), 0)
        jf = jb.astype(F32)
        tq_b = t0 + (lax.broadcasted_iota(jnp.int32, (n_b, G * QT), 1) & (QT - 1))
        cur = tq_b >> SEL_SHIFT
        forced = (jb == 0) | (jb == cur) | (jb == cur - 1)
        visible = jb * SEL_BLOCK <= tq_b
        score = jnp.where(visible, jnp.where(forced, FORCE_SCORE, imp), -1.0)
        sel = jnp.zeros((n_b, G * QT), F32)
        for _ in range(min(SEL_TOPK, n_b)):
            mx = jnp.max(score, axis=0, keepdims=True)
            jmin = jnp.min(jnp.where(score == mx, jf, 1e9), axis=0, keepdims=True)
            hit = jf == jmin
            sel = jnp.where(hit, 1.0, sel)
            score = jnp.where(hit, -3e38, score)
        keep_ref[0:n_b, :] = jnp.where(visible, sel, 0.0)
        if n_b < n_sel:
            keep_ref[n_b:n_sel, :] = jnp.zeros((n_sel - n_b, G * QT), F32)

    for v in range(NV):
        @pl.when((qb * NV) // nq == v)
        def _():
            compressed_and_select((v + 1) * n_cmp_pad // NV, (v + 1) * n_sel // NV)

    o_c = oc_ref[...]
    ji = lax.broadcasted_iota(jnp.int32, (n_sel, G * QT), 0)

    ST = SEL_TILE
    bias_all = (keep_ref[...] - 1.0) * (-NEG_INF)
    first_own = t0 // SEL_BLOCK
    vrows = lambda g: slice(g * V_ROWS, (g + 1) * V_ROWS)

    def with_bias_rows(bias):
        rows = spread(bias).astype(BF16)
        if n_sel < LANES:
            rows = jnp.concatenate([rows, jnp.zeros((LANES - n_sel, NQ), BF16)], axis=0)
        return jnp.concatenate([qpad, rows], axis=0)

    rhs_ref[...] = with_bias_rows(jnp.where(ji < first_own, bias_all, NEG_INF))
    n_tiles = (t0 + ST - 1) // ST
    last_tile = ks_ref.shape[1] // ST - 1
    p_bufs = (p0_ref, p1_ref)

    def sel_scores(kt, s_ref):
        k0 = pl.multiple_of(jnp.minimum(kt, last_tile) * ST, ST)
        s_ref[...] = _dot(ks_ref[0, pl.ds(k0, ST), :], rhs_ref[...])

    def sel_values(kt, slot, acc, alpha):
        kt = jnp.clip(kt, 0, last_tile)
        return acc * alpha + values_dot(lambda g: vst_ref[0, kt, vrows(g), :], p_bufs[slot][...])

    def sel_softmax(s_ref, slot, m):
        s = s_ref[...]
        m_new = jnp.maximum(m, jnp.max(s, axis=0, keepdims=True))
        p_bufs[slot][...] = jnp.exp2(s - m_new).astype(BF16)
        return m_new, jnp.exp2(m - m_new)

    def sel_pair(a, carry, s_now, s_next):
        m, acc, alpha0, alpha1 = carry
        acc = sel_values(a - 2, 0, acc, alpha0)
        acc = sel_values(a - 1, 1, acc, alpha1)
        sel_scores(a + 2, s_next[0])
        sel_scores(a + 3, s_next[1])
        m, alpha0 = sel_softmax(s_now[0], 0, m)
        m, alpha1 = sel_softmax(s_now[1], 1, m)
        return m, acc, alpha0, alpha1

    bufs_a, bufs_b = (s0_ref, s1_ref), (s2_ref, s3_ref)
    sel_scores(0, s0_ref)
    sel_scores(1, s1_ref)
    p0_ref[...] = jnp.zeros_like(p0_ref)
    p1_ref[...] = jnp.zeros_like(p1_ref)
    own = _dot(ks_ref[0, pl.ds(pl.multiple_of(t0, QT), QT), :], with_bias_rows(bias_all))

    n_wt = (WINDOW + QT) // KT
    k0w = pl.multiple_of(jnp.maximum(t0 - WINDOW, 0), KT)
    kt_w = k0w // KT
    keys_w = kw_ref[0, pl.ds(k0w, WINDOW + QT), :]
    dw = (t0 + lax.broadcasted_iota(jnp.int32, (WINDOW + QT, QT), 1)
          - (k0w + lax.broadcasted_iota(jnp.int32, (WINDOW + QT, QT), 0)))
    sw = _dot(keys_w, qpad) + tile_all(jnp.where(dw >= 0, jnp.where(dw < WINDOW, 0.0, NEG_INF), NEG_INF))
    pw = jnp.exp2(sw - jnp.max(sw, axis=0, keepdims=True)).astype(BF16)
    acc_w = values_dot(lambda g: vwt_ref[0, kt_w, vrows(g), :], pw[0:KT])
    for j in range(1, n_wt):
        acc_w = acc_w + values_dot(lambda g: vwt_ref[0, kt_w + j, vrows(g), :], pw[j * KT:(j + 1) * KT])

    n_pairs = (n_tiles + 1) // 2
    one = jnp.ones((1, NQ), F32)
    m_s, acc_s, alpha0, alpha1 = lax.fori_loop(
        0, n_pairs,
        lambda j, carry: lax.cond(j % 2 == 0,
                                  lambda c: sel_pair(2 * j, c, bufs_a, bufs_b),
                                  lambda c: sel_pair(2 * j, c, bufs_b, bufs_a), carry),
        (jnp.full((1, NQ), NEG_INF, F32), jnp.zeros((V_ROWS, NQ), F32), one, one))
    acc_s = sel_values(2 * n_pairs - 2, 0, acc_s, alpha0)
    acc_s = sel_values(2 * n_pairs - 1, 1, acc_s, alpha1)
    seen = lax.broadcasted_iota(jnp.int32, (QT, QT), 0) <= lax.broadcasted_iota(jnp.int32, (QT, QT), 1)
    own = jnp.where(tile_all(seen), own, NEG_INF)
    m_new = jnp.maximum(m_s, jnp.max(own, axis=0, keepdims=True))
    acc_s = acc_s * jnp.exp2(m_s - m_new) + values_dot(lambda g: vsd_ref[0, qb, vrows(g), :],
                                                       jnp.exp2(own - m_new).astype(BF16))

    gates = _sigmoid(gt_ref[0])
    grow = lambda j: jnp.concatenate([gates[g, j, r:r + 1, :] for g in range(G) for r in range(R)], axis=1)
    o = (grow(0) * o_c + grow(1) * (acc_s[0:D] / acc_s[D:D + 1])
         + grow(2) * (acc_w[0:D] / acc_w[D:D + 1]))
    halves = []
    for h in range(G * R // 2):
        pair = jnp.concatenate([o[:, (2 * h) * QT:(2 * h + 1) * QT],
                                o[:, (2 * h + 1) * QT:(2 * h + 2) * QT]], axis=0)
        halves.append(pair.T)
    o_ref[...] = jnp.concatenate(halves, axis=1)


def _nsa_attn(qt, kcmp, vct, ks3, vst, vsd, kw3, vwt, gt, ov_t, B, T):
    G, R = NSA_KV_HEADS, NSA_GROUP
    nq = T // Q_TILE
    nk = T // KEY_TILE
    nch = kcmp.shape[1]
    n_sel = ov_t.shape[0]
    NQ = G * R * Q_TILE
    assert (T // SEL_TILE) % 2 == 0 and n_sel <= LANES and Q_TILE == KEY_TILE
    assert nq % CMP_VARIANTS == 0 and n_sel % (8 * CMP_VARIANTS) == 0 and nch % (8 * CMP_VARIANTS) == 0
    return pl.pallas_call(
        _nsa_attn_kernel,
        grid=(B, nq),
        in_specs=[pl.BlockSpec((1, 1, NSA_WIDTH, Q_TILE), lambda b, q: (b, q, 0, 0)),
                  pl.BlockSpec((1, nch, LANES), lambda b, q: (b, 0, 0)),
                  pl.BlockSpec((1, G * HEAD_DIM, nch), lambda b, q: (b, 0, 0)),
                  pl.BlockSpec((1, T, 2 * LANES), lambda b, q: (b, 0, 0)),
                  pl.BlockSpec((1, T // SEL_TILE, G * V_ROWS, SEL_TILE), lambda b, q: (b, 0, 0, 0)),
                  pl.BlockSpec((1, nk, G * V_ROWS, KEY_TILE), lambda b, q: (b, 0, 0, 0)),
                  pl.BlockSpec((1, T, LANES), lambda b, q: (b, 0, 0)),
                  pl.BlockSpec((1, nk, G * V_ROWS, KEY_TILE), lambda b, q: (b, 0, 0, 0)),
                  pl.BlockSpec((1, G, 3, R, Q_TILE), lambda b, q: (b, 0, 0, 0, q)),
                  pl.BlockSpec((n_sel, nch), lambda b, q: (0, 0))],
        out_specs=pl.BlockSpec((Q_TILE, NSA_WIDTH), lambda b, q: (b * nq + q, 0)),
        out_shape=jax.ShapeDtypeStruct((B * T, NSA_WIDTH), F32),
        scratch_shapes=[pltpu.VMEM((2 * LANES, NQ), BF16), pltpu.VMEM((HEAD_DIM, NQ), F32),
                        pltpu.VMEM((n_sel, G * Q_TILE), F32),
                        *[pltpu.VMEM((SEL_TILE, NQ), F32)] * 4,
                        *[pltpu.VMEM((SEL_TILE, NQ), BF16)] * 2],
        compiler_params=_params("arbitrary", "arbitrary"),
        name="nsa_attn",
    )(qt, kcmp, vct, ks3, vst, vsd, kw3, vwt, gt, ov_t)


def _first_index_of(vals, target):
    idx = jnp.full_like(target, float(len(vals) - 1))
    for i in range(len(vals) - 2, -1, -1):
        idx = jnp.where(vals[i] == target, float(i), idx)
    return idx


def _pick(vals, idx):
    out = vals[-1]
    for i in range(len(vals) - 2, -1, -1):
        out = jnp.where(idx == float(i), vals[i], out)
    return out


def _route_rows(score, bias):
    E, G, P = N_EXPERTS, N_GROUPS, EXPERTS_PER_GROUP
    sel = score + bias
    s = [sel[e:e + 1, :] for e in range(E)]
    raw = [score[e:e + 1, :] for e in range(E)]
    grp = []
    for gi in range(G):
        a = s[gi * P:(gi + 1) * P]
        best = None
        for i in range(P):
            for j in range(i + 1, P):
                pair = a[i] + a[j]
                best = pair if best is None else jnp.maximum(best, pair)
        grp.append(best)
    gmax = functools.reduce(jnp.maximum, grp)
    g_star = _first_index_of(grp, gmax)
    v = [_pick([s[gi * P + i] for gi in range(G)], g_star) for i in range(P)]
    w = [_pick([raw[gi * P + i] for gi in range(G)], g_star) for i in range(P)]
    i1 = _first_index_of(v, functools.reduce(jnp.maximum, v))
    v2 = [jnp.where(i1 == float(i), -jnp.inf, v[i]) for i in range(P)]
    i2 = _first_index_of(v2, functools.reduce(jnp.maximum, v2))
    w1, w2 = _pick(w, i1), _pick(w, i2)
    tot = w1 + w2
    zero = jnp.zeros_like(tot)
    e1, e2 = g_star * P + i1, g_star * P + i2
    n = score.shape[1]
    eidx = lax.broadcasted_iota(jnp.int32, (E, n), 0).astype(F32)
    oh1, oh2 = jnp.where(eidx == e1, 1.0, 0.0), jnp.where(eidx == e2, 1.0, 0.0)
    earlier = jnp.where(lax.broadcasted_iota(jnp.int32, (n, n), 0) < lax.broadcasted_iota(jnp.int32, (n, n), 1),
                        1.0, 0.0).astype(BF16)
    cnt = _dot(jnp.concatenate([oh1, oh2], axis=0).astype(BF16), earlier)
    rank1 = jnp.sum(oh1 * cnt[0:E], axis=0, keepdims=True)
    rank2 = jnp.sum(oh2 * cnt[E:2 * E], axis=0, keepdims=True)
    lane = lax.broadcasted_iota(jnp.int32, (E, LANES), 1)
    totals = jnp.where(lane == 0, jnp.sum(oh1, axis=1, keepdims=True),
                       jnp.where(lane == 1, jnp.sum(oh2, axis=1, keepdims=True), 0.0))
    return jnp.concatenate([e1, e2, w1 / tot, w2 / tot, rank1, rank2, zero, zero], axis=0), totals


def _merge_kernel(ys_ref, g_ref, bonus_ref, gng_ref, gnb_ref, bd_ref, yb_ref, pm_ref, x_ref, mod_ref,
                  ng_ref, wa_ref, wb_ref, wo_ref, rw_ref, rb_ref, o_x, o_h, o_route, o_tot):
    m = mod_ref[pl.program_id(0)]
    bd = bd_ref[...]
    y = ys_ref[...]
    mean = _head_sums(y, bd)
    yc = y - mean
    var = _head_sums(yc * yc, bd)
    ya = (yc * lax.rsqrt(var + RWKV_GN_EPS) * gng_ref[...] + gnb_ref[...] + bonus_ref[...]) * g_ref[...]
    pm = pm_ref[...]
    D = x_ref.shape[1]
    mix = (_sigmoid(pm[:, 0:D]) * _dot(ya.astype(BF16), wa_ref[...])
           + _sigmoid(pm[:, D:2 * D]) * _dot(yb_ref[...].astype(BF16), wb_ref[...]))
    x = x_ref[...] + m[2:3] * _dot(mix.astype(BF16), wo_ref[...])
    o_x[...] = x
    ms = jnp.mean(x * x, axis=-1, keepdims=True)
    h = x * lax.rsqrt(ms + NORM_EPS) * ng_ref[...]
    h = h * (1.0 + m[4:5]) + m[3:4]
    o_h[...] = _pack_bf16_pairs(h)
    score = _sigmoid(_dot_3pass(h, rw_ref).T[0:N_EXPERTS, :])
    o_route[...], o_tot[...] = _route_rows(score, rb_ref[...])


def _merge(ys, g, bonus, gn_g, gn_b, yb, pm, x2, mod, ng, wa, wb, wo, router_w, router_b, B, T, tm=256):
    N, D = x2.shape
    W = RWKV_WIDTH
    nt = T // tm
    row = lambda b, t: (b * nt + t, 0)
    full = lambda shape: pl.BlockSpec(shape, lambda b, t: (0,) * len(shape))
    return pl.pallas_call(
        _merge_kernel,
        grid=(B, nt),
        in_specs=[pl.BlockSpec((tm, W), row), pl.BlockSpec((tm, W), row), pl.BlockSpec((tm, W), row),
                  full((1, W)), full((1, W)), full((HEAD_SUM_WIDTH, HEAD_SUM_WIDTH)),
                  pl.BlockSpec((tm, NSA_WIDTH), row), pl.BlockSpec((tm, 2 * D), row),
                  pl.BlockSpec((tm, D), row), full((B, 6, D)), full((1, D)),
                  full((W, D)), full((NSA_WIDTH, D)), full((D, D)), full((2, D, LANES)),
                  full((N_EXPERTS, 1))],
        out_specs=[pl.BlockSpec((tm, D), row), pl.BlockSpec((tm, D // 2), row),
                   pl.BlockSpec((8, tm), lambda b, t: (0, b * nt + t)),
                   pl.BlockSpec((N_EXPERTS, LANES), lambda b, t: (b * nt + t, 0))],
        out_shape=[jax.ShapeDtypeStruct((N, D), F32), jax.ShapeDtypeStruct((N, D // 2), F32),
                   jax.ShapeDtypeStruct((8, N), F32), jax.ShapeDtypeStruct((N // tm * N_EXPERTS, LANES), F32)],
        compiler_params=_params("arbitrary", "arbitrary"),
        name="merge_out",
    )(ys, g, bonus, gn_g.reshape(1, W), gn_b.reshape(1, W), _head_block_diag(HEAD_SUM_WIDTH, 1.0 / HEAD_DIM),
      yb, pm, x2, mod, ng, wa, wb, wo,
      _hi_lo(jnp.zeros((D, LANES), F32).at[:, :N_EXPERTS].set(router_w)), router_b.reshape(N_EXPERTS, 1))


def _route(route, totals, N):
    wts = route[TOP_K:2 * TOP_K].T
    NK = N * TOP_K
    E = N_EXPERTS
    n_tiles = totals.shape[0] // E
    expert = route[0:TOP_K].astype(jnp.int32)
    rank = route[2 * TOP_K:3 * TOP_K].astype(jnp.int32)
    per = totals.reshape(n_tiles, E, LANES)[:, :, 0:TOP_K].astype(jnp.int32).transpose(0, 2, 1)
    per = per.reshape(n_tiles * TOP_K, E)
    csum = jnp.cumsum(per, axis=0)
    counts = csum[-1]
    padded = (counts + MOE_BLOCK - 1) // MOE_BLOCK * MOE_BLOCK
    pad_end = jnp.cumsum(padded)
    pad_start = pad_end - padded
    first = (pad_start[None, :] + csum - per).reshape(n_tiles, TOP_K, E).transpose(1, 0, 2)
    first = jnp.repeat(first, N // n_tiles, axis=1)
    mine = expert[:, :, None] == jnp.arange(E, dtype=jnp.int32)[None, None, :]
    dest = (jnp.sum(jnp.where(mine, first, 0), axis=-1) + rank).reshape(-1)
    n_blk = -(-NK // MOE_BLOCK) + N_EXPERTS
    blk_start = jnp.arange(n_blk, dtype=jnp.int32) * MOE_BLOCK
    blk_expert = jnp.sum((pad_end[None, :] <= blk_start[:, None]).astype(jnp.int32), axis=1)
    blk_expert = jnp.clip(blk_expert, 0, N_EXPERTS - 1)
    blk_valid = jnp.clip((pad_start + counts)[blk_expert] - blk_start, 0, MOE_BLOCK).astype(jnp.int32)
    dest = jnp.pad(dest.astype(jnp.int32).reshape(NK // SC_WINDOW, SC_WINDOW), ((0, 0), (0, LANES - SC_WINDOW)))
    return wts, dest, blk_expert, blk_valid, n_blk


SC_WINDOW = 64


def _sc_mesh():
    return plsc.VectorSubcoreMesh(core_axis_name="c", subcore_axis_name="s")


def _sc_dispatch(h, dest, n_slots):
    N, D = h.shape
    W = SC_WINDOW
    nw = N // W

    @pl.kernel(out_type=jax.ShapeDtypeStruct((n_slots, D), h.dtype), mesh=_sc_mesh(), scratch_types=[])
    def dispatch(h_hbm, i_hbm, o_hbm):
        def body(x_vmem, i_vmem):
            pltpu.sync_copy(x_vmem, o_hbm.at[i_vmem.at[0, pl.ds(0, W)]])

        pltpu.emit_pipeline(
            body, grid=(TOP_K, nw),
            in_specs=[pl.BlockSpec((W, D), lambda k, i: (i, 0)),
                      pl.BlockSpec((1, LANES), lambda k, i: (k * nw + i, 0))],
            out_specs=[], core_axis_name=("c", "s"),
            dimension_semantics=(pltpu.PARALLEL, pltpu.PARALLEL))(h_hbm, i_hbm)

    return dispatch(h, dest)


def _sc_collect(ys, dest):
    W = SC_WINDOW
    NK = dest.shape[0] * W
    D = ys.shape[1]
    half = NK // TOP_K // W

    @pl.kernel(out_type=jax.ShapeDtypeStruct((NK, D), ys.dtype), mesh=_sc_mesh(), scratch_types=[])
    def collect(y_hbm, i_hbm, o_hbm):
        def body(i_vmem, o_vmem):
            pltpu.sync_copy(y_hbm.at[i_vmem.at[0, pl.ds(0, W)]], o_vmem)

        pltpu.emit_pipeline(
            body, grid=(TOP_K, half),
            in_specs=[pl.BlockSpec((1, LANES), lambda k, i: (k * half + i, 0))],
            out_specs=[pl.BlockSpec((W, D), lambda k, i: (k * half + i, 0))],
            core_axis_name=("c", "s"),
            dimension_semantics=(pltpu.PARALLEL, pltpu.PARALLEL))(i_hbm, o_hbm)

    return collect(ys, dest)


def _moe_dense_kernel(be_ref, nv_ref, x_ref, wg_ref, wu_ref, wd_ref, o_ref, wg_b, wu_b, wd_b):
    i = pl.program_id(0)
    nv = nv_ref[i]

    @pl.when((i == 0) | (be_ref[i] != be_ref[jnp.maximum(i - 1, 0)]))
    def _():
        wg_b[...] = wg_ref[0, 0].astype(BF16)
        wu_b[...] = wu_ref[0, 0].astype(BF16)
        wd_b[...] = wd_ref[0, 0].astype(BF16)

    @pl.when(nv > 0)
    def _():
        x_lo, x_hi = _unpack_bf16_pairs(x_ref[...])
        half = x_lo.shape[1]
        gate = _dot(x_lo, wg_b[0:half, :]) + _dot(x_hi, wg_b[half:, :])
        up = _dot(x_lo, wu_b[0:half, :]) + _dot(x_hi, wu_b[half:, :])
        o_ref[...] = _pack_bf16_pairs(_dot((gate * _sigmoid(gate) * up).astype(BF16), wd_b[...]))

    @pl.when(nv == 0)
    def _():
        o_ref[...] = jnp.zeros_like(o_ref)


def _moe_dense(xs, blk_expert, blk_valid, n_blk, layer, wg, wu, wd):
    P = xs.shape[0]
    D, DE = wg.shape[2:]
    wmap = lambda i, be, nv: (layer, be[i], 0, 0)
    grid_spec = pltpu.PrefetchScalarGridSpec(
        num_scalar_prefetch=2,
        grid=(n_blk,),
        in_specs=[pl.BlockSpec((MOE_BLOCK, D // 2), lambda i, be, nv: (i, 0)), pl.BlockSpec((1, 1, D, DE), wmap),
                  pl.BlockSpec((1, 1, D, DE), wmap), pl.BlockSpec((1, 1, DE, D), wmap)],
        out_specs=pl.BlockSpec((MOE_BLOCK, D // 2), lambda i, be, nv: (i, 0)),
        scratch_shapes=[pltpu.VMEM((D, DE), BF16), pltpu.VMEM((D, DE), BF16), pltpu.VMEM((DE, D), BF16)],
    )
    return pl.pallas_call(
        _moe_dense_kernel,
        grid_spec=grid_spec,
        out_shape=jax.ShapeDtypeStruct((P, D // 2), F32),
        compiler_params=_params("arbitrary"),
        name="moe_experts",
    )(blk_expert, blk_valid, xs, wg, wu, wd)


def _final_kernel(x_ref, y0_ref, y1_ref, w_ref, mod_ref, o_ref):
    o_ref[...] = _moe_residual(x_ref, y0_ref, y1_ref, w_ref, mod_ref[pl.program_id(0)])


def _final(x2, ybuf, wts, mod, B, T, tm=512):
    N, D = x2.shape
    nt = T // tm
    row = lambda b, t: (b * nt + t, 0)
    return pl.pallas_call(
        _final_kernel,
        grid=(B, nt),
        in_specs=[pl.BlockSpec((tm, D), row), pl.BlockSpec((tm, D // 2), row),
                  pl.BlockSpec((tm, D // 2), lambda b, t: (N // tm + b * nt + t, 0)),
                  pl.BlockSpec((tm, TOP_K), row), pl.BlockSpec((B, 6, D), lambda b, t: (0, 0, 0))],
        out_specs=pl.BlockSpec((tm, D), row),
        out_shape=jax.ShapeDtypeStruct((N, D), F32),
        compiler_params=_params("arbitrary", "arbitrary"),
        name="moe_combine",
    )(x2, ybuf, ybuf, wts, mod)


def _overlap_t(n_sel, n_cmp_pad):
    ci = jnp.arange(n_cmp_pad)[None, :] * CMP_STRIDE
    sj = jnp.arange(n_sel)[:, None] * SEL_BLOCK
    ov = (ci <= sj + SEL_BLOCK - 1) & (ci + CMP_BLOCK - 1 >= sj) & (jnp.arange(n_cmp_pad)[None, :] < n_cmp_pad - 1)
    return ov.astype(BF16)


def kernel(x, c, w_ada, b_ada, norm_g, w_in, b_in, rwkv_mu, rwkv_w0, rwkv_w2, rwkv_a0, rwkv_a2, rwkv_g2,
           rwkv_k_k, rwkv_k_a, rwkv_r_k, rwkv_gn_g, rwkv_gn_b, qk_norm_g, cmp_pos, cmp_w1, cmp_w2,
           w_up_rwkv, w_up_nsa, w_out, router_w, router_b, exp_w_gate, exp_w_up, exp_w_down):
    B, T, D = x.shape
    L = w_ada.shape[0]
    N = B * T
    mods = _ada(c, w_ada, b_ada)
    tables = _rope_tables(jnp.arange(T, dtype=jnp.int32))
    nch = T // CMP_STRIDE
    tables_cmp = _rope_tables(jnp.arange(nch, dtype=jnp.int32) * CMP_STRIDE + CMP_BLOCK - 1)
    ov_t = _overlap_t(T // SEL_BLOCK, nch)
    n_gate = NSA_GATE_COLS
    x2 = x.reshape(N, D)
    pending_moe = None
    for l in range(L):
        g0 = _SEG_KV[1] + n_gate
        w_b = w_in[l].astype(BF16)
        w_pad = jnp.concatenate([w_b[:, :g0], jnp.zeros((D, GATE_PAD - n_gate), BF16), w_b[:, g0:]], axis=1)
        b_pad = jnp.concatenate([b_in[l][:g0], jnp.zeros((GATE_PAD - n_gate,), F32), b_in[l][g0:]]).reshape(1, -1)
        outs = _inproj(x2, pending_moe, mods[l], norm_g[l, 0].reshape(1, D), w_pad, b_pad, B, T)
        if pending_moe is not None:
            x2, outs = outs[0], outs[1:]
        p_rw, p_q, p_kv, p_gate, p_merge = outs
        r, k, v, al, bb, ld, g, bonus = _rwkv_pre(p_rw, rwkv_mu[l], rwkv_w0[l], rwkv_w2[l], rwkv_a0[l],
                                                  rwkv_a2[l], rwkv_g2[l], rwkv_k_k[l], rwkv_k_a[l],
                                                  rwkv_r_k[l], B, T)
        ys = _rwkv_scan(r, k, v, al, bb, ld, B, T)
        qt, ks, kw, vst, vsd, vwt = _nsa_prep(p_q, p_kv, tables, qk_norm_g[l], B, T)
        kv3 = p_kv.reshape(B, T, KV_COLS)
        kcmp = _nsa_cmp(kv3, 0, cmp_pos[l], cmp_w1[l], cmp_w2[l], qk_norm_g[l, 1], tables_cmp)
        vct = _nsa_cmp(kv3, 1, cmp_pos[l], cmp_w1[l], cmp_w2[l], None, None)
        gt = p_gate[:, :n_gate].reshape(B, T, NSA_KV_HEADS, NSA_GROUP, 3).transpose(0, 2, 4, 3, 1)
        yb = _nsa_attn(qt, kcmp, vct, ks.reshape(B, T, 2 * LANES), vst, vsd, kw.reshape(B, T, LANES), vwt, gt, ov_t,
                       B, T)
        x2, h2, route, totals = _merge(ys, g, bonus, rwkv_gn_g[l], rwkv_gn_b[l], yb, p_merge, x2, mods[l],
                               norm_g[l, 1].reshape(1, D), w_up_rwkv[l].astype(BF16),
                               w_up_nsa[l].astype(BF16), w_out[l].astype(BF16), router_w, router_b, B, T)
        wts, dest, blk_expert, blk_valid, n_blk = _route(route, totals, N)
        xs = _sc_dispatch(h2, dest, n_blk * MOE_BLOCK)
        ys = _moe_dense(xs, blk_expert, blk_valid, n_blk, l, exp_w_gate, exp_w_up, exp_w_down)
        ybuf = _sc_collect(ys, dest)
        pending_moe = (ybuf, wts, mods[l])
    return _final(x2, *pending_moe, B, T).reshape(B, T, D)
```

```python
import functools
import math

import jax
import jax.numpy as jnp
from jax import lax
from jax.experimental import pallas as pl
from jax.experimental.pallas import tpu as pltpu
from jax.experimental.pallas import tpu_sc as plsc

F32 = jnp.float32
BF16 = jnp.bfloat16
HI = lax.Precision.HIGHEST

D_MODEL = 1024
RWKV_HEADS = 8
HEAD_DIM = 64
RWKV_WIDTH = RWKV_HEADS * HEAD_DIM
DECAY_LORA = 64
ICLR_LORA = 64
GATE_LORA = 128
RWKV_GN_EPS = 64e-5
RWKV_COLS = 3 * RWKV_WIDTH + DECAY_LORA + ICLR_LORA + GATE_LORA

NSA_Q_HEADS = 8
NSA_KV_HEADS = 2
NSA_GROUP = NSA_Q_HEADS // NSA_KV_HEADS
NSA_WIDTH = NSA_Q_HEADS * HEAD_DIM
CMP_STRIDE = 16
CMP_BLOCK = 2 * CMP_STRIDE
CMP_HIDDEN = 256
SEL_BLOCK = 64
SEL_SHIFT = 6
SEL_TOPK = 16
WINDOW = 512
FORCE_SCORE = 1e4
NEG_INF = -1e30
ROPE_THETA = 500000.0
ROPE_DIM = HEAD_DIM // 4
KV_COLS = 6 * NSA_KV_HEADS * HEAD_DIM
NSA_GATE_COLS = 3 * NSA_Q_HEADS
GATE_PAD = 128

N_EXPERTS = 16
N_GROUPS = 4
EXPERTS_PER_GROUP = N_EXPERTS // N_GROUPS
TOP_K = 2
D_EXPERT = 512
MOE_BLOCK = 512
NORM_EPS = 1e-6

LANES = 128
CHUNK = 64
KEY_TILE = 128
SEL_TILE = 512
CMP_VARIANTS = 8
V_ROWS = 80
Q_SCALE = HEAD_DIM ** -0.5 * math.log2(math.e)
Q_TILE = 128
F32_TINY = float(jnp.finfo(jnp.float32).tiny)

_SEG_RW = (0, RWKV_COLS)
_SEG_Q = (_SEG_RW[1], _SEG_RW[1] + NSA_WIDTH)
_SEG_KV = (_SEG_Q[1], _SEG_Q[1] + KV_COLS)
_SEG_GATE = (_SEG_KV[1], _SEG_KV[1] + GATE_PAD)
_SEG_MERGE = (_SEG_GATE[1], _SEG_GATE[1] + 2 * D_MODEL)
IN_COLS_PAD = _SEG_MERGE[1]

_VMEM_LIMIT = 56 * 1024 * 1024


def _dot(a, b, precision=None):
    return jnp.dot(a, b, preferred_element_type=F32, precision=precision)


def _dot_tb(a, b, precision=None):
    return lax.dot_general(a, b, (((1,), (1,)), ((), ())), preferred_element_type=F32,
                           precision=precision)


def _split_bf16(x, terms):
    parts = []
    for _ in range(terms - 1):
        parts.append(x.astype(BF16))
        x = x - parts[-1].astype(F32)
    parts.append(x.astype(BF16))
    return parts


def _dot_split_lhs(x, w_bf, terms=2):
    return functools.reduce(jnp.add, [_dot(p, w_bf) for p in _split_bf16(x, terms)])


def _dot_split_rhs(w_bf, x, terms=2):
    return functools.reduce(jnp.add, [_dot(w_bf, p) for p in _split_bf16(x, terms)])


def _dot_3pass(x, w_hl_ref):
    x_hi, x_lo = _split_bf16(x, 2)
    w_hi = w_hl_ref[0]
    return _dot(x_hi, w_hi) + _dot(x_lo, w_hi) + _dot(x_hi, w_hl_ref[1])


def _pack_bf16_pairs(x):
    n = x.shape[1] // 2
    bits = lax.bitcast_convert_type(x.astype(BF16).astype(F32), jnp.uint32)
    return lax.bitcast_convert_type(bits[:, 0:n] | (bits[:, n:] >> 16), F32)


def _unpack_bf16_pairs(packed):
    bits = lax.bitcast_convert_type(packed, jnp.uint32)
    lo = lax.bitcast_convert_type(bits & jnp.uint32(0xFFFF0000), F32)
    hi = lax.bitcast_convert_type(bits << 16, F32)
    return lo.astype(BF16), hi.astype(BF16)


def _hi_lo(w):
    hi = w.astype(BF16)
    return jnp.stack([hi, (w - hi.astype(F32)).astype(BF16)])


def _params(*sem):
    return pltpu.CompilerParams(dimension_semantics=sem, vmem_limit_bytes=_VMEM_LIMIT)


def _sigmoid(x):
    return 1.0 / (1.0 + jnp.exp(-x))


def _ada_kernel(c_ref, w_ref, b_ref, o_ref):
    c = c_ref[...]
    s = c * _sigmoid(c)
    o_ref[0] = _dot(s, w_ref[0], HI) + b_ref[0]


def _ada(c, w_ada, b_ada):
    L, D, D6 = w_ada.shape
    B = c.shape[0]
    rows = 8
    cp = jnp.zeros((rows, D), F32).at[:B].set(c)
    tn = 1536
    out = pl.pallas_call(
        _ada_kernel,
        grid=(L, D6 // tn),
        in_specs=[pl.BlockSpec((rows, D), lambda l, j: (0, 0)),
                  pl.BlockSpec((1, D, tn), lambda l, j: (l, 0, j)),
                  pl.BlockSpec((1, 1, tn), lambda l, j: (l, 0, j))],
        out_specs=pl.BlockSpec((1, rows, tn), lambda l, j: (l, 0, j)),
        out_shape=jax.ShapeDtypeStruct((L, rows, D6), F32),
        compiler_params=_params("arbitrary", "arbitrary"),
        name="ada_mod",
    )(cp, w_ada, b_ada.reshape(L, 1, D6))
    return out[:, :B].reshape(L, B, 6, D)


def _moe_residual(x_ref, y0_ref, y1_ref, w_ref, m_prev):
    w = w_ref[...]
    rows = lambda y_ref: jnp.concatenate([h.astype(F32) for h in _unpack_bf16_pairs(y_ref[...])], axis=1)
    return x_ref[...] + m_prev[5:6] * (w[:, 0:1] * rows(y0_ref) + w[:, 1:2] * rows(y1_ref))


def _inproj_kernel(*refs, after_moe):
    if after_moe:
        x_ref, y0_ref, y1_ref, wts_ref, modp_ref, mod_ref, g_ref, w_ref, b_ref, o_x = refs[:10]
        x = _moe_residual(x_ref, y0_ref, y1_ref, wts_ref, modp_ref[pl.program_id(0)])
        o_x[...] = x
    else:
        x_ref, mod_ref, g_ref, w_ref, b_ref = refs[:5]
        x = x_ref[...]
    o_rw, o_q, o_kv, o_gate, o_merge = refs[-5:]
    m = mod_ref[pl.program_id(0)]
    ms = jnp.mean(x * x, axis=-1, keepdims=True)
    h = x * lax.rsqrt(ms + NORM_EPS) * g_ref[...]
    h = h * (1.0 + m[1:2]) + m[0:1]
    hb = h.astype(BF16)
    for o, (a, e) in ((o_rw, _SEG_RW), (o_q, _SEG_Q), (o_kv, _SEG_KV), (o_gate, _SEG_GATE),
                      (o_merge, _SEG_MERGE)):
        o[...] = (_dot(hb, w_ref[:, a:e]) + b_ref[:, a:e]).astype(o.dtype)


def _inproj(x2, pending_moe, mod, g, w_pad, b_pad, B, T, tm=512):
    N, D = x2.shape
    nt = T // tm
    row = lambda b, t: (b * nt + t, 0)
    mods_spec = pl.BlockSpec((B, 6, D), lambda b, t: (0, 0, 0))
    widths = [e - a for a, e in (_SEG_RW, _SEG_Q, _SEG_KV, _SEG_GATE, _SEG_MERGE)]
    in_specs = [pl.BlockSpec((tm, D), row)]
    args = [x2]
    if pending_moe is not None:
        ybuf, wts, mod_prev = pending_moe
        in_specs += [pl.BlockSpec((tm, D // 2), row),
                     pl.BlockSpec((tm, D // 2), lambda b, t: (N // tm + b * nt + t, 0)),
                     pl.BlockSpec((tm, TOP_K), row), mods_spec]
        args += [ybuf, ybuf, wts, mod_prev]
        widths = [D] + widths
    in_specs += [mods_spec, pl.BlockSpec((1, D), lambda b, t: (0, 0)),
                 pl.BlockSpec((D, IN_COLS_PAD), lambda b, t: (0, 0)),
                 pl.BlockSpec((1, IN_COLS_PAD), lambda b, t: (0, 0))]
    return pl.pallas_call(
        functools.partial(_inproj_kernel, after_moe=pending_moe is not None),
        grid=(B, nt),
        in_specs=in_specs,
        out_specs=[pl.BlockSpec((tm, w), row) for w in widths],
        out_shape=[jax.ShapeDtypeStruct((N, w), BF16 if i == len(widths) - 1 else F32)
                   for i, w in enumerate(widths)],
        compiler_params=_params("arbitrary", "arbitrary"),
        name="in_proj",
    )(*args, mod, g, w_pad, b_pad)


def _rwkv_pre_kernel(p_ref, mu_ref, w0_ref, w2_ref, a0_ref, a2_ref, g2_ref, kk_ref, ka_ref, rk_ref,
                     bd_ref, o_r, o_k, o_v, o_al, o_b, o_ld, o_g, o_bonus, carry_ref):
    W = RWKV_WIDTH

    @pl.when(pl.program_id(1) == 0)
    def _():
        carry_ref[...] = jnp.zeros_like(carry_ref)

    p = p_ref[...]
    ts = p.shape[0]
    rows = lax.broadcasted_iota(jnp.int32, p.shape, 0)
    shifted = jnp.where(rows == 0, carry_ref[0:1, :], pltpu.roll(p, 1, 0))
    carry_ref[0:1, :] = p[ts - 1:ts, :]
    pm = p + (shifted - p) * mu_ref[...]
    r = pm[:, 0:W]
    k = pm[:, W:2 * W]
    v = pm[:, 2 * W:3 * W]
    wa = pm[:, 3 * W:3 * W + DECAY_LORA + ICLR_LORA]
    gl = pm[:, 3 * W + DECAY_LORA + ICLR_LORA:]
    xw = w0_ref[...] + _dot_3pass(jnp.tanh(wa), w2_ref)
    ld = -math.exp(-0.5) * _sigmoid(xw)
    a = _sigmoid(a0_ref[...] + _dot_3pass(wa, a2_ref))
    g = _dot_3pass(_sigmoid(gl), g2_ref)
    bd = bd_ref[...]
    kk = k * kk_ref[...]
    nrm = jnp.sqrt(_head_sums(kk * kk, bd))
    kk = kk / jnp.maximum(nrm, 1e-12)
    k2 = k * (1.0 + (a - 1.0) * ka_ref[...])
    bonus = _head_sums(r * k2 * rk_ref[...], bd) * v
    o_r[...] = r
    o_k[...] = k2
    o_v[...] = v
    o_al[...] = kk
    o_b[...] = -kk * a
    o_ld[...] = ld
    o_g[...] = g
    o_bonus[...] = bonus


HEAD_SUM_WIDTH = 256


def _head_sums(x, bd):
    w = bd.shape[0]
    parts = [_dot_split_lhs(x[:, j:j + w], bd) for j in range(0, x.shape[1], w)]
    return parts[0] if len(parts) == 1 else jnp.concatenate(parts, axis=1)


def _head_block_diag(width, scale=1.0):
    i = jnp.arange(width) // HEAD_DIM
    return ((i[:, None] == i[None, :]).astype(F32) * scale).astype(BF16)


def _rwkv_pre(p_rw, mu, w0, w2, a0, a2, g2, k_k, k_a, r_k, B, T, ts=512):
    N = p_rw.shape[0]
    W = RWKV_WIDTH
    nt = T // ts
    row = lambda b, t: (b * nt + t, 0)
    zl = jnp.zeros((DECAY_LORA, W), F32)
    w2p = jnp.concatenate([w2, zl], axis=0)
    a2p = jnp.concatenate([zl, a2], axis=0)
    full = lambda shape: pl.BlockSpec(shape, lambda b, t: (0,) * len(shape))
    vec = lambda z: z.reshape(1, -1)
    return pl.pallas_call(
        _rwkv_pre_kernel,
        grid=(B, nt),
        in_specs=[pl.BlockSpec((ts, RWKV_COLS), row), full((1, RWKV_COLS)), full((1, W)),
                  full((2, 2 * DECAY_LORA, W)), full((1, W)), full((2, 2 * DECAY_LORA, W)),
                  full((2, GATE_LORA, W)), full((1, W)), full((1, W)), full((1, W)),
                  full((HEAD_SUM_WIDTH, HEAD_SUM_WIDTH))],
        out_specs=[pl.BlockSpec((ts, W), row)] * 8,
        out_shape=[jax.ShapeDtypeStruct((N, W), F32)] * 8,
        scratch_shapes=[pltpu.VMEM((8, RWKV_COLS), F32)],
        compiler_params=_params("arbitrary", "arbitrary"),
        name="rwkv_pre",
    )(p_rw, vec(mu), vec(w0), _hi_lo(w2p), vec(a0), _hi_lo(a2p), _hi_lo(g2), vec(k_k), vec(k_a), vec(r_k),
      _head_block_diag(HEAD_SUM_WIDTH))


def _bf(x):
    return x.astype(BF16)


def _scan_local(chunks, eye, strict, incl, m0, m1, between_stages=lambda: None):
    C = CHUNK
    n = range(len(chunks))
    st = lambda z: jnp.concatenate([z * m0, z * m1], axis=0)
    zero = jnp.zeros((2 * C, 2 * C), F32)
    at_b, rt_s, vs, vs_b, lhs_a, rhs_a, bk_t, dcol = [], [], [], [], [], [], [], []
    for r, k, v, al, bb, ld, cum in chunks:
        tot = cum[C - 1:C, :]
        dinv = jnp.exp(-cum)
        dend = jnp.exp(tot - cum)
        at_b.append(_bf(st(al * jnp.exp(cum - ld))))
        rt_s.append(st(r * jnp.exp(cum)))
        vs.append(st(v))
        vs_b.append(_bf(vs[-1]))
        lhs_a.append(jnp.concatenate([at_b[-1], _bf(rt_s[-1])], axis=0))
        rhs_a.append(_bf(jnp.concatenate([st(bb * dinv), st(k * dinv)], axis=0)))
        bk_t.append(_bf(jnp.concatenate([st(bb * dend).T, st(k * dend).T], axis=1)))
        dcol.append(jnp.sum(eye * jnp.exp(tot), axis=1, keepdims=True))
    between_stages()
    A = [_dot_tb(lhs_a[i], rhs_a[i]) for i in n]
    between_stages()
    a_ab = [jnp.where(strict, A[i][0:2 * C, 0:2 * C], zero) for i in n]
    a_ak = [_bf(jnp.where(strict, A[i][0:2 * C, 2 * C:4 * C], zero)) for i in n]
    a_r = [_bf(jnp.concatenate([jnp.where(incl, A[i][2 * C:4 * C, 0:2 * C], zero),
                                jnp.where(incl, A[i][2 * C:4 * C, 2 * C:4 * C], zero)], axis=1)) for i in n]
    akv = [_bf(_dot(a_ak[i], vs_b[i])) for i in n]
    between_stages()
    tinv = [eye + a_ab[i] for i in n]
    pw_b = [_bf(a_ab[i]) for i in n]
    pw_b = [_bf(_dot(pw_b[i], pw_b[i])) for i in n]
    between_stages()
    for step in range(5):
        rhs = [jnp.concatenate([pw_b[i], _bf(tinv[i])], axis=1) for i in n]
        if step == 4:
            rhs = [_bf(tinv[i]) for i in n]
        prod = [_dot(pw_b[i], rhs[i]) for i in n]
        tinv = [tinv[i] + prod[i][:, -2 * C:] for i in n]
        pw_b = [_bf(prod[i][:, 0:2 * C]) for i in n]
        between_stages()
    X = [_dot(_bf(tinv[i]), jnp.concatenate([at_b[i], akv[i]], axis=1)) for i in n]
    between_stages()
    w_b = [_bf(X[i][:, 0:LANES]) for i in n]
    uv0 = [jnp.concatenate([_bf(X[i][:, LANES:2 * LANES]), vs_b[i]], axis=0) for i in n]
    m_h = [_bf(_dot(bk_t[i][:, 0:2 * C], w_b[i])) for i in n]
    g_h = [_dot(bk_t[i], uv0[i]) for i in n]
    between_stages()
    q_h = [_bf(rt_s[i] + _dot(a_r[i][:, 0:2 * C], w_b[i])) for i in n]
    y0 = [_dot(a_r[i], uv0[i]) for i in n]
    return [(m_h[i], g_h[i], dcol[i], q_h[i], y0[i]) for i in n]


def _rwkv_scan_kernel(r_ref, k_ref, v_ref, al_ref, b_ref, ld_ref, o_ref, h_ref, *local_refs):
    C = CHUNK
    tc = r_ref.shape[0]
    nc = tc // C

    @pl.when(pl.program_id(2) == 0)
    def _():
        h_ref[...] = jnp.zeros_like(h_ref)
        for ref in local_refs:
            ref[...] = jnp.zeros_like(ref)

    seq = {"H": h_ref[...], "c": 0}

    def one_step():
        c = seq["c"]
        if c < nc:
            m_h, g_h, dcol, q_h, y0 = (ref[c] for ref in local_refs)
            h_b = _bf(seq["H"])
            Y = _dot(q_h, h_b) + y0
            o_ref[c * C:(c + 1) * C, :] = Y[0:C] + Y[C:2 * C]
            seq["H"] = dcol * seq["H"] + _dot(m_h, h_b) + g_h
            seq["c"] = c + 1

    tri = jnp.where(lax.broadcasted_iota(jnp.int32, (C, C), 1) <= lax.broadcasted_iota(jnp.int32, (C, C), 0),
                    1.0, 0.0).astype(BF16)
    r2 = lax.broadcasted_iota(jnp.int32, (2 * C, 2 * C), 0)
    c2 = lax.broadcasted_iota(jnp.int32, (2 * C, 2 * C), 1)
    eye = (r2 == c2).astype(F32)
    strict = (c2 & (C - 1)) < (r2 & (C - 1))
    incl = (c2 & (C - 1)) <= (r2 & (C - 1))
    lane = lax.broadcasted_iota(jnp.int32, (C, LANES), 1)
    m0 = (lane < HEAD_DIM).astype(F32)
    m1 = 1.0 - m0
    cum = _dot_split_rhs(tri, jnp.concatenate([ld_ref[c * C:(c + 1) * C, :] for c in range(nc)], axis=1), 3)
    chunks = []
    for c in range(nc):
        sl = slice(c * C, (c + 1) * C)
        chunks.append((r_ref[sl, :], k_ref[sl, :], v_ref[sl, :], al_ref[sl, :], b_ref[sl, :], ld_ref[sl, :],
                       cum[:, c * LANES:(c + 1) * LANES]))
    local = _scan_local(chunks, eye, strict, incl, m0, m1, between_stages=one_step)
    while seq["c"] < nc:
        one_step()
    h_ref[...] = seq["H"]
    for c, parts in enumerate(local):
        for ref, part in zip(local_refs, parts):
            ref[c] = part


def _rwkv_scan(r, k, v, al, bb, ld, B, T, tc=512):
    N, W = r.shape
    nt = T // tc
    nc = tc // CHUNK
    in_spec = pl.BlockSpec((tc, LANES), lambda b, h, t: (b * nt + jnp.minimum(t, nt - 1), h))
    out_spec = pl.BlockSpec((tc, LANES), lambda b, h, t: (b * nt + jnp.maximum(t - 1, 0), h))
    sq = (nc, LANES, LANES)
    return pl.pallas_call(
        _rwkv_scan_kernel,
        grid=(B, W // LANES, nt + 1),
        in_specs=[in_spec] * 6,
        out_specs=out_spec,
        out_shape=jax.ShapeDtypeStruct((N, W), F32),
        scratch_shapes=[pltpu.VMEM((LANES, LANES), F32), pltpu.VMEM(sq, BF16), pltpu.VMEM(sq, F32),
                        pltpu.VMEM((nc, LANES, 1), F32), pltpu.VMEM(sq, BF16), pltpu.VMEM(sq, F32)],
        compiler_params=_params("arbitrary", "arbitrary", "arbitrary"),
        name="rwkv_scan",
    )(r, k, v, al, bb, ld)


def _rope_tables(pos):
    half = ROPE_DIM // 2
    inv = jnp.power(ROPE_THETA, -jnp.arange(half, dtype=F32) * 2.0 / ROPE_DIM)
    ang = pos.astype(F32)[:, None] * inv[None, :]
    cos, sin = jnp.cos(ang), jnp.sin(ang)
    n = pos.shape[0]
    rest = HEAD_DIM - ROPE_DIM
    c = jnp.concatenate([cos, cos, jnp.ones((n, rest), F32)], axis=1)
    s_dn = jnp.concatenate([-sin, jnp.zeros((n, half + rest), F32)], axis=1)
    s_up = jnp.concatenate([jnp.zeros((n, half), F32), sin, jnp.zeros((n, rest), F32)], axis=1)
    rep = LANES // HEAD_DIM
    return jnp.tile(c, (1, rep)), jnp.tile(s_dn, (1, rep)), jnp.tile(s_up, (1, rep))


def _norm_rope(x, bd, g, c, s_dn, s_up):
    width = x.shape[1]
    half = ROPE_DIM // 2
    rep = width // LANES
    tile = (lambda z: jnp.concatenate([z] * rep, axis=1)) if rep > 1 else (lambda z: z)
    ms = _head_sums(x * x, bd)
    xn = x * lax.rsqrt(ms + NORM_EPS) * g
    return (xn * tile(c) + pltpu.roll(xn, width - half, 1) * tile(s_dn)
            + pltpu.roll(xn, half, 1) * tile(s_up))


def _nsa_prep_kernel(q_ref, kv_ref, c_ref, sd_ref, su_ref, gq_ref, gs_ref, gw_ref, bdq_ref, bdk_ref,
                     o_qt, o_ks, o_kw, o_vst, o_vsd, o_vwt):
    c, sd, su = c_ref[...], sd_ref[...], su_ref[...]
    q = _norm_rope(q_ref[...], bdq_ref[...], gq_ref[...], c, sd, su) * Q_SCALE
    qt = q.T
    ts = q.shape[0]
    kv = kv_ref[...]
    bdk = bdk_ref[...]
    pos = pl.program_id(1) * ts + lax.broadcasted_iota(jnp.int32, (ts, LANES), 0)
    blk_onehot = jnp.where((pos >> SEL_SHIFT) == lax.broadcasted_iota(jnp.int32, (ts, LANES), 1), 1.0, 0.0)
    ks = _norm_rope(kv[:, 2 * LANES:3 * LANES], bdk, gs_ref[...], c, sd, su)
    o_ks[...] = jnp.concatenate([ks, blk_onehot], axis=1).astype(BF16)
    o_kw[...] = _norm_rope(kv[:, 4 * LANES:5 * LANES], bdk, gw_ref[...], c, sd, su).astype(BF16)
    ones_rows = jnp.where(lax.broadcasted_iota(jnp.int32, (V_ROWS - HEAD_DIM, q.shape[0]), 0) == 0, 1.0, 0.0)

    def values_t(x):
        xt = x.T
        return jnp.concatenate([xt[0:HEAD_DIM], ones_rows, xt[HEAD_DIM:2 * HEAD_DIM], ones_rows], axis=0)

    vst = values_t(kv[:, 3 * LANES:4 * LANES])
    vwt = values_t(kv[:, 5 * LANES:6 * LANES])
    for j in range(q.shape[0] // KEY_TILE):
        sl = slice(j * KEY_TILE, (j + 1) * KEY_TILE)
        o_qt[0, j] = qt[:, sl].astype(BF16)
        o_vsd[0, j] = vst[:, sl].astype(BF16)
        o_vwt[0, j] = vwt[:, sl].astype(BF16)
    for j in range(q.shape[0] // SEL_TILE):
        o_vst[0, j] = vst[:, j * SEL_TILE:(j + 1) * SEL_TILE].astype(BF16)


def _nsa_prep(q, kv, tables, qk_g, B, T, ts=512):
    N = q.shape[0]
    nt = T // ts
    nk = ts // KEY_TILE
    ns = ts // SEL_TILE
    row = lambda b, t: (b * nt + t, 0)
    full = lambda shape: pl.BlockSpec(shape, lambda b, t: (0,) * len(shape))
    tab = pl.BlockSpec((ts, LANES), lambda b, t: (t, 0))
    gq = jnp.tile(qk_g[0], NSA_Q_HEADS).reshape(1, NSA_WIDTH)
    gs = jnp.tile(qk_g[2], NSA_KV_HEADS).reshape(1, LANES)
    gw = jnp.tile(qk_g[3], NSA_KV_HEADS).reshape(1, LANES)
    tiled = lambda rows: pl.BlockSpec((1, nk, rows, KEY_TILE), lambda b, t: (b, t, 0, 0))
    return pl.pallas_call(
        _nsa_prep_kernel,
        grid=(B, nt),
        in_specs=[pl.BlockSpec((ts, NSA_WIDTH), row), pl.BlockSpec((ts, KV_COLS), row), tab, tab, tab,
                  full((1, NSA_WIDTH)), full((1, LANES)), full((1, LANES)),
                  full((HEAD_SUM_WIDTH, HEAD_SUM_WIDTH)), full((LANES, LANES))],
        out_specs=[tiled(NSA_WIDTH), pl.BlockSpec((ts, 2 * LANES), row), pl.BlockSpec((ts, LANES), row),
                   pl.BlockSpec((1, ns, NSA_KV_HEADS * V_ROWS, SEL_TILE), lambda b, t: (b, t, 0, 0)),
                   tiled(NSA_KV_HEADS * V_ROWS), tiled(NSA_KV_HEADS * V_ROWS)],
        out_shape=[jax.ShapeDtypeStruct((B, T // KEY_TILE, NSA_WIDTH, KEY_TILE), BF16),
                   jax.ShapeDtypeStruct((N, 2 * LANES), BF16), jax.ShapeDtypeStruct((N, LANES), BF16),
                   jax.ShapeDtypeStruct((B, T // SEL_TILE, NSA_KV_HEADS * V_ROWS, SEL_TILE), BF16),
                   jax.ShapeDtypeStruct((B, T // KEY_TILE, NSA_KV_HEADS * V_ROWS, KEY_TILE), BF16),
                   jax.ShapeDtypeStruct((B, T // KEY_TILE, NSA_KV_HEADS * V_ROWS, KEY_TILE), BF16)],
        compiler_params=_params("arbitrary", "arbitrary"),
        name="nsa_prep",
    )(q, kv, *tables, gq, gs, gw, _head_block_diag(HEAD_SUM_WIDTH, 1.0 / HEAD_DIM),
      _head_block_diag(LANES, 1.0 / HEAD_DIM))


def _gelu_tanh(x):
    return 0.5 * x * (1.0 + jnp.tanh(0.7978845608028654 * (x + 0.044715 * x * x * x)))


def _nsa_cmp_kernel(x_ref, pos_ref, w1_ref, w2_ref, *rest, is_key):
    if is_key:
        g_ref, c_ref, sd_ref, su_ref, bd_ref, o_ref, xs_ref = rest
    else:
        o_ref, xs_ref = rest
    nch = xs_ref.shape[0]
    S = CMP_STRIDE
    for j in range(S):
        xs_ref[:, j * LANES:(j + 1) * LANES] = x_ref[0, pl.ds(j, nch, stride=S), :]
    xs = xs_ref[...]
    first = _dot((xs + pos_ref[0:1, :]).astype(BF16), w1_ref[0])
    second = _dot((xs + pos_ref[1:2, :]).astype(BF16), w1_ref[1])
    hid = first + pltpu.roll(second, nch - 1, 0)
    out = _dot(_gelu_tanh(hid).astype(BF16), w2_ref[...])
    rows = lax.broadcasted_iota(jnp.int32, out.shape, 0)
    if is_key:
        out = _norm_rope(out, bd_ref[...], g_ref[...], c_ref[...], sd_ref[...], su_ref[...])
        o_ref[0] = jnp.where(rows < nch - 1, out, 0.0).astype(BF16)
    else:
        o_ref[0] = jnp.where(rows < nch - 1, out, 0.0).T.astype(BF16)


def _nsa_cmp(kv3, which, cmp_pos, cmp_w1, cmp_w2, g_k, tables_cmp):
    B, T, _ = kv3.shape
    S = CMP_STRIDE
    nch = T // S
    is_key = which == 0
    eye2 = jnp.eye(NSA_KV_HEADS, dtype=F32)
    w1 = cmp_w1[which].reshape(CMP_BLOCK, HEAD_DIM, CMP_HIDDEN)
    w1 = jnp.einsum('jdh,ge->jgdeh', w1, eye2).reshape(2, S * LANES, NSA_KV_HEADS * CMP_HIDDEN)
    w2 = jnp.einsum('hd,ge->ghed', cmp_w2[which], eye2).reshape(NSA_KV_HEADS * CMP_HIDDEN, LANES)
    pos = jnp.tile(cmp_pos[which].reshape(2, S, 1, HEAD_DIM), (1, 1, NSA_KV_HEADS, 1)).reshape(2, S * LANES)
    full = lambda shape: pl.BlockSpec(shape, lambda b: (0,) * len(shape))
    in_specs = [pl.BlockSpec((1, T, LANES), lambda b: (b, 0, which)), full(pos.shape), full(w1.shape),
                full(w2.shape)]
    args = [kv3, pos, w1.astype(BF16), w2.astype(BF16)]
    if is_key:
        in_specs += [full((1, LANES)), full((nch, LANES)), full((nch, LANES)), full((nch, LANES)),
                     full((LANES, LANES))]
        args += [jnp.tile(g_k, NSA_KV_HEADS).reshape(1, LANES), *tables_cmp,
                 _head_block_diag(LANES, 1.0 / HEAD_DIM)]
        out_spec = pl.BlockSpec((1, nch, LANES), lambda b: (b, 0, 0))
        out_shape = jax.ShapeDtypeStruct((B, nch, LANES), BF16)
    else:
        out_spec = pl.BlockSpec((1, LANES, nch), lambda b: (b, 0, 0))
        out_shape = jax.ShapeDtypeStruct((B, LANES, nch), BF16)
    return pl.pallas_call(
        functools.partial(_nsa_cmp_kernel, is_key=is_key),
        grid=(B,),
        in_specs=in_specs,
        out_specs=out_spec,
        out_shape=out_shape,
        scratch_shapes=[pltpu.VMEM((nch, S * LANES), F32)],
        compiler_params=_params("arbitrary"),
        name="nsa_cmp_k" if is_key else "nsa_cmp_v",
    )(*args)


def _nsa_attn_kernel(qt_ref, kc_ref, vct_ref, ks_ref, vst_ref, vsd_ref, kw_ref, vwt_ref, gt_ref, ov_ref, o_ref,
                     rhs_ref, oc_ref, keep_ref, s0_ref, s1_ref, s2_ref, s3_ref, p0_ref, p1_ref):
    qb = pl.program_id(1)
    G = NSA_KV_HEADS
    R = NSA_GROUP
    QT = Q_TILE
    KT = KEY_TILE
    CG = R * QT
    NQ = G * CG
    D = HEAD_DIM
    t0 = qb * QT
    n_cmp_pad = kc_ref.shape[1]
    n_sel = ov_ref.shape[0]
    cols = lambda g: slice(g * CG, (g + 1) * CG)

    q_cols = []
    for g in range(G):
        q_g = jnp.concatenate([qt_ref[0, 0, (g * R + r) * D:(g * R + r + 1) * D, :] for r in range(R)], axis=1)
        q_cols.append(jnp.concatenate([q_g if gg == g else jnp.zeros_like(q_g) for gg in range(G)], axis=0))
    qpad = jnp.concatenate(q_cols, axis=1)

    tq_row = t0 + (lax.broadcasted_iota(jnp.int32, (1, NQ), 1) & (QT - 1))
    spread = lambda z: jnp.concatenate([z[:, g * QT:(g + 1) * QT] for g in range(G) for _ in range(R)], axis=1)
    tile_all = lambda z: jnp.concatenate([z] * (G * R), axis=1)

    def values_dot(v_of_group, p):
        return jnp.concatenate([_dot(v_of_group(g), p[:, cols(g)]) for g in range(G)], axis=1)

    NV = CMP_VARIANTS
    nq = ks_ref.shape[1] // QT

    def compressed_and_select(n_c, n_b):
        sc = _dot(kc_ref[0, 0:n_c, :], qpad)
        n_i = lax.broadcasted_iota(jnp.int32, (n_c, 1), 0)
        cend = jnp.where(n_i < n_cmp_pad - 1, n_i * CMP_STRIDE + (CMP_BLOCK - 1), jnp.int32(2 ** 30))
        cvalid = cend <= tq_row
        sc = jnp.where(cvalid, sc, NEG_INF)
        mc = jnp.max(sc, axis=0, keepdims=True)
        ec = jnp.where(cvalid, jnp.exp2(sc - mc), 0.0)
        pc = ec / jnp.maximum(jnp.sum(ec, axis=0, keepdims=True), F32_TINY)
        pc_b = pc.astype(BF16)
        oc_ref[...] = values_dot(lambda g: vct_ref[0, g * D:(g + 1) * D, 0:n_c], pc_b)
        sums = []
        for g in range(G):
            acc = pc[:, g * CG:g * CG + QT]
            for r in range(1, R):
                acc = acc + pc[:, g * CG + r * QT:g * CG + (r + 1) * QT]
            sums.append(acc)
        imp = _dot_split_rhs(ov_ref[0:n_b, 0:n_c], jnp.concatenate(sums, axis=1))
        jb = lax.broadcasted_iota(jnp.int32, (n_b, G * QT), 0)
        jf = jb.astype(F32)
        tq_b = t0 + (lax.broadcasted_iota(jnp.int32, (n_b, G * QT), 1) & (QT - 1))
        cur = tq_b >> SEL_SHIFT
        forced = (jb == 0) | (jb == cur) | (jb == cur - 1)
        visible = jb * SEL_BLOCK <= tq_b
        score = jnp.where(visible, jnp.where(forced, FORCE_SCORE, imp), -1.0)
        sel = jnp.zeros((n_b, G * QT), F32)
        for _ in range(min(SEL_TOPK, n_b)):
            mx = jnp.max(score, axis=0, keepdims=True)
            jmin = jnp.min(jnp.where(score == mx, jf, 1e9), axis=0, keepdims=True)
            hit = jf == jmin
            sel = jnp.where(hit, 1.0, sel)
            score = jnp.where(hit, -3e38, score)
        keep_ref[0:n_b, :] = jnp.where(visible, sel, 0.0)
        if n_b < n_sel:
            keep_ref[n_b:n_sel, :] = jnp.zeros((n_sel - n_b, G * QT), F32)

    for v in range(NV):
        @pl.when((qb * NV) // nq == v)
        def _():
            compressed_and_select((v + 1) * n_cmp_pad // NV, (v + 1) * n_sel // NV)

    o_c = oc_ref[...]
    ji = lax.broadcasted_iota(jnp.int32, (n_sel, G * QT), 0)

    ST = SEL_TILE
    bias_all = (keep_ref[...] - 1.0) * (-NEG_INF)
    first_own = t0 // SEL_BLOCK
    vrows = lambda g: slice(g * V_ROWS, (g + 1) * V_ROWS)

    def with_bias_rows(bias):
        rows = spread(bias).astype(BF16)
        if n_sel < LANES:
            rows = jnp.concatenate([rows, jnp.zeros((LANES - n_sel, NQ), BF16)], axis=0)
        return jnp.concatenate([qpad, rows], axis=0)

    rhs_ref[...] = with_bias_rows(jnp.where(ji < first_own, bias_all, NEG_INF))
    n_tiles = (t0 + ST - 1) // ST
    last_tile = ks_ref.shape[1] // ST - 1
    p_bufs = (p0_ref, p1_ref)

    def sel_scores(kt, s_ref):
        k0 = pl.multiple_of(jnp.minimum(kt, last_tile) * ST, ST)
        s_ref[...] = _dot(ks_ref[0, pl.ds(k0, ST), :], rhs_ref[...])

    def sel_values(kt, slot, acc, alpha):
        kt = jnp.clip(kt, 0, last_tile)
        return acc * alpha + values_dot(lambda g: vst_ref[0, kt, vrows(g), :], p_bufs[slot][...])

    def sel_softmax(s_ref, slot, m):
        s = s_ref[...]
        m_new = jnp.maximum(m, jnp.max(s, axis=0, keepdims=True))
        p_bufs[slot][...] = jnp.exp2(s - m_new).astype(BF16)
        return m_new, jnp.exp2(m - m_new)

    def sel_pair(a, carry, s_now, s_next):
        m, acc, alpha0, alpha1 = carry
        acc = sel_values(a - 2, 0, acc, alpha0)
        acc = sel_values(a - 1, 1, acc, alpha1)
        sel_scores(a + 2, s_next[0])
        sel_scores(a + 3, s_next[1])
        m, alpha0 = sel_softmax(s_now[0], 0, m)
        m, alpha1 = sel_softmax(s_now[1], 1, m)
        return m, acc, alpha0, alpha1

    bufs_a, bufs_b = (s0_ref, s1_ref), (s2_ref, s3_ref)
    sel_scores(0, s0_ref)
    sel_scores(1, s1_ref)
    p0_ref[...] = jnp.zeros_like(p0_ref)
    p1_ref[...] = jnp.zeros_like(p1_ref)
    own = _dot(ks_ref[0, pl.ds(pl.multiple_of(t0, QT), QT), :], with_bias_rows(bias_all))

    n_wt = (WINDOW + QT) // KT
    k0w = pl.multiple_of(jnp.maximum(t0 - WINDOW, 0), KT)
    kt_w = k0w // KT
    keys_w = kw_ref[0, pl.ds(k0w, WINDOW + QT), :]
    dw = (t0 + lax.broadcasted_iota(jnp.int32, (WINDOW + QT, QT), 1)
          - (k0w + lax.broadcasted_iota(jnp.int32, (WINDOW + QT, QT), 0)))
    sw = _dot(keys_w, qpad) + tile_all(jnp.where(dw >= 0, jnp.where(dw < WINDOW, 0.0, NEG_INF), NEG_INF))
    pw = jnp.exp2(sw - jnp.max(sw, axis=0, keepdims=True)).astype(BF16)
    acc_w = values_dot(lambda g: vwt_ref[0, kt_w, vrows(g), :], pw[0:KT])
    for j in range(1, n_wt):
        acc_w = acc_w + values_dot(lambda g: vwt_ref[0, kt_w + j, vrows(g), :], pw[j * KT:(j + 1) * KT])

    n_pairs = (n_tiles + 1) // 2
    one = jnp.ones((1, NQ), F32)
    m_s, acc_s, alpha0, alpha1 = lax.fori_loop(
        0, n_pairs,
        lambda j, carry: lax.cond(j % 2 == 0,
                                  lambda c: sel_pair(2 * j, c, bufs_a, bufs_b),
                                  lambda c: sel_pair(2 * j, c, bufs_b, bufs_a), carry),
        (jnp.full((1, NQ), NEG_INF, F32), jnp.zeros((V_ROWS, NQ), F32), one, one))
    acc_s = sel_values(2 * n_pairs - 2, 0, acc_s, alpha0)
    acc_s = sel_values(2 * n_pairs - 1, 1, acc_s, alpha1)
    seen = lax.broadcasted_iota(jnp.int32, (QT, QT), 0) <= lax.broadcasted_iota(jnp.int32, (QT, QT), 1)
    own = jnp.where(tile_all(seen), own, NEG_INF)
    m_new = jnp.maximum(m_s, jnp.max(own, axis=0, keepdims=True))
    acc_s = acc_s * jnp.exp2(m_s - m_new) + values_dot(lambda g: vsd_ref[0, qb, vrows(g), :],
                                                       jnp.exp2(own - m_new).astype(BF16))

    gates = _sigmoid(gt_ref[0])
    grow = lambda j: jnp.concatenate([gates[g, j, r:r + 1, :] for g in range(G) for r in range(R)], axis=1)
    o = (grow(0) * o_c + grow(1) * (acc_s[0:D] / acc_s[D:D + 1])
         + grow(2) * (acc_w[0:D] / acc_w[D:D + 1]))
    halves = []
    for h in range(G * R // 2):
        pair = jnp.concatenate([o[:, (2 * h) * QT:(2 * h + 1) * QT],
                                o[:, (2 * h + 1) * QT:(2 * h + 2) * QT]], axis=0)
        halves.append(pair.T)
    o_ref[...] = jnp.concatenate(halves, axis=1)


def _nsa_attn(qt, kcmp, vct, ks3, vst, vsd, kw3, vwt, gt, ov_t, B, T):
    G, R = NSA_KV_HEADS, NSA_GROUP
    nq = T // Q_TILE
    nk = T // KEY_TILE
    nch = kcmp.shape[1]
    n_sel = ov_t.shape[0]
    NQ = G * R * Q_TILE
    assert (T // SEL_TILE) % 2 == 0 and n_sel <= LANES and Q_TILE == KEY_TILE
    assert nq % CMP_VARIANTS == 0 and n_sel % (8 * CMP_VARIANTS) == 0 and nch % (8 * CMP_VARIANTS) == 0
    return pl.pallas_call(
        _nsa_attn_kernel,
        grid=(B, nq),
        in_specs=[pl.BlockSpec((1, 1, NSA_WIDTH, Q_TILE), lambda b, q: (b, q, 0, 0)),
                  pl.BlockSpec((1, nch, LANES), lambda b, q: (b, 0, 0)),
                  pl.BlockSpec((1, G * HEAD_DIM, nch), lambda b, q: (b, 0, 0)),
                  pl.BlockSpec((1, T, 2 * LANES), lambda b, q: (b, 0, 0)),
                  pl.BlockSpec((1, T // SEL_TILE, G * V_ROWS, SEL_TILE), lambda b, q: (b, 0, 0, 0)),
                  pl.BlockSpec((1, nk, G * V_ROWS, KEY_TILE), lambda b, q: (b, 0, 0, 0)),
                  pl.BlockSpec((1, T, LANES), lambda b, q: (b, 0, 0)),
                  pl.BlockSpec((1, nk, G * V_ROWS, KEY_TILE), lambda b, q: (b, 0, 0, 0)),
                  pl.BlockSpec((1, G, 3, R, Q_TILE), lambda b, q: (b, 0, 0, 0, q)),
                  pl.BlockSpec((n_sel, nch), lambda b, q: (0, 0))],
        out_specs=pl.BlockSpec((Q_TILE, NSA_WIDTH), lambda b, q: (b * nq + q, 0)),
        out_shape=jax.ShapeDtypeStruct((B * T, NSA_WIDTH), F32),
        scratch_shapes=[pltpu.VMEM((2 * LANES, NQ), BF16), pltpu.VMEM((HEAD_DIM, NQ), F32),
                        pltpu.VMEM((n_sel, G * Q_TILE), F32),
                        *[pltpu.VMEM((SEL_TILE, NQ), F32)] * 4,
                        *[pltpu.VMEM((SEL_TILE, NQ), BF16)] * 2],
        compiler_params=_params("arbitrary", "arbitrary"),
        name="nsa_attn",
    )(qt, kcmp, vct, ks3, vst, vsd, kw3, vwt, gt, ov_t)


def _first_index_of(vals, target):
    idx = jnp.full_like(target, float(len(vals) - 1))
    for i in range(len(vals) - 2, -1, -1):
        idx = jnp.where(vals[i] == target, float(i), idx)
    return idx


def _pick(vals, idx):
    out = vals[-1]
    for i in range(len(vals) - 2, -1, -1):
        out = jnp.where(idx == float(i), vals[i], out)
    return out


def _route_rows(score, bias):
    E, G, P = N_EXPERTS, N_GROUPS, EXPERTS_PER_GROUP
    sel = score + bias
    s = [sel[e:e + 1, :] for e in range(E)]
    raw = [score[e:e + 1, :] for e in range(E)]
    grp = []
    for gi in range(G):
        a = s[gi * P:(gi + 1) * P]
        best = None
        for i in range(P):
            for j in range(i + 1, P):
                pair = a[i] + a[j]
                best = pair if best is None else jnp.maximum(best, pair)
        grp.append(best)
    gmax = functools.reduce(jnp.maximum, grp)
    g_star = _first_index_of(grp, gmax)
    v = [_pick([s[gi * P + i] for gi in range(G)], g_star) for i in range(P)]
    w = [_pick([raw[gi * P + i] for gi in range(G)], g_star) for i in range(P)]
    i1 = _first_index_of(v, functools.reduce(jnp.maximum, v))
    v2 = [jnp.where(i1 == float(i), -jnp.inf, v[i]) for i in range(P)]
    i2 = _first_index_of(v2, functools.reduce(jnp.maximum, v2))
    w1, w2 = _pick(w, i1), _pick(w, i2)
    tot = w1 + w2
    zero = jnp.zeros_like(tot)
    e1, e2 = g_star * P + i1, g_star * P + i2
    n = score.shape[1]
    eidx = lax.broadcasted_iota(jnp.int32, (E, n), 0).astype(F32)
    oh1, oh2 = jnp.where(eidx == e1, 1.0, 0.0), jnp.where(eidx == e2, 1.0, 0.0)
    earlier = jnp.where(lax.broadcasted_iota(jnp.int32, (n, n), 0) < lax.broadcasted_iota(jnp.int32, (n, n), 1),
                        1.0, 0.0).astype(BF16)
    cnt = _dot(jnp.concatenate([oh1, oh2], axis=0).astype(BF16), earlier)
    rank1 = jnp.sum(oh1 * cnt[0:E], axis=0, keepdims=True)
    rank2 = jnp.sum(oh2 * cnt[E:2 * E], axis=0, keepdims=True)
    lane = lax.broadcasted_iota(jnp.int32, (E, LANES), 1)
    totals = jnp.where(lane == 0, jnp.sum(oh1, axis=1, keepdims=True),
                       jnp.where(lane == 1, jnp.sum(oh2, axis=1, keepdims=True), 0.0))
    return jnp.concatenate([e1, e2, w1 / tot, w2 / tot, rank1, rank2, zero, zero], axis=0), totals


def _merge_kernel(ys_ref, g_ref, bonus_ref, gng_ref, gnb_ref, bd_ref, yb_ref, pm_ref, x_ref, mod_ref,
                  ng_ref, wa_ref, wb_ref, wo_ref, rw_ref, rb_ref, o_x, o_h, o_route, o_tot):
    m = mod_ref[pl.program_id(0)]
    bd = bd_ref[...]
    y = ys_ref[...]
    mean = _head_sums(y, bd)
    yc = y - mean
    var = _head_sums(yc * yc, bd)
    ya = (yc * lax.rsqrt(var + RWKV_GN_EPS) * gng_ref[...] + gnb_ref[...] + bonus_ref[...]) * g_ref[...]
    pm = pm_ref[...].astype(F32)
    D = x_ref.shape[1]
    mix = (_sigmoid(pm[:, 0:D]) * _dot(ya.astype(BF16), wa_ref[...])
           + _sigmoid(pm[:, D:2 * D]) * _dot(yb_ref[...].astype(BF16), wb_ref[...]))
    x = x_ref[...] + m[2:3] * _dot(mix.astype(BF16), wo_ref[...])
    o_x[...] = x
    ms = jnp.mean(x * x, axis=-1, keepdims=True)
    h = x * lax.rsqrt(ms + NORM_EPS) * ng_ref[...]
    h = h * (1.0 + m[4:5]) + m[3:4]
    o_h[...] = _pack_bf16_pairs(h)
    score = _sigmoid(_dot_3pass(h, rw_ref).T[0:N_EXPERTS, :])
    o_route[...], o_tot[...] = _route_rows(score, rb_ref[...])


def _merge(ys, g, bonus, gn_g, gn_b, yb, pm, x2, mod, ng, wa, wb, wo, router_w, router_b, B, T, tm=256):
    N, D = x2.shape
    W = RWKV_WIDTH
    nt = T // tm
    row = lambda b, t: (b * nt + t, 0)
    full = lambda shape: pl.BlockSpec(shape, lambda b, t: (0,) * len(shape))
    return pl.pallas_call(
        _merge_kernel,
        grid=(B, nt),
        in_specs=[pl.BlockSpec((tm, W), row), pl.BlockSpec((tm, W), row), pl.BlockSpec((tm, W), row),
                  full((1, W)), full((1, W)), full((HEAD_SUM_WIDTH, HEAD_SUM_WIDTH)),
                  pl.BlockSpec((tm, NSA_WIDTH), row), pl.BlockSpec((tm, 2 * D), row),
                  pl.BlockSpec((tm, D), row), full((B, 6, D)), full((1, D)),
                  full((W, D)), full((NSA_WIDTH, D)), full((D, D)), full((2, D, LANES)),
                  full((N_EXPERTS, 1))],
        out_specs=[pl.BlockSpec((tm, D), row), pl.BlockSpec((tm, D // 2), row),
                   pl.BlockSpec((8, tm), lambda b, t: (0, b * nt + t)),
                   pl.BlockSpec((N_EXPERTS, LANES), lambda b, t: (b * nt + t, 0))],
        out_shape=[jax.ShapeDtypeStruct((N, D), F32), jax.ShapeDtypeStruct((N, D // 2), F32),
                   jax.ShapeDtypeStruct((8, N), F32), jax.ShapeDtypeStruct((N // tm * N_EXPERTS, LANES), F32)],
        compiler_params=_params("arbitrary", "arbitrary"),
        name="merge_out",
    )(ys, g, bonus, gn_g.reshape(1, W), gn_b.reshape(1, W), _head_block_diag(HEAD_SUM_WIDTH, 1.0 / HEAD_DIM),
      yb, pm, x2, mod, ng, wa, wb, wo,
      _hi_lo(jnp.zeros((D, LANES), F32).at[:, :N_EXPERTS].set(router_w)), router_b.reshape(N_EXPERTS, 1))


def _route(route, totals, N):
    wts = route[TOP_K:2 * TOP_K].T
    NK = N * TOP_K
    E = N_EXPERTS
    n_tiles = totals.shape[0] // E
    expert = route[0:TOP_K].astype(jnp.int32)
    rank = route[2 * TOP_K:3 * TOP_K].astype(jnp.int32)
    per = totals.reshape(n_tiles, E, LANES)[:, :, 0:TOP_K].astype(jnp.int32).transpose(0, 2, 1)
    per = per.reshape(n_tiles * TOP_K, E)
    csum = jnp.cumsum(per, axis=0)
    counts = csum[-1]
    padded = (counts + MOE_BLOCK - 1) // MOE_BLOCK * MOE_BLOCK
    pad_end = jnp.cumsum(padded)
    pad_start = pad_end - padded
    first = (pad_start[None, :] + csum - per).reshape(n_tiles, TOP_K, E).transpose(1, 0, 2)
    first = jnp.repeat(first, N // n_tiles, axis=1)
    mine = expert[:, :, None] == jnp.arange(E, dtype=jnp.int32)[None, None, :]
    dest = (jnp.sum(jnp.where(mine, first, 0), axis=-1) + rank).reshape(-1)
    n_blk = -(-NK // MOE_BLOCK) + N_EXPERTS
    blk_start = jnp.arange(n_blk, dtype=jnp.int32) * MOE_BLOCK
    blk_expert = jnp.sum((pad_end[None, :] <= blk_start[:, None]).astype(jnp.int32), axis=1)
    blk_expert = jnp.clip(blk_expert, 0, N_EXPERTS - 1)
    blk_valid = jnp.clip((pad_start + counts)[blk_expert] - blk_start, 0, MOE_BLOCK).astype(jnp.int32)
    dest = jnp.pad(dest.astype(jnp.int32).reshape(NK // SC_WINDOW, SC_WINDOW), ((0, 0), (0, LANES - SC_WINDOW)))
    return wts, dest, blk_expert, blk_valid, n_blk


SC_WINDOW = 64


def _sc_mesh():
    return plsc.VectorSubcoreMesh(core_axis_name="c", subcore_axis_name="s")


def _sc_dispatch(h, dest, n_slots):
    N, D = h.shape
    W = SC_WINDOW
    nw = N // W

    @pl.kernel(out_type=jax.ShapeDtypeStruct((n_slots, D), h.dtype), mesh=_sc_mesh(), scratch_types=[])
    def dispatch(h_hbm, i_hbm, o_hbm):
        def body(x_vmem, i_vmem):
            pltpu.sync_copy(x_vmem, o_hbm.at[i_vmem.at[0, pl.ds(0, W)]])

        pltpu.emit_pipeline(
            body, grid=(TOP_K, nw),
            in_specs=[pl.BlockSpec((W, D), lambda k, i: (i, 0)),
                      pl.BlockSpec((1, LANES), lambda k, i: (k * nw + i, 0))],
            out_specs=[], core_axis_name=("c", "s"),
            dimension_semantics=(pltpu.PARALLEL, pltpu.PARALLEL))(h_hbm, i_hbm)

    return dispatch(h, dest)


def _sc_collect(ys, dest):
    W = SC_WINDOW
    NK = dest.shape[0] * W
    D = ys.shape[1]
    half = NK // TOP_K // W

    @pl.kernel(out_type=jax.ShapeDtypeStruct((NK, D), ys.dtype), mesh=_sc_mesh(), scratch_types=[])
    def collect(y_hbm, i_hbm, o_hbm):
        def body(i_vmem, o_vmem):
            pltpu.sync_copy(y_hbm.at[i_vmem.at[0, pl.ds(0, W)]], o_vmem)

        pltpu.emit_pipeline(
            body, grid=(TOP_K, half),
            in_specs=[pl.BlockSpec((1, LANES), lambda k, i: (k * half + i, 0))],
            out_specs=[pl.BlockSpec((W, D), lambda k, i: (k * half + i, 0))],
            core_axis_name=("c", "s"),
            dimension_semantics=(pltpu.PARALLEL, pltpu.PARALLEL))(i_hbm, o_hbm)

    return collect(ys, dest)


def _moe_dense_kernel(be_ref, nv_ref, x_ref, wg_ref, wu_ref, wd_ref, o_ref, wg_b, wu_b, wd_b):
    i = pl.program_id(0)
    nv = nv_ref[i]

    @pl.when((i == 0) | (be_ref[i] != be_ref[jnp.maximum(i - 1, 0)]))
    def _():
        wg_b[...] = wg_ref[0, 0].astype(BF16)
        wu_b[...] = wu_ref[0, 0].astype(BF16)
        wd_b[...] = wd_ref[0, 0].astype(BF16)

    @pl.when(nv > 0)
    def _():
        x_lo, x_hi = _unpack_bf16_pairs(x_ref[...])
        half = x_lo.shape[1]
        gate = _dot(x_lo, wg_b[0:half, :]) + _dot(x_hi, wg_b[half:, :])
        up = _dot(x_lo, wu_b[0:half, :]) + _dot(x_hi, wu_b[half:, :])
        o_ref[...] = _pack_bf16_pairs(_dot((gate * _sigmoid(gate) * up).astype(BF16), wd_b[...]))

    @pl.when(nv == 0)
    def _():
        o_ref[...] = jnp.zeros_like(o_ref)


def _moe_dense(xs, blk_expert, blk_valid, n_blk, layer, wg, wu, wd):
    P = xs.shape[0]
    D, DE = wg.shape[2:]
    wmap = lambda i, be, nv: (layer, be[i], 0, 0)
    grid_spec = pltpu.PrefetchScalarGridSpec(
        num_scalar_prefetch=2,
        grid=(n_blk,),
        in_specs=[pl.BlockSpec((MOE_BLOCK, D // 2), lambda i, be, nv: (i, 0)), pl.BlockSpec((1, 1, D, DE), wmap),
                  pl.BlockSpec((1, 1, D, DE), wmap), pl.BlockSpec((1, 1, DE, D), wmap)],
        out_specs=pl.BlockSpec((MOE_BLOCK, D // 2), lambda i, be, nv: (i, 0)),
        scratch_shapes=[pltpu.VMEM((D, DE), BF16), pltpu.VMEM((D, DE), BF16), pltpu.VMEM((DE, D), BF16)],
    )
    return pl.pallas_call(
        _moe_dense_kernel,
        grid_spec=grid_spec,
        out_shape=jax.ShapeDtypeStruct((P, D // 2), F32),
        compiler_params=_params("arbitrary"),
        name="moe_experts",
    )(blk_expert, blk_valid, xs, wg, wu, wd)


def _final_kernel(x_ref, y0_ref, y1_ref, w_ref, mod_ref, o_ref):
    o_ref[...] = _moe_residual(x_ref, y0_ref, y1_ref, w_ref, mod_ref[pl.program_id(0)])


def _final(x2, ybuf, wts, mod, B, T, tm=512):
    N, D = x2.shape
    nt = T // tm
    row = lambda b, t: (b * nt + t, 0)
    return pl.pallas_call(
        _final_kernel,
        grid=(B, nt),
        in_specs=[pl.BlockSpec((tm, D), row), pl.BlockSpec((tm, D // 2), row),
                  pl.BlockSpec((tm, D // 2), lambda b, t: (N // tm + b * nt + t, 0)),
                  pl.BlockSpec((tm, TOP_K), row), pl.BlockSpec((B, 6, D), lambda b, t: (0, 0, 0))],
        out_specs=pl.BlockSpec((tm, D), row),
        out_shape=jax.ShapeDtypeStruct((N, D), F32),
        compiler_params=_params("arbitrary", "arbitrary"),
        name="moe_combine",
    )(x2, ybuf, ybuf, wts, mod)


def _overlap_t(n_sel, n_cmp_pad):
    ci = jnp.arange(n_cmp_pad)[None, :] * CMP_STRIDE
    sj = jnp.arange(n_sel)[:, None] * SEL_BLOCK
    ov = (ci <= sj + SEL_BLOCK - 1) & (ci + CMP_BLOCK - 1 >= sj) & (jnp.arange(n_cmp_pad)[None, :] < n_cmp_pad - 1)
    return ov.astype(BF16)


def kernel(x, c, w_ada, b_ada, norm_g, w_in, b_in, rwkv_mu, rwkv_w0, rwkv_w2, rwkv_a0, rwkv_a2, rwkv_g2,
           rwkv_k_k, rwkv_k_a, rwkv_r_k, rwkv_gn_g, rwkv_gn_b, qk_norm_g, cmp_pos, cmp_w1, cmp_w2,
           w_up_rwkv, w_up_nsa, w_out, router_w, router_b, exp_w_gate, exp_w_up, exp_w_down):
    B, T, D = x.shape
    L = w_ada.shape[0]
    N = B * T
    mods = _ada(c, w_ada, b_ada)
    tables = _rope_tables(jnp.arange(T, dtype=jnp.int32))
    nch = T // CMP_STRIDE
    tables_cmp = _rope_tables(jnp.arange(nch, dtype=jnp.int32) * CMP_STRIDE + CMP_BLOCK - 1)
    ov_t = _overlap_t(T // SEL_BLOCK, nch)
    n_gate = NSA_GATE_COLS
    x2 = x.reshape(N, D)
    pending_moe = None
    for l in range(L):
        g0 = _SEG_KV[1] + n_gate
        w_b = w_in[l].astype(BF16)
        w_pad = jnp.concatenate([w_b[:, :g0], jnp.zeros((D, GATE_PAD - n_gate), BF16), w_b[:, g0:]], axis=1)
        b_pad = jnp.concatenate([b_in[l][:g0], jnp.zeros((GATE_PAD - n_gate,), F32), b_in[l][g0:]]).reshape(1, -1)
        outs = _inproj(x2, pending_moe, mods[l], norm_g[l, 0].reshape(1, D), w_pad, b_pad, B, T)
        if pending_moe is not None:
            x2, outs = outs[0], outs[1:]
        p_rw, p_q, p_kv, p_gate, p_merge = outs
        r, k, v, al, bb, ld, g, bonus = _rwkv_pre(p_rw, rwkv_mu[l], rwkv_w0[l], rwkv_w2[l], rwkv_a0[l],
                                                  rwkv_a2[l], rwkv_g2[l], rwkv_k_k[l], rwkv_k_a[l],
                                                  rwkv_r_k[l], B, T)
        ys = _rwkv_scan(r, k, v, al, bb, ld, B, T)
        qt, ks, kw, vst, vsd, vwt = _nsa_prep(p_q, p_kv, tables, qk_norm_g[l], B, T)
        kv3 = p_kv.reshape(B, T, KV_COLS)
        kcmp = _nsa_cmp(kv3, 0, cmp_pos[l], cmp_w1[l], cmp_w2[l], qk_norm_g[l, 1], tables_cmp)
        vct = _nsa_cmp(kv3, 1, cmp_pos[l], cmp_w1[l], cmp_w2[l], None, None)
        gt = p_gate[:, :n_gate].reshape(B, T, NSA_KV_HEADS, NSA_GROUP, 3).transpose(0, 2, 4, 3, 1)
        yb = _nsa_attn(qt, kcmp, vct, ks.reshape(B, T, 2 * LANES), vst, vsd, kw.reshape(B, T, LANES), vwt, gt, ov_t,
                       B, T)
        x2, h2, route, totals = _merge(ys, g, bonus, rwkv_gn_g[l], rwkv_gn_b[l], yb, p_merge, x2, mods[l],
                               norm_g[l, 1].reshape(1, D), w_up_rwkv[l].astype(BF16),
                               w_up_nsa[l].astype(BF16), w_out[l].astype(BF16), router_w, router_b, B, T)
        wts, dest, blk_expert, blk_valid, n_blk = _route(route, totals, N)
        xs = _sc_dispatch(h2, dest, n_blk * MOE_BLOCK)
        ys = _moe_dense(xs, blk_expert, blk_valid, n_blk, l, exp_w_gate, exp_w_up, exp_w_down)
        ybuf = _sc_collect(ys, dest)
        pending_moe = (ybuf, wts, mods[l])
    return _final(x2, *pending_moe, B, T).reshape(B, T, D)
```
